```python
import jax, jax.numpy as jnp
from jax import lax
import numpy as np

D_MODEL = 2048
BATCH = 8
SEQ = 2048
DEPTH = 2

N_A_LAYERS = DEPTH // 2
N_B_LAYERS = DEPTH - N_A_LAYERS
PLE_DIM = 256
D_FF = 4 * D_MODEL
HGRN_HEAD_DIM = 128
HGRN_HEADS = D_MODEL // HGRN_HEAD_DIM
HGRN_CHUNK = 16
FOX_HEAD_DIM = 128
FOX_HEADS = D_MODEL // FOX_HEAD_DIM
Q_BLOCK = 128
EPS = 1e-6

kernel_name = "yoco_hgrn2_fox_hybrid"


def rms_norm(x, gain):
    xf = x.astype(jnp.float32)
    y = xf * lax.rsqrt(jnp.mean(xf * xf, axis=-1, keepdims=True) + EPS)
    return (y * gain.astype(jnp.float32)).astype(x.dtype)


def hgrn2_mixer(u, w_in, lb, head_gain, w_out):
    bsz, seq, _ = u.shape
    nc = seq // HGRN_CHUNK
    q, f, i, g = jnp.split(u @ w_in, 4, axis=-1)

    def to_chunks(t):
        t = t.reshape(bsz, nc, HGRN_CHUNK, HGRN_HEADS, HGRN_HEAD_DIM)
        return t.transpose(0, 3, 1, 2, 4).astype(jnp.float32)

    lb_h = lb.astype(jnp.float32).reshape(HGRN_HEADS, 1, 1, HGRN_HEAD_DIM)
    fg = lb_h + (1.0 - lb_h) * jax.nn.sigmoid(to_chunks(f))
    k = 1.0 - fg
    b = jnp.cumsum(jnp.log(fg), axis=3)
    b_last = b[:, :, :, -1:, :]
    qc = jax.nn.silu(to_chunks(q)) * (HGRN_HEAD_DIM ** -0.5)
    v = to_chunks(i)

    q_in = qc * jnp.exp(b)
    k_in = k * jnp.exp(-b)
    k_end = k * jnp.exp(b_last - b)
    causal = jnp.tril(jnp.ones((HGRN_CHUNK, HGRN_CHUNK), dtype=bool))
    att = jnp.where(causal, jnp.einsum('bhncd,bhnsd->bhncs', q_in, k_in), 0.0)
    o_intra = jnp.einsum('bhncs,bhnse->bhnce', att, v)

    def step(state, inp):
        q_n, k_n, v_n, dec_n = inp
        o_n = jnp.einsum('bhcd,bhde->bhce', q_n, state)
        state = dec_n[..., None] * state + jnp.einsum('bhcd,bhce->bhde', k_n, v_n)
        return state, o_n

    xs = (jnp.moveaxis(q_in, 2, 0), jnp.moveaxis(k_end, 2, 0), jnp.moveaxis(v, 2, 0),
          jnp.moveaxis(jnp.exp(b_last[:, :, :, 0, :]), 2, 0))
    init = jnp.zeros((bsz, HGRN_HEADS, HGRN_HEAD_DIM, HGRN_HEAD_DIM), jnp.float32)
    _, o_inter = lax.scan(step, init, xs)
    o = o_intra + jnp.moveaxis(o_inter, 0, 2)

    o = o.transpose(0, 2, 3, 1, 4).reshape(bsz, seq, HGRN_HEADS, HGRN_HEAD_DIM)
    o = o * lax.rsqrt(jnp.mean(o * o, axis=-1, keepdims=True) + EPS) * head_gain.astype(jnp.float32)
    o = o.reshape(bsz, seq, D_MODEL) * jax.nn.silu(g.astype(jnp.float32))
    return o.astype(u.dtype) @ w_out


def shared_kv(stream, kv_norm, w_kvf, b_f):
    bsz, seq, _ = stream.shape
    hk = rms_norm(stream, kv_norm) @ w_kvf
    k = hk[..., :D_MODEL].reshape(bsz, seq, FOX_HEADS, FOX_HEAD_DIM)
    v = hk[..., D_MODEL:2 * D_MODEL].reshape(bsz, seq, FOX_HEADS, FOX_HEAD_DIM)
    f_logit = hk[..., 2 * D_MODEL:].astype(jnp.float32) + b_f.astype(jnp.float32)
    dcum = jnp.cumsum(jax.nn.log_sigmoid(f_logit), axis=1).transpose(0, 2, 1)
    return k, v, dcum


def fox_mixer(u, k, v, dcum, w_q, w_out):
    bsz, seq, _ = u.shape
    q = (u @ w_q).reshape(bsz, seq, FOX_HEADS, FOX_HEAD_DIM) * (FOX_HEAD_DIM ** -0.5)
    outs = []
    for blk in range(seq // Q_BLOCK):
        start, end = blk * Q_BLOCK, (blk + 1) * Q_BLOCK
        logits = jnp.einsum('bqhd,bkhd->bhqk', q[:, start:end], k[:, :end]).astype(jnp.float32)
        logits = logits + dcum[:, :, start:end, None] - dcum[:, :, None, :end]
        causal = jnp.arange(start, end)[:, None] >= jnp.arange(end)[None, :]
        probs = jax.nn.softmax(jnp.where(causal, logits, -jnp.inf), axis=-1)
        outs.append(jnp.einsum('bhqk,bkhd->bqhd', probs.astype(v.dtype), v[:, :end]))
    o = jnp.concatenate(outs, axis=1).reshape(bsz, seq, D_MODEL)
    return o @ w_out


def sq_relu_mlp(u, w_up, w_down):
    hid = jax.nn.relu(u @ w_up)
    return (hid * hid) @ w_down


def _fwd_setup_inputs(seed: int = 0) -> dict:
    key = jax.random.key(seed)
    ks = jax.random.split(key, 20)
    f32 = jnp.float32

    def dense(k, shape, fan_in):
        return jax.random.normal(k, shape, f32) * (fan_in ** -0.5)

    def gain(k, shape):
        return 1.0 + 0.02 * jax.random.normal(k, shape, f32)

    return {
        "x": jax.random.normal(ks[0], (BATCH, SEQ, D_MODEL), f32),
        "p": jax.random.normal(ks[1], (DEPTH, BATCH, SEQ, PLE_DIM), f32),
        "mix_norm": gain(ks[2], (DEPTH, D_MODEL)),
        "mlp_norm": gain(ks[3], (DEPTH, D_MODEL)),
        "ple_norm": gain(ks[4], (DEPTH, D_MODEL)),
        "w_a_in": dense(ks[5], (N_A_LAYERS, D_MODEL, 4 * D_MODEL), D_MODEL),
        "a_lb_logits": 0.3 * jax.random.normal(ks[6], (N_A_LAYERS + 1, D_MODEL), f32),
        "a_head_gain": gain(ks[7], (N_A_LAYERS, HGRN_HEAD_DIM)),
        "w_a_out": dense(ks[8], (N_A_LAYERS, D_MODEL, D_MODEL), D_MODEL),
        "kv_norm": gain(ks[9], (D_MODEL,)),
        "w_kvf": dense(ks[10], (D_MODEL, 2 * D_MODEL + FOX_HEADS), D_MODEL),
        "b_f": 2.0 + 0.5 * jax.random.normal(ks[11], (FOX_HEADS,), f32),
        "w_b_q": dense(ks[12], (N_B_LAYERS, D_MODEL, D_MODEL), D_MODEL),
        "w_b_out": dense(ks[13], (N_B_LAYERS, D_MODEL, D_MODEL), D_MODEL),
        "w_mlp_up": dense(ks[14], (DEPTH, D_MODEL, D_FF), D_MODEL),
        "w_mlp_down": dense(ks[15], (DEPTH, D_FF, D_MODEL), D_FF),
        "w_ple_gate": dense(ks[16], (DEPTH, D_MODEL, D_MODEL), D_MODEL),
        "w_ple_up": dense(ks[17], (DEPTH, PLE_DIM, D_MODEL), PLE_DIM),
        "final_norm": gain(ks[18], (D_MODEL,)),
    }


def _fwd_reference(x, p, mix_norm, mlp_norm, ple_norm, w_a_in, a_lb_logits, a_head_gain, w_a_out,
              kv_norm, w_kvf, b_f, w_b_q, w_b_out, w_mlp_up, w_mlp_down, w_ple_gate, w_ple_up,
              final_norm):
    lb_all = jnp.cumsum(jax.nn.softmax(a_lb_logits.astype(jnp.float32), axis=0), axis=0)
    h = x
    k_sh = v_sh = d_sh = None
    for layer in range(DEPTH):
        u = rms_norm(h, mix_norm[layer])
        if layer < N_A_LAYERS:
            h = h + hgrn2_mixer(u, w_a_in[layer], lb_all[layer], a_head_gain[layer], w_a_out[layer])
        else:
            j = layer - N_A_LAYERS
            h = h + fox_mixer(u, k_sh, v_sh, d_sh, w_b_q[j], w_b_out[j])
        h = h + sq_relu_mlp(rms_norm(h, mlp_norm[layer]), w_mlp_up[layer], w_mlp_down[layer])
        gate = jax.nn.sigmoid(rms_norm(h, ple_norm[layer]) @ w_ple_gate[layer])
        h = h + (p[layer].astype(h.dtype) @ w_ple_up[layer]) * gate
        if layer == N_A_LAYERS - 1:
            k_sh, v_sh, d_sh = shared_kv(h, kv_norm, w_kvf, b_f)
    return rms_norm(h, final_norm)


import jax as _jax
import jax.numpy as _jnp

TWIN_FORMAT = 'train_step'
FWD_PARAMS = ['x', 'p', 'mix_norm', 'mlp_norm', 'ple_norm', 'w_a_in', 'a_lb_logits', 'a_head_gain', 'w_a_out', 'kv_norm', 'w_kvf', 'b_f', 'w_b_q', 'w_b_out', 'w_mlp_up', 'w_mlp_down', 'w_ple_gate', 'w_ple_up', 'final_norm']
TWIN_WEIGHTS = ['mix_norm', 'mlp_norm', 'ple_norm', 'w_a_in', 'a_lb_logits', 'a_head_gain', 'w_a_out', 'kv_norm', 'w_kvf', 'b_f', 'w_b_q', 'w_b_out', 'w_mlp_up', 'w_mlp_down', 'w_ple_gate', 'w_ple_up', 'final_norm']
TWIN_DIFF_INPUT = 'x'
TWIN_INPUTS = ['x', 'p', 'mix_norm', 'mlp_norm', 'ple_norm', 'w_a_in', 'a_lb_logits', 'a_head_gain', 'w_a_out', 'kv_norm', 'w_kvf', 'b_f', 'w_b_q', 'w_b_out', 'w_mlp_up', 'w_mlp_down', 'w_ple_gate', 'w_ple_up', 'final_norm', 'loss_target', 'm_mix_norm', 'm_mlp_norm', 'm_ple_norm', 'm_w_a_in', 'm_a_lb_logits', 'm_a_head_gain', 'm_w_a_out', 'm_kv_norm', 'm_w_kvf', 'm_b_f', 'm_w_b_q', 'm_w_b_out', 'm_w_mlp_up', 'm_w_mlp_down', 'm_w_ple_gate', 'm_w_ple_up', 'm_final_norm', 'v_mix_norm', 'v_mlp_norm', 'v_ple_norm', 'v_w_a_in', 'v_a_lb_logits', 'v_a_head_gain', 'v_w_a_out', 'v_kv_norm', 'v_w_kvf', 'v_b_f', 'v_w_b_q', 'v_w_b_out', 'v_w_mlp_up', 'v_w_mlp_down', 'v_w_ple_gate', 'v_w_ple_up', 'v_final_norm']
TWIN_OUTPUTS = ['loss', 'grad_x', 'grad_mix_norm', 'grad_mlp_norm', 'grad_ple_norm', 'grad_w_a_in', 'grad_a_lb_logits', 'grad_a_head_gain', 'grad_w_a_out', 'grad_kv_norm', 'grad_w_kvf', 'grad_b_f', 'grad_w_b_q', 'grad_w_b_out', 'grad_w_mlp_up', 'grad_w_mlp_down', 'grad_w_ple_gate', 'grad_w_ple_up', 'grad_final_norm', 'delta_mix_norm', 'delta_mlp_norm', 'delta_ple_norm', 'delta_w_a_in', 'delta_a_lb_logits', 'delta_a_head_gain', 'delta_w_a_out', 'delta_kv_norm', 'delta_w_kvf', 'delta_b_f', 'delta_w_b_q', 'delta_w_b_out', 'delta_w_mlp_up', 'delta_w_mlp_down', 'delta_w_ple_gate', 'delta_w_ple_up', 'delta_final_norm', 'new_m_mix_norm', 'new_m_mlp_norm', 'new_m_ple_norm', 'new_m_w_a_in', 'new_m_a_lb_logits', 'new_m_a_head_gain', 'new_m_w_a_out', 'new_m_kv_norm', 'new_m_w_kvf', 'new_m_b_f', 'new_m_w_b_q', 'new_m_w_b_out', 'new_m_w_mlp_up', 'new_m_w_mlp_down', 'new_m_w_ple_gate', 'new_m_w_ple_up', 'new_m_final_norm', 'new_v_mix_norm', 'new_v_mlp_norm', 'new_v_ple_norm', 'new_v_w_a_in', 'new_v_a_lb_logits', 'new_v_a_head_gain', 'new_v_w_a_out', 'new_v_kv_norm', 'new_v_w_kvf', 'new_v_b_f', 'new_v_w_b_q', 'new_v_w_b_out', 'new_v_w_mlp_up', 'new_v_w_mlp_down', 'new_v_w_ple_gate', 'new_v_w_ple_up', 'new_v_final_norm']
TWIN_LEAF_KINDS = {'loss': 'loss', 'grad_x': 'grad_x', 'grad_mix_norm': 'grad_w', 'grad_mlp_norm': 'grad_w', 'grad_ple_norm': 'grad_w', 'grad_w_a_in': 'grad_w', 'grad_a_lb_logits': 'grad_w', 'grad_a_head_gain': 'grad_w', 'grad_w_a_out': 'grad_w', 'grad_kv_norm': 'grad_w', 'grad_w_kvf': 'grad_w', 'grad_b_f': 'grad_w', 'grad_w_b_q': 'grad_w', 'grad_w_b_out': 'grad_w', 'grad_w_mlp_up': 'grad_w', 'grad_w_mlp_down': 'grad_w', 'grad_w_ple_gate': 'grad_w', 'grad_w_ple_up': 'grad_w', 'grad_final_norm': 'grad_w', 'delta_mix_norm': 'delta_w', 'delta_mlp_norm': 'delta_w', 'delta_ple_norm': 'delta_w', 'delta_w_a_in': 'delta_w', 'delta_a_lb_logits': 'delta_w', 'delta_a_head_gain': 'delta_w', 'delta_w_a_out': 'delta_w', 'delta_kv_norm': 'delta_w', 'delta_w_kvf': 'delta_w', 'delta_b_f': 'delta_w', 'delta_w_b_q': 'delta_w', 'delta_w_b_out': 'delta_w', 'delta_w_mlp_up': 'delta_w', 'delta_w_mlp_down': 'delta_w', 'delta_w_ple_gate': 'delta_w', 'delta_w_ple_up': 'delta_w', 'delta_final_norm': 'delta_w', 'new_m_mix_norm': 'new_m', 'new_m_mlp_norm': 'new_m', 'new_m_ple_norm': 'new_m', 'new_m_w_a_in': 'new_m', 'new_m_a_lb_logits': 'new_m', 'new_m_a_head_gain': 'new_m', 'new_m_w_a_out': 'new_m', 'new_m_kv_norm': 'new_m', 'new_m_w_kvf': 'new_m', 'new_m_b_f': 'new_m', 'new_m_w_b_q': 'new_m', 'new_m_w_b_out': 'new_m', 'new_m_w_mlp_up': 'new_m', 'new_m_w_mlp_down': 'new_m', 'new_m_w_ple_gate': 'new_m', 'new_m_w_ple_up': 'new_m', 'new_m_final_norm': 'new_m', 'new_v_mix_norm': 'new_v', 'new_v_mlp_norm': 'new_v', 'new_v_ple_norm': 'new_v', 'new_v_w_a_in': 'new_v', 'new_v_a_lb_logits': 'new_v', 'new_v_a_head_gain': 'new_v', 'new_v_w_a_out': 'new_v', 'new_v_kv_norm': 'new_v', 'new_v_w_kvf': 'new_v', 'new_v_b_f': 'new_v', 'new_v_w_b_q': 'new_v', 'new_v_w_b_out': 'new_v', 'new_v_w_mlp_up': 'new_v', 'new_v_w_mlp_down': 'new_v', 'new_v_w_ple_gate': 'new_v', 'new_v_w_ple_up': 'new_v', 'new_v_final_norm': 'new_v'}


def _forward(args):
    return _fwd_reference(*[args[k] for k in FWD_PARAMS])


def _output_shape():
    out = _jax.eval_shape(lambda: _forward(_fwd_setup_inputs(0)))
    return out.shape, out.dtype

N_MICROBATCH = 1
ADAM_LR = 0.001
ADAM_B1 = 0.9
ADAM_B2 = 0.999
ADAM_EPS = 1e-08
ADAM_WD = 0.01
ADAM_STEP = 10
PER_EXAMPLE_BATCH_AXIS = {'x': 0, 'p': 1, 'loss_target': 0}
SHARED_INPUTS = []
_WEIGHT_DTYPES = {'mix_norm': _jnp.float32, 'mlp_norm': _jnp.float32, 'ple_norm': _jnp.float32, 'w_a_in': _jnp.float32, 'a_lb_logits': _jnp.float32, 'a_head_gain': _jnp.float32, 'w_a_out': _jnp.float32, 'kv_norm': _jnp.float32, 'w_kvf': _jnp.float32, 'b_f': _jnp.float32, 'w_b_q': _jnp.float32, 'w_b_out': _jnp.float32, 'w_mlp_up': _jnp.float32, 'w_mlp_down': _jnp.float32, 'w_ple_gate': _jnp.float32, 'w_ple_up': _jnp.float32, 'final_norm': _jnp.float32}
MOMENT_SCALE = {'mix_norm': 3.692543e-02, 'mlp_norm': 4.702948e-02, 'ple_norm': 6.949167e-03, 'w_a_in': 2.512919e-02, 'a_lb_logits': 3.137031e-03, 'a_head_gain': 1.364772e-01, 'w_a_out': 3.470736e-02, 'kv_norm': 2.133268e-02, 'w_kvf': 1.561506e-02, 'b_f': 1.474670e-01, 'w_b_q': 9.764204e-03, 'w_b_out': 1.910803e-02, 'w_mlp_up': 2.330509e-02, 'w_mlp_down': 4.549643e-02, 'w_ple_gate': 6.982030e-03, 'w_ple_up': 1.752519e-02, 'final_norm': 8.136330e+00}


def _to_microbatches(a, axis):
    t = _jnp.moveaxis(a, axis, 0)
    t = t.reshape((N_MICROBATCH, t.shape[0] // N_MICROBATCH) + t.shape[1:])
    return _jnp.moveaxis(t, 1, axis + 1)


def setup_inputs(seed: int = 0) -> dict:
    inp = _fwd_setup_inputs(seed)
    key = _jax.random.fold_in(_jax.random.key(seed), 7919)
    shape, _ = _output_shape()
    out = dict(inp)
    out["loss_target"] = _jax.random.normal(_jax.random.fold_in(key, 0), shape, _jnp.float32)
    for i, name in enumerate(TWIN_WEIGHTS):
        w = inp[name].astype(_jnp.float32)
        if MOMENT_SCALE is None:
            s = _jnp.sqrt(_jnp.mean(_jnp.square(w)) + 1e-30)
        else:
            s = MOMENT_SCALE[name]
        km, kv = _jax.random.split(_jax.random.fold_in(key, i + 1))
        out[name] = w
        out["m_" + name] = s * _jax.random.normal(km, w.shape, _jnp.float32)
        out["v_" + name] = (s * s) * _jax.random.uniform(kv, w.shape, _jnp.float32, 0.5, 1.5)
    if N_MICROBATCH > 1:
        for name, axis in PER_EXAMPLE_BATCH_AXIS.items():
            out[name] = _to_microbatches(out[name], axis)
    return {'x': out['x'], 'p': out['p'], 'mix_norm': out['mix_norm'], 'mlp_norm': out['mlp_norm'], 'ple_norm': out['ple_norm'], 'w_a_in': out['w_a_in'], 'a_lb_logits': out['a_lb_logits'], 'a_head_gain': out['a_head_gain'], 'w_a_out': out['w_a_out'], 'kv_norm': out['kv_norm'], 'w_kvf': out['w_kvf'], 'b_f': out['b_f'], 'w_b_q': out['w_b_q'], 'w_b_out': out['w_b_out'], 'w_mlp_up': out['w_mlp_up'], 'w_mlp_down': out['w_mlp_down'], 'w_ple_gate': out['w_ple_gate'], 'w_ple_up': out['w_ple_up'], 'final_norm': out['final_norm'], 'loss_target': out['loss_target'], 'm_mix_norm': out['m_mix_norm'], 'm_mlp_norm': out['m_mlp_norm'], 'm_ple_norm': out['m_ple_norm'], 'm_w_a_in': out['m_w_a_in'], 'm_a_lb_logits': out['m_a_lb_logits'], 'm_a_head_gain': out['m_a_head_gain'], 'm_w_a_out': out['m_w_a_out'], 'm_kv_norm': out['m_kv_norm'], 'm_w_kvf': out['m_w_kvf'], 'm_b_f': out['m_b_f'], 'm_w_b_q': out['m_w_b_q'], 'm_w_b_out': out['m_w_b_out'], 'm_w_mlp_up': out['m_w_mlp_up'], 'm_w_mlp_down': out['m_w_mlp_down'], 'm_w_ple_gate': out['m_w_ple_gate'], 'm_w_ple_up': out['m_w_ple_up'], 'm_final_norm': out['m_final_norm'], 'v_mix_norm': out['v_mix_norm'], 'v_mlp_norm': out['v_mlp_norm'], 'v_ple_norm': out['v_ple_norm'], 'v_w_a_in': out['v_w_a_in'], 'v_a_lb_logits': out['v_a_lb_logits'], 'v_a_head_gain': out['v_a_head_gain'], 'v_w_a_out': out['v_w_a_out'], 'v_kv_norm': out['v_kv_norm'], 'v_w_kvf': out['v_w_kvf'], 'v_b_f': out['v_b_f'], 'v_w_b_q': out['v_w_b_q'], 'v_w_b_out': out['v_w_b_out'], 'v_w_mlp_up': out['v_w_mlp_up'], 'v_w_mlp_down': out['v_w_mlp_down'], 'v_w_ple_gate': out['v_w_ple_gate'], 'v_w_ple_up': out['v_w_ple_up'], 'v_final_norm': out['v_final_norm']}


def _loss(weights, diff, rest, loss_target):
    with _jax.named_scope("forward"):
        args = {**rest, TWIN_DIFF_INPUT: diff, **{k: w.astype(_WEIGHT_DTYPES[k]) for k, w in weights.items()}}
        y = _forward(args)
    with _jax.named_scope("loss_head"):
        err = _jnp.square(y.astype(_jnp.float32) - loss_target)
        return 0.5 * _jnp.sum(_jnp.mean(err, axis=-1)) if err.ndim else 0.5 * err


def _adamw(w, g, m, v):
    m = ADAM_B1 * m + (1.0 - ADAM_B1) * g
    v = ADAM_B2 * v + (1.0 - ADAM_B2) * _jnp.square(g)
    m_hat = m / (1.0 - ADAM_B1 ** ADAM_STEP)
    v_hat = v / (1.0 - ADAM_B2 ** ADAM_STEP)
    delta = -ADAM_LR * (m_hat / (_jnp.sqrt(v_hat) + ADAM_EPS) + ADAM_WD * w)
    return delta, m, v


def reference(x, p, mix_norm, mlp_norm, ple_norm, w_a_in, a_lb_logits, a_head_gain, w_a_out, kv_norm, w_kvf, b_f, w_b_q, w_b_out, w_mlp_up, w_mlp_down, w_ple_gate, w_ple_up, final_norm, loss_target, m_mix_norm, m_mlp_norm, m_ple_norm, m_w_a_in, m_a_lb_logits, m_a_head_gain, m_w_a_out, m_kv_norm, m_w_kvf, m_b_f, m_w_b_q, m_w_b_out, m_w_mlp_up, m_w_mlp_down, m_w_ple_gate, m_w_ple_up, m_final_norm, v_mix_norm, v_mlp_norm, v_ple_norm, v_w_a_in, v_a_lb_logits, v_a_head_gain, v_w_a_out, v_kv_norm, v_w_kvf, v_b_f, v_w_b_q, v_w_b_out, v_w_mlp_up, v_w_mlp_down, v_w_ple_gate, v_w_ple_up, v_final_norm):
    given = dict(x=x, p=p, mix_norm=mix_norm, mlp_norm=mlp_norm, ple_norm=ple_norm, w_a_in=w_a_in, a_lb_logits=a_lb_logits, a_head_gain=a_head_gain, w_a_out=w_a_out, kv_norm=kv_norm, w_kvf=w_kvf, b_f=b_f, w_b_q=w_b_q, w_b_out=w_b_out, w_mlp_up=w_mlp_up, w_mlp_down=w_mlp_down, w_ple_gate=w_ple_gate, w_ple_up=w_ple_up, final_norm=final_norm, loss_target=loss_target, m_mix_norm=m_mix_norm, m_mlp_norm=m_mlp_norm, m_ple_norm=m_ple_norm, m_w_a_in=m_w_a_in, m_a_lb_logits=m_a_lb_logits, m_a_head_gain=m_a_head_gain, m_w_a_out=m_w_a_out, m_kv_norm=m_kv_norm, m_w_kvf=m_w_kvf, m_b_f=m_b_f, m_w_b_q=m_w_b_q, m_w_b_out=m_w_b_out, m_w_mlp_up=m_w_mlp_up, m_w_mlp_down=m_w_mlp_down, m_w_ple_gate=m_w_ple_gate, m_w_ple_up=m_w_ple_up, m_final_norm=m_final_norm, v_mix_norm=v_mix_norm, v_mlp_norm=v_mlp_norm, v_ple_norm=v_ple_norm, v_w_a_in=v_w_a_in, v_a_lb_logits=v_a_lb_logits, v_a_head_gain=v_a_head_gain, v_w_a_out=v_w_a_out, v_kv_norm=v_kv_norm, v_w_kvf=v_w_kvf, v_b_f=v_b_f, v_w_b_q=v_w_b_q, v_w_b_out=v_w_b_out, v_w_mlp_up=v_w_mlp_up, v_w_mlp_down=v_w_mlp_down, v_w_ple_gate=v_w_ple_gate, v_w_ple_up=v_w_ple_up, v_final_norm=v_final_norm)
    weights = {n: given[n] for n in TWIN_WEIGHTS}
    shared = {n: given[n] for n in SHARED_INPUTS}
    per_example = {n: given[n] for n in ['x', 'p']}
    grad_fn = _jax.value_and_grad(_loss, argnums=(0, 1))

    def one_microbatch(ex, loss_target):
        ex = dict(ex)
        diff = ex.pop(TWIN_DIFF_INPUT)
        return grad_fn(weights, diff, {**shared, **ex}, loss_target)

    if N_MICROBATCH == 1:
        loss, (grad_w, grad_x) = one_microbatch(per_example, given["loss_target"])
    else:
        def body(carry, xs):
            loss_sum, grad_sum = carry
            l_k, (gw_k, gx_k) = one_microbatch(xs[0], xs[1])
            with _jax.named_scope("update"):
                return (loss_sum + l_k, _jax.tree.map(_jnp.add, grad_sum, gw_k)), gx_k

        init = (_jnp.zeros((), _jnp.float32), _jax.tree.map(_jnp.zeros_like, weights))
        (loss, grad_w), grad_x = _jax.lax.scan(body, init, (per_example, given["loss_target"]))
    with _jax.named_scope("update"):
        delta_w, new_m, new_v = {}, {}, {}
        for n in TWIN_WEIGHTS:
            delta_w[n], new_m[n], new_v[n] = _adamw(weights[n], grad_w[n], given["m_" + n], given["v_" + n])
    return (loss, grad_x, *[grad_w[n] for n in TWIN_WEIGHTS], *[delta_w[n] for n in TWIN_WEIGHTS],
            *[new_m[n] for n in TWIN_WEIGHTS], *[new_v[n] for n in TWIN_WEIGHTS])
```

```python
import functools

import jax
import jax.numpy as jnp
from jax import lax
from jax.experimental import pallas as pl
from jax.experimental.pallas import tpu as pltpu

F32 = jnp.float32
BF16 = jnp.bfloat16
HEAD_DIM = 128
CHUNK = 16
TILE = 128
NORM_EPS = 1e-6
N_DEV = 8
MESH_AXES = ("x", "y", "c")
VMEM_LIMIT_BYTES = 48 * 1024 * 1024
ROW_TILE_BYTES = 1024 * 1024
LR, B1, B2, ADAM_EPS, WD, STEP = 0.001, 0.9, 0.999, 1e-08, 0.01, 10
NEG_BIG = -1e30

NN = (((1,), (0,)), ((), ()))
NT = (((1,), (1,)), ((), ()))
TN = (((0,), (0,)), ((), ()))


def _params(semantics):
    return pltpu.CompilerParams(dimension_semantics=semantics, vmem_limit_bytes=VMEM_LIMIT_BYTES)


def _tile(n, prefs):
    for p in prefs:
        if n % p == 0:
            return p
    return n


def _row_tile(rows, limit):
    for cand in (2048, 1024, 512, 256, 128, 64, 32, 16):
        if cand <= limit and rows % cand == 0:
            return cand
    return rows


def _mm_call(name, a, b, dims, grid, a_spec, b_spec, o_spec, o_shape, acc_shape, k_axes, out_dtype):
    nk = 1
    for ax in k_axes:
        nk *= grid[ax]

    def body(a_ref, b_ref, o_ref, acc_ref):
        k = 0
        for ax in k_axes:
            k = k * grid[ax] + pl.program_id(ax)

        @pl.when(k == 0)
        def _():
            acc_ref[...] = jnp.zeros_like(acc_ref)

        acc_ref[...] += lax.dot_general(a_ref[...], b_ref[...], dims, preferred_element_type=F32)

        @pl.when(k == nk - 1)
        def _():
            o_ref[...] = acc_ref[...].astype(o_ref.dtype)

    sem = tuple("arbitrary" if ax in k_axes else "parallel" for ax in range(len(grid)))
    return pl.pallas_call(
        body, name=name, grid=grid, in_specs=[a_spec, b_spec], out_specs=o_spec,
        out_shape=jax.ShapeDtypeStruct(o_shape, out_dtype),
        scratch_shapes=[pltpu.VMEM(acc_shape, F32)], compiler_params=_params(sem),
    )(a, b)


def mm_nn(name, a, b3, out_dtype=F32, out3=False):
    m, k = a.shape
    g, _, n = b3.shape
    tm, tk = _tile(m, (1024, 512, 256)), _tile(k, (512, 256))
    tn = n if out3 else _tile(n, (1024, 512, 256, 128))
    nj = n // tn
    grid = (m // tm, g, nj, k // tk)
    a_spec = pl.BlockSpec((tm, tk), lambda i, gg, j, kk: (i, kk))
    b_spec = pl.BlockSpec((None, tk, tn), lambda i, gg, j, kk: (gg, kk, j))
    if out3:
        o_spec = pl.BlockSpec((None, tm, tn), lambda i, gg, j, kk: (gg, i, j))
        o_shape = (g, m, n)
    else:
        o_spec = pl.BlockSpec((tm, tn), lambda i, gg, j, kk: (i, gg * nj + j))
        o_shape = (m, g * n)
    return _mm_call(name, a, b3, NN, grid, a_spec, b_spec, o_spec, o_shape, (tm, tn), (3,), out_dtype)


def mm_nt(name, a, b3, out_dtype=F32):
    g, k, n = b3.shape
    a3 = a.ndim == 3
    m = a.shape[1] if a3 else a.shape[0]
    tm, tko = _tile(m, (1024, 512, 256)), _tile(k, (1024, 512, 256))
    tc = n if a3 else _tile(n, (512, 256, 128))
    nc = n // tc
    grid = (m // tm, k // tko, g, nc)
    if a3:
        a_spec = pl.BlockSpec((None, tm, tc), lambda i, j, gg, c: (gg, i, c))
    else:
        a_spec = pl.BlockSpec((tm, tc), lambda i, j, gg, c: (i, gg * nc + c))
    b_spec = pl.BlockSpec((None, tko, tc), lambda i, j, gg, c: (gg, j, c))
    o_spec = pl.BlockSpec((tm, tko), lambda i, j, gg, c: (i, j))
    return _mm_call(name, a, b3, NT, grid, a_spec, b_spec, o_spec, (m, k), (tm, tko), (2, 3), out_dtype)


def mm_tn(name, a, b, g, out_dtype=BF16):
    t, k = a.shape
    b3 = b.ndim == 3
    n = b.shape[2] if b3 else b.shape[1] // g
    tm = _tile(k, (1024, 512, 256))
    tn = n if b3 else _tile(n, (1024, 512, 256, 128))
    tt = _tile(t, (512, 256))
    nj = n // tn
    grid = (g, k // tm, nj, t // tt)
    a_spec = pl.BlockSpec((tt, tm), lambda gg, i, j, s: (s, i))
    if b3:
        b_spec = pl.BlockSpec((None, tt, tn), lambda gg, i, j, s: (gg, s, j))
    else:
        b_spec = pl.BlockSpec((tt, tn), lambda gg, i, j, s: (s, gg * nj + j))
    o_spec = pl.BlockSpec((None, tm, tn), lambda gg, i, j, s: (gg, i, j))
    return _mm_call(name, a, b, TN, grid, a_spec, b_spec, o_spec, (g, k, n), (tm, tn), (3,), out_dtype)


def rowwise(name, fn, rows, vecs=()):
    t = rows[0].shape[0]
    wmax = max(r.shape[1] for r in rows)
    tr = _row_tile(t, ROW_TILE_BYTES // (4 * wmax))
    row_s = [jax.ShapeDtypeStruct((tr, r.shape[1]), r.dtype) for r in rows]
    vec_s = [jax.ShapeDtypeStruct(v.shape, v.dtype) for v in vecs]
    out_rows_s, out_sums_s = jax.eval_shape(fn, *row_s, *vec_s)
    n_in, n_r = len(rows) + len(vecs), len(out_rows_s)

    def body(*refs):
        i = pl.program_id(0)
        o_rows, o_sums = fn(*[r[...] for r in refs[:n_in]])
        for ref, val in zip(refs[n_in:n_in + n_r], o_rows):
            ref[...] = val

        if out_sums_s:
            @pl.when(i == 0)
            def _():
                for ref in refs[n_in + n_r:]:
                    ref[...] = jnp.zeros_like(ref)

            for ref, val in zip(refs[n_in + n_r:], o_sums):
                ref[...] += val

    in_specs = [pl.BlockSpec((tr, r.shape[1]), lambda i: (i, 0)) for r in rows]
    in_specs += [pl.BlockSpec(v.shape, lambda i: (0, 0)) for v in vecs]
    out_specs = [pl.BlockSpec((tr, s.shape[1]), lambda i: (i, 0)) for s in out_rows_s]
    out_specs += [pl.BlockSpec(s.shape, lambda i: (0, 0)) for s in out_sums_s]
    out_shape = [jax.ShapeDtypeStruct((t, s.shape[1]), s.dtype) for s in out_rows_s]
    out_shape += [jax.ShapeDtypeStruct(s.shape, s.dtype) for s in out_sums_s]
    outs = pl.pallas_call(
        body, name=name, grid=(t // tr,), in_specs=in_specs, out_specs=out_specs, out_shape=out_shape,
        compiler_params=_params(("arbitrary",)),
    )(*rows, *vecs)
    return outs[:n_r], outs[n_r:]


def _rms(x, gain):
    return x * lax.rsqrt(jnp.mean(x * x, axis=-1, keepdims=True) + NORM_EPS) * gain


def _norm_fwd(x, gain):
    return (_rms(x, gain).astype(BF16),), ()


def _add_norm_fwd(h, a, gain):
    h = h + a
    return (h, _rms(h, gain).astype(BF16)), ()


def _relu2_fwd(pre):
    r = jnp.maximum(pre, 0.0)
    return ((r * r).astype(BF16),), ()


def _ple(h, gpre, pu):
    return h + pu * jax.nn.sigmoid(gpre)


def _ple_two_norms_fwd(h, gpre, pu, gain_a, gain_b):
    h = _ple(h, gpre, pu)
    return (h, _rms(h, gain_a).astype(BF16), _rms(h, gain_b).astype(BF16)), ()


def _tail_fwd_bwd(h, gpre, pu, target, gain):
    def row_loss(h, gpre, pu, gain):
        y = _rms(_ple(h, gpre, pu), gain)
        return 0.5 * jnp.mean(jnp.square(y - target), axis=-1, keepdims=True)

    loss, vjp = jax.vjp(row_loss, h, gpre, pu, gain)
    dh, dgpre, dpu, dgain = vjp(jnp.ones_like(loss))
    loss = jnp.broadcast_to(jnp.sum(loss, axis=0, keepdims=True), (1, 128))
    return (dh, dgpre.astype(BF16), dpu.astype(BF16)), (dgain, loss)


def _ple_bwd(gpre, pu, dh):
    _, vjp = jax.vjp(lambda g, u: pu_times_gate(g, u), gpre, pu)
    dgpre, dpu = vjp(dh)
    return (dgpre.astype(BF16), dpu.astype(BF16)), ()


def pu_times_gate(gpre, pu):
    return pu * jax.nn.sigmoid(gpre)


def _norm_bwd(h, du, dh_in, gain):
    _, vjp = jax.vjp(_rms, h, gain)
    dh, dgain = vjp(du)
    dh = dh_in + dh
    return (dh, dh.astype(BF16)), (dgain,)


def _two_norms_bwd(h, du_a, du_b, dh_in, gain_a, gain_b):
    _, vjp = jax.vjp(lambda h, ga, gb: (_rms(h, ga), _rms(h, gb)), h, gain_a, gain_b)
    dh, dga, dgb = vjp((du_a, du_b))
    return (dh_in + dh,), (dga, dgb)


def _relu2_bwd(pre, dact):
    return ((dact * 2.0 * jnp.maximum(pre, 0.0)).astype(BF16),), ()


def _bf16_dot(dims_fwd, dims_da, dims_db, swap_da, swap_db):
    @jax.custom_vjp
    def dot(a, b):
        return lax.dot_general(a.astype(BF16), b.astype(BF16), dims_fwd, preferred_element_type=F32)

    def fwd(a, b):
        return dot(a, b), (a, b)

    def bwd(res, ct):
        a, b = res
        ct, a, b = ct.astype(BF16), a.astype(BF16), b.astype(BF16)
        da = lax.dot_general(*((b, ct) if swap_da else (ct, b)), dims_da, preferred_element_type=F32)
        db = lax.dot_general(*((ct, a) if swap_db else (a, ct)), dims_db, preferred_element_type=F32)
        return da, db

    dot.defvjp(fwd, bwd)
    return dot


_dot_nn = _bf16_dot(NN, NT, TN, False, False)
_dot_nt = _bf16_dot(NT, NN, TN, False, True)
_dot_tn = _bf16_dot(TN, NT, NN, True, False)


def _chunk_masks(transposed):
    r = lax.broadcasted_iota(jnp.int32, (TILE, TILE), 0)
    c = lax.broadcasted_iota(jnp.int32, (TILE, TILE), 1)
    same = (r // CHUNK) == (c // CHUNK)
    causal = same & ((r <= c) if transposed else (c <= r))
    return causal, same


@jax.custom_vjp
def _chunk_sums(x):
    causal, same = _chunk_masks(False)
    m = jnp.concatenate([causal.astype(F32), same.astype(F32)], axis=0)
    return jnp.dot(m, x, precision=lax.Precision.HIGHEST, preferred_element_type=F32)


def _chunk_sums_fwd(x):
    return _chunk_sums(x), None


def _chunk_sums_bwd(_, ct):
    causal_t, same = _chunk_masks(True)
    m_t = jnp.concatenate([causal_t.astype(F32), same.astype(F32)], axis=1)
    return (jnp.dot(m_t, ct, precision=lax.Precision.HIGHEST, preferred_element_type=F32),)


_chunk_sums.defvjp(_chunk_sums_fwd, _chunk_sums_bwd)


def _hgrn_tile(q, f, i, g, lgt, hg, st):
    d = q.shape[1]
    l0, l1 = lgt[0:1], lgt[1:2]
    mx = jnp.maximum(l0, l1)
    e0, e1 = jnp.exp(l0 - mx), jnp.exp(l1 - mx)
    lb = e0 / (e0 + e1)
    fg = lb + (1.0 - lb) * jax.nn.sigmoid(f)
    k = 1.0 - fg
    causal, _ = _chunk_masks(False)
    both = _chunk_sums(jnp.log(fg))
    b, b_last = both[:TILE], both[TILE:]
    q_in = q * jax.nn.sigmoid(q) * (d ** -0.5) * jnp.exp(b)
    k_in = k * jnp.exp(-b)
    k_end = k * jnp.exp(b_last - b)
    att = jnp.where(causal, _dot_nt(q_in, k_in), 0.0)
    o_intra = _dot_nn(att, i)
    outs = []
    for n in range(TILE // CHUNK):
        rows = slice(n * CHUNK, (n + 1) * CHUNK)
        outs.append(o_intra[rows] + _dot_nt(q_in[rows], st))
        decay = jnp.exp(jnp.mean(b_last[rows], axis=0, keepdims=True))
        st = st * decay + _dot_tn(i[rows], k_end[rows])
    o = jnp.concatenate(outs, axis=0)
    o = o * lax.rsqrt(jnp.mean(o * o, axis=-1, keepdims=True) + NORM_EPS) * hg
    return o * (g * jax.nn.sigmoid(g)), st


def hgrn_fwd(z, lgt, hg):
    t, d4 = z.shape
    d = d4 // 4
    nh, nt = d // HEAD_DIM, t // TILE

    def body(q_ref, f_ref, i_ref, g_ref, lgt_ref, hg_ref, o_ref, st_out_ref, st_ref):
        tt = pl.program_id(1)

        @pl.when(tt == 0)
        def _():
            st_ref[...] = jnp.zeros_like(st_ref)

        st = st_ref[...]
        st_out_ref[...] = st
        o, st = _hgrn_tile(q_ref[...], f_ref[...], i_ref[...], g_ref[...], lgt_ref[...], hg_ref[...], st)
        o_ref[...] = o.astype(o_ref.dtype)
        st_ref[...] = st

    def part(p):
        return pl.BlockSpec((TILE, HEAD_DIM), lambda h, tt: (tt, p * nh + h))

    return pl.pallas_call(
        body, name="hgrn_fwd", grid=(nh, nt),
        in_specs=[part(0), part(1), part(2), part(3),
                  pl.BlockSpec((2, HEAD_DIM), lambda h, tt: (0, h)),
                  pl.BlockSpec((1, HEAD_DIM), lambda h, tt: (0, 0))],
        out_specs=[pl.BlockSpec((TILE, HEAD_DIM), lambda h, tt: (tt, h)),
                   pl.BlockSpec((None, None, HEAD_DIM, HEAD_DIM), lambda h, tt: (h, tt, 0, 0))],
        out_shape=[jax.ShapeDtypeStruct((t, d), BF16),
                   jax.ShapeDtypeStruct((nh, nt, HEAD_DIM, HEAD_DIM), F32)],
        scratch_shapes=[pltpu.VMEM((HEAD_DIM, HEAD_DIM), F32)],
        compiler_params=_params(("parallel", "arbitrary")),
    )(z, z, z, z, lgt, hg)


def hgrn_bwd(z, lgt, hg, states, dout):
    t, d4 = z.shape
    d = d4 // 4
    nh, nt = d // HEAD_DIM, t // TILE

    def body(q_ref, f_ref, i_ref, g_ref, lgt_ref, hg_ref, st_in_ref, do_ref, dz_ref, dlgt_ref, dhg_ref, dst_ref):
        h, tt = pl.program_id(0), pl.program_id(1)

        @pl.when(tt == 0)
        def _():
            dst_ref[...] = jnp.zeros_like(dst_ref)
            dlgt_ref[...] = jnp.zeros_like(dlgt_ref)

        @pl.when((tt == 0) & (h == 0))
        def _():
            dhg_ref[...] = jnp.zeros_like(dhg_ref)

        _, vjp = jax.vjp(_hgrn_tile, q_ref[...], f_ref[...], i_ref[...], g_ref[...], lgt_ref[...], hg_ref[...],
                         st_in_ref[...])
        dq, df, di, dg, dlgt, dhg, dst = vjp((do_ref[...], dst_ref[...]))
        dz_ref[0] = dq.astype(dz_ref.dtype)
        dz_ref[1] = df.astype(dz_ref.dtype)
        dz_ref[2] = di.astype(dz_ref.dtype)
        dz_ref[3] = dg.astype(dz_ref.dtype)
        dlgt_ref[...] += dlgt
        dhg_ref[...] += dhg
        dst_ref[...] = dst

    def part(p):
        return pl.BlockSpec((TILE, HEAD_DIM), lambda h, tt: (nt - 1 - tt, p * nh + h))

    return pl.pallas_call(
        body, name="hgrn_bwd", grid=(nh, nt),
        in_specs=[part(0), part(1), part(2), part(3),
                  pl.BlockSpec((2, HEAD_DIM), lambda h, tt: (0, h)),
                  pl.BlockSpec((1, HEAD_DIM), lambda h, tt: (0, 0)),
                  pl.BlockSpec((None, None, HEAD_DIM, HEAD_DIM), lambda h, tt: (h, nt - 1 - tt, 0, 0)),
                  pl.BlockSpec((TILE, HEAD_DIM), lambda h, tt: (nt - 1 - tt, h))],
        out_specs=[pl.BlockSpec((4, TILE, HEAD_DIM), lambda h, tt: (0, nt - 1 - tt, h)),
                   pl.BlockSpec((2, HEAD_DIM), lambda h, tt: (0, h)),
                   pl.BlockSpec((1, HEAD_DIM), lambda h, tt: (0, 0))],
        out_shape=[jax.ShapeDtypeStruct((4, t, d), BF16),
                   jax.ShapeDtypeStruct((2, d), F32),
                   jax.ShapeDtypeStruct((1, HEAD_DIM), F32)],
        scratch_shapes=[pltpu.VMEM((HEAD_DIM, HEAD_DIM), F32)],
        compiler_params=_params(("arbitrary", "arbitrary")),
    )(z, z, z, z, lgt, hg, states, dout)


def _log_sigmoid(x):
    return jnp.minimum(x, 0.0) - jnp.log(1.0 + jnp.exp(-jnp.abs(x)))


def decay_fwd(fl_t, b_f):
    nh, t = fl_t.shape

    def body(fl_ref, b_ref, out_ref):
        r = lax.broadcasted_iota(jnp.int32, (128, 128), 0)
        c = lax.broadcasted_iota(jnp.int32, (128, 128), 1)
        upper = (r <= c).astype(F32)
        carry = jnp.zeros((nh, 1), F32)
        for j in range(t // 128):
            cols = slice(j * 128, (j + 1) * 128)
            ls = _log_sigmoid(fl_ref[:, cols] + b_ref[...])
            out_ref[:, cols] = carry + jnp.dot(ls, upper, precision=lax.Precision.HIGHEST,
                                               preferred_element_type=F32)
            carry = carry + jnp.sum(ls, axis=1, keepdims=True)

    return pl.pallas_call(body, name="decay_fwd", out_shape=jax.ShapeDtypeStruct((nh, t), F32),
                          compiler_params=_params(None))(fl_t, b_f)


def decay_bwd(fl_t, b_f, ddcum):
    nh, t = fl_t.shape

    def body(fl_ref, b_ref, dd_ref, dfl_ref, db_ref):
        r = lax.broadcasted_iota(jnp.int32, (128, 128), 0)
        c = lax.broadcasted_iota(jnp.int32, (128, 128), 1)
        lower = (r >= c).astype(F32)
        carry = jnp.zeros((nh, 1), F32)
        db = jnp.zeros((nh, 1), F32)
        for j in reversed(range(t // 128)):
            cols = slice(j * 128, (j + 1) * 128)
            dd = dd_ref[:, cols]
            dls = carry + jnp.dot(dd, lower, precision=lax.Precision.HIGHEST, preferred_element_type=F32)
            carry = carry + jnp.sum(dd, axis=1, keepdims=True)
            dfl = dls * jax.nn.sigmoid(-(fl_ref[:, cols] + b_ref[...]))
            dfl_ref[:, cols] = dfl
            db = db + jnp.sum(dfl, axis=1, keepdims=True)
        db_ref[...] = db

    return pl.pallas_call(body, name="decay_bwd",
                          out_shape=[jax.ShapeDtypeStruct((nh, t), F32), jax.ShapeDtypeStruct((nh, 1), F32)],
                          compiler_params=_params(None))(fl_t, b_f, ddcum)


def _attn_probs_logits(q_ref, k_ref, dcol_ref, drow_ref, i, tq, t):
    qs = (q_ref[...] * (HEAD_DIM ** -0.5)).astype(BF16)
    s = lax.dot_general(qs, k_ref[...], NT, preferred_element_type=F32)
    s = s + dcol_ref[...] - drow_ref[...]
    row = i * tq + lax.broadcasted_iota(jnp.int32, (tq, t), 0)
    col = lax.broadcasted_iota(jnp.int32, (tq, t), 1)
    mask = col <= row
    return qs, jnp.where(mask, s, NEG_BIG), mask


def attn_fwd(q, k, v, dcol, drow):
    t, d = q.shape
    nh = d // HEAD_DIM
    tq = _tile(t, (256, 128))

    def body(q_ref, k_ref, v_ref, dcol_ref, drow_ref, o_ref, lse_ref):
        i = pl.program_id(1)
        _, s, _ = _attn_probs_logits(q_ref, k_ref, dcol_ref, drow_ref, i, tq, t)
        m = jnp.max(s, axis=1, keepdims=True)
        p = jnp.exp(s - m)
        l = jnp.sum(p, axis=1, keepdims=True)
        o = jnp.dot(p.astype(BF16), v_ref[...], preferred_element_type=F32)
        o_ref[...] = (o / l).astype(o_ref.dtype)
        lse_ref[...] = m + jnp.log(l)

    return pl.pallas_call(
        body, name="attn_fwd", grid=(nh, t // tq),
        in_specs=[pl.BlockSpec((tq, HEAD_DIM), lambda h, i: (i, h)),
                  pl.BlockSpec((t, HEAD_DIM), lambda h, i: (0, h)),
                  pl.BlockSpec((t, HEAD_DIM), lambda h, i: (0, h)),
                  pl.BlockSpec((None, tq, 1), lambda h, i: (h, i, 0)),
                  pl.BlockSpec((None, 1, t), lambda h, i: (h, 0, 0))],
        out_specs=[pl.BlockSpec((tq, HEAD_DIM), lambda h, i: (i, h)),
                   pl.BlockSpec((None, tq, 1), lambda h, i: (h, i, 0))],
        out_shape=[jax.ShapeDtypeStruct((t, d), BF16), jax.ShapeDtypeStruct((nh, t, 1), F32)],
        compiler_params=_params(("parallel", "parallel")),
    )(q, k, v, dcol, drow)


def attn_bwd(q, k, v, dcol, drow, lse, do):
    t, d = q.shape
    nh = d // HEAD_DIM
    tq = _tile(t, (256, 128))
    nq = t // tq

    def body(q_ref, k_ref, v_ref, dcol_ref, drow_ref, lse_ref, do_ref,
             dq_ref, dk_ref, dv_ref, ddrow_ref, dk_acc, dv_acc):
        i = pl.program_id(1)

        @pl.when(i == 0)
        def _():
            dk_acc[...] = jnp.zeros_like(dk_acc)
            dv_acc[...] = jnp.zeros_like(dv_acc)
            ddrow_ref[...] = jnp.zeros_like(ddrow_ref)

        qs, s, mask = _attn_probs_logits(q_ref, k_ref, dcol_ref, drow_ref, i, tq, t)
        p = jnp.where(mask, jnp.exp(s - lse_ref[...]), 0.0)
        do = do_ref[...]
        dp = lax.dot_general(do, v_ref[...], NT, preferred_element_type=F32)
        ds = p * (dp - jnp.sum(p * dp, axis=1, keepdims=True))
        dsb = ds.astype(BF16)
        dq = jnp.dot(dsb, k_ref[...], preferred_element_type=F32) * (HEAD_DIM ** -0.5)
        dq_ref[...] = dq.astype(dq_ref.dtype)
        dk_acc[...] += lax.dot_general(dsb, qs, TN, preferred_element_type=F32)
        dv_acc[...] += lax.dot_general(p.astype(BF16), do, TN, preferred_element_type=F32)
        ddrow_ref[...] -= jnp.sum(ds, axis=0, keepdims=True)

        @pl.when(i == nq - 1)
        def _():
            dk_ref[...] = dk_acc[...].astype(dk_ref.dtype)
            dv_ref[...] = dv_acc[...].astype(dv_ref.dtype)

    tile = pl.BlockSpec((tq, HEAD_DIM), lambda h, i: (i, h))
    full = pl.BlockSpec((t, HEAD_DIM), lambda h, i: (0, h))
    col = pl.BlockSpec((None, tq, 1), lambda h, i: (h, i, 0))
    rowv = pl.BlockSpec((None, 1, t), lambda h, i: (h, 0, 0))
    return pl.pallas_call(
        body, name="attn_bwd", grid=(nh, nq),
        in_specs=[tile, full, full, col, rowv, col, tile],
        out_specs=[tile, full, full, rowv],
        out_shape=[jax.ShapeDtypeStruct((t, d), BF16), jax.ShapeDtypeStruct((t, d), BF16),
                   jax.ShapeDtypeStruct((t, d), BF16), jax.ShapeDtypeStruct((nh, 1, t), F32)],
        scratch_shapes=[pltpu.VMEM((t, HEAD_DIM), F32), pltpu.VMEM((t, HEAD_DIM), F32)],
        compiler_params=_params(("parallel", "arbitrary")),
    )(q, k, v, dcol, drow, lse, do)


def _my_index():
    return (lax.axis_index("x") * 2 + lax.axis_index("y")) * 2 + lax.axis_index("c")


def _exchange(name, src, gather):
    shape = src.shape if gather else src.shape[1:]

    def body(src_ref, out_ref, send_sems, recv_sems, local_sem):
        x, y, c = (lax.axis_index(a) for a in MESH_AXES)
        me = (x * 2 + y) * 2 + c
        mine = src_ref if gather else src_ref.at[me]
        local = pltpu.make_async_copy(mine, out_ref.at[me], local_sem)
        local.start()
        copies = []
        for dlt in range(1, N_DEV):
            dx, dy, dc = dlt // 4, (dlt // 2) % 2, dlt % 2
            px, py, pc = x ^ dx, y ^ dy, c ^ dc
            peer = (px * 2 + py) * 2 + pc
            copies.append(pltpu.make_async_remote_copy(
                src_ref=src_ref if gather else src_ref.at[peer], dst_ref=out_ref.at[me],
                send_sem=send_sems.at[dlt - 1], recv_sem=recv_sems.at[dlt - 1],
                device_id=(px, py, pc), device_id_type=pl.DeviceIdType.MESH))
        for cp in copies:
            cp.start()
        for cp in copies:
            cp.wait_recv()
        for cp in copies:
            cp.wait_send()
        local.wait()

    return pl.pallas_call(
        body, name=name, out_shape=jax.ShapeDtypeStruct((N_DEV,) + tuple(shape), src.dtype),
        in_specs=[pl.BlockSpec(memory_space=pl.ANY)], out_specs=pl.BlockSpec(memory_space=pl.ANY),
        scratch_shapes=[pltpu.SemaphoreType.DMA((N_DEV - 1,)), pltpu.SemaphoreType.DMA((N_DEV - 1,)),
                        pltpu.SemaphoreType.DMA],
        compiler_params=pltpu.CompilerParams(has_side_effects=True),
    )(src)


def all_gather(name, x):
    return _exchange(name, x, True)


def scatter_to_owners(name, g):
    return _exchange(name, g, False)


def adamw_reduce(name, parts, w, m, v):
    r, wd = w.shape
    tr = _row_tile(r, 2 * ROW_TILE_BYTES // (8 * wd))

    def body(p_ref, w_ref, m_ref, v_ref, g_ref, d_ref, nm_ref, nv_ref):
        g = p_ref[0].astype(F32)
        for dev in range(1, N_DEV):
            g = g + p_ref[dev].astype(F32)
        nm = B1 * m_ref[...] + (1.0 - B1) * g
        nv = B2 * v_ref[...] + (1.0 - B2) * jnp.square(g)
        m_hat = nm / (1.0 - B1 ** STEP)
        v_hat = nv / (1.0 - B2 ** STEP)
        g_ref[...] = g
        d_ref[...] = -LR * (m_hat / (jnp.sqrt(v_hat) + ADAM_EPS) + WD * w_ref[...])
        nm_ref[...] = nm
        nv_ref[...] = nv

    spec = pl.BlockSpec((tr, wd), lambda i: (i, 0))
    return pl.pallas_call(
        body, name=name, grid=(r // tr,),
        in_specs=[pl.BlockSpec((N_DEV, tr, wd), lambda i: (0, i, 0)), spec, spec, spec],
        out_specs=[spec] * 4, out_shape=[jax.ShapeDtypeStruct((r, wd), F32)] * 4,
        compiler_params=_params(("parallel",)),
    )(parts, w, m, v)


def _pack_rows(vectors, rows=None):
    flat = jnp.concatenate([a.reshape(-1).astype(F32) for a in vectors])
    n = flat.shape[0]
    if rows is None:
        rows = -(-n // 1024) * 8
    return jnp.pad(flat, (0, rows * 128 - n)).reshape(rows, 128)


def _unpack_rows(packed, like):
    flat = packed.reshape(-1)
    out, pos = [], 0
    for a in like:
        out.append(flat[pos:pos + a.size].reshape(a.shape))
        pos += a.size
    return out


def kernel(x, p, mix_norm, mlp_norm, ple_norm, w_a_in, a_lb_logits, a_head_gain, w_a_out, kv_norm, w_kvf, b_f, w_b_q, w_b_out, w_mlp_up, w_mlp_down, w_ple_gate, w_ple_up, final_norm, loss_target, m_mix_norm, m_mlp_norm, m_ple_norm, m_w_a_in, m_a_lb_logits, m_a_head_gain, m_w_a_out, m_kv_norm, m_w_kvf, m_b_f, m_w_b_q, m_w_b_out, m_w_mlp_up, m_w_mlp_down, m_w_ple_gate, m_w_ple_up, m_final_norm, v_mix_norm, v_mlp_norm, v_ple_norm, v_w_a_in, v_a_lb_logits, v_a_head_gain, v_w_a_out, v_kv_norm, v_w_kvf, v_b_f, v_w_b_q, v_w_b_out, v_w_mlp_up, v_w_mlp_down, v_w_ple_gate, v_w_ple_up, v_final_norm):
    t, d = x.shape[1], x.shape[2]
    nh = d // HEAD_DIM
    n_layers = 2
    x2 = x.reshape(t, d)
    target = loss_target.reshape(t, d)
    me = _my_index()

    def gathered(name, shard):
        return all_gather("ag_" + name, shard.astype(BF16))

    def rows_joined(g):
        return g.reshape(1, g.shape[0] * g.shape[1], g.shape[2])

    wg_a_in = gathered("w_a_in", w_a_in[0])
    wg_a_out = rows_joined(gathered("w_a_out", w_a_out[0]))
    wg_kvf = gathered("w_kvf", w_kvf)
    wg_b_q = rows_joined(gathered("w_b_q", w_b_q[0]))
    wg_b_out = rows_joined(gathered("w_b_out", w_b_out[0]))
    wg_up = [gathered(f"w_mlp_up{l}", w_mlp_up[l]) for l in range(n_layers)]
    wg_down = [rows_joined(gathered(f"w_mlp_down{l}", w_mlp_down[l])) for l in range(n_layers)]
    wg_gate = [rows_joined(gathered(f"w_ple_gate{l}", w_ple_gate[l])) for l in range(n_layers)]
    wg_pup = [gathered(f"w_ple_up{l}", w_ple_up[l]) for l in range(n_layers)]
    lgt = all_gather("ag_lb_logits", a_lb_logits)
    lgt = lgt.transpose(1, 0, 2).reshape(2, d)
    p_bf = [p[l, 0].astype(BF16) for l in range(n_layers)]

    def row(vec):
        return vec.reshape(1, -1)

    def mlp_ple_fwd(l, h_in, a):
        (h_a, u_mlp), _ = rowwise(f"add_norm_mlp{l}", _add_norm_fwd, [h_in, a], [row(mlp_norm[l])])
        pre = mm_nn(f"mlp_up{l}", u_mlp, wg_up[l])
        (act,), _ = rowwise(f"relu2_{l}", _relu2_fwd, [pre])
        mo = mm_nn(f"mlp_down{l}", act, wg_down[l])
        (h_b, u_ple), _ = rowwise(f"add_norm_ple{l}", _add_norm_fwd, [h_a, mo], [row(ple_norm[l])])
        gpre = mm_nn(f"ple_gate{l}", u_ple, wg_gate[l])
        pu = mm_nn(f"ple_up{l}", p_bf[l], wg_pup[l])
        return dict(h_a=h_a, u_mlp=u_mlp, pre=pre, act=act, h_b=h_b, u_ple=u_ple, gpre=gpre, pu=pu)

    (u0,), _ = rowwise("norm_mix0", _norm_fwd, [x2], [row(mix_norm[0])])
    z = mm_nn("a_in", u0, wg_a_in)
    og, states = hgrn_fwd(z, lgt, a_head_gain)
    a0 = mm_nn("a_out", og, wg_a_out)
    s0 = mlp_ple_fwd(0, x2, a0)
    (h3, u_kv, u1), _ = rowwise("ple_norms", _ple_two_norms_fwd, [s0["h_b"], s0["gpre"], s0["pu"]],
                                [row(kv_norm), row(mix_norm[1])])
    hk = mm_nn("kvf", u_kv, wg_kvf, out3=True)
    hk = hk.transpose(1, 0, 2).reshape(t, -1)
    k_bf, v_bf = hk[:, :d].astype(BF16), hk[:, d:2 * d].astype(BF16)
    fl_t = hk[:, 2 * d:].T
    b_f_col = b_f.reshape(nh, 1)
    dcum = decay_fwd(fl_t, b_f_col)
    dcol, drow = dcum.reshape(nh, t, 1), dcum.reshape(nh, 1, t)
    q = mm_nn("b_q", u1, wg_b_q)
    o, lse = attn_fwd(q, k_bf, v_bf, dcol, drow)
    a1 = mm_nn("b_out", o, wg_b_out)
    s1 = mlp_ple_fwd(1, h3, a1)

    (dh, dgpre, dpu), (d_final, loss_rows) = rowwise(
        "tail", _tail_fwd_bwd, [s1["h_b"], s1["gpre"], s1["pu"], target], [row(final_norm)])
    loss = lax.psum(loss_rows[0, 0], MESH_AXES)

    grads = {}

    def mlp_ple_bwd(l, s, dh, dgpre, dpu):
        du = mm_nt(f"d_ple_gate_x{l}", dgpre, wg_gate[l])
        grads[f"w_ple_gate{l}"] = mm_tn(f"d_ple_gate_w{l}", s["u_ple"], dgpre, 1).reshape(N_DEV, -1, d)
        grads[f"w_ple_up{l}"] = mm_tn(f"d_ple_up_w{l}", p_bf[l], dpu, N_DEV)
        (dh, dh_bf), (d_ple,) = rowwise(f"d_norm_ple{l}", _norm_bwd, [s["h_b"], du, dh], [row(ple_norm[l])])
        dact = mm_nt(f"d_mlp_down_x{l}", dh_bf, wg_down[l])
        grads[f"w_mlp_down{l}"] = mm_tn(f"d_mlp_down_w{l}", s["act"], dh_bf, 1).reshape(N_DEV, -1, d)
        (dpre,), _ = rowwise(f"d_relu2_{l}", _relu2_bwd, [s["pre"], dact])
        du = mm_nt(f"d_mlp_up_x{l}", dpre, wg_up[l])
        grads[f"w_mlp_up{l}"] = mm_tn(f"d_mlp_up_w{l}", s["u_mlp"], dpre, N_DEV)
        (dh, dh_bf), (d_mlp,) = rowwise(f"d_norm_mlp{l}", _norm_bwd, [s["h_a"], du, dh], [row(mlp_norm[l])])
        return dh, dh_bf, d_ple, d_mlp

    dh, dh_bf, d_ple1, d_mlp1 = mlp_ple_bwd(1, s1, dh, dgpre, dpu)
    do = mm_nt("d_b_out_x", dh_bf, wg_b_out, out_dtype=BF16)
    grads["w_b_out"] = mm_tn("d_b_out_w", o, dh_bf, 1).reshape(N_DEV, -1, d)
    dq, dk, dv, ddrow = attn_bwd(q, k_bf, v_bf, dcol, drow, lse, do)
    du1 = mm_nt("d_b_q_x", dq, wg_b_q)
    grads["w_b_q"] = mm_tn("d_b_q_w", u1, dq, 1).reshape(N_DEV, -1, d)
    dfl_t, d_b_f = decay_bwd(fl_t, b_f_col, ddrow.reshape(nh, t))
    dhk = jnp.concatenate([dk, dv, dfl_t.T.astype(BF16)], axis=1)
    dhk = dhk.reshape(t, N_DEV, -1).transpose(1, 0, 2)
    du_kv = mm_nt("d_kvf_x", dhk, wg_kvf)
    grads["w_kvf"] = mm_tn("d_kvf_w", u_kv, dhk, N_DEV)
    (dh,), (d_kv_norm, d_mix1) = rowwise("d_ple_norms", _two_norms_bwd, [h3, du_kv, du1, dh],
                                         [row(kv_norm), row(mix_norm[1])])
    (dgpre, dpu), _ = rowwise("d_ple0", _ple_bwd, [s0["gpre"], s0["pu"], dh])
    dh, dh_bf, d_ple0, d_mlp0 = mlp_ple_bwd(0, s0, dh, dgpre, dpu)
    dog = mm_nt("d_a_out_x", dh_bf, wg_a_out)
    grads["w_a_out"] = mm_tn("d_a_out_w", og, dh_bf, 1).reshape(N_DEV, -1, d)
    dz4, d_lgt, d_hg = hgrn_bwd(z, lgt, a_head_gain, states, dog)
    dz = dz4.transpose(1, 0, 2).reshape(t, 4 * d)
    du0 = mm_nt("d_a_in_x", dz, wg_a_in)
    grads["w_a_in"] = mm_tn("d_a_in_w", u0, dz, N_DEV)
    (dx, _), (d_mix0,) = rowwise("d_norm_mix0", _norm_bwd, [x2, du0, dh], [row(mix_norm[0])])

    new = {}

    def update(name, parts, w, m, v):
        shp = w.shape
        w2, m2, v2 = (a.reshape(-1, shp[-1]) for a in (w, m, v))
        parts = parts.reshape(N_DEV, -1, shp[-1])
        new[name] = tuple(a.reshape(shp) for a in adamw_reduce("adamw_" + name, parts, w2, m2, v2))

    def exchange_update(name, g_layers, w, m, v):
        recv = [scatter_to_owners(f"rs_{name}{l}", g) for l, g in enumerate(g_layers)]
        parts = recv[0][:, None] if len(recv) == 1 else jnp.stack(recv, axis=1)
        update(name, parts, w, m, v)

    exchange_update("w_a_in", [grads["w_a_in"]], w_a_in, m_w_a_in, v_w_a_in)
    exchange_update("w_a_out", [grads["w_a_out"]], w_a_out, m_w_a_out, v_w_a_out)
    exchange_update("w_kvf", [grads["w_kvf"]], w_kvf, m_w_kvf, v_w_kvf)
    exchange_update("w_b_q", [grads["w_b_q"]], w_b_q, m_w_b_q, v_w_b_q)
    exchange_update("w_b_out", [grads["w_b_out"]], w_b_out, m_w_b_out, v_w_b_out)
    for name, w, m, v in (("w_mlp_up", w_mlp_up, m_w_mlp_up, v_w_mlp_up),
                          ("w_mlp_down", w_mlp_down, m_w_mlp_down, v_w_mlp_down),
                          ("w_ple_gate", w_ple_gate, m_w_ple_gate, v_w_ple_gate),
                          ("w_ple_up", w_ple_up, m_w_ple_up, v_w_ple_up)):
        exchange_update(name, [grads[f"{name}{l}"] for l in range(n_layers)], w, m, v)

    small = dict(mix_norm=jnp.concatenate([d_mix0, d_mix1]), mlp_norm=jnp.concatenate([d_mlp0, d_mlp1]),
                 ple_norm=jnp.concatenate([d_ple0, d_ple1]), a_head_gain=d_hg, kv_norm=d_kv_norm.reshape(d),
                 b_f=d_b_f.reshape(nh), final_norm=d_final.reshape(d))
    small_w = dict(mix_norm=(mix_norm, m_mix_norm, v_mix_norm), mlp_norm=(mlp_norm, m_mlp_norm, v_mlp_norm),
                   ple_norm=(ple_norm, m_ple_norm, v_ple_norm),
                   a_head_gain=(a_head_gain, m_a_head_gain, v_a_head_gain), kv_norm=(kv_norm, m_kv_norm, v_kv_norm),
                   b_f=(b_f, m_b_f, v_b_f), final_norm=(final_norm, m_final_norm, v_final_norm))
    names = list(small)
    packed = _pack_rows([d_lgt] + [small[n] for n in names])
    everyone = all_gather("ag_small_grads", packed)
    n_lgt_rows = d_lgt.size // 128
    lgt_parts = everyone[:, :n_lgt_rows].reshape(N_DEV, 2, d)
    lgt_parts = lax.dynamic_slice_in_dim(lgt_parts, me * a_lb_logits.shape[1], a_lb_logits.shape[1], axis=2)
    update("a_lb_logits", lgt_parts, a_lb_logits, m_a_lb_logits, v_a_lb_logits)
    rest = everyone[:, n_lgt_rows:]
    like = [small_w[n][0] for n in names]
    packed_w, packed_m, packed_v = (_pack_rows([small_w[n][j] for n in names], rest.shape[1]) for j in range(3))
    outs = adamw_reduce("adamw_small", rest, packed_w, packed_m, packed_v)
    unpacked = [_unpack_rows(a, like) for a in outs]
    for j, n in enumerate(names):
        new[n] = tuple(unpacked[q][j] for q in range(4))

    order = ["mix_norm", "mlp_norm", "ple_norm", "w_a_in", "a_lb_logits", "a_head_gain", "w_a_out", "kv_norm",
             "w_kvf", "b_f", "w_b_q", "w_b_out", "w_mlp_up", "w_mlp_down", "w_ple_gate", "w_ple_up", "final_norm"]
    result = [loss, dx.reshape(x.shape)]
    for j in range(4):
        result += [new[n][j] for n in order]
    return tuple(result)
```

```python
import functools

import jax
import jax.numpy as jnp
from jax import lax
from jax.experimental import pallas as pl
from jax.experimental.pallas import tpu as pltpu

F32 = jnp.float32
BF16 = jnp.bfloat16
HEAD_DIM = 128
CHUNK = 16
TILE = 128
NORM_EPS = 1e-6
N_DEV = 8
MESH_AXES = ("x", "y", "c")
VMEM_LIMIT_BYTES = 48 * 1024 * 1024
ROW_TILE_BYTES = 1024 * 1024
LR, B1, B2, ADAM_EPS, WD, STEP = 0.001, 0.9, 0.999, 1e-08, 0.01, 10
NEG_BIG = -1e30

NN = (((1,), (0,)), ((), ()))
NT = (((1,), (1,)), ((), ()))
TN = (((0,), (0,)), ((), ()))


def _params(semantics):
    return pltpu.CompilerParams(dimension_semantics=semantics, vmem_limit_bytes=VMEM_LIMIT_BYTES)


def _tile(n, prefs):
    for p in prefs:
        if n % p == 0:
            return p
    return n


def _row_tile(rows, limit):
    for cand in (2048, 1024, 512, 256, 128, 64, 32, 16):
        if cand <= limit and rows % cand == 0:
            return cand
    return rows


def _mm_call(name, a, b, dims, grid, a_spec, b_spec, o_spec, o_shape, acc_shape, k_axes, out_dtype, deps=()):
    nk = 1
    for ax in k_axes:
        nk *= grid[ax]

    def body(a_ref, b_ref, *rest):
        o_ref, acc_ref = rest[-2:]
        k = 0
        for ax in k_axes:
            k = k * grid[ax] + pl.program_id(ax)

        @pl.when(k == 0)
        def _():
            acc_ref[...] = jnp.zeros_like(acc_ref)

        acc_ref[...] += lax.dot_general(a_ref[...], b_ref[...], dims, preferred_element_type=F32)

        @pl.when(k == nk - 1)
        def _():
            o_ref[...] = acc_ref[...].astype(o_ref.dtype)

    sem = tuple("arbitrary" if ax in k_axes else "parallel" for ax in range(len(grid)))
    return pl.pallas_call(
        body, name=name, grid=grid, in_specs=[a_spec, b_spec] + [pl.BlockSpec(memory_space=pl.ANY)] * len(deps),
        out_specs=o_spec, out_shape=jax.ShapeDtypeStruct(o_shape, out_dtype),
        scratch_shapes=[pltpu.VMEM(acc_shape, F32)], compiler_params=_params(sem),
    )(a, b, *deps)


def mm_nn(name, a, b3, out_dtype=F32, out3=False, deps=()):
    m, k = a.shape
    g, _, n = b3.shape
    tm, tk = _tile(m, (1024, 512, 256)), _tile(k, (512, 256))
    tn = n if out3 else _tile(n, (1024, 512, 256, 128))
    nj = n // tn
    grid = (m // tm, g, nj, k // tk)
    a_spec = pl.BlockSpec((tm, tk), lambda i, gg, j, kk: (i, kk))
    b_spec = pl.BlockSpec((None, tk, tn), lambda i, gg, j, kk: (gg, kk, j))
    if out3:
        o_spec = pl.BlockSpec((None, tm, tn), lambda i, gg, j, kk: (gg, i, j))
        o_shape = (g, m, n)
    else:
        o_spec = pl.BlockSpec((tm, tn), lambda i, gg, j, kk: (i, gg * nj + j))
        o_shape = (m, g * n)
    return _mm_call(name, a, b3, NN, grid, a_spec, b_spec, o_spec, o_shape, (tm, tn), (3,), out_dtype, deps)


def mm_nt(name, a, b3, out_dtype=F32, deps=()):
    g, k, n = b3.shape
    a3 = a.ndim == 3
    m = a.shape[1] if a3 else a.shape[0]
    tm, tko = _tile(m, (1024, 512, 256)), _tile(k, (1024, 512, 256))
    tc = n if a3 else _tile(n, (512, 256, 128))
    nc = n // tc
    grid = (m // tm, k // tko, g, nc)
    if a3:
        a_spec = pl.BlockSpec((None, tm, tc), lambda i, j, gg, c: (gg, i, c))
    else:
        a_spec = pl.BlockSpec((tm, tc), lambda i, j, gg, c: (i, gg * nc + c))
    b_spec = pl.BlockSpec((None, tko, tc), lambda i, j, gg, c: (gg, j, c))
    o_spec = pl.BlockSpec((tm, tko), lambda i, j, gg, c: (i, j))
    return _mm_call(name, a, b3, NT, grid, a_spec, b_spec, o_spec, (m, k), (tm, tko), (2, 3), out_dtype, deps)


def mm_tn(name, a, b, g, out_dtype=BF16, deps=()):
    t, k = a.shape
    b3 = b.ndim == 3
    n = b.shape[2] if b3 else b.shape[1] // g
    tm = _tile(k, (1024, 512, 256))
    tn = n if b3 else _tile(n, (1024, 512, 256, 128))
    tt = _tile(t, (512, 256))
    nj = n // tn
    grid = (g, k // tm, nj, t // tt)
    a_spec = pl.BlockSpec((tt, tm), lambda gg, i, j, s: (s, i))
    if b3:
        b_spec = pl.BlockSpec((None, tt, tn), lambda gg, i, j, s: (gg, s, j))
    else:
        b_spec = pl.BlockSpec((tt, tn), lambda gg, i, j, s: (s, gg * nj + j))
    o_spec = pl.BlockSpec((None, tm, tn), lambda gg, i, j, s: (gg, i, j))
    return _mm_call(name, a, b, TN, grid, a_spec, b_spec, o_spec, (g, k, n), (tm, tn), (3,), out_dtype, deps)


def rowwise(name, fn, rows, vecs=()):
    t = rows[0].shape[0]
    wmax = max(r.shape[1] for r in rows)
    tr = _row_tile(t, ROW_TILE_BYTES // (4 * wmax))
    row_s = [jax.ShapeDtypeStruct((tr, r.shape[1]), r.dtype) for r in rows]
    vec_s = [jax.ShapeDtypeStruct(v.shape, v.dtype) for v in vecs]
    out_rows_s, out_sums_s = jax.eval_shape(fn, *row_s, *vec_s)
    n_in, n_r = len(rows) + len(vecs), len(out_rows_s)

    def body(*refs):
        i = pl.program_id(0)
        o_rows, o_sums = fn(*[r[...] for r in refs[:n_in]])
        for ref, val in zip(refs[n_in:n_in + n_r], o_rows):
            ref[...] = val

        if out_sums_s:
            @pl.when(i == 0)
            def _():
                for ref in refs[n_in + n_r:]:
                    ref[...] = jnp.zeros_like(ref)

            for ref, val in zip(refs[n_in + n_r:], o_sums):
                ref[...] += val

    in_specs = [pl.BlockSpec((tr, r.shape[1]), lambda i: (i, 0)) for r in rows]
    in_specs += [pl.BlockSpec(v.shape, lambda i: (0, 0)) for v in vecs]
    out_specs = [pl.BlockSpec((tr, s.shape[1]), lambda i: (i, 0)) for s in out_rows_s]
    out_specs += [pl.BlockSpec(s.shape, lambda i: (0, 0)) for s in out_sums_s]
    out_shape = [jax.ShapeDtypeStruct((t, s.shape[1]), s.dtype) for s in out_rows_s]
    out_shape += [jax.ShapeDtypeStruct(s.shape, s.dtype) for s in out_sums_s]
    outs = pl.pallas_call(
        body, name=name, grid=(t // tr,), in_specs=in_specs, out_specs=out_specs, out_shape=out_shape,
        compiler_params=_params(("arbitrary",)),
    )(*rows, *vecs)
    return outs[:n_r], outs[n_r:]


def _rms(x, gain):
    return x * lax.rsqrt(jnp.mean(x * x, axis=-1, keepdims=True) + NORM_EPS) * gain


def _norm_fwd(x, gain):
    return (_rms(x, gain).astype(BF16),), ()


def _add_norm_fwd(h, a, gain):
    h = h + a
    return (h, _rms(h, gain).astype(BF16)), ()


def _relu2_fwd(pre):
    r = jnp.maximum(pre, 0.0)
    return ((r * r).astype(BF16),), ()


def _ple(h, gpre, pu):
    return h + pu * jax.nn.sigmoid(gpre)


def _ple_two_norms_fwd(h, gpre, pu, gain_a, gain_b):
    h = _ple(h, gpre, pu)
    return (h, _rms(h, gain_a).astype(BF16), _rms(h, gain_b).astype(BF16)), ()


def _tail_fwd_bwd(h, gpre, pu, target, gain):
    def row_loss(h, gpre, pu, gain):
        y = _rms(_ple(h, gpre, pu), gain)
        return 0.5 * jnp.mean(jnp.square(y - target), axis=-1, keepdims=True)

    loss, vjp = jax.vjp(row_loss, h, gpre, pu, gain)
    dh, dgpre, dpu, dgain = vjp(jnp.ones_like(loss))
    loss = jnp.broadcast_to(jnp.sum(loss, axis=0, keepdims=True), (1, 128))
    return (dh, dgpre.astype(BF16), dpu.astype(BF16)), (dgain, loss)


def _ple_bwd(gpre, pu, dh):
    _, vjp = jax.vjp(lambda g, u: pu_times_gate(g, u), gpre, pu)
    dgpre, dpu = vjp(dh)
    return (dgpre.astype(BF16), dpu.astype(BF16)), ()


def pu_times_gate(gpre, pu):
    return pu * jax.nn.sigmoid(gpre)


def _norm_bwd(h, du, dh_in, gain):
    _, vjp = jax.vjp(_rms, h, gain)
    dh, dgain = vjp(du)
    dh = dh_in + dh
    return (dh, dh.astype(BF16)), (dgain,)


def _two_norms_bwd(h, du_a, du_b, dh_in, gain_a, gain_b):
    _, vjp = jax.vjp(lambda h, ga, gb: (_rms(h, ga), _rms(h, gb)), h, gain_a, gain_b)
    dh, dga, dgb = vjp((du_a, du_b))
    return (dh_in + dh,), (dga, dgb)


def _relu2_bwd(pre, dact):
    return ((dact * 2.0 * jnp.maximum(pre, 0.0)).astype(BF16),), ()


def _bf16_dot(dims_fwd, dims_da, dims_db, swap_da, swap_db):
    @jax.custom_vjp
    def dot(a, b):
        return lax.dot_general(a.astype(BF16), b.astype(BF16), dims_fwd, preferred_element_type=F32)

    def fwd(a, b):
        return dot(a, b), (a, b)

    def bwd(res, ct):
        a, b = res
        ct, a, b = ct.astype(BF16), a.astype(BF16), b.astype(BF16)
        da = lax.dot_general(*((b, ct) if swap_da else (ct, b)), dims_da, preferred_element_type=F32)
        db = lax.dot_general(*((ct, a) if swap_db else (a, ct)), dims_db, preferred_element_type=F32)
        return da, db

    dot.defvjp(fwd, bwd)
    return dot


_dot_nn = _bf16_dot(NN, NT, TN, False, False)
_dot_nt = _bf16_dot(NT, NN, TN, False, True)
_dot_tn = _bf16_dot(TN, NT, NN, True, False)


def _chunk_masks(transposed):
    r = lax.broadcasted_iota(jnp.int32, (TILE, TILE), 0)
    c = lax.broadcasted_iota(jnp.int32, (TILE, TILE), 1)
    same = (r // CHUNK) == (c // CHUNK)
    causal = same & ((r <= c) if transposed else (c <= r))
    return causal, same


@jax.custom_vjp
def _chunk_sums(x):
    causal, same = _chunk_masks(False)
    m = jnp.concatenate([causal.astype(F32), same.astype(F32)], axis=0)
    return jnp.dot(m, x, precision=lax.Precision.HIGHEST, preferred_element_type=F32)


def _chunk_sums_fwd(x):
    return _chunk_sums(x), None


def _chunk_sums_bwd(_, ct):
    causal_t, same = _chunk_masks(True)
    m_t = jnp.concatenate([causal_t.astype(F32), same.astype(F32)], axis=1)
    return (jnp.dot(m_t, ct, precision=lax.Precision.HIGHEST, preferred_element_type=F32),)


_chunk_sums.defvjp(_chunk_sums_fwd, _chunk_sums_bwd)


def _hgrn_tile(q, f, i, g, lgt, hg, st):
    d = q.shape[1]
    l0, l1 = lgt[0:1], lgt[1:2]
    mx = jnp.maximum(l0, l1)
    e0, e1 = jnp.exp(l0 - mx), jnp.exp(l1 - mx)
    lb = e0 / (e0 + e1)
    fg = lb + (1.0 - lb) * jax.nn.sigmoid(f)
    k = 1.0 - fg
    causal, _ = _chunk_masks(False)
    both = _chunk_sums(jnp.log(fg))
    b, b_last = both[:TILE], both[TILE:]
    q_in = q * jax.nn.sigmoid(q) * (d ** -0.5) * jnp.exp(b)
    k_in = k * jnp.exp(-b)
    k_end = k * jnp.exp(b_last - b)
    att = jnp.where(causal, _dot_nt(q_in, k_in), 0.0)
    o_intra = _dot_nn(att, i)
    outs = []
    for n in range(TILE // CHUNK):
        rows = slice(n * CHUNK, (n + 1) * CHUNK)
        outs.append(o_intra[rows] + _dot_nt(q_in[rows], st))
        decay = jnp.exp(jnp.mean(b_last[rows], axis=0, keepdims=True))
        st = st * decay + _dot_tn(i[rows], k_end[rows])
    o = jnp.concatenate(outs, axis=0)
    o = o * lax.rsqrt(jnp.mean(o * o, axis=-1, keepdims=True) + NORM_EPS) * hg
    return o * (g * jax.nn.sigmoid(g)), st


def hgrn_fwd(z, lgt, hg):
    t, d4 = z.shape
    d = d4 // 4
    nh, nt = d // HEAD_DIM, t // TILE

    def body(q_ref, f_ref, i_ref, g_ref, lgt_ref, hg_ref, o_ref, st_out_ref, st_ref):
        tt = pl.program_id(1)

        @pl.when(tt == 0)
        def _():
            st_ref[...] = jnp.zeros_like(st_ref)

        st = st_ref[...]
        st_out_ref[...] = st
        o, st = _hgrn_tile(q_ref[...], f_ref[...], i_ref[...], g_ref[...], lgt_ref[...], hg_ref[...], st)
        o_ref[...] = o.astype(o_ref.dtype)
        st_ref[...] = st

    def part(p):
        return pl.BlockSpec((TILE, HEAD_DIM), lambda h, tt: (tt, p * nh + h))

    return pl.pallas_call(
        body, name="hgrn_fwd", grid=(nh, nt),
        in_specs=[part(0), part(1), part(2), part(3),
                  pl.BlockSpec((2, HEAD_DIM), lambda h, tt: (0, h)),
                  pl.BlockSpec((1, HEAD_DIM), lambda h, tt: (0, 0))],
        out_specs=[pl.BlockSpec((TILE, HEAD_DIM), lambda h, tt: (tt, h)),
                   pl.BlockSpec((None, None, HEAD_DIM, HEAD_DIM), lambda h, tt: (h, tt, 0, 0))],
        out_shape=[jax.ShapeDtypeStruct((t, d), BF16),
                   jax.ShapeDtypeStruct((nh, nt, HEAD_DIM, HEAD_DIM), F32)],
        scratch_shapes=[pltpu.VMEM((HEAD_DIM, HEAD_DIM), F32)],
        compiler_params=_params(("parallel", "arbitrary")),
    )(z, z, z, z, lgt, hg)


def hgrn_bwd(z, lgt, hg, states, dout):
    t, d4 = z.shape
    d = d4 // 4
    nh, nt = d // HEAD_DIM, t // TILE

    def body(q_ref, f_ref, i_ref, g_ref, lgt_ref, hg_ref, st_in_ref, do_ref, dz_ref, dlgt_ref, dhg_ref, dst_ref):
        h, tt = pl.program_id(0), pl.program_id(1)

        @pl.when(tt == 0)
        def _():
            dst_ref[...] = jnp.zeros_like(dst_ref)
            dlgt_ref[...] = jnp.zeros_like(dlgt_ref)

        @pl.when((tt == 0) & (h == 0))
        def _():
            dhg_ref[...] = jnp.zeros_like(dhg_ref)

        _, vjp = jax.vjp(_hgrn_tile, q_ref[...], f_ref[...], i_ref[...], g_ref[...], lgt_ref[...], hg_ref[...],
                         st_in_ref[...])
        dq, df, di, dg, dlgt, dhg, dst = vjp((do_ref[...], dst_ref[...]))
        dz_ref[0] = dq.astype(dz_ref.dtype)
        dz_ref[1] = df.astype(dz_ref.dtype)
        dz_ref[2] = di.astype(dz_ref.dtype)
        dz_ref[3] = dg.astype(dz_ref.dtype)
        dlgt_ref[...] += dlgt
        dhg_ref[...] += dhg
        dst_ref[...] = dst

    def part(p):
        return pl.BlockSpec((TILE, HEAD_DIM), lambda h, tt: (nt - 1 - tt, p * nh + h))

    return pl.pallas_call(
        body, name="hgrn_bwd", grid=(nh, nt),
        in_specs=[part(0), part(1), part(2), part(3),
                  pl.BlockSpec((2, HEAD_DIM), lambda h, tt: (0, h)),
                  pl.BlockSpec((1, HEAD_DIM), lambda h, tt: (0, 0)),
                  pl.BlockSpec((None, None, HEAD_DIM, HEAD_DIM), lambda h, tt: (h, nt - 1 - tt, 0, 0)),
                  pl.BlockSpec((TILE, HEAD_DIM), lambda h, tt: (nt - 1 - tt, h))],
        out_specs=[pl.BlockSpec((4, TILE, HEAD_DIM), lambda h, tt: (0, nt - 1 - tt, h)),
                   pl.BlockSpec((2, HEAD_DIM), lambda h, tt: (0, h)),
                   pl.BlockSpec((1, HEAD_DIM), lambda h, tt: (0, 0))],
        out_shape=[jax.ShapeDtypeStruct((4, t, d), BF16),
                   jax.ShapeDtypeStruct((2, d), F32),
                   jax.ShapeDtypeStruct((1, HEAD_DIM), F32)],
        scratch_shapes=[pltpu.VMEM((HEAD_DIM, HEAD_DIM), F32)],
        compiler_params=_params(("arbitrary", "arbitrary")),
    )(z, z, z, z, lgt, hg, states, dout)


def _log_sigmoid(x):
    return jnp.minimum(x, 0.0) - jnp.log(1.0 + jnp.exp(-jnp.abs(x)))


def decay_fwd(fl_t, b_f):
    nh, t = fl_t.shape

    def body(fl_ref, b_ref, out_ref):
        r = lax.broadcasted_iota(jnp.int32, (128, 128), 0)
        c = lax.broadcasted_iota(jnp.int32, (128, 128), 1)
        upper = (r <= c).astype(F32)
        carry = jnp.zeros((nh, 1), F32)
        for j in range(t // 128):
            cols = slice(j * 128, (j + 1) * 128)
            ls = _log_sigmoid(fl_ref[:, cols] + b_ref[...])
            out_ref[:, cols] = carry + jnp.dot(ls, upper, precision=lax.Precision.HIGHEST,
                                               preferred_element_type=F32)
            carry = carry + jnp.sum(ls, axis=1, keepdims=True)

    return pl.pallas_call(body, name="decay_fwd", out_shape=jax.ShapeDtypeStruct((nh, t), F32),
                          compiler_params=_params(None))(fl_t, b_f)


def decay_bwd(fl_t, b_f, ddcum):
    nh, t = fl_t.shape

    def body(fl_ref, b_ref, dd_ref, dfl_ref, db_ref):
        r = lax.broadcasted_iota(jnp.int32, (128, 128), 0)
        c = lax.broadcasted_iota(jnp.int32, (128, 128), 1)
        lower = (r >= c).astype(F32)
        carry = jnp.zeros((nh, 1), F32)
        db = jnp.zeros((nh, 1), F32)
        for j in reversed(range(t // 128)):
            cols = slice(j * 128, (j + 1) * 128)
            dd = dd_ref[:, cols]
            dls = carry + jnp.dot(dd, lower, precision=lax.Precision.HIGHEST, preferred_element_type=F32)
            carry = carry + jnp.sum(dd, axis=1, keepdims=True)
            dfl = dls * jax.nn.sigmoid(-(fl_ref[:, cols] + b_ref[...]))
            dfl_ref[:, cols] = dfl
            db = db + jnp.sum(dfl, axis=1, keepdims=True)
        db_ref[...] = db

    return pl.pallas_call(body, name="decay_bwd",
                          out_shape=[jax.ShapeDtypeStruct((nh, t), F32), jax.ShapeDtypeStruct((nh, 1), F32)],
                          compiler_params=_params(None))(fl_t, b_f, ddcum)


def _attn_probs_logits(q_ref, k_ref, dcol_ref, drow_ref, i, tq, t):
    qs = (q_ref[...] * (HEAD_DIM ** -0.5)).astype(BF16)
    s = lax.dot_general(qs, k_ref[...], NT, preferred_element_type=F32)
    s = s + dcol_ref[...] - drow_ref[...]
    row = i * tq + lax.broadcasted_iota(jnp.int32, (tq, t), 0)
    col = lax.broadcasted_iota(jnp.int32, (tq, t), 1)
    mask = col <= row
    return qs, jnp.where(mask, s, NEG_BIG), mask


def attn_fwd(q, k, v, dcol, drow):
    t, d = q.shape
    nh = d // HEAD_DIM
    tq = _tile(t, (256, 128))

    def body(q_ref, k_ref, v_ref, dcol_ref, drow_ref, o_ref, lse_ref):
        i = pl.program_id(1)
        _, s, _ = _attn_probs_logits(q_ref, k_ref, dcol_ref, drow_ref, i, tq, t)
        m = jnp.max(s, axis=1, keepdims=True)
        p = jnp.exp(s - m)
        l = jnp.sum(p, axis=1, keepdims=True)
        o = jnp.dot(p.astype(BF16), v_ref[...], preferred_element_type=F32)
        o_ref[...] = (o / l).astype(o_ref.dtype)
        lse_ref[...] = m + jnp.log(l)

    return pl.pallas_call(
        body, name="attn_fwd", grid=(nh, t // tq),
        in_specs=[pl.BlockSpec((tq, HEAD_DIM), lambda h, i: (i, h)),
                  pl.BlockSpec((t, HEAD_DIM), lambda h, i: (0, h)),
                  pl.BlockSpec((t, HEAD_DIM), lambda h, i: (0, h)),
                  pl.BlockSpec((None, tq, 1), lambda h, i: (h, i, 0)),
                  pl.BlockSpec((None, 1, t), lambda h, i: (h, 0, 0))],
        out_specs=[pl.BlockSpec((tq, HEAD_DIM), lambda h, i: (i, h)),
                   pl.BlockSpec((None, tq, 1), lambda h, i: (h, i, 0))],
        out_shape=[jax.ShapeDtypeStruct((t, d), BF16), jax.ShapeDtypeStruct((nh, t, 1), F32)],
        compiler_params=_params(("parallel", "parallel")),
    )(q, k, v, dcol, drow)


def attn_bwd(q, k, v, dcol, drow, lse, do):
    t, d = q.shape
    nh = d // HEAD_DIM
    tq = _tile(t, (256, 128))
    nq = t // tq

    def body(q_ref, k_ref, v_ref, dcol_ref, drow_ref, lse_ref, do_ref,
             dq_ref, dk_ref, dv_ref, ddrow_ref, dk_acc, dv_acc):
        i = pl.program_id(1)

        @pl.when(i == 0)
        def _():
            dk_acc[...] = jnp.zeros_like(dk_acc)
            dv_acc[...] = jnp.zeros_like(dv_acc)
            ddrow_ref[...] = jnp.zeros_like(ddrow_ref)

        qs, s, mask = _attn_probs_logits(q_ref, k_ref, dcol_ref, drow_ref, i, tq, t)
        p = jnp.where(mask, jnp.exp(s - lse_ref[...]), 0.0)
        do = do_ref[...]
        dp = lax.dot_general(do, v_ref[...], NT, preferred_element_type=F32)
        ds = p * (dp - jnp.sum(p * dp, axis=1, keepdims=True))
        dsb = ds.astype(BF16)
        dq = jnp.dot(dsb, k_ref[...], preferred_element_type=F32) * (HEAD_DIM ** -0.5)
        dq_ref[...] = dq.astype(dq_ref.dtype)
        dk_acc[...] += lax.dot_general(dsb, qs, TN, preferred_element_type=F32)
        dv_acc[...] += lax.dot_general(p.astype(BF16), do, TN, preferred_element_type=F32)
        ddrow_ref[...] -= jnp.sum(ds, axis=0, keepdims=True)

        @pl.when(i == nq - 1)
        def _():
            dk_ref[...] = dk_acc[...].astype(dk_ref.dtype)
            dv_ref[...] = dv_acc[...].astype(dv_ref.dtype)

    tile = pl.BlockSpec((tq, HEAD_DIM), lambda h, i: (i, h))
    full = pl.BlockSpec((t, HEAD_DIM), lambda h, i: (0, h))
    col = pl.BlockSpec((None, tq, 1), lambda h, i: (h, i, 0))
    rowv = pl.BlockSpec((None, 1, t), lambda h, i: (h, 0, 0))
    return pl.pallas_call(
        body, name="attn_bwd", grid=(nh, nq),
        in_specs=[tile, full, full, col, rowv, col, tile],
        out_specs=[tile, full, full, rowv],
        out_shape=[jax.ShapeDtypeStruct((t, d), BF16), jax.ShapeDtypeStruct((t, d), BF16),
                   jax.ShapeDtypeStruct((t, d), BF16), jax.ShapeDtypeStruct((nh, 1, t), F32)],
        scratch_shapes=[pltpu.VMEM((t, HEAD_DIM), F32), pltpu.VMEM((t, HEAD_DIM), F32)],
        compiler_params=_params(("parallel", "arbitrary")),
    )(q, k, v, dcol, drow, lse, do)


def _my_index():
    return (lax.axis_index("x") * 2 + lax.axis_index("y")) * 2 + lax.axis_index("c")


def _exchange(name, src, gather):
    shape = src.shape if gather else src.shape[1:]

    def body(src_ref, out_ref, send_sems, recv_sems, local_sem):
        x, y, c = (lax.axis_index(a) for a in MESH_AXES)
        me = (x * 2 + y) * 2 + c
        mine = src_ref if gather else src_ref.at[me]
        local = pltpu.make_async_copy(mine, out_ref.at[me], local_sem)
        local.start()
        copies = []
        for dlt in range(1, N_DEV):
            dx, dy, dc = dlt // 4, (dlt // 2) % 2, dlt % 2
            px, py, pc = x ^ dx, y ^ dy, c ^ dc
            peer = (px * 2 + py) * 2 + pc
            copies.append(pltpu.make_async_remote_copy(
                src_ref=src_ref if gather else src_ref.at[peer], dst_ref=out_ref.at[me],
                send_sem=send_sems.at[dlt - 1], recv_sem=recv_sems.at[dlt - 1],
                device_id=(px, py, pc), device_id_type=pl.DeviceIdType.MESH))
        for cp in copies:
            cp.start()
        for cp in copies:
            cp.wait_recv()
        for cp in copies:
            cp.wait_send()
        local.wait()

    return pl.pallas_call(
        body, name=name, out_shape=jax.ShapeDtypeStruct((N_DEV,) + tuple(shape), src.dtype),
        in_specs=[pl.BlockSpec(memory_space=pl.ANY)], out_specs=pl.BlockSpec(memory_space=pl.ANY),
        scratch_shapes=[pltpu.SemaphoreType.DMA((N_DEV - 1,)), pltpu.SemaphoreType.DMA((N_DEV - 1,)),
                        pltpu.SemaphoreType.DMA],
        compiler_params=pltpu.CompilerParams(has_side_effects=True),
    )(src)


def all_gather(name, x):
    return _exchange(name, x, True)


_HBM = pl.BlockSpec(memory_space=pltpu.HBM)
_SEM = pl.BlockSpec(memory_space=pltpu.SEMAPHORE)
_DATAFLOW = pltpu.SideEffectType.DATAFLOW_SIDE_EFFECTING


def _peer_copies(src_ref, land_ref, send_sems, recv_sems, gather):
    x, y, c = (lax.axis_index(a) for a in MESH_AXES)
    me = (x * 2 + y) * 2 + c
    copies = []
    for dlt in range(1, N_DEV):
        px, py, pc = x ^ (dlt // 4), y ^ ((dlt // 2) % 2), c ^ (dlt % 2)
        peer = (px * 2 + py) * 2 + pc
        copies.append(pltpu.make_async_remote_copy(
            src_ref=src_ref if gather else src_ref.at[peer], dst_ref=land_ref.at[me],
            send_sem=send_sems.at[dlt - 1], recv_sem=recv_sems.at[dlt - 1],
            device_id=(px, py, pc), device_id_type=pl.DeviceIdType.MESH))
    return copies


def exchange_start(name, srcs, gather):
    n = len(srcs)
    lands = [lax.empty((N_DEV,) + tuple(s.shape if gather else s.shape[1:]), s.dtype) for s in srcs]

    def body(*refs):
        src_refs, land_refs = refs[:n], refs[n:2 * n]
        send_sems, recv_sems = refs[2 * n:3 * n], refs[3 * n:4 * n]
        token = refs[-1]
        for j in range(n):
            for cp in _peer_copies(src_refs[j], land_refs[j], send_sems[j], recv_sems[j], gather):
                cp.start()
        token[...] = jnp.zeros_like(token)

    sems = [pltpu.SemaphoreType.DMA((N_DEV - 1,))] * (2 * n)
    thru = [pltpu.HBM(a.shape, a.dtype) for a in list(srcs) + lands]
    outs = pl.pallas_call(
        body, name=name, out_shape=tuple(sems + thru + [jax.ShapeDtypeStruct((8, 128), F32)]),
        in_specs=[_HBM] * (2 * n), out_specs=tuple([_SEM] * (2 * n) + [_HBM] * (2 * n) + [pl.BlockSpec(memory_space=pltpu.VMEM)]),
        input_output_aliases={j: 2 * n + j for j in range(2 * n)},
        compiler_params=pltpu.CompilerParams(has_side_effects=_DATAFLOW),
    )(*[pltpu.with_memory_space_constraint(a, pltpu.HBM) for a in list(srcs) + lands])
    handles = [(outs[j], outs[n + j], outs[2 * n + j], outs[3 * n + j]) for j in range(n)]
    return handles, outs[-1]


def exchange_wait(name, handle, after, gather):
    send_sems, recv_sems, src, land = handle

    def body(src_ref, land_ref, send_ref, recv_ref, after_ref, src_out, land_out):
        for cp in _peer_copies(src_ref, land_ref, send_ref, recv_ref, gather):
            cp.wait_send()
            cp.wait_recv()

    return pl.pallas_call(
        body, name=name, out_shape=(pltpu.HBM(src.shape, src.dtype), pltpu.HBM(land.shape, land.dtype)),
        in_specs=[_HBM, _HBM, _SEM, _SEM, pl.BlockSpec(memory_space=pl.ANY)], out_specs=(_HBM, _HBM),
        input_output_aliases={0: 0, 1: 1},
        compiler_params=pltpu.CompilerParams(has_side_effects=_DATAFLOW),
    )(src, land, send_sems, recv_sems, after)[1]


def adamw_reduce(name, parts, w, m, v):
    nl, r, wd = w.shape
    tr = _row_tile(r, 2 * ROW_TILE_BYTES // (8 * wd))

    def body(*refs):
        p_refs = refs[:nl]
        w_ref, m_ref, v_ref, g_ref, d_ref, nm_ref, nv_ref = refs[nl:]
        layer = pl.program_id(0)
        for j in range(nl):
            @pl.when(layer == j)
            def _(j=j):
                g = p_refs[j][0].astype(F32)
                for dev in range(1, N_DEV):
                    g = g + p_refs[j][dev].astype(F32)
                nm = B1 * m_ref[...] + (1.0 - B1) * g
                nv = B2 * v_ref[...] + (1.0 - B2) * jnp.square(g)
                m_hat = nm / (1.0 - B1 ** STEP)
                v_hat = nv / (1.0 - B2 ** STEP)
                g_ref[...] = g
                d_ref[...] = -LR * (m_hat / (jnp.sqrt(v_hat) + ADAM_EPS) + WD * w_ref[...])
                nm_ref[...] = nm
                nv_ref[...] = nv

    def part_spec(j):
        return pl.BlockSpec((N_DEV, tr, wd), lambda l, i: (0, jnp.where(l == j, i, 0), 0))

    spec = pl.BlockSpec((None, tr, wd), lambda l, i: (l, i, 0))
    return pl.pallas_call(
        body, name=name, grid=(nl, r // tr),
        in_specs=[part_spec(j) for j in range(nl)] + [spec, spec, spec],
        out_specs=[spec] * 4, out_shape=[jax.ShapeDtypeStruct((nl, r, wd), F32)] * 4,
        compiler_params=_params(("arbitrary", "arbitrary")),
    )(*parts, w, m, v)


def _pack_rows(vectors, rows=None):
    flat = jnp.concatenate([a.reshape(-1).astype(F32) for a in vectors])
    n = flat.shape[0]
    if rows is None:
        rows = -(-n // 1024) * 8
    return jnp.pad(flat, (0, rows * 128 - n)).reshape(rows, 128)


def _unpack_rows(packed, like):
    flat = packed.reshape(-1)
    out, pos = [], 0
    for a in like:
        out.append(flat[pos:pos + a.size].reshape(a.shape))
        pos += a.size
    return out


def kernel(x, p, mix_norm, mlp_norm, ple_norm, w_a_in, a_lb_logits, a_head_gain, w_a_out, kv_norm, w_kvf, b_f, w_b_q, w_b_out, w_mlp_up, w_mlp_down, w_ple_gate, w_ple_up, final_norm, loss_target, m_mix_norm, m_mlp_norm, m_ple_norm, m_w_a_in, m_a_lb_logits, m_a_head_gain, m_w_a_out, m_kv_norm, m_w_kvf, m_b_f, m_w_b_q, m_w_b_out, m_w_mlp_up, m_w_mlp_down, m_w_ple_gate, m_w_ple_up, m_final_norm, v_mix_norm, v_mlp_norm, v_ple_norm, v_w_a_in, v_a_lb_logits, v_a_head_gain, v_w_a_out, v_kv_norm, v_w_kvf, v_b_f, v_w_b_q, v_w_b_out, v_w_mlp_up, v_w_mlp_down, v_w_ple_gate, v_w_ple_up, v_final_norm):
    t, d = x.shape[1], x.shape[2]
    nh = d // HEAD_DIM
    n_layers = 2
    x2 = x.reshape(t, d)
    target = loss_target.reshape(t, d)
    me = _my_index()

    shards = {"w_a_in": w_a_in[0], "w_a_out": w_a_out[0], "w_kvf": w_kvf, "w_b_q": w_b_q[0], "w_b_out": w_b_out[0]}
    for l in range(n_layers):
        shards.update({f"w_mlp_up{l}": w_mlp_up[l], f"w_mlp_down{l}": w_mlp_down[l],
                       f"w_ple_gate{l}": w_ple_gate[l], f"w_ple_up{l}": w_ple_up[l]})
    first_use = ["w_a_in", "w_a_out", "w_mlp_up0", "w_mlp_down0", "w_ple_gate0", "w_ple_up0", "w_kvf", "w_b_q",
                 "w_b_out", "w_mlp_up1", "w_mlp_down1", "w_ple_gate1", "w_ple_up1"]
    row_sharded = ("w_a_out", "w_b_q", "w_b_out", "w_mlp_down", "w_ple_gate")
    shards_bf = [shards[n].astype(BF16) for n in first_use]
    ag_handles, _ = exchange_start("ag_start", shards_bf, True)
    weights = {}

    def weight(name, after=None):
        if name not in weights:
            j = first_use.index(name)
            land = exchange_wait("ag_wait_" + name, ag_handles[j], after, True)
            g = lax.dynamic_update_slice(land, shards_bf[j][None], (me, 0, 0))
            if name.rstrip("01") in row_sharded:
                g = g.reshape(1, g.shape[0] * g.shape[1], g.shape[2])
            weights[name] = g
        return weights[name]

    lgt = all_gather("ag_lb_logits", a_lb_logits)
    lgt = lgt.transpose(1, 0, 2).reshape(2, d)
    p_bf = [p[l, 0].astype(BF16) for l in range(n_layers)]

    def row(vec):
        return vec.reshape(1, -1)

    def mlp_ple_fwd(l, h_in, a):
        (h_a, u_mlp), _ = rowwise(f"add_norm_mlp{l}", _add_norm_fwd, [h_in, a], [row(mlp_norm[l])])
        pre = mm_nn(f"mlp_up{l}", u_mlp, weight(f"w_mlp_up{l}", u_mlp))
        (act,), _ = rowwise(f"relu2_{l}", _relu2_fwd, [pre])
        mo = mm_nn(f"mlp_down{l}", act, weight(f"w_mlp_down{l}", act))
        (h_b, u_ple), _ = rowwise(f"add_norm_ple{l}", _add_norm_fwd, [h_a, mo], [row(ple_norm[l])])
        gpre = mm_nn(f"ple_gate{l}", u_ple, weight(f"w_ple_gate{l}", u_ple))
        pu = mm_nn(f"ple_up{l}", p_bf[l], weight(f"w_ple_up{l}", gpre))
        return dict(h_a=h_a, u_mlp=u_mlp, pre=pre, act=act, h_b=h_b, u_ple=u_ple, gpre=gpre, pu=pu)

    (u0,), _ = rowwise("norm_mix0", _norm_fwd, [x2], [row(mix_norm[0])])
    z = mm_nn("a_in", u0, weight("w_a_in", u0))
    og, states = hgrn_fwd(z, lgt, a_head_gain)
    a0 = mm_nn("a_out", og, weight("w_a_out", og))
    s0 = mlp_ple_fwd(0, x2, a0)
    (h3, u_kv, u1), _ = rowwise("ple_norms", _ple_two_norms_fwd, [s0["h_b"], s0["gpre"], s0["pu"]],
                                [row(kv_norm), row(mix_norm[1])])
    hk = mm_nn("kvf", u_kv, weight("w_kvf", u_kv), out3=True)
    hk = hk.transpose(1, 0, 2).reshape(t, -1)
    k_bf, v_bf = hk[:, :d].astype(BF16), hk[:, d:2 * d].astype(BF16)
    fl_t = hk[:, 2 * d:].T
    b_f_col = b_f.reshape(nh, 1)
    dcum = decay_fwd(fl_t, b_f_col)
    dcol, drow = dcum.reshape(nh, t, 1), dcum.reshape(nh, 1, t)
    q = mm_nn("b_q", u1, weight("w_b_q", dcum))
    o, lse = attn_fwd(q, k_bf, v_bf, dcol, drow)
    a1 = mm_nn("b_out", o, weight("w_b_out", o))
    s1 = mlp_ple_fwd(1, h3, a1)

    (dh, dgpre, dpu), (d_final, loss_rows) = rowwise(
        "tail", _tail_fwd_bwd, [s1["h_b"], s1["gpre"], s1["pu"], target], [row(final_norm)])
    loss = lax.psum(loss_rows[0, 0], MESH_AXES)

    sent = {}
    tokens = []

    def send_grad(name, g):
        g = g.reshape(N_DEV, -1, g.shape[-1])
        (handle,), token = exchange_start("rs_start_" + name, [g], False)
        sent[name] = (handle, g)
        tokens.append(token)

    def after_sends():
        deps = tuple(tokens)
        tokens.clear()
        return deps

    def mlp_ple_bwd(l, s, dh, dgpre, dpu):
        du = mm_nt(f"d_ple_gate_x{l}", dgpre, weight(f"w_ple_gate{l}"), deps=after_sends())
        send_grad(f"w_ple_gate{l}", mm_tn(f"d_ple_gate_w{l}", s["u_ple"], dgpre, 1))
        send_grad(f"w_ple_up{l}", mm_tn(f"d_ple_up_w{l}", p_bf[l], dpu, N_DEV, deps=after_sends()))
        (dh, dh_bf), (d_ple,) = rowwise(f"d_norm_ple{l}", _norm_bwd, [s["h_b"], du, dh], [row(ple_norm[l])])
        dact = mm_nt(f"d_mlp_down_x{l}", dh_bf, weight(f"w_mlp_down{l}"), deps=after_sends())
        send_grad(f"w_mlp_down{l}", mm_tn(f"d_mlp_down_w{l}", s["act"], dh_bf, 1))
        (dpre,), _ = rowwise(f"d_relu2_{l}", _relu2_bwd, [s["pre"], dact])
        du = mm_nt(f"d_mlp_up_x{l}", dpre, weight(f"w_mlp_up{l}"), deps=after_sends())
        send_grad(f"w_mlp_up{l}", mm_tn(f"d_mlp_up_w{l}", s["u_mlp"], dpre, N_DEV))
        (dh, dh_bf), (d_mlp,) = rowwise(f"d_norm_mlp{l}", _norm_bwd, [s["h_a"], du, dh], [row(mlp_norm[l])])
        return dh, dh_bf, d_ple, d_mlp

    dh, dh_bf, d_ple1, d_mlp1 = mlp_ple_bwd(1, s1, dh, dgpre, dpu)
    do = mm_nt("d_b_out_x", dh_bf, weight("w_b_out"), out_dtype=BF16, deps=after_sends())
    send_grad("w_b_out", mm_tn("d_b_out_w", o, dh_bf, 1))
    dq, dk, dv, ddrow = attn_bwd(q, k_bf, v_bf, dcol, drow, lse, do)
    du1 = mm_nt("d_b_q_x", dq, weight("w_b_q"), deps=after_sends())
    send_grad("w_b_q", mm_tn("d_b_q_w", u1, dq, 1))
    dfl_t, d_b_f = decay_bwd(fl_t, b_f_col, ddrow.reshape(nh, t))
    dhk = jnp.concatenate([dk, dv, dfl_t.T.astype(BF16)], axis=1)
    dhk = dhk.reshape(t, N_DEV, -1).transpose(1, 0, 2)
    du_kv = mm_nt("d_kvf_x", dhk, weight("w_kvf"), deps=after_sends())
    send_grad("w_kvf", mm_tn("d_kvf_w", u_kv, dhk, N_DEV))
    (dh,), (d_kv_norm, d_mix1) = rowwise("d_ple_norms", _two_norms_bwd, [h3, du_kv, du1, dh],
                                         [row(kv_norm), row(mix_norm[1])])
    (dgpre, dpu), _ = rowwise("d_ple0", _ple_bwd, [s0["gpre"], s0["pu"], dh])
    dh, dh_bf, d_ple0, d_mlp0 = mlp_ple_bwd(0, s0, dh, dgpre, dpu)
    dog = mm_nt("d_a_out_x", dh_bf, weight("w_a_out"), deps=after_sends())
    send_grad("w_a_out", mm_tn("d_a_out_w", og, dh_bf, 1))
    dz4, d_lgt, d_hg = hgrn_bwd(z, lgt, a_head_gain, states, dog)
    dz = dz4.transpose(1, 0, 2).reshape(t, 4 * d)
    du0 = mm_nt("d_a_in_x", dz, weight("w_a_in"), deps=after_sends())
    send_grad("w_a_in", mm_tn("d_a_in_w", u0, dz, N_DEV))
    (dx, _), (d_mix0,) = rowwise("d_norm_mix0", _norm_bwd, [x2, du0, dh], [row(mix_norm[0])])

    new = {}
    last = [dx]

    def update(name, parts, w, m, v):
        shp = w.shape
        w3, m3, v3 = (a.reshape(len(parts), -1, shp[-1]) for a in (w, m, v))
        new[name] = tuple(a.reshape(shp) for a in adamw_reduce("adamw_" + name, parts, w3, m3, v3))
        last[0] = new[name][0]

    def receive_update(name, layers, w, m, v):
        parts = {}
        for sfx in layers:
            handle, g = sent[name + sfx]
            land = exchange_wait(f"rs_wait_{name}{sfx}", handle, last[0], False)
            parts[sfx] = lax.dynamic_update_slice(land, lax.dynamic_slice_in_dim(g, me, 1, 0), (me, 0, 0))
        update(name, [parts[sfx] for sfx in sorted(layers)], w, m, v)

    both = ("1", "0")
    receive_update("w_b_out", ("",), w_b_out, m_w_b_out, v_w_b_out)
    receive_update("w_b_q", ("",), w_b_q, m_w_b_q, v_w_b_q)
    receive_update("w_kvf", ("",), w_kvf, m_w_kvf, v_w_kvf)
    receive_update("w_ple_gate", both, w_ple_gate, m_w_ple_gate, v_w_ple_gate)
    receive_update("w_ple_up", both, w_ple_up, m_w_ple_up, v_w_ple_up)
    receive_update("w_mlp_down", both, w_mlp_down, m_w_mlp_down, v_w_mlp_down)
    receive_update("w_mlp_up", both, w_mlp_up, m_w_mlp_up, v_w_mlp_up)
    receive_update("w_a_out", ("",), w_a_out, m_w_a_out, v_w_a_out)
    receive_update("w_a_in", ("",), w_a_in, m_w_a_in, v_w_a_in)

    small = dict(mix_norm=jnp.concatenate([d_mix0, d_mix1]), mlp_norm=jnp.concatenate([d_mlp0, d_mlp1]),
                 ple_norm=jnp.concatenate([d_ple0, d_ple1]), a_head_gain=d_hg, kv_norm=d_kv_norm.reshape(d),
                 b_f=d_b_f.reshape(nh), final_norm=d_final.reshape(d))
    small_w = dict(mix_norm=(mix_norm, m_mix_norm, v_mix_norm), mlp_norm=(mlp_norm, m_mlp_norm, v_mlp_norm),
                   ple_norm=(ple_norm, m_ple_norm, v_ple_norm),
                   a_head_gain=(a_head_gain, m_a_head_gain, v_a_head_gain), kv_norm=(kv_norm, m_kv_norm, v_kv_norm),
                   b_f=(b_f, m_b_f, v_b_f), final_norm=(final_norm, m_final_norm, v_final_norm))
    names = list(small)
    packed = _pack_rows([d_lgt] + [small[n] for n in names])
    everyone = all_gather("ag_small_grads", packed)
    n_lgt_rows = d_lgt.size // 128
    lgt_parts = everyone[:, :n_lgt_rows].reshape(N_DEV, 2, d)
    lgt_parts = lax.dynamic_slice_in_dim(lgt_parts, me * a_lb_logits.shape[1], a_lb_logits.shape[1], axis=2)
    update("a_lb_logits", [lgt_parts], a_lb_logits, m_a_lb_logits, v_a_lb_logits)
    rest = everyone[:, n_lgt_rows:]
    like = [small_w[n][0] for n in names]
    packed_w, packed_m, packed_v = (_pack_rows([small_w[n][j] for n in names], rest.shape[1])[None] for j in range(3))
    outs = adamw_reduce("adamw_small", [rest], packed_w, packed_m, packed_v)
    unpacked = [_unpack_rows(a, like) for a in outs]
    for j, n in enumerate(names):
        new[n] = tuple(unpacked[q][j] for q in range(4))

    order = ["mix_norm", "mlp_norm", "ple_norm", "w_a_in", "a_lb_logits", "a_head_gain", "w_a_out", "kv_norm",
             "w_kvf", "b_f", "w_b_q", "w_b_out", "w_mlp_up", "w_mlp_down", "w_ple_gate", "w_ple_up", "final_norm"]
    result = [loss, dx.reshape(x.shape)]
    for j in range(4):
        result += [new[n][j] for n in order]
    return tuple(result)
```

```python
import functools

import jax
import jax.numpy as jnp
from jax import lax
from jax.experimental import pallas as pl
from jax.experimental.pallas import tpu as pltpu

F32 = jnp.float32
BF16 = jnp.bfloat16
HEAD_DIM = 128
CHUNK = 16
TILE = 128
NORM_EPS = 1e-6
N_DEV = 8
MESH_AXES = ("x", "y", "c")
VMEM_LIMIT_BYTES = 48 * 1024 * 1024
ROW_TILE_BYTES = 1024 * 1024
LR, B1, B2, ADAM_EPS, WD, STEP = 0.001, 0.9, 0.999, 1e-08, 0.01, 10
NEG_BIG = -1e30

NN = (((1,), (0,)), ((), ()))
NT = (((1,), (1,)), ((), ()))
TN = (((0,), (0,)), ((), ()))


def _params(semantics):
    return pltpu.CompilerParams(dimension_semantics=semantics, vmem_limit_bytes=VMEM_LIMIT_BYTES)


def _tile(n, prefs):
    for p in prefs:
        if n % p == 0:
            return p
    return n


def _row_tile(rows, limit):
    for cand in (2048, 1024, 512, 256, 128, 64, 32, 16):
        if cand <= limit and rows % cand == 0:
            return cand
    return rows


def _mm_call(name, a, b, dims, grid, a_spec, b_spec, o_spec, o_shape, acc_shape, k_axes, out_dtype, deps=()):
    nk = 1
    for ax in k_axes:
        nk *= grid[ax]

    def body(a_ref, b_ref, *rest):
        o_ref, acc_ref = rest[-2:]
        k = 0
        for ax in k_axes:
            k = k * grid[ax] + pl.program_id(ax)

        @pl.when(k == 0)
        def _():
            acc_ref[...] = jnp.zeros_like(acc_ref)

        acc_ref[...] += lax.dot_general(a_ref[...], b_ref[...], dims, preferred_element_type=F32)

        @pl.when(k == nk - 1)
        def _():
            o_ref[...] = acc_ref[...].astype(o_ref.dtype)

    sem = tuple("arbitrary" if ax in k_axes else "parallel" for ax in range(len(grid)))
    return pl.pallas_call(
        body, name=name, grid=grid, in_specs=[a_spec, b_spec] + [pl.BlockSpec(memory_space=pl.ANY)] * len(deps),
        out_specs=o_spec, out_shape=jax.ShapeDtypeStruct(o_shape, out_dtype),
        scratch_shapes=[pltpu.VMEM(acc_shape, F32)], compiler_params=_params(sem),
    )(a, b, *deps)


def mm_nn(name, a, b3, out_dtype=F32, out3=False, deps=()):
    m, k = a.shape
    g, _, n = b3.shape
    tm, tk = _tile(m, (1024, 512, 256)), _tile(k, (512, 256))
    tn = n if out3 else _tile(n, (1024, 512, 256, 128))
    nj = n // tn
    grid = (m // tm, g, nj, k // tk)
    a_spec = pl.BlockSpec((tm, tk), lambda i, gg, j, kk: (i, kk))
    b_spec = pl.BlockSpec((None, tk, tn), lambda i, gg, j, kk: (gg, kk, j))
    if out3:
        o_spec = pl.BlockSpec((None, tm, tn), lambda i, gg, j, kk: (gg, i, j))
        o_shape = (g, m, n)
    else:
        o_spec = pl.BlockSpec((tm, tn), lambda i, gg, j, kk: (i, gg * nj + j))
        o_shape = (m, g * n)
    return _mm_call(name, a, b3, NN, grid, a_spec, b_spec, o_spec, o_shape, (tm, tn), (3,), out_dtype, deps)


def mm_nt(name, a, b3, out_dtype=F32, deps=()):
    g, k, n = b3.shape
    a3 = a.ndim == 3
    m = a.shape[1] if a3 else a.shape[0]
    tm, tko = _tile(m, (1024, 512, 256)), _tile(k, (1024, 512, 256))
    tc = n if a3 else _tile(n, (512, 256, 128))
    nc = n // tc
    grid = (m // tm, k // tko, g, nc)
    if a3:
        a_spec = pl.BlockSpec((None, tm, tc), lambda i, j, gg, c: (gg, i, c))
    else:
        a_spec = pl.BlockSpec((tm, tc), lambda i, j, gg, c: (i, gg * nc + c))
    b_spec = pl.BlockSpec((None, tko, tc), lambda i, j, gg, c: (gg, j, c))
    o_spec = pl.BlockSpec((tm, tko), lambda i, j, gg, c: (i, j))
    return _mm_call(name, a, b3, NT, grid, a_spec, b_spec, o_spec, (m, k), (tm, tko), (2, 3), out_dtype, deps)


def mm_tn(name, a, b, g, out_dtype=BF16, deps=()):
    t, k = a.shape
    b3 = b.ndim == 3
    n = b.shape[2] if b3 else b.shape[1] // g
    tm = _tile(k, (1024, 512, 256))
    tn = n if b3 else _tile(n, (1024, 512, 256, 128))
    tt = _tile(t, (512, 256))
    nj = n // tn
    grid = (g, k // tm, nj, t // tt)
    a_spec = pl.BlockSpec((tt, tm), lambda gg, i, j, s: (s, i))
    if b3:
        b_spec = pl.BlockSpec((None, tt, tn), lambda gg, i, j, s: (gg, s, j))
    else:
        b_spec = pl.BlockSpec((tt, tn), lambda gg, i, j, s: (s, gg * nj + j))
    o_spec = pl.BlockSpec((None, tm, tn), lambda gg, i, j, s: (gg, i, j))
    return _mm_call(name, a, b, TN, grid, a_spec, b_spec, o_spec, (g, k, n), (tm, tn), (3,), out_dtype, deps)


def rowwise(name, fn, rows, vecs=()):
    t = rows[0].shape[0]
    wmax = max(r.shape[1] for r in rows)
    tr = _row_tile(t, ROW_TILE_BYTES // (4 * wmax))
    row_s = [jax.ShapeDtypeStruct((tr, r.shape[1]), r.dtype) for r in rows]
    vec_s = [jax.ShapeDtypeStruct(v.shape, v.dtype) for v in vecs]
    out_rows_s, out_sums_s = jax.eval_shape(fn, *row_s, *vec_s)
    n_in, n_r = len(rows) + len(vecs), len(out_rows_s)

    def body(*refs):
        i = pl.program_id(0)
        o_rows, o_sums = fn(*[r[...] for r in refs[:n_in]])
        for ref, val in zip(refs[n_in:n_in + n_r], o_rows):
            ref[...] = val

        if out_sums_s:
            @pl.when(i == 0)
            def _():
                for ref in refs[n_in + n_r:]:
                    ref[...] = jnp.zeros_like(ref)

            for ref, val in zip(refs[n_in + n_r:], o_sums):
                ref[...] += val

    in_specs = [pl.BlockSpec((tr, r.shape[1]), lambda i: (i, 0)) for r in rows]
    in_specs += [pl.BlockSpec(v.shape, lambda i: (0, 0)) for v in vecs]
    out_specs = [pl.BlockSpec((tr, s.shape[1]), lambda i: (i, 0)) for s in out_rows_s]
    out_specs += [pl.BlockSpec(s.shape, lambda i: (0, 0)) for s in out_sums_s]
    out_shape = [jax.ShapeDtypeStruct((t, s.shape[1]), s.dtype) for s in out_rows_s]
    out_shape += [jax.ShapeDtypeStruct(s.shape, s.dtype) for s in out_sums_s]
    outs = pl.pallas_call(
        body, name=name, grid=(t // tr,), in_specs=in_specs, out_specs=out_specs, out_shape=out_shape,
        compiler_params=_params(("arbitrary",)),
    )(*rows, *vecs)
    return outs[:n_r], outs[n_r:]


def _rms(x, gain):
    return x * lax.rsqrt(jnp.mean(x * x, axis=-1, keepdims=True) + NORM_EPS) * gain


def _norm_fwd(x, gain):
    return (_rms(x, gain).astype(BF16),), ()


def _add_norm_fwd(h, a, gain):
    h = h + a
    return (h, _rms(h, gain).astype(BF16)), ()


def _relu2_fwd(pre):
    r = jnp.maximum(pre, 0.0)
    return ((r * r).astype(BF16),), ()


def _ple(h, gpre, pu):
    return h + pu * jax.nn.sigmoid(gpre)


def _ple_two_norms_fwd(h, gpre, pu, gain_a, gain_b):
    h = _ple(h, gpre, pu)
    return (h, _rms(h, gain_a).astype(BF16), _rms(h, gain_b).astype(BF16)), ()


def _tail_fwd_bwd(h, gpre, pu, target, gain):
    def row_loss(h, gpre, pu, gain):
        y = _rms(_ple(h, gpre, pu), gain)
        return 0.5 * jnp.mean(jnp.square(y - target), axis=-1, keepdims=True)

    loss, vjp = jax.vjp(row_loss, h, gpre, pu, gain)
    dh, dgpre, dpu, dgain = vjp(jnp.ones_like(loss))
    loss = jnp.broadcast_to(jnp.sum(loss, axis=0, keepdims=True), (1, 128))
    return (dh, dgpre.astype(BF16), dpu.astype(BF16)), (dgain, loss)


def _ple_bwd(gpre, pu, dh):
    _, vjp = jax.vjp(lambda g, u: pu_times_gate(g, u), gpre, pu)
    dgpre, dpu = vjp(dh)
    return (dgpre.astype(BF16), dpu.astype(BF16)), ()


def pu_times_gate(gpre, pu):
    return pu * jax.nn.sigmoid(gpre)


def _norm_bwd(h, du, dh_in, gain):
    _, vjp = jax.vjp(_rms, h, gain)
    dh, dgain = vjp(du)
    dh = dh_in + dh
    return (dh, dh.astype(BF16)), (dgain,)


def _two_norms_bwd(h, du_a, du_b, dh_in, gain_a, gain_b):
    _, vjp = jax.vjp(lambda h, ga, gb: (_rms(h, ga), _rms(h, gb)), h, gain_a, gain_b)
    dh, dga, dgb = vjp((du_a, du_b))
    return (dh_in + dh,), (dga, dgb)


def _relu2_bwd(pre, dact):
    return ((dact * 2.0 * jnp.maximum(pre, 0.0)).astype(BF16),), ()


def _bf16_dot(dims_fwd, dims_da, dims_db, swap_da, swap_db):
    @jax.custom_vjp
    def dot(a, b):
        return lax.dot_general(a.astype(BF16), b.astype(BF16), dims_fwd, preferred_element_type=F32)

    def fwd(a, b):
        return dot(a, b), (a, b)

    def bwd(res, ct):
        a, b = res
        ct, a, b = ct.astype(BF16), a.astype(BF16), b.astype(BF16)
        da = lax.dot_general(*((b, ct) if swap_da else (ct, b)), dims_da, preferred_element_type=F32)
        db = lax.dot_general(*((ct, a) if swap_db else (a, ct)), dims_db, preferred_element_type=F32)
        return da, db

    dot.defvjp(fwd, bwd)
    return dot


_dot_nn = _bf16_dot(NN, NT, TN, False, False)
_dot_nt = _bf16_dot(NT, NN, TN, False, True)
_dot_tn = _bf16_dot(TN, NT, NN, True, False)


def _chunk_masks(transposed):
    r = lax.broadcasted_iota(jnp.int32, (TILE, TILE), 0)
    c = lax.broadcasted_iota(jnp.int32, (TILE, TILE), 1)
    same = (r // CHUNK) == (c // CHUNK)
    causal = same & ((r <= c) if transposed else (c <= r))
    return causal, same


def _chunk_scan(x, reverse):
    pos = lax.broadcasted_iota(jnp.int32, x.shape, 0) % CHUNK
    step = 1
    while step < CHUNK:
        if reverse:
            x = x + jnp.where(pos < CHUNK - step, pltpu.roll(x, x.shape[0] - step, axis=0), 0.0)
        else:
            x = x + jnp.where(pos >= step, pltpu.roll(x, step, axis=0), 0.0)
        step *= 2
    return x


def _chunk_total(x):
    return _chunk_scan(x, False) + _chunk_scan(x, True) - x


@jax.custom_vjp
def _chunk_sums(x):
    return _chunk_scan(x, False), _chunk_total(x)


def _chunk_sums_fwd(x):
    return _chunk_sums(x), None


def _chunk_sums_bwd(_, ct):
    return (_chunk_scan(ct[0], True) + _chunk_total(ct[1]),)


_chunk_sums.defvjp(_chunk_sums_fwd, _chunk_sums_bwd)


def _hgrn_tile(q, f, i, g, lgt, hg, st):
    d = q.shape[1]
    l0, l1 = lgt[0:1], lgt[1:2]
    mx = jnp.maximum(l0, l1)
    e0, e1 = jnp.exp(l0 - mx), jnp.exp(l1 - mx)
    lb = e0 / (e0 + e1)
    fg = lb + (1.0 - lb) * jax.nn.sigmoid(f)
    k = 1.0 - fg
    causal, _ = _chunk_masks(False)
    b, b_last = _chunk_sums(jnp.log(fg))
    q_in = q * jax.nn.sigmoid(q) * (d ** -0.5) * jnp.exp(b)
    k_in = k * jnp.exp(-b)
    k_end = k * jnp.exp(b_last - b)
    att = jnp.where(causal, _dot_nt(q_in, k_in), 0.0)
    o_intra = _dot_nn(att, i)
    outs = []
    for n in range(TILE // CHUNK):
        rows = slice(n * CHUNK, (n + 1) * CHUNK)
        outs.append(o_intra[rows] + _dot_nt(q_in[rows], st))
        decay = jnp.exp(jnp.mean(b_last[rows], axis=0, keepdims=True))
        st = st * decay + _dot_tn(i[rows], k_end[rows])
    o = jnp.concatenate(outs, axis=0)
    o = o * lax.rsqrt(jnp.mean(o * o, axis=-1, keepdims=True) + NORM_EPS) * hg
    return o * (g * jax.nn.sigmoid(g)), st


def hgrn_fwd(z, lgt, hg):
    t, d4 = z.shape
    d = d4 // 4
    nh, nt = d // HEAD_DIM, t // TILE

    def body(q_ref, f_ref, i_ref, g_ref, lgt_ref, hg_ref, o_ref, st_out_ref, st_ref):
        tt = pl.program_id(1)

        @pl.when(tt == 0)
        def _():
            st_ref[...] = jnp.zeros_like(st_ref)

        st = st_ref[...]
        st_out_ref[...] = st
        o, st = _hgrn_tile(q_ref[...], f_ref[...], i_ref[...], g_ref[...], lgt_ref[...], hg_ref[...], st)
        o_ref[...] = o.astype(o_ref.dtype)
        st_ref[...] = st

    def part(p):
        return pl.BlockSpec((TILE, HEAD_DIM), lambda h, tt: (tt, p * nh + h))

    return pl.pallas_call(
        body, name="hgrn_fwd", grid=(nh, nt),
        in_specs=[part(0), part(1), part(2), part(3),
                  pl.BlockSpec((2, HEAD_DIM), lambda h, tt: (0, h)),
                  pl.BlockSpec((1, HEAD_DIM), lambda h, tt: (0, 0))],
        out_specs=[pl.BlockSpec((TILE, HEAD_DIM), lambda h, tt: (tt, h)),
                   pl.BlockSpec((None, None, HEAD_DIM, HEAD_DIM), lambda h, tt: (h, tt, 0, 0))],
        out_shape=[jax.ShapeDtypeStruct((t, d), BF16),
                   jax.ShapeDtypeStruct((nh, nt, HEAD_DIM, HEAD_DIM), F32)],
        scratch_shapes=[pltpu.VMEM((HEAD_DIM, HEAD_DIM), F32)],
        compiler_params=_params(("parallel", "arbitrary")),
    )(z, z, z, z, lgt, hg)


def hgrn_bwd(z, lgt, hg, states, dout):
    t, d4 = z.shape
    d = d4 // 4
    nh, nt = d // HEAD_DIM, t // TILE

    def body(q_ref, f_ref, i_ref, g_ref, lgt_ref, hg_ref, st_in_ref, do_ref, dz_ref, dlgt_ref, dhg_ref, dst_ref):
        h, tt = pl.program_id(0), pl.program_id(1)

        @pl.when(tt == 0)
        def _():
            dst_ref[...] = jnp.zeros_like(dst_ref)
            dlgt_ref[...] = jnp.zeros_like(dlgt_ref)

        @pl.when((tt == 0) & (h == 0))
        def _():
            dhg_ref[...] = jnp.zeros_like(dhg_ref)

        _, vjp = jax.vjp(_hgrn_tile, q_ref[...], f_ref[...], i_ref[...], g_ref[...], lgt_ref[...], hg_ref[...],
                         st_in_ref[...])
        dq, df, di, dg, dlgt, dhg, dst = vjp((do_ref[...], dst_ref[...]))
        dz_ref[0] = dq.astype(dz_ref.dtype)
        dz_ref[1] = df.astype(dz_ref.dtype)
        dz_ref[2] = di.astype(dz_ref.dtype)
        dz_ref[3] = dg.astype(dz_ref.dtype)
        dlgt_ref[...] += dlgt
        dhg_ref[...] += dhg
        dst_ref[...] = dst

    def part(p):
        return pl.BlockSpec((TILE, HEAD_DIM), lambda h, tt: (nt - 1 - tt, p * nh + h))

    return pl.pallas_call(
        body, name="hgrn_bwd", grid=(nh, nt),
        in_specs=[part(0), part(1), part(2), part(3),
                  pl.BlockSpec((2, HEAD_DIM), lambda h, tt: (0, h)),
                  pl.BlockSpec((1, HEAD_DIM), lambda h, tt: (0, 0)),
                  pl.BlockSpec((None, None, HEAD_DIM, HEAD_DIM), lambda h, tt: (h, nt - 1 - tt, 0, 0)),
                  pl.BlockSpec((TILE, HEAD_DIM), lambda h, tt: (nt - 1 - tt, h))],
        out_specs=[pl.BlockSpec((4, TILE, HEAD_DIM), lambda h, tt: (0, nt - 1 - tt, h)),
                   pl.BlockSpec((2, HEAD_DIM), lambda h, tt: (0, h)),
                   pl.BlockSpec((1, HEAD_DIM), lambda h, tt: (0, 0))],
        out_shape=[jax.ShapeDtypeStruct((4, t, d), BF16),
                   jax.ShapeDtypeStruct((2, d), F32),
                   jax.ShapeDtypeStruct((1, HEAD_DIM), F32)],
        scratch_shapes=[pltpu.VMEM((HEAD_DIM, HEAD_DIM), F32)],
        compiler_params=_params(("arbitrary", "arbitrary")),
    )(z, z, z, z, lgt, hg, states, dout)


def _log_sigmoid(x):
    return jnp.minimum(x, 0.0) - jnp.log(1.0 + jnp.exp(-jnp.abs(x)))


def decay_fwd(fl_t, b_f):
    nh, t = fl_t.shape

    def body(fl_ref, b_ref, out_ref):
        r = lax.broadcasted_iota(jnp.int32, (128, 128), 0)
        c = lax.broadcasted_iota(jnp.int32, (128, 128), 1)
        upper = (r <= c).astype(F32)
        carry = jnp.zeros((nh, 1), F32)
        for j in range(t // 128):
            cols = slice(j * 128, (j + 1) * 128)
            ls = _log_sigmoid(fl_ref[:, cols] + b_ref[...])
            out_ref[:, cols] = carry + jnp.dot(ls, upper, precision=lax.Precision.HIGHEST,
                                               preferred_element_type=F32)
            carry = carry + jnp.sum(ls, axis=1, keepdims=True)

    return pl.pallas_call(body, name="decay_fwd", out_shape=jax.ShapeDtypeStruct((nh, t), F32),
                          compiler_params=_params(None))(fl_t, b_f)


def decay_bwd(fl_t, b_f, ddcum):
    nh, t = fl_t.shape

    def body(fl_ref, b_ref, dd_ref, dfl_ref, db_ref):
        r = lax.broadcasted_iota(jnp.int32, (128, 128), 0)
        c = lax.broadcasted_iota(jnp.int32, (128, 128), 1)
        lower = (r >= c).astype(F32)
        carry = jnp.zeros((nh, 1), F32)
        db = jnp.zeros((nh, 1), F32)
        for j in reversed(range(t // 128)):
            cols = slice(j * 128, (j + 1) * 128)
            dd = dd_ref[:, cols]
            dls = carry + jnp.dot(dd, lower, precision=lax.Precision.HIGHEST, preferred_element_type=F32)
            carry = carry + jnp.sum(dd, axis=1, keepdims=True)
            dfl = dls * jax.nn.sigmoid(-(fl_ref[:, cols] + b_ref[...]))
            dfl_ref[:, cols] = dfl
            db = db + jnp.sum(dfl, axis=1, keepdims=True)
        db_ref[...] = db

    return pl.pallas_call(body, name="decay_bwd",
                          out_shape=[jax.ShapeDtypeStruct((nh, t), F32), jax.ShapeDtypeStruct((nh, 1), F32)],
                          compiler_params=_params(None))(fl_t, b_f, ddcum)


def _attn_probs_logits(q_ref, k_ref, dcol_ref, drow_ref, i, tq, t):
    qs = (q_ref[...] * (HEAD_DIM ** -0.5)).astype(BF16)
    s = lax.dot_general(qs, k_ref[...], NT, preferred_element_type=F32)
    s = s + dcol_ref[...] - drow_ref[...]
    row = i * tq + lax.broadcasted_iota(jnp.int32, (tq, t), 0)
    col = lax.broadcasted_iota(jnp.int32, (tq, t), 1)
    mask = col <= row
    return qs, jnp.where(mask, s, NEG_BIG), mask


def attn_fwd(q, k, v, dcol, drow):
    t, d = q.shape
    nh = d // HEAD_DIM
    tq = _tile(t, (256, 128))

    def body(q_ref, k_ref, v_ref, dcol_ref, drow_ref, o_ref, lse_ref):
        i = pl.program_id(1)
        _, s, _ = _attn_probs_logits(q_ref, k_ref, dcol_ref, drow_ref, i, tq, t)
        m = jnp.max(s, axis=1, keepdims=True)
        p = jnp.exp(s - m)
        l = jnp.sum(p, axis=1, keepdims=True)
        o = jnp.dot(p.astype(BF16), v_ref[...], preferred_element_type=F32)
        o_ref[...] = (o / l).astype(o_ref.dtype)
        lse_ref[...] = m + jnp.log(l)

    return pl.pallas_call(
        body, name="attn_fwd", grid=(nh, t // tq),
        in_specs=[pl.BlockSpec((tq, HEAD_DIM), lambda h, i: (i, h)),
                  pl.BlockSpec((t, HEAD_DIM), lambda h, i: (0, h)),
                  pl.BlockSpec((t, HEAD_DIM), lambda h, i: (0, h)),
                  pl.BlockSpec((None, tq, 1), lambda h, i: (h, i, 0)),
                  pl.BlockSpec((None, 1, t), lambda h, i: (h, 0, 0))],
        out_specs=[pl.BlockSpec((tq, HEAD_DIM), lambda h, i: (i, h)),
                   pl.BlockSpec((None, tq, 1), lambda h, i: (h, i, 0))],
        out_shape=[jax.ShapeDtypeStruct((t, d), BF16), jax.ShapeDtypeStruct((nh, t, 1), F32)],
        compiler_params=_params(("parallel", "parallel")),
    )(q, k, v, dcol, drow)


def attn_bwd(q, k, v, dcol, drow, lse, do):
    t, d = q.shape
    nh = d // HEAD_DIM
    tq = _tile(t, (256, 128))
    nq = t // tq

    def body(q_ref, k_ref, v_ref, dcol_ref, drow_ref, lse_ref, do_ref,
             dq_ref, dk_ref, dv_ref, ddrow_ref, dk_acc, dv_acc):
        i = pl.program_id(1)

        @pl.when(i == 0)
        def _():
            dk_acc[...] = jnp.zeros_like(dk_acc)
            dv_acc[...] = jnp.zeros_like(dv_acc)
            ddrow_ref[...] = jnp.zeros_like(ddrow_ref)

        qs, s, mask = _attn_probs_logits(q_ref, k_ref, dcol_ref, drow_ref, i, tq, t)
        p = jnp.where(mask, jnp.exp(s - lse_ref[...]), 0.0)
        do = do_ref[...]
        dp = lax.dot_general(do, v_ref[...], NT, preferred_element_type=F32)
        ds = p * (dp - jnp.sum(p * dp, axis=1, keepdims=True))
        dsb = ds.astype(BF16)
        dq = jnp.dot(dsb, k_ref[...], preferred_element_type=F32) * (HEAD_DIM ** -0.5)
        dq_ref[...] = dq.astype(dq_ref.dtype)
        dk_acc[...] += lax.dot_general(dsb, qs, TN, preferred_element_type=F32)
        dv_acc[...] += lax.dot_general(p.astype(BF16), do, TN, preferred_element_type=F32)
        ddrow_ref[...] -= jnp.sum(ds, axis=0, keepdims=True)

        @pl.when(i == nq - 1)
        def _():
            dk_ref[...] = dk_acc[...].astype(dk_ref.dtype)
            dv_ref[...] = dv_acc[...].astype(dv_ref.dtype)

    tile = pl.BlockSpec((tq, HEAD_DIM), lambda h, i: (i, h))
    full = pl.BlockSpec((t, HEAD_DIM), lambda h, i: (0, h))
    col = pl.BlockSpec((None, tq, 1), lambda h, i: (h, i, 0))
    rowv = pl.BlockSpec((None, 1, t), lambda h, i: (h, 0, 0))
    return pl.pallas_call(
        body, name="attn_bwd", grid=(nh, nq),
        in_specs=[tile, full, full, col, rowv, col, tile],
        out_specs=[tile, full, full, rowv],
        out_shape=[jax.ShapeDtypeStruct((t, d), BF16), jax.ShapeDtypeStruct((t, d), BF16),
                   jax.ShapeDtypeStruct((t, d), BF16), jax.ShapeDtypeStruct((nh, 1, t), F32)],
        scratch_shapes=[pltpu.VMEM((t, HEAD_DIM), F32), pltpu.VMEM((t, HEAD_DIM), F32)],
        compiler_params=_params(("parallel", "arbitrary")),
    )(q, k, v, dcol, drow, lse, do)


def _my_index():
    return (lax.axis_index("x") * 2 + lax.axis_index("y")) * 2 + lax.axis_index("c")


def _exchange(name, src, gather):
    shape = src.shape if gather else src.shape[1:]

    def body(src_ref, out_ref, send_sems, recv_sems, local_sem):
        x, y, c = (lax.axis_index(a) for a in MESH_AXES)
        me = (x * 2 + y) * 2 + c
        mine = src_ref if gather else src_ref.at[me]
        local = pltpu.make_async_copy(mine, out_ref.at[me], local_sem)
        local.start()
        copies = []
        for dlt in range(1, N_DEV):
            dx, dy, dc = dlt // 4, (dlt // 2) % 2, dlt % 2
            px, py, pc = x ^ dx, y ^ dy, c ^ dc
            peer = (px * 2 + py) * 2 + pc
            copies.append(pltpu.make_async_remote_copy(
                src_ref=src_ref if gather else src_ref.at[peer], dst_ref=out_ref.at[me],
                send_sem=send_sems.at[dlt - 1], recv_sem=recv_sems.at[dlt - 1],
                device_id=(px, py, pc), device_id_type=pl.DeviceIdType.MESH))
        for cp in copies:
            cp.start()
        for cp in copies:
            cp.wait_recv()
        for cp in copies:
            cp.wait_send()
        local.wait()

    return pl.pallas_call(
        body, name=name, out_shape=jax.ShapeDtypeStruct((N_DEV,) + tuple(shape), src.dtype),
        in_specs=[pl.BlockSpec(memory_space=pl.ANY)], out_specs=pl.BlockSpec(memory_space=pl.ANY),
        scratch_shapes=[pltpu.SemaphoreType.DMA((N_DEV - 1,)), pltpu.SemaphoreType.DMA((N_DEV - 1,)),
                        pltpu.SemaphoreType.DMA],
        compiler_params=pltpu.CompilerParams(has_side_effects=True),
    )(src)


def all_gather(name, x):
    return _exchange(name, x, True)


_HBM = pl.BlockSpec(memory_space=pltpu.HBM)
_SEM = pl.BlockSpec(memory_space=pltpu.SEMAPHORE)
_DATAFLOW = pltpu.SideEffectType.DATAFLOW_SIDE_EFFECTING


def _peer_copies(src_ref, land_ref, send_sems, recv_sems, gather):
    x, y, c = (lax.axis_index(a) for a in MESH_AXES)
    me = (x * 2 + y) * 2 + c
    copies = []
    for dlt in range(1, N_DEV):
        px, py, pc = x ^ (dlt // 4), y ^ ((dlt // 2) % 2), c ^ (dlt % 2)
        peer = (px * 2 + py) * 2 + pc
        copies.append(pltpu.make_async_remote_copy(
            src_ref=src_ref if gather else src_ref.at[peer], dst_ref=land_ref.at[me],
            send_sem=send_sems.at[dlt - 1], recv_sem=recv_sems.at[dlt - 1],
            device_id=(px, py, pc), device_id_type=pl.DeviceIdType.MESH))
    return copies


def exchange_start(name, srcs, gather):
    n = len(srcs)
    lands = [lax.empty((N_DEV,) + tuple(s.shape if gather else s.shape[1:]), s.dtype) for s in srcs]

    def body(*refs):
        src_refs, land_refs = refs[:n], refs[n:2 * n]
        send_sems, recv_sems = refs[2 * n:3 * n], refs[3 * n:4 * n]
        token = refs[-1]
        for j in range(n):
            for cp in _peer_copies(src_refs[j], land_refs[j], send_sems[j], recv_sems[j], gather):
                cp.start()
        token[...] = jnp.zeros_like(token)

    sems = [pltpu.SemaphoreType.DMA((N_DEV - 1,))] * (2 * n)
    thru = [pltpu.HBM(a.shape, a.dtype) for a in list(srcs) + lands]
    outs = pl.pallas_call(
        body, name=name, out_shape=tuple(sems + thru + [jax.ShapeDtypeStruct((8, 128), F32)]),
        in_specs=[_HBM] * (2 * n), out_specs=tuple([_SEM] * (2 * n) + [_HBM] * (2 * n) + [pl.BlockSpec(memory_space=pltpu.VMEM)]),
        input_output_aliases={j: 2 * n + j for j in range(2 * n)},
        compiler_params=pltpu.CompilerParams(has_side_effects=_DATAFLOW),
    )(*[pltpu.with_memory_space_constraint(a, pltpu.HBM) for a in list(srcs) + lands])
    handles = [(outs[j], outs[n + j], outs[2 * n + j], outs[3 * n + j]) for j in range(n)]
    return handles, outs[-1]


def exchange_wait(name, handle, after, gather):
    send_sems, recv_sems, src, land = handle

    def body(src_ref, land_ref, send_ref, recv_ref, after_ref, src_out, land_out):
        for cp in _peer_copies(src_ref, land_ref, send_ref, recv_ref, gather):
            cp.wait_send()
            cp.wait_recv()

    return pl.pallas_call(
        body, name=name, out_shape=(pltpu.HBM(src.shape, src.dtype), pltpu.HBM(land.shape, land.dtype)),
        in_specs=[_HBM, _HBM, _SEM, _SEM, pl.BlockSpec(memory_space=pl.ANY)], out_specs=(_HBM, _HBM),
        input_output_aliases={0: 0, 1: 1},
        compiler_params=pltpu.CompilerParams(has_side_effects=_DATAFLOW),
    )(src, land, send_sems, recv_sems, after)


def adamw_reduce(name, parts, w, m, v):
    nl, r, wd = w.shape
    tr = _row_tile(r, 2 * ROW_TILE_BYTES // (8 * wd))

    def body(*refs):
        p_refs = refs[:nl]
        w_ref, m_ref, v_ref, g_ref, d_ref, nm_ref, nv_ref = refs[nl:]
        layer = pl.program_id(0)
        for j in range(nl):
            @pl.when(layer == j)
            def _(j=j):
                g = p_refs[j][0].astype(F32)
                for dev in range(1, N_DEV):
                    g = g + p_refs[j][dev].astype(F32)
                nm = B1 * m_ref[...] + (1.0 - B1) * g
                nv = B2 * v_ref[...] + (1.0 - B2) * jnp.square(g)
                m_hat = nm / (1.0 - B1 ** STEP)
                v_hat = nv / (1.0 - B2 ** STEP)
                g_ref[...] = g
                d_ref[...] = -LR * (m_hat / (jnp.sqrt(v_hat) + ADAM_EPS) + WD * w_ref[...])
                nm_ref[...] = nm
                nv_ref[...] = nv

    def part_spec(j):
        return pl.BlockSpec((N_DEV, tr, wd), lambda l, i: (0, jnp.where(l == j, i, 0), 0))

    spec = pl.BlockSpec((None, tr, wd), lambda l, i: (l, i, 0))
    return pl.pallas_call(
        body, name=name, grid=(nl, r // tr),
        in_specs=[part_spec(j) for j in range(nl)] + [spec, spec, spec],
        out_specs=[spec] * 4, out_shape=[jax.ShapeDtypeStruct((nl, r, wd), F32)] * 4,
        compiler_params=_params(("arbitrary", "arbitrary")),
    )(*parts, w, m, v)


def _pack_rows(vectors, rows=None):
    flat = jnp.concatenate([a.reshape(-1).astype(F32) for a in vectors])
    n = flat.shape[0]
    if rows is None:
        rows = -(-n // 1024) * 8
    return jnp.pad(flat, (0, rows * 128 - n)).reshape(rows, 128)


def _unpack_rows(packed, like):
    flat = packed.reshape(-1)
    out, pos = [], 0
    for a in like:
        out.append(flat[pos:pos + a.size].reshape(a.shape))
        pos += a.size
    return out


def kernel(x, p, mix_norm, mlp_norm, ple_norm, w_a_in, a_lb_logits, a_head_gain, w_a_out, kv_norm, w_kvf, b_f, w_b_q, w_b_out, w_mlp_up, w_mlp_down, w_ple_gate, w_ple_up, final_norm, loss_target, m_mix_norm, m_mlp_norm, m_ple_norm, m_w_a_in, m_a_lb_logits, m_a_head_gain, m_w_a_out, m_kv_norm, m_w_kvf, m_b_f, m_w_b_q, m_w_b_out, m_w_mlp_up, m_w_mlp_down, m_w_ple_gate, m_w_ple_up, m_final_norm, v_mix_norm, v_mlp_norm, v_ple_norm, v_w_a_in, v_a_lb_logits, v_a_head_gain, v_w_a_out, v_kv_norm, v_w_kvf, v_b_f, v_w_b_q, v_w_b_out, v_w_mlp_up, v_w_mlp_down, v_w_ple_gate, v_w_ple_up, v_final_norm):
    t, d = x.shape[1], x.shape[2]
    nh = d // HEAD_DIM
    n_layers = 2
    x2 = x.reshape(t, d)
    target = loss_target.reshape(t, d)
    me = _my_index()

    shards = {"w_a_in": w_a_in[0], "w_a_out": w_a_out[0], "w_kvf": w_kvf, "w_b_q": w_b_q[0], "w_b_out": w_b_out[0]}
    for l in range(n_layers):
        shards.update({f"w_mlp_up{l}": w_mlp_up[l], f"w_mlp_down{l}": w_mlp_down[l],
                       f"w_ple_gate{l}": w_ple_gate[l], f"w_ple_up{l}": w_ple_up[l]})
    first_use = ["a_lb_logits", "w_a_in", "w_a_out", "w_mlp_up0", "w_mlp_down0", "w_ple_gate0", "w_ple_up0", "w_kvf",
                 "w_b_q", "w_b_out", "w_mlp_up1", "w_mlp_down1", "w_ple_gate1", "w_ple_up1"]
    row_sharded = ("w_a_out", "w_b_q", "w_b_out", "w_mlp_down", "w_ple_gate")
    shards_bf = [a_lb_logits] + [shards[n].astype(BF16) for n in first_use[1:]]
    ag_handles, _ = exchange_start("ag_start", shards_bf, True)
    weights = {}

    def weight(name, after=None):
        if name not in weights:
            j = first_use.index(name)
            own, land = exchange_wait("ag_wait_" + name, ag_handles[j], after, True)
            g = lax.dynamic_update_slice(land, own[None], (me, 0, 0))
            if name.rstrip("01") in row_sharded:
                g = g.reshape(1, g.shape[0] * g.shape[1], g.shape[2])
            weights[name] = g
        return weights[name]

    lgt = weight("a_lb_logits", x2).transpose(1, 0, 2).reshape(2, d)
    p_bf = [p[l, 0].astype(BF16) for l in range(n_layers)]

    def row(vec):
        return vec.reshape(1, -1)

    def mlp_ple_fwd(l, h_in, a):
        (h_a, u_mlp), _ = rowwise(f"add_norm_mlp{l}", _add_norm_fwd, [h_in, a], [row(mlp_norm[l])])
        pre = mm_nn(f"mlp_up{l}", u_mlp, weight(f"w_mlp_up{l}", u_mlp))
        (act,), _ = rowwise(f"relu2_{l}", _relu2_fwd, [pre])
        mo = mm_nn(f"mlp_down{l}", act, weight(f"w_mlp_down{l}", act))
        (h_b, u_ple), _ = rowwise(f"add_norm_ple{l}", _add_norm_fwd, [h_a, mo], [row(ple_norm[l])])
        gpre = mm_nn(f"ple_gate{l}", u_ple, weight(f"w_ple_gate{l}", u_ple))
        pu = mm_nn(f"ple_up{l}", p_bf[l], weight(f"w_ple_up{l}", gpre))
        return dict(h_a=h_a, u_mlp=u_mlp, pre=pre, act=act, h_b=h_b, u_ple=u_ple, gpre=gpre, pu=pu)

    (u0,), _ = rowwise("norm_mix0", _norm_fwd, [x2], [row(mix_norm[0])])
    z = mm_nn("a_in", u0, weight("w_a_in", u0))
    og, states = hgrn_fwd(z, lgt, a_head_gain)
    a0 = mm_nn("a_out", og, weight("w_a_out", og))
    s0 = mlp_ple_fwd(0, x2, a0)
    (h3, u_kv, u1), _ = rowwise("ple_norms", _ple_two_norms_fwd, [s0["h_b"], s0["gpre"], s0["pu"]],
                                [row(kv_norm), row(mix_norm[1])])
    hk = mm_nn("kvf", u_kv, weight("w_kvf", u_kv), out3=True)
    hk = hk.transpose(1, 0, 2).reshape(t, -1)
    k_bf, v_bf = hk[:, :d].astype(BF16), hk[:, d:2 * d].astype(BF16)
    fl_t = hk[:, 2 * d:].T
    b_f_col = b_f.reshape(nh, 1)
    dcum = decay_fwd(fl_t, b_f_col)
    dcol, drow = dcum.reshape(nh, t, 1), dcum.reshape(nh, 1, t)
    q = mm_nn("b_q", u1, weight("w_b_q", dcum))
    o, lse = attn_fwd(q, k_bf, v_bf, dcol, drow)
    a1 = mm_nn("b_out", o, weight("w_b_out", o))
    s1 = mlp_ple_fwd(1, h3, a1)

    (dh, dgpre, dpu), (d_final, loss_rows) = rowwise(
        "tail", _tail_fwd_bwd, [s1["h_b"], s1["gpre"], s1["pu"], target], [row(final_norm)])
    loss = lax.psum(loss_rows[0, 0], MESH_AXES)

    sent = {}
    tokens = []

    def send_grad(name, g):
        g = g.reshape(N_DEV, -1, g.shape[-1])
        (handle,), token = exchange_start("rs_start_" + name, [g], False)
        sent[name] = handle
        tokens.append(token)

    def after_sends():
        deps = tuple(tokens)
        tokens.clear()
        return deps

    def mlp_ple_bwd(l, s, dh, dgpre, dpu):
        send_grad(f"w_ple_gate{l}", mm_tn(f"d_ple_gate_w{l}", s["u_ple"], dgpre, 1, deps=after_sends()))
        send_grad(f"w_ple_up{l}", mm_tn(f"d_ple_up_w{l}", p_bf[l], dpu, N_DEV, deps=after_sends()))
        du = mm_nt(f"d_ple_gate_x{l}", dgpre, weight(f"w_ple_gate{l}"), deps=after_sends())
        (dh, dh_bf), (d_ple,) = rowwise(f"d_norm_ple{l}", _norm_bwd, [s["h_b"], du, dh], [row(ple_norm[l])])
        send_grad(f"w_mlp_down{l}", mm_tn(f"d_mlp_down_w{l}", s["act"], dh_bf, 1))
        dact = mm_nt(f"d_mlp_down_x{l}", dh_bf, weight(f"w_mlp_down{l}"), deps=after_sends())
        (dpre,), _ = rowwise(f"d_relu2_{l}", _relu2_bwd, [s["pre"], dact])
        send_grad(f"w_mlp_up{l}", mm_tn(f"d_mlp_up_w{l}", s["u_mlp"], dpre, N_DEV))
        du = mm_nt(f"d_mlp_up_x{l}", dpre, weight(f"w_mlp_up{l}"), deps=after_sends())
        (dh, dh_bf), (d_mlp,) = rowwise(f"d_norm_mlp{l}", _norm_bwd, [s["h_a"], du, dh], [row(mlp_norm[l])])
        return dh, dh_bf, d_ple, d_mlp

    dh, dh_bf, d_ple1, d_mlp1 = mlp_ple_bwd(1, s1, dh, dgpre, dpu)
    send_grad("w_b_out", mm_tn("d_b_out_w", o, dh_bf, 1))
    do = mm_nt("d_b_out_x", dh_bf, weight("w_b_out"), out_dtype=BF16, deps=after_sends())
    dq, dk, dv, ddrow = attn_bwd(q, k_bf, v_bf, dcol, drow, lse, do)
    send_grad("w_b_q", mm_tn("d_b_q_w", u1, dq, 1))
    du1 = mm_nt("d_b_q_x", dq, weight("w_b_q"), deps=after_sends())
    dfl_t, d_b_f = decay_bwd(fl_t, b_f_col, ddrow.reshape(nh, t))
    dhk = jnp.concatenate([dk, dv, dfl_t.T.astype(BF16)], axis=1)
    dhk = dhk.reshape(t, N_DEV, -1).transpose(1, 0, 2)
    send_grad("w_kvf", mm_tn("d_kvf_w", u_kv, dhk, N_DEV))
    du_kv = mm_nt("d_kvf_x", dhk, weight("w_kvf"), deps=after_sends())
    (dh,), (d_kv_norm, d_mix1) = rowwise("d_ple_norms", _two_norms_bwd, [h3, du_kv, du1, dh],
                                         [row(kv_norm), row(mix_norm[1])])
    (dgpre, dpu), _ = rowwise("d_ple0", _ple_bwd, [s0["gpre"], s0["pu"], dh])
    dh, dh_bf, d_ple0, d_mlp0 = mlp_ple_bwd(0, s0, dh, dgpre, dpu)
    send_grad("w_a_out", mm_tn("d_a_out_w", og, dh_bf, 1))
    dog = mm_nt("d_a_out_x", dh_bf, weight("w_a_out"), deps=after_sends())
    dz4, d_lgt, d_hg = hgrn_bwd(z, lgt, a_head_gain, states, dog)
    dz = dz4.transpose(1, 0, 2).reshape(t, 4 * d)
    send_grad("w_a_in", mm_tn("d_a_in_w", u0, dz, N_DEV))
    du0 = mm_nt("d_a_in_x", dz, weight("w_a_in"), deps=after_sends())
    (dx, _), (d_mix0,) = rowwise("d_norm_mix0", _norm_bwd, [x2, du0, dh], [row(mix_norm[0])])

    new = {}
    last = [dx]

    def update(name, parts, w, m, v):
        shp = w.shape
        w3, m3, v3 = (a.reshape(len(parts), -1, shp[-1]) for a in (w, m, v))
        new[name] = tuple(a.reshape(shp) for a in adamw_reduce("adamw_" + name, parts, w3, m3, v3))
        last[0] = new[name][0]

    def receive_update(name, layers, w, m, v):
        parts = {}
        for sfx in layers:
            g, land = exchange_wait(f"rs_wait_{name}{sfx}", sent[name + sfx], last[0], False)
            parts[sfx] = lax.dynamic_update_slice(land, lax.dynamic_slice_in_dim(g, me, 1, 0), (me, 0, 0))
        update(name, [parts[sfx] for sfx in sorted(layers)], w, m, v)

    both = ("1", "0")
    receive_update("w_b_out", ("",), w_b_out, m_w_b_out, v_w_b_out)
    receive_update("w_b_q", ("",), w_b_q, m_w_b_q, v_w_b_q)
    receive_update("w_kvf", ("",), w_kvf, m_w_kvf, v_w_kvf)
    receive_update("w_ple_gate", both, w_ple_gate, m_w_ple_gate, v_w_ple_gate)
    receive_update("w_ple_up", both, w_ple_up, m_w_ple_up, v_w_ple_up)
    receive_update("w_mlp_down", both, w_mlp_down, m_w_mlp_down, v_w_mlp_down)
    receive_update("w_mlp_up", both, w_mlp_up, m_w_mlp_up, v_w_mlp_up)
    receive_update("w_a_out", ("",), w_a_out, m_w_a_out, v_w_a_out)
    receive_update("w_a_in", ("",), w_a_in, m_w_a_in, v_w_a_in)

    small = dict(mix_norm=jnp.concatenate([d_mix0, d_mix1]), mlp_norm=jnp.concatenate([d_mlp0, d_mlp1]),
                 ple_norm=jnp.concatenate([d_ple0, d_ple1]), a_head_gain=d_hg, kv_norm=d_kv_norm.reshape(d),
                 b_f=d_b_f.reshape(nh), final_norm=d_final.reshape(d))
    small_w = dict(mix_norm=(mix_norm, m_mix_norm, v_mix_norm), mlp_norm=(mlp_norm, m_mlp_norm, v_mlp_norm),
                   ple_norm=(ple_norm, m_ple_norm, v_ple_norm),
                   a_head_gain=(a_head_gain, m_a_head_gain, v_a_head_gain), kv_norm=(kv_norm, m_kv_norm, v_kv_norm),
                   b_f=(b_f, m_b_f, v_b_f), final_norm=(final_norm, m_final_norm, v_final_norm))
    names = list(small)
    packed = _pack_rows([d_lgt] + [small[n] for n in names])
    everyone = all_gather("ag_small_grads", packed)
    n_lgt_rows = d_lgt.size // 128
    lgt_parts = everyone[:, :n_lgt_rows].reshape(N_DEV, 2, d)
    lgt_parts = lax.dynamic_slice_in_dim(lgt_parts, me * a_lb_logits.shape[1], a_lb_logits.shape[1], axis=2)
    update("a_lb_logits", [lgt_parts], a_lb_logits, m_a_lb_logits, v_a_lb_logits)
    rest = everyone[:, n_lgt_rows:]
    like = [small_w[n][0] for n in names]
    packed_w, packed_m, packed_v = (_pack_rows([small_w[n][j] for n in names], rest.shape[1])[None] for j in range(3))
    outs = adamw_reduce("adamw_small", [rest], packed_w, packed_m, packed_v)
    unpacked = [_unpack_rows(a, like) for a in outs]
    for j, n in enumerate(names):
        new[n] = tuple(unpacked[q][j] for q in range(4))

    order = ["mix_norm", "mlp_norm", "ple_norm", "w_a_in", "a_lb_logits", "a_head_gain", "w_a_out", "kv_norm",
             "w_kvf", "b_f", "w_b_q", "w_b_out", "w_mlp_up", "w_mlp_down", "w_ple_gate", "w_ple_up", "final_norm"]
    result = [loss, dx.reshape(x.shape)]
    for j in range(4):
        result += [new[n][j] for n in order]
    return tuple(result)
```

```python
import functools

import jax
import jax.numpy as jnp
from jax import lax
from jax.experimental import pallas as pl
from jax.experimental.pallas import tpu as pltpu

F32 = jnp.float32
BF16 = jnp.bfloat16
HEAD_DIM = 128
CHUNK = 16
TILE = 128
NORM_EPS = 1e-6
N_DEV = 8
MESH_AXES = ("x", "y", "c")
VMEM_LIMIT_BYTES = 48 * 1024 * 1024
ROW_TILE_BYTES = 1024 * 1024
LR, B1, B2, ADAM_EPS, WD, STEP = 0.001, 0.9, 0.999, 1e-08, 0.01, 10
NEG_BIG = -1e30

NN = (((1,), (0,)), ((), ()))
NT = (((1,), (1,)), ((), ()))
TN = (((0,), (0,)), ((), ()))


def _params(semantics):
    return pltpu.CompilerParams(dimension_semantics=semantics, vmem_limit_bytes=VMEM_LIMIT_BYTES)


def _tile(n, prefs):
    for p in prefs:
        if n % p == 0:
            return p
    return n


def _row_tile(rows, limit):
    for cand in (2048, 1024, 512, 256, 128, 64, 32, 16):
        if cand <= limit and rows % cand == 0:
            return cand
    return rows


def _mm_call(name, a, b, dims, grid, a_spec, b_spec, o_spec, o_shape, acc_shape, k_axes, out_dtype, deps=()):
    nk = 1
    for ax in k_axes:
        nk *= grid[ax]

    def one_step(a_ref, b_ref, *rest):
        o_ref = rest[-1]
        o_ref[...] = lax.dot_general(a_ref[...], b_ref[...], dims, preferred_element_type=F32).astype(o_ref.dtype)

    def accumulate(a_ref, b_ref, *rest):
        o_ref, acc_ref = rest[-2:]
        k = 0
        for ax in k_axes:
            k = k * grid[ax] + pl.program_id(ax)
        part = lax.dot_general(a_ref[...], b_ref[...], dims, preferred_element_type=F32)

        @pl.when(k == 0)
        def _():
            acc_ref[...] = part

        @pl.when((k > 0) & (k < nk - 1))
        def _():
            acc_ref[...] += part

        @pl.when(k == nk - 1)
        def _():
            o_ref[...] = (acc_ref[...] + part).astype(o_ref.dtype)

    sem = tuple("arbitrary" if ax in k_axes else "parallel" for ax in range(len(grid)))
    return pl.pallas_call(
        one_step if nk == 1 else accumulate, name=name, grid=grid,
        in_specs=[a_spec, b_spec] + [pl.BlockSpec(memory_space=pl.ANY)] * len(deps),
        out_specs=o_spec, out_shape=jax.ShapeDtypeStruct(o_shape, out_dtype),
        scratch_shapes=[] if nk == 1 else [pltpu.VMEM(acc_shape, F32)], compiler_params=_params(sem),
    )(a, b, *deps)


def mm_nn(name, a, b3, out_dtype=F32, out3=False, deps=()):
    m, k = a.shape
    g, _, n = b3.shape
    tm, tk = _tile(m, (1024, 512, 256)), _tile(k, (2048, 1024, 512, 256))
    tn = n if out3 else _tile(n, (1024, 512, 256, 128))
    nj = n // tn
    grid = (m // tm, g, nj, k // tk)
    a_spec = pl.BlockSpec((tm, tk), lambda i, gg, j, kk: (i, kk))
    b_spec = pl.BlockSpec((None, tk, tn), lambda i, gg, j, kk: (gg, kk, j))
    if out3:
        o_spec = pl.BlockSpec((None, tm, tn), lambda i, gg, j, kk: (gg, i, j))
        o_shape = (g, m, n)
    else:
        o_spec = pl.BlockSpec((tm, tn), lambda i, gg, j, kk: (i, gg * nj + j))
        o_shape = (m, g * n)
    return _mm_call(name, a, b3, NN, grid, a_spec, b_spec, o_spec, o_shape, (tm, tn), (3,), out_dtype, deps)


def mm_nt(name, a, b3, out_dtype=F32, deps=()):
    g, k, n = b3.shape
    a3 = a.ndim == 3
    m = a.shape[1] if a3 else a.shape[0]
    tm, tko = _tile(m, (1024, 512, 256)), _tile(k, (1024, 512, 256))
    tc = n if a3 else _tile(n, (2048, 1024, 512, 256, 128))
    nc = n // tc
    grid = (m // tm, k // tko, g, nc)
    if a3:
        a_spec = pl.BlockSpec((None, tm, tc), lambda i, j, gg, c: (gg, i, c))
    else:
        a_spec = pl.BlockSpec((tm, tc), lambda i, j, gg, c: (i, gg * nc + c))
    b_spec = pl.BlockSpec((None, tko, tc), lambda i, j, gg, c: (gg, j, c))
    o_spec = pl.BlockSpec((tm, tko), lambda i, j, gg, c: (i, j))
    return _mm_call(name, a, b3, NT, grid, a_spec, b_spec, o_spec, (m, k), (tm, tko), (2, 3), out_dtype, deps)


def mm_tn(name, a, b, g, out_dtype=BF16, deps=()):
    t, k = a.shape
    b3 = b.ndim == 3
    n = b.shape[2] if b3 else b.shape[1] // g
    tm = _tile(k, (1024, 512, 256))
    tn = n if b3 else _tile(n, (1024, 512, 256, 128))
    tt = _tile(t, (2048, 1024, 512, 256))
    nj = n // tn
    grid = (g, k // tm, nj, t // tt)
    a_spec = pl.BlockSpec((tt, tm), lambda gg, i, j, s: (s, i))
    if b3:
        b_spec = pl.BlockSpec((None, tt, tn), lambda gg, i, j, s: (gg, s, j))
    else:
        b_spec = pl.BlockSpec((tt, tn), lambda gg, i, j, s: (s, gg * nj + j))
    o_spec = pl.BlockSpec((None, tm, tn), lambda gg, i, j, s: (gg, i, j))
    return _mm_call(name, a, b, TN, grid, a_spec, b_spec, o_spec, (g, k, n), (tm, tn), (3,), out_dtype, deps)


def rowwise(name, fn, rows, vecs=()):
    t = rows[0].shape[0]
    wmax = max(r.shape[1] for r in rows)
    tr = _row_tile(t, ROW_TILE_BYTES // (4 * wmax))
    row_s = [jax.ShapeDtypeStruct((tr, r.shape[1]), r.dtype) for r in rows]
    vec_s = [jax.ShapeDtypeStruct(v.shape, v.dtype) for v in vecs]
    out_rows_s, out_sums_s = jax.eval_shape(fn, *row_s, *vec_s)
    n_in, n_r = len(rows) + len(vecs), len(out_rows_s)

    def body(*refs):
        i = pl.program_id(0)
        o_rows, o_sums = fn(*[r[...] for r in refs[:n_in]])
        for ref, val in zip(refs[n_in:n_in + n_r], o_rows):
            ref[...] = val

        if out_sums_s:
            @pl.when(i == 0)
            def _():
                for ref in refs[n_in + n_r:]:
                    ref[...] = jnp.zeros_like(ref)

            for ref, val in zip(refs[n_in + n_r:], o_sums):
                ref[...] += val

    in_specs = [pl.BlockSpec((tr, r.shape[1]), lambda i: (i, 0)) for r in rows]
    in_specs += [pl.BlockSpec(v.shape, lambda i: (0, 0)) for v in vecs]
    out_specs = [pl.BlockSpec((tr, s.shape[1]), lambda i: (i, 0)) for s in out_rows_s]
    out_specs += [pl.BlockSpec(s.shape, lambda i: (0, 0)) for s in out_sums_s]
    out_shape = [jax.ShapeDtypeStruct((t, s.shape[1]), s.dtype) for s in out_rows_s]
    out_shape += [jax.ShapeDtypeStruct(s.shape, s.dtype) for s in out_sums_s]
    outs = pl.pallas_call(
        body, name=name, grid=(t // tr,), in_specs=in_specs, out_specs=out_specs, out_shape=out_shape,
        compiler_params=_params(("arbitrary",)),
    )(*rows, *vecs)
    return outs[:n_r], outs[n_r:]


def _rms(x, gain):
    return x * lax.rsqrt(jnp.mean(x * x, axis=-1, keepdims=True) + NORM_EPS) * gain


def _norm_fwd(x, gain):
    return (_rms(x, gain).astype(BF16),), ()


def _add_norm_fwd(h, a, gain):
    h = h + a
    return (h, _rms(h, gain).astype(BF16)), ()


def _relu2_fwd(pre):
    r = jnp.maximum(pre, 0.0)
    return ((r * r).astype(BF16),), ()


def _ple(h, gpre, pu):
    return h + pu * jax.nn.sigmoid(gpre)


def _ple_two_norms_fwd(h, gpre, pu, gain_a, gain_b):
    h = _ple(h, gpre, pu)
    return (h, _rms(h, gain_a).astype(BF16), _rms(h, gain_b).astype(BF16)), ()


def _tail_fwd_bwd(h, gpre, pu, target, gain):
    def row_loss(h, gpre, pu, gain):
        y = _rms(_ple(h, gpre, pu), gain)
        return 0.5 * jnp.mean(jnp.square(y - target), axis=-1, keepdims=True)

    loss, vjp = jax.vjp(row_loss, h, gpre, pu, gain)
    dh, dgpre, dpu, dgain = vjp(jnp.ones_like(loss))
    loss = jnp.broadcast_to(jnp.sum(loss, axis=0, keepdims=True), (1, 128))
    return (dh, dgpre.astype(BF16), dpu.astype(BF16)), (dgain, loss)


def _ple_bwd(gpre, pu, dh):
    _, vjp = jax.vjp(lambda g, u: pu_times_gate(g, u), gpre, pu)
    dgpre, dpu = vjp(dh)
    return (dgpre.astype(BF16), dpu.astype(BF16)), ()


def pu_times_gate(gpre, pu):
    return pu * jax.nn.sigmoid(gpre)


def _norm_bwd(h, du, dh_in, gain):
    _, vjp = jax.vjp(_rms, h, gain)
    dh, dgain = vjp(du)
    dh = dh_in + dh
    return (dh, dh.astype(BF16)), (dgain,)


def _two_norms_bwd(h, du_a, du_b, dh_in, gain_a, gain_b):
    _, vjp = jax.vjp(lambda h, ga, gb: (_rms(h, ga), _rms(h, gb)), h, gain_a, gain_b)
    dh, dga, dgb = vjp((du_a, du_b))
    return (dh_in + dh,), (dga, dgb)


def _relu2_bwd(pre, dact):
    return ((dact * 2.0 * jnp.maximum(pre, 0.0)).astype(BF16),), ()


def _bf16_dot(dims_fwd, dims_da, dims_db, swap_da, swap_db):
    @jax.custom_vjp
    def dot(a, b):
        return lax.dot_general(a.astype(BF16), b.astype(BF16), dims_fwd, preferred_element_type=F32)

    def fwd(a, b):
        return dot(a, b), (a, b)

    def bwd(res, ct):
        a, b = res
        ct, a, b = ct.astype(BF16), a.astype(BF16), b.astype(BF16)
        da = lax.dot_general(*((b, ct) if swap_da else (ct, b)), dims_da, preferred_element_type=F32)
        db = lax.dot_general(*((ct, a) if swap_db else (a, ct)), dims_db, preferred_element_type=F32)
        return da, db

    dot.defvjp(fwd, bwd)
    return dot


_dot_nn = _bf16_dot(NN, NT, TN, False, False)
_dot_nt = _bf16_dot(NT, NN, TN, False, True)
_dot_tn = _bf16_dot(TN, NT, NN, True, False)


def _chunk_masks(transposed):
    r = lax.broadcasted_iota(jnp.int32, (TILE, TILE), 0)
    c = lax.broadcasted_iota(jnp.int32, (TILE, TILE), 1)
    same = (r // CHUNK) == (c // CHUNK)
    causal = same & ((r <= c) if transposed else (c <= r))
    return causal, same


def _chunk_scan(x, reverse):
    pos = lax.broadcasted_iota(jnp.int32, x.shape, 0) % CHUNK
    step = 1
    while step < CHUNK:
        if reverse:
            x = x + jnp.where(pos < CHUNK - step, pltpu.roll(x, x.shape[0] - step, axis=0), 0.0)
        else:
            x = x + jnp.where(pos >= step, pltpu.roll(x, step, axis=0), 0.0)
        step *= 2
    return x


def _chunk_total(x):
    return _chunk_scan(x, False) + _chunk_scan(x, True) - x


@jax.custom_vjp
def _chunk_sums(x):
    return _chunk_scan(x, False), _chunk_total(x)


def _chunk_sums_fwd(x):
    return _chunk_sums(x), None


def _chunk_sums_bwd(_, ct):
    return (_chunk_scan(ct[0], True) + _chunk_total(ct[1]),)


_chunk_sums.defvjp(_chunk_sums_fwd, _chunk_sums_bwd)


def _hgrn_tile(q, f, i, g, lgt, hg, st):
    d = q.shape[1]
    l0, l1 = lgt[0:1], lgt[1:2]
    mx = jnp.maximum(l0, l1)
    e0, e1 = jnp.exp(l0 - mx), jnp.exp(l1 - mx)
    lb = e0 / (e0 + e1)
    fg = lb + (1.0 - lb) * jax.nn.sigmoid(f)
    k = 1.0 - fg
    causal, _ = _chunk_masks(False)
    b, b_last = _chunk_sums(jnp.log(fg))
    q_in = q * jax.nn.sigmoid(q) * (d ** -0.5) * jnp.exp(b)
    k_in = k * jnp.exp(-b)
    k_end = k * jnp.exp(b_last - b)
    att = jnp.where(causal, _dot_nt(q_in, k_in), 0.0)
    o_intra = _dot_nn(att, i)
    outs = []
    for n in range(TILE // CHUNK):
        rows = slice(n * CHUNK, (n + 1) * CHUNK)
        outs.append(o_intra[rows] + _dot_nt(q_in[rows], st))
        decay = jnp.exp(jnp.mean(b_last[rows], axis=0, keepdims=True))
        st = st * decay + _dot_tn(i[rows], k_end[rows])
    o = jnp.concatenate(outs, axis=0)
    o = o * lax.rsqrt(jnp.mean(o * o, axis=-1, keepdims=True) + NORM_EPS) * hg
    return o * (g * jax.nn.sigmoid(g)), st


def hgrn_fwd(z, lgt, hg):
    t, d4 = z.shape
    d = d4 // 4
    nh, nt = d // HEAD_DIM, t // TILE

    def body(q_ref, f_ref, i_ref, g_ref, lgt_ref, hg_ref, o_ref, st_out_ref, st_ref):
        tt = pl.program_id(1)

        @pl.when(tt == 0)
        def _():
            st_ref[...] = jnp.zeros_like(st_ref)

        st = st_ref[...]
        st_out_ref[...] = st
        o, st = _hgrn_tile(q_ref[...], f_ref[...], i_ref[...], g_ref[...], lgt_ref[...], hg_ref[...], st)
        o_ref[...] = o.astype(o_ref.dtype)
        st_ref[...] = st

    def part(p):
        return pl.BlockSpec((TILE, HEAD_DIM), lambda h, tt: (tt, p * nh + h))

    return pl.pallas_call(
        body, name="hgrn_fwd", grid=(nh, nt),
        in_specs=[part(0), part(1), part(2), part(3),
                  pl.BlockSpec((2, HEAD_DIM), lambda h, tt: (0, h)),
                  pl.BlockSpec((1, HEAD_DIM), lambda h, tt: (0, 0))],
        out_specs=[pl.BlockSpec((TILE, HEAD_DIM), lambda h, tt: (tt, h)),
                   pl.BlockSpec((None, None, HEAD_DIM, HEAD_DIM), lambda h, tt: (h, tt, 0, 0))],
        out_shape=[jax.ShapeDtypeStruct((t, d), BF16),
                   jax.ShapeDtypeStruct((nh, nt, HEAD_DIM, HEAD_DIM), F32)],
        scratch_shapes=[pltpu.VMEM((HEAD_DIM, HEAD_DIM), F32)],
        compiler_params=_params(("parallel", "arbitrary")),
    )(z, z, z, z, lgt, hg)


def hgrn_bwd(z, lgt, hg, states, dout):
    t, d4 = z.shape
    d = d4 // 4
    nh, nt = d // HEAD_DIM, t // TILE

    def body(q_ref, f_ref, i_ref, g_ref, lgt_ref, hg_ref, st_in_ref, do_ref, dz_ref, dlgt_ref, dhg_ref, dst_ref):
        h, tt = pl.program_id(0), pl.program_id(1)

        @pl.when(tt == 0)
        def _():
            dst_ref[...] = jnp.zeros_like(dst_ref)
            dlgt_ref[...] = jnp.zeros_like(dlgt_ref)

        @pl.when((tt == 0) & (h == 0))
        def _():
            dhg_ref[...] = jnp.zeros_like(dhg_ref)

        _, vjp = jax.vjp(_hgrn_tile, q_ref[...], f_ref[...], i_ref[...], g_ref[...], lgt_ref[...], hg_ref[...],
                         st_in_ref[...])
        dq, df, di, dg, dlgt, dhg, dst = vjp((do_ref[...], dst_ref[...]))
        dz_ref[0] = dq.astype(dz_ref.dtype)
        dz_ref[1] = df.astype(dz_ref.dtype)
        dz_ref[2] = di.astype(dz_ref.dtype)
        dz_ref[3] = dg.astype(dz_ref.dtype)
        dlgt_ref[...] += dlgt
        dhg_ref[...] += dhg
        dst_ref[...] = dst

    def part(p):
        return pl.BlockSpec((TILE, HEAD_DIM), lambda h, tt: (nt - 1 - tt, p * nh + h))

    return pl.pallas_call(
        body, name="hgrn_bwd", grid=(nh, nt),
        in_specs=[part(0), part(1), part(2), part(3),
                  pl.BlockSpec((2, HEAD_DIM), lambda h, tt: (0, h)),
                  pl.BlockSpec((1, HEAD_DIM), lambda h, tt: (0, 0)),
                  pl.BlockSpec((None, None, HEAD_DIM, HEAD_DIM), lambda h, tt: (h, nt - 1 - tt, 0, 0)),
                  pl.BlockSpec((TILE, HEAD_DIM), lambda h, tt: (nt - 1 - tt, h))],
        out_specs=[pl.BlockSpec((4, TILE, HEAD_DIM), lambda h, tt: (0, nt - 1 - tt, h)),
                   pl.BlockSpec((2, HEAD_DIM), lambda h, tt: (0, h)),
                   pl.BlockSpec((1, HEAD_DIM), lambda h, tt: (0, 0))],
        out_shape=[jax.ShapeDtypeStruct((4, t, d), BF16),
                   jax.ShapeDtypeStruct((2, d), F32),
                   jax.ShapeDtypeStruct((1, HEAD_DIM), F32)],
        scratch_shapes=[pltpu.VMEM((HEAD_DIM, HEAD_DIM), F32)],
        compiler_params=_params(("arbitrary", "arbitrary")),
    )(z, z, z, z, lgt, hg, states, dout)


def _log_sigmoid(x):
    return jnp.minimum(x, 0.0) - jnp.log(1.0 + jnp.exp(-jnp.abs(x)))


def decay_fwd(fl_t, b_f):
    nh, t = fl_t.shape

    def body(fl_ref, b_ref, out_ref):
        r = lax.broadcasted_iota(jnp.int32, (128, 128), 0)
        c = lax.broadcasted_iota(jnp.int32, (128, 128), 1)
        upper = (r <= c).astype(F32)
        carry = jnp.zeros((nh, 1), F32)
        for j in range(t // 128):
            cols = slice(j * 128, (j + 1) * 128)
            ls = _log_sigmoid(fl_ref[:, cols] + b_ref[...])
            out_ref[:, cols] = carry + jnp.dot(ls, upper, precision=lax.Precision.HIGHEST,
                                               preferred_element_type=F32)
            carry = carry + jnp.sum(ls, axis=1, keepdims=True)

    return pl.pallas_call(body, name="decay_fwd", out_shape=jax.ShapeDtypeStruct((nh, t), F32),
                          compiler_params=_params(None))(fl_t, b_f)


def decay_bwd(fl_t, b_f, ddcum):
    nh, t = fl_t.shape

    def body(fl_ref, b_ref, dd_ref, dfl_ref, db_ref):
        r = lax.broadcasted_iota(jnp.int32, (128, 128), 0)
        c = lax.broadcasted_iota(jnp.int32, (128, 128), 1)
        lower = (r >= c).astype(F32)
        carry = jnp.zeros((nh, 1), F32)
        db = jnp.zeros((nh, 1), F32)
        for j in reversed(range(t // 128)):
            cols = slice(j * 128, (j + 1) * 128)
            dd = dd_ref[:, cols]
            dls = carry + jnp.dot(dd, lower, precision=lax.Precision.HIGHEST, preferred_element_type=F32)
            carry = carry + jnp.sum(dd, axis=1, keepdims=True)
            dfl = dls * jax.nn.sigmoid(-(fl_ref[:, cols] + b_ref[...]))
            dfl_ref[:, cols] = dfl
            db = db + jnp.sum(dfl, axis=1, keepdims=True)
        db_ref[...] = db

    return pl.pallas_call(body, name="decay_bwd",
                          out_shape=[jax.ShapeDtypeStruct((nh, t), F32), jax.ShapeDtypeStruct((nh, 1), F32)],
                          compiler_params=_params(None))(fl_t, b_f, ddcum)


def _attn_probs_logits(q_ref, k_ref, dcol_ref, drow_ref, i, tq, t):
    qs = (q_ref[...] * (HEAD_DIM ** -0.5)).astype(BF16)
    s = lax.dot_general(qs, k_ref[...], NT, preferred_element_type=F32)
    s = s + dcol_ref[...] - drow_ref[...]
    row = i * tq + lax.broadcasted_iota(jnp.int32, (tq, t), 0)
    col = lax.broadcasted_iota(jnp.int32, (tq, t), 1)
    mask = col <= row
    return qs, jnp.where(mask, s, NEG_BIG), mask


def attn_fwd(q, k, v, dcol, drow):
    t, d = q.shape
    nh = d // HEAD_DIM
    tq = _tile(t, (256, 128))

    def body(q_ref, k_ref, v_ref, dcol_ref, drow_ref, o_ref, lse_ref):
        i = pl.program_id(1)
        _, s, _ = _attn_probs_logits(q_ref, k_ref, dcol_ref, drow_ref, i, tq, t)
        m = jnp.max(s, axis=1, keepdims=True)
        p = jnp.exp(s - m)
        l = jnp.sum(p, axis=1, keepdims=True)
        o = jnp.dot(p.astype(BF16), v_ref[...], preferred_element_type=F32)
        o_ref[...] = (o / l).astype(o_ref.dtype)
        lse_ref[...] = m + jnp.log(l)

    return pl.pallas_call(
        body, name="attn_fwd", grid=(nh, t // tq),
        in_specs=[pl.BlockSpec((tq, HEAD_DIM), lambda h, i: (i, h)),
                  pl.BlockSpec((t, HEAD_DIM), lambda h, i: (0, h)),
                  pl.BlockSpec((t, HEAD_DIM), lambda h, i: (0, h)),
                  pl.BlockSpec((None, tq, 1), lambda h, i: (h, i, 0)),
                  pl.BlockSpec((None, 1, t), lambda h, i: (h, 0, 0))],
        out_specs=[pl.BlockSpec((tq, HEAD_DIM), lambda h, i: (i, h)),
                   pl.BlockSpec((None, tq, 1), lambda h, i: (h, i, 0))],
        out_shape=[jax.ShapeDtypeStruct((t, d), BF16), jax.ShapeDtypeStruct((nh, t, 1), F32)],
        compiler_params=_params(("parallel", "parallel")),
    )(q, k, v, dcol, drow)


def attn_bwd(q, k, v, dcol, drow, lse, do):
    t, d = q.shape
    nh = d // HEAD_DIM
    tq = _tile(t, (256, 128))
    nq = t // tq

    def body(q_ref, k_ref, v_ref, dcol_ref, drow_ref, lse_ref, do_ref,
             dq_ref, dk_ref, dv_ref, ddrow_ref, dk_acc, dv_acc):
        i = pl.program_id(1)

        @pl.when(i == 0)
        def _():
            dk_acc[...] = jnp.zeros_like(dk_acc)
            dv_acc[...] = jnp.zeros_like(dv_acc)
            ddrow_ref[...] = jnp.zeros_like(ddrow_ref)

        qs, s, mask = _attn_probs_logits(q_ref, k_ref, dcol_ref, drow_ref, i, tq, t)
        p = jnp.where(mask, jnp.exp(s - lse_ref[...]), 0.0)
        do = do_ref[...]
        dp = lax.dot_general(do, v_ref[...], NT, preferred_element_type=F32)
        ds = p * (dp - jnp.sum(p * dp, axis=1, keepdims=True))
        dsb = ds.astype(BF16)
        dq = jnp.dot(dsb, k_ref[...], preferred_element_type=F32) * (HEAD_DIM ** -0.5)
        dq_ref[...] = dq.astype(dq_ref.dtype)
        dk_acc[...] += lax.dot_general(dsb, qs, TN, preferred_element_type=F32)
        dv_acc[...] += lax.dot_general(p.astype(BF16), do, TN, preferred_element_type=F32)
        ddrow_ref[...] -= jnp.sum(ds, axis=0, keepdims=True)

        @pl.when(i == nq - 1)
        def _():
            dk_ref[...] = dk_acc[...].astype(dk_ref.dtype)
            dv_ref[...] = dv_acc[...].astype(dv_ref.dtype)

    tile = pl.BlockSpec((tq, HEAD_DIM), lambda h, i: (i, h))
    full = pl.BlockSpec((t, HEAD_DIM), lambda h, i: (0, h))
    col = pl.BlockSpec((None, tq, 1), lambda h, i: (h, i, 0))
    rowv = pl.BlockSpec((None, 1, t), lambda h, i: (h, 0, 0))
    return pl.pallas_call(
        body, name="attn_bwd", grid=(nh, nq),
        in_specs=[tile, full, full, col, rowv, col, tile],
        out_specs=[tile, full, full, rowv],
        out_shape=[jax.ShapeDtypeStruct((t, d), BF16), jax.ShapeDtypeStruct((t, d), BF16),
                   jax.ShapeDtypeStruct((t, d), BF16), jax.ShapeDtypeStruct((nh, 1, t), F32)],
        scratch_shapes=[pltpu.VMEM((t, HEAD_DIM), F32), pltpu.VMEM((t, HEAD_DIM), F32)],
        compiler_params=_params(("parallel", "arbitrary")),
    )(q, k, v, dcol, drow, lse, do)


def _my_index():
    return (lax.axis_index("x") * 2 + lax.axis_index("y")) * 2 + lax.axis_index("c")


def _exchange(name, src, gather, deps=()):
    shape = src.shape if gather else src.shape[1:]

    def body(src_ref, *rest):
        out_ref, send_sems, recv_sems, local_sem = rest[len(deps):]
        x, y, c = (lax.axis_index(a) for a in MESH_AXES)
        me = (x * 2 + y) * 2 + c
        mine = src_ref if gather else src_ref.at[me]
        local = pltpu.make_async_copy(mine, out_ref.at[me], local_sem)
        local.start()
        copies = []
        for dlt in range(1, N_DEV):
            dx, dy, dc = dlt // 4, (dlt // 2) % 2, dlt % 2
            px, py, pc = x ^ dx, y ^ dy, c ^ dc
            peer = (px * 2 + py) * 2 + pc
            copies.append(pltpu.make_async_remote_copy(
                src_ref=src_ref if gather else src_ref.at[peer], dst_ref=out_ref.at[me],
                send_sem=send_sems.at[dlt - 1], recv_sem=recv_sems.at[dlt - 1],
                device_id=(px, py, pc), device_id_type=pl.DeviceIdType.MESH))
        for cp in copies:
            cp.start()
        for cp in copies:
            cp.wait_recv()
        for cp in copies:
            cp.wait_send()
        local.wait()

    return pl.pallas_call(
        body, name=name, out_shape=jax.ShapeDtypeStruct((N_DEV,) + tuple(shape), src.dtype),
        in_specs=[pl.BlockSpec(memory_space=pl.ANY)] * (1 + len(deps)), out_specs=pl.BlockSpec(memory_space=pl.ANY),
        scratch_shapes=[pltpu.SemaphoreType.DMA((N_DEV - 1,)), pltpu.SemaphoreType.DMA((N_DEV - 1,)),
                        pltpu.SemaphoreType.DMA],
        compiler_params=pltpu.CompilerParams(has_side_effects=True),
    )(src, *deps)


def all_gather(name, x, deps=()):
    return _exchange(name, x, True, deps)


_HBM = pl.BlockSpec(memory_space=pltpu.HBM)
_SEM = pl.BlockSpec(memory_space=pltpu.SEMAPHORE)
_DATAFLOW = pltpu.SideEffectType.DATAFLOW_SIDE_EFFECTING


def _peer_copies(src_ref, land_ref, send_sems, recv_sems, gather):
    x, y, c = (lax.axis_index(a) for a in MESH_AXES)
    me = (x * 2 + y) * 2 + c
    copies = []
    for dlt in range(1, N_DEV):
        px, py, pc = x ^ (dlt // 4), y ^ ((dlt // 2) % 2), c ^ (dlt % 2)
        peer = (px * 2 + py) * 2 + pc
        copies.append(pltpu.make_async_remote_copy(
            src_ref=src_ref if gather else src_ref.at[peer], dst_ref=land_ref.at[me],
            send_sem=send_sems.at[dlt - 1], recv_sem=recv_sems.at[dlt - 1],
            device_id=(px, py, pc), device_id_type=pl.DeviceIdType.MESH))
    return copies


def exchange_start(name, srcs, gather):
    n = len(srcs)
    lands = [lax.empty((N_DEV,) + tuple(s.shape if gather else s.shape[1:]), s.dtype) for s in srcs]

    def body(*refs):
        src_refs, land_refs = refs[:n], refs[n:2 * n]
        send_sems, recv_sems = refs[2 * n:3 * n], refs[3 * n:4 * n]
        token = refs[-1]
        for j in range(n):
            for cp in _peer_copies(src_refs[j], land_refs[j], send_sems[j], recv_sems[j], gather):
                cp.start()
        token[...] = jnp.zeros_like(token)

    sems = [pltpu.SemaphoreType.DMA((N_DEV - 1,))] * (2 * n)
    thru = [pltpu.HBM(a.shape, a.dtype) for a in list(srcs) + lands]
    outs = pl.pallas_call(
        body, name=name, out_shape=tuple(sems + thru + [jax.ShapeDtypeStruct((8, 128), F32)]),
        in_specs=[_HBM] * (2 * n), out_specs=tuple([_SEM] * (2 * n) + [_HBM] * (2 * n) + [pl.BlockSpec(memory_space=pltpu.VMEM)]),
        input_output_aliases={j: 2 * n + j for j in range(2 * n)},
        compiler_params=pltpu.CompilerParams(has_side_effects=_DATAFLOW),
    )(*[pltpu.with_memory_space_constraint(a, pltpu.HBM) for a in list(srcs) + lands])
    handles = [(outs[j], outs[n + j], outs[2 * n + j], outs[3 * n + j]) for j in range(n)]
    return handles, outs[-1]


def exchange_wait(name, handle, after, gather):
    send_sems, recv_sems, src, land = handle

    def body(src_ref, land_ref, send_ref, recv_ref, after_ref, src_out, land_out):
        for cp in _peer_copies(src_ref, land_ref, send_ref, recv_ref, gather):
            cp.wait_send()
            cp.wait_recv()

    return pl.pallas_call(
        body, name=name, out_shape=(pltpu.HBM(src.shape, src.dtype), pltpu.HBM(land.shape, land.dtype)),
        in_specs=[_HBM, _HBM, _SEM, _SEM, pl.BlockSpec(memory_space=pl.ANY)], out_specs=(_HBM, _HBM),
        input_output_aliases={0: 0, 1: 1},
        compiler_params=pltpu.CompilerParams(has_side_effects=_DATAFLOW),
    )(src, land, send_sems, recv_sems, after)


def adamw_reduce(name, parts, w, m, v):
    nl, r, wd = w.shape
    tr = _row_tile(r, 2 * ROW_TILE_BYTES // (8 * wd))

    def body(*refs):
        p_refs = refs[:nl]
        w_ref, m_ref, v_ref, g_ref, d_ref, nm_ref, nv_ref = refs[nl:]
        layer = pl.program_id(0)
        for j in range(nl):
            @pl.when(layer == j)
            def _(j=j):
                g = p_refs[j][0].astype(F32)
                for dev in range(1, N_DEV):
                    g = g + p_refs[j][dev].astype(F32)
                nm = B1 * m_ref[...] + (1.0 - B1) * g
                nv = B2 * v_ref[...] + (1.0 - B2) * jnp.square(g)
                m_hat = nm / (1.0 - B1 ** STEP)
                v_hat = nv / (1.0 - B2 ** STEP)
                g_ref[...] = g
                d_ref[...] = -LR * (m_hat / (jnp.sqrt(v_hat) + ADAM_EPS) + WD * w_ref[...])
                nm_ref[...] = nm
                nv_ref[...] = nv

    def part_spec(j):
        return pl.BlockSpec((N_DEV, tr, wd), lambda l, i: (0, jnp.where(l == j, i, 0), 0))

    spec = pl.BlockSpec((None, tr, wd), lambda l, i: (l, i, 0))
    return pl.pallas_call(
        body, name=name, grid=(nl, r // tr),
        in_specs=[part_spec(j) for j in range(nl)] + [spec, spec, spec],
        out_specs=[spec] * 4, out_shape=[jax.ShapeDtypeStruct((nl, r, wd), F32)] * 4,
        compiler_params=_params(("arbitrary", "arbitrary")),
    )(*parts, w, m, v)


def _pack_rows(vectors, rows=None):
    flat = jnp.concatenate([a.reshape(-1).astype(F32) for a in vectors])
    n = flat.shape[0]
    if rows is None:
        rows = -(-n // 1024) * 8
    return jnp.pad(flat, (0, rows * 128 - n)).reshape(rows, 128)


def _unpack_rows(packed, like):
    flat = packed.reshape(-1)
    out, pos = [], 0
    for a in like:
        out.append(flat[pos:pos + a.size].reshape(a.shape))
        pos += a.size
    return out


def kernel(x, p, mix_norm, mlp_norm, ple_norm, w_a_in, a_lb_logits, a_head_gain, w_a_out, kv_norm, w_kvf, b_f, w_b_q, w_b_out, w_mlp_up, w_mlp_down, w_ple_gate, w_ple_up, final_norm, loss_target, m_mix_norm, m_mlp_norm, m_ple_norm, m_w_a_in, m_a_lb_logits, m_a_head_gain, m_w_a_out, m_kv_norm, m_w_kvf, m_b_f, m_w_b_q, m_w_b_out, m_w_mlp_up, m_w_mlp_down, m_w_ple_gate, m_w_ple_up, m_final_norm, v_mix_norm, v_mlp_norm, v_ple_norm, v_w_a_in, v_a_lb_logits, v_a_head_gain, v_w_a_out, v_kv_norm, v_w_kvf, v_b_f, v_w_b_q, v_w_b_out, v_w_mlp_up, v_w_mlp_down, v_w_ple_gate, v_w_ple_up, v_final_norm):
    t, d = x.shape[1], x.shape[2]
    nh = d // HEAD_DIM
    n_layers = 2
    x2 = x.reshape(t, d)
    target = loss_target.reshape(t, d)
    me = _my_index()

    shards = {"w_a_in": w_a_in[0], "w_a_out": w_a_out[0], "w_kvf": w_kvf, "w_b_q": w_b_q[0], "w_b_out": w_b_out[0]}
    for l in range(n_layers):
        shards.update({f"w_mlp_up{l}": w_mlp_up[l], f"w_mlp_down{l}": w_mlp_down[l],
                       f"w_ple_gate{l}": w_ple_gate[l], f"w_ple_up{l}": w_ple_up[l]})
    first_use = ["a_lb_logits", "w_a_in", "w_a_out", "w_mlp_up0", "w_mlp_down0", "w_ple_gate0", "w_ple_up0", "w_kvf",
                 "w_b_q", "w_b_out", "w_mlp_up1", "w_mlp_down1", "w_ple_gate1", "w_ple_up1"]
    row_sharded = ("w_a_out", "w_b_q", "w_b_out", "w_mlp_down", "w_ple_gate")
    shards_bf = [a_lb_logits] + [shards[n].astype(BF16) for n in first_use[1:]]
    ag_handles, _ = exchange_start("ag_start", shards_bf, True)
    weights = {}

    def weight(name, after=None):
        if name not in weights:
            j = first_use.index(name)
            own, land = exchange_wait("ag_wait_" + name, ag_handles[j], after, True)
            g = lax.dynamic_update_slice(land, own[None], (me, 0, 0))
            if name.rstrip("01") in row_sharded:
                g = g.reshape(1, g.shape[0] * g.shape[1], g.shape[2])
            weights[name] = g
        return weights[name]

    lgt = weight("a_lb_logits", x2).transpose(1, 0, 2).reshape(2, d)
    p_bf = [p[l, 0].astype(BF16) for l in range(n_layers)]

    def row(vec):
        return vec.reshape(1, -1)

    def mlp_ple_fwd(l, h_in, a):
        (h_a, u_mlp), _ = rowwise(f"add_norm_mlp{l}", _add_norm_fwd, [h_in, a], [row(mlp_norm[l])])
        pre = mm_nn(f"mlp_up{l}", u_mlp, weight(f"w_mlp_up{l}", u_mlp))
        (act,), _ = rowwise(f"relu2_{l}", _relu2_fwd, [pre])
        mo = mm_nn(f"mlp_down{l}", act, weight(f"w_mlp_down{l}", act))
        (h_b, u_ple), _ = rowwise(f"add_norm_ple{l}", _add_norm_fwd, [h_a, mo], [row(ple_norm[l])])
        gpre = mm_nn(f"ple_gate{l}", u_ple, weight(f"w_ple_gate{l}", u_ple))
        pu = mm_nn(f"ple_up{l}", p_bf[l], weight(f"w_ple_up{l}", gpre))
        return dict(h_a=h_a, u_mlp=u_mlp, pre=pre, act=act, h_b=h_b, u_ple=u_ple, gpre=gpre, pu=pu)

    (u0,), _ = rowwise("norm_mix0", _norm_fwd, [x2], [row(mix_norm[0])])
    z = mm_nn("a_in", u0, weight("w_a_in", u0))
    og, states = hgrn_fwd(z, lgt, a_head_gain)
    a0 = mm_nn("a_out", og, weight("w_a_out", og))
    s0 = mlp_ple_fwd(0, x2, a0)
    (h3, u_kv, u1), _ = rowwise("ple_norms", _ple_two_norms_fwd, [s0["h_b"], s0["gpre"], s0["pu"]],
                                [row(kv_norm), row(mix_norm[1])])
    hk = mm_nn("kvf", u_kv, weight("w_kvf", u_kv), out3=True)
    hk = hk.transpose(1, 0, 2).reshape(t, -1)
    k_bf, v_bf = hk[:, :d].astype(BF16), hk[:, d:2 * d].astype(BF16)
    fl_t = hk[:, 2 * d:].T
    b_f_col = b_f.reshape(nh, 1)
    dcum = decay_fwd(fl_t, b_f_col)
    dcol, drow = dcum.reshape(nh, t, 1), dcum.reshape(nh, 1, t)
    q = mm_nn("b_q", u1, weight("w_b_q", dcum))
    o, lse = attn_fwd(q, k_bf, v_bf, dcol, drow)
    a1 = mm_nn("b_out", o, weight("w_b_out", o))
    s1 = mlp_ple_fwd(1, h3, a1)

    (dh, dgpre, dpu), (d_final, loss_rows) = rowwise(
        "tail", _tail_fwd_bwd, [s1["h_b"], s1["gpre"], s1["pu"], target], [row(final_norm)])
    loss = lax.psum(loss_rows[0, 0], MESH_AXES)

    sent = {}
    tokens = []

    def send_grad(name, g):
        g = g.reshape(N_DEV, -1, g.shape[-1])
        (handle,), token = exchange_start("rs_start_" + name, [g], False)
        sent[name] = handle
        tokens.append(token)

    def after_sends():
        deps = tuple(tokens)
        tokens.clear()
        return deps

    def mlp_ple_bwd(l, s, dh, dgpre, dpu):
        send_grad(f"w_ple_gate{l}", mm_tn(f"d_ple_gate_w{l}", s["u_ple"], dgpre, 1, deps=after_sends()))
        send_grad(f"w_ple_up{l}", mm_tn(f"d_ple_up_w{l}", p_bf[l], dpu, N_DEV, deps=after_sends()))
        du = mm_nt(f"d_ple_gate_x{l}", dgpre, weight(f"w_ple_gate{l}"), deps=after_sends())
        (dh, dh_bf), (d_ple,) = rowwise(f"d_norm_ple{l}", _norm_bwd, [s["h_b"], du, dh], [row(ple_norm[l])])
        send_grad(f"w_mlp_down{l}", mm_tn(f"d_mlp_down_w{l}", s["act"], dh_bf, 1))
        dact = mm_nt(f"d_mlp_down_x{l}", dh_bf, weight(f"w_mlp_down{l}"), deps=after_sends())
        (dpre,), _ = rowwise(f"d_relu2_{l}", _relu2_bwd, [s["pre"], dact])
        send_grad(f"w_mlp_up{l}", mm_tn(f"d_mlp_up_w{l}", s["u_mlp"], dpre, N_DEV))
        du = mm_nt(f"d_mlp_up_x{l}", dpre, weight(f"w_mlp_up{l}"), deps=after_sends())
        (dh, dh_bf), (d_mlp,) = rowwise(f"d_norm_mlp{l}", _norm_bwd, [s["h_a"], du, dh], [row(mlp_norm[l])])
        return dh, dh_bf, d_ple, d_mlp

    dh, dh_bf, d_ple1, d_mlp1 = mlp_ple_bwd(1, s1, dh, dgpre, dpu)
    send_grad("w_b_out", mm_tn("d_b_out_w", o, dh_bf, 1))
    do = mm_nt("d_b_out_x", dh_bf, weight("w_b_out"), out_dtype=BF16, deps=after_sends())
    dq, dk, dv, ddrow = attn_bwd(q, k_bf, v_bf, dcol, drow, lse, do)
    send_grad("w_b_q", mm_tn("d_b_q_w", u1, dq, 1))
    du1 = mm_nt("d_b_q_x", dq, weight("w_b_q"), deps=after_sends())
    dfl_t, d_b_f = decay_bwd(fl_t, b_f_col, ddrow.reshape(nh, t))
    dhk = jnp.concatenate([dk, dv, dfl_t.T.astype(BF16)], axis=1)
    dhk = dhk.reshape(t, N_DEV, -1).transpose(1, 0, 2)
    send_grad("w_kvf", mm_tn("d_kvf_w", u_kv, dhk, N_DEV))
    du_kv = mm_nt("d_kvf_x", dhk, weight("w_kvf"), deps=after_sends())
    (dh,), (d_kv_norm, d_mix1) = rowwise("d_ple_norms", _two_norms_bwd, [h3, du_kv, du1, dh],
                                         [row(kv_norm), row(mix_norm[1])])
    (dgpre, dpu), _ = rowwise("d_ple0", _ple_bwd, [s0["gpre"], s0["pu"], dh])
    dh, dh_bf, d_ple0, d_mlp0 = mlp_ple_bwd(0, s0, dh, dgpre, dpu)
    send_grad("w_a_out", mm_tn("d_a_out_w", og, dh_bf, 1))
    dog = mm_nt("d_a_out_x", dh_bf, weight("w_a_out"), deps=after_sends())
    dz4, d_lgt, d_hg = hgrn_bwd(z, lgt, a_head_gain, states, dog)
    dz = dz4.transpose(1, 0, 2).reshape(t, 4 * d)
    send_grad("w_a_in", mm_tn("d_a_in_w", u0, dz, N_DEV))
    du0 = mm_nt("d_a_in_x", dz, weight("w_a_in"), deps=after_sends())
    (dx, _), (d_mix0,) = rowwise("d_norm_mix0", _norm_bwd, [x2, du0, dh], [row(mix_norm[0])])

    new = {}
    last = [dx]

    def update(name, parts, w, m, v):
        shp = w.shape
        w3, m3, v3 = (a.reshape(len(parts), -1, shp[-1]) for a in (w, m, v))
        new[name] = tuple(a.reshape(shp) for a in adamw_reduce("adamw_" + name, parts, w3, m3, v3))
        last[0] = new[name][0]

    def receive_update(name, layers, w, m, v):
        parts = {}
        for sfx in layers:
            g, land = exchange_wait(f"rs_wait_{name}{sfx}", sent[name + sfx], last[0], False)
            parts[sfx] = lax.dynamic_update_slice(land, lax.dynamic_slice_in_dim(g, me, 1, 0), (me, 0, 0))
        update(name, [parts[sfx] for sfx in sorted(layers)], w, m, v)

    both = ("1", "0")
    receive_update("w_b_out", ("",), w_b_out, m_w_b_out, v_w_b_out)
    receive_update("w_b_q", ("",), w_b_q, m_w_b_q, v_w_b_q)
    receive_update("w_kvf", ("",), w_kvf, m_w_kvf, v_w_kvf)
    receive_update("w_ple_gate", both, w_ple_gate, m_w_ple_gate, v_w_ple_gate)
    receive_update("w_ple_up", both, w_ple_up, m_w_ple_up, v_w_ple_up)
    receive_update("w_mlp_down", both, w_mlp_down, m_w_mlp_down, v_w_mlp_down)
    receive_update("w_mlp_up", both, w_mlp_up, m_w_mlp_up, v_w_mlp_up)
    receive_update("w_a_out", ("",), w_a_out, m_w_a_out, v_w_a_out)
    receive_update("w_a_in", ("",), w_a_in, m_w_a_in, v_w_a_in)

    small = dict(mix_norm=jnp.concatenate([d_mix0, d_mix1]), mlp_norm=jnp.concatenate([d_mlp0, d_mlp1]),
                 ple_norm=jnp.concatenate([d_ple0, d_ple1]), a_head_gain=d_hg, kv_norm=d_kv_norm.reshape(d),
                 b_f=d_b_f.reshape(nh), final_norm=d_final.reshape(d))
    small_w = dict(mix_norm=(mix_norm, m_mix_norm, v_mix_norm), mlp_norm=(mlp_norm, m_mlp_norm, v_mlp_norm),
                   ple_norm=(ple_norm, m_ple_norm, v_ple_norm),
                   a_head_gain=(a_head_gain, m_a_head_gain, v_a_head_gain), kv_norm=(kv_norm, m_kv_norm, v_kv_norm),
                   b_f=(b_f, m_b_f, v_b_f), final_norm=(final_norm, m_final_norm, v_final_norm))
    names = list(small)
    packed = _pack_rows([d_lgt] + [small[n] for n in names])
    everyone = all_gather("ag_small_grads", packed, deps=(last[0],))
    n_lgt_rows = d_lgt.size // 128
    lgt_parts = everyone[:, :n_lgt_rows].reshape(N_DEV, 2, d)
    lgt_parts = lax.dynamic_slice_in_dim(lgt_parts, me * a_lb_logits.shape[1], a_lb_logits.shape[1], axis=2)
    update("a_lb_logits", [lgt_parts], a_lb_logits, m_a_lb_logits, v_a_lb_logits)
    rest = everyone[:, n_lgt_rows:]
    like = [small_w[n][0] for n in names]
    packed_w, packed_m, packed_v = (_pack_rows([small_w[n][j] for n in names], rest.shape[1])[None] for j in range(3))
    outs = adamw_reduce("adamw_small", [rest], packed_w, packed_m, packed_v)
    unpacked = [_unpack_rows(a, like) for a in outs]
    for j, n in enumerate(names):
        new[n] = tuple(unpacked[q][j] for q in range(4))

    order = ["mix_norm", "mlp_norm", "ple_norm", "w_a_in", "a_lb_logits", "a_head_gain", "w_a_out", "kv_norm",
             "w_kvf", "b_f", "w_b_q", "w_b_out", "w_mlp_up", "w_mlp_down", "w_ple_gate", "w_ple_up", "final_norm"]
    result = [loss, dx.reshape(x.shape)]
    for j in range(4):
        result += [new[n][j] for n in order]
    return tuple(result)
```

```python
import functools

import jax
import jax.numpy as jnp
from jax import lax
from jax.experimental import pallas as pl
from jax.experimental.pallas import tpu as pltpu

F32 = jnp.float32
BF16 = jnp.bfloat16
HEAD_DIM = 128
CHUNK = 16
TILE = 128
NORM_EPS = 1e-6
N_DEV = 8
MESH_AXES = ("x", "y", "c")
VMEM_LIMIT_BYTES = 48 * 1024 * 1024
ROW_TILE_BYTES = 1024 * 1024
LR, B1, B2, ADAM_EPS, WD, STEP = 0.001, 0.9, 0.999, 1e-08, 0.01, 10
NEG_BIG = -1e30

NN = (((1,), (0,)), ((), ()))
NT = (((1,), (1,)), ((), ()))
TN = (((0,), (0,)), ((), ()))


def _params(semantics):
    return pltpu.CompilerParams(dimension_semantics=semantics, vmem_limit_bytes=VMEM_LIMIT_BYTES)


def _tile(n, prefs):
    for p in prefs:
        if n % p == 0:
            return p
    return n


def _row_tile(rows, limit):
    for cand in (2048, 1024, 512, 256, 128, 64, 32, 16):
        if cand <= limit and rows % cand == 0:
            return cand
    return rows


def _mm_call(name, a, b, dims, grid, a_spec, b_spec, o_spec, o_shape, acc_shape, k_axes, out_dtype, deps=()):
    nk = 1
    for ax in k_axes:
        nk *= grid[ax]

    def one_step(a_ref, b_ref, *rest):
        o_ref = rest[-1]
        o_ref[...] = lax.dot_general(a_ref[...], b_ref[...], dims, preferred_element_type=F32).astype(o_ref.dtype)

    def accumulate(a_ref, b_ref, *rest):
        o_ref, acc_ref = rest[-2:]
        k = 0
        for ax in k_axes:
            k = k * grid[ax] + pl.program_id(ax)
        part = lax.dot_general(a_ref[...], b_ref[...], dims, preferred_element_type=F32)

        @pl.when(k == 0)
        def _():
            acc_ref[...] = part

        @pl.when((k > 0) & (k < nk - 1))
        def _():
            acc_ref[...] += part

        @pl.when(k == nk - 1)
        def _():
            o_ref[...] = (acc_ref[...] + part).astype(o_ref.dtype)

    sem = tuple("arbitrary" if ax in k_axes else "parallel" for ax in range(len(grid)))
    return pl.pallas_call(
        one_step if nk == 1 else accumulate, name=name, grid=grid,
        in_specs=[a_spec, b_spec] + [pl.BlockSpec(memory_space=pl.ANY)] * len(deps),
        out_specs=o_spec, out_shape=jax.ShapeDtypeStruct(o_shape, out_dtype),
        scratch_shapes=[] if nk == 1 else [pltpu.VMEM(acc_shape, F32)], compiler_params=_params(sem),
    )(a, b, *deps)


def mm_nn(name, a, b3, out_dtype=F32, out3=False, deps=()):
    m, k = a.shape
    g, _, n = b3.shape
    tm, tk = _tile(m, (1024, 512, 256)), _tile(k, (2048, 1024, 512, 256))
    tn = n if out3 else _tile(n, (1024, 512, 256, 128))
    nj = n // tn
    grid = (m // tm, g, nj, k // tk)
    a_spec = pl.BlockSpec((tm, tk), lambda i, gg, j, kk: (i, kk))
    b_spec = pl.BlockSpec((None, tk, tn), lambda i, gg, j, kk: (gg, kk, j))
    if out3:
        o_spec = pl.BlockSpec((None, tm, tn), lambda i, gg, j, kk: (gg, i, j))
        o_shape = (g, m, n)
    else:
        o_spec = pl.BlockSpec((tm, tn), lambda i, gg, j, kk: (i, gg * nj + j))
        o_shape = (m, g * n)
    return _mm_call(name, a, b3, NN, grid, a_spec, b_spec, o_spec, o_shape, (tm, tn), (3,), out_dtype, deps)


def mm_nt(name, a, b3, out_dtype=F32, deps=()):
    g, k, n = b3.shape
    a3 = a.ndim == 3
    m = a.shape[1] if a3 else a.shape[0]
    tm, tko = _tile(m, (1024, 512, 256)), _tile(k, (1024, 512, 256))
    tc = n if a3 else _tile(n, (2048, 1024, 512, 256, 128))
    nc = n // tc
    grid = (m // tm, k // tko, g, nc)
    if a3:
        a_spec = pl.BlockSpec((None, tm, tc), lambda i, j, gg, c: (gg, i, c))
    else:
        a_spec = pl.BlockSpec((tm, tc), lambda i, j, gg, c: (i, gg * nc + c))
    b_spec = pl.BlockSpec((None, tko, tc), lambda i, j, gg, c: (gg, j, c))
    o_spec = pl.BlockSpec((tm, tko), lambda i, j, gg, c: (i, j))
    return _mm_call(name, a, b3, NT, grid, a_spec, b_spec, o_spec, (m, k), (tm, tko), (2, 3), out_dtype, deps)


def mm_tn(name, a, b, g, out_dtype=BF16, deps=()):
    t, k = a.shape
    b3 = b.ndim == 3
    n = b.shape[2] if b3 else b.shape[1] // g
    tm = _tile(k, (1024, 512, 256))
    tn = n if b3 else _tile(n, (1024, 512, 256, 128))
    tt = _tile(t, (2048, 1024, 512, 256))
    nj = n // tn
    grid = (g, k // tm, nj, t // tt)
    a_spec = pl.BlockSpec((tt, tm), lambda gg, i, j, s: (s, i))
    if b3:
        b_spec = pl.BlockSpec((None, tt, tn), lambda gg, i, j, s: (gg, s, j))
    else:
        b_spec = pl.BlockSpec((tt, tn), lambda gg, i, j, s: (s, gg * nj + j))
    o_spec = pl.BlockSpec((None, tm, tn), lambda gg, i, j, s: (gg, i, j))
    return _mm_call(name, a, b, TN, grid, a_spec, b_spec, o_spec, (g, k, n), (tm, tn), (3,), out_dtype, deps)


def rowwise(name, fn, rows, vecs=()):
    t = rows[0].shape[0]
    wmax = max(r.shape[1] for r in rows)
    tr = _row_tile(t, ROW_TILE_BYTES // (4 * wmax))
    row_s = [jax.ShapeDtypeStruct((tr, r.shape[1]), r.dtype) for r in rows]
    vec_s = [jax.ShapeDtypeStruct(v.shape, v.dtype) for v in vecs]
    out_rows_s, out_sums_s = jax.eval_shape(fn, *row_s, *vec_s)
    n_in, n_r = len(rows) + len(vecs), len(out_rows_s)

    def body(*refs):
        i = pl.program_id(0)
        o_rows, o_sums = fn(*[r[...] for r in refs[:n_in]])
        for ref, val in zip(refs[n_in:n_in + n_r], o_rows):
            ref[...] = val

        if out_sums_s:
            @pl.when(i == 0)
            def _():
                for ref in refs[n_in + n_r:]:
                    ref[...] = jnp.zeros_like(ref)

            for ref, val in zip(refs[n_in + n_r:], o_sums):
                ref[...] += val

    in_specs = [pl.BlockSpec((tr, r.shape[1]), lambda i: (i, 0)) for r in rows]
    in_specs += [pl.BlockSpec(v.shape, lambda i: (0, 0)) for v in vecs]
    out_specs = [pl.BlockSpec((tr, s.shape[1]), lambda i: (i, 0)) for s in out_rows_s]
    out_specs += [pl.BlockSpec(s.shape, lambda i: (0, 0)) for s in out_sums_s]
    out_shape = [jax.ShapeDtypeStruct((t, s.shape[1]), s.dtype) for s in out_rows_s]
    out_shape += [jax.ShapeDtypeStruct(s.shape, s.dtype) for s in out_sums_s]
    outs = pl.pallas_call(
        body, name=name, grid=(t // tr,), in_specs=in_specs, out_specs=out_specs, out_shape=out_shape,
        compiler_params=_params(("arbitrary",)),
    )(*rows, *vecs)
    return outs[:n_r], outs[n_r:]


def _rms(x, gain):
    return x * lax.rsqrt(jnp.mean(x * x, axis=-1, keepdims=True) + NORM_EPS) * gain


def _norm_fwd(x, gain):
    return (_rms(x, gain).astype(BF16),), ()


def _add_norm_fwd(h, a, gain):
    h = h + a
    return (h, _rms(h, gain).astype(BF16)), ()


def _relu2_fwd(pre):
    r = jnp.maximum(pre, 0.0)
    return ((r * r).astype(BF16),), ()


def _ple(h, gpre, pu):
    return h + pu * jax.nn.sigmoid(gpre)


def _ple_two_norms_fwd(h, gpre, pu, gain_a, gain_b):
    h = _ple(h, gpre, pu)
    return (h, _rms(h, gain_a).astype(BF16), _rms(h, gain_b).astype(BF16)), ()


def _tail_fwd_bwd(h, gpre, pu, target, gain):
    def row_loss(h, gpre, pu, gain):
        y = _rms(_ple(h, gpre, pu), gain)
        return 0.5 * jnp.mean(jnp.square(y - target), axis=-1, keepdims=True)

    loss, vjp = jax.vjp(row_loss, h, gpre, pu, gain)
    dh, dgpre, dpu, dgain = vjp(jnp.ones_like(loss))
    loss = jnp.broadcast_to(jnp.sum(loss, axis=0, keepdims=True), (1, 128))
    return (dh, dgpre.astype(BF16), dpu.astype(BF16)), (dgain, loss)


def _ple_bwd(gpre, pu, dh):
    _, vjp = jax.vjp(lambda g, u: pu_times_gate(g, u), gpre, pu)
    dgpre, dpu = vjp(dh)
    return (dgpre.astype(BF16), dpu.astype(BF16)), ()


def pu_times_gate(gpre, pu):
    return pu * jax.nn.sigmoid(gpre)


def _norm_bwd(h, du, dh_in, gain):
    _, vjp = jax.vjp(_rms, h, gain)
    dh, dgain = vjp(du)
    dh = dh_in + dh
    return (dh, dh.astype(BF16)), (dgain,)


def _two_norms_bwd(h, du_a, du_b, dh_in, gain_a, gain_b):
    _, vjp = jax.vjp(lambda h, ga, gb: (_rms(h, ga), _rms(h, gb)), h, gain_a, gain_b)
    dh, dga, dgb = vjp((du_a, du_b))
    return (dh_in + dh,), (dga, dgb)


def _relu2_bwd(pre, dact):
    return ((dact * 2.0 * jnp.maximum(pre, 0.0)).astype(BF16),), ()


def _bf16_dot(dims_fwd, dims_da, dims_db, swap_da, swap_db):
    @jax.custom_vjp
    def dot(a, b):
        return lax.dot_general(a.astype(BF16), b.astype(BF16), dims_fwd, preferred_element_type=F32)

    def fwd(a, b):
        return dot(a, b), (a, b)

    def bwd(res, ct):
        a, b = res
        ct, a, b = ct.astype(BF16), a.astype(BF16), b.astype(BF16)
        da = lax.dot_general(*((b, ct) if swap_da else (ct, b)), dims_da, preferred_element_type=F32)
        db = lax.dot_general(*((ct, a) if swap_db else (a, ct)), dims_db, preferred_element_type=F32)
        return da, db

    dot.defvjp(fwd, bwd)
    return dot


_dot_nn = _bf16_dot(NN, NT, TN, False, False)
_dot_nt = _bf16_dot(NT, NN, TN, False, True)
_dot_tn = _bf16_dot(TN, NT, NN, True, False)


def _chunk_masks(transposed):
    r = lax.broadcasted_iota(jnp.int32, (TILE, TILE), 0)
    c = lax.broadcasted_iota(jnp.int32, (TILE, TILE), 1)
    same = (r // CHUNK) == (c // CHUNK)
    causal = same & ((r <= c) if transposed else (c <= r))
    return causal, same


def _chunk_scan(x, reverse):
    pos = lax.broadcasted_iota(jnp.int32, x.shape, 0) % CHUNK
    step = 1
    while step < CHUNK:
        if reverse:
            x = x + jnp.where(pos < CHUNK - step, pltpu.roll(x, x.shape[0] - step, axis=0), 0.0)
        else:
            x = x + jnp.where(pos >= step, pltpu.roll(x, step, axis=0), 0.0)
        step *= 2
    return x


def _chunk_total(x):
    return _chunk_scan(x, False) + _chunk_scan(x, True) - x


@jax.custom_vjp
def _chunk_sums(x):
    return _chunk_scan(x, False), _chunk_total(x)


def _chunk_sums_fwd(x):
    return _chunk_sums(x), None


def _chunk_sums_bwd(_, ct):
    return (_chunk_scan(ct[0], True) + _chunk_total(ct[1]),)


_chunk_sums.defvjp(_chunk_sums_fwd, _chunk_sums_bwd)


def _hgrn_tile(q, f, i, g, lgt, hg, st):
    d = q.shape[1]
    l0, l1 = lgt[0:1], lgt[1:2]
    mx = jnp.maximum(l0, l1)
    e0, e1 = jnp.exp(l0 - mx), jnp.exp(l1 - mx)
    lb = e0 / (e0 + e1)
    fg = lb + (1.0 - lb) * jax.nn.sigmoid(f)
    k = 1.0 - fg
    causal, _ = _chunk_masks(False)
    b, b_last = _chunk_sums(jnp.log(fg))
    q_in = q * jax.nn.sigmoid(q) * (d ** -0.5) * jnp.exp(b)
    k_in = k * jnp.exp(-b)
    k_end = k * jnp.exp(b_last - b)
    att = jnp.where(causal, _dot_nt(q_in, k_in), 0.0)
    o_intra = _dot_nn(att, i)
    outs = []
    for n in range(TILE // CHUNK):
        rows = slice(n * CHUNK, (n + 1) * CHUNK)
        outs.append(o_intra[rows] + _dot_nt(q_in[rows], st))
        decay = jnp.exp(jnp.mean(b_last[rows], axis=0, keepdims=True))
        st = st * decay + _dot_tn(i[rows], k_end[rows])
    o = jnp.concatenate(outs, axis=0)
    o = o * lax.rsqrt(jnp.mean(o * o, axis=-1, keepdims=True) + NORM_EPS) * hg
    return o * (g * jax.nn.sigmoid(g)), st


def hgrn_fwd(z, lgt, hg):
    t, d4 = z.shape
    d = d4 // 4
    nh, nt = d // HEAD_DIM, t // TILE

    def body(q_ref, f_ref, i_ref, g_ref, lgt_ref, hg_ref, o_ref, st_out_ref, st_ref):
        tt = pl.program_id(1)

        @pl.when(tt == 0)
        def _():
            st_ref[...] = jnp.zeros_like(st_ref)

        st = st_ref[...]
        st_out_ref[...] = st
        o, st = _hgrn_tile(q_ref[...], f_ref[...], i_ref[...], g_ref[...], lgt_ref[...], hg_ref[...], st)
        o_ref[...] = o.astype(o_ref.dtype)
        st_ref[...] = st

    def part(p):
        return pl.BlockSpec((TILE, HEAD_DIM), lambda h, tt: (tt, p * nh + h))

    return pl.pallas_call(
        body, name="hgrn_fwd", grid=(nh, nt),
        in_specs=[part(0), part(1), part(2), part(3),
                  pl.BlockSpec((2, HEAD_DIM), lambda h, tt: (0, h)),
                  pl.BlockSpec((1, HEAD_DIM), lambda h, tt: (0, 0))],
        out_specs=[pl.BlockSpec((TILE, HEAD_DIM), lambda h, tt: (tt, h)),
                   pl.BlockSpec((None, None, HEAD_DIM, HEAD_DIM), lambda h, tt: (h, tt, 0, 0))],
        out_shape=[jax.ShapeDtypeStruct((t, d), BF16),
                   jax.ShapeDtypeStruct((nh, nt, HEAD_DIM, HEAD_DIM), F32)],
        scratch_shapes=[pltpu.VMEM((HEAD_DIM, HEAD_DIM), F32)],
        compiler_params=_params(("parallel", "arbitrary")),
    )(z, z, z, z, lgt, hg)


def hgrn_bwd(z, lgt, hg, states, dout):
    t, d4 = z.shape
    d = d4 // 4
    nh, nt = d // HEAD_DIM, t // TILE

    def body(q_ref, f_ref, i_ref, g_ref, lgt_ref, hg_ref, st_in_ref, do_ref, dz_ref, dlgt_ref, dhg_ref, dst_ref):
        h, tt = pl.program_id(0), pl.program_id(1)

        @pl.when(tt == 0)
        def _():
            dst_ref[...] = jnp.zeros_like(dst_ref)
            dlgt_ref[...] = jnp.zeros_like(dlgt_ref)

        @pl.when((tt == 0) & (h == 0))
        def _():
            dhg_ref[...] = jnp.zeros_like(dhg_ref)

        _, vjp = jax.vjp(_hgrn_tile, q_ref[...], f_ref[...], i_ref[...], g_ref[...], lgt_ref[...], hg_ref[...],
                         st_in_ref[...])
        dq, df, di, dg, dlgt, dhg, dst = vjp((do_ref[...], dst_ref[...]))
        dz_ref[0] = dq.astype(dz_ref.dtype)
        dz_ref[1] = df.astype(dz_ref.dtype)
        dz_ref[2] = di.astype(dz_ref.dtype)
        dz_ref[3] = dg.astype(dz_ref.dtype)
        dlgt_ref[...] += dlgt
        dhg_ref[...] += dhg
        dst_ref[...] = dst

    def part(p):
        return pl.BlockSpec((TILE, HEAD_DIM), lambda h, tt: (nt - 1 - tt, p * nh + h))

    return pl.pallas_call(
        body, name="hgrn_bwd", grid=(nh, nt),
        in_specs=[part(0), part(1), part(2), part(3),
                  pl.BlockSpec((2, HEAD_DIM), lambda h, tt: (0, h)),
                  pl.BlockSpec((1, HEAD_DIM), lambda h, tt: (0, 0)),
                  pl.BlockSpec((None, None, HEAD_DIM, HEAD_DIM), lambda h, tt: (h, nt - 1 - tt, 0, 0)),
                  pl.BlockSpec((TILE, HEAD_DIM), lambda h, tt: (nt - 1 - tt, h))],
        out_specs=[pl.BlockSpec((4, TILE, HEAD_DIM), lambda h, tt: (0, nt - 1 - tt, h)),
                   pl.BlockSpec((2, HEAD_DIM), lambda h, tt: (0, h)),
                   pl.BlockSpec((1, HEAD_DIM), lambda h, tt: (0, 0))],
        out_shape=[jax.ShapeDtypeStruct((4, t, d), BF16),
                   jax.ShapeDtypeStruct((2, d), F32),
                   jax.ShapeDtypeStruct((1, HEAD_DIM), F32)],
        scratch_shapes=[pltpu.VMEM((HEAD_DIM, HEAD_DIM), F32)],
        compiler_params=_params(("arbitrary", "arbitrary")),
    )(z, z, z, z, lgt, hg, states, dout)


def _log_sigmoid(x):
    return jnp.minimum(x, 0.0) - jnp.log(1.0 + jnp.exp(-jnp.abs(x)))


def decay_fwd(fl_t, b_f):
    nh, t = fl_t.shape

    def body(fl_ref, b_ref, out_ref):
        r = lax.broadcasted_iota(jnp.int32, (128, 128), 0)
        c = lax.broadcasted_iota(jnp.int32, (128, 128), 1)
        upper = (r <= c).astype(F32)
        carry = jnp.zeros((nh, 1), F32)
        for j in range(t // 128):
            cols = slice(j * 128, (j + 1) * 128)
            ls = _log_sigmoid(fl_ref[:, cols] + b_ref[...])
            out_ref[:, cols] = carry + jnp.dot(ls, upper, precision=lax.Precision.HIGHEST,
                                               preferred_element_type=F32)
            carry = carry + jnp.sum(ls, axis=1, keepdims=True)

    return pl.pallas_call(body, name="decay_fwd", out_shape=jax.ShapeDtypeStruct((nh, t), F32),
                          compiler_params=_params(None))(fl_t, b_f)


def decay_bwd(fl_t, b_f, ddcum):
    nh, t = fl_t.shape

    def body(fl_ref, b_ref, dd_ref, dfl_ref, db_ref):
        r = lax.broadcasted_iota(jnp.int32, (128, 128), 0)
        c = lax.broadcasted_iota(jnp.int32, (128, 128), 1)
        lower = (r >= c).astype(F32)
        carry = jnp.zeros((nh, 1), F32)
        db = jnp.zeros((nh, 1), F32)
        for j in reversed(range(t // 128)):
            cols = slice(j * 128, (j + 1) * 128)
            dd = dd_ref[:, cols]
            dls = carry + jnp.dot(dd, lower, precision=lax.Precision.HIGHEST, preferred_element_type=F32)
            carry = carry + jnp.sum(dd, axis=1, keepdims=True)
            dfl = dls * jax.nn.sigmoid(-(fl_ref[:, cols] + b_ref[...]))
            dfl_ref[:, cols] = dfl
            db = db + jnp.sum(dfl, axis=1, keepdims=True)
        db_ref[...] = db

    return pl.pallas_call(body, name="decay_bwd",
                          out_shape=[jax.ShapeDtypeStruct((nh, t), F32), jax.ShapeDtypeStruct((nh, 1), F32)],
                          compiler_params=_params(None))(fl_t, b_f, ddcum)


def _attn_probs_logits(q_ref, k_ref, dcol_ref, drow_ref, i, tq, t):
    qs = (q_ref[...] * (HEAD_DIM ** -0.5)).astype(BF16)
    s = lax.dot_general(qs, k_ref[...], NT, preferred_element_type=F32)
    s = s + dcol_ref[...] - drow_ref[...]
    row = i * tq + lax.broadcasted_iota(jnp.int32, (tq, t), 0)
    col = lax.broadcasted_iota(jnp.int32, (tq, t), 1)
    mask = col <= row
    return qs, jnp.where(mask, s, NEG_BIG), mask


def attn_fwd(q, k, v, dcol, drow):
    t, d = q.shape
    nh = d // HEAD_DIM
    tq = _tile(t, (256, 128))

    def body(q_ref, k_ref, v_ref, dcol_ref, drow_ref, o_ref, lse_ref):
        i = pl.program_id(1)
        _, s, _ = _attn_probs_logits(q_ref, k_ref, dcol_ref, drow_ref, i, tq, t)
        m = jnp.max(s, axis=1, keepdims=True)
        p = jnp.exp(s - m)
        l = jnp.sum(p, axis=1, keepdims=True)
        o = jnp.dot(p.astype(BF16), v_ref[...], preferred_element_type=F32)
        o_ref[...] = (o / l).astype(o_ref.dtype)
        lse_ref[...] = m + jnp.log(l)

    return pl.pallas_call(
        body, name="attn_fwd", grid=(nh, t // tq),
        in_specs=[pl.BlockSpec((tq, HEAD_DIM), lambda h, i: (i, h)),
                  pl.BlockSpec((t, HEAD_DIM), lambda h, i: (0, h)),
                  pl.BlockSpec((t, HEAD_DIM), lambda h, i: (0, h)),
                  pl.BlockSpec((None, tq, 1), lambda h, i: (h, i, 0)),
                  pl.BlockSpec((None, 1, t), lambda h, i: (h, 0, 0))],
        out_specs=[pl.BlockSpec((tq, HEAD_DIM), lambda h, i: (i, h)),
                   pl.BlockSpec((None, tq, 1), lambda h, i: (h, i, 0))],
        out_shape=[jax.ShapeDtypeStruct((t, d), BF16), jax.ShapeDtypeStruct((nh, t, 1), F32)],
        compiler_params=_params(("parallel", "parallel")),
    )(q, k, v, dcol, drow)


def attn_bwd(q, k, v, dcol, drow, lse, do):
    t, d = q.shape
    nh = d // HEAD_DIM
    tq = _tile(t, (256, 128))
    nq = t // tq

    def body(q_ref, k_ref, v_ref, dcol_ref, drow_ref, lse_ref, do_ref,
             dq_ref, dk_ref, dv_ref, ddrow_ref, dk_acc, dv_acc):
        i = pl.program_id(1)

        @pl.when(i == 0)
        def _():
            dk_acc[...] = jnp.zeros_like(dk_acc)
            dv_acc[...] = jnp.zeros_like(dv_acc)
            ddrow_ref[...] = jnp.zeros_like(ddrow_ref)

        qs, s, mask = _attn_probs_logits(q_ref, k_ref, dcol_ref, drow_ref, i, tq, t)
        p = jnp.where(mask, jnp.exp(s - lse_ref[...]), 0.0)
        do = do_ref[...]
        dp = lax.dot_general(do, v_ref[...], NT, preferred_element_type=F32)
        ds = p * (dp - jnp.sum(p * dp, axis=1, keepdims=True))
        dsb = ds.astype(BF16)
        dq = jnp.dot(dsb, k_ref[...], preferred_element_type=F32) * (HEAD_DIM ** -0.5)
        dq_ref[...] = dq.astype(dq_ref.dtype)
        dk_acc[...] += lax.dot_general(dsb, qs, TN, preferred_element_type=F32)
        dv_acc[...] += lax.dot_general(p.astype(BF16), do, TN, preferred_element_type=F32)
        ddrow_ref[...] -= jnp.sum(ds, axis=0, keepdims=True)

        @pl.when(i == nq - 1)
        def _():
            dk_ref[...] = dk_acc[...].astype(dk_ref.dtype)
            dv_ref[...] = dv_acc[...].astype(dv_ref.dtype)

    tile = pl.BlockSpec((tq, HEAD_DIM), lambda h, i: (i, h))
    full = pl.BlockSpec((t, HEAD_DIM), lambda h, i: (0, h))
    col = pl.BlockSpec((None, tq, 1), lambda h, i: (h, i, 0))
    rowv = pl.BlockSpec((None, 1, t), lambda h, i: (h, 0, 0))
    return pl.pallas_call(
        body, name="attn_bwd", grid=(nh, nq),
        in_specs=[tile, full, full, col, rowv, col, tile],
        out_specs=[tile, full, full, rowv],
        out_shape=[jax.ShapeDtypeStruct((t, d), BF16), jax.ShapeDtypeStruct((t, d), BF16),
                   jax.ShapeDtypeStruct((t, d), BF16), jax.ShapeDtypeStruct((nh, 1, t), F32)],
        scratch_shapes=[pltpu.VMEM((t, HEAD_DIM), F32), pltpu.VMEM((t, HEAD_DIM), F32)],
        compiler_params=_params(("parallel", "arbitrary")),
    )(q, k, v, dcol, drow, lse, do)


def _my_index():
    return (lax.axis_index("x") * 2 + lax.axis_index("y")) * 2 + lax.axis_index("c")


def _exchange(name, src, gather, deps=()):
    shape = src.shape if gather else src.shape[1:]

    def body(src_ref, *rest):
        out_ref, send_sems, recv_sems, local_sem = rest[len(deps):]
        x, y, c = (lax.axis_index(a) for a in MESH_AXES)
        me = (x * 2 + y) * 2 + c
        mine = src_ref if gather else src_ref.at[me]
        local = pltpu.make_async_copy(mine, out_ref.at[me], local_sem)
        local.start()
        copies = []
        for dlt in range(1, N_DEV):
            dx, dy, dc = dlt // 4, (dlt // 2) % 2, dlt % 2
            px, py, pc = x ^ dx, y ^ dy, c ^ dc
            peer = (px * 2 + py) * 2 + pc
            copies.append(pltpu.make_async_remote_copy(
                src_ref=src_ref if gather else src_ref.at[peer], dst_ref=out_ref.at[me],
                send_sem=send_sems.at[dlt - 1], recv_sem=recv_sems.at[dlt - 1],
                device_id=(px, py, pc), device_id_type=pl.DeviceIdType.MESH))
        for cp in copies:
            cp.start()
        for cp in copies:
            cp.wait_recv()
        for cp in copies:
            cp.wait_send()
        local.wait()

    return pl.pallas_call(
        body, name=name, out_shape=jax.ShapeDtypeStruct((N_DEV,) + tuple(shape), src.dtype),
        in_specs=[pl.BlockSpec(memory_space=pl.ANY)] * (1 + len(deps)), out_specs=pl.BlockSpec(memory_space=pl.ANY),
        scratch_shapes=[pltpu.SemaphoreType.DMA((N_DEV - 1,)), pltpu.SemaphoreType.DMA((N_DEV - 1,)),
                        pltpu.SemaphoreType.DMA],
        compiler_params=pltpu.CompilerParams(has_side_effects=True),
    )(src, *deps)


def all_gather(name, x, deps=()):
    return _exchange(name, x, True, deps)


_HBM = pl.BlockSpec(memory_space=pltpu.HBM)
_SEM = pl.BlockSpec(memory_space=pltpu.SEMAPHORE)
_DATAFLOW = pltpu.SideEffectType.DATAFLOW_SIDE_EFFECTING


def _peer_copies(src_ref, land_ref, send_sems, recv_sems, gather):
    x, y, c = (lax.axis_index(a) for a in MESH_AXES)
    me = (x * 2 + y) * 2 + c
    copies = []
    for dlt in range(1, N_DEV):
        px, py, pc = x ^ (dlt // 4), y ^ ((dlt // 2) % 2), c ^ (dlt % 2)
        peer = (px * 2 + py) * 2 + pc
        copies.append(pltpu.make_async_remote_copy(
            src_ref=src_ref if gather else src_ref.at[peer], dst_ref=land_ref.at[me],
            send_sem=send_sems.at[dlt - 1], recv_sem=recv_sems.at[dlt - 1],
            device_id=(px, py, pc), device_id_type=pl.DeviceIdType.MESH))
    return copies


def exchange_start(name, srcs, gather):
    n = len(srcs)
    lands = [lax.empty((N_DEV,) + tuple(s.shape if gather else s.shape[1:]), s.dtype) for s in srcs]

    def body(*refs):
        src_refs, land_refs = refs[:n], refs[n:2 * n]
        send_sems, recv_sems = refs[2 * n:3 * n], refs[3 * n:4 * n]
        token = refs[-1]
        for j in range(n):
            for cp in _peer_copies(src_refs[j], land_refs[j], send_sems[j], recv_sems[j], gather):
                cp.start()
        token[...] = jnp.zeros_like(token)

    sems = [pltpu.SemaphoreType.DMA((N_DEV - 1,))] * (2 * n)
    thru = [pltpu.HBM(a.shape, a.dtype) for a in list(srcs) + lands]
    outs = pl.pallas_call(
        body, name=name, out_shape=tuple(sems + thru + [jax.ShapeDtypeStruct((8, 128), F32)]),
        in_specs=[_HBM] * (2 * n), out_specs=tuple([_SEM] * (2 * n) + [_HBM] * (2 * n) + [pl.BlockSpec(memory_space=pltpu.VMEM)]),
        input_output_aliases={j: 2 * n + j for j in range(2 * n)},
        compiler_params=pltpu.CompilerParams(has_side_effects=_DATAFLOW),
    )(*[pltpu.with_memory_space_constraint(a, pltpu.HBM) for a in list(srcs) + lands])
    handles = [(outs[j], outs[n + j], outs[2 * n + j], outs[3 * n + j]) for j in range(n)]
    return handles, outs[-1]


def exchange_wait(name, handle, after, gather):
    send_sems, recv_sems, src, land = handle

    def body(src_ref, land_ref, send_ref, recv_ref, after_ref, src_out, land_out):
        for cp in _peer_copies(src_ref, land_ref, send_ref, recv_ref, gather):
            cp.wait_send()
            cp.wait_recv()

    return pl.pallas_call(
        body, name=name, out_shape=(pltpu.HBM(src.shape, src.dtype), pltpu.HBM(land.shape, land.dtype)),
        in_specs=[_HBM, _HBM, _SEM, _SEM, pl.BlockSpec(memory_space=pl.ANY)], out_specs=(_HBM, _HBM),
        input_output_aliases={0: 0, 1: 1},
        compiler_params=pltpu.CompilerParams(has_side_effects=_DATAFLOW),
    )(src, land, send_sems, recv_sems, after)


N_OTHER_CHIPS = 3


def _two_level_places():
    x, y, c = (lax.axis_index(a) for a in MESH_AXES)
    return (x, y, c), (x * 2 + y) * 2 + c, (x, y, 1 - c), [(1 - x, y), (x, 1 - y), (1 - x, 1 - y)]


def _first_copies(src_ref, land_ref, send_sems, recv_sems):
    (x, y, c), me, other_core, chips = _two_level_places()
    targets = [other_core] + [(cx, cy, c) for cx, cy in chips]
    return [pltpu.make_async_remote_copy(
        src_ref=src_ref, dst_ref=land_ref.at[me], send_sem=send_sems.at[k], recv_sem=recv_sems.at[k],
        device_id=to, device_id_type=pl.DeviceIdType.MESH) for k, to in enumerate(targets)]


def _passed_on_copies(land_ref, send_sems, recv_sems):
    (x, y, c), me, other_core, chips = _two_level_places()
    copies = []
    for k, (cx, cy) in enumerate(chips):
        slot = land_ref.at[(cx * 2 + cy) * 2 + c]
        copies.append(pltpu.make_async_remote_copy(
            src_ref=slot, dst_ref=slot, send_sem=send_sems.at[k], recv_sem=recv_sems.at[k],
            device_id=other_core, device_id_type=pl.DeviceIdType.MESH))
    return copies


def gather_start(name, srcs):
    n = len(srcs)
    lands = [lax.empty((N_DEV,) + tuple(s.shape), s.dtype) for s in srcs]

    def body(*refs):
        src_refs, land_refs = refs[:n], refs[n:2 * n]
        send_sems, recv_sems = refs[2 * n:3 * n], refs[3 * n:4 * n]
        for j in range(n):
            for cp in _first_copies(src_refs[j], land_refs[j], send_sems[j], recv_sems[j]):
                cp.start()

    sems = [pltpu.SemaphoreType.DMA((1 + N_OTHER_CHIPS,))] * (2 * n)
    thru = [pltpu.HBM(a.shape, a.dtype) for a in list(srcs) + lands]
    outs = pl.pallas_call(
        body, name=name, out_shape=tuple(sems + thru), in_specs=[_HBM] * (2 * n),
        out_specs=tuple([_SEM] * (2 * n) + [_HBM] * (2 * n)),
        input_output_aliases={j: 2 * n + j for j in range(2 * n)},
        compiler_params=pltpu.CompilerParams(has_side_effects=_DATAFLOW),
    )(*[pltpu.with_memory_space_constraint(a, pltpu.HBM) for a in list(srcs) + lands])
    return [[outs[j], outs[n + j], outs[2 * n + j], outs[3 * n + j]] for j in range(n)]


def gather_pass_on(name, handle, after):
    send_sems, recv_sems, src, land = handle

    def body(src_ref, land_ref, recv_ref, after_ref, land_out, send2, recv2, token):
        arrivals = _first_copies(src_ref, land_ref, recv_ref, recv_ref)
        for k, cp in enumerate(_passed_on_copies(land_ref, send2, recv2)):
            arrivals[1 + k].wait_recv()
            cp.start()
        token[...] = jnp.zeros_like(token)

    sem3 = pltpu.SemaphoreType.DMA((N_OTHER_CHIPS,))
    land, send2, recv2, token = pl.pallas_call(
        body, name=name,
        out_shape=(pltpu.HBM(land.shape, land.dtype), sem3, sem3, jax.ShapeDtypeStruct((8, 128), F32)),
        in_specs=[_HBM, _HBM, _SEM, pl.BlockSpec(memory_space=pl.ANY)],
        out_specs=(_HBM, _SEM, _SEM, pl.BlockSpec(memory_space=pltpu.VMEM)),
        input_output_aliases={1: 0}, compiler_params=pltpu.CompilerParams(has_side_effects=_DATAFLOW),
    )(src, land, recv_sems, after)
    return [send_sems, recv_sems, src, land, send2, recv2], token


def gather_wait(name, handle, after):
    send_sems, recv_sems, src, land, send2, recv2 = handle

    def body(src_ref, land_ref, send_ref, recv_ref, send2_ref, recv2_ref, after_ref, src_out, land_out):
        first = _first_copies(src_ref, land_ref, send_ref, recv_ref)
        for cp in first:
            cp.wait_send()
        first[0].wait_recv()
        for cp in _passed_on_copies(land_ref, send2_ref, recv2_ref):
            cp.wait_send()
            cp.wait_recv()

    return pl.pallas_call(
        body, name=name, out_shape=(pltpu.HBM(src.shape, src.dtype), pltpu.HBM(land.shape, land.dtype)),
        in_specs=[_HBM, _HBM, _SEM, _SEM, _SEM, _SEM, pl.BlockSpec(memory_space=pl.ANY)], out_specs=(_HBM, _HBM),
        input_output_aliases={0: 0, 1: 1}, compiler_params=pltpu.CompilerParams(has_side_effects=_DATAFLOW),
    )(src, land, send_sems, recv_sems, send2, recv2, after)


def adamw_reduce(name, parts, w, m, v):
    nl, r, wd = w.shape
    tr = _row_tile(r, 2 * ROW_TILE_BYTES // (8 * wd))

    def body(*refs):
        p_refs = refs[:nl]
        w_ref, m_ref, v_ref, g_ref, d_ref, nm_ref, nv_ref = refs[nl:]
        layer = pl.program_id(0)
        for j in range(nl):
            @pl.when(layer == j)
            def _(j=j):
                g = p_refs[j][0].astype(F32)
                for dev in range(1, N_DEV):
                    g = g + p_refs[j][dev].astype(F32)
                nm = B1 * m_ref[...] + (1.0 - B1) * g
                nv = B2 * v_ref[...] + (1.0 - B2) * jnp.square(g)
                m_hat = nm / (1.0 - B1 ** STEP)
                v_hat = nv / (1.0 - B2 ** STEP)
                g_ref[...] = g
                d_ref[...] = -LR * (m_hat / (jnp.sqrt(v_hat) + ADAM_EPS) + WD * w_ref[...])
                nm_ref[...] = nm
                nv_ref[...] = nv

    def part_spec(j):
        return pl.BlockSpec((N_DEV, tr, wd), lambda l, i: (0, jnp.where(l == j, i, 0), 0))

    spec = pl.BlockSpec((None, tr, wd), lambda l, i: (l, i, 0))
    return pl.pallas_call(
        body, name=name, grid=(nl, r // tr),
        in_specs=[part_spec(j) for j in range(nl)] + [spec, spec, spec],
        out_specs=[spec] * 4, out_shape=[jax.ShapeDtypeStruct((nl, r, wd), F32)] * 4,
        compiler_params=_params(("arbitrary", "arbitrary")),
    )(*parts, w, m, v)


def _pack_rows(vectors, rows=None):
    flat = jnp.concatenate([a.reshape(-1).astype(F32) for a in vectors])
    n = flat.shape[0]
    if rows is None:
        rows = -(-n // 1024) * 8
    return jnp.pad(flat, (0, rows * 128 - n)).reshape(rows, 128)


def _unpack_rows(packed, like):
    flat = packed.reshape(-1)
    out, pos = [], 0
    for a in like:
        out.append(flat[pos:pos + a.size].reshape(a.shape))
        pos += a.size
    return out


def kernel(x, p, mix_norm, mlp_norm, ple_norm, w_a_in, a_lb_logits, a_head_gain, w_a_out, kv_norm, w_kvf, b_f, w_b_q, w_b_out, w_mlp_up, w_mlp_down, w_ple_gate, w_ple_up, final_norm, loss_target, m_mix_norm, m_mlp_norm, m_ple_norm, m_w_a_in, m_a_lb_logits, m_a_head_gain, m_w_a_out, m_kv_norm, m_w_kvf, m_b_f, m_w_b_q, m_w_b_out, m_w_mlp_up, m_w_mlp_down, m_w_ple_gate, m_w_ple_up, m_final_norm, v_mix_norm, v_mlp_norm, v_ple_norm, v_w_a_in, v_a_lb_logits, v_a_head_gain, v_w_a_out, v_kv_norm, v_w_kvf, v_b_f, v_w_b_q, v_w_b_out, v_w_mlp_up, v_w_mlp_down, v_w_ple_gate, v_w_ple_up, v_final_norm):
    t, d = x.shape[1], x.shape[2]
    nh = d // HEAD_DIM
    n_layers = 2
    x2 = x.reshape(t, d)
    target = loss_target.reshape(t, d)
    me = _my_index()

    shards = {"w_a_in": w_a_in[0], "w_a_out": w_a_out[0], "w_kvf": w_kvf, "w_b_q": w_b_q[0], "w_b_out": w_b_out[0]}
    for l in range(n_layers):
        shards.update({f"w_mlp_up{l}": w_mlp_up[l], f"w_mlp_down{l}": w_mlp_down[l],
                       f"w_ple_gate{l}": w_ple_gate[l], f"w_ple_up{l}": w_ple_up[l]})
    first_use = ["a_lb_logits", "w_a_in", "w_a_out", "w_mlp_up0", "w_mlp_down0", "w_ple_gate0", "w_ple_up0", "w_kvf",
                 "w_b_q", "w_b_out", "w_mlp_up1", "w_mlp_down1", "w_ple_gate1", "w_ple_up1"]
    row_sharded = ("w_a_out", "w_b_q", "w_b_out", "w_mlp_down", "w_ple_gate")
    shards_bf = [a_lb_logits] + [shards[n].astype(BF16) for n in first_use[1:]]
    ag_handles = gather_start("ag_start", shards_bf)
    passed_on = {}
    weights = {}

    def pass_on(j, after):
        if j < len(first_use) and j not in passed_on:
            passed_on[j] = gather_pass_on("ag_pass_" + first_use[j], ag_handles[j], after)

    def weight(name, after=None):
        if name not in weights:
            j = first_use.index(name)
            pass_on(j, after)
            pass_on(j + 1, after)
            behind = passed_on[j + 1][1] if j + 1 in passed_on else after
            own, land = gather_wait("ag_wait_" + name, passed_on[j][0], behind)
            g = lax.dynamic_update_slice(land, own[None], (me, 0, 0))
            if name.rstrip("01") in row_sharded:
                g = g.reshape(1, g.shape[0] * g.shape[1], g.shape[2])
            weights[name] = g
        return weights[name]

    lgt = weight("a_lb_logits", x2).transpose(1, 0, 2).reshape(2, d)
    p_bf = [p[l, 0].astype(BF16) for l in range(n_layers)]

    def row(vec):
        return vec.reshape(1, -1)

    def mlp_ple_fwd(l, h_in, a):
        (h_a, u_mlp), _ = rowwise(f"add_norm_mlp{l}", _add_norm_fwd, [h_in, a], [row(mlp_norm[l])])
        pre = mm_nn(f"mlp_up{l}", u_mlp, weight(f"w_mlp_up{l}", u_mlp))
        (act,), _ = rowwise(f"relu2_{l}", _relu2_fwd, [pre])
        mo = mm_nn(f"mlp_down{l}", act, weight(f"w_mlp_down{l}", act))
        (h_b, u_ple), _ = rowwise(f"add_norm_ple{l}", _add_norm_fwd, [h_a, mo], [row(ple_norm[l])])
        gpre = mm_nn(f"ple_gate{l}", u_ple, weight(f"w_ple_gate{l}", u_ple))
        pu = mm_nn(f"ple_up{l}", p_bf[l], weight(f"w_ple_up{l}", gpre))
        return dict(h_a=h_a, u_mlp=u_mlp, pre=pre, act=act, h_b=h_b, u_ple=u_ple, gpre=gpre, pu=pu)

    (u0,), _ = rowwise("norm_mix0", _norm_fwd, [x2], [row(mix_norm[0])])
    z = mm_nn("a_in", u0, weight("w_a_in", u0))
    og, states = hgrn_fwd(z, lgt, a_head_gain)
    a0 = mm_nn("a_out", og, weight("w_a_out", og))
    s0 = mlp_ple_fwd(0, x2, a0)
    (h3, u_kv, u1), _ = rowwise("ple_norms", _ple_two_norms_fwd, [s0["h_b"], s0["gpre"], s0["pu"]],
                                [row(kv_norm), row(mix_norm[1])])
    hk = mm_nn("kvf", u_kv, weight("w_kvf", u_kv), out3=True)
    hk = hk.transpose(1, 0, 2).reshape(t, -1)
    k_bf, v_bf = hk[:, :d].astype(BF16), hk[:, d:2 * d].astype(BF16)
    fl_t = hk[:, 2 * d:].T
    b_f_col = b_f.reshape(nh, 1)
    dcum = decay_fwd(fl_t, b_f_col)
    dcol, drow = dcum.reshape(nh, t, 1), dcum.reshape(nh, 1, t)
    q = mm_nn("b_q", u1, weight("w_b_q", dcum))
    o, lse = attn_fwd(q, k_bf, v_bf, dcol, drow)
    a1 = mm_nn("b_out", o, weight("w_b_out", o))
    s1 = mlp_ple_fwd(1, h3, a1)

    (dh, dgpre, dpu), (d_final, loss_rows) = rowwise(
        "tail", _tail_fwd_bwd, [s1["h_b"], s1["gpre"], s1["pu"], target], [row(final_norm)])
    loss = lax.psum(loss_rows[0, 0], MESH_AXES)

    sent = {}
    tokens = []

    def send_grad(name, g):
        g = g.reshape(N_DEV, -1, g.shape[-1])
        (handle,), token = exchange_start("rs_start_" + name, [g], False)
        sent[name] = handle
        tokens.append(token)

    def after_sends():
        deps = tuple(tokens)
        tokens.clear()
        return deps

    def mlp_ple_bwd(l, s, dh, dgpre, dpu):
        send_grad(f"w_ple_gate{l}", mm_tn(f"d_ple_gate_w{l}", s["u_ple"], dgpre, 1, deps=after_sends()))
        send_grad(f"w_ple_up{l}", mm_tn(f"d_ple_up_w{l}", p_bf[l], dpu, N_DEV, deps=after_sends()))
        du = mm_nt(f"d_ple_gate_x{l}", dgpre, weight(f"w_ple_gate{l}"), deps=after_sends())
        (dh, dh_bf), (d_ple,) = rowwise(f"d_norm_ple{l}", _norm_bwd, [s["h_b"], du, dh], [row(ple_norm[l])])
        send_grad(f"w_mlp_down{l}", mm_tn(f"d_mlp_down_w{l}", s["act"], dh_bf, 1))
        dact = mm_nt(f"d_mlp_down_x{l}", dh_bf, weight(f"w_mlp_down{l}"), deps=after_sends())
        (dpre,), _ = rowwise(f"d_relu2_{l}", _relu2_bwd, [s["pre"], dact])
        send_grad(f"w_mlp_up{l}", mm_tn(f"d_mlp_up_w{l}", s["u_mlp"], dpre, N_DEV))
        du = mm_nt(f"d_mlp_up_x{l}", dpre, weight(f"w_mlp_up{l}"), deps=after_sends())
        (dh, dh_bf), (d_mlp,) = rowwise(f"d_norm_mlp{l}", _norm_bwd, [s["h_a"], du, dh], [row(mlp_norm[l])])
        return dh, dh_bf, d_ple, d_mlp

    dh, dh_bf, d_ple1, d_mlp1 = mlp_ple_bwd(1, s1, dh, dgpre, dpu)
    send_grad("w_b_out", mm_tn("d_b_out_w", o, dh_bf, 1))
    do = mm_nt("d_b_out_x", dh_bf, weight("w_b_out"), out_dtype=BF16, deps=after_sends())
    dq, dk, dv, ddrow = attn_bwd(q, k_bf, v_bf, dcol, drow, lse, do)
    send_grad("w_b_q", mm_tn("d_b_q_w", u1, dq, 1))
    du1 = mm_nt("d_b_q_x", dq, weight("w_b_q"), deps=after_sends())
    dfl_t, d_b_f = decay_bwd(fl_t, b_f_col, ddrow.reshape(nh, t))
    dhk = jnp.concatenate([dk, dv, dfl_t.T.astype(BF16)], axis=1)
    dhk = dhk.reshape(t, N_DEV, -1).transpose(1, 0, 2)
    send_grad("w_kvf", mm_tn("d_kvf_w", u_kv, dhk, N_DEV))
    du_kv = mm_nt("d_kvf_x", dhk, weight("w_kvf"), deps=after_sends())
    (dh,), (d_kv_norm, d_mix1) = rowwise("d_ple_norms", _two_norms_bwd, [h3, du_kv, du1, dh],
                                         [row(kv_norm), row(mix_norm[1])])
    (dgpre, dpu), _ = rowwise("d_ple0", _ple_bwd, [s0["gpre"], s0["pu"], dh])
    dh, dh_bf, d_ple0, d_mlp0 = mlp_ple_bwd(0, s0, dh, dgpre, dpu)
    send_grad("w_a_out", mm_tn("d_a_out_w", og, dh_bf, 1))
    dog = mm_nt("d_a_out_x", dh_bf, weight("w_a_out"), deps=after_sends())
    dz4, d_lgt, d_hg = hgrn_bwd(z, lgt, a_head_gain, states, dog)
    dz = dz4.transpose(1, 0, 2).reshape(t, 4 * d)
    send_grad("w_a_in", mm_tn("d_a_in_w", u0, dz, N_DEV))
    du0 = mm_nt("d_a_in_x", dz, weight("w_a_in"), deps=after_sends())
    (dx, _), (d_mix0,) = rowwise("d_norm_mix0", _norm_bwd, [x2, du0, dh], [row(mix_norm[0])])

    new = {}
    last = [dx]

    def update(name, parts, w, m, v):
        shp = w.shape
        w3, m3, v3 = (a.reshape(len(parts), -1, shp[-1]) for a in (w, m, v))
        new[name] = tuple(a.reshape(shp) for a in adamw_reduce("adamw_" + name, parts, w3, m3, v3))
        last[0] = new[name][0]

    def receive_update(name, layers, w, m, v):
        parts = {}
        for sfx in layers:
            g, land = exchange_wait(f"rs_wait_{name}{sfx}", sent[name + sfx], last[0], False)
            parts[sfx] = lax.dynamic_update_slice(land, lax.dynamic_slice_in_dim(g, me, 1, 0), (me, 0, 0))
        update(name, [parts[sfx] for sfx in sorted(layers)], w, m, v)

    both = ("1", "0")
    receive_update("w_b_out", ("",), w_b_out, m_w_b_out, v_w_b_out)
    receive_update("w_b_q", ("",), w_b_q, m_w_b_q, v_w_b_q)
    receive_update("w_kvf", ("",), w_kvf, m_w_kvf, v_w_kvf)
    receive_update("w_ple_gate", both, w_ple_gate, m_w_ple_gate, v_w_ple_gate)
    receive_update("w_ple_up", both, w_ple_up, m_w_ple_up, v_w_ple_up)
    receive_update("w_mlp_down", both, w_mlp_down, m_w_mlp_down, v_w_mlp_down)
    receive_update("w_mlp_up", both, w_mlp_up, m_w_mlp_up, v_w_mlp_up)
    receive_update("w_a_out", ("",), w_a_out, m_w_a_out, v_w_a_out)
    receive_update("w_a_in", ("",), w_a_in, m_w_a_in, v_w_a_in)

    small = dict(mix_norm=jnp.concatenate([d_mix0, d_mix1]), mlp_norm=jnp.concatenate([d_mlp0, d_mlp1]),
                 ple_norm=jnp.concatenate([d_ple0, d_ple1]), a_head_gain=d_hg, kv_norm=d_kv_norm.reshape(d),
                 b_f=d_b_f.reshape(nh), final_norm=d_final.reshape(d))
    small_w = dict(mix_norm=(mix_norm, m_mix_norm, v_mix_norm), mlp_norm=(mlp_norm, m_mlp_norm, v_mlp_norm),
                   ple_norm=(ple_norm, m_ple_norm, v_ple_norm),
                   a_head_gain=(a_head_gain, m_a_head_gain, v_a_head_gain), kv_norm=(kv_norm, m_kv_norm, v_kv_norm),
                   b_f=(b_f, m_b_f, v_b_f), final_norm=(final_norm, m_final_norm, v_final_norm))
    names = list(small)
    packed = _pack_rows([d_lgt] + [small[n] for n in names])
    everyone = all_gather("ag_small_grads", packed, deps=(last[0],))
    n_lgt_rows = d_lgt.size // 128
    lgt_parts = everyone[:, :n_lgt_rows].reshape(N_DEV, 2, d)
    lgt_parts = lax.dynamic_slice_in_dim(lgt_parts, me * a_lb_logits.shape[1], a_lb_logits.shape[1], axis=2)
    update("a_lb_logits", [lgt_parts], a_lb_logits, m_a_lb_logits, v_a_lb_logits)
    rest = everyone[:, n_lgt_rows:]
    like = [small_w[n][0] for n in names]
    packed_w, packed_m, packed_v = (_pack_rows([small_w[n][j] for n in names], rest.shape[1])[None] for j in range(3))
    outs = adamw_reduce("adamw_small", [rest], packed_w, packed_m, packed_v)
    unpacked = [_unpack_rows(a, like) for a in outs]
    for j, n in enumerate(names):
        new[n] = tuple(unpacked[q][j] for q in range(4))

    order = ["mix_norm", "mlp_norm", "ple_norm", "w_a_in", "a_lb_logits", "a_head_gain", "w_a_out", "kv_norm",
             "w_kvf", "b_f", "w_b_q", "w_b_out", "w_mlp_up", "w_mlp_down", "w_ple_gate", "w_ple_up", "final_norm"]
    result = [loss, dx.reshape(x.shape)]
    for j in range(4):
        result += [new[n][j] for n in order]
    return tuple(result)
```

```python
import functools

import jax
import jax.numpy as jnp
from jax import lax
from jax.experimental import pallas as pl
from jax.experimental.pallas import tpu as pltpu

F32 = jnp.float32
BF16 = jnp.bfloat16
HEAD_DIM = 128
CHUNK = 16
TILE = 128
HEADS_PER_STEP = 2
NORM_EPS = 1e-6
N_DEV = 8
MESH_AXES = ("x", "y", "c")
VMEM_LIMIT_BYTES = 48 * 1024 * 1024
ROW_TILE_BYTES = 1024 * 1024
LR, B1, B2, ADAM_EPS, WD, STEP = 0.001, 0.9, 0.999, 1e-08, 0.01, 10
NEG_BIG = -1e30

NN = (((1,), (0,)), ((), ()))
NT = (((1,), (1,)), ((), ()))
TN = (((0,), (0,)), ((), ()))


def _params(semantics):
    return pltpu.CompilerParams(dimension_semantics=semantics, vmem_limit_bytes=VMEM_LIMIT_BYTES)


def _tile(n, prefs):
    for p in prefs:
        if n % p == 0:
            return p
    return n


def _row_tile(rows, limit):
    for cand in (2048, 1024, 512, 256, 128, 64, 32, 16):
        if cand <= limit and rows % cand == 0:
            return cand
    return rows


def _mm_call(name, a, b, dims, grid, a_spec, b_spec, o_spec, o_shape, acc_shape, k_axes, out_dtype, deps=()):
    nk = 1
    for ax in k_axes:
        nk *= grid[ax]

    def one_step(a_ref, b_ref, *rest):
        o_ref = rest[-1]
        o_ref[...] = lax.dot_general(a_ref[...], b_ref[...], dims, preferred_element_type=F32).astype(o_ref.dtype)

    def accumulate(a_ref, b_ref, *rest):
        o_ref, acc_ref = rest[-2:]
        k = 0
        for ax in k_axes:
            k = k * grid[ax] + pl.program_id(ax)
        part = lax.dot_general(a_ref[...], b_ref[...], dims, preferred_element_type=F32)

        @pl.when(k == 0)
        def _():
            acc_ref[...] = part

        @pl.when((k > 0) & (k < nk - 1))
        def _():
            acc_ref[...] += part

        @pl.when(k == nk - 1)
        def _():
            o_ref[...] = (acc_ref[...] + part).astype(o_ref.dtype)

    sem = tuple("arbitrary" if ax in k_axes else "parallel" for ax in range(len(grid)))
    return pl.pallas_call(
        one_step if nk == 1 else accumulate, name=name, grid=grid,
        in_specs=[a_spec, b_spec] + [pl.BlockSpec(memory_space=pl.ANY)] * len(deps),
        out_specs=o_spec, out_shape=jax.ShapeDtypeStruct(o_shape, out_dtype),
        scratch_shapes=[] if nk == 1 else [pltpu.VMEM(acc_shape, F32)], compiler_params=_params(sem),
    )(a, b, *deps)


def mm_nn(name, a, b3, out_dtype=F32, out3=False, deps=()):
    m, k = a.shape
    g, _, n = b3.shape
    tm, tk = _tile(m, (1024, 512, 256)), _tile(k, (2048, 1024, 512, 256))
    tn = n if out3 else _tile(n, (1024, 512, 256, 128))
    nj = n // tn
    grid = (m // tm, g, nj, k // tk)
    a_spec = pl.BlockSpec((tm, tk), lambda i, gg, j, kk: (i, kk))
    b_spec = pl.BlockSpec((None, tk, tn), lambda i, gg, j, kk: (gg, kk, j))
    if out3:
        o_spec = pl.BlockSpec((None, tm, tn), lambda i, gg, j, kk: (gg, i, j))
        o_shape = (g, m, n)
    else:
        o_spec = pl.BlockSpec((tm, tn), lambda i, gg, j, kk: (i, gg * nj + j))
        o_shape = (m, g * n)
    return _mm_call(name, a, b3, NN, grid, a_spec, b_spec, o_spec, o_shape, (tm, tn), (3,), out_dtype, deps)


def mm_nt(name, a, b3, out_dtype=F32, deps=()):
    g, k, n = b3.shape
    a3 = a.ndim == 3
    m = a.shape[1] if a3 else a.shape[0]
    tm, tko = _tile(m, (1024, 512, 256)), _tile(k, (1024, 512, 256))
    tc = n if a3 else _tile(n, (2048, 1024, 512, 256, 128))
    nc = n // tc
    grid = (m // tm, k // tko, g, nc)
    if a3:
        a_spec = pl.BlockSpec((None, tm, tc), lambda i, j, gg, c: (gg, i, c))
    else:
        a_spec = pl.BlockSpec((tm, tc), lambda i, j, gg, c: (i, gg * nc + c))
    b_spec = pl.BlockSpec((None, tko, tc), lambda i, j, gg, c: (gg, j, c))
    o_spec = pl.BlockSpec((tm, tko), lambda i, j, gg, c: (i, j))
    return _mm_call(name, a, b3, NT, grid, a_spec, b_spec, o_spec, (m, k), (tm, tko), (2, 3), out_dtype, deps)


def mm_tn(name, a, b, g, out_dtype=BF16, deps=()):
    t, k = a.shape
    b3 = b.ndim == 3
    n = b.shape[2] if b3 else b.shape[1] // g
    tm = _tile(k, (1024, 512, 256))
    tn = n if b3 else _tile(n, (1024, 512, 256, 128))
    tt = _tile(t, (2048, 1024, 512, 256))
    nj = n // tn
    grid = (g, k // tm, nj, t // tt)
    a_spec = pl.BlockSpec((tt, tm), lambda gg, i, j, s: (s, i))
    if b3:
        b_spec = pl.BlockSpec((None, tt, tn), lambda gg, i, j, s: (gg, s, j))
    else:
        b_spec = pl.BlockSpec((tt, tn), lambda gg, i, j, s: (s, gg * nj + j))
    o_spec = pl.BlockSpec((None, tm, tn), lambda gg, i, j, s: (gg, i, j))
    return _mm_call(name, a, b, TN, grid, a_spec, b_spec, o_spec, (g, k, n), (tm, tn), (3,), out_dtype, deps)


def rowwise(name, fn, rows, vecs=()):
    t = rows[0].shape[0]
    wmax = max(r.shape[1] for r in rows)
    tr = _row_tile(t, ROW_TILE_BYTES // (4 * wmax))
    row_s = [jax.ShapeDtypeStruct((tr, r.shape[1]), r.dtype) for r in rows]
    vec_s = [jax.ShapeDtypeStruct(v.shape, v.dtype) for v in vecs]
    out_rows_s, out_sums_s = jax.eval_shape(fn, *row_s, *vec_s)
    n_in, n_r = len(rows) + len(vecs), len(out_rows_s)

    def body(*refs):
        i = pl.program_id(0)
        o_rows, o_sums = fn(*[r[...] for r in refs[:n_in]])
        for ref, val in zip(refs[n_in:n_in + n_r], o_rows):
            ref[...] = val

        if out_sums_s:
            @pl.when(i == 0)
            def _():
                for ref in refs[n_in + n_r:]:
                    ref[...] = jnp.zeros_like(ref)

            for ref, val in zip(refs[n_in + n_r:], o_sums):
                ref[...] += val

    in_specs = [pl.BlockSpec((tr, r.shape[1]), lambda i: (i, 0)) for r in rows]
    in_specs += [pl.BlockSpec(v.shape, lambda i: (0, 0)) for v in vecs]
    out_specs = [pl.BlockSpec((tr, s.shape[1]), lambda i: (i, 0)) for s in out_rows_s]
    out_specs += [pl.BlockSpec(s.shape, lambda i: (0, 0)) for s in out_sums_s]
    out_shape = [jax.ShapeDtypeStruct((t, s.shape[1]), s.dtype) for s in out_rows_s]
    out_shape += [jax.ShapeDtypeStruct(s.shape, s.dtype) for s in out_sums_s]
    outs = pl.pallas_call(
        body, name=name, grid=(t // tr,), in_specs=in_specs, out_specs=out_specs, out_shape=out_shape,
        compiler_params=_params(("arbitrary",)),
    )(*rows, *vecs)
    return outs[:n_r], outs[n_r:]


def _rms(x, gain):
    return x * lax.rsqrt(jnp.mean(x * x, axis=-1, keepdims=True) + NORM_EPS) * gain


def _norm_fwd(x, gain):
    return (_rms(x, gain).astype(BF16),), ()


def _add_norm_fwd(h, a, gain):
    h = h + a
    return (h, _rms(h, gain).astype(BF16)), ()


def _relu2_fwd(pre):
    r = jnp.maximum(pre, 0.0)
    return ((r * r).astype(BF16),), ()


def _ple(h, gpre, pu):
    return h + pu * jax.nn.sigmoid(gpre)


def _ple_two_norms_fwd(h, gpre, pu, gain_a, gain_b):
    h = _ple(h, gpre, pu)
    return (h, _rms(h, gain_a).astype(BF16), _rms(h, gain_b).astype(BF16)), ()


def _tail_fwd_bwd(h, gpre, pu, target, gain):
    def row_loss(h, gpre, pu, gain):
        y = _rms(_ple(h, gpre, pu), gain)
        return 0.5 * jnp.mean(jnp.square(y - target), axis=-1, keepdims=True)

    loss, vjp = jax.vjp(row_loss, h, gpre, pu, gain)
    dh, dgpre, dpu, dgain = vjp(jnp.ones_like(loss))
    loss = jnp.broadcast_to(jnp.sum(loss, axis=0, keepdims=True), (1, 128))
    return (dh, dgpre.astype(BF16), dpu.astype(BF16)), (dgain, loss)


def _ple_bwd(gpre, pu, dh):
    _, vjp = jax.vjp(lambda g, u: pu_times_gate(g, u), gpre, pu)
    dgpre, dpu = vjp(dh)
    return (dgpre.astype(BF16), dpu.astype(BF16)), ()


def pu_times_gate(gpre, pu):
    return pu * jax.nn.sigmoid(gpre)


def _norm_bwd(h, du, dh_in, gain):
    _, vjp = jax.vjp(_rms, h, gain)
    dh, dgain = vjp(du)
    dh = dh_in + dh
    return (dh, dh.astype(BF16)), (dgain,)


def _two_norms_bwd(h, du_a, du_b, dh_in, gain_a, gain_b):
    _, vjp = jax.vjp(lambda h, ga, gb: (_rms(h, ga), _rms(h, gb)), h, gain_a, gain_b)
    dh, dga, dgb = vjp((du_a, du_b))
    return (dh_in + dh,), (dga, dgb)


def _relu2_bwd(pre, dact):
    return ((dact * 2.0 * jnp.maximum(pre, 0.0)).astype(BF16),), ()


def _bf16_dot(dims_fwd, dims_da, dims_db, swap_da, swap_db):
    @jax.custom_vjp
    def dot(a, b):
        return lax.dot_general(a.astype(BF16), b.astype(BF16), dims_fwd, preferred_element_type=F32)

    def fwd(a, b):
        return dot(a, b), (a, b)

    def bwd(res, ct):
        a, b = res
        ct, a, b = ct.astype(BF16), a.astype(BF16), b.astype(BF16)
        da = lax.dot_general(*((b, ct) if swap_da else (ct, b)), dims_da, preferred_element_type=F32)
        db = lax.dot_general(*((ct, a) if swap_db else (a, ct)), dims_db, preferred_element_type=F32)
        return da, db

    dot.defvjp(fwd, bwd)
    return dot


_dot_nn = _bf16_dot(NN, NT, TN, False, False)
_dot_nt = _bf16_dot(NT, NN, TN, False, True)
_dot_tn = _bf16_dot(TN, NT, NN, True, False)


def _chunk_masks(transposed):
    r = lax.broadcasted_iota(jnp.int32, (TILE, TILE), 0)
    c = lax.broadcasted_iota(jnp.int32, (TILE, TILE), 1)
    same = (r // CHUNK) == (c // CHUNK)
    causal = same & ((r <= c) if transposed else (c <= r))
    return causal, same


def _chunk_scan(x, reverse):
    pos = lax.broadcasted_iota(jnp.int32, x.shape, 0) % CHUNK
    step = 1
    while step < CHUNK:
        if reverse:
            x = x + jnp.where(pos < CHUNK - step, pltpu.roll(x, x.shape[0] - step, axis=0), 0.0)
        else:
            x = x + jnp.where(pos >= step, pltpu.roll(x, step, axis=0), 0.0)
        step *= 2
    return x


def _chunk_total(x):
    return _chunk_scan(x, False) + _chunk_scan(x, True) - x


@jax.custom_vjp
def _chunk_sums(x):
    return _chunk_scan(x, False), _chunk_total(x)


def _chunk_sums_fwd(x):
    return _chunk_sums(x), None


def _chunk_sums_bwd(_, ct):
    return (_chunk_scan(ct[0], True) + _chunk_total(ct[1]),)


_chunk_sums.defvjp(_chunk_sums_fwd, _chunk_sums_bwd)


def _hgrn_tile(q, f, i, g, lgt, hg, st):
    d = q.shape[1]
    l0, l1 = lgt[0:1], lgt[1:2]
    mx = jnp.maximum(l0, l1)
    e0, e1 = jnp.exp(l0 - mx), jnp.exp(l1 - mx)
    lb = e0 / (e0 + e1)
    fg = lb + (1.0 - lb) * jax.nn.sigmoid(f)
    k = 1.0 - fg
    causal, _ = _chunk_masks(False)
    b, b_last = _chunk_sums(jnp.log(fg))
    q_in = q * jax.nn.sigmoid(q) * (d ** -0.5) * jnp.exp(b)
    k_in = k * jnp.exp(-b)
    k_end = k * jnp.exp(b_last - b)
    att = jnp.where(causal, _dot_nt(q_in, k_in), 0.0)
    o_intra = _dot_nn(att, i)
    n_chunks = TILE // CHUNK
    chunk_of_row = lax.broadcasted_iota(jnp.int32, (TILE, 1), 0) // CHUNK

    def spread(a):
        return jnp.concatenate([jnp.where(chunk_of_row == n, a, 0.0) for n in range(n_chunks)], axis=1)

    increments = _dot_tn(i, spread(k_end))
    states = []
    for n in range(n_chunks):
        states.append(st)
        decay = jnp.exp(jnp.mean(b_last[n * CHUNK:(n + 1) * CHUNK], axis=0, keepdims=True))
        st = st * decay + increments[:, n * d:(n + 1) * d]
    o = o_intra + _dot_nt(spread(q_in), jnp.concatenate(states, axis=1))
    o = o * lax.rsqrt(jnp.mean(o * o, axis=-1, keepdims=True) + NORM_EPS) * hg
    return o * (g * jax.nn.sigmoid(g)), st


def hgrn_fwd(z, lgt, hg):
    t, d4 = z.shape
    d = d4 // 4
    nh, nt = d // HEAD_DIM, t // TILE
    hp = HEADS_PER_STEP
    wide = hp * HEAD_DIM

    def body(q_ref, f_ref, i_ref, g_ref, lgt_ref, hg_ref, o_ref, st_out_ref, st_ref):
        tt = pl.program_id(1)

        @pl.when(tt == 0)
        def _():
            st_ref[...] = jnp.zeros_like(st_ref)

        for hh in range(hp):
            cols = slice(hh * HEAD_DIM, (hh + 1) * HEAD_DIM)
            st = st_ref[hh]
            st_out_ref[hh] = st
            o, st = _hgrn_tile(q_ref[:, cols], f_ref[:, cols], i_ref[:, cols], g_ref[:, cols], lgt_ref[:, cols],
                               hg_ref[...], st)
            o_ref[:, cols] = o.astype(o_ref.dtype)
            st_ref[hh] = st

    def part(p):
        return pl.BlockSpec((TILE, wide), lambda h, tt: (tt, p * (nh // hp) + h))

    return pl.pallas_call(
        body, name="hgrn_fwd", grid=(nh // hp, nt),
        in_specs=[part(0), part(1), part(2), part(3),
                  pl.BlockSpec((2, wide), lambda h, tt: (0, h)),
                  pl.BlockSpec((1, HEAD_DIM), lambda h, tt: (0, 0))],
        out_specs=[pl.BlockSpec((TILE, wide), lambda h, tt: (tt, h)),
                   pl.BlockSpec((hp, None, HEAD_DIM, HEAD_DIM), lambda h, tt: (h, tt, 0, 0))],
        out_shape=[jax.ShapeDtypeStruct((t, d), BF16),
                   jax.ShapeDtypeStruct((nh, nt, HEAD_DIM, HEAD_DIM), F32)],
        scratch_shapes=[pltpu.VMEM((hp, HEAD_DIM, HEAD_DIM), F32)],
        compiler_params=_params(("parallel", "arbitrary")),
    )(z, z, z, z, lgt, hg)


def hgrn_bwd(z, lgt, hg, states, dout):
    t, d4 = z.shape
    d = d4 // 4
    nh, nt = d // HEAD_DIM, t // TILE
    hp = HEADS_PER_STEP
    wide = hp * HEAD_DIM

    def body(q_ref, f_ref, i_ref, g_ref, lgt_ref, hg_ref, st_in_ref, do_ref, dz_ref, dlgt_ref, dhg_ref, dst_ref):
        h, tt = pl.program_id(0), pl.program_id(1)

        @pl.when(tt == 0)
        def _():
            dst_ref[...] = jnp.zeros_like(dst_ref)
            dlgt_ref[...] = jnp.zeros_like(dlgt_ref)

        @pl.when((tt == 0) & (h == 0))
        def _():
            dhg_ref[...] = jnp.zeros_like(dhg_ref)

        for hh in range(hp):
            cols = slice(hh * HEAD_DIM, (hh + 1) * HEAD_DIM)
            _, vjp = jax.vjp(_hgrn_tile, q_ref[:, cols], f_ref[:, cols], i_ref[:, cols], g_ref[:, cols],
                             lgt_ref[:, cols], hg_ref[...], st_in_ref[hh])
            grads = vjp((do_ref[:, cols], dst_ref[hh]))
            for p in range(4):
                dz_ref[p, :, cols] = grads[p].astype(dz_ref.dtype)
            dlgt_ref[:, cols] += grads[4]
            dhg_ref[...] += grads[5]
            dst_ref[hh] = grads[6]

    def part(p):
        return pl.BlockSpec((TILE, wide), lambda h, tt: (nt - 1 - tt, p * (nh // hp) + h))

    return pl.pallas_call(
        body, name="hgrn_bwd", grid=(nh // hp, nt),
        in_specs=[part(0), part(1), part(2), part(3),
                  pl.BlockSpec((2, wide), lambda h, tt: (0, h)),
                  pl.BlockSpec((1, HEAD_DIM), lambda h, tt: (0, 0)),
                  pl.BlockSpec((hp, None, HEAD_DIM, HEAD_DIM), lambda h, tt: (h, nt - 1 - tt, 0, 0)),
                  pl.BlockSpec((TILE, wide), lambda h, tt: (nt - 1 - tt, h))],
        out_specs=[pl.BlockSpec((4, TILE, wide), lambda h, tt: (0, nt - 1 - tt, h)),
                   pl.BlockSpec((2, wide), lambda h, tt: (0, h)),
                   pl.BlockSpec((1, HEAD_DIM), lambda h, tt: (0, 0))],
        out_shape=[jax.ShapeDtypeStruct((4, t, d), BF16),
                   jax.ShapeDtypeStruct((2, d), F32),
                   jax.ShapeDtypeStruct((1, HEAD_DIM), F32)],
        scratch_shapes=[pltpu.VMEM((hp, HEAD_DIM, HEAD_DIM), F32)],
        compiler_params=_params(("arbitrary", "arbitrary")),
    )(z, z, z, z, lgt, hg, states, dout)


def _log_sigmoid(x):
    return jnp.minimum(x, 0.0) - jnp.log(1.0 + jnp.exp(-jnp.abs(x)))


def decay_fwd(fl_t, b_f):
    nh, t = fl_t.shape

    def body(fl_ref, b_ref, out_ref):
        r = lax.broadcasted_iota(jnp.int32, (128, 128), 0)
        c = lax.broadcasted_iota(jnp.int32, (128, 128), 1)
        upper = (r <= c).astype(F32)
        carry = jnp.zeros((nh, 1), F32)
        for j in range(t // 128):
            cols = slice(j * 128, (j + 1) * 128)
            ls = _log_sigmoid(fl_ref[:, cols] + b_ref[...])
            out_ref[:, cols] = carry + jnp.dot(ls, upper, precision=lax.Precision.HIGHEST,
                                               preferred_element_type=F32)
            carry = carry + jnp.sum(ls, axis=1, keepdims=True)

    return pl.pallas_call(body, name="decay_fwd", out_shape=jax.ShapeDtypeStruct((nh, t), F32),
                          compiler_params=_params(None))(fl_t, b_f)


def decay_bwd(fl_t, b_f, ddcum):
    nh, t = fl_t.shape

    def body(fl_ref, b_ref, dd_ref, dfl_ref, db_ref):
        r = lax.broadcasted_iota(jnp.int32, (128, 128), 0)
        c = lax.broadcasted_iota(jnp.int32, (128, 128), 1)
        lower = (r >= c).astype(F32)
        carry = jnp.zeros((nh, 1), F32)
        db = jnp.zeros((nh, 1), F32)
        for j in reversed(range(t // 128)):
            cols = slice(j * 128, (j + 1) * 128)
            dd = dd_ref[:, cols]
            dls = carry + jnp.dot(dd, lower, precision=lax.Precision.HIGHEST, preferred_element_type=F32)
            carry = carry + jnp.sum(dd, axis=1, keepdims=True)
            dfl = dls * jax.nn.sigmoid(-(fl_ref[:, cols] + b_ref[...]))
            dfl_ref[:, cols] = dfl
            db = db + jnp.sum(dfl, axis=1, keepdims=True)
        db_ref[...] = db

    return pl.pallas_call(body, name="decay_bwd",
                          out_shape=[jax.ShapeDtypeStruct((nh, t), F32), jax.ShapeDtypeStruct((nh, 1), F32)],
                          compiler_params=_params(None))(fl_t, b_f, ddcum)


def _attn_probs_logits(q_ref, k_ref, dcol_ref, drow_ref, i, tq, t):
    qs = (q_ref[...] * (HEAD_DIM ** -0.5)).astype(BF16)
    s = lax.dot_general(qs, k_ref[...], NT, preferred_element_type=F32)
    s = s + dcol_ref[...] - drow_ref[...]
    row = i * tq + lax.broadcasted_iota(jnp.int32, (tq, t), 0)
    col = lax.broadcasted_iota(jnp.int32, (tq, t), 1)
    mask = col <= row
    return qs, jnp.where(mask, s, NEG_BIG), mask


def attn_fwd(q, k, v, dcol, drow):
    t, d = q.shape
    nh = d // HEAD_DIM
    tq = _tile(t, (256, 128))

    def body(q_ref, k_ref, v_ref, dcol_ref, drow_ref, o_ref, lse_ref):
        i = pl.program_id(1)
        _, s, _ = _attn_probs_logits(q_ref, k_ref, dcol_ref, drow_ref, i, tq, t)
        m = jnp.max(s, axis=1, keepdims=True)
        p = jnp.exp(s - m)
        l = jnp.sum(p, axis=1, keepdims=True)
        o = jnp.dot(p.astype(BF16), v_ref[...], preferred_element_type=F32)
        o_ref[...] = (o / l).astype(o_ref.dtype)
        lse_ref[...] = m + jnp.log(l)

    return pl.pallas_call(
        body, name="attn_fwd", grid=(nh, t // tq),
        in_specs=[pl.BlockSpec((tq, HEAD_DIM), lambda h, i: (i, h)),
                  pl.BlockSpec((t, HEAD_DIM), lambda h, i: (0, h)),
                  pl.BlockSpec((t, HEAD_DIM), lambda h, i: (0, h)),
                  pl.BlockSpec((None, tq, 1), lambda h, i: (h, i, 0)),
                  pl.BlockSpec((None, 1, t), lambda h, i: (h, 0, 0))],
        out_specs=[pl.BlockSpec((tq, HEAD_DIM), lambda h, i: (i, h)),
                   pl.BlockSpec((None, tq, 1), lambda h, i: (h, i, 0))],
        out_shape=[jax.ShapeDtypeStruct((t, d), BF16), jax.ShapeDtypeStruct((nh, t, 1), F32)],
        compiler_params=_params(("parallel", "parallel")),
    )(q, k, v, dcol, drow)


def attn_bwd(q, k, v, dcol, drow, lse, do):
    t, d = q.shape
    nh = d // HEAD_DIM
    tq = _tile(t, (256, 128))
    nq = t // tq

    def body(q_ref, k_ref, v_ref, dcol_ref, drow_ref, lse_ref, do_ref,
             dq_ref, dk_ref, dv_ref, ddrow_ref, dk_acc, dv_acc):
        i = pl.program_id(1)

        @pl.when(i == 0)
        def _():
            dk_acc[...] = jnp.zeros_like(dk_acc)
            dv_acc[...] = jnp.zeros_like(dv_acc)
            ddrow_ref[...] = jnp.zeros_like(ddrow_ref)

        qs, s, mask = _attn_probs_logits(q_ref, k_ref, dcol_ref, drow_ref, i, tq, t)
        p = jnp.where(mask, jnp.exp(s - lse_ref[...]), 0.0)
        do = do_ref[...]
        dp = lax.dot_general(do, v_ref[...], NT, preferred_element_type=F32)
        ds = p * (dp - jnp.sum(p * dp, axis=1, keepdims=True))
        dsb = ds.astype(BF16)
        dq = jnp.dot(dsb, k_ref[...], preferred_element_type=F32) * (HEAD_DIM ** -0.5)
        dq_ref[...] = dq.astype(dq_ref.dtype)
        dk_acc[...] += lax.dot_general(dsb, qs, TN, preferred_element_type=F32)
        dv_acc[...] += lax.dot_general(p.astype(BF16), do, TN, preferred_element_type=F32)
        ddrow_ref[...] -= jnp.sum(ds, axis=0, keepdims=True)

        @pl.when(i == nq - 1)
        def _():
            dk_ref[...] = dk_acc[...].astype(dk_ref.dtype)
            dv_ref[...] = dv_acc[...].astype(dv_ref.dtype)

    tile = pl.BlockSpec((tq, HEAD_DIM), lambda h, i: (i, h))
    full = pl.BlockSpec((t, HEAD_DIM), lambda h, i: (0, h))
    col = pl.BlockSpec((None, tq, 1), lambda h, i: (h, i, 0))
    rowv = pl.BlockSpec((None, 1, t), lambda h, i: (h, 0, 0))
    return pl.pallas_call(
        body, name="attn_bwd", grid=(nh, nq),
        in_specs=[tile, full, full, col, rowv, col, tile],
        out_specs=[tile, full, full, rowv],
        out_shape=[jax.ShapeDtypeStruct((t, d), BF16), jax.ShapeDtypeStruct((t, d), BF16),
                   jax.ShapeDtypeStruct((t, d), BF16), jax.ShapeDtypeStruct((nh, 1, t), F32)],
        scratch_shapes=[pltpu.VMEM((t, HEAD_DIM), F32), pltpu.VMEM((t, HEAD_DIM), F32)],
        compiler_params=_params(("parallel", "arbitrary")),
    )(q, k, v, dcol, drow, lse, do)


def _my_index():
    return (lax.axis_index("x") * 2 + lax.axis_index("y")) * 2 + lax.axis_index("c")


def _exchange(name, src, gather, deps=()):
    shape = src.shape if gather else src.shape[1:]

    def body(src_ref, *rest):
        out_ref, send_sems, recv_sems, local_sem = rest[len(deps):]
        x, y, c = (lax.axis_index(a) for a in MESH_AXES)
        me = (x * 2 + y) * 2 + c
        mine = src_ref if gather else src_ref.at[me]
        local = pltpu.make_async_copy(mine, out_ref.at[me], local_sem)
        local.start()
        copies = []
        for dlt in range(1, N_DEV):
            dx, dy, dc = dlt // 4, (dlt // 2) % 2, dlt % 2
            px, py, pc = x ^ dx, y ^ dy, c ^ dc
            peer = (px * 2 + py) * 2 + pc
            copies.append(pltpu.make_async_remote_copy(
                src_ref=src_ref if gather else src_ref.at[peer], dst_ref=out_ref.at[me],
                send_sem=send_sems.at[dlt - 1], recv_sem=recv_sems.at[dlt - 1],
                device_id=(px, py, pc), device_id_type=pl.DeviceIdType.MESH))
        for cp in copies:
            cp.start()
        for cp in copies:
            cp.wait_recv()
        for cp in copies:
            cp.wait_send()
        local.wait()

    return pl.pallas_call(
        body, name=name, out_shape=jax.ShapeDtypeStruct((N_DEV,) + tuple(shape), src.dtype),
        in_specs=[pl.BlockSpec(memory_space=pl.ANY)] * (1 + len(deps)), out_specs=pl.BlockSpec(memory_space=pl.ANY),
        scratch_shapes=[pltpu.SemaphoreType.DMA((N_DEV - 1,)), pltpu.SemaphoreType.DMA((N_DEV - 1,)),
                        pltpu.SemaphoreType.DMA],
        compiler_params=pltpu.CompilerParams(has_side_effects=True),
    )(src, *deps)


def all_gather(name, x, deps=()):
    return _exchange(name, x, True, deps)


_HBM = pl.BlockSpec(memory_space=pltpu.HBM)
_SEM = pl.BlockSpec(memory_space=pltpu.SEMAPHORE)
_DATAFLOW = pltpu.SideEffectType.DATAFLOW_SIDE_EFFECTING


def _peer_copies(src_ref, land_ref, send_sems, recv_sems, gather):
    x, y, c = (lax.axis_index(a) for a in MESH_AXES)
    me = (x * 2 + y) * 2 + c
    copies = []
    for dlt in range(1, N_DEV):
        px, py, pc = x ^ (dlt // 4), y ^ ((dlt // 2) % 2), c ^ (dlt % 2)
        peer = (px * 2 + py) * 2 + pc
        copies.append(pltpu.make_async_remote_copy(
            src_ref=src_ref if gather else src_ref.at[peer], dst_ref=land_ref.at[me],
            send_sem=send_sems.at[dlt - 1], recv_sem=recv_sems.at[dlt - 1],
            device_id=(px, py, pc), device_id_type=pl.DeviceIdType.MESH))
    return copies


def exchange_start(name, srcs, gather):
    n = len(srcs)
    lands = [lax.empty((N_DEV,) + tuple(s.shape if gather else s.shape[1:]), s.dtype) for s in srcs]

    def body(*refs):
        src_refs, land_refs = refs[:n], refs[n:2 * n]
        send_sems, recv_sems = refs[2 * n:3 * n], refs[3 * n:4 * n]
        token = refs[-1]
        for j in range(n):
            for cp in _peer_copies(src_refs[j], land_refs[j], send_sems[j], recv_sems[j], gather):
                cp.start()
        token[...] = jnp.zeros_like(token)

    sems = [pltpu.SemaphoreType.DMA((N_DEV - 1,))] * (2 * n)
    thru = [pltpu.HBM(a.shape, a.dtype) for a in list(srcs) + lands]
    outs = pl.pallas_call(
        body, name=name, out_shape=tuple(sems + thru + [jax.ShapeDtypeStruct((8, 128), F32)]),
        in_specs=[_HBM] * (2 * n), out_specs=tuple([_SEM] * (2 * n) + [_HBM] * (2 * n) + [pl.BlockSpec(memory_space=pltpu.VMEM)]),
        input_output_aliases={j: 2 * n + j for j in range(2 * n)},
        compiler_params=pltpu.CompilerParams(has_side_effects=_DATAFLOW),
    )(*[pltpu.with_memory_space_constraint(a, pltpu.HBM) for a in list(srcs) + lands])
    handles = [(outs[j], outs[n + j], outs[2 * n + j], outs[3 * n + j]) for j in range(n)]
    return handles, outs[-1]


def exchange_wait(name, handle, after, gather):
    send_sems, recv_sems, src, land = handle

    def body(src_ref, land_ref, send_ref, recv_ref, after_ref, src_out, land_out):
        for cp in _peer_copies(src_ref, land_ref, send_ref, recv_ref, gather):
            cp.wait_send()
            cp.wait_recv()

    return pl.pallas_call(
        body, name=name, out_shape=(pltpu.HBM(src.shape, src.dtype), pltpu.HBM(land.shape, land.dtype)),
        in_specs=[_HBM, _HBM, _SEM, _SEM, pl.BlockSpec(memory_space=pl.ANY)], out_specs=(_HBM, _HBM),
        input_output_aliases={0: 0, 1: 1},
        compiler_params=pltpu.CompilerParams(has_side_effects=_DATAFLOW),
    )(src, land, send_sems, recv_sems, after)


N_OTHER_CHIPS = 3


def _two_level_places():
    x, y, c = (lax.axis_index(a) for a in MESH_AXES)
    return (x, y, c), (x * 2 + y) * 2 + c, (x, y, 1 - c), [(1 - x, y), (x, 1 - y), (1 - x, 1 - y)]


def _first_copies(src_ref, land_ref, send_sems, recv_sems):
    (x, y, c), me, other_core, chips = _two_level_places()
    targets = [other_core] + [(cx, cy, c) for cx, cy in chips]
    return [pltpu.make_async_remote_copy(
        src_ref=src_ref, dst_ref=land_ref.at[me], send_sem=send_sems.at[k], recv_sem=recv_sems.at[k],
        device_id=to, device_id_type=pl.DeviceIdType.MESH) for k, to in enumerate(targets)]


def _passed_on_copies(land_ref, send_sems, recv_sems):
    (x, y, c), me, other_core, chips = _two_level_places()
    copies = []
    for k, (cx, cy) in enumerate(chips):
        slot = land_ref.at[(cx * 2 + cy) * 2 + c]
        copies.append(pltpu.make_async_remote_copy(
            src_ref=slot, dst_ref=slot, send_sem=send_sems.at[k], recv_sem=recv_sems.at[k],
            device_id=other_core, device_id_type=pl.DeviceIdType.MESH))
    return copies


def gather_start(name, srcs):
    n = len(srcs)
    lands = [lax.empty((N_DEV,) + tuple(s.shape), s.dtype) for s in srcs]

    def body(*refs):
        src_refs, land_refs = refs[:n], refs[n:2 * n]
        send_sems, recv_sems = refs[2 * n:3 * n], refs[3 * n:4 * n]
        for j in range(n):
            for cp in _first_copies(src_refs[j], land_refs[j], send_sems[j], recv_sems[j]):
                cp.start()

    sems = [pltpu.SemaphoreType.DMA((1 + N_OTHER_CHIPS,))] * (2 * n)
    thru = [pltpu.HBM(a.shape, a.dtype) for a in list(srcs) + lands]
    outs = pl.pallas_call(
        body, name=name, out_shape=tuple(sems + thru), in_specs=[_HBM] * (2 * n),
        out_specs=tuple([_SEM] * (2 * n) + [_HBM] * (2 * n)),
        input_output_aliases={j: 2 * n + j for j in range(2 * n)},
        compiler_params=pltpu.CompilerParams(has_side_effects=_DATAFLOW),
    )(*[pltpu.with_memory_space_constraint(a, pltpu.HBM) for a in list(srcs) + lands])
    return [[outs[j], outs[n + j], outs[2 * n + j], outs[3 * n + j]] for j in range(n)]


def gather_pass_on(name, handle, after):
    send_sems, recv_sems, src, land = handle

    def body(src_ref, land_ref, recv_ref, after_ref, land_out, send2, recv2, token):
        arrivals = _first_copies(src_ref, land_ref, recv_ref, recv_ref)
        for k, cp in enumerate(_passed_on_copies(land_ref, send2, recv2)):
            arrivals[1 + k].wait_recv()
            cp.start()
        token[...] = jnp.zeros_like(token)

    sem3 = pltpu.SemaphoreType.DMA((N_OTHER_CHIPS,))
    land, send2, recv2, token = pl.pallas_call(
        body, name=name,
        out_shape=(pltpu.HBM(land.shape, land.dtype), sem3, sem3, jax.ShapeDtypeStruct((8, 128), F32)),
        in_specs=[_HBM, _HBM, _SEM, pl.BlockSpec(memory_space=pl.ANY)],
        out_specs=(_HBM, _SEM, _SEM, pl.BlockSpec(memory_space=pltpu.VMEM)),
        input_output_aliases={1: 0}, compiler_params=pltpu.CompilerParams(has_side_effects=_DATAFLOW),
    )(src, land, recv_sems, after)
    return [send_sems, recv_sems, src, land, send2, recv2], token


def gather_wait(name, handle, after):
    send_sems, recv_sems, src, land, send2, recv2 = handle

    def body(src_ref, land_ref, send_ref, recv_ref, send2_ref, recv2_ref, after_ref, src_out, land_out):
        first = _first_copies(src_ref, land_ref, send_ref, recv_ref)
        for cp in first:
            cp.wait_send()
        first[0].wait_recv()
        for cp in _passed_on_copies(land_ref, send2_ref, recv2_ref):
            cp.wait_send()
            cp.wait_recv()

    return pl.pallas_call(
        body, name=name, out_shape=(pltpu.HBM(src.shape, src.dtype), pltpu.HBM(land.shape, land.dtype)),
        in_specs=[_HBM, _HBM, _SEM, _SEM, _SEM, _SEM, pl.BlockSpec(memory_space=pl.ANY)], out_specs=(_HBM, _HBM),
        input_output_aliases={0: 0, 1: 1}, compiler_params=pltpu.CompilerParams(has_side_effects=_DATAFLOW),
    )(src, land, send_sems, recv_sems, send2, recv2, after)


def adamw_reduce(name, parts, w, m, v):
    nl, r, wd = w.shape
    tr = _row_tile(r, 2 * ROW_TILE_BYTES // (8 * wd))

    def body(*refs):
        p_refs = refs[:nl]
        w_ref, m_ref, v_ref, g_ref, d_ref, nm_ref, nv_ref = refs[nl:]
        layer = pl.program_id(0)
        for j in range(nl):
            @pl.when(layer == j)
            def _(j=j):
                g = p_refs[j][0].astype(F32)
                for dev in range(1, N_DEV):
                    g = g + p_refs[j][dev].astype(F32)
                nm = B1 * m_ref[...] + (1.0 - B1) * g
                nv = B2 * v_ref[...] + (1.0 - B2) * jnp.square(g)
                m_hat = nm / (1.0 - B1 ** STEP)
                v_hat = nv / (1.0 - B2 ** STEP)
                g_ref[...] = g
                d_ref[...] = -LR * (m_hat / (jnp.sqrt(v_hat) + ADAM_EPS) + WD * w_ref[...])
                nm_ref[...] = nm
                nv_ref[...] = nv

    def part_spec(j):
        return pl.BlockSpec((N_DEV, tr, wd), lambda l, i: (0, jnp.where(l == j, i, 0), 0))

    spec = pl.BlockSpec((None, tr, wd), lambda l, i: (l, i, 0))
    return pl.pallas_call(
        body, name=name, grid=(nl, r // tr),
        in_specs=[part_spec(j) for j in range(nl)] + [spec, spec, spec],
        out_specs=[spec] * 4, out_shape=[jax.ShapeDtypeStruct((nl, r, wd), F32)] * 4,
        compiler_params=_params(("arbitrary", "arbitrary")),
    )(*parts, w, m, v)


def _pack_rows(vectors, rows=None):
    flat = jnp.concatenate([a.reshape(-1).astype(F32) for a in vectors])
    n = flat.shape[0]
    if rows is None:
        rows = -(-n // 1024) * 8
    return jnp.pad(flat, (0, rows * 128 - n)).reshape(rows, 128)


def _unpack_rows(packed, like):
    flat = packed.reshape(-1)
    out, pos = [], 0
    for a in like:
        out.append(flat[pos:pos + a.size].reshape(a.shape))
        pos += a.size
    return out


def kernel(x, p, mix_norm, mlp_norm, ple_norm, w_a_in, a_lb_logits, a_head_gain, w_a_out, kv_norm, w_kvf, b_f, w_b_q, w_b_out, w_mlp_up, w_mlp_down, w_ple_gate, w_ple_up, final_norm, loss_target, m_mix_norm, m_mlp_norm, m_ple_norm, m_w_a_in, m_a_lb_logits, m_a_head_gain, m_w_a_out, m_kv_norm, m_w_kvf, m_b_f, m_w_b_q, m_w_b_out, m_w_mlp_up, m_w_mlp_down, m_w_ple_gate, m_w_ple_up, m_final_norm, v_mix_norm, v_mlp_norm, v_ple_norm, v_w_a_in, v_a_lb_logits, v_a_head_gain, v_w_a_out, v_kv_norm, v_w_kvf, v_b_f, v_w_b_q, v_w_b_out, v_w_mlp_up, v_w_mlp_down, v_w_ple_gate, v_w_ple_up, v_final_norm):
    t, d = x.shape[1], x.shape[2]
    nh = d // HEAD_DIM
    n_layers = 2
    x2 = x.reshape(t, d)
    target = loss_target.reshape(t, d)
    me = _my_index()

    shards = {"w_a_in": w_a_in[0], "w_a_out": w_a_out[0], "w_kvf": w_kvf, "w_b_q": w_b_q[0], "w_b_out": w_b_out[0]}
    for l in range(n_layers):
        shards.update({f"w_mlp_up{l}": w_mlp_up[l], f"w_mlp_down{l}": w_mlp_down[l],
                       f"w_ple_gate{l}": w_ple_gate[l], f"w_ple_up{l}": w_ple_up[l]})
    first_use = ["a_lb_logits", "w_a_in", "w_a_out", "w_mlp_up0", "w_mlp_down0", "w_ple_gate0", "w_ple_up0", "w_kvf",
                 "w_b_q", "w_b_out", "w_mlp_up1", "w_mlp_down1", "w_ple_gate1", "w_ple_up1"]
    row_sharded = ("w_a_out", "w_b_q", "w_b_out", "w_mlp_down", "w_ple_gate")
    shards_bf = [a_lb_logits] + [shards[n].astype(BF16) for n in first_use[1:]]
    ag_handles = gather_start("ag_start", shards_bf)
    passed_on = {}
    weights = {}

    def pass_on(j, after):
        if j < len(first_use) and j not in passed_on:
            passed_on[j] = gather_pass_on("ag_pass_" + first_use[j], ag_handles[j], after)

    def weight(name, after=None):
        if name not in weights:
            j = first_use.index(name)
            pass_on(j, after)
            pass_on(j + 1, after)
            behind = passed_on[j + 1][1] if j + 1 in passed_on else after
            own, land = gather_wait("ag_wait_" + name, passed_on[j][0], behind)
            g = lax.dynamic_update_slice(land, own[None], (me, 0, 0))
            if name.rstrip("01") in row_sharded:
                g = g.reshape(1, g.shape[0] * g.shape[1], g.shape[2])
            weights[name] = g
        return weights[name]

    lgt = weight("a_lb_logits", x2).transpose(1, 0, 2).reshape(2, d)
    p_bf = [p[l, 0].astype(BF16) for l in range(n_layers)]

    def row(vec):
        return vec.reshape(1, -1)

    def mlp_ple_fwd(l, h_in, a):
        (h_a, u_mlp), _ = rowwise(f"add_norm_mlp{l}", _add_norm_fwd, [h_in, a], [row(mlp_norm[l])])
        pre = mm_nn(f"mlp_up{l}", u_mlp, weight(f"w_mlp_up{l}", u_mlp))
        (act,), _ = rowwise(f"relu2_{l}", _relu2_fwd, [pre])
        mo = mm_nn(f"mlp_down{l}", act, weight(f"w_mlp_down{l}", act))
        (h_b, u_ple), _ = rowwise(f"add_norm_ple{l}", _add_norm_fwd, [h_a, mo], [row(ple_norm[l])])
        gpre = mm_nn(f"ple_gate{l}", u_ple, weight(f"w_ple_gate{l}", u_ple))
        pu = mm_nn(f"ple_up{l}", p_bf[l], weight(f"w_ple_up{l}", gpre))
        return dict(h_a=h_a, u_mlp=u_mlp, pre=pre, act=act, h_b=h_b, u_ple=u_ple, gpre=gpre, pu=pu)

    (u0,), _ = rowwise("norm_mix0", _norm_fwd, [x2], [row(mix_norm[0])])
    z = mm_nn("a_in", u0, weight("w_a_in", u0))
    og, states = hgrn_fwd(z, lgt, a_head_gain)
    a0 = mm_nn("a_out", og, weight("w_a_out", og))
    s0 = mlp_ple_fwd(0, x2, a0)
    (h3, u_kv, u1), _ = rowwise("ple_norms", _ple_two_norms_fwd, [s0["h_b"], s0["gpre"], s0["pu"]],
                                [row(kv_norm), row(mix_norm[1])])
    hk = mm_nn("kvf", u_kv, weight("w_kvf", u_kv), out3=True)
    hk = hk.transpose(1, 0, 2).reshape(t, -1)
    k_bf, v_bf = hk[:, :d].astype(BF16), hk[:, d:2 * d].astype(BF16)
    fl_t = hk[:, 2 * d:].T
    b_f_col = b_f.reshape(nh, 1)
    dcum = decay_fwd(fl_t, b_f_col)
    dcol, drow = dcum.reshape(nh, t, 1), dcum.reshape(nh, 1, t)
    q = mm_nn("b_q", u1, weight("w_b_q", dcum))
    o, lse = attn_fwd(q, k_bf, v_bf, dcol, drow)
    a1 = mm_nn("b_out", o, weight("w_b_out", o))
    s1 = mlp_ple_fwd(1, h3, a1)

    (dh, dgpre, dpu), (d_final, loss_rows) = rowwise(
        "tail", _tail_fwd_bwd, [s1["h_b"], s1["gpre"], s1["pu"], target], [row(final_norm)])
    loss = lax.psum(loss_rows[0, 0], MESH_AXES)

    sent = {}
    tokens = []

    def send_grad(name, g):
        g = g.reshape(N_DEV, -1, g.shape[-1])
        (handle,), token = exchange_start("rs_start_" + name, [g], False)
        sent[name] = handle
        tokens.append(token)

    def after_sends():
        deps = tuple(tokens)
        tokens.clear()
        return deps

    def mlp_ple_bwd(l, s, dh, dgpre, dpu):
        send_grad(f"w_ple_gate{l}", mm_tn(f"d_ple_gate_w{l}", s["u_ple"], dgpre, 1, deps=after_sends()))
        send_grad(f"w_ple_up{l}", mm_tn(f"d_ple_up_w{l}", p_bf[l], dpu, N_DEV, deps=after_sends()))
        du = mm_nt(f"d_ple_gate_x{l}", dgpre, weight(f"w_ple_gate{l}"), deps=after_sends())
        (dh, dh_bf), (d_ple,) = rowwise(f"d_norm_ple{l}", _norm_bwd, [s["h_b"], du, dh], [row(ple_norm[l])])
        send_grad(f"w_mlp_down{l}", mm_tn(f"d_mlp_down_w{l}", s["act"], dh_bf, 1))
        dact = mm_nt(f"d_mlp_down_x{l}", dh_bf, weight(f"w_mlp_down{l}"), deps=after_sends())
        (dpre,), _ = rowwise(f"d_relu2_{l}", _relu2_bwd, [s["pre"], dact])
        send_grad(f"w_mlp_up{l}", mm_tn(f"d_mlp_up_w{l}", s["u_mlp"], dpre, N_DEV))
        du = mm_nt(f"d_mlp_up_x{l}", dpre, weight(f"w_mlp_up{l}"), deps=after_sends())
        (dh, dh_bf), (d_mlp,) = rowwise(f"d_norm_mlp{l}", _norm_bwd, [s["h_a"], du, dh], [row(mlp_norm[l])])
        return dh, dh_bf, d_ple, d_mlp

    dh, dh_bf, d_ple1, d_mlp1 = mlp_ple_bwd(1, s1, dh, dgpre, dpu)
    send_grad("w_b_out", mm_tn("d_b_out_w", o, dh_bf, 1))
    do = mm_nt("d_b_out_x", dh_bf, weight("w_b_out"), out_dtype=BF16, deps=after_sends())
    dq, dk, dv, ddrow = attn_bwd(q, k_bf, v_bf, dcol, drow, lse, do)
    send_grad("w_b_q", mm_tn("d_b_q_w", u1, dq, 1))
    du1 = mm_nt("d_b_q_x", dq, weight("w_b_q"), deps=after_sends())
    dfl_t, d_b_f = decay_bwd(fl_t, b_f_col, ddrow.reshape(nh, t))
    dhk = jnp.concatenate([dk, dv, dfl_t.T.astype(BF16)], axis=1)
    dhk = dhk.reshape(t, N_DEV, -1).transpose(1, 0, 2)
    send_grad("w_kvf", mm_tn("d_kvf_w", u_kv, dhk, N_DEV))
    du_kv = mm_nt("d_kvf_x", dhk, weight("w_kvf"), deps=after_sends())
    (dh,), (d_kv_norm, d_mix1) = rowwise("d_ple_norms", _two_norms_bwd, [h3, du_kv, du1, dh],
                                         [row(kv_norm), row(mix_norm[1])])
    (dgpre, dpu), _ = rowwise("d_ple0", _ple_bwd, [s0["gpre"], s0["pu"], dh])
    dh, dh_bf, d_ple0, d_mlp0 = mlp_ple_bwd(0, s0, dh, dgpre, dpu)
    send_grad("w_a_out", mm_tn("d_a_out_w", og, dh_bf, 1))
    dog = mm_nt("d_a_out_x", dh_bf, weight("w_a_out"), deps=after_sends())
    dz4, d_lgt, d_hg = hgrn_bwd(z, lgt, a_head_gain, states, dog)
    dz = dz4.transpose(1, 0, 2).reshape(t, 4 * d)
    send_grad("w_a_in", mm_tn("d_a_in_w", u0, dz, N_DEV))
    du0 = mm_nt("d_a_in_x", dz, weight("w_a_in"), deps=after_sends())
    (dx, _), (d_mix0,) = rowwise("d_norm_mix0", _norm_bwd, [x2, du0, dh], [row(mix_norm[0])])

    new = {}
    last = [dx]

    def update(name, parts, w, m, v):
        shp = w.shape
        w3, m3, v3 = (a.reshape(len(parts), -1, shp[-1]) for a in (w, m, v))
        new[name] = tuple(a.reshape(shp) for a in adamw_reduce("adamw_" + name, parts, w3, m3, v3))
        last[0] = new[name][0]

    def receive_update(name, layers, w, m, v):
        parts = {}
        for sfx in layers:
            g, land = exchange_wait(f"rs_wait_{name}{sfx}", sent[name + sfx], last[0], False)
            parts[sfx] = lax.dynamic_update_slice(land, lax.dynamic_slice_in_dim(g, me, 1, 0), (me, 0, 0))
        update(name, [parts[sfx] for sfx in sorted(layers)], w, m, v)

    both = ("1", "0")
    receive_update("w_b_out", ("",), w_b_out, m_w_b_out, v_w_b_out)
    receive_update("w_b_q", ("",), w_b_q, m_w_b_q, v_w_b_q)
    receive_update("w_kvf", ("",), w_kvf, m_w_kvf, v_w_kvf)
    receive_update("w_ple_gate", both, w_ple_gate, m_w_ple_gate, v_w_ple_gate)
    receive_update("w_ple_up", both, w_ple_up, m_w_ple_up, v_w_ple_up)
    receive_update("w_mlp_down", both, w_mlp_down, m_w_mlp_down, v_w_mlp_down)
    receive_update("w_mlp_up", both, w_mlp_up, m_w_mlp_up, v_w_mlp_up)
    receive_update("w_a_out", ("",), w_a_out, m_w_a_out, v_w_a_out)
    receive_update("w_a_in", ("",), w_a_in, m_w_a_in, v_w_a_in)

    small = dict(mix_norm=jnp.concatenate([d_mix0, d_mix1]), mlp_norm=jnp.concatenate([d_mlp0, d_mlp1]),
                 ple_norm=jnp.concatenate([d_ple0, d_ple1]), a_head_gain=d_hg, kv_norm=d_kv_norm.reshape(d),
                 b_f=d_b_f.reshape(nh), final_norm=d_final.reshape(d))
    small_w = dict(mix_norm=(mix_norm, m_mix_norm, v_mix_norm), mlp_norm=(mlp_norm, m_mlp_norm, v_mlp_norm),
                   ple_norm=(ple_norm, m_ple_norm, v_ple_norm),
                   a_head_gain=(a_head_gain, m_a_head_gain, v_a_head_gain), kv_norm=(kv_norm, m_kv_norm, v_kv_norm),
                   b_f=(b_f, m_b_f, v_b_f), final_norm=(final_norm, m_final_norm, v_final_norm))
    names = list(small)
    packed = _pack_rows([d_lgt] + [small[n] for n in names])
    everyone = all_gather("ag_small_grads", packed, deps=(last[0],))
    n_lgt_rows = d_lgt.size // 128
    lgt_parts = everyone[:, :n_lgt_rows].reshape(N_DEV, 2, d)
    lgt_parts = lax.dynamic_slice_in_dim(lgt_parts, me * a_lb_logits.shape[1], a_lb_logits.shape[1], axis=2)
    update("a_lb_logits", [lgt_parts], a_lb_logits, m_a_lb_logits, v_a_lb_logits)
    rest = everyone[:, n_lgt_rows:]
    like = [small_w[n][0] for n in names]
    packed_w, packed_m, packed_v = (_pack_rows([small_w[n][j] for n in names], rest.shape[1])[None] for j in range(3))
    outs = adamw_reduce("adamw_small", [rest], packed_w, packed_m, packed_v)
    unpacked = [_unpack_rows(a, like) for a in outs]
    for j, n in enumerate(names):
        new[n] = tuple(unpacked[q][j] for q in range(4))

    order = ["mix_norm", "mlp_norm", "ple_norm", "w_a_in", "a_lb_logits", "a_head_gain", "w_a_out", "kv_norm",
             "w_kvf", "b_f", "w_b_q", "w_b_out", "w_mlp_up", "w_mlp_down", "w_ple_gate", "w_ple_up", "final_norm"]
    result = [loss, dx.reshape(x.shape)]
    for j in range(4):
        result += [new[n][j] for n in order]
    return tuple(result)
```

```python
import functools

import jax
import jax.numpy as jnp
from jax import lax
from jax.experimental import pallas as pl
from jax.experimental.pallas import tpu as pltpu

F32 = jnp.float32
BF16 = jnp.bfloat16
HEAD_DIM = 128
CHUNK = 16
TILE = 128
HEADS_PER_STEP = 2
NORM_EPS = 1e-6
N_DEV = 8
MESH_AXES = ("x", "y", "c")
VMEM_LIMIT_BYTES = 48 * 1024 * 1024
ROW_TILE_BYTES = 1024 * 1024
LR, B1, B2, ADAM_EPS, WD, STEP = 0.001, 0.9, 0.999, 1e-08, 0.01, 10
NEG_BIG = -1e30

NN = (((1,), (0,)), ((), ()))
NT = (((1,), (1,)), ((), ()))
TN = (((0,), (0,)), ((), ()))


def _params(semantics):
    return pltpu.CompilerParams(dimension_semantics=semantics, vmem_limit_bytes=VMEM_LIMIT_BYTES)


def _tile(n, prefs):
    for p in prefs:
        if n % p == 0:
            return p
    return n


def _row_tile(rows, limit):
    for cand in (2048, 1024, 512, 256, 128, 64, 32, 16):
        if cand <= limit and rows % cand == 0:
            return cand
    return rows


def _mm_call(name, a, b, dims, grid, a_spec, b_spec, o_spec, o_shape, acc_shape, k_axes, out_dtype, deps=(),
             fuse=None):
    nk = 1
    for ax in k_axes:
        nk *= grid[ax]
    fn, extra, out_dtypes = fuse if fuse else (lambda acc: (acc,), (), (out_dtype,))
    n_extra, n_out = len(extra), len(out_dtypes)

    def finish(acc, rest):
        o_refs = rest[n_extra + len(deps):n_extra + len(deps) + n_out]
        for ref, val in zip(o_refs, fn(acc, *[r[...] for r in rest[:n_extra]])):
            ref[...] = val.astype(ref.dtype)

    def one_step(a_ref, b_ref, *rest):
        finish(lax.dot_general(a_ref[...], b_ref[...], dims, preferred_element_type=F32), rest)

    def accumulate(a_ref, b_ref, *rest):
        acc_ref = rest[-1]
        k = 0
        for ax in k_axes:
            k = k * grid[ax] + pl.program_id(ax)
        part = lax.dot_general(a_ref[...], b_ref[...], dims, preferred_element_type=F32)

        @pl.when(k == 0)
        def _():
            acc_ref[...] = part

        @pl.when((k > 0) & (k < nk - 1))
        def _():
            acc_ref[...] += part

        @pl.when(k == nk - 1)
        def _():
            finish(acc_ref[...] + part, rest)

    sem = tuple("arbitrary" if ax in k_axes else "parallel" for ax in range(len(grid)))
    outs = pl.pallas_call(
        one_step if nk == 1 else accumulate, name=name, grid=grid,
        in_specs=[a_spec, b_spec] + [o_spec] * n_extra + [pl.BlockSpec(memory_space=pl.ANY)] * len(deps),
        out_specs=[o_spec] * n_out, out_shape=[jax.ShapeDtypeStruct(o_shape, dt) for dt in out_dtypes],
        scratch_shapes=[] if nk == 1 else [pltpu.VMEM(acc_shape, F32)], compiler_params=_params(sem),
    )(a, b, *extra, *deps)
    return outs if fuse else outs[0]


def mm_nn(name, a, b3, out_dtype=F32, out3=False, deps=(), fuse=None):
    m, k = a.shape
    g, _, n = b3.shape
    tm, tk = _tile(m, (1024, 512, 256)), _tile(k, (2048, 1024, 512, 256))
    tn = n if out3 else _tile(n, (1024, 512, 256, 128))
    nj = n // tn
    grid = (m // tm, g, nj, k // tk)
    a_spec = pl.BlockSpec((tm, tk), lambda i, gg, j, kk: (i, kk))
    b_spec = pl.BlockSpec((None, tk, tn), lambda i, gg, j, kk: (gg, kk, j))
    if out3:
        o_spec = pl.BlockSpec((None, tm, tn), lambda i, gg, j, kk: (gg, i, j))
        o_shape = (g, m, n)
    else:
        o_spec = pl.BlockSpec((tm, tn), lambda i, gg, j, kk: (i, gg * nj + j))
        o_shape = (m, g * n)
    return _mm_call(name, a, b3, NN, grid, a_spec, b_spec, o_spec, o_shape, (tm, tn), (3,), out_dtype, deps, fuse)


def mm_nt(name, a, b3, out_dtype=F32, deps=(), fuse=None):
    g, k, n = b3.shape
    a3 = a.ndim == 3
    m = a.shape[1] if a3 else a.shape[0]
    tm, tko = _tile(m, (1024, 512, 256)), _tile(k, (1024, 512, 256))
    tc = n if a3 else _tile(n, (2048, 1024, 512, 256, 128))
    nc = n // tc
    grid = (m // tm, k // tko, g, nc)
    if a3:
        a_spec = pl.BlockSpec((None, tm, tc), lambda i, j, gg, c: (gg, i, c))
    else:
        a_spec = pl.BlockSpec((tm, tc), lambda i, j, gg, c: (i, gg * nc + c))
    b_spec = pl.BlockSpec((None, tko, tc), lambda i, j, gg, c: (gg, j, c))
    o_spec = pl.BlockSpec((tm, tko), lambda i, j, gg, c: (i, j))
    return _mm_call(name, a, b3, NT, grid, a_spec, b_spec, o_spec, (m, k), (tm, tko), (2, 3), out_dtype, deps, fuse)


def mm_tn(name, a, b, g, out_dtype=BF16, deps=()):
    t, k = a.shape
    b3 = b.ndim == 3
    n = b.shape[2] if b3 else b.shape[1] // g
    tm = _tile(k, (1024, 512, 256))
    tn = n if b3 else _tile(n, (1024, 512, 256, 128))
    tt = _tile(t, (2048, 1024, 512, 256))
    nj = n // tn
    grid = (g, k // tm, nj, t // tt)
    a_spec = pl.BlockSpec((tt, tm), lambda gg, i, j, s: (s, i))
    if b3:
        b_spec = pl.BlockSpec((None, tt, tn), lambda gg, i, j, s: (gg, s, j))
    else:
        b_spec = pl.BlockSpec((tt, tn), lambda gg, i, j, s: (s, gg * nj + j))
    o_spec = pl.BlockSpec((None, tm, tn), lambda gg, i, j, s: (gg, i, j))
    return _mm_call(name, a, b, TN, grid, a_spec, b_spec, o_spec, (g, k, n), (tm, tn), (3,), out_dtype, deps)


def rowwise(name, fn, rows, vecs=()):
    t = rows[0].shape[0]
    wmax = max(r.shape[1] for r in rows)
    tr = _row_tile(t, ROW_TILE_BYTES // (4 * wmax))
    row_s = [jax.ShapeDtypeStruct((tr, r.shape[1]), r.dtype) for r in rows]
    vec_s = [jax.ShapeDtypeStruct(v.shape, v.dtype) for v in vecs]
    out_rows_s, out_sums_s = jax.eval_shape(fn, *row_s, *vec_s)
    n_in, n_r = len(rows) + len(vecs), len(out_rows_s)

    def body(*refs):
        i = pl.program_id(0)
        o_rows, o_sums = fn(*[r[...] for r in refs[:n_in]])
        for ref, val in zip(refs[n_in:n_in + n_r], o_rows):
            ref[...] = val

        if out_sums_s:
            @pl.when(i == 0)
            def _():
                for ref in refs[n_in + n_r:]:
                    ref[...] = jnp.zeros_like(ref)

            for ref, val in zip(refs[n_in + n_r:], o_sums):
                ref[...] += val

    in_specs = [pl.BlockSpec((tr, r.shape[1]), lambda i: (i, 0)) for r in rows]
    in_specs += [pl.BlockSpec(v.shape, lambda i: (0, 0)) for v in vecs]
    out_specs = [pl.BlockSpec((tr, s.shape[1]), lambda i: (i, 0)) for s in out_rows_s]
    out_specs += [pl.BlockSpec(s.shape, lambda i: (0, 0)) for s in out_sums_s]
    out_shape = [jax.ShapeDtypeStruct((t, s.shape[1]), s.dtype) for s in out_rows_s]
    out_shape += [jax.ShapeDtypeStruct(s.shape, s.dtype) for s in out_sums_s]
    outs = pl.pallas_call(
        body, name=name, grid=(t // tr,), in_specs=in_specs, out_specs=out_specs, out_shape=out_shape,
        compiler_params=_params(("arbitrary",)),
    )(*rows, *vecs)
    return outs[:n_r], outs[n_r:]


def _rms(x, gain):
    return x * lax.rsqrt(jnp.mean(x * x, axis=-1, keepdims=True) + NORM_EPS) * gain


def _norm_fwd(x, gain):
    return (_rms(x, gain).astype(BF16),), ()


def _add_norm_fwd(h, a, gain):
    h = h + a
    return (h, _rms(h, gain).astype(BF16)), ()


def _relu2(pre):
    r = jnp.maximum(pre, 0.0)
    return pre, r * r


def _ple(h, gpre, pu):
    return h + pu * jax.nn.sigmoid(gpre)


def _ple_two_norms_fwd(h, gpre, pu, gain_a, gain_b):
    h = _ple(h, gpre, pu)
    return (h, _rms(h, gain_a).astype(BF16), _rms(h, gain_b).astype(BF16)), ()


def _tail_fwd_bwd(h, gpre, pu, target, gain):
    def row_loss(h, gpre, pu, gain):
        y = _rms(_ple(h, gpre, pu), gain)
        return 0.5 * jnp.mean(jnp.square(y - target), axis=-1, keepdims=True)

    loss, vjp = jax.vjp(row_loss, h, gpre, pu, gain)
    dh, dgpre, dpu, dgain = vjp(jnp.ones_like(loss))
    loss = jnp.broadcast_to(jnp.sum(loss, axis=0, keepdims=True), (1, 128))
    return (dh, dgpre.astype(BF16), dpu.astype(BF16)), (dgain, loss)


def _ple_bwd(gpre, pu, dh):
    _, vjp = jax.vjp(lambda g, u: pu_times_gate(g, u), gpre, pu)
    dgpre, dpu = vjp(dh)
    return (dgpre.astype(BF16), dpu.astype(BF16)), ()


def pu_times_gate(gpre, pu):
    return pu * jax.nn.sigmoid(gpre)


def _norm_bwd(h, du, dh_in, gain):
    _, vjp = jax.vjp(_rms, h, gain)
    dh, dgain = vjp(du)
    dh = dh_in + dh
    return (dh, dh.astype(BF16)), (dgain,)


def _two_norms_bwd(h, du_a, du_b, dh_in, gain_a, gain_b):
    _, vjp = jax.vjp(lambda h, ga, gb: (_rms(h, ga), _rms(h, gb)), h, gain_a, gain_b)
    dh, dga, dgb = vjp((du_a, du_b))
    return (dh_in + dh,), (dga, dgb)


def _relu2_bwd(dact, pre):
    return (dact * 2.0 * jnp.maximum(pre.astype(F32), 0.0),)


def _bf16_dot(dims_fwd, dims_da, dims_db, swap_da, swap_db):
    @jax.custom_vjp
    def dot(a, b):
        return lax.dot_general(a.astype(BF16), b.astype(BF16), dims_fwd, preferred_element_type=F32)

    def fwd(a, b):
        return dot(a, b), (a, b)

    def bwd(res, ct):
        a, b = res
        ct, a, b = ct.astype(BF16), a.astype(BF16), b.astype(BF16)
        da = lax.dot_general(*((b, ct) if swap_da else (ct, b)), dims_da, preferred_element_type=F32)
        db = lax.dot_general(*((ct, a) if swap_db else (a, ct)), dims_db, preferred_element_type=F32)
        return da, db

    dot.defvjp(fwd, bwd)
    return dot


_dot_nn = _bf16_dot(NN, NT, TN, False, False)
_dot_nt = _bf16_dot(NT, NN, TN, False, True)
_dot_tn = _bf16_dot(TN, NT, NN, True, False)


def _chunk_masks(transposed):
    r = lax.broadcasted_iota(jnp.int32, (TILE, TILE), 0)
    c = lax.broadcasted_iota(jnp.int32, (TILE, TILE), 1)
    same = (r // CHUNK) == (c // CHUNK)
    causal = same & ((r <= c) if transposed else (c <= r))
    return causal, same


def _chunk_scan(x, reverse):
    pos = lax.broadcasted_iota(jnp.int32, x.shape, 0) % CHUNK
    step = 1
    while step < CHUNK:
        if reverse:
            x = x + jnp.where(pos < CHUNK - step, pltpu.roll(x, x.shape[0] - step, axis=0), 0.0)
        else:
            x = x + jnp.where(pos >= step, pltpu.roll(x, step, axis=0), 0.0)
        step *= 2
    return x


def _chunk_total(x):
    return _chunk_scan(x, False) + _chunk_scan(x, True) - x


@jax.custom_vjp
def _chunk_sums(x):
    return _chunk_scan(x, False), _chunk_total(x)


def _chunk_sums_fwd(x):
    return _chunk_sums(x), None


def _chunk_sums_bwd(_, ct):
    return (_chunk_scan(ct[0], True) + _chunk_total(ct[1]),)


_chunk_sums.defvjp(_chunk_sums_fwd, _chunk_sums_bwd)


def _hgrn_tile(q, f, i, g, lgt, hg, st):
    d = q.shape[1]
    l0, l1 = lgt[0:1], lgt[1:2]
    mx = jnp.maximum(l0, l1)
    e0, e1 = jnp.exp(l0 - mx), jnp.exp(l1 - mx)
    lb = e0 / (e0 + e1)
    fg = lb + (1.0 - lb) * jax.nn.sigmoid(f)
    k = 1.0 - fg
    causal, _ = _chunk_masks(False)
    b, b_last = _chunk_sums(jnp.log(fg))
    q_in = q * jax.nn.sigmoid(q) * (d ** -0.5) * jnp.exp(b)
    k_in = k * jnp.exp(-b)
    k_end = k * jnp.exp(b_last - b)
    att = jnp.where(causal, _dot_nt(q_in, k_in), 0.0)
    o_intra = _dot_nn(att, i)
    n_chunks = TILE // CHUNK
    chunk_of_row = lax.broadcasted_iota(jnp.int32, (TILE, 1), 0) // CHUNK

    def spread(a):
        return jnp.concatenate([jnp.where(chunk_of_row == n, a, 0.0) for n in range(n_chunks)], axis=1)

    increments = _dot_tn(i, spread(k_end))
    states = []
    for n in range(n_chunks):
        states.append(st)
        decay = jnp.exp(jnp.mean(b_last[n * CHUNK:(n + 1) * CHUNK], axis=0, keepdims=True))
        st = st * decay + increments[:, n * d:(n + 1) * d]
    o = o_intra + _dot_nt(spread(q_in), jnp.concatenate(states, axis=1))
    o = o * lax.rsqrt(jnp.mean(o * o, axis=-1, keepdims=True) + NORM_EPS) * hg
    return o * (g * jax.nn.sigmoid(g)), st


def hgrn_fwd(z, lgt, hg):
    t, d4 = z.shape
    d = d4 // 4
    nh, nt = d // HEAD_DIM, t // TILE
    hp = HEADS_PER_STEP
    wide = hp * HEAD_DIM

    def body(q_ref, f_ref, i_ref, g_ref, lgt_ref, hg_ref, o_ref, st_out_ref, st_ref):
        tt = pl.program_id(1)

        @pl.when(tt == 0)
        def _():
            st_ref[...] = jnp.zeros_like(st_ref)

        for hh in range(hp):
            cols = slice(hh * HEAD_DIM, (hh + 1) * HEAD_DIM)
            st = st_ref[hh]
            st_out_ref[hh] = st
            o, st = _hgrn_tile(q_ref[:, cols], f_ref[:, cols], i_ref[:, cols], g_ref[:, cols], lgt_ref[:, cols],
                               hg_ref[...], st)
            o_ref[:, cols] = o.astype(o_ref.dtype)
            st_ref[hh] = st

    def part(p):
        return pl.BlockSpec((TILE, wide), lambda h, tt: (tt, p * (nh // hp) + h))

    return pl.pallas_call(
        body, name="hgrn_fwd", grid=(nh // hp, nt),
        in_specs=[part(0), part(1), part(2), part(3),
                  pl.BlockSpec((2, wide), lambda h, tt: (0, h)),
                  pl.BlockSpec((1, HEAD_DIM), lambda h, tt: (0, 0))],
        out_specs=[pl.BlockSpec((TILE, wide), lambda h, tt: (tt, h)),
                   pl.BlockSpec((hp, None, HEAD_DIM, HEAD_DIM), lambda h, tt: (h, tt, 0, 0))],
        out_shape=[jax.ShapeDtypeStruct((t, d), BF16),
                   jax.ShapeDtypeStruct((nh, nt, HEAD_DIM, HEAD_DIM), F32)],
        scratch_shapes=[pltpu.VMEM((hp, HEAD_DIM, HEAD_DIM), F32)],
        compiler_params=_params(("parallel", "arbitrary")),
    )(z, z, z, z, lgt, hg)


def hgrn_bwd(z, lgt, hg, states, dout):
    t, d4 = z.shape
    d = d4 // 4
    nh, nt = d // HEAD_DIM, t // TILE
    hp = HEADS_PER_STEP
    wide = hp * HEAD_DIM

    def body(q_ref, f_ref, i_ref, g_ref, lgt_ref, hg_ref, st_in_ref, do_ref, dz_ref, dlgt_ref, dhg_ref, dst_ref):
        h, tt = pl.program_id(0), pl.program_id(1)

        @pl.when(tt == 0)
        def _():
            dst_ref[...] = jnp.zeros_like(dst_ref)
            dlgt_ref[...] = jnp.zeros_like(dlgt_ref)

        @pl.when((tt == 0) & (h == 0))
        def _():
            dhg_ref[...] = jnp.zeros_like(dhg_ref)

        for hh in range(hp):
            cols = slice(hh * HEAD_DIM, (hh + 1) * HEAD_DIM)
            _, vjp = jax.vjp(_hgrn_tile, q_ref[:, cols], f_ref[:, cols], i_ref[:, cols], g_ref[:, cols],
                             lgt_ref[:, cols], hg_ref[...], st_in_ref[hh])
            grads = vjp((do_ref[:, cols], dst_ref[hh]))
            for p in range(4):
                dz_ref[p, :, cols] = grads[p].astype(dz_ref.dtype)
            dlgt_ref[:, cols] += grads[4]
            dhg_ref[...] += grads[5]
            dst_ref[hh] = grads[6]

    def part(p):
        return pl.BlockSpec((TILE, wide), lambda h, tt: (nt - 1 - tt, p * (nh // hp) + h))

    return pl.pallas_call(
        body, name="hgrn_bwd", grid=(nh // hp, nt),
        in_specs=[part(0), part(1), part(2), part(3),
                  pl.BlockSpec((2, wide), lambda h, tt: (0, h)),
                  pl.BlockSpec((1, HEAD_DIM), lambda h, tt: (0, 0)),
                  pl.BlockSpec((hp, None, HEAD_DIM, HEAD_DIM), lambda h, tt: (h, nt - 1 - tt, 0, 0)),
                  pl.BlockSpec((TILE, wide), lambda h, tt: (nt - 1 - tt, h))],
        out_specs=[pl.BlockSpec((4, TILE, wide), lambda h, tt: (0, nt - 1 - tt, h)),
                   pl.BlockSpec((2, wide), lambda h, tt: (0, h)),
                   pl.BlockSpec((1, HEAD_DIM), lambda h, tt: (0, 0))],
        out_shape=[jax.ShapeDtypeStruct((4, t, d), BF16),
                   jax.ShapeDtypeStruct((2, d), F32),
                   jax.ShapeDtypeStruct((1, HEAD_DIM), F32)],
        scratch_shapes=[pltpu.VMEM((hp, HEAD_DIM, HEAD_DIM), F32)],
        compiler_params=_params(("arbitrary", "arbitrary")),
    )(z, z, z, z, lgt, hg, states, dout)


def _log_sigmoid(x):
    return jnp.minimum(x, 0.0) - jnp.log(1.0 + jnp.exp(-jnp.abs(x)))


def decay_fwd(fl_t, b_f):
    nh, t = fl_t.shape

    def body(fl_ref, b_ref, out_ref):
        r = lax.broadcasted_iota(jnp.int32, (128, 128), 0)
        c = lax.broadcasted_iota(jnp.int32, (128, 128), 1)
        upper = (r <= c).astype(F32)
        carry = jnp.zeros((nh, 1), F32)
        for j in range(t // 128):
            cols = slice(j * 128, (j + 1) * 128)
            ls = _log_sigmoid(fl_ref[:, cols] + b_ref[...])
            out_ref[:, cols] = carry + jnp.dot(ls, upper, precision=lax.Precision.HIGHEST,
                                               preferred_element_type=F32)
            carry = carry + jnp.sum(ls, axis=1, keepdims=True)

    return pl.pallas_call(body, name="decay_fwd", out_shape=jax.ShapeDtypeStruct((nh, t), F32),
                          compiler_params=_params(None))(fl_t, b_f)


def decay_bwd(fl_t, b_f, ddcum):
    nh, t = fl_t.shape

    def body(fl_ref, b_ref, dd_ref, dfl_ref, db_ref):
        r = lax.broadcasted_iota(jnp.int32, (128, 128), 0)
        c = lax.broadcasted_iota(jnp.int32, (128, 128), 1)
        lower = (r >= c).astype(F32)
        carry = jnp.zeros((nh, 1), F32)
        db = jnp.zeros((nh, 1), F32)
        for j in reversed(range(t // 128)):
            cols = slice(j * 128, (j + 1) * 128)
            dd = dd_ref[:, cols]
            dls = carry + jnp.dot(dd, lower, precision=lax.Precision.HIGHEST, preferred_element_type=F32)
            carry = carry + jnp.sum(dd, axis=1, keepdims=True)
            dfl = dls * jax.nn.sigmoid(-(fl_ref[:, cols] + b_ref[...]))
            dfl_ref[:, cols] = dfl
            db = db + jnp.sum(dfl, axis=1, keepdims=True)
        db_ref[...] = db

    return pl.pallas_call(body, name="decay_bwd",
                          out_shape=[jax.ShapeDtypeStruct((nh, t), F32), jax.ShapeDtypeStruct((nh, 1), F32)],
                          compiler_params=_params(None))(fl_t, b_f, ddcum)


def _attn_probs_logits(q_ref, k_ref, dcol_ref, drow_ref, i, tq, t):
    qs = (q_ref[...] * (HEAD_DIM ** -0.5)).astype(BF16)
    s = lax.dot_general(qs, k_ref[...], NT, preferred_element_type=F32)
    s = s + dcol_ref[...] - drow_ref[...]
    row = i * tq + lax.broadcasted_iota(jnp.int32, (tq, t), 0)
    col = lax.broadcasted_iota(jnp.int32, (tq, t), 1)
    mask = col <= row
    return qs, jnp.where(mask, s, NEG_BIG), mask


def attn_fwd(q, k, v, dcol, drow):
    t, d = q.shape
    nh = d // HEAD_DIM
    tq = _tile(t, (256, 128))

    def body(q_ref, k_ref, v_ref, dcol_ref, drow_ref, o_ref, lse_ref):
        i = pl.program_id(1)
        _, s, _ = _attn_probs_logits(q_ref, k_ref, dcol_ref, drow_ref, i, tq, t)
        m = jnp.max(s, axis=1, keepdims=True)
        p = jnp.exp(s - m)
        l = jnp.sum(p, axis=1, keepdims=True)
        o = jnp.dot(p.astype(BF16), v_ref[...], preferred_element_type=F32)
        o_ref[...] = (o / l).astype(o_ref.dtype)
        lse_ref[...] = m + jnp.log(l)

    return pl.pallas_call(
        body, name="attn_fwd", grid=(nh, t // tq),
        in_specs=[pl.BlockSpec((tq, HEAD_DIM), lambda h, i: (i, h)),
                  pl.BlockSpec((t, HEAD_DIM), lambda h, i: (0, h)),
                  pl.BlockSpec((t, HEAD_DIM), lambda h, i: (0, h)),
                  pl.BlockSpec((None, tq, 1), lambda h, i: (h, i, 0)),
                  pl.BlockSpec((None, 1, t), lambda h, i: (h, 0, 0))],
        out_specs=[pl.BlockSpec((tq, HEAD_DIM), lambda h, i: (i, h)),
                   pl.BlockSpec((None, tq, 1), lambda h, i: (h, i, 0))],
        out_shape=[jax.ShapeDtypeStruct((t, d), BF16), jax.ShapeDtypeStruct((nh, t, 1), F32)],
        compiler_params=_params(("parallel", "parallel")),
    )(q, k, v, dcol, drow)


def attn_bwd(q, k, v, dcol, drow, lse, do):
    t, d = q.shape
    nh = d // HEAD_DIM
    tq = _tile(t, (256, 128))
    nq = t // tq

    def body(q_ref, k_ref, v_ref, dcol_ref, drow_ref, lse_ref, do_ref,
             dq_ref, dk_ref, dv_ref, ddrow_ref, dk_acc, dv_acc):
        i = pl.program_id(1)

        @pl.when(i == 0)
        def _():
            dk_acc[...] = jnp.zeros_like(dk_acc)
            dv_acc[...] = jnp.zeros_like(dv_acc)
            ddrow_ref[...] = jnp.zeros_like(ddrow_ref)

        qs, s, mask = _attn_probs_logits(q_ref, k_ref, dcol_ref, drow_ref, i, tq, t)
        p = jnp.where(mask, jnp.exp(s - lse_ref[...]), 0.0)
        do = do_ref[...]
        dp = lax.dot_general(do, v_ref[...], NT, preferred_element_type=F32)
        ds = p * (dp - jnp.sum(p * dp, axis=1, keepdims=True))
        dsb = ds.astype(BF16)
        dq = jnp.dot(dsb, k_ref[...], preferred_element_type=F32) * (HEAD_DIM ** -0.5)
        dq_ref[...] = dq.astype(dq_ref.dtype)
        dk_acc[...] += lax.dot_general(dsb, qs, TN, preferred_element_type=F32)
        dv_acc[...] += lax.dot_general(p.astype(BF16), do, TN, preferred_element_type=F32)
        ddrow_ref[...] -= jnp.sum(ds, axis=0, keepdims=True)

        @pl.when(i == nq - 1)
        def _():
            dk_ref[...] = dk_acc[...].astype(dk_ref.dtype)
            dv_ref[...] = dv_acc[...].astype(dv_ref.dtype)

    tile = pl.BlockSpec((tq, HEAD_DIM), lambda h, i: (i, h))
    full = pl.BlockSpec((t, HEAD_DIM), lambda h, i: (0, h))
    col = pl.BlockSpec((None, tq, 1), lambda h, i: (h, i, 0))
    rowv = pl.BlockSpec((None, 1, t), lambda h, i: (h, 0, 0))
    return pl.pallas_call(
        body, name="attn_bwd", grid=(nh, nq),
        in_specs=[tile, full, full, col, rowv, col, tile],
        out_specs=[tile, full, full, rowv],
        out_shape=[jax.ShapeDtypeStruct((t, d), BF16), jax.ShapeDtypeStruct((t, d), BF16),
                   jax.ShapeDtypeStruct((t, d), BF16), jax.ShapeDtypeStruct((nh, 1, t), F32)],
        scratch_shapes=[pltpu.VMEM((t, HEAD_DIM), F32), pltpu.VMEM((t, HEAD_DIM), F32)],
        compiler_params=_params(("parallel", "arbitrary")),
    )(q, k, v, dcol, drow, lse, do)


def _my_index():
    return (lax.axis_index("x") * 2 + lax.axis_index("y")) * 2 + lax.axis_index("c")


def _exchange(name, src, gather, deps=()):
    shape = src.shape if gather else src.shape[1:]

    def body(src_ref, *rest):
        out_ref, send_sems, recv_sems, local_sem = rest[len(deps):]
        x, y, c = (lax.axis_index(a) for a in MESH_AXES)
        me = (x * 2 + y) * 2 + c
        mine = src_ref if gather else src_ref.at[me]
        local = pltpu.make_async_copy(mine, out_ref.at[me], local_sem)
        local.start()
        copies = []
        for dlt in range(1, N_DEV):
            dx, dy, dc = dlt // 4, (dlt // 2) % 2, dlt % 2
            px, py, pc = x ^ dx, y ^ dy, c ^ dc
            peer = (px * 2 + py) * 2 + pc
            copies.append(pltpu.make_async_remote_copy(
                src_ref=src_ref if gather else src_ref.at[peer], dst_ref=out_ref.at[me],
                send_sem=send_sems.at[dlt - 1], recv_sem=recv_sems.at[dlt - 1],
                device_id=(px, py, pc), device_id_type=pl.DeviceIdType.MESH))
        for cp in copies:
            cp.start()
        for cp in copies:
            cp.wait_recv()
        for cp in copies:
            cp.wait_send()
        local.wait()

    return pl.pallas_call(
        body, name=name, out_shape=jax.ShapeDtypeStruct((N_DEV,) + tuple(shape), src.dtype),
        in_specs=[pl.BlockSpec(memory_space=pl.ANY)] * (1 + len(deps)), out_specs=pl.BlockSpec(memory_space=pl.ANY),
        scratch_shapes=[pltpu.SemaphoreType.DMA((N_DEV - 1,)), pltpu.SemaphoreType.DMA((N_DEV - 1,)),
                        pltpu.SemaphoreType.DMA],
        compiler_params=pltpu.CompilerParams(has_side_effects=True),
    )(src, *deps)


def all_gather(name, x, deps=()):
    return _exchange(name, x, True, deps)


_HBM = pl.BlockSpec(memory_space=pltpu.HBM)
_SEM = pl.BlockSpec(memory_space=pltpu.SEMAPHORE)
_DATAFLOW = pltpu.SideEffectType.DATAFLOW_SIDE_EFFECTING


def _peer_copies(src_ref, land_ref, send_sems, recv_sems, gather):
    x, y, c = (lax.axis_index(a) for a in MESH_AXES)
    me = (x * 2 + y) * 2 + c
    copies = []
    for dlt in range(1, N_DEV):
        px, py, pc = x ^ (dlt // 4), y ^ ((dlt // 2) % 2), c ^ (dlt % 2)
        peer = (px * 2 + py) * 2 + pc
        copies.append(pltpu.make_async_remote_copy(
            src_ref=src_ref if gather else src_ref.at[peer], dst_ref=land_ref.at[me],
            send_sem=send_sems.at[dlt - 1], recv_sem=recv_sems.at[dlt - 1],
            device_id=(px, py, pc), device_id_type=pl.DeviceIdType.MESH))
    return copies


def exchange_start(name, srcs, gather):
    n = len(srcs)
    lands = [lax.empty((N_DEV,) + tuple(s.shape if gather else s.shape[1:]), s.dtype) for s in srcs]

    def body(*refs):
        src_refs, land_refs = refs[:n], refs[n:2 * n]
        send_sems, recv_sems = refs[2 * n:3 * n], refs[3 * n:4 * n]
        token = refs[-1]
        for j in range(n):
            for cp in _peer_copies(src_refs[j], land_refs[j], send_sems[j], recv_sems[j], gather):
                cp.start()
        token[...] = jnp.zeros_like(token)

    sems = [pltpu.SemaphoreType.DMA((N_DEV - 1,))] * (2 * n)
    thru = [pltpu.HBM(a.shape, a.dtype) for a in list(srcs) + lands]
    outs = pl.pallas_call(
        body, name=name, out_shape=tuple(sems + thru + [jax.ShapeDtypeStruct((8, 128), F32)]),
        in_specs=[_HBM] * (2 * n), out_specs=tuple([_SEM] * (2 * n) + [_HBM] * (2 * n) + [pl.BlockSpec(memory_space=pltpu.VMEM)]),
        input_output_aliases={j: 2 * n + j for j in range(2 * n)},
        compiler_params=pltpu.CompilerParams(has_side_effects=_DATAFLOW),
    )(*[pltpu.with_memory_space_constraint(a, pltpu.HBM) for a in list(srcs) + lands])
    handles = [(outs[j], outs[n + j], outs[2 * n + j], outs[3 * n + j]) for j in range(n)]
    return handles, outs[-1]


def exchange_wait(name, handle, after, gather):
    send_sems, recv_sems, src, land = handle

    def body(src_ref, land_ref, send_ref, recv_ref, after_ref, src_out, land_out):
        for cp in _peer_copies(src_ref, land_ref, send_ref, recv_ref, gather):
            cp.wait_send()
            cp.wait_recv()

    return pl.pallas_call(
        body, name=name, out_shape=(pltpu.HBM(src.shape, src.dtype), pltpu.HBM(land.shape, land.dtype)),
        in_specs=[_HBM, _HBM, _SEM, _SEM, pl.BlockSpec(memory_space=pl.ANY)], out_specs=(_HBM, _HBM),
        input_output_aliases={0: 0, 1: 1},
        compiler_params=pltpu.CompilerParams(has_side_effects=_DATAFLOW),
    )(src, land, send_sems, recv_sems, after)


N_OTHER_CHIPS = 3


def _two_level_places():
    x, y, c = (lax.axis_index(a) for a in MESH_AXES)
    return (x, y, c), (x * 2 + y) * 2 + c, (x, y, 1 - c), [(1 - x, y), (x, 1 - y), (1 - x, 1 - y)]


def _first_copies(land_ref, send_sems, recv_sems):
    (x, y, c), me, other_core, chips = _two_level_places()
    targets = [other_core] + [(cx, cy, c) for cx, cy in chips]
    return [pltpu.make_async_remote_copy(
        src_ref=land_ref.at[me], dst_ref=land_ref.at[me], send_sem=send_sems.at[k], recv_sem=recv_sems.at[k],
        device_id=to, device_id_type=pl.DeviceIdType.MESH) for k, to in enumerate(targets)]


def _passed_on_copies(land_ref, send_sems, recv_sems):
    (x, y, c), me, other_core, chips = _two_level_places()
    copies = []
    for k, (cx, cy) in enumerate(chips):
        slot = land_ref.at[(cx * 2 + cy) * 2 + c]
        copies.append(pltpu.make_async_remote_copy(
            src_ref=slot, dst_ref=slot, send_sem=send_sems.at[k], recv_sem=recv_sems.at[k],
            device_id=other_core, device_id_type=pl.DeviceIdType.MESH))
    return copies


def gather_start(name, lands):
    n = len(lands)

    def body(*refs):
        land_refs, send_sems, recv_sems = refs[:n], refs[n:2 * n], refs[2 * n:3 * n]
        for j in range(n):
            for cp in _first_copies(land_refs[j], send_sems[j], recv_sems[j]):
                cp.start()

    sems = [pltpu.SemaphoreType.DMA((1 + N_OTHER_CHIPS,))] * (2 * n)
    outs = pl.pallas_call(
        body, name=name, out_shape=tuple(sems + [pltpu.HBM(a.shape, a.dtype) for a in lands]),
        in_specs=[_HBM] * n, out_specs=tuple([_SEM] * (2 * n) + [_HBM] * n),
        input_output_aliases={j: 2 * n + j for j in range(n)},
        compiler_params=pltpu.CompilerParams(has_side_effects=_DATAFLOW),
    )(*[pltpu.with_memory_space_constraint(a, pltpu.HBM) for a in lands])
    return [[outs[j], outs[n + j], outs[2 * n + j]] for j in range(n)]


def gather_pass_on(name, handle, after):
    send_sems, recv_sems, land = handle

    def body(land_ref, recv_ref, after_ref, land_out, send2, recv2, token):
        arrivals = _first_copies(land_ref, recv_ref, recv_ref)
        for k, cp in enumerate(_passed_on_copies(land_ref, send2, recv2)):
            arrivals[1 + k].wait_recv()
            cp.start()
        token[...] = jnp.zeros_like(token)

    sem3 = pltpu.SemaphoreType.DMA((N_OTHER_CHIPS,))
    land, send2, recv2, token = pl.pallas_call(
        body, name=name,
        out_shape=(pltpu.HBM(land.shape, land.dtype), sem3, sem3, jax.ShapeDtypeStruct((8, 128), F32)),
        in_specs=[_HBM, _SEM, pl.BlockSpec(memory_space=pl.ANY)],
        out_specs=(_HBM, _SEM, _SEM, pl.BlockSpec(memory_space=pltpu.VMEM)),
        input_output_aliases={0: 0}, compiler_params=pltpu.CompilerParams(has_side_effects=_DATAFLOW),
    )(land, recv_sems, after)
    return [send_sems, recv_sems, land, send2, recv2], token


def gather_wait(name, handle, after):
    send_sems, recv_sems, land, send2, recv2 = handle

    def body(land_ref, send_ref, recv_ref, send2_ref, recv2_ref, after_ref, land_out):
        first = _first_copies(land_ref, send_ref, recv_ref)
        for cp in first:
            cp.wait_send()
        first[0].wait_recv()
        for cp in _passed_on_copies(land_ref, send2_ref, recv2_ref):
            cp.wait_send()
            cp.wait_recv()

    return pl.pallas_call(
        body, name=name, out_shape=pltpu.HBM(land.shape, land.dtype),
        in_specs=[_HBM, _SEM, _SEM, _SEM, _SEM, pl.BlockSpec(memory_space=pl.ANY)], out_specs=_HBM,
        input_output_aliases={0: 0}, compiler_params=pltpu.CompilerParams(has_side_effects=_DATAFLOW),
    )(land, send_sems, recv_sems, send2, recv2, after)


def adamw_reduce(name, parts, w, m, v):
    nl, r, wd = w.shape
    tr = _row_tile(r, 2 * ROW_TILE_BYTES // (8 * wd))

    def body(*refs):
        p_refs = refs[:nl]
        w_ref, m_ref, v_ref, g_ref, d_ref, nm_ref, nv_ref = refs[nl:]
        layer = pl.program_id(0)
        for j in range(nl):
            @pl.when(layer == j)
            def _(j=j):
                g = p_refs[j][0].astype(F32)
                for dev in range(1, N_DEV):
                    g = g + p_refs[j][dev].astype(F32)
                nm = B1 * m_ref[...] + (1.0 - B1) * g
                nv = B2 * v_ref[...] + (1.0 - B2) * jnp.square(g)
                m_hat = nm / (1.0 - B1 ** STEP)
                v_hat = nv / (1.0 - B2 ** STEP)
                g_ref[...] = g
                d_ref[...] = -LR * (m_hat / (jnp.sqrt(v_hat) + ADAM_EPS) + WD * w_ref[...])
                nm_ref[...] = nm
                nv_ref[...] = nv

    def part_spec(j):
        return pl.BlockSpec((N_DEV, tr, wd), lambda l, i: (0, jnp.where(l == j, i, 0), 0))

    spec = pl.BlockSpec((None, tr, wd), lambda l, i: (l, i, 0))
    return pl.pallas_call(
        body, name=name, grid=(nl, r // tr),
        in_specs=[part_spec(j) for j in range(nl)] + [spec, spec, spec],
        out_specs=[spec] * 4, out_shape=[jax.ShapeDtypeStruct((nl, r, wd), F32)] * 4,
        compiler_params=_params(("arbitrary", "arbitrary")),
    )(*parts, w, m, v)


def _pack_rows(vectors, rows=None):
    flat = jnp.concatenate([a.reshape(-1).astype(F32) for a in vectors])
    n = flat.shape[0]
    if rows is None:
        rows = -(-n // 1024) * 8
    return jnp.pad(flat, (0, rows * 128 - n)).reshape(rows, 128)


def _unpack_rows(packed, like):
    flat = packed.reshape(-1)
    out, pos = [], 0
    for a in like:
        out.append(flat[pos:pos + a.size].reshape(a.shape))
        pos += a.size
    return out


def kernel(x, p, mix_norm, mlp_norm, ple_norm, w_a_in, a_lb_logits, a_head_gain, w_a_out, kv_norm, w_kvf, b_f, w_b_q, w_b_out, w_mlp_up, w_mlp_down, w_ple_gate, w_ple_up, final_norm, loss_target, m_mix_norm, m_mlp_norm, m_ple_norm, m_w_a_in, m_a_lb_logits, m_a_head_gain, m_w_a_out, m_kv_norm, m_w_kvf, m_b_f, m_w_b_q, m_w_b_out, m_w_mlp_up, m_w_mlp_down, m_w_ple_gate, m_w_ple_up, m_final_norm, v_mix_norm, v_mlp_norm, v_ple_norm, v_w_a_in, v_a_lb_logits, v_a_head_gain, v_w_a_out, v_kv_norm, v_w_kvf, v_b_f, v_w_b_q, v_w_b_out, v_w_mlp_up, v_w_mlp_down, v_w_ple_gate, v_w_ple_up, v_final_norm):
    t, d = x.shape[1], x.shape[2]
    nh = d // HEAD_DIM
    n_layers = 2
    x2 = x.reshape(t, d)
    target = loss_target.reshape(t, d)
    me = _my_index()

    shards = {"w_a_in": w_a_in[0], "w_a_out": w_a_out[0], "w_kvf": w_kvf, "w_b_q": w_b_q[0], "w_b_out": w_b_out[0]}
    for l in range(n_layers):
        shards.update({f"w_mlp_up{l}": w_mlp_up[l], f"w_mlp_down{l}": w_mlp_down[l],
                       f"w_ple_gate{l}": w_ple_gate[l], f"w_ple_up{l}": w_ple_up[l]})
    first_use = ["a_lb_logits", "w_a_in", "w_a_out", "w_mlp_up0", "w_mlp_down0", "w_ple_gate0", "w_ple_up0", "w_kvf",
                 "w_b_q", "w_b_out", "w_mlp_up1", "w_mlp_down1", "w_ple_gate1", "w_ple_up1"]
    row_sharded = ("w_a_out", "w_b_q", "w_b_out", "w_mlp_down", "w_ple_gate")
    shards_bf = [a_lb_logits] + [shards[n].astype(BF16) for n in first_use[1:]]
    ag_handles = gather_start("ag_start", [
        lax.dynamic_update_slice(lax.empty((N_DEV,) + a.shape, a.dtype), a[None], (me, 0, 0)) for a in shards_bf])
    passed_on = {}
    weights = {}

    def pass_on(j, after):
        if j < len(first_use) and j not in passed_on:
            passed_on[j] = gather_pass_on("ag_pass_" + first_use[j], ag_handles[j], after)

    def weight(name, after=None):
        if name not in weights:
            j = first_use.index(name)
            pass_on(j, after)
            pass_on(j + 1, after)
            behind = passed_on[j + 1][1] if j + 1 in passed_on else after
            g = gather_wait("ag_wait_" + name, passed_on[j][0], behind)
            if name.rstrip("01") in row_sharded:
                g = g.reshape(1, g.shape[0] * g.shape[1], g.shape[2])
            weights[name] = g
        return weights[name]

    lgt = weight("a_lb_logits", x2).transpose(1, 0, 2).reshape(2, d)
    p_bf = [p[l, 0].astype(BF16) for l in range(n_layers)]

    def row(vec):
        return vec.reshape(1, -1)

    def mlp_ple_fwd(l, h_in, a):
        (h_a, u_mlp), _ = rowwise(f"add_norm_mlp{l}", _add_norm_fwd, [h_in, a], [row(mlp_norm[l])])
        pre, act = mm_nn(f"mlp_up{l}", u_mlp, weight(f"w_mlp_up{l}", u_mlp), fuse=(_relu2, (), (BF16, BF16)))
        mo = mm_nn(f"mlp_down{l}", act, weight(f"w_mlp_down{l}", act))
        (h_b, u_ple), _ = rowwise(f"add_norm_ple{l}", _add_norm_fwd, [h_a, mo], [row(ple_norm[l])])
        gpre = mm_nn(f"ple_gate{l}", u_ple, weight(f"w_ple_gate{l}", u_ple))
        pu = mm_nn(f"ple_up{l}", p_bf[l], weight(f"w_ple_up{l}", gpre))
        return dict(h_a=h_a, u_mlp=u_mlp, pre=pre, act=act, h_b=h_b, u_ple=u_ple, gpre=gpre, pu=pu)

    (u0,), _ = rowwise("norm_mix0", _norm_fwd, [x2], [row(mix_norm[0])])
    z = mm_nn("a_in", u0, weight("w_a_in", u0))
    og, states = hgrn_fwd(z, lgt, a_head_gain)
    a0 = mm_nn("a_out", og, weight("w_a_out", og))
    s0 = mlp_ple_fwd(0, x2, a0)
    (h3, u_kv, u1), _ = rowwise("ple_norms", _ple_two_norms_fwd, [s0["h_b"], s0["gpre"], s0["pu"]],
                                [row(kv_norm), row(mix_norm[1])])
    hk = mm_nn("kvf", u_kv, weight("w_kvf", u_kv), out3=True)
    hk = hk.transpose(1, 0, 2).reshape(t, -1)
    k_bf, v_bf = hk[:, :d].astype(BF16), hk[:, d:2 * d].astype(BF16)
    fl_t = hk[:, 2 * d:].T
    b_f_col = b_f.reshape(nh, 1)
    dcum = decay_fwd(fl_t, b_f_col)
    dcol, drow = dcum.reshape(nh, t, 1), dcum.reshape(nh, 1, t)
    q = mm_nn("b_q", u1, weight("w_b_q", dcum))
    o, lse = attn_fwd(q, k_bf, v_bf, dcol, drow)
    a1 = mm_nn("b_out", o, weight("w_b_out", o))
    s1 = mlp_ple_fwd(1, h3, a1)

    (dh, dgpre, dpu), (d_final, loss_rows) = rowwise(
        "tail", _tail_fwd_bwd, [s1["h_b"], s1["gpre"], s1["pu"], target], [row(final_norm)])
    loss = lax.psum(loss_rows[0, 0], MESH_AXES)

    sent = {}
    tokens = []

    def send_grad(name, g):
        g = g.reshape(N_DEV, -1, g.shape[-1])
        (handle,), token = exchange_start("rs_start_" + name, [g], False)
        sent[name] = handle
        tokens.append(token)

    def after_sends():
        deps = tuple(tokens)
        tokens.clear()
        return deps

    def mlp_ple_bwd(l, s, dh, dgpre, dpu):
        send_grad(f"w_ple_gate{l}", mm_tn(f"d_ple_gate_w{l}", s["u_ple"], dgpre, 1, deps=after_sends()))
        send_grad(f"w_ple_up{l}", mm_tn(f"d_ple_up_w{l}", p_bf[l], dpu, N_DEV, deps=after_sends()))
        du = mm_nt(f"d_ple_gate_x{l}", dgpre, weight(f"w_ple_gate{l}"), deps=after_sends())
        (dh, dh_bf), (d_ple,) = rowwise(f"d_norm_ple{l}", _norm_bwd, [s["h_b"], du, dh], [row(ple_norm[l])])
        send_grad(f"w_mlp_down{l}", mm_tn(f"d_mlp_down_w{l}", s["act"], dh_bf, 1))
        (dpre,) = mm_nt(f"d_mlp_down_x{l}", dh_bf, weight(f"w_mlp_down{l}"), deps=after_sends(),
                        fuse=(_relu2_bwd, (s["pre"],), (BF16,)))
        send_grad(f"w_mlp_up{l}", mm_tn(f"d_mlp_up_w{l}", s["u_mlp"], dpre, N_DEV))
        du = mm_nt(f"d_mlp_up_x{l}", dpre, weight(f"w_mlp_up{l}"), deps=after_sends())
        (dh, dh_bf), (d_mlp,) = rowwise(f"d_norm_mlp{l}", _norm_bwd, [s["h_a"], du, dh], [row(mlp_norm[l])])
        return dh, dh_bf, d_ple, d_mlp

    dh, dh_bf, d_ple1, d_mlp1 = mlp_ple_bwd(1, s1, dh, dgpre, dpu)
    send_grad("w_b_out", mm_tn("d_b_out_w", o, dh_bf, 1))
    do = mm_nt("d_b_out_x", dh_bf, weight("w_b_out"), out_dtype=BF16, deps=after_sends())
    dq, dk, dv, ddrow = attn_bwd(q, k_bf, v_bf, dcol, drow, lse, do)
    send_grad("w_b_q", mm_tn("d_b_q_w", u1, dq, 1))
    du1 = mm_nt("d_b_q_x", dq, weight("w_b_q"), deps=after_sends())
    dfl_t, d_b_f = decay_bwd(fl_t, b_f_col, ddrow.reshape(nh, t))
    dhk = jnp.concatenate([dk, dv, dfl_t.T.astype(BF16)], axis=1)
    dhk = dhk.reshape(t, N_DEV, -1).transpose(1, 0, 2)
    send_grad("w_kvf", mm_tn("d_kvf_w", u_kv, dhk, N_DEV))
    du_kv = mm_nt("d_kvf_x", dhk, weight("w_kvf"), deps=after_sends())
    (dh,), (d_kv_norm, d_mix1) = rowwise("d_ple_norms", _two_norms_bwd, [h3, du_kv, du1, dh],
                                         [row(kv_norm), row(mix_norm[1])])
    (dgpre, dpu), _ = rowwise("d_ple0", _ple_bwd, [s0["gpre"], s0["pu"], dh])
    dh, dh_bf, d_ple0, d_mlp0 = mlp_ple_bwd(0, s0, dh, dgpre, dpu)
    send_grad("w_a_out", mm_tn("d_a_out_w", og, dh_bf, 1))
    dog = mm_nt("d_a_out_x", dh_bf, weight("w_a_out"), deps=after_sends())
    dz4, d_lgt, d_hg = hgrn_bwd(z, lgt, a_head_gain, states, dog)
    dz = dz4.transpose(1, 0, 2).reshape(t, 4 * d)
    send_grad("w_a_in", mm_tn("d_a_in_w", u0, dz, N_DEV))
    du0 = mm_nt("d_a_in_x", dz, weight("w_a_in"), deps=after_sends())
    (dx, _), (d_mix0,) = rowwise("d_norm_mix0", _norm_bwd, [x2, du0, dh], [row(mix_norm[0])])

    new = {}
    last = [dx]

    def update(name, parts, w, m, v):
        shp = w.shape
        w3, m3, v3 = (a.reshape(len(parts), -1, shp[-1]) for a in (w, m, v))
        new[name] = tuple(a.reshape(shp) for a in adamw_reduce("adamw_" + name, parts, w3, m3, v3))
        last[0] = new[name][0]

    def receive_update(name, layers, w, m, v):
        parts = {}
        for sfx in layers:
            g, land = exchange_wait(f"rs_wait_{name}{sfx}", sent[name + sfx], last[0], False)
            parts[sfx] = lax.dynamic_update_slice(land, lax.dynamic_slice_in_dim(g, me, 1, 0), (me, 0, 0))
        update(name, [parts[sfx] for sfx in sorted(layers)], w, m, v)

    both = ("1", "0")
    receive_update("w_b_out", ("",), w_b_out, m_w_b_out, v_w_b_out)
    receive_update("w_b_q", ("",), w_b_q, m_w_b_q, v_w_b_q)
    receive_update("w_kvf", ("",), w_kvf, m_w_kvf, v_w_kvf)
    receive_update("w_ple_gate", both, w_ple_gate, m_w_ple_gate, v_w_ple_gate)
    receive_update("w_ple_up", both, w_ple_up, m_w_ple_up, v_w_ple_up)
    receive_update("w_mlp_down", both, w_mlp_down, m_w_mlp_down, v_w_mlp_down)
    receive_update("w_mlp_up", both, w_mlp_up, m_w_mlp_up, v_w_mlp_up)
    receive_update("w_a_out", ("",), w_a_out, m_w_a_out, v_w_a_out)
    receive_update("w_a_in", ("",), w_a_in, m_w_a_in, v_w_a_in)

    small = dict(mix_norm=jnp.concatenate([d_mix0, d_mix1]), mlp_norm=jnp.concatenate([d_mlp0, d_mlp1]),
                 ple_norm=jnp.concatenate([d_ple0, d_ple1]), a_head_gain=d_hg, kv_norm=d_kv_norm.reshape(d),
                 b_f=d_b_f.reshape(nh), final_norm=d_final.reshape(d))
    small_w = dict(mix_norm=(mix_norm, m_mix_norm, v_mix_norm), mlp_norm=(mlp_norm, m_mlp_norm, v_mlp_norm),
                   ple_norm=(ple_norm, m_ple_norm, v_ple_norm),
                   a_head_gain=(a_head_gain, m_a_head_gain, v_a_head_gain), kv_norm=(kv_norm, m_kv_norm, v_kv_norm),
                   b_f=(b_f, m_b_f, v_b_f), final_norm=(final_norm, m_final_norm, v_final_norm))
    names = list(small)
    packed = _pack_rows([d_lgt] + [small[n] for n in names])
    everyone = all_gather("ag_small_grads", packed, deps=(last[0],))
    n_lgt_rows = d_lgt.size // 128
    lgt_parts = everyone[:, :n_lgt_rows].reshape(N_DEV, 2, d)
    lgt_parts = lax.dynamic_slice_in_dim(lgt_parts, me * a_lb_logits.shape[1], a_lb_logits.shape[1], axis=2)
    update("a_lb_logits", [lgt_parts], a_lb_logits, m_a_lb_logits, v_a_lb_logits)
    rest = everyone[:, n_lgt_rows:]
    like = [small_w[n][0] for n in names]
    packed_w, packed_m, packed_v = (_pack_rows([small_w[n][j] for n in names], rest.shape[1])[None] for j in range(3))
    outs = adamw_reduce("adamw_small", [rest], packed_w, packed_m, packed_v)
    unpacked = [_unpack_rows(a, like) for a in outs]
    for j, n in enumerate(names):
        new[n] = tuple(unpacked[q][j] for q in range(4))

    order = ["mix_norm", "mlp_norm", "ple_norm", "w_a_in", "a_lb_logits", "a_head_gain", "w_a_out", "kv_norm",
             "w_kvf", "b_f", "w_b_q", "w_b_out", "w_mlp_up", "w_mlp_down", "w_ple_gate", "w_ple_up", "final_norm"]
    result = [loss, dx.reshape(x.shape)]
    for j in range(4):
        result += [new[n][j] for n in order]
    return tuple(result)
```

```python
import functools

import jax
import jax.numpy as jnp
from jax import lax
from jax.experimental import pallas as pl
from jax.experimental.pallas import tpu as pltpu

F32 = jnp.float32
BF16 = jnp.bfloat16
HEAD_DIM = 128
CHUNK = 16
TILE = 128
HEADS_PER_STEP = 2
NORM_EPS = 1e-6
N_DEV = 8
MESH_AXES = ("x", "y", "c")
VMEM_LIMIT_BYTES = 48 * 1024 * 1024
ROW_TILE_BYTES = 1024 * 1024
LR, B1, B2, ADAM_EPS, WD, STEP = 0.001, 0.9, 0.999, 1e-08, 0.01, 10
NEG_BIG = -1e30

NN = (((1,), (0,)), ((), ()))
NT = (((1,), (1,)), ((), ()))
TN = (((0,), (0,)), ((), ()))


def _params(semantics):
    return pltpu.CompilerParams(dimension_semantics=semantics, vmem_limit_bytes=VMEM_LIMIT_BYTES)


def _tile(n, prefs):
    for p in prefs:
        if n % p == 0:
            return p
    return n


def _row_tile(rows, limit):
    for cand in (2048, 1024, 512, 256, 128, 64, 32, 16):
        if cand <= limit and rows % cand == 0:
            return cand
    return rows


def _mm_call(name, a, b, dims, grid, a_spec, b_spec, o_spec, o_shape, acc_shape, k_axes, out_dtype, deps=(),
             fuse=None):
    nk = 1
    for ax in k_axes:
        nk *= grid[ax]
    fn, extra, out_dtypes = fuse if fuse else (lambda acc: (acc,), (), (out_dtype,))
    n_extra, n_out = len(extra), len(out_dtypes)

    def finish(acc, rest):
        o_refs = rest[n_extra + len(deps):n_extra + len(deps) + n_out]
        for ref, val in zip(o_refs, fn(acc, *[r[...] for r in rest[:n_extra]])):
            ref[...] = val.astype(ref.dtype)

    def one_step(a_ref, b_ref, *rest):
        finish(lax.dot_general(a_ref[...], b_ref[...], dims, preferred_element_type=F32), rest)

    def accumulate(a_ref, b_ref, *rest):
        acc_ref = rest[-1]
        k = 0
        for ax in k_axes:
            k = k * grid[ax] + pl.program_id(ax)
        part = lax.dot_general(a_ref[...], b_ref[...], dims, preferred_element_type=F32)

        @pl.when(k == 0)
        def _():
            acc_ref[...] = part

        @pl.when((k > 0) & (k < nk - 1))
        def _():
            acc_ref[...] += part

        @pl.when(k == nk - 1)
        def _():
            finish(acc_ref[...] + part, rest)

    sem = tuple("arbitrary" if ax in k_axes else "parallel" for ax in range(len(grid)))
    outs = pl.pallas_call(
        one_step if nk == 1 else accumulate, name=name, grid=grid,
        in_specs=[a_spec, b_spec] + [o_spec] * n_extra + [pl.BlockSpec(memory_space=pl.ANY)] * len(deps),
        out_specs=[o_spec] * n_out, out_shape=[jax.ShapeDtypeStruct(o_shape, dt) for dt in out_dtypes],
        scratch_shapes=[] if nk == 1 else [pltpu.VMEM(acc_shape, F32)], compiler_params=_params(sem),
    )(a, b, *extra, *deps)
    return outs if fuse else outs[0]


def mm_nn(name, a, b3, out_dtype=F32, out3=False, deps=(), fuse=None):
    m, k = a.shape
    g, _, n = b3.shape
    tm, tk = _tile(m, (1024, 512, 256)), _tile(k, (2048, 1024, 512, 256))
    tn = n if out3 else _tile(n, (1024, 512, 256, 128))
    nj = n // tn
    grid = (m // tm, g, nj, k // tk)
    a_spec = pl.BlockSpec((tm, tk), lambda i, gg, j, kk: (i, kk))
    b_spec = pl.BlockSpec((None, tk, tn), lambda i, gg, j, kk: (gg, kk, j))
    if out3:
        o_spec = pl.BlockSpec((None, tm, tn), lambda i, gg, j, kk: (gg, i, j))
        o_shape = (g, m, n)
    else:
        o_spec = pl.BlockSpec((tm, tn), lambda i, gg, j, kk: (i, gg * nj + j))
        o_shape = (m, g * n)
    return _mm_call(name, a, b3, NN, grid, a_spec, b_spec, o_spec, o_shape, (tm, tn), (3,), out_dtype, deps, fuse)


def mm_nt(name, a, b3, out_dtype=F32, deps=(), fuse=None):
    g, k, n = b3.shape
    a3 = a.ndim == 3
    m = a.shape[1] if a3 else a.shape[0]
    tm, tko = _tile(m, (1024, 512, 256)), _tile(k, (1024, 512, 256))
    tc = n if a3 else _tile(n, (2048, 1024, 512, 256, 128))
    nc = n // tc
    grid = (m // tm, k // tko, g, nc)
    if a3:
        a_spec = pl.BlockSpec((None, tm, tc), lambda i, j, gg, c: (gg, i, c))
    else:
        a_spec = pl.BlockSpec((tm, tc), lambda i, j, gg, c: (i, gg * nc + c))
    b_spec = pl.BlockSpec((None, tko, tc), lambda i, j, gg, c: (gg, j, c))
    o_spec = pl.BlockSpec((tm, tko), lambda i, j, gg, c: (i, j))
    return _mm_call(name, a, b3, NT, grid, a_spec, b_spec, o_spec, (m, k), (tm, tko), (2, 3), out_dtype, deps, fuse)


def mm_tn(name, a, b, g, out_dtype=BF16, deps=()):
    t, k = a.shape
    b3 = b.ndim == 3
    n = b.shape[2] if b3 else b.shape[1] // g
    tm = _tile(k, (1024, 512, 256))
    tn = n if b3 else _tile(n, (1024, 512, 256, 128))
    tt = _tile(t, (2048, 1024, 512, 256))
    nj = n // tn
    grid = (g, k // tm, nj, t // tt)
    a_spec = pl.BlockSpec((tt, tm), lambda gg, i, j, s: (s, i))
    if b3:
        b_spec = pl.BlockSpec((None, tt, tn), lambda gg, i, j, s: (gg, s, j))
    else:
        b_spec = pl.BlockSpec((tt, tn), lambda gg, i, j, s: (s, gg * nj + j))
    o_spec = pl.BlockSpec((None, tm, tn), lambda gg, i, j, s: (gg, i, j))
    return _mm_call(name, a, b, TN, grid, a_spec, b_spec, o_spec, (g, k, n), (tm, tn), (3,), out_dtype, deps)


def rowwise(name, fn, rows, vecs=()):
    t = rows[0].shape[0]
    wmax = max(r.shape[1] for r in rows)
    tr = _row_tile(t, ROW_TILE_BYTES // (4 * wmax))
    row_s = [jax.ShapeDtypeStruct((tr, r.shape[1]), r.dtype) for r in rows]
    vec_s = [jax.ShapeDtypeStruct(v.shape, v.dtype) for v in vecs]
    out_rows_s, out_sums_s = jax.eval_shape(fn, *row_s, *vec_s)
    n_in, n_r = len(rows) + len(vecs), len(out_rows_s)

    def body(*refs):
        i = pl.program_id(0)
        o_rows, o_sums = fn(*[r[...] for r in refs[:n_in]])
        for ref, val in zip(refs[n_in:n_in + n_r], o_rows):
            ref[...] = val

        if out_sums_s:
            @pl.when(i == 0)
            def _():
                for ref in refs[n_in + n_r:]:
                    ref[...] = jnp.zeros_like(ref)

            for ref, val in zip(refs[n_in + n_r:], o_sums):
                ref[...] += val

    in_specs = [pl.BlockSpec((tr, r.shape[1]), lambda i: (i, 0)) for r in rows]
    in_specs += [pl.BlockSpec(v.shape, lambda i: (0, 0)) for v in vecs]
    out_specs = [pl.BlockSpec((tr, s.shape[1]), lambda i: (i, 0)) for s in out_rows_s]
    out_specs += [pl.BlockSpec(s.shape, lambda i: (0, 0)) for s in out_sums_s]
    out_shape = [jax.ShapeDtypeStruct((t, s.shape[1]), s.dtype) for s in out_rows_s]
    out_shape += [jax.ShapeDtypeStruct(s.shape, s.dtype) for s in out_sums_s]
    outs = pl.pallas_call(
        body, name=name, grid=(t // tr,), in_specs=in_specs, out_specs=out_specs, out_shape=out_shape,
        compiler_params=_params(("arbitrary",)),
    )(*rows, *vecs)
    return outs[:n_r], outs[n_r:]


def _rms(x, gain):
    return x * lax.rsqrt(jnp.mean(x * x, axis=-1, keepdims=True) + NORM_EPS) * gain


def _norm_fwd(x, gain):
    return (_rms(x, gain).astype(BF16),), ()


def _add_norm_fwd(h, a, gain):
    h = h + a
    return (h, _rms(h, gain).astype(BF16)), ()


def _relu2(pre):
    r = jnp.maximum(pre, 0.0)
    return pre, r * r


def _ple(h, gpre, pu):
    return h + pu * jax.nn.sigmoid(gpre)


def _ple_two_norms_fwd(h, gpre, pu, gain_a, gain_b):
    h = _ple(h, gpre, pu)
    return (h, _rms(h, gain_a).astype(BF16), _rms(h, gain_b).astype(BF16)), ()


def _tail_fwd_bwd(h, gpre, pu, target, gain):
    def row_loss(h, gpre, pu, gain):
        y = _rms(_ple(h, gpre, pu), gain)
        return 0.5 * jnp.mean(jnp.square(y - target), axis=-1, keepdims=True)

    loss, vjp = jax.vjp(row_loss, h, gpre, pu, gain)
    dh, dgpre, dpu, dgain = vjp(jnp.ones_like(loss))
    loss = jnp.broadcast_to(jnp.sum(loss, axis=0, keepdims=True), (1, 128))
    return (dh, dgpre.astype(BF16), dpu.astype(BF16)), (dgain, loss)


def _ple_bwd(gpre, pu, dh):
    _, vjp = jax.vjp(lambda g, u: pu_times_gate(g, u), gpre, pu)
    dgpre, dpu = vjp(dh)
    return (dgpre.astype(BF16), dpu.astype(BF16)), ()


def pu_times_gate(gpre, pu):
    return pu * jax.nn.sigmoid(gpre)


def _norm_bwd(h, du, dh_in, gain):
    _, vjp = jax.vjp(_rms, h, gain)
    dh, dgain = vjp(du)
    dh = dh_in + dh
    return (dh, dh.astype(BF16)), (dgain,)


def _two_norms_bwd(h, du_a, du_b, dh_in, gain_a, gain_b):
    _, vjp = jax.vjp(lambda h, ga, gb: (_rms(h, ga), _rms(h, gb)), h, gain_a, gain_b)
    dh, dga, dgb = vjp((du_a, du_b))
    return (dh_in + dh,), (dga, dgb)


def _relu2_bwd(dact, pre):
    return (dact * 2.0 * jnp.maximum(pre.astype(F32), 0.0),)


def _bf16_dot(dims_fwd, dims_da, dims_db, swap_da, swap_db):
    @jax.custom_vjp
    def dot(a, b):
        return lax.dot_general(a.astype(BF16), b.astype(BF16), dims_fwd, preferred_element_type=F32)

    def fwd(a, b):
        return dot(a, b), (a, b)

    def bwd(res, ct):
        a, b = res
        ct, a, b = ct.astype(BF16), a.astype(BF16), b.astype(BF16)
        da = lax.dot_general(*((b, ct) if swap_da else (ct, b)), dims_da, preferred_element_type=F32)
        db = lax.dot_general(*((ct, a) if swap_db else (a, ct)), dims_db, preferred_element_type=F32)
        return da, db

    dot.defvjp(fwd, bwd)
    return dot


_dot_nn = _bf16_dot(NN, NT, TN, False, False)
_dot_nt = _bf16_dot(NT, NN, TN, False, True)
_dot_tn = _bf16_dot(TN, NT, NN, True, False)


def _chunk_masks(transposed):
    r = lax.broadcasted_iota(jnp.int32, (TILE, TILE), 0)
    c = lax.broadcasted_iota(jnp.int32, (TILE, TILE), 1)
    same = (r // CHUNK) == (c // CHUNK)
    causal = same & ((r <= c) if transposed else (c <= r))
    return causal, same


def _chunk_scan(x, reverse):
    pos = lax.broadcasted_iota(jnp.int32, x.shape, 0) % CHUNK
    step = 1
    while step < CHUNK:
        if reverse:
            x = x + jnp.where(pos < CHUNK - step, pltpu.roll(x, x.shape[0] - step, axis=0), 0.0)
        else:
            x = x + jnp.where(pos >= step, pltpu.roll(x, step, axis=0), 0.0)
        step *= 2
    return x


def _chunk_total(x):
    return _chunk_scan(x, False) + _chunk_scan(x, True) - x


@jax.custom_vjp
def _chunk_sums(x):
    return _chunk_scan(x, False), _chunk_total(x)


def _chunk_sums_fwd(x):
    return _chunk_sums(x), None


def _chunk_sums_bwd(_, ct):
    return (_chunk_scan(ct[0], True) + _chunk_total(ct[1]),)


_chunk_sums.defvjp(_chunk_sums_fwd, _chunk_sums_bwd)


def _hgrn_tile(q, f, i, g, lgt, hg, st):
    d = q.shape[1]
    l0, l1 = lgt[0:1], lgt[1:2]
    mx = jnp.maximum(l0, l1)
    e0, e1 = jnp.exp(l0 - mx), jnp.exp(l1 - mx)
    lb = e0 / (e0 + e1)
    fg = lb + (1.0 - lb) * jax.nn.sigmoid(f)
    k = 1.0 - fg
    causal, _ = _chunk_masks(False)
    b, b_last = _chunk_sums(jnp.log(fg))
    q_in = q * jax.nn.sigmoid(q) * (d ** -0.5) * jnp.exp(b)
    k_in = k * jnp.exp(-b)
    k_end = k * jnp.exp(b_last - b)
    att = jnp.where(causal, _dot_nt(q_in, k_in), 0.0)
    o_intra = _dot_nn(att, i)
    n_chunks = TILE // CHUNK
    chunk_of_row = lax.broadcasted_iota(jnp.int32, (TILE, 1), 0) // CHUNK

    def spread(a):
        return jnp.concatenate([jnp.where(chunk_of_row == n, a, 0.0) for n in range(n_chunks)], axis=1)

    increments = _dot_tn(i, spread(k_end))
    states = []
    for n in range(n_chunks):
        states.append(st)
        decay = jnp.exp(jnp.mean(b_last[n * CHUNK:(n + 1) * CHUNK], axis=0, keepdims=True))
        st = st * decay + increments[:, n * d:(n + 1) * d]
    o = o_intra + _dot_nt(spread(q_in), jnp.concatenate(states, axis=1))
    o = o * lax.rsqrt(jnp.mean(o * o, axis=-1, keepdims=True) + NORM_EPS) * hg
    return o * (g * jax.nn.sigmoid(g)), st


def hgrn_fwd(z, lgt, hg):
    t, d4 = z.shape
    d = d4 // 4
    nh, nt = d // HEAD_DIM, t // TILE
    hp = HEADS_PER_STEP
    wide = hp * HEAD_DIM

    def body(q_ref, f_ref, i_ref, g_ref, lgt_ref, hg_ref, o_ref, st_out_ref, st_ref):
        tt = pl.program_id(1)

        @pl.when(tt == 0)
        def _():
            st_ref[...] = jnp.zeros_like(st_ref)

        for hh in range(hp):
            cols = slice(hh * HEAD_DIM, (hh + 1) * HEAD_DIM)
            st = st_ref[hh]
            st_out_ref[hh] = st
            o, st = _hgrn_tile(q_ref[:, cols], f_ref[:, cols], i_ref[:, cols], g_ref[:, cols], lgt_ref[:, cols],
                               hg_ref[...], st)
            o_ref[:, cols] = o.astype(o_ref.dtype)
            st_ref[hh] = st

    def part(p):
        return pl.BlockSpec((TILE, wide), lambda h, tt: (tt, p * (nh // hp) + h))

    return pl.pallas_call(
        body, name="hgrn_fwd", grid=(nh // hp, nt),
        in_specs=[part(0), part(1), part(2), part(3),
                  pl.BlockSpec((2, wide), lambda h, tt: (0, h)),
                  pl.BlockSpec((1, HEAD_DIM), lambda h, tt: (0, 0))],
        out_specs=[pl.BlockSpec((TILE, wide), lambda h, tt: (tt, h)),
                   pl.BlockSpec((hp, None, HEAD_DIM, HEAD_DIM), lambda h, tt: (h, tt, 0, 0))],
        out_shape=[jax.ShapeDtypeStruct((t, d), BF16),
                   jax.ShapeDtypeStruct((nh, nt, HEAD_DIM, HEAD_DIM), F32)],
        scratch_shapes=[pltpu.VMEM((hp, HEAD_DIM, HEAD_DIM), F32)],
        compiler_params=_params(("parallel", "arbitrary")),
    )(z, z, z, z, lgt, hg)


def hgrn_bwd(z, lgt, hg, states, dout):
    t, d4 = z.shape
    d = d4 // 4
    nh, nt = d // HEAD_DIM, t // TILE
    hp = HEADS_PER_STEP
    wide = hp * HEAD_DIM

    def body(q_ref, f_ref, i_ref, g_ref, lgt_ref, hg_ref, st_in_ref, do_ref, dz_ref, dlgt_ref, dhg_ref, dst_ref):
        h, tt = pl.program_id(0), pl.program_id(1)

        @pl.when(tt == 0)
        def _():
            dst_ref[...] = jnp.zeros_like(dst_ref)
            dlgt_ref[...] = jnp.zeros_like(dlgt_ref)

        @pl.when((tt == 0) & (h == 0))
        def _():
            dhg_ref[...] = jnp.zeros_like(dhg_ref)

        for hh in range(hp):
            cols = slice(hh * HEAD_DIM, (hh + 1) * HEAD_DIM)
            _, vjp = jax.vjp(_hgrn_tile, q_ref[:, cols], f_ref[:, cols], i_ref[:, cols], g_ref[:, cols],
                             lgt_ref[:, cols], hg_ref[...], st_in_ref[hh])
            grads = vjp((do_ref[:, cols], dst_ref[hh]))
            for p in range(4):
                dz_ref[p, :, cols] = grads[p].astype(dz_ref.dtype)
            dlgt_ref[:, cols] += grads[4]
            dhg_ref[...] += grads[5]
            dst_ref[hh] = grads[6]

    def part(p):
        return pl.BlockSpec((TILE, wide), lambda h, tt: (nt - 1 - tt, p * (nh // hp) + h))

    return pl.pallas_call(
        body, name="hgrn_bwd", grid=(nh // hp, nt),
        in_specs=[part(0), part(1), part(2), part(3),
                  pl.BlockSpec((2, wide), lambda h, tt: (0, h)),
                  pl.BlockSpec((1, HEAD_DIM), lambda h, tt: (0, 0)),
                  pl.BlockSpec((hp, None, HEAD_DIM, HEAD_DIM), lambda h, tt: (h, nt - 1 - tt, 0, 0)),
                  pl.BlockSpec((TILE, wide), lambda h, tt: (nt - 1 - tt, h))],
        out_specs=[pl.BlockSpec((4, TILE, wide), lambda h, tt: (0, nt - 1 - tt, h)),
                   pl.BlockSpec((2, wide), lambda h, tt: (0, h)),
                   pl.BlockSpec((1, HEAD_DIM), lambda h, tt: (0, 0))],
        out_shape=[jax.ShapeDtypeStruct((4, t, d), BF16),
                   jax.ShapeDtypeStruct((2, d), F32),
                   jax.ShapeDtypeStruct((1, HEAD_DIM), F32)],
        scratch_shapes=[pltpu.VMEM((hp, HEAD_DIM, HEAD_DIM), F32)],
        compiler_params=_params(("arbitrary", "arbitrary")),
    )(z, z, z, z, lgt, hg, states, dout)


def _log_sigmoid(x):
    return jnp.minimum(x, 0.0) - jnp.log(1.0 + jnp.exp(-jnp.abs(x)))


def decay_fwd(fl_t, b_f):
    nh, t = fl_t.shape

    def body(fl_ref, b_ref, out_ref):
        r = lax.broadcasted_iota(jnp.int32, (128, 128), 0)
        c = lax.broadcasted_iota(jnp.int32, (128, 128), 1)
        upper = (r <= c).astype(F32)
        carry = jnp.zeros((nh, 1), F32)
        for j in range(t // 128):
            cols = slice(j * 128, (j + 1) * 128)
            ls = _log_sigmoid(fl_ref[:, cols] + b_ref[...])
            out_ref[:, cols] = carry + jnp.dot(ls, upper, precision=lax.Precision.HIGHEST,
                                               preferred_element_type=F32)
            carry = carry + jnp.sum(ls, axis=1, keepdims=True)

    return pl.pallas_call(body, name="decay_fwd", out_shape=jax.ShapeDtypeStruct((nh, t), F32),
                          compiler_params=_params(None))(fl_t, b_f)


def decay_bwd(fl_t, b_f, ddcum):
    nh, t = fl_t.shape

    def body(fl_ref, b_ref, dd_ref, dfl_ref, db_ref):
        r = lax.broadcasted_iota(jnp.int32, (128, 128), 0)
        c = lax.broadcasted_iota(jnp.int32, (128, 128), 1)
        lower = (r >= c).astype(F32)
        carry = jnp.zeros((nh, 1), F32)
        db = jnp.zeros((nh, 1), F32)
        for j in reversed(range(t // 128)):
            cols = slice(j * 128, (j + 1) * 128)
            dd = dd_ref[:, cols]
            dls = carry + jnp.dot(dd, lower, precision=lax.Precision.HIGHEST, preferred_element_type=F32)
            carry = carry + jnp.sum(dd, axis=1, keepdims=True)
            dfl = dls * jax.nn.sigmoid(-(fl_ref[:, cols] + b_ref[...]))
            dfl_ref[:, cols] = dfl
            db = db + jnp.sum(dfl, axis=1, keepdims=True)
        db_ref[...] = db

    return pl.pallas_call(body, name="decay_bwd",
                          out_shape=[jax.ShapeDtypeStruct((nh, t), F32), jax.ShapeDtypeStruct((nh, 1), F32)],
                          compiler_params=_params(None))(fl_t, b_f, ddcum)


def _attn_tiles(t):
    tq = _tile(t, (256, 128))
    return tq, max(tq, min(512, t // 2))


def _span_logits(qs, k_ref, dcol_ref, drow_ref, i, tq, sp, span):
    cols = slice(sp * span, (sp + 1) * span)
    s = lax.dot_general(qs, k_ref[cols, :], NT, preferred_element_type=F32)
    s = s + dcol_ref[...] - drow_ref[:, cols]
    row = i * tq + lax.broadcasted_iota(jnp.int32, (tq, span), 0)
    col = sp * span + lax.broadcasted_iota(jnp.int32, (tq, span), 1)
    return jnp.where(col <= row, s, NEG_BIG)


def attn_fwd(q, k, v, dcol, drow):
    t, d = q.shape
    nh = d // HEAD_DIM
    tq, span = _attn_tiles(t)

    def body(q_ref, k_ref, v_ref, dcol_ref, drow_ref, o_ref, lse_ref, s_ref, m_ref, l_ref, acc_ref):
        i = pl.program_id(1)
        qs = (q_ref[...] * (HEAD_DIM ** -0.5)).astype(BF16)
        last_row = i * tq + tq - 1
        m_ref[...] = jnp.full(m_ref.shape, NEG_BIG, F32)
        l_ref[...] = jnp.zeros_like(l_ref)
        acc_ref[...] = jnp.zeros_like(acc_ref)
        for sp in range(t // span):
            cols = slice(sp * span, (sp + 1) * span)

            @pl.when(sp * span <= last_row)
            def _():
                s = _span_logits(qs, k_ref, dcol_ref, drow_ref, i, tq, sp, span)
                s_ref[:, cols] = s
                m_ref[...] = jnp.maximum(m_ref[...], jnp.max(s, axis=1, keepdims=True))

        for sp in range(t // span):
            cols = slice(sp * span, (sp + 1) * span)

            @pl.when(sp * span <= last_row)
            def _():
                p = jnp.exp(s_ref[:, cols] - m_ref[...])
                l_ref[...] += jnp.sum(p, axis=1, keepdims=True)
                acc_ref[...] += jnp.dot(p.astype(BF16), v_ref[cols, :], preferred_element_type=F32)

        o_ref[...] = (acc_ref[...] / l_ref[...]).astype(o_ref.dtype)
        lse_ref[...] = m_ref[...] + jnp.log(l_ref[...])

    return pl.pallas_call(
        body, name="attn_fwd", grid=(nh, t // tq),
        in_specs=[pl.BlockSpec((tq, HEAD_DIM), lambda h, i: (i, h)),
                  pl.BlockSpec((t, HEAD_DIM), lambda h, i: (0, h)),
                  pl.BlockSpec((t, HEAD_DIM), lambda h, i: (0, h)),
                  pl.BlockSpec((None, tq, 1), lambda h, i: (h, i, 0)),
                  pl.BlockSpec((None, 1, t), lambda h, i: (h, 0, 0))],
        out_specs=[pl.BlockSpec((tq, HEAD_DIM), lambda h, i: (i, h)),
                   pl.BlockSpec((None, tq, 1), lambda h, i: (h, i, 0))],
        out_shape=[jax.ShapeDtypeStruct((t, d), BF16), jax.ShapeDtypeStruct((nh, t, 1), F32)],
        scratch_shapes=[pltpu.VMEM((tq, t), F32), pltpu.VMEM((tq, 1), F32), pltpu.VMEM((tq, 1), F32),
                        pltpu.VMEM((tq, HEAD_DIM), F32)],
        compiler_params=_params(("parallel", "parallel")),
    )(q, k, v, dcol, drow)


def attn_bwd(q, k, v, dcol, drow, lse, do):
    t, d = q.shape
    nh = d // HEAD_DIM
    tq, span = _attn_tiles(t)
    nq = t // tq

    def body(q_ref, k_ref, v_ref, dcol_ref, drow_ref, lse_ref, do_ref,
             dq_ref, dk_ref, dv_ref, ddrow_ref, dk_acc, dv_acc, p_ref, dp_ref, delta_ref, dq_acc):
        i = pl.program_id(1)

        @pl.when(i == 0)
        def _():
            dk_acc[...] = jnp.zeros_like(dk_acc)
            dv_acc[...] = jnp.zeros_like(dv_acc)
            ddrow_ref[...] = jnp.zeros_like(ddrow_ref)

        qs = (q_ref[...] * (HEAD_DIM ** -0.5)).astype(BF16)
        do = do_ref[...]
        last_row = i * tq + tq - 1
        delta_ref[...] = jnp.zeros_like(delta_ref)
        dq_acc[...] = jnp.zeros_like(dq_acc)
        for sp in range(t // span):
            cols = slice(sp * span, (sp + 1) * span)

            @pl.when(sp * span <= last_row)
            def _():
                p = jnp.exp(_span_logits(qs, k_ref, dcol_ref, drow_ref, i, tq, sp, span) - lse_ref[...])
                dp = lax.dot_general(do, v_ref[cols, :], NT, preferred_element_type=F32)
                p_ref[:, cols] = p
                dp_ref[:, cols] = dp
                delta_ref[...] += jnp.sum(p * dp, axis=1, keepdims=True)

        for sp in range(t // span):
            cols = slice(sp * span, (sp + 1) * span)

            @pl.when(sp * span <= last_row)
            def _():
                p = p_ref[:, cols]
                ds = p * (dp_ref[:, cols] - delta_ref[...])
                dsb = ds.astype(BF16)
                dq_acc[...] += jnp.dot(dsb, k_ref[cols, :], preferred_element_type=F32)
                dk_acc[cols, :] += lax.dot_general(dsb, qs, TN, preferred_element_type=F32)
                dv_acc[cols, :] += lax.dot_general(p.astype(BF16), do, TN, preferred_element_type=F32)
                ddrow_ref[:, cols] -= jnp.sum(ds, axis=0, keepdims=True)

        dq_ref[...] = (dq_acc[...] * (HEAD_DIM ** -0.5)).astype(dq_ref.dtype)

        @pl.when(i == nq - 1)
        def _():
            dk_ref[...] = dk_acc[...].astype(dk_ref.dtype)
            dv_ref[...] = dv_acc[...].astype(dv_ref.dtype)

    tile = pl.BlockSpec((tq, HEAD_DIM), lambda h, i: (i, h))
    full = pl.BlockSpec((t, HEAD_DIM), lambda h, i: (0, h))
    col = pl.BlockSpec((None, tq, 1), lambda h, i: (h, i, 0))
    rowv = pl.BlockSpec((None, 1, t), lambda h, i: (h, 0, 0))
    return pl.pallas_call(
        body, name="attn_bwd", grid=(nh, nq),
        in_specs=[tile, full, full, col, rowv, col, tile],
        out_specs=[tile, full, full, rowv],
        out_shape=[jax.ShapeDtypeStruct((t, d), BF16), jax.ShapeDtypeStruct((t, d), BF16),
                   jax.ShapeDtypeStruct((t, d), BF16), jax.ShapeDtypeStruct((nh, 1, t), F32)],
        scratch_shapes=[pltpu.VMEM((t, HEAD_DIM), F32), pltpu.VMEM((t, HEAD_DIM), F32),
                        pltpu.VMEM((tq, t), F32), pltpu.VMEM((tq, t), F32), pltpu.VMEM((tq, 1), F32),
                        pltpu.VMEM((tq, HEAD_DIM), F32)],
        compiler_params=_params(("parallel", "arbitrary")),
    )(q, k, v, dcol, drow, lse, do)


def _my_index():
    return (lax.axis_index("x") * 2 + lax.axis_index("y")) * 2 + lax.axis_index("c")


def _exchange(name, src, gather, deps=()):
    shape = src.shape if gather else src.shape[1:]

    def body(src_ref, *rest):
        out_ref, send_sems, recv_sems, local_sem = rest[len(deps):]
        x, y, c = (lax.axis_index(a) for a in MESH_AXES)
        me = (x * 2 + y) * 2 + c
        mine = src_ref if gather else src_ref.at[me]
        local = pltpu.make_async_copy(mine, out_ref.at[me], local_sem)
        local.start()
        copies = []
        for dlt in range(1, N_DEV):
            dx, dy, dc = dlt // 4, (dlt // 2) % 2, dlt % 2
            px, py, pc = x ^ dx, y ^ dy, c ^ dc
            peer = (px * 2 + py) * 2 + pc
            copies.append(pltpu.make_async_remote_copy(
                src_ref=src_ref if gather else src_ref.at[peer], dst_ref=out_ref.at[me],
                send_sem=send_sems.at[dlt - 1], recv_sem=recv_sems.at[dlt - 1],
                device_id=(px, py, pc), device_id_type=pl.DeviceIdType.MESH))
        for cp in copies:
            cp.start()
        for cp in copies:
            cp.wait_recv()
        for cp in copies:
            cp.wait_send()
        local.wait()

    return pl.pallas_call(
        body, name=name, out_shape=jax.ShapeDtypeStruct((N_DEV,) + tuple(shape), src.dtype),
        in_specs=[pl.BlockSpec(memory_space=pl.ANY)] * (1 + len(deps)), out_specs=pl.BlockSpec(memory_space=pl.ANY),
        scratch_shapes=[pltpu.SemaphoreType.DMA((N_DEV - 1,)), pltpu.SemaphoreType.DMA((N_DEV - 1,)),
                        pltpu.SemaphoreType.DMA],
        compiler_params=pltpu.CompilerParams(has_side_effects=True),
    )(src, *deps)


def all_gather(name, x, deps=()):
    return _exchange(name, x, True, deps)


_HBM = pl.BlockSpec(memory_space=pltpu.HBM)
_SEM = pl.BlockSpec(memory_space=pltpu.SEMAPHORE)
_DATAFLOW = pltpu.SideEffectType.DATAFLOW_SIDE_EFFECTING


def _peer_copies(src_ref, land_ref, send_sems, recv_sems, gather):
    x, y, c = (lax.axis_index(a) for a in MESH_AXES)
    me = (x * 2 + y) * 2 + c
    copies = []
    for dlt in range(1, N_DEV):
        px, py, pc = x ^ (dlt // 4), y ^ ((dlt // 2) % 2), c ^ (dlt % 2)
        peer = (px * 2 + py) * 2 + pc
        copies.append(pltpu.make_async_remote_copy(
            src_ref=src_ref if gather else src_ref.at[peer], dst_ref=land_ref.at[me],
            send_sem=send_sems.at[dlt - 1], recv_sem=recv_sems.at[dlt - 1],
            device_id=(px, py, pc), device_id_type=pl.DeviceIdType.MESH))
    return copies


def exchange_start(name, srcs, gather):
    n = len(srcs)
    lands = [lax.empty((N_DEV,) + tuple(s.shape if gather else s.shape[1:]), s.dtype) for s in srcs]

    def body(*refs):
        src_refs, land_refs = refs[:n], refs[n:2 * n]
        send_sems, recv_sems = refs[2 * n:3 * n], refs[3 * n:4 * n]
        token = refs[-1]
        for j in range(n):
            for cp in _peer_copies(src_refs[j], land_refs[j], send_sems[j], recv_sems[j], gather):
                cp.start()
        token[...] = jnp.zeros_like(token)

    sems = [pltpu.SemaphoreType.DMA((N_DEV - 1,))] * (2 * n)
    thru = [pltpu.HBM(a.shape, a.dtype) for a in list(srcs) + lands]
    outs = pl.pallas_call(
        body, name=name, out_shape=tuple(sems + thru + [jax.ShapeDtypeStruct((8, 128), F32)]),
        in_specs=[_HBM] * (2 * n), out_specs=tuple([_SEM] * (2 * n) + [_HBM] * (2 * n) + [pl.BlockSpec(memory_space=pltpu.VMEM)]),
        input_output_aliases={j: 2 * n + j for j in range(2 * n)},
        compiler_params=pltpu.CompilerParams(has_side_effects=_DATAFLOW),
    )(*[pltpu.with_memory_space_constraint(a, pltpu.HBM) for a in list(srcs) + lands])
    handles = [(outs[j], outs[n + j], outs[2 * n + j], outs[3 * n + j]) for j in range(n)]
    return handles, outs[-1]


def exchange_wait(name, handle, after, gather):
    send_sems, recv_sems, src, land = handle

    def body(src_ref, land_ref, send_ref, recv_ref, after_ref, src_out, land_out):
        for cp in _peer_copies(src_ref, land_ref, send_ref, recv_ref, gather):
            cp.wait_send()
            cp.wait_recv()

    return pl.pallas_call(
        body, name=name, out_shape=(pltpu.HBM(src.shape, src.dtype), pltpu.HBM(land.shape, land.dtype)),
        in_specs=[_HBM, _HBM, _SEM, _SEM, pl.BlockSpec(memory_space=pl.ANY)], out_specs=(_HBM, _HBM),
        input_output_aliases={0: 0, 1: 1},
        compiler_params=pltpu.CompilerParams(has_side_effects=_DATAFLOW),
    )(src, land, send_sems, recv_sems, after)


N_OTHER_CHIPS = 3


def _two_level_places():
    x, y, c = (lax.axis_index(a) for a in MESH_AXES)
    return (x, y, c), (x * 2 + y) * 2 + c, (x, y, 1 - c), [(1 - x, y), (x, 1 - y), (1 - x, 1 - y)]


def _first_copies(land_ref, send_sems, recv_sems):
    (x, y, c), me, other_core, chips = _two_level_places()
    targets = [other_core] + [(cx, cy, c) for cx, cy in chips]
    return [pltpu.make_async_remote_copy(
        src_ref=land_ref.at[me], dst_ref=land_ref.at[me], send_sem=send_sems.at[k], recv_sem=recv_sems.at[k],
        device_id=to, device_id_type=pl.DeviceIdType.MESH) for k, to in enumerate(targets)]


def _passed_on_copies(land_ref, send_sems, recv_sems):
    (x, y, c), me, other_core, chips = _two_level_places()
    copies = []
    for k, (cx, cy) in enumerate(chips):
        slot = land_ref.at[(cx * 2 + cy) * 2 + c]
        copies.append(pltpu.make_async_remote_copy(
            src_ref=slot, dst_ref=slot, send_sem=send_sems.at[k], recv_sem=recv_sems.at[k],
            device_id=other_core, device_id_type=pl.DeviceIdType.MESH))
    return copies


def gather_start(name, lands):
    n = len(lands)

    def body(*refs):
        land_refs, send_sems, recv_sems = refs[:n], refs[n:2 * n], refs[2 * n:3 * n]
        for j in range(n):
            for cp in _first_copies(land_refs[j], send_sems[j], recv_sems[j]):
                cp.start()

    sems = [pltpu.SemaphoreType.DMA((1 + N_OTHER_CHIPS,))] * (2 * n)
    outs = pl.pallas_call(
        body, name=name, out_shape=tuple(sems + [pltpu.HBM(a.shape, a.dtype) for a in lands]),
        in_specs=[_HBM] * n, out_specs=tuple([_SEM] * (2 * n) + [_HBM] * n),
        input_output_aliases={j: 2 * n + j for j in range(n)},
        compiler_params=pltpu.CompilerParams(has_side_effects=_DATAFLOW),
    )(*[pltpu.with_memory_space_constraint(a, pltpu.HBM) for a in lands])
    return [[outs[j], outs[n + j], outs[2 * n + j]] for j in range(n)]


def gather_pass_on(name, handle, after):
    send_sems, recv_sems, land = handle

    def body(land_ref, recv_ref, after_ref, land_out, send2, recv2, token):
        arrivals = _first_copies(land_ref, recv_ref, recv_ref)
        for k, cp in enumerate(_passed_on_copies(land_ref, send2, recv2)):
            arrivals[1 + k].wait_recv()
            cp.start()
        token[...] = jnp.zeros_like(token)

    sem3 = pltpu.SemaphoreType.DMA((N_OTHER_CHIPS,))
    land, send2, recv2, token = pl.pallas_call(
        body, name=name,
        out_shape=(pltpu.HBM(land.shape, land.dtype), sem3, sem3, jax.ShapeDtypeStruct((8, 128), F32)),
        in_specs=[_HBM, _SEM, pl.BlockSpec(memory_space=pl.ANY)],
        out_specs=(_HBM, _SEM, _SEM, pl.BlockSpec(memory_space=pltpu.VMEM)),
        input_output_aliases={0: 0}, compiler_params=pltpu.CompilerParams(has_side_effects=_DATAFLOW),
    )(land, recv_sems, after)
    return [send_sems, recv_sems, land, send2, recv2], token


def gather_wait(name, handle, after):
    send_sems, recv_sems, land, send2, recv2 = handle

    def body(land_ref, send_ref, recv_ref, send2_ref, recv2_ref, after_ref, land_out):
        first = _first_copies(land_ref, send_ref, recv_ref)
        for cp in first:
            cp.wait_send()
        first[0].wait_recv()
        for cp in _passed_on_copies(land_ref, send2_ref, recv2_ref):
            cp.wait_send()
            cp.wait_recv()

    return pl.pallas_call(
        body, name=name, out_shape=pltpu.HBM(land.shape, land.dtype),
        in_specs=[_HBM, _SEM, _SEM, _SEM, _SEM, pl.BlockSpec(memory_space=pl.ANY)], out_specs=_HBM,
        input_output_aliases={0: 0}, compiler_params=pltpu.CompilerParams(has_side_effects=_DATAFLOW),
    )(land, send_sems, recv_sems, send2, recv2, after)


def adamw_reduce(name, parts, w, m, v):
    nl, r, wd = w.shape
    tr = _row_tile(r, 2 * ROW_TILE_BYTES // (8 * wd))

    def body(*refs):
        p_refs = refs[:nl]
        w_ref, m_ref, v_ref, g_ref, d_ref, nm_ref, nv_ref = refs[nl:]
        layer = pl.program_id(0)
        for j in range(nl):
            @pl.when(layer == j)
            def _(j=j):
                g = p_refs[j][0].astype(F32)
                for dev in range(1, N_DEV):
                    g = g + p_refs[j][dev].astype(F32)
                nm = B1 * m_ref[...] + (1.0 - B1) * g
                nv = B2 * v_ref[...] + (1.0 - B2) * jnp.square(g)
                m_hat = nm / (1.0 - B1 ** STEP)
                v_hat = nv / (1.0 - B2 ** STEP)
                g_ref[...] = g
                d_ref[...] = -LR * (m_hat / (jnp.sqrt(v_hat) + ADAM_EPS) + WD * w_ref[...])
                nm_ref[...] = nm
                nv_ref[...] = nv

    def part_spec(j):
        return pl.BlockSpec((N_DEV, tr, wd), lambda l, i: (0, jnp.where(l == j, i, 0), 0))

    spec = pl.BlockSpec((None, tr, wd), lambda l, i: (l, i, 0))
    return pl.pallas_call(
        body, name=name, grid=(nl, r // tr),
        in_specs=[part_spec(j) for j in range(nl)] + [spec, spec, spec],
        out_specs=[spec] * 4, out_shape=[jax.ShapeDtypeStruct((nl, r, wd), F32)] * 4,
        compiler_params=_params(("arbitrary", "arbitrary")),
    )(*parts, w, m, v)


def _pack_rows(vectors, rows=None):
    flat = jnp.concatenate([a.reshape(-1).astype(F32) for a in vectors])
    n = flat.shape[0]
    if rows is None:
        rows = -(-n // 1024) * 8
    return jnp.pad(flat, (0, rows * 128 - n)).reshape(rows, 128)


def _unpack_rows(packed, like):
    flat = packed.reshape(-1)
    out, pos = [], 0
    for a in like:
        out.append(flat[pos:pos + a.size].reshape(a.shape))
        pos += a.size
    return out


def kernel(x, p, mix_norm, mlp_norm, ple_norm, w_a_in, a_lb_logits, a_head_gain, w_a_out, kv_norm, w_kvf, b_f, w_b_q, w_b_out, w_mlp_up, w_mlp_down, w_ple_gate, w_ple_up, final_norm, loss_target, m_mix_norm, m_mlp_norm, m_ple_norm, m_w_a_in, m_a_lb_logits, m_a_head_gain, m_w_a_out, m_kv_norm, m_w_kvf, m_b_f, m_w_b_q, m_w_b_out, m_w_mlp_up, m_w_mlp_down, m_w_ple_gate, m_w_ple_up, m_final_norm, v_mix_norm, v_mlp_norm, v_ple_norm, v_w_a_in, v_a_lb_logits, v_a_head_gain, v_w_a_out, v_kv_norm, v_w_kvf, v_b_f, v_w_b_q, v_w_b_out, v_w_mlp_up, v_w_mlp_down, v_w_ple_gate, v_w_ple_up, v_final_norm):
    t, d = x.shape[1], x.shape[2]
    nh = d // HEAD_DIM
    n_layers = 2
    x2 = x.reshape(t, d)
    target = loss_target.reshape(t, d)
    me = _my_index()

    shards = {"w_a_in": w_a_in[0], "w_a_out": w_a_out[0], "w_kvf": w_kvf, "w_b_q": w_b_q[0], "w_b_out": w_b_out[0]}
    for l in range(n_layers):
        shards.update({f"w_mlp_up{l}": w_mlp_up[l], f"w_mlp_down{l}": w_mlp_down[l],
                       f"w_ple_gate{l}": w_ple_gate[l], f"w_ple_up{l}": w_ple_up[l]})
    first_use = ["a_lb_logits", "w_a_in", "w_a_out", "w_mlp_up0", "w_mlp_down0", "w_ple_gate0", "w_ple_up0", "w_kvf",
                 "w_b_q", "w_b_out", "w_mlp_up1", "w_mlp_down1", "w_ple_gate1", "w_ple_up1"]
    row_sharded = ("w_a_out", "w_b_q", "w_b_out", "w_mlp_down", "w_ple_gate")
    shards_bf = [a_lb_logits] + [shards[n].astype(BF16) for n in first_use[1:]]
    ag_handles = gather_start("ag_start", [
        lax.dynamic_update_slice(lax.empty((N_DEV,) + a.shape, a.dtype), a[None], (me, 0, 0)) for a in shards_bf])
    passed_on = {}
    weights = {}

    def pass_on(j, after):
        if j < len(first_use) and j not in passed_on:
            passed_on[j] = gather_pass_on("ag_pass_" + first_use[j], ag_handles[j], after)

    def weight(name, after=None):
        if name not in weights:
            j = first_use.index(name)
            pass_on(j, after)
            pass_on(j + 1, after)
            behind = passed_on[j + 1][1] if j + 1 in passed_on else after
            g = gather_wait("ag_wait_" + name, passed_on[j][0], behind)
            if name.rstrip("01") in row_sharded:
                g = g.reshape(1, g.shape[0] * g.shape[1], g.shape[2])
            weights[name] = g
        return weights[name]

    lgt = weight("a_lb_logits", x2).transpose(1, 0, 2).reshape(2, d)
    p_bf = [p[l, 0].astype(BF16) for l in range(n_layers)]

    def row(vec):
        return vec.reshape(1, -1)

    def mlp_ple_fwd(l, h_in, a):
        (h_a, u_mlp), _ = rowwise(f"add_norm_mlp{l}", _add_norm_fwd, [h_in, a], [row(mlp_norm[l])])
        pre, act = mm_nn(f"mlp_up{l}", u_mlp, weight(f"w_mlp_up{l}", u_mlp), fuse=(_relu2, (), (BF16, BF16)))
        mo = mm_nn(f"mlp_down{l}", act, weight(f"w_mlp_down{l}", act))
        (h_b, u_ple), _ = rowwise(f"add_norm_ple{l}", _add_norm_fwd, [h_a, mo], [row(ple_norm[l])])
        gpre = mm_nn(f"ple_gate{l}", u_ple, weight(f"w_ple_gate{l}", u_ple))
        pu = mm_nn(f"ple_up{l}", p_bf[l], weight(f"w_ple_up{l}", gpre))
        return dict(h_a=h_a, u_mlp=u_mlp, pre=pre, act=act, h_b=h_b, u_ple=u_ple, gpre=gpre, pu=pu)

    (u0,), _ = rowwise("norm_mix0", _norm_fwd, [x2], [row(mix_norm[0])])
    z = mm_nn("a_in", u0, weight("w_a_in", u0))
    og, states = hgrn_fwd(z, lgt, a_head_gain)
    a0 = mm_nn("a_out", og, weight("w_a_out", og))
    s0 = mlp_ple_fwd(0, x2, a0)
    (h3, u_kv, u1), _ = rowwise("ple_norms", _ple_two_norms_fwd, [s0["h_b"], s0["gpre"], s0["pu"]],
                                [row(kv_norm), row(mix_norm[1])])
    hk = mm_nn("kvf", u_kv, weight("w_kvf", u_kv), out3=True)
    hk = hk.transpose(1, 0, 2).reshape(t, -1)
    k_bf, v_bf = hk[:, :d].astype(BF16), hk[:, d:2 * d].astype(BF16)
    fl_t = hk[:, 2 * d:].T
    b_f_col = b_f.reshape(nh, 1)
    dcum = decay_fwd(fl_t, b_f_col)
    dcol, drow = dcum.reshape(nh, t, 1), dcum.reshape(nh, 1, t)
    q = mm_nn("b_q", u1, weight("w_b_q", dcum))
    o, lse = attn_fwd(q, k_bf, v_bf, dcol, drow)
    a1 = mm_nn("b_out", o, weight("w_b_out", o))
    s1 = mlp_ple_fwd(1, h3, a1)

    (dh, dgpre, dpu), (d_final, loss_rows) = rowwise(
        "tail", _tail_fwd_bwd, [s1["h_b"], s1["gpre"], s1["pu"], target], [row(final_norm)])

    sent = {}
    tokens = []

    def send_grad(name, g):
        g = g.reshape(N_DEV, -1, g.shape[-1])
        (handle,), token = exchange_start("rs_start_" + name, [g], False)
        sent[name] = handle
        tokens.append(token)

    def after_sends():
        deps = tuple(tokens)
        tokens.clear()
        return deps

    def mlp_ple_bwd(l, s, dh, dgpre, dpu):
        send_grad(f"w_ple_gate{l}", mm_tn(f"d_ple_gate_w{l}", s["u_ple"], dgpre, 1, deps=after_sends()))
        send_grad(f"w_ple_up{l}", mm_tn(f"d_ple_up_w{l}", p_bf[l], dpu, N_DEV, deps=after_sends()))
        du = mm_nt(f"d_ple_gate_x{l}", dgpre, weight(f"w_ple_gate{l}"), deps=after_sends())
        (dh, dh_bf), (d_ple,) = rowwise(f"d_norm_ple{l}", _norm_bwd, [s["h_b"], du, dh], [row(ple_norm[l])])
        send_grad(f"w_mlp_down{l}", mm_tn(f"d_mlp_down_w{l}", s["act"], dh_bf, 1))
        (dpre,) = mm_nt(f"d_mlp_down_x{l}", dh_bf, weight(f"w_mlp_down{l}"), deps=after_sends(),
                        fuse=(_relu2_bwd, (s["pre"],), (BF16,)))
        send_grad(f"w_mlp_up{l}", mm_tn(f"d_mlp_up_w{l}", s["u_mlp"], dpre, N_DEV))
        du = mm_nt(f"d_mlp_up_x{l}", dpre, weight(f"w_mlp_up{l}"), deps=after_sends())
        (dh, dh_bf), (d_mlp,) = rowwise(f"d_norm_mlp{l}", _norm_bwd, [s["h_a"], du, dh], [row(mlp_norm[l])])
        return dh, dh_bf, d_ple, d_mlp

    dh, dh_bf, d_ple1, d_mlp1 = mlp_ple_bwd(1, s1, dh, dgpre, dpu)
    send_grad("w_b_out", mm_tn("d_b_out_w", o, dh_bf, 1))
    do = mm_nt("d_b_out_x", dh_bf, weight("w_b_out"), out_dtype=BF16, deps=after_sends())
    dq, dk, dv, ddrow = attn_bwd(q, k_bf, v_bf, dcol, drow, lse, do)
    send_grad("w_b_q", mm_tn("d_b_q_w", u1, dq, 1))
    du1 = mm_nt("d_b_q_x", dq, weight("w_b_q"), deps=after_sends())
    dfl_t, d_b_f = decay_bwd(fl_t, b_f_col, ddrow.reshape(nh, t))
    dhk = jnp.concatenate([dk, dv, dfl_t.T.astype(BF16)], axis=1)
    dhk = dhk.reshape(t, N_DEV, -1).transpose(1, 0, 2)
    send_grad("w_kvf", mm_tn("d_kvf_w", u_kv, dhk, N_DEV))
    du_kv = mm_nt("d_kvf_x", dhk, weight("w_kvf"), deps=after_sends())
    (dh,), (d_kv_norm, d_mix1) = rowwise("d_ple_norms", _two_norms_bwd, [h3, du_kv, du1, dh],
                                         [row(kv_norm), row(mix_norm[1])])
    (dgpre, dpu), _ = rowwise("d_ple0", _ple_bwd, [s0["gpre"], s0["pu"], dh])
    dh, dh_bf, d_ple0, d_mlp0 = mlp_ple_bwd(0, s0, dh, dgpre, dpu)
    send_grad("w_a_out", mm_tn("d_a_out_w", og, dh_bf, 1))
    dog = mm_nt("d_a_out_x", dh_bf, weight("w_a_out"), deps=after_sends())
    dz4, d_lgt, d_hg = hgrn_bwd(z, lgt, a_head_gain, states, dog)
    dz = dz4.transpose(1, 0, 2).reshape(t, 4 * d)
    send_grad("w_a_in", mm_tn("d_a_in_w", u0, dz, N_DEV))
    du0 = mm_nt("d_a_in_x", dz, weight("w_a_in"), deps=after_sends())
    (dx, _), (d_mix0,) = rowwise("d_norm_mix0", _norm_bwd, [x2, du0, dh], [row(mix_norm[0])])

    new = {}
    last = [dx]

    def update(name, parts, w, m, v):
        shp = w.shape
        w3, m3, v3 = (a.reshape(len(parts), -1, shp[-1]) for a in (w, m, v))
        new[name] = tuple(a.reshape(shp) for a in adamw_reduce("adamw_" + name, parts, w3, m3, v3))
        last[0] = new[name][0]

    def receive_update(name, layers, w, m, v):
        parts = {}
        for sfx in layers:
            g, land = exchange_wait(f"rs_wait_{name}{sfx}", sent[name + sfx], last[0], False)
            parts[sfx] = lax.dynamic_update_slice(land, lax.dynamic_slice_in_dim(g, me, 1, 0), (me, 0, 0))
        update(name, [parts[sfx] for sfx in sorted(layers)], w, m, v)

    both = ("1", "0")
    receive_update("w_b_out", ("",), w_b_out, m_w_b_out, v_w_b_out)
    receive_update("w_b_q", ("",), w_b_q, m_w_b_q, v_w_b_q)
    receive_update("w_kvf", ("",), w_kvf, m_w_kvf, v_w_kvf)
    receive_update("w_ple_gate", both, w_ple_gate, m_w_ple_gate, v_w_ple_gate)
    receive_update("w_ple_up", both, w_ple_up, m_w_ple_up, v_w_ple_up)
    receive_update("w_mlp_down", both, w_mlp_down, m_w_mlp_down, v_w_mlp_down)
    receive_update("w_mlp_up", both, w_mlp_up, m_w_mlp_up, v_w_mlp_up)
    receive_update("w_a_out", ("",), w_a_out, m_w_a_out, v_w_a_out)
    receive_update("w_a_in", ("",), w_a_in, m_w_a_in, v_w_a_in)

    small = dict(mix_norm=jnp.concatenate([d_mix0, d_mix1]), mlp_norm=jnp.concatenate([d_mlp0, d_mlp1]),
                 ple_norm=jnp.concatenate([d_ple0, d_ple1]), a_head_gain=d_hg, kv_norm=d_kv_norm.reshape(d),
                 b_f=d_b_f.reshape(nh), final_norm=d_final.reshape(d))
    small_w = dict(mix_norm=(mix_norm, m_mix_norm, v_mix_norm), mlp_norm=(mlp_norm, m_mlp_norm, v_mlp_norm),
                   ple_norm=(ple_norm, m_ple_norm, v_ple_norm),
                   a_head_gain=(a_head_gain, m_a_head_gain, v_a_head_gain), kv_norm=(kv_norm, m_kv_norm, v_kv_norm),
                   b_f=(b_f, m_b_f, v_b_f), final_norm=(final_norm, m_final_norm, v_final_norm))
    names = list(small)
    packed = _pack_rows([d_lgt] + [small[n] for n in names])
    everyone = all_gather("ag_small_grads", packed, deps=(last[0],))
    n_lgt_rows = d_lgt.size // 128
    lgt_parts = everyone[:, :n_lgt_rows].reshape(N_DEV, 2, d)
    lgt_parts = lax.dynamic_slice_in_dim(lgt_parts, me * a_lb_logits.shape[1], a_lb_logits.shape[1], axis=2)
    update("a_lb_logits", [lgt_parts], a_lb_logits, m_a_lb_logits, v_a_lb_logits)
    rest = everyone[:, n_lgt_rows:]
    like = [small_w[n][0] for n in names]
    packed_w, packed_m, packed_v = (_pack_rows([small_w[n][j] for n in names], rest.shape[1])[None] for j in range(3))
    outs = adamw_reduce("adamw_small", [rest], packed_w, packed_m, packed_v)
    unpacked = [_unpack_rows(a, like) for a in outs]
    for j, n in enumerate(names):
        new[n] = tuple(unpacked[q][j] for q in range(4))

    order = ["mix_norm", "mlp_norm", "ple_norm", "w_a_in", "a_lb_logits", "a_head_gain", "w_a_out", "kv_norm",
             "w_kvf", "b_f", "w_b_q", "w_b_out", "w_mlp_up", "w_mlp_down", "w_ple_gate", "w_ple_up", "final_norm"]
    loss_here, _ = lax.optimization_barrier((loss_rows[0, 0], new["final_norm"][0]))
    loss = lax.psum(loss_here, MESH_AXES)
    result = [loss, dx.reshape(x.shape)]
    for j in range(4):
        result += [new[n][j] for n in order]
    return tuple(result)
```

```python
import functools

import jax
import jax.numpy as jnp
from jax import lax
from jax.experimental import pallas as pl
from jax.experimental.pallas import tpu as pltpu

F32 = jnp.float32
BF16 = jnp.bfloat16
HEAD_DIM = 128
CHUNK = 16
TILE = 128
HEADS_PER_STEP = 2
NORM_EPS = 1e-6
N_DEV = 8
MESH_AXES = ("x", "y", "c")
VMEM_LIMIT_BYTES = 48 * 1024 * 1024
ROW_TILE_BYTES = 1024 * 1024
LR, B1, B2, ADAM_EPS, WD, STEP = 0.001, 0.9, 0.999, 1e-08, 0.01, 10
NEG_BIG = -1e30

NN = (((1,), (0,)), ((), ()))
NT = (((1,), (1,)), ((), ()))
TN = (((0,), (0,)), ((), ()))


def _params(semantics):
    return pltpu.CompilerParams(dimension_semantics=semantics, vmem_limit_bytes=VMEM_LIMIT_BYTES)


def _tile(n, prefs):
    for p in prefs:
        if n % p == 0:
            return p
    return n


def _row_tile(rows, limit):
    for cand in (2048, 1024, 512, 256, 128, 64, 32, 16):
        if cand <= limit and rows % cand == 0:
            return cand
    return rows


def _mm_call(name, a, b, dims, grid, a_spec, b_spec, o_spec, o_shape, acc_shape, k_axes, out_dtype, deps=(),
             fuse=None):
    nk = 1
    for ax in k_axes:
        nk *= grid[ax]
    fn, extra, out_dtypes = fuse if fuse else (lambda acc: (acc,), (), (out_dtype,))
    n_extra, n_out = len(extra), len(out_dtypes)

    def finish(acc, rest):
        o_refs = rest[n_extra + len(deps):n_extra + len(deps) + n_out]
        for ref, val in zip(o_refs, fn(acc, *[r[...] for r in rest[:n_extra]])):
            ref[...] = val.astype(ref.dtype)

    def one_step(a_ref, b_ref, *rest):
        finish(lax.dot_general(a_ref[...], b_ref[...], dims, preferred_element_type=F32), rest)

    def accumulate(a_ref, b_ref, *rest):
        acc_ref = rest[-1]
        k = 0
        for ax in k_axes:
            k = k * grid[ax] + pl.program_id(ax)
        part = lax.dot_general(a_ref[...], b_ref[...], dims, preferred_element_type=F32)

        @pl.when(k == 0)
        def _():
            acc_ref[...] = part

        @pl.when((k > 0) & (k < nk - 1))
        def _():
            acc_ref[...] += part

        @pl.when(k == nk - 1)
        def _():
            finish(acc_ref[...] + part, rest)

    sem = tuple("arbitrary" if ax in k_axes else "parallel" for ax in range(len(grid)))
    outs = pl.pallas_call(
        one_step if nk == 1 else accumulate, name=name, grid=grid,
        in_specs=[a_spec, b_spec] + [o_spec] * n_extra + [pl.BlockSpec(memory_space=pl.ANY)] * len(deps),
        out_specs=[o_spec] * n_out, out_shape=[jax.ShapeDtypeStruct(o_shape, dt) for dt in out_dtypes],
        scratch_shapes=[] if nk == 1 else [pltpu.VMEM(acc_shape, F32)], compiler_params=_params(sem),
    )(a, b, *extra, *deps)
    return outs if fuse else outs[0]


def mm_nn(name, a, b3, out_dtype=F32, out3=False, deps=(), fuse=None):
    m, k = a.shape
    g, _, n = b3.shape
    tm, tk = _tile(m, (1024, 512, 256)), _tile(k, (2048, 1024, 512, 256))
    tn = n if out3 else _tile(n, (1024, 512, 256, 128))
    nj = n // tn
    grid = (m // tm, g, nj, k // tk)
    a_spec = pl.BlockSpec((tm, tk), lambda i, gg, j, kk: (i, kk))
    b_spec = pl.BlockSpec((None, tk, tn), lambda i, gg, j, kk: (gg, kk, j))
    if out3:
        o_spec = pl.BlockSpec((None, tm, tn), lambda i, gg, j, kk: (gg, i, j))
        o_shape = (g, m, n)
    else:
        o_spec = pl.BlockSpec((tm, tn), lambda i, gg, j, kk: (i, gg * nj + j))
        o_shape = (m, g * n)
    return _mm_call(name, a, b3, NN, grid, a_spec, b_spec, o_spec, o_shape, (tm, tn), (3,), out_dtype, deps, fuse)


def mm_nt(name, a, b3, out_dtype=F32, deps=(), fuse=None, stacked=False):
    g, k, n = b3.shape
    a3 = a.ndim == 3 and not stacked
    m = a.shape[1] if a.ndim == 3 else a.shape[0]
    tm, tko = _tile(m, (1024, 512, 256)), _tile(k, (1024, 512, 256))
    tc = n if a3 else _tile(n, (2048, 1024, 512, 256, 128))
    nc = n // tc
    grid = (m // tm, k // tko, g, nc)
    if a3:
        a_spec = pl.BlockSpec((None, tm, tc), lambda i, j, gg, c: (gg, i, c))
    elif stacked:
        per = g // a.shape[0]
        a_spec = pl.BlockSpec((None, tm, tc), lambda i, j, gg, c: (gg // per, i, (gg % per) * nc + c))
    else:
        a_spec = pl.BlockSpec((tm, tc), lambda i, j, gg, c: (i, gg * nc + c))
    b_spec = pl.BlockSpec((None, tko, tc), lambda i, j, gg, c: (gg, j, c))
    o_spec = pl.BlockSpec((tm, tko), lambda i, j, gg, c: (i, j))
    return _mm_call(name, a, b3, NT, grid, a_spec, b_spec, o_spec, (m, k), (tm, tko), (2, 3), out_dtype, deps, fuse)


def mm_tn(name, a, b, g, out_dtype=BF16, deps=(), stacked=False):
    t, k = a.shape
    b3 = b.ndim == 3 and not stacked
    n = b.shape[2] if b3 else (b.shape[0] * b.shape[2] if stacked else b.shape[1]) // g
    tm = _tile(k, (1024, 512, 256))
    tn = n if b3 else _tile(n, (1024, 512, 256, 128))
    tt = _tile(t, (2048, 1024, 512, 256))
    nj = n // tn
    grid = (g, k // tm, nj, t // tt)
    a_spec = pl.BlockSpec((tt, tm), lambda gg, i, j, s: (s, i))
    if b3:
        b_spec = pl.BlockSpec((None, tt, tn), lambda gg, i, j, s: (gg, s, j))
    elif stacked:
        per = g // b.shape[0]
        b_spec = pl.BlockSpec((None, tt, tn), lambda gg, i, j, s: (gg // per, s, (gg % per) * nj + j))
    else:
        b_spec = pl.BlockSpec((tt, tn), lambda gg, i, j, s: (s, gg * nj + j))
    o_spec = pl.BlockSpec((None, tm, tn), lambda gg, i, j, s: (gg, i, j))
    return _mm_call(name, a, b, TN, grid, a_spec, b_spec, o_spec, (g, k, n), (tm, tn), (3,), out_dtype, deps)


def rowwise(name, fn, rows, vecs=()):
    t = rows[0].shape[0]
    wmax = max(r.shape[1] for r in rows)
    tr = _row_tile(t, ROW_TILE_BYTES // (4 * wmax))
    row_s = [jax.ShapeDtypeStruct((tr, r.shape[1]), r.dtype) for r in rows]
    vec_s = [jax.ShapeDtypeStruct(v.shape, v.dtype) for v in vecs]
    out_rows_s, out_sums_s = jax.eval_shape(fn, *row_s, *vec_s)
    n_in, n_r = len(rows) + len(vecs), len(out_rows_s)

    def body(*refs):
        i = pl.program_id(0)
        o_rows, o_sums = fn(*[r[...] for r in refs[:n_in]])
        for ref, val in zip(refs[n_in:n_in + n_r], o_rows):
            ref[...] = val

        if out_sums_s:
            @pl.when(i == 0)
            def _():
                for ref in refs[n_in + n_r:]:
                    ref[...] = jnp.zeros_like(ref)

            for ref, val in zip(refs[n_in + n_r:], o_sums):
                ref[...] += val

    in_specs = [pl.BlockSpec((tr, r.shape[1]), lambda i: (i, 0)) for r in rows]
    in_specs += [pl.BlockSpec(v.shape, lambda i: (0, 0)) for v in vecs]
    out_specs = [pl.BlockSpec((tr, s.shape[1]), lambda i: (i, 0)) for s in out_rows_s]
    out_specs += [pl.BlockSpec(s.shape, lambda i: (0, 0)) for s in out_sums_s]
    out_shape = [jax.ShapeDtypeStruct((t, s.shape[1]), s.dtype) for s in out_rows_s]
    out_shape += [jax.ShapeDtypeStruct(s.shape, s.dtype) for s in out_sums_s]
    outs = pl.pallas_call(
        body, name=name, grid=(t // tr,), in_specs=in_specs, out_specs=out_specs, out_shape=out_shape,
        compiler_params=_params(("arbitrary",)),
    )(*rows, *vecs)
    return outs[:n_r], outs[n_r:]


def _rms(x, gain):
    return x * lax.rsqrt(jnp.mean(x * x, axis=-1, keepdims=True) + NORM_EPS) * gain


def _norm_fwd(x, gain):
    return (_rms(x, gain).astype(BF16),), ()


def _add_norm_fwd(h, a, gain):
    h = h + a
    return (h, _rms(h, gain).astype(BF16)), ()


def _relu2(pre):
    r = jnp.maximum(pre, 0.0)
    return pre, r * r


def _ple(h, gpre, pu):
    return h + pu * jax.nn.sigmoid(gpre)


def _ple_two_norms_fwd(h, gpre, pu, gain_a, gain_b):
    h = _ple(h, gpre, pu)
    return (h, _rms(h, gain_a).astype(BF16), _rms(h, gain_b).astype(BF16)), ()


def _tail_fwd_bwd(h, gpre, pu, target, gain):
    def row_loss(h, gpre, pu, gain):
        y = _rms(_ple(h, gpre, pu), gain)
        return 0.5 * jnp.mean(jnp.square(y - target), axis=-1, keepdims=True)

    loss, vjp = jax.vjp(row_loss, h, gpre, pu, gain)
    dh, dgpre, dpu, dgain = vjp(jnp.ones_like(loss))
    loss = jnp.broadcast_to(jnp.sum(loss, axis=0, keepdims=True), (1, 128))
    return (dh, dgpre.astype(BF16), dpu.astype(BF16)), (dgain, loss)


def _ple_bwd(gpre, pu, dh):
    _, vjp = jax.vjp(lambda g, u: pu_times_gate(g, u), gpre, pu)
    dgpre, dpu = vjp(dh)
    return (dgpre.astype(BF16), dpu.astype(BF16)), ()


def pu_times_gate(gpre, pu):
    return pu * jax.nn.sigmoid(gpre)


def _norm_bwd(h, du, dh_in, gain):
    _, vjp = jax.vjp(_rms, h, gain)
    dh, dgain = vjp(du)
    dh = dh_in + dh
    return (dh, dh.astype(BF16)), (dgain,)


def _two_norms_bwd(h, du_a, du_b, dh_in, gain_a, gain_b):
    _, vjp = jax.vjp(lambda h, ga, gb: (_rms(h, ga), _rms(h, gb)), h, gain_a, gain_b)
    dh, dga, dgb = vjp((du_a, du_b))
    return (dh_in + dh,), (dga, dgb)


def _relu2_bwd(dact, pre):
    return (dact * 2.0 * jnp.maximum(pre.astype(F32), 0.0),)


def _bf16_dot(dims_fwd, dims_da, dims_db, swap_da, swap_db):
    @jax.custom_vjp
    def dot(a, b):
        return lax.dot_general(a.astype(BF16), b.astype(BF16), dims_fwd, preferred_element_type=F32)

    def fwd(a, b):
        return dot(a, b), (a, b)

    def bwd(res, ct):
        a, b = res
        ct, a, b = ct.astype(BF16), a.astype(BF16), b.astype(BF16)
        da = lax.dot_general(*((b, ct) if swap_da else (ct, b)), dims_da, preferred_element_type=F32)
        db = lax.dot_general(*((ct, a) if swap_db else (a, ct)), dims_db, preferred_element_type=F32)
        return da, db

    dot.defvjp(fwd, bwd)
    return dot


_dot_nn = _bf16_dot(NN, NT, TN, False, False)
_dot_nt = _bf16_dot(NT, NN, TN, False, True)
_dot_tn = _bf16_dot(TN, NT, NN, True, False)


def _chunk_masks(transposed):
    r = lax.broadcasted_iota(jnp.int32, (TILE, TILE), 0)
    c = lax.broadcasted_iota(jnp.int32, (TILE, TILE), 1)
    same = (r // CHUNK) == (c // CHUNK)
    causal = same & ((r <= c) if transposed else (c <= r))
    return causal, same


def _chunk_scan(x, reverse):
    pos = lax.broadcasted_iota(jnp.int32, x.shape, 0) % CHUNK
    step = 1
    while step < CHUNK:
        if reverse:
            x = x + jnp.where(pos < CHUNK - step, pltpu.roll(x, x.shape[0] - step, axis=0), 0.0)
        else:
            x = x + jnp.where(pos >= step, pltpu.roll(x, step, axis=0), 0.0)
        step *= 2
    return x


def _chunk_total(x):
    return _chunk_scan(x, False) + _chunk_scan(x, True) - x


@jax.custom_vjp
def _chunk_sums(x):
    return _chunk_scan(x, False), _chunk_total(x)


def _chunk_sums_fwd(x):
    return _chunk_sums(x), None


def _chunk_sums_bwd(_, ct):
    return (_chunk_scan(ct[0], True) + _chunk_total(ct[1]),)


_chunk_sums.defvjp(_chunk_sums_fwd, _chunk_sums_bwd)


def _hgrn_tile(q, f, i, g, lgt, hg, st):
    d = q.shape[1]
    l0, l1 = lgt[0:1], lgt[1:2]
    mx = jnp.maximum(l0, l1)
    e0, e1 = jnp.exp(l0 - mx), jnp.exp(l1 - mx)
    lb = e0 / (e0 + e1)
    fg = lb + (1.0 - lb) * jax.nn.sigmoid(f)
    k = 1.0 - fg
    causal, _ = _chunk_masks(False)
    b, b_last = _chunk_sums(jnp.log(fg))
    q_in = q * jax.nn.sigmoid(q) * (d ** -0.5) * jnp.exp(b)
    k_in = k * jnp.exp(-b)
    k_end = k * jnp.exp(b_last - b)
    att = jnp.where(causal, _dot_nt(q_in, k_in), 0.0)
    o_intra = _dot_nn(att, i)
    n_chunks = TILE // CHUNK
    chunk_of_row = lax.broadcasted_iota(jnp.int32, (TILE, 1), 0) // CHUNK

    def spread(a):
        return jnp.concatenate([jnp.where(chunk_of_row == n, a, 0.0) for n in range(n_chunks)], axis=1)

    increments = _dot_tn(i, spread(k_end))
    states = []
    for n in range(n_chunks):
        states.append(st)
        decay = jnp.exp(jnp.mean(b_last[n * CHUNK:(n + 1) * CHUNK], axis=0, keepdims=True))
        st = st * decay + increments[:, n * d:(n + 1) * d]
    o = o_intra + _dot_nt(spread(q_in), jnp.concatenate(states, axis=1))
    o = o * lax.rsqrt(jnp.mean(o * o, axis=-1, keepdims=True) + NORM_EPS) * hg
    return o * (g * jax.nn.sigmoid(g)), st


def hgrn_fwd(z, lgt, hg):
    t, d4 = z.shape
    d = d4 // 4
    nh, nt = d // HEAD_DIM, t // TILE
    hp = HEADS_PER_STEP
    wide = hp * HEAD_DIM

    def body(q_ref, f_ref, i_ref, g_ref, lgt_ref, hg_ref, o_ref, st_out_ref, st_ref):
        tt = pl.program_id(1)

        @pl.when(tt == 0)
        def _():
            st_ref[...] = jnp.zeros_like(st_ref)

        for hh in range(hp):
            cols = slice(hh * HEAD_DIM, (hh + 1) * HEAD_DIM)
            st = st_ref[hh]
            st_out_ref[hh] = st
            o, st = _hgrn_tile(q_ref[:, cols], f_ref[:, cols], i_ref[:, cols], g_ref[:, cols], lgt_ref[:, cols],
                               hg_ref[...], st)
            o_ref[:, cols] = o.astype(o_ref.dtype)
            st_ref[hh] = st

    def part(p):
        return pl.BlockSpec((TILE, wide), lambda h, tt: (tt, p * (nh // hp) + h))

    return pl.pallas_call(
        body, name="hgrn_fwd", grid=(nh // hp, nt),
        in_specs=[part(0), part(1), part(2), part(3),
                  pl.BlockSpec((2, wide), lambda h, tt: (0, h)),
                  pl.BlockSpec((1, HEAD_DIM), lambda h, tt: (0, 0))],
        out_specs=[pl.BlockSpec((TILE, wide), lambda h, tt: (tt, h)),
                   pl.BlockSpec((hp, None, HEAD_DIM, HEAD_DIM), lambda h, tt: (h, tt, 0, 0))],
        out_shape=[jax.ShapeDtypeStruct((t, d), BF16),
                   jax.ShapeDtypeStruct((nh, nt, HEAD_DIM, HEAD_DIM), F32)],
        scratch_shapes=[pltpu.VMEM((hp, HEAD_DIM, HEAD_DIM), F32)],
        compiler_params=_params(("parallel", "arbitrary")),
    )(z, z, z, z, lgt, hg)


def hgrn_bwd(z, lgt, hg, states, dout):
    t, d4 = z.shape
    d = d4 // 4
    nh, nt = d // HEAD_DIM, t // TILE
    hp = HEADS_PER_STEP
    wide = hp * HEAD_DIM

    def body(q_ref, f_ref, i_ref, g_ref, lgt_ref, hg_ref, st_in_ref, do_ref, dz_ref, dlgt_ref, dhg_ref, dst_ref):
        h, tt = pl.program_id(0), pl.program_id(1)

        @pl.when(tt == 0)
        def _():
            dst_ref[...] = jnp.zeros_like(dst_ref)
            dlgt_ref[...] = jnp.zeros_like(dlgt_ref)

        @pl.when((tt == 0) & (h == 0))
        def _():
            dhg_ref[...] = jnp.zeros_like(dhg_ref)

        for hh in range(hp):
            cols = slice(hh * HEAD_DIM, (hh + 1) * HEAD_DIM)
            _, vjp = jax.vjp(_hgrn_tile, q_ref[:, cols], f_ref[:, cols], i_ref[:, cols], g_ref[:, cols],
                             lgt_ref[:, cols], hg_ref[...], st_in_ref[hh])
            grads = vjp((do_ref[:, cols], dst_ref[hh]))
            for p in range(4):
                dz_ref[p, :, cols] = grads[p].astype(dz_ref.dtype)
            dlgt_ref[:, cols] += grads[4]
            dhg_ref[...] += grads[5]
            dst_ref[hh] = grads[6]

    def part(p):
        return pl.BlockSpec((TILE, wide), lambda h, tt: (nt - 1 - tt, p * (nh // hp) + h))

    return pl.pallas_call(
        body, name="hgrn_bwd", grid=(nh // hp, nt),
        in_specs=[part(0), part(1), part(2), part(3),
                  pl.BlockSpec((2, wide), lambda h, tt: (0, h)),
                  pl.BlockSpec((1, HEAD_DIM), lambda h, tt: (0, 0)),
                  pl.BlockSpec((hp, None, HEAD_DIM, HEAD_DIM), lambda h, tt: (h, nt - 1 - tt, 0, 0)),
                  pl.BlockSpec((TILE, wide), lambda h, tt: (nt - 1 - tt, h))],
        out_specs=[pl.BlockSpec((4, TILE, wide), lambda h, tt: (0, nt - 1 - tt, h)),
                   pl.BlockSpec((2, wide), lambda h, tt: (0, h)),
                   pl.BlockSpec((1, HEAD_DIM), lambda h, tt: (0, 0))],
        out_shape=[jax.ShapeDtypeStruct((4, t, d), BF16),
                   jax.ShapeDtypeStruct((2, d), F32),
                   jax.ShapeDtypeStruct((1, HEAD_DIM), F32)],
        scratch_shapes=[pltpu.VMEM((hp, HEAD_DIM, HEAD_DIM), F32)],
        compiler_params=_params(("arbitrary", "arbitrary")),
    )(z, z, z, z, lgt, hg, states, dout)


def _log_sigmoid(x):
    return jnp.minimum(x, 0.0) - jnp.log(1.0 + jnp.exp(-jnp.abs(x)))


def decay_fwd(fl_t, b_f):
    nh, t = fl_t.shape

    def body(fl_ref, b_ref, out_ref):
        r = lax.broadcasted_iota(jnp.int32, (128, 128), 0)
        c = lax.broadcasted_iota(jnp.int32, (128, 128), 1)
        upper = (r <= c).astype(F32)
        carry = jnp.zeros((nh, 1), F32)
        for j in range(t // 128):
            cols = slice(j * 128, (j + 1) * 128)
            ls = _log_sigmoid(fl_ref[:, cols] + b_ref[...])
            out_ref[:, cols] = carry + jnp.dot(ls, upper, precision=lax.Precision.HIGHEST,
                                               preferred_element_type=F32)
            carry = carry + jnp.sum(ls, axis=1, keepdims=True)

    return pl.pallas_call(body, name="decay_fwd", out_shape=jax.ShapeDtypeStruct((nh, t), F32),
                          compiler_params=_params(None))(fl_t, b_f)


def decay_bwd(fl_t, b_f, ddcum):
    nh, t = fl_t.shape

    def body(fl_ref, b_ref, dd_ref, dfl_ref, db_ref):
        r = lax.broadcasted_iota(jnp.int32, (128, 128), 0)
        c = lax.broadcasted_iota(jnp.int32, (128, 128), 1)
        lower = (r >= c).astype(F32)
        carry = jnp.zeros((nh, 1), F32)
        db = jnp.zeros((nh, 1), F32)
        for j in reversed(range(t // 128)):
            cols = slice(j * 128, (j + 1) * 128)
            dd = dd_ref[:, cols]
            dls = carry + jnp.dot(dd, lower, precision=lax.Precision.HIGHEST, preferred_element_type=F32)
            carry = carry + jnp.sum(dd, axis=1, keepdims=True)
            dfl = dls * jax.nn.sigmoid(-(fl_ref[:, cols] + b_ref[...]))
            dfl_ref[:, cols] = dfl
            db = db + jnp.sum(dfl, axis=1, keepdims=True)
        db_ref[...] = db

    return pl.pallas_call(body, name="decay_bwd",
                          out_shape=[jax.ShapeDtypeStruct((nh, t), F32), jax.ShapeDtypeStruct((nh, 1), F32)],
                          compiler_params=_params(None))(fl_t, b_f, ddcum)


def _attn_parts(t):
    tq = _tile(t, (256, 128))
    per_part = 2 if t // tq >= 4 else 1
    return tq, [(first, per_part, (first + per_part) * tq) for first in range(0, t // tq, per_part)]


def _attn_logits(q_ref, k_ref, dcol_ref, drow_ref, row0, tq, keys):
    qs = (q_ref[...] * (HEAD_DIM ** -0.5)).astype(BF16)
    s = lax.dot_general(qs, k_ref[...], NT, preferred_element_type=F32)
    s = s + dcol_ref[...] - drow_ref[...]
    row = row0 + lax.broadcasted_iota(jnp.int32, (tq, keys), 0)
    col = lax.broadcasted_iota(jnp.int32, (tq, keys), 1)
    return qs, jnp.where(col <= row, s, NEG_BIG)


def attn_fwd(q, k, v, dcol, drow):
    t, d = q.shape
    nh = d // HEAD_DIM
    tq, parts = _attn_parts(t)
    o = lse = None
    for first, count, keys in parts:
        def body(q_ref, k_ref, v_ref, dcol_ref, drow_ref, *rest, first=first, keys=keys):
            o_ref, lse_ref = rest[-2:]
            _, s = _attn_logits(q_ref, k_ref, dcol_ref, drow_ref, (first + pl.program_id(1)) * tq, tq, keys)
            m = jnp.max(s, axis=1, keepdims=True)
            p = jnp.exp(s - m)
            l = jnp.sum(p, axis=1, keepdims=True)
            acc = jnp.dot(p.astype(BF16), v_ref[...], preferred_element_type=F32)
            o_ref[...] = (acc / l).astype(o_ref.dtype)
            lse_ref[...] = m + jnp.log(l)

        tile = pl.BlockSpec((tq, HEAD_DIM), lambda h, i, first=first: (first + i, h))
        col = pl.BlockSpec((None, tq, 1), lambda h, i, first=first: (h, first + i, 0))
        seen = pl.BlockSpec((keys, HEAD_DIM), lambda h, i: (0, h))
        carried = [] if o is None else [o, lse]
        o, lse = pl.pallas_call(
            body, name=f"attn_fwd_{first}", grid=(nh, count),
            in_specs=[tile, seen, seen, col, pl.BlockSpec((None, 1, keys), lambda h, i: (h, 0, 0))]
            + [pl.BlockSpec(memory_space=pl.ANY)] * len(carried),
            out_specs=[tile, col],
            out_shape=[jax.ShapeDtypeStruct((t, d), BF16), jax.ShapeDtypeStruct((nh, t, 1), F32)],
            input_output_aliases={5: 0, 6: 1} if carried else {},
            compiler_params=_params(("parallel", "parallel")),
        )(q, k, v, dcol, drow, *carried)
    return o, lse


def attn_bwd(q, k, v, dcol, drow, lse, do):
    t, d = q.shape
    nh = d // HEAD_DIM
    tq, parts = _attn_parts(t)
    dq = dk = dv = ddrow = None
    for first, count, keys in reversed(parts):
        first_call = dq is None

        def body(q_ref, k_ref, v_ref, dcol_ref, drow_ref, lse_ref, do_ref, *rest, first=first, keys=keys,
                 count=count, first_call=first_call):
            dq_ref, dk_ref, dv_ref, ddrow_ref, dk_acc, dv_acc, ddrow_acc = rest[-7:]
            i = pl.program_id(1)

            @pl.when(i == 0)
            def _():
                if first_call:
                    dk_acc[...] = jnp.zeros_like(dk_acc)
                    dv_acc[...] = jnp.zeros_like(dv_acc)
                    ddrow_acc[...] = jnp.zeros_like(ddrow_acc)
                else:
                    dk_acc[...] = rest[1][...]
                    dv_acc[...] = rest[2][...]
                    ddrow_acc[...] = rest[3][...]

            qs, s = _attn_logits(q_ref, k_ref, dcol_ref, drow_ref, (first + i) * tq, tq, keys)
            p = jnp.exp(s - lse_ref[...])
            do = do_ref[...]
            dp = lax.dot_general(do, v_ref[...], NT, preferred_element_type=F32)
            ds = p * (dp - jnp.sum(p * dp, axis=1, keepdims=True))
            dsb = ds.astype(BF16)
            dq_ref[...] = (jnp.dot(dsb, k_ref[...], preferred_element_type=F32) * (HEAD_DIM ** -0.5)).astype(dq_ref.dtype)
            dk_acc[...] += lax.dot_general(dsb, qs, TN, preferred_element_type=F32)
            dv_acc[...] += lax.dot_general(p.astype(BF16), do, TN, preferred_element_type=F32)
            ddrow_acc[...] -= jnp.sum(ds, axis=0, keepdims=True)

            @pl.when(i == count - 1)
            def _():
                dk_ref[...] = dk_acc[...]
                dv_ref[...] = dv_acc[...]
                ddrow_ref[...] = ddrow_acc[...]

        tile = pl.BlockSpec((tq, HEAD_DIM), lambda h, i, first=first: (first + i, h))
        col = pl.BlockSpec((None, tq, 1), lambda h, i, first=first: (h, first + i, 0))
        seen = pl.BlockSpec((keys, HEAD_DIM), lambda h, i: (0, h))
        seen_row = pl.BlockSpec((None, 1, keys), lambda h, i: (h, 0, 0))
        carried = [] if first_call else [dq, dk, dv, ddrow]
        carried_specs = [] if first_call else [pl.BlockSpec(memory_space=pl.ANY), seen, seen, seen_row]
        dq, dk, dv, ddrow = pl.pallas_call(
            body, name=f"attn_bwd_{first}", grid=(nh, count),
            in_specs=[tile, seen, seen, col, seen_row, col, tile] + carried_specs,
            out_specs=[tile, seen, seen, seen_row],
            out_shape=[jax.ShapeDtypeStruct((t, d), BF16), jax.ShapeDtypeStruct((t, d), F32),
                       jax.ShapeDtypeStruct((t, d), F32), jax.ShapeDtypeStruct((nh, 1, t), F32)],
            scratch_shapes=[pltpu.VMEM((keys, HEAD_DIM), F32), pltpu.VMEM((keys, HEAD_DIM), F32),
                            pltpu.VMEM((1, keys), F32)],
            input_output_aliases={} if first_call else {7: 0, 8: 1, 9: 2, 10: 3},
            compiler_params=_params(("parallel", "arbitrary")),
        )(q, k, v, dcol, drow, lse, do, *carried)
    return dq, dk, dv, ddrow


def _my_index():
    return (lax.axis_index("x") * 2 + lax.axis_index("y")) * 2 + lax.axis_index("c")


def _exchange(name, src, gather, deps=()):
    shape = src.shape if gather else src.shape[1:]

    def body(src_ref, *rest):
        out_ref, send_sems, recv_sems, local_sem = rest[len(deps):]
        x, y, c = (lax.axis_index(a) for a in MESH_AXES)
        me = (x * 2 + y) * 2 + c
        mine = src_ref if gather else src_ref.at[me]
        local = pltpu.make_async_copy(mine, out_ref.at[me], local_sem)
        local.start()
        copies = []
        for dlt in range(1, N_DEV):
            dx, dy, dc = dlt // 4, (dlt // 2) % 2, dlt % 2
            px, py, pc = x ^ dx, y ^ dy, c ^ dc
            peer = (px * 2 + py) * 2 + pc
            copies.append(pltpu.make_async_remote_copy(
                src_ref=src_ref if gather else src_ref.at[peer], dst_ref=out_ref.at[me],
                send_sem=send_sems.at[dlt - 1], recv_sem=recv_sems.at[dlt - 1],
                device_id=(px, py, pc), device_id_type=pl.DeviceIdType.MESH))
        for cp in copies:
            cp.start()
        for cp in copies:
            cp.wait_recv()
        for cp in copies:
            cp.wait_send()
        local.wait()

    return pl.pallas_call(
        body, name=name, out_shape=jax.ShapeDtypeStruct((N_DEV,) + tuple(shape), src.dtype),
        in_specs=[pl.BlockSpec(memory_space=pl.ANY)] * (1 + len(deps)), out_specs=pl.BlockSpec(memory_space=pl.ANY),
        scratch_shapes=[pltpu.SemaphoreType.DMA((N_DEV - 1,)), pltpu.SemaphoreType.DMA((N_DEV - 1,)),
                        pltpu.SemaphoreType.DMA],
        compiler_params=pltpu.CompilerParams(has_side_effects=True),
    )(src, *deps)


def all_gather(name, x, deps=()):
    return _exchange(name, x, True, deps)


_HBM = pl.BlockSpec(memory_space=pltpu.HBM)
_SEM = pl.BlockSpec(memory_space=pltpu.SEMAPHORE)
_DATAFLOW = pltpu.SideEffectType.DATAFLOW_SIDE_EFFECTING


def _peer_copies(src_ref, land_ref, send_sems, recv_sems, gather):
    x, y, c = (lax.axis_index(a) for a in MESH_AXES)
    me = (x * 2 + y) * 2 + c
    copies = []
    for dlt in range(1, N_DEV):
        px, py, pc = x ^ (dlt // 4), y ^ ((dlt // 2) % 2), c ^ (dlt % 2)
        peer = (px * 2 + py) * 2 + pc
        copies.append(pltpu.make_async_remote_copy(
            src_ref=src_ref if gather else src_ref.at[peer], dst_ref=land_ref.at[me],
            send_sem=send_sems.at[dlt - 1], recv_sem=recv_sems.at[dlt - 1],
            device_id=(px, py, pc), device_id_type=pl.DeviceIdType.MESH))
    return copies


def exchange_start(name, srcs, gather):
    n = len(srcs)
    lands = [lax.empty((N_DEV,) + tuple(s.shape if gather else s.shape[1:]), s.dtype) for s in srcs]

    def body(*refs):
        src_refs, land_refs = refs[:n], refs[n:2 * n]
        send_sems, recv_sems = refs[2 * n:3 * n], refs[3 * n:4 * n]
        token = refs[-1]
        for j in range(n):
            for cp in _peer_copies(src_refs[j], land_refs[j], send_sems[j], recv_sems[j], gather):
                cp.start()
        token[...] = jnp.zeros_like(token)

    sems = [pltpu.SemaphoreType.DMA((N_DEV - 1,))] * (2 * n)
    thru = [pltpu.HBM(a.shape, a.dtype) for a in list(srcs) + lands]
    outs = pl.pallas_call(
        body, name=name, out_shape=tuple(sems + thru + [jax.ShapeDtypeStruct((8, 128), F32)]),
        in_specs=[_HBM] * (2 * n), out_specs=tuple([_SEM] * (2 * n) + [_HBM] * (2 * n) + [pl.BlockSpec(memory_space=pltpu.VMEM)]),
        input_output_aliases={j: 2 * n + j for j in range(2 * n)},
        compiler_params=pltpu.CompilerParams(has_side_effects=_DATAFLOW),
    )(*[pltpu.with_memory_space_constraint(a, pltpu.HBM) for a in list(srcs) + lands])
    handles = [(outs[j], outs[n + j], outs[2 * n + j], outs[3 * n + j]) for j in range(n)]
    return handles, outs[-1]


def exchange_wait(name, handle, after, gather):
    send_sems, recv_sems, src, land = handle

    def body(src_ref, land_ref, send_ref, recv_ref, after_ref, src_out, land_out):
        for cp in _peer_copies(src_ref, land_ref, send_ref, recv_ref, gather):
            cp.wait_send()
            cp.wait_recv()

    return pl.pallas_call(
        body, name=name, out_shape=(pltpu.HBM(src.shape, src.dtype), pltpu.HBM(land.shape, land.dtype)),
        in_specs=[_HBM, _HBM, _SEM, _SEM, pl.BlockSpec(memory_space=pl.ANY)], out_specs=(_HBM, _HBM),
        input_output_aliases={0: 0, 1: 1},
        compiler_params=pltpu.CompilerParams(has_side_effects=_DATAFLOW),
    )(src, land, send_sems, recv_sems, after)


N_OTHER_CHIPS = 3


def _two_level_places():
    x, y, c = (lax.axis_index(a) for a in MESH_AXES)
    return (x, y, c), (x * 2 + y) * 2 + c, (x, y, 1 - c), [(1 - x, y), (x, 1 - y), (1 - x, 1 - y)]


def _first_copies(land_ref, send_sems, recv_sems):
    (x, y, c), me, other_core, chips = _two_level_places()
    targets = [other_core] + [(cx, cy, c) for cx, cy in chips]
    return [pltpu.make_async_remote_copy(
        src_ref=land_ref.at[me], dst_ref=land_ref.at[me], send_sem=send_sems.at[k], recv_sem=recv_sems.at[k],
        device_id=to, device_id_type=pl.DeviceIdType.MESH) for k, to in enumerate(targets)]


def _passed_on_copies(land_ref, send_sems, recv_sems):
    (x, y, c), me, other_core, chips = _two_level_places()
    copies = []
    for k, (cx, cy) in enumerate(chips):
        slot = land_ref.at[(cx * 2 + cy) * 2 + c]
        copies.append(pltpu.make_async_remote_copy(
            src_ref=slot, dst_ref=slot, send_sem=send_sems.at[k], recv_sem=recv_sems.at[k],
            device_id=other_core, device_id_type=pl.DeviceIdType.MESH))
    return copies


def gather_start(name, lands):
    n = len(lands)

    def body(*refs):
        land_refs, send_sems, recv_sems = refs[:n], refs[n:2 * n], refs[2 * n:3 * n]
        for j in range(n):
            for cp in _first_copies(land_refs[j], send_sems[j], recv_sems[j]):
                cp.start()

    sems = [pltpu.SemaphoreType.DMA((1 + N_OTHER_CHIPS,))] * (2 * n)
    outs = pl.pallas_call(
        body, name=name, out_shape=tuple(sems + [pltpu.HBM(a.shape, a.dtype) for a in lands]),
        in_specs=[_HBM] * n, out_specs=tuple([_SEM] * (2 * n) + [_HBM] * n),
        input_output_aliases={j: 2 * n + j for j in range(n)},
        compiler_params=pltpu.CompilerParams(has_side_effects=_DATAFLOW),
    )(*[pltpu.with_memory_space_constraint(a, pltpu.HBM) for a in lands])
    return [[outs[j], outs[n + j], outs[2 * n + j]] for j in range(n)]


def gather_pass_on(name, handle, after):
    send_sems, recv_sems, land = handle

    def body(land_ref, recv_ref, after_ref, land_out, send2, recv2, token):
        arrivals = _first_copies(land_ref, recv_ref, recv_ref)
        for k, cp in enumerate(_passed_on_copies(land_ref, send2, recv2)):
            arrivals[1 + k].wait_recv()
            cp.start()
        token[...] = jnp.zeros_like(token)

    sem3 = pltpu.SemaphoreType.DMA((N_OTHER_CHIPS,))
    land, send2, recv2, token = pl.pallas_call(
        body, name=name,
        out_shape=(pltpu.HBM(land.shape, land.dtype), sem3, sem3, jax.ShapeDtypeStruct((8, 128), F32)),
        in_specs=[_HBM, _SEM, pl.BlockSpec(memory_space=pl.ANY)],
        out_specs=(_HBM, _SEM, _SEM, pl.BlockSpec(memory_space=pltpu.VMEM)),
        input_output_aliases={0: 0}, compiler_params=pltpu.CompilerParams(has_side_effects=_DATAFLOW),
    )(land, recv_sems, after)
    return [send_sems, recv_sems, land, send2, recv2], token


def gather_wait(name, handle, after):
    send_sems, recv_sems, land, send2, recv2 = handle

    def body(land_ref, send_ref, recv_ref, send2_ref, recv2_ref, after_ref, land_out):
        first = _first_copies(land_ref, send_ref, recv_ref)
        for cp in first:
            cp.wait_send()
        first[0].wait_recv()
        for cp in _passed_on_copies(land_ref, send2_ref, recv2_ref):
            cp.wait_send()
            cp.wait_recv()

    return pl.pallas_call(
        body, name=name, out_shape=pltpu.HBM(land.shape, land.dtype),
        in_specs=[_HBM, _SEM, _SEM, _SEM, _SEM, pl.BlockSpec(memory_space=pl.ANY)], out_specs=_HBM,
        input_output_aliases={0: 0}, compiler_params=pltpu.CompilerParams(has_side_effects=_DATAFLOW),
    )(land, send_sems, recv_sems, send2, recv2, after)


def adamw_reduce(name, parts, w, m, v):
    nl, r, wd = w.shape
    tr = _row_tile(r, 2 * ROW_TILE_BYTES // (8 * wd))

    def body(*refs):
        p_refs = refs[:nl]
        w_ref, m_ref, v_ref, g_ref, d_ref, nm_ref, nv_ref = refs[nl:]
        layer = pl.program_id(0)
        for j in range(nl):
            @pl.when(layer == j)
            def _(j=j):
                g = p_refs[j][0].astype(F32)
                for dev in range(1, N_DEV):
                    g = g + p_refs[j][dev].astype(F32)
                nm = B1 * m_ref[...] + (1.0 - B1) * g
                nv = B2 * v_ref[...] + (1.0 - B2) * jnp.square(g)
                m_hat = nm / (1.0 - B1 ** STEP)
                v_hat = nv / (1.0 - B2 ** STEP)
                g_ref[...] = g
                d_ref[...] = -LR * (m_hat / (jnp.sqrt(v_hat) + ADAM_EPS) + WD * w_ref[...])
                nm_ref[...] = nm
                nv_ref[...] = nv

    def part_spec(j):
        return pl.BlockSpec((N_DEV, tr, wd), lambda l, i: (0, jnp.where(l == j, i, 0), 0))

    spec = pl.BlockSpec((None, tr, wd), lambda l, i: (l, i, 0))
    return pl.pallas_call(
        body, name=name, grid=(nl, r // tr),
        in_specs=[part_spec(j) for j in range(nl)] + [spec, spec, spec],
        out_specs=[spec] * 4, out_shape=[jax.ShapeDtypeStruct((nl, r, wd), F32)] * 4,
        compiler_params=_params(("arbitrary", "arbitrary")),
    )(*parts, w, m, v)


def _pack_rows(vectors, rows=None):
    flat = jnp.concatenate([a.reshape(-1).astype(F32) for a in vectors])
    n = flat.shape[0]
    if rows is None:
        rows = -(-n // 1024) * 8
    return jnp.pad(flat, (0, rows * 128 - n)).reshape(rows, 128)


def _unpack_rows(packed, like):
    flat = packed.reshape(-1)
    out, pos = [], 0
    for a in like:
        out.append(flat[pos:pos + a.size].reshape(a.shape))
        pos += a.size
    return out


def kernel(x, p, mix_norm, mlp_norm, ple_norm, w_a_in, a_lb_logits, a_head_gain, w_a_out, kv_norm, w_kvf, b_f, w_b_q, w_b_out, w_mlp_up, w_mlp_down, w_ple_gate, w_ple_up, final_norm, loss_target, m_mix_norm, m_mlp_norm, m_ple_norm, m_w_a_in, m_a_lb_logits, m_a_head_gain, m_w_a_out, m_kv_norm, m_w_kvf, m_b_f, m_w_b_q, m_w_b_out, m_w_mlp_up, m_w_mlp_down, m_w_ple_gate, m_w_ple_up, m_final_norm, v_mix_norm, v_mlp_norm, v_ple_norm, v_w_a_in, v_a_lb_logits, v_a_head_gain, v_w_a_out, v_kv_norm, v_w_kvf, v_b_f, v_w_b_q, v_w_b_out, v_w_mlp_up, v_w_mlp_down, v_w_ple_gate, v_w_ple_up, v_final_norm):
    t, d = x.shape[1], x.shape[2]
    nh = d // HEAD_DIM
    n_layers = 2
    x2 = x.reshape(t, d)
    target = loss_target.reshape(t, d)
    me = _my_index()

    shards = {"w_a_in": w_a_in[0], "w_a_out": w_a_out[0], "w_kvf": w_kvf, "w_b_q": w_b_q[0], "w_b_out": w_b_out[0]}
    for l in range(n_layers):
        shards.update({f"w_mlp_up{l}": w_mlp_up[l], f"w_mlp_down{l}": w_mlp_down[l],
                       f"w_ple_gate{l}": w_ple_gate[l], f"w_ple_up{l}": w_ple_up[l]})
    first_use = ["a_lb_logits", "w_a_in", "w_a_out", "w_mlp_up0", "w_mlp_down0", "w_ple_gate0", "w_ple_up0", "w_kvf",
                 "w_b_q", "w_b_out", "w_mlp_up1", "w_mlp_down1", "w_ple_gate1", "w_ple_up1"]
    row_sharded = ("w_a_out", "w_b_q", "w_b_out", "w_mlp_down", "w_ple_gate")
    shards_bf = [a_lb_logits] + [shards[n].astype(BF16) for n in first_use[1:]]
    ag_handles = gather_start("ag_start", [
        lax.dynamic_update_slice(lax.empty((N_DEV,) + a.shape, a.dtype), a[None], (me, 0, 0)) for a in shards_bf])
    passed_on = {}
    weights = {}

    def pass_on(j, after):
        if j < len(first_use) and j not in passed_on:
            passed_on[j] = gather_pass_on("ag_pass_" + first_use[j], ag_handles[j], after)

    def weight(name, after=None):
        if name not in weights:
            j = first_use.index(name)
            pass_on(j, after)
            pass_on(j + 1, after)
            behind = passed_on[j + 1][1] if j + 1 in passed_on else after
            g = gather_wait("ag_wait_" + name, passed_on[j][0], behind)
            if name.rstrip("01") in row_sharded:
                g = g.reshape(1, g.shape[0] * g.shape[1], g.shape[2])
            weights[name] = g
        return weights[name]

    lgt = weight("a_lb_logits", x2).transpose(1, 0, 2).reshape(2, d)
    p_bf = [p[l, 0].astype(BF16) for l in range(n_layers)]

    def row(vec):
        return vec.reshape(1, -1)

    def mlp_ple_fwd(l, h_in, a):
        (h_a, u_mlp), _ = rowwise(f"add_norm_mlp{l}", _add_norm_fwd, [h_in, a], [row(mlp_norm[l])])
        pre, act = mm_nn(f"mlp_up{l}", u_mlp, weight(f"w_mlp_up{l}", u_mlp), fuse=(_relu2, (), (BF16, BF16)))
        mo = mm_nn(f"mlp_down{l}", act, weight(f"w_mlp_down{l}", act))
        (h_b, u_ple), _ = rowwise(f"add_norm_ple{l}", _add_norm_fwd, [h_a, mo], [row(ple_norm[l])])
        gpre = mm_nn(f"ple_gate{l}", u_ple, weight(f"w_ple_gate{l}", u_ple))
        pu = mm_nn(f"ple_up{l}", p_bf[l], weight(f"w_ple_up{l}", gpre))
        return dict(h_a=h_a, u_mlp=u_mlp, pre=pre, act=act, h_b=h_b, u_ple=u_ple, gpre=gpre, pu=pu)

    (u0,), _ = rowwise("norm_mix0", _norm_fwd, [x2], [row(mix_norm[0])])
    z = mm_nn("a_in", u0, weight("w_a_in", u0))
    og, states = hgrn_fwd(z, lgt, a_head_gain)
    a0 = mm_nn("a_out", og, weight("w_a_out", og))
    s0 = mlp_ple_fwd(0, x2, a0)
    (h3, u_kv, u1), _ = rowwise("ple_norms", _ple_two_norms_fwd, [s0["h_b"], s0["gpre"], s0["pu"]],
                                [row(kv_norm), row(mix_norm[1])])
    hk = mm_nn("kvf", u_kv, weight("w_kvf", u_kv), out3=True)
    hk = hk.transpose(1, 0, 2).reshape(t, -1)
    k_bf, v_bf = hk[:, :d].astype(BF16), hk[:, d:2 * d].astype(BF16)
    fl_t = hk[:, 2 * d:].T
    b_f_col = b_f.reshape(nh, 1)
    dcum = decay_fwd(fl_t, b_f_col)
    dcol, drow = dcum.reshape(nh, t, 1), dcum.reshape(nh, 1, t)
    q = mm_nn("b_q", u1, weight("w_b_q", dcum))
    o, lse = attn_fwd(q, k_bf, v_bf, dcol, drow)
    a1 = mm_nn("b_out", o, weight("w_b_out", o))
    s1 = mlp_ple_fwd(1, h3, a1)

    (dh, dgpre, dpu), (d_final, loss_rows) = rowwise(
        "tail", _tail_fwd_bwd, [s1["h_b"], s1["gpre"], s1["pu"], target], [row(final_norm)])

    sent = {}
    tokens = []

    def send_grad(name, g):
        g = g.reshape(N_DEV, -1, g.shape[-1])
        (handle,), token = exchange_start("rs_start_" + name, [g], False)
        sent[name] = handle
        tokens.append(token)

    def after_sends():
        deps = tuple(tokens)
        tokens.clear()
        return deps

    def mlp_ple_bwd(l, s, dh, dgpre, dpu):
        send_grad(f"w_ple_gate{l}", mm_tn(f"d_ple_gate_w{l}", s["u_ple"], dgpre, 1, deps=after_sends()))
        send_grad(f"w_ple_up{l}", mm_tn(f"d_ple_up_w{l}", p_bf[l], dpu, N_DEV, deps=after_sends()))
        du = mm_nt(f"d_ple_gate_x{l}", dgpre, weight(f"w_ple_gate{l}"), deps=after_sends())
        (dh, dh_bf), (d_ple,) = rowwise(f"d_norm_ple{l}", _norm_bwd, [s["h_b"], du, dh], [row(ple_norm[l])])
        send_grad(f"w_mlp_down{l}", mm_tn(f"d_mlp_down_w{l}", s["act"], dh_bf, 1))
        (dpre,) = mm_nt(f"d_mlp_down_x{l}", dh_bf, weight(f"w_mlp_down{l}"), deps=after_sends(),
                        fuse=(_relu2_bwd, (s["pre"],), (BF16,)))
        send_grad(f"w_mlp_up{l}", mm_tn(f"d_mlp_up_w{l}", s["u_mlp"], dpre, N_DEV))
        du = mm_nt(f"d_mlp_up_x{l}", dpre, weight(f"w_mlp_up{l}"), deps=after_sends())
        (dh, dh_bf), (d_mlp,) = rowwise(f"d_norm_mlp{l}", _norm_bwd, [s["h_a"], du, dh], [row(mlp_norm[l])])
        return dh, dh_bf, d_ple, d_mlp

    dh, dh_bf, d_ple1, d_mlp1 = mlp_ple_bwd(1, s1, dh, dgpre, dpu)
    send_grad("w_b_out", mm_tn("d_b_out_w", o, dh_bf, 1))
    do = mm_nt("d_b_out_x", dh_bf, weight("w_b_out"), out_dtype=BF16, deps=after_sends())
    dq, dk, dv, ddrow = attn_bwd(q, k_bf, v_bf, dcol, drow, lse, do)
    send_grad("w_b_q", mm_tn("d_b_q_w", u1, dq, 1))
    du1 = mm_nt("d_b_q_x", dq, weight("w_b_q"), deps=after_sends())
    dfl_t, d_b_f = decay_bwd(fl_t, b_f_col, ddrow.reshape(nh, t))
    dhk = jnp.concatenate([dk.astype(BF16), dv.astype(BF16), dfl_t.T.astype(BF16)], axis=1)
    dhk = dhk.reshape(t, N_DEV, -1).transpose(1, 0, 2)
    send_grad("w_kvf", mm_tn("d_kvf_w", u_kv, dhk, N_DEV))
    du_kv = mm_nt("d_kvf_x", dhk, weight("w_kvf"), deps=after_sends())
    (dh,), (d_kv_norm, d_mix1) = rowwise("d_ple_norms", _two_norms_bwd, [h3, du_kv, du1, dh],
                                         [row(kv_norm), row(mix_norm[1])])
    (dgpre, dpu), _ = rowwise("d_ple0", _ple_bwd, [s0["gpre"], s0["pu"], dh])
    dh, dh_bf, d_ple0, d_mlp0 = mlp_ple_bwd(0, s0, dh, dgpre, dpu)
    send_grad("w_a_out", mm_tn("d_a_out_w", og, dh_bf, 1))
    dog = mm_nt("d_a_out_x", dh_bf, weight("w_a_out"), deps=after_sends())
    dz4, d_lgt, d_hg = hgrn_bwd(z, lgt, a_head_gain, states, dog)
    send_grad("w_a_in", mm_tn("d_a_in_w", u0, dz4, N_DEV, stacked=True))
    du0 = mm_nt("d_a_in_x", dz4, weight("w_a_in"), deps=after_sends(), stacked=True)
    (dx, _), (d_mix0,) = rowwise("d_norm_mix0", _norm_bwd, [x2, du0, dh], [row(mix_norm[0])])

    new = {}
    last = [dx]

    def update(name, parts, w, m, v):
        shp = w.shape
        w3, m3, v3 = (a.reshape(len(parts), -1, shp[-1]) for a in (w, m, v))
        new[name] = tuple(a.reshape(shp) for a in adamw_reduce("adamw_" + name, parts, w3, m3, v3))
        last[0] = new[name][0]

    def receive_update(name, layers, w, m, v):
        parts = {}
        for sfx in layers:
            g, land = exchange_wait(f"rs_wait_{name}{sfx}", sent[name + sfx], last[0], False)
            parts[sfx] = lax.dynamic_update_slice(land, lax.dynamic_slice_in_dim(g, me, 1, 0), (me, 0, 0))
        update(name, [parts[sfx] for sfx in sorted(layers)], w, m, v)

    both = ("1", "0")
    receive_update("w_b_out", ("",), w_b_out, m_w_b_out, v_w_b_out)
    receive_update("w_b_q", ("",), w_b_q, m_w_b_q, v_w_b_q)
    receive_update("w_kvf", ("",), w_kvf, m_w_kvf, v_w_kvf)
    receive_update("w_ple_gate", both, w_ple_gate, m_w_ple_gate, v_w_ple_gate)
    receive_update("w_ple_up", both, w_ple_up, m_w_ple_up, v_w_ple_up)
    receive_update("w_mlp_down", both, w_mlp_down, m_w_mlp_down, v_w_mlp_down)
    receive_update("w_mlp_up", both, w_mlp_up, m_w_mlp_up, v_w_mlp_up)
    receive_update("w_a_out", ("",), w_a_out, m_w_a_out, v_w_a_out)
    receive_update("w_a_in", ("",), w_a_in, m_w_a_in, v_w_a_in)

    small = dict(mix_norm=jnp.concatenate([d_mix0, d_mix1]), mlp_norm=jnp.concatenate([d_mlp0, d_mlp1]),
                 ple_norm=jnp.concatenate([d_ple0, d_ple1]), a_head_gain=d_hg, kv_norm=d_kv_norm.reshape(d),
                 b_f=d_b_f.reshape(nh), final_norm=d_final.reshape(d))
    small_w = dict(mix_norm=(mix_norm, m_mix_norm, v_mix_norm), mlp_norm=(mlp_norm, m_mlp_norm, v_mlp_norm),
                   ple_norm=(ple_norm, m_ple_norm, v_ple_norm),
                   a_head_gain=(a_head_gain, m_a_head_gain, v_a_head_gain), kv_norm=(kv_norm, m_kv_norm, v_kv_norm),
                   b_f=(b_f, m_b_f, v_b_f), final_norm=(final_norm, m_final_norm, v_final_norm))
    names = list(small)
    packed = _pack_rows([d_lgt] + [small[n] for n in names])
    everyone = all_gather("ag_small_grads", packed, deps=(last[0],))
    n_lgt_rows = d_lgt.size // 128
    lgt_parts = everyone[:, :n_lgt_rows].reshape(N_DEV, 2, d)
    lgt_parts = lax.dynamic_slice_in_dim(lgt_parts, me * a_lb_logits.shape[1], a_lb_logits.shape[1], axis=2)
    update("a_lb_logits", [lgt_parts], a_lb_logits, m_a_lb_logits, v_a_lb_logits)
    rest = everyone[:, n_lgt_rows:]
    like = [small_w[n][0] for n in names]
    packed_w, packed_m, packed_v = (_pack_rows([small_w[n][j] for n in names], rest.shape[1])[None] for j in range(3))
    outs = adamw_reduce("adamw_small", [rest], packed_w, packed_m, packed_v)
    unpacked = [_unpack_rows(a, like) for a in outs]
    for j, n in enumerate(names):
        new[n] = tuple(unpacked[q][j] for q in range(4))

    order = ["mix_norm", "mlp_norm", "ple_norm", "w_a_in", "a_lb_logits", "a_head_gain", "w_a_out", "kv_norm",
             "w_kvf", "b_f", "w_b_q", "w_b_out", "w_mlp_up", "w_mlp_down", "w_ple_gate", "w_ple_up", "final_norm"]
    loss_here, _ = lax.optimization_barrier((loss_rows[0, 0], new["final_norm"][0]))
    loss = lax.psum(loss_here, MESH_AXES)
    result = [loss, dx.reshape(x.shape)]
    for j in range(4):
        result += [new[n][j] for n in order]
    return tuple(result)
```

```python
import functools

import jax
import jax.numpy as jnp
from jax import lax
from jax.experimental import pallas as pl
from jax.experimental.pallas import tpu as pltpu

F32 = jnp.float32
BF16 = jnp.bfloat16
HEAD_DIM = 128
CHUNK = 16
TILE = 128
HEADS_PER_STEP = 2
NORM_EPS = 1e-6
N_DEV = 8
MESH_AXES = ("x", "y", "c")
VMEM_LIMIT_BYTES = 48 * 1024 * 1024
ROW_TILE_BYTES = 1024 * 1024
LR, B1, B2, ADAM_EPS, WD, STEP = 0.001, 0.9, 0.999, 1e-08, 0.01, 10
NEG_BIG = -1e30

NN = (((1,), (0,)), ((), ()))
NT = (((1,), (1,)), ((), ()))
TN = (((0,), (0,)), ((), ()))


def _params(semantics):
    return pltpu.CompilerParams(dimension_semantics=semantics, vmem_limit_bytes=VMEM_LIMIT_BYTES)


def _tile(n, prefs):
    for p in prefs:
        if n % p == 0:
            return p
    return n


def _row_tile(rows, limit):
    for cand in (2048, 1024, 512, 256, 128, 64, 32, 16):
        if cand <= limit and rows % cand == 0:
            return cand
    return rows


def _mm_call(name, a, b, dims, grid, a_spec, b_spec, o_spec, o_shape, acc_shape, k_axes, out_dtype, deps=(),
             fuse=None):
    nk = 1
    for ax in k_axes:
        nk *= grid[ax]
    fn, extra, out_dtypes = fuse if fuse else (lambda acc: (acc,), (), (out_dtype,))
    n_extra, n_out = len(extra), len(out_dtypes)

    def finish(acc, rest):
        o_refs = rest[n_extra + len(deps):n_extra + len(deps) + n_out]
        for ref, val in zip(o_refs, fn(acc, *[r[...] for r in rest[:n_extra]])):
            ref[...] = val.astype(ref.dtype)

    def one_step(a_ref, b_ref, *rest):
        finish(lax.dot_general(a_ref[...], b_ref[...], dims, preferred_element_type=F32), rest)

    def accumulate(a_ref, b_ref, *rest):
        acc_ref = rest[-1]
        k = 0
        for ax in k_axes:
            k = k * grid[ax] + pl.program_id(ax)
        part = lax.dot_general(a_ref[...], b_ref[...], dims, preferred_element_type=F32)

        @pl.when(k == 0)
        def _():
            acc_ref[...] = part

        @pl.when((k > 0) & (k < nk - 1))
        def _():
            acc_ref[...] += part

        @pl.when(k == nk - 1)
        def _():
            finish(acc_ref[...] + part, rest)

    sem = tuple("arbitrary" if ax in k_axes else "parallel" for ax in range(len(grid)))
    outs = pl.pallas_call(
        one_step if nk == 1 else accumulate, name=name, grid=grid,
        in_specs=[a_spec, b_spec] + [o_spec] * n_extra + [pl.BlockSpec(memory_space=pl.ANY)] * len(deps),
        out_specs=[o_spec] * n_out, out_shape=[jax.ShapeDtypeStruct(o_shape, dt) for dt in out_dtypes],
        scratch_shapes=[] if nk == 1 else [pltpu.VMEM(acc_shape, F32)], compiler_params=_params(sem),
    )(a, b, *extra, *deps)
    return outs if fuse else outs[0]


def mm_nn(name, a, b3, out_dtype=F32, out3=False, deps=(), fuse=None):
    m, k = a.shape
    g, _, n = b3.shape
    tm, tk = _tile(m, (1024, 512, 256)), _tile(k, (2048, 1024, 512, 256))
    tn = n if out3 else _tile(n, (1024, 512, 256, 128))
    nj = n // tn
    grid = (m // tm, g, nj, k // tk)
    a_spec = pl.BlockSpec((tm, tk), lambda i, gg, j, kk: (i, kk))
    b_spec = pl.BlockSpec((None, tk, tn), lambda i, gg, j, kk: (gg, kk, j))
    if out3:
        o_spec = pl.BlockSpec((None, tm, tn), lambda i, gg, j, kk: (gg, i, j))
        o_shape = (g, m, n)
    else:
        o_spec = pl.BlockSpec((tm, tn), lambda i, gg, j, kk: (i, gg * nj + j))
        o_shape = (m, g * n)
    return _mm_call(name, a, b3, NN, grid, a_spec, b_spec, o_spec, o_shape, (tm, tn), (3,), out_dtype, deps, fuse)


def mm_nt(name, a, b3, out_dtype=F32, deps=(), fuse=None, stacked=False):
    g, k, n = b3.shape
    a3 = a.ndim == 3 and not stacked
    m = a.shape[1] if a.ndim == 3 else a.shape[0]
    tm, tko = _tile(m, (1024, 512, 256)), _tile(k, (1024, 512, 256))
    tc = n if a3 else _tile(n, (2048, 1024, 512, 256, 128))
    nc = n // tc
    grid = (m // tm, k // tko, g, nc)
    if a3:
        a_spec = pl.BlockSpec((None, tm, tc), lambda i, j, gg, c: (gg, i, c))
    elif stacked:
        per = g // a.shape[0]
        a_spec = pl.BlockSpec((None, tm, tc), lambda i, j, gg, c: (gg // per, i, (gg % per) * nc + c))
    else:
        a_spec = pl.BlockSpec((tm, tc), lambda i, j, gg, c: (i, gg * nc + c))
    b_spec = pl.BlockSpec((None, tko, tc), lambda i, j, gg, c: (gg, j, c))
    o_spec = pl.BlockSpec((tm, tko), lambda i, j, gg, c: (i, j))
    return _mm_call(name, a, b3, NT, grid, a_spec, b_spec, o_spec, (m, k), (tm, tko), (2, 3), out_dtype, deps, fuse)


def mm_tn(name, a, b, g, out_dtype=BF16, deps=(), stacked=False):
    t, k = a.shape
    b3 = b.ndim == 3 and not stacked
    n = b.shape[2] if b3 else (b.shape[0] * b.shape[2] if stacked else b.shape[1]) // g
    tm = _tile(k, (1024, 512, 256))
    tn = n if b3 else _tile(n, (1024, 512, 256, 128))
    tt = _tile(t, (2048, 1024, 512, 256))
    nj = n // tn
    grid = (g, k // tm, nj, t // tt)
    a_spec = pl.BlockSpec((tt, tm), lambda gg, i, j, s: (s, i))
    if b3:
        b_spec = pl.BlockSpec((None, tt, tn), lambda gg, i, j, s: (gg, s, j))
    elif stacked:
        per = g // b.shape[0]
        b_spec = pl.BlockSpec((None, tt, tn), lambda gg, i, j, s: (gg // per, s, (gg % per) * nj + j))
    else:
        b_spec = pl.BlockSpec((tt, tn), lambda gg, i, j, s: (s, gg * nj + j))
    o_spec = pl.BlockSpec((None, tm, tn), lambda gg, i, j, s: (gg, i, j))
    return _mm_call(name, a, b, TN, grid, a_spec, b_spec, o_spec, (g, k, n), (tm, tn), (3,), out_dtype, deps)


def rowwise(name, fn, rows, vecs=()):
    t = rows[0].shape[0]
    wmax = max(r.shape[1] for r in rows)
    tr = _row_tile(t, ROW_TILE_BYTES // (4 * wmax))
    row_s = [jax.ShapeDtypeStruct((tr, r.shape[1]), r.dtype) for r in rows]
    vec_s = [jax.ShapeDtypeStruct(v.shape, v.dtype) for v in vecs]
    out_rows_s, out_sums_s = jax.eval_shape(fn, *row_s, *vec_s)
    n_in, n_r = len(rows) + len(vecs), len(out_rows_s)

    def body(*refs):
        i = pl.program_id(0)
        o_rows, o_sums = fn(*[r[...] for r in refs[:n_in]])
        for ref, val in zip(refs[n_in:n_in + n_r], o_rows):
            ref[...] = val

        if out_sums_s:
            @pl.when(i == 0)
            def _():
                for ref in refs[n_in + n_r:]:
                    ref[...] = jnp.zeros_like(ref)

            for ref, val in zip(refs[n_in + n_r:], o_sums):
                ref[...] += val

    in_specs = [pl.BlockSpec((tr, r.shape[1]), lambda i: (i, 0)) for r in rows]
    in_specs += [pl.BlockSpec(v.shape, lambda i: (0, 0)) for v in vecs]
    out_specs = [pl.BlockSpec((tr, s.shape[1]), lambda i: (i, 0)) for s in out_rows_s]
    out_specs += [pl.BlockSpec(s.shape, lambda i: (0, 0)) for s in out_sums_s]
    out_shape = [jax.ShapeDtypeStruct((t, s.shape[1]), s.dtype) for s in out_rows_s]
    out_shape += [jax.ShapeDtypeStruct(s.shape, s.dtype) for s in out_sums_s]
    outs = pl.pallas_call(
        body, name=name, grid=(t // tr,), in_specs=in_specs, out_specs=out_specs, out_shape=out_shape,
        compiler_params=_params(("arbitrary",)),
    )(*rows, *vecs)
    return outs[:n_r], outs[n_r:]


def _rms(x, gain):
    return x * lax.rsqrt(jnp.mean(x * x, axis=-1, keepdims=True) + NORM_EPS) * gain


def _norm_fwd(x, gain):
    return (_rms(x, gain).astype(BF16),), ()


def _add_norm_fwd(h, a, gain):
    h = h + a
    return (h, _rms(h, gain).astype(BF16)), ()


def _relu2(pre):
    r = jnp.maximum(pre, 0.0)
    return pre, r * r


def _ple(h, gpre, pu):
    return h + pu * jax.nn.sigmoid(gpre)


def _ple_two_norms_fwd(h, gpre, pu, gain_a, gain_b):
    h = _ple(h, gpre, pu)
    return (h, _rms(h, gain_a).astype(BF16), _rms(h, gain_b).astype(BF16)), ()


def _tail_fwd_bwd(h, gpre, pu, target, gain):
    def row_loss(h, gpre, pu, gain):
        y = _rms(_ple(h, gpre, pu), gain)
        return 0.5 * jnp.mean(jnp.square(y - target), axis=-1, keepdims=True)

    loss, vjp = jax.vjp(row_loss, h, gpre, pu, gain)
    dh, dgpre, dpu, dgain = vjp(jnp.ones_like(loss))
    loss = jnp.broadcast_to(jnp.sum(loss, axis=0, keepdims=True), (1, 128))
    return (dh, dgpre.astype(BF16), dpu.astype(BF16)), (dgain, loss)


def _ple_bwd(gpre, pu, dh):
    _, vjp = jax.vjp(lambda g, u: pu_times_gate(g, u), gpre, pu)
    dgpre, dpu = vjp(dh)
    return (dgpre.astype(BF16), dpu.astype(BF16)), ()


def pu_times_gate(gpre, pu):
    return pu * jax.nn.sigmoid(gpre)


def _norm_bwd(h, du, dh_in, gain):
    _, vjp = jax.vjp(_rms, h, gain)
    dh, dgain = vjp(du)
    dh = dh_in + dh
    return (dh, dh.astype(BF16)), (dgain,)


def _two_norms_bwd(h, du_a, du_b, dh_in, gain_a, gain_b):
    _, vjp = jax.vjp(lambda h, ga, gb: (_rms(h, ga), _rms(h, gb)), h, gain_a, gain_b)
    dh, dga, dgb = vjp((du_a, du_b))
    return (dh_in + dh,), (dga, dgb)


def _relu2_bwd(dact, pre):
    return (dact * 2.0 * jnp.maximum(pre.astype(F32), 0.0),)


def _bf16_dot(dims_fwd, dims_da, dims_db, swap_da, swap_db):
    @jax.custom_vjp
    def dot(a, b):
        return lax.dot_general(a.astype(BF16), b.astype(BF16), dims_fwd, preferred_element_type=F32)

    def fwd(a, b):
        return dot(a, b), (a, b)

    def bwd(res, ct):
        a, b = res
        ct, a, b = ct.astype(BF16), a.astype(BF16), b.astype(BF16)
        da = lax.dot_general(*((b, ct) if swap_da else (ct, b)), dims_da, preferred_element_type=F32)
        db = lax.dot_general(*((ct, a) if swap_db else (a, ct)), dims_db, preferred_element_type=F32)
        return da, db

    dot.defvjp(fwd, bwd)
    return dot


_dot_nn = _bf16_dot(NN, NT, TN, False, False)
_dot_nt = _bf16_dot(NT, NN, TN, False, True)
_dot_tn = _bf16_dot(TN, NT, NN, True, False)


def _chunk_masks(transposed):
    r = lax.broadcasted_iota(jnp.int32, (TILE, TILE), 0)
    c = lax.broadcasted_iota(jnp.int32, (TILE, TILE), 1)
    same = (r // CHUNK) == (c // CHUNK)
    causal = same & ((r <= c) if transposed else (c <= r))
    return causal, same


def _chunk_scan(x, reverse):
    pos = lax.broadcasted_iota(jnp.int32, x.shape, 0) % CHUNK
    step = 1
    while step < CHUNK:
        if reverse:
            x = x + jnp.where(pos < CHUNK - step, pltpu.roll(x, x.shape[0] - step, axis=0), 0.0)
        else:
            x = x + jnp.where(pos >= step, pltpu.roll(x, step, axis=0), 0.0)
        step *= 2
    return x


def _chunk_total(x):
    return _chunk_scan(x, False) + _chunk_scan(x, True) - x


@jax.custom_vjp
def _chunk_sums(x):
    return _chunk_scan(x, False), _chunk_total(x)


def _chunk_sums_fwd(x):
    return _chunk_sums(x), None


def _chunk_sums_bwd(_, ct):
    return (_chunk_scan(ct[0], True) + _chunk_total(ct[1]),)


_chunk_sums.defvjp(_chunk_sums_fwd, _chunk_sums_bwd)


def _hgrn_tile(q, f, i, g, lgt, hg, st):
    d = q.shape[1]
    l0, l1 = lgt[0:1], lgt[1:2]
    mx = jnp.maximum(l0, l1)
    e0, e1 = jnp.exp(l0 - mx), jnp.exp(l1 - mx)
    lb = e0 / (e0 + e1)
    fg = lb + (1.0 - lb) * jax.nn.sigmoid(f)
    k = 1.0 - fg
    causal, _ = _chunk_masks(False)
    b, b_last = _chunk_sums(jnp.log(fg))
    q_in = q * jax.nn.sigmoid(q) * (d ** -0.5) * jnp.exp(b)
    k_in = k * jnp.exp(-b)
    k_end = k * jnp.exp(b_last - b)
    att = jnp.where(causal, _dot_nt(q_in, k_in), 0.0)
    o_intra = _dot_nn(att, i)
    n_chunks = TILE // CHUNK
    chunk_of_row = lax.broadcasted_iota(jnp.int32, (TILE, 1), 0) // CHUNK

    def spread(a):
        return jnp.concatenate([jnp.where(chunk_of_row == n, a, 0.0) for n in range(n_chunks)], axis=1)

    increments = _dot_tn(i, spread(k_end))
    states = []
    for n in range(n_chunks):
        states.append(st)
        decay = jnp.exp(jnp.mean(b_last[n * CHUNK:(n + 1) * CHUNK], axis=0, keepdims=True))
        st = st * decay + increments[:, n * d:(n + 1) * d]
    o = o_intra + _dot_nt(spread(q_in), jnp.concatenate(states, axis=1))
    o = o * lax.rsqrt(jnp.mean(o * o, axis=-1, keepdims=True) + NORM_EPS) * hg
    return o * (g * jax.nn.sigmoid(g)), st


def hgrn_fwd(z, lgt, hg):
    t, d4 = z.shape
    d = d4 // 4
    nh, nt = d // HEAD_DIM, t // TILE
    hp = HEADS_PER_STEP
    wide = hp * HEAD_DIM

    def body(q_ref, f_ref, i_ref, g_ref, lgt_ref, hg_ref, o_ref, st_out_ref, st_ref):
        tt = pl.program_id(1)

        @pl.when(tt == 0)
        def _():
            st_ref[...] = jnp.zeros_like(st_ref)

        for hh in range(hp):
            cols = slice(hh * HEAD_DIM, (hh + 1) * HEAD_DIM)
            st = st_ref[hh]
            st_out_ref[hh] = st
            o, st = _hgrn_tile(q_ref[:, cols], f_ref[:, cols], i_ref[:, cols], g_ref[:, cols], lgt_ref[:, cols],
                               hg_ref[...], st)
            o_ref[:, cols] = o.astype(o_ref.dtype)
            st_ref[hh] = st

    def part(p):
        return pl.BlockSpec((TILE, wide), lambda h, tt: (tt, p * (nh // hp) + h))

    return pl.pallas_call(
        body, name="hgrn_fwd", grid=(nh // hp, nt),
        in_specs=[part(0), part(1), part(2), part(3),
                  pl.BlockSpec((2, wide), lambda h, tt: (0, h)),
                  pl.BlockSpec((1, HEAD_DIM), lambda h, tt: (0, 0))],
        out_specs=[pl.BlockSpec((TILE, wide), lambda h, tt: (tt, h)),
                   pl.BlockSpec((hp, None, HEAD_DIM, HEAD_DIM), lambda h, tt: (h, tt, 0, 0))],
        out_shape=[jax.ShapeDtypeStruct((t, d), BF16),
                   jax.ShapeDtypeStruct((nh, nt, HEAD_DIM, HEAD_DIM), F32)],
        scratch_shapes=[pltpu.VMEM((hp, HEAD_DIM, HEAD_DIM), F32)],
        compiler_params=_params(("parallel", "arbitrary")),
    )(z, z, z, z, lgt, hg)


def hgrn_bwd(z, lgt, hg, states, dout):
    t, d4 = z.shape
    d = d4 // 4
    nh, nt = d // HEAD_DIM, t // TILE
    hp = HEADS_PER_STEP
    wide = hp * HEAD_DIM

    def body(q_ref, f_ref, i_ref, g_ref, lgt_ref, hg_ref, st_in_ref, do_ref, dz_ref, dlgt_ref, dhg_ref, dst_ref):
        h, tt = pl.program_id(0), pl.program_id(1)

        @pl.when(tt == 0)
        def _():
            dst_ref[...] = jnp.zeros_like(dst_ref)
            dlgt_ref[...] = jnp.zeros_like(dlgt_ref)

        @pl.when((tt == 0) & (h == 0))
        def _():
            dhg_ref[...] = jnp.zeros_like(dhg_ref)

        for hh in range(hp):
            cols = slice(hh * HEAD_DIM, (hh + 1) * HEAD_DIM)
            _, vjp = jax.vjp(_hgrn_tile, q_ref[:, cols], f_ref[:, cols], i_ref[:, cols], g_ref[:, cols],
                             lgt_ref[:, cols], hg_ref[...], st_in_ref[hh])
            grads = vjp((do_ref[:, cols], dst_ref[hh]))
            for p in range(4):
                dz_ref[p, :, cols] = grads[p].astype(dz_ref.dtype)
            dlgt_ref[:, cols] += grads[4]
            dhg_ref[...] += grads[5]
            dst_ref[hh] = grads[6]

    def part(p):
        return pl.BlockSpec((TILE, wide), lambda h, tt: (nt - 1 - tt, p * (nh // hp) + h))

    return pl.pallas_call(
        body, name="hgrn_bwd", grid=(nh // hp, nt),
        in_specs=[part(0), part(1), part(2), part(3),
                  pl.BlockSpec((2, wide), lambda h, tt: (0, h)),
                  pl.BlockSpec((1, HEAD_DIM), lambda h, tt: (0, 0)),
                  pl.BlockSpec((hp, None, HEAD_DIM, HEAD_DIM), lambda h, tt: (h, nt - 1 - tt, 0, 0)),
                  pl.BlockSpec((TILE, wide), lambda h, tt: (nt - 1 - tt, h))],
        out_specs=[pl.BlockSpec((4, TILE, wide), lambda h, tt: (0, nt - 1 - tt, h)),
                   pl.BlockSpec((2, wide), lambda h, tt: (0, h)),
                   pl.BlockSpec((1, HEAD_DIM), lambda h, tt: (0, 0))],
        out_shape=[jax.ShapeDtypeStruct((4, t, d), BF16),
                   jax.ShapeDtypeStruct((2, d), F32),
                   jax.ShapeDtypeStruct((1, HEAD_DIM), F32)],
        scratch_shapes=[pltpu.VMEM((hp, HEAD_DIM, HEAD_DIM), F32)],
        compiler_params=_params(("arbitrary", "arbitrary")),
    )(z, z, z, z, lgt, hg, states, dout)


def _log_sigmoid(x):
    return jnp.minimum(x, 0.0) - jnp.log(1.0 + jnp.exp(-jnp.abs(x)))


def decay_fwd(fl_t, b_f):
    nh, t = fl_t.shape

    def body(fl_ref, b_ref, out_ref):
        r = lax.broadcasted_iota(jnp.int32, (128, 128), 0)
        c = lax.broadcasted_iota(jnp.int32, (128, 128), 1)
        upper = (r <= c).astype(F32)
        carry = jnp.zeros((nh, 1), F32)
        for j in range(t // 128):
            cols = slice(j * 128, (j + 1) * 128)
            ls = _log_sigmoid(fl_ref[:, cols] + b_ref[...])
            out_ref[:, cols] = carry + jnp.dot(ls, upper, precision=lax.Precision.HIGHEST,
                                               preferred_element_type=F32)
            carry = carry + jnp.sum(ls, axis=1, keepdims=True)

    return pl.pallas_call(body, name="decay_fwd", out_shape=jax.ShapeDtypeStruct((nh, t), F32),
                          compiler_params=_params(None))(fl_t, b_f)


def decay_bwd(fl_t, b_f, ddcum):
    nh, t = fl_t.shape

    def body(fl_ref, b_ref, dd_ref, dfl_ref, db_ref):
        r = lax.broadcasted_iota(jnp.int32, (128, 128), 0)
        c = lax.broadcasted_iota(jnp.int32, (128, 128), 1)
        lower = (r >= c).astype(F32)
        carry = jnp.zeros((nh, 1), F32)
        db = jnp.zeros((nh, 1), F32)
        for j in reversed(range(t // 128)):
            cols = slice(j * 128, (j + 1) * 128)
            dd = dd_ref[:, cols]
            dls = carry + jnp.dot(dd, lower, precision=lax.Precision.HIGHEST, preferred_element_type=F32)
            carry = carry + jnp.sum(dd, axis=1, keepdims=True)
            dfl = dls * jax.nn.sigmoid(-(fl_ref[:, cols] + b_ref[...]))
            dfl_ref[:, cols] = dfl
            db = db + jnp.sum(dfl, axis=1, keepdims=True)
        db_ref[...] = db

    return pl.pallas_call(body, name="decay_bwd",
                          out_shape=[jax.ShapeDtypeStruct((nh, t), F32), jax.ShapeDtypeStruct((nh, 1), F32)],
                          compiler_params=_params(None))(fl_t, b_f, ddcum)


def _attn_parts(t):
    tq = _tile(t, (256, 128))
    per_part = 2 if t // tq >= 4 else 1
    return tq, [(first, per_part, (first + per_part) * tq) for first in range(0, t // tq, per_part)]


def _attn_logits(q_ref, k_ref, dcol_ref, drow_ref, row0, tq, keys):
    qs = (q_ref[...] * (HEAD_DIM ** -0.5)).astype(BF16)
    s = lax.dot_general(qs, k_ref[...], NT, preferred_element_type=F32)
    s = s + dcol_ref[...] - drow_ref[...]
    row = row0 + lax.broadcasted_iota(jnp.int32, (tq, keys), 0)
    col = lax.broadcasted_iota(jnp.int32, (tq, keys), 1)
    return qs, jnp.where(col <= row, s, NEG_BIG)


def attn_fwd(q, k, v, dcol, drow):
    t, d = q.shape
    nh = d // HEAD_DIM
    tq, parts = _attn_parts(t)
    o = lse = None
    for first, count, keys in parts:
        def body(q_ref, k_ref, v_ref, dcol_ref, drow_ref, *rest, first=first, keys=keys):
            o_ref, lse_ref = rest[-2:]
            _, s = _attn_logits(q_ref, k_ref, dcol_ref, drow_ref, (first + pl.program_id(1)) * tq, tq, keys)
            m = jnp.max(s, axis=1, keepdims=True)
            p = jnp.exp(s - m)
            l = jnp.sum(p, axis=1, keepdims=True)
            acc = jnp.dot(p.astype(BF16), v_ref[...], preferred_element_type=F32)
            o_ref[...] = (acc / l).astype(o_ref.dtype)
            lse_ref[...] = m + jnp.log(l)

        tile = pl.BlockSpec((tq, HEAD_DIM), lambda h, i, first=first: (first + i, h))
        col = pl.BlockSpec((None, tq, 1), lambda h, i, first=first: (h, first + i, 0))
        seen = pl.BlockSpec((keys, HEAD_DIM), lambda h, i: (0, h))
        carried = [] if o is None else [o, lse]
        o, lse = pl.pallas_call(
            body, name=f"attn_fwd_{first}", grid=(nh, count),
            in_specs=[tile, seen, seen, col, pl.BlockSpec((None, 1, keys), lambda h, i: (h, 0, 0))]
            + [pl.BlockSpec(memory_space=pl.ANY)] * len(carried),
            out_specs=[tile, col],
            out_shape=[jax.ShapeDtypeStruct((t, d), BF16), jax.ShapeDtypeStruct((nh, t, 1), F32)],
            input_output_aliases={5: 0, 6: 1} if carried else {},
            compiler_params=_params(("parallel", "parallel")),
        )(q, k, v, dcol, drow, *carried)
    return o, lse


def attn_bwd(q, k, v, dcol, drow, lse, do):
    t, d = q.shape
    nh = d // HEAD_DIM
    tq, parts = _attn_parts(t)
    dq = dk = dv = ddrow = None
    for first, count, keys in reversed(parts):
        first_call = dq is None

        def body(q_ref, k_ref, v_ref, dcol_ref, drow_ref, lse_ref, do_ref, *rest, first=first, keys=keys,
                 count=count, first_call=first_call):
            dq_ref, dk_ref, dv_ref, ddrow_ref, dk_acc, dv_acc, ddrow_acc = rest[-7:]
            i = pl.program_id(1)

            @pl.when(i == 0)
            def _():
                if first_call:
                    dk_acc[...] = jnp.zeros_like(dk_acc)
                    dv_acc[...] = jnp.zeros_like(dv_acc)
                    ddrow_acc[...] = jnp.zeros_like(ddrow_acc)
                else:
                    dk_acc[...] = rest[1][...]
                    dv_acc[...] = rest[2][...]
                    ddrow_acc[...] = rest[3][...]

            qs, s = _attn_logits(q_ref, k_ref, dcol_ref, drow_ref, (first + i) * tq, tq, keys)
            p = jnp.exp(s - lse_ref[...])
            do = do_ref[...]
            dp = lax.dot_general(do, v_ref[...], NT, preferred_element_type=F32)
            ds = p * (dp - jnp.sum(p * dp, axis=1, keepdims=True))
            dsb = ds.astype(BF16)
            dq_ref[...] = (jnp.dot(dsb, k_ref[...], preferred_element_type=F32) * (HEAD_DIM ** -0.5)).astype(dq_ref.dtype)
            dk_acc[...] += lax.dot_general(dsb, qs, TN, preferred_element_type=F32)
            dv_acc[...] += lax.dot_general(p.astype(BF16), do, TN, preferred_element_type=F32)
            ddrow_acc[...] -= jnp.sum(ds, axis=0, keepdims=True)

            @pl.when(i == count - 1)
            def _():
                dk_ref[...] = dk_acc[...]
                dv_ref[...] = dv_acc[...]
                ddrow_ref[...] = ddrow_acc[...]

        tile = pl.BlockSpec((tq, HEAD_DIM), lambda h, i, first=first: (first + i, h))
        col = pl.BlockSpec((None, tq, 1), lambda h, i, first=first: (h, first + i, 0))
        seen = pl.BlockSpec((keys, HEAD_DIM), lambda h, i: (0, h))
        seen_row = pl.BlockSpec((None, 1, keys), lambda h, i: (h, 0, 0))
        carried = [] if first_call else [dq, dk, dv, ddrow]
        carried_specs = [] if first_call else [pl.BlockSpec(memory_space=pl.ANY), seen, seen, seen_row]
        dq, dk, dv, ddrow = pl.pallas_call(
            body, name=f"attn_bwd_{first}", grid=(nh, count),
            in_specs=[tile, seen, seen, col, seen_row, col, tile] + carried_specs,
            out_specs=[tile, seen, seen, seen_row],
            out_shape=[jax.ShapeDtypeStruct((t, d), BF16), jax.ShapeDtypeStruct((t, d), F32),
                       jax.ShapeDtypeStruct((t, d), F32), jax.ShapeDtypeStruct((nh, 1, t), F32)],
            scratch_shapes=[pltpu.VMEM((keys, HEAD_DIM), F32), pltpu.VMEM((keys, HEAD_DIM), F32),
                            pltpu.VMEM((1, keys), F32)],
            input_output_aliases={} if first_call else {7: 0, 8: 1, 9: 2, 10: 3},
            compiler_params=_params(("parallel", "arbitrary")),
        )(q, k, v, dcol, drow, lse, do, *carried)
    return dq, dk, dv, ddrow


def _my_index():
    return (lax.axis_index("x") * 2 + lax.axis_index("y")) * 2 + lax.axis_index("c")


def _exchange(name, src, gather, deps=()):
    shape = src.shape if gather else src.shape[1:]

    def body(src_ref, *rest):
        out_ref, send_sems, recv_sems, local_sem = rest[len(deps):]
        x, y, c = (lax.axis_index(a) for a in MESH_AXES)
        me = (x * 2 + y) * 2 + c
        mine = src_ref if gather else src_ref.at[me]
        local = pltpu.make_async_copy(mine, out_ref.at[me], local_sem)
        local.start()
        copies = []
        for dlt in range(1, N_DEV):
            dx, dy, dc = dlt // 4, (dlt // 2) % 2, dlt % 2
            px, py, pc = x ^ dx, y ^ dy, c ^ dc
            peer = (px * 2 + py) * 2 + pc
            copies.append(pltpu.make_async_remote_copy(
                src_ref=src_ref if gather else src_ref.at[peer], dst_ref=out_ref.at[me],
                send_sem=send_sems.at[dlt - 1], recv_sem=recv_sems.at[dlt - 1],
                device_id=(px, py, pc), device_id_type=pl.DeviceIdType.MESH))
        for cp in copies:
            cp.start()
        for cp in copies:
            cp.wait_recv()
        for cp in copies:
            cp.wait_send()
        local.wait()

    return pl.pallas_call(
        body, name=name, out_shape=jax.ShapeDtypeStruct((N_DEV,) + tuple(shape), src.dtype),
        in_specs=[pl.BlockSpec(memory_space=pl.ANY)] * (1 + len(deps)), out_specs=pl.BlockSpec(memory_space=pl.ANY),
        scratch_shapes=[pltpu.SemaphoreType.DMA((N_DEV - 1,)), pltpu.SemaphoreType.DMA((N_DEV - 1,)),
                        pltpu.SemaphoreType.DMA],
        compiler_params=pltpu.CompilerParams(has_side_effects=True),
    )(src, *deps)


def all_gather(name, x, deps=()):
    return _exchange(name, x, True, deps)


_HBM = pl.BlockSpec(memory_space=pltpu.HBM)
_SEM = pl.BlockSpec(memory_space=pltpu.SEMAPHORE)
_DATAFLOW = pltpu.SideEffectType.DATAFLOW_SIDE_EFFECTING


def _peer_copies(src_ref, land_ref, send_sems, recv_sems, gather):
    x, y, c = (lax.axis_index(a) for a in MESH_AXES)
    me = (x * 2 + y) * 2 + c
    copies = []
    for dlt in range(1, N_DEV):
        px, py, pc = x ^ (dlt // 4), y ^ ((dlt // 2) % 2), c ^ (dlt % 2)
        peer = (px * 2 + py) * 2 + pc
        copies.append(pltpu.make_async_remote_copy(
            src_ref=src_ref if gather else src_ref.at[peer], dst_ref=land_ref.at[me],
            send_sem=send_sems.at[dlt - 1], recv_sem=recv_sems.at[dlt - 1],
            device_id=(px, py, pc), device_id_type=pl.DeviceIdType.MESH))
    return copies


def exchange_start(name, srcs, gather):
    n = len(srcs)
    lands = [lax.empty((N_DEV,) + tuple(s.shape if gather else s.shape[1:]), s.dtype) for s in srcs]

    def body(*refs):
        src_refs, land_refs = refs[:n], refs[n:2 * n]
        send_sems, recv_sems = refs[2 * n:3 * n], refs[3 * n:4 * n]
        token = refs[-1]
        for j in range(n):
            for cp in _peer_copies(src_refs[j], land_refs[j], send_sems[j], recv_sems[j], gather):
                cp.start()
        token[...] = jnp.zeros_like(token)

    sems = [pltpu.SemaphoreType.DMA((N_DEV - 1,))] * (2 * n)
    thru = [pltpu.HBM(a.shape, a.dtype) for a in list(srcs) + lands]
    outs = pl.pallas_call(
        body, name=name, out_shape=tuple(sems + thru + [jax.ShapeDtypeStruct((8, 128), F32)]),
        in_specs=[_HBM] * (2 * n), out_specs=tuple([_SEM] * (2 * n) + [_HBM] * (2 * n) + [pl.BlockSpec(memory_space=pltpu.VMEM)]),
        input_output_aliases={j: 2 * n + j for j in range(2 * n)},
        compiler_params=pltpu.CompilerParams(has_side_effects=_DATAFLOW),
    )(*[pltpu.with_memory_space_constraint(a, pltpu.HBM) for a in list(srcs) + lands])
    handles = [(outs[j], outs[n + j], outs[2 * n + j], outs[3 * n + j]) for j in range(n)]
    return handles, outs[-1]


def exchange_wait(name, handle, after, gather):
    send_sems, recv_sems, src, land = handle

    def body(src_ref, land_ref, send_ref, recv_ref, after_ref, src_out, land_out):
        for cp in _peer_copies(src_ref, land_ref, send_ref, recv_ref, gather):
            cp.wait_send()
            cp.wait_recv()

    return pl.pallas_call(
        body, name=name, out_shape=(pltpu.HBM(src.shape, src.dtype), pltpu.HBM(land.shape, land.dtype)),
        in_specs=[_HBM, _HBM, _SEM, _SEM, pl.BlockSpec(memory_space=pl.ANY)], out_specs=(_HBM, _HBM),
        input_output_aliases={0: 0, 1: 1},
        compiler_params=pltpu.CompilerParams(has_side_effects=_DATAFLOW),
    )(src, land, send_sems, recv_sems, after)


N_OTHER_CHIPS = 3


def _two_level_places():
    x, y, c = (lax.axis_index(a) for a in MESH_AXES)
    return (x, y, c), (x * 2 + y) * 2 + c, (x, y, 1 - c), [(1 - x, y), (x, 1 - y), (1 - x, 1 - y)]


def _first_copies(land_ref, send_sems, recv_sems):
    (x, y, c), me, other_core, chips = _two_level_places()
    targets = [other_core] + [(cx, cy, c) for cx, cy in chips]
    return [pltpu.make_async_remote_copy(
        src_ref=land_ref.at[me], dst_ref=land_ref.at[me], send_sem=send_sems.at[k], recv_sem=recv_sems.at[k],
        device_id=to, device_id_type=pl.DeviceIdType.MESH) for k, to in enumerate(targets)]


def _passed_on_copies(land_ref, send_sems, recv_sems):
    (x, y, c), me, other_core, chips = _two_level_places()
    copies = []
    for k, (cx, cy) in enumerate(chips):
        slot = land_ref.at[(cx * 2 + cy) * 2 + c]
        copies.append(pltpu.make_async_remote_copy(
            src_ref=slot, dst_ref=slot, send_sem=send_sems.at[k], recv_sem=recv_sems.at[k],
            device_id=other_core, device_id_type=pl.DeviceIdType.MESH))
    return copies


def gather_start(name, lands):
    n = len(lands)

    def body(*refs):
        land_refs, send_sems, recv_sems = refs[:n], refs[n:2 * n], refs[2 * n:3 * n]
        for j in range(n):
            for cp in _first_copies(land_refs[j], send_sems[j], recv_sems[j]):
                cp.start()

    sems = [pltpu.SemaphoreType.DMA((1 + N_OTHER_CHIPS,))] * (2 * n)
    outs = pl.pallas_call(
        body, name=name, out_shape=tuple(sems + [pltpu.HBM(a.shape, a.dtype) for a in lands]),
        in_specs=[_HBM] * n, out_specs=tuple([_SEM] * (2 * n) + [_HBM] * n),
        input_output_aliases={j: 2 * n + j for j in range(n)},
        compiler_params=pltpu.CompilerParams(has_side_effects=_DATAFLOW),
    )(*[pltpu.with_memory_space_constraint(a, pltpu.HBM) for a in lands])
    return [[outs[j], outs[n + j], outs[2 * n + j]] for j in range(n)]


def gather_pass_on(name, handle, after):
    send_sems, recv_sems, land = handle

    def body(land_ref, recv_ref, after_ref, land_out, send2, recv2, token):
        arrivals = _first_copies(land_ref, recv_ref, recv_ref)
        for k, cp in enumerate(_passed_on_copies(land_ref, send2, recv2)):
            arrivals[1 + k].wait_recv()
            cp.start()
        token[...] = jnp.zeros_like(token)

    sem3 = pltpu.SemaphoreType.DMA((N_OTHER_CHIPS,))
    land, send2, recv2, token = pl.pallas_call(
        body, name=name,
        out_shape=(pltpu.HBM(land.shape, land.dtype), sem3, sem3, jax.ShapeDtypeStruct((8, 128), F32)),
        in_specs=[_HBM, _SEM, pl.BlockSpec(memory_space=pl.ANY)],
        out_specs=(_HBM, _SEM, _SEM, pl.BlockSpec(memory_space=pltpu.VMEM)),
        input_output_aliases={0: 0}, compiler_params=pltpu.CompilerParams(has_side_effects=_DATAFLOW),
    )(land, recv_sems, after)
    return [send_sems, recv_sems, land, send2, recv2], token


def gather_wait(name, handle, after):
    send_sems, recv_sems, land, send2, recv2 = handle

    def body(land_ref, send_ref, recv_ref, send2_ref, recv2_ref, after_ref, land_out):
        first = _first_copies(land_ref, send_ref, recv_ref)
        for cp in first:
            cp.wait_send()
        first[0].wait_recv()
        for cp in _passed_on_copies(land_ref, send2_ref, recv2_ref):
            cp.wait_send()
            cp.wait_recv()

    return pl.pallas_call(
        body, name=name, out_shape=pltpu.HBM(land.shape, land.dtype),
        in_specs=[_HBM, _SEM, _SEM, _SEM, _SEM, pl.BlockSpec(memory_space=pl.ANY)], out_specs=_HBM,
        input_output_aliases={0: 0}, compiler_params=pltpu.CompilerParams(has_side_effects=_DATAFLOW),
    )(land, send_sems, recv_sems, send2, recv2, after)


N_CHIPS = 4


def _pair_copies(g_ref, half_ref, send_sems, recv_sems):
    (x, y, c), me, other_core, chips = _two_level_places()
    return [pltpu.make_async_remote_copy(
        src_ref=g_ref.at[chip * 2 + (1 - c)], dst_ref=half_ref.at[chip], send_sem=send_sems.at[chip],
        recv_sem=recv_sems.at[chip], device_id=other_core, device_id_type=pl.DeviceIdType.MESH)
        for chip in range(N_CHIPS)]


def pair_start(name, g):
    half = lax.empty((N_CHIPS,) + g.shape[1:], g.dtype)

    def body(g_ref, half_ref, send_sems, recv_sems, g_out, half_out, token):
        for cp in _pair_copies(g_ref, half_ref, send_sems, recv_sems):
            cp.start()
        token[...] = jnp.zeros_like(token)

    sem = pltpu.SemaphoreType.DMA((N_CHIPS,))
    outs = pl.pallas_call(
        body, name=name,
        out_shape=(sem, sem, pltpu.HBM(g.shape, g.dtype), pltpu.HBM(half.shape, half.dtype),
                   jax.ShapeDtypeStruct((8, 128), F32)),
        in_specs=[_HBM, _HBM], out_specs=(_SEM, _SEM, _HBM, _HBM, pl.BlockSpec(memory_space=pltpu.VMEM)),
        input_output_aliases={0: 2, 1: 3}, compiler_params=pltpu.CompilerParams(has_side_effects=_DATAFLOW),
    )(pltpu.with_memory_space_constraint(g, pltpu.HBM), half)
    return list(outs[:4]), outs[4]


def pair_wait(name, handle, after):
    send_sems, recv_sems, g, half = handle

    def body(g_ref, half_ref, send_ref, recv_ref, after_ref, g_out, half_out):
        for cp in _pair_copies(g_ref, half_ref, send_ref, recv_ref):
            cp.wait_send()
            cp.wait_recv()

    return pl.pallas_call(
        body, name=name, out_shape=(pltpu.HBM(g.shape, g.dtype), pltpu.HBM(half.shape, half.dtype)),
        in_specs=[_HBM, _HBM, _SEM, _SEM, pl.BlockSpec(memory_space=pl.ANY)], out_specs=(_HBM, _HBM),
        input_output_aliases={0: 0, 1: 1}, compiler_params=pltpu.CompilerParams(has_side_effects=_DATAFLOW),
    )(g, half, send_sems, recv_sems, after)


def pair_sum(name, g, half):
    _, r, wd = g.shape
    tr = _row_tile(r, 4 * ROW_TILE_BYTES // (4 * wd))
    kind = lax.axis_index("c").astype(jnp.int32).reshape(1)

    def body(kind_ref, g_ref, half_ref, o_ref):
        o_ref[...] = (g_ref[...].astype(F32) + half_ref[...].astype(F32)).astype(o_ref.dtype)

    spec = pl.BlockSpec((None, tr, wd), lambda chip, i, kind_ref: (chip, i, 0))
    return pl.pallas_call(
        body, name=name,
        grid_spec=pltpu.PrefetchScalarGridSpec(
            num_scalar_prefetch=1, grid=(N_CHIPS, r // tr),
            in_specs=[pl.BlockSpec((None, tr, wd), lambda chip, i, kind_ref: (chip * 2 + kind_ref[0], i, 0)), spec],
            out_specs=spec),
        out_shape=jax.ShapeDtypeStruct((N_CHIPS, r, wd), g.dtype),
        compiler_params=_params(("parallel", "parallel")),
    )(kind, g, half)


def _chip_copies(sums_ref, land_ref, send_sems, recv_sems):
    (x, y, c), me, other_core, chips = _two_level_places()
    return [pltpu.make_async_remote_copy(
        src_ref=sums_ref.at[cx * 2 + cy], dst_ref=land_ref.at[x * 2 + y], send_sem=send_sems.at[k],
        recv_sem=recv_sems.at[k], device_id=(cx, cy, c), device_id_type=pl.DeviceIdType.MESH)
        for k, (cx, cy) in enumerate(chips)]


def chip_start(name, sums):
    land = lax.empty(sums.shape, sums.dtype)

    def body(sums_ref, land_ref, send_sems, recv_sems, sums_out, land_out, token):
        for cp in _chip_copies(sums_ref, land_ref, send_sems, recv_sems):
            cp.start()
        token[...] = jnp.zeros_like(token)

    sem = pltpu.SemaphoreType.DMA((N_OTHER_CHIPS,))
    outs = pl.pallas_call(
        body, name=name,
        out_shape=(sem, sem, pltpu.HBM(sums.shape, sums.dtype), pltpu.HBM(land.shape, land.dtype),
                   jax.ShapeDtypeStruct((8, 128), F32)),
        in_specs=[_HBM, _HBM], out_specs=(_SEM, _SEM, _HBM, _HBM, pl.BlockSpec(memory_space=pltpu.VMEM)),
        input_output_aliases={0: 2, 1: 3}, compiler_params=pltpu.CompilerParams(has_side_effects=_DATAFLOW),
    )(pltpu.with_memory_space_constraint(sums, pltpu.HBM), land)
    return list(outs[:4]), outs[4]


def chip_wait(name, handle, after):
    send_sems, recv_sems, sums, land = handle

    def body(sums_ref, land_ref, send_ref, recv_ref, after_ref, sums_out, land_out):
        for cp in _chip_copies(sums_ref, land_ref, send_ref, recv_ref):
            cp.wait_send()
            cp.wait_recv()

    return pl.pallas_call(
        body, name=name, out_shape=(pltpu.HBM(sums.shape, sums.dtype), pltpu.HBM(land.shape, land.dtype)),
        in_specs=[_HBM, _HBM, _SEM, _SEM, pl.BlockSpec(memory_space=pl.ANY)], out_specs=(_HBM, _HBM),
        input_output_aliases={0: 0, 1: 1}, compiler_params=pltpu.CompilerParams(has_side_effects=_DATAFLOW),
    )(sums, land, send_sems, recv_sems, after)


def adamw_reduce(name, parts, w, m, v):
    nl, r, wd = w.shape
    tr = _row_tile(r, 2 * ROW_TILE_BYTES // (8 * wd))

    def body(*refs):
        p_refs = refs[:nl]
        w_ref, m_ref, v_ref, g_ref, d_ref, nm_ref, nv_ref = refs[nl:]
        layer = pl.program_id(0)
        for j in range(nl):
            @pl.when(layer == j)
            def _(j=j):
                g = p_refs[j][0].astype(F32)
                for sender in range(1, p_refs[j].shape[0]):
                    g = g + p_refs[j][sender].astype(F32)
                nm = B1 * m_ref[...] + (1.0 - B1) * g
                nv = B2 * v_ref[...] + (1.0 - B2) * jnp.square(g)
                m_hat = nm / (1.0 - B1 ** STEP)
                v_hat = nv / (1.0 - B2 ** STEP)
                g_ref[...] = g
                d_ref[...] = -LR * (m_hat / (jnp.sqrt(v_hat) + ADAM_EPS) + WD * w_ref[...])
                nm_ref[...] = nm
                nv_ref[...] = nv

    def part_spec(j):
        return pl.BlockSpec((parts[j].shape[0], tr, wd), lambda l, i: (0, jnp.where(l == j, i, 0), 0))

    spec = pl.BlockSpec((None, tr, wd), lambda l, i: (l, i, 0))
    return pl.pallas_call(
        body, name=name, grid=(nl, r // tr),
        in_specs=[part_spec(j) for j in range(nl)] + [spec, spec, spec],
        out_specs=[spec] * 4, out_shape=[jax.ShapeDtypeStruct((nl, r, wd), F32)] * 4,
        compiler_params=_params(("arbitrary", "arbitrary")),
    )(*parts, w, m, v)


def _pack_rows(vectors, rows=None):
    flat = jnp.concatenate([a.reshape(-1).astype(F32) for a in vectors])
    n = flat.shape[0]
    if rows is None:
        rows = -(-n // 1024) * 8
    return jnp.pad(flat, (0, rows * 128 - n)).reshape(rows, 128)


def _unpack_rows(packed, like):
    flat = packed.reshape(-1)
    out, pos = [], 0
    for a in like:
        out.append(flat[pos:pos + a.size].reshape(a.shape))
        pos += a.size
    return out


def kernel(x, p, mix_norm, mlp_norm, ple_norm, w_a_in, a_lb_logits, a_head_gain, w_a_out, kv_norm, w_kvf, b_f, w_b_q, w_b_out, w_mlp_up, w_mlp_down, w_ple_gate, w_ple_up, final_norm, loss_target, m_mix_norm, m_mlp_norm, m_ple_norm, m_w_a_in, m_a_lb_logits, m_a_head_gain, m_w_a_out, m_kv_norm, m_w_kvf, m_b_f, m_w_b_q, m_w_b_out, m_w_mlp_up, m_w_mlp_down, m_w_ple_gate, m_w_ple_up, m_final_norm, v_mix_norm, v_mlp_norm, v_ple_norm, v_w_a_in, v_a_lb_logits, v_a_head_gain, v_w_a_out, v_kv_norm, v_w_kvf, v_b_f, v_w_b_q, v_w_b_out, v_w_mlp_up, v_w_mlp_down, v_w_ple_gate, v_w_ple_up, v_final_norm):
    t, d = x.shape[1], x.shape[2]
    nh = d // HEAD_DIM
    n_layers = 2
    x2 = x.reshape(t, d)
    target = loss_target.reshape(t, d)
    me = _my_index()

    shards = {"w_a_in": w_a_in[0], "w_a_out": w_a_out[0], "w_kvf": w_kvf, "w_b_q": w_b_q[0], "w_b_out": w_b_out[0]}
    for l in range(n_layers):
        shards.update({f"w_mlp_up{l}": w_mlp_up[l], f"w_mlp_down{l}": w_mlp_down[l],
                       f"w_ple_gate{l}": w_ple_gate[l], f"w_ple_up{l}": w_ple_up[l]})
    first_use = ["a_lb_logits", "w_a_in", "w_a_out", "w_mlp_up0", "w_mlp_down0", "w_ple_gate0", "w_ple_up0", "w_kvf",
                 "w_b_q", "w_b_out", "w_mlp_up1", "w_mlp_down1", "w_ple_gate1", "w_ple_up1"]
    row_sharded = ("w_a_out", "w_b_q", "w_b_out", "w_mlp_down", "w_ple_gate")
    shards_bf = [a_lb_logits] + [shards[n].astype(BF16) for n in first_use[1:]]
    ag_handles = gather_start("ag_start", [
        lax.dynamic_update_slice(lax.empty((N_DEV,) + a.shape, a.dtype), a[None], (me, 0, 0)) for a in shards_bf])
    passed_on = {}
    weights = {}

    def pass_on(j, after):
        if j < len(first_use) and j not in passed_on:
            passed_on[j] = gather_pass_on("ag_pass_" + first_use[j], ag_handles[j], after)

    def weight(name, after=None):
        if name not in weights:
            j = first_use.index(name)
            pass_on(j, after)
            pass_on(j + 1, after)
            behind = passed_on[j + 1][1] if j + 1 in passed_on else after
            g = gather_wait("ag_wait_" + name, passed_on[j][0], behind)
            if name.rstrip("01") in row_sharded:
                g = g.reshape(1, g.shape[0] * g.shape[1], g.shape[2])
            weights[name] = g
        return weights[name]

    lgt = weight("a_lb_logits", x2).transpose(1, 0, 2).reshape(2, d)
    p_bf = [p[l, 0].astype(BF16) for l in range(n_layers)]

    def row(vec):
        return vec.reshape(1, -1)

    def mlp_ple_fwd(l, h_in, a):
        (h_a, u_mlp), _ = rowwise(f"add_norm_mlp{l}", _add_norm_fwd, [h_in, a], [row(mlp_norm[l])])
        pre, act = mm_nn(f"mlp_up{l}", u_mlp, weight(f"w_mlp_up{l}", u_mlp), fuse=(_relu2, (), (BF16, BF16)))
        mo = mm_nn(f"mlp_down{l}", act, weight(f"w_mlp_down{l}", act))
        (h_b, u_ple), _ = rowwise(f"add_norm_ple{l}", _add_norm_fwd, [h_a, mo], [row(ple_norm[l])])
        gpre = mm_nn(f"ple_gate{l}", u_ple, weight(f"w_ple_gate{l}", u_ple))
        pu = mm_nn(f"ple_up{l}", p_bf[l], weight(f"w_ple_up{l}", gpre))
        return dict(h_a=h_a, u_mlp=u_mlp, pre=pre, act=act, h_b=h_b, u_ple=u_ple, gpre=gpre, pu=pu)

    (u0,), _ = rowwise("norm_mix0", _norm_fwd, [x2], [row(mix_norm[0])])
    z = mm_nn("a_in", u0, weight("w_a_in", u0))
    og, states = hgrn_fwd(z, lgt, a_head_gain)
    a0 = mm_nn("a_out", og, weight("w_a_out", og))
    s0 = mlp_ple_fwd(0, x2, a0)
    (h3, u_kv, u1), _ = rowwise("ple_norms", _ple_two_norms_fwd, [s0["h_b"], s0["gpre"], s0["pu"]],
                                [row(kv_norm), row(mix_norm[1])])
    hk = mm_nn("kvf", u_kv, weight("w_kvf", u_kv), out3=True)
    hk = hk.transpose(1, 0, 2).reshape(t, -1)
    k_bf, v_bf = hk[:, :d].astype(BF16), hk[:, d:2 * d].astype(BF16)
    fl_t = hk[:, 2 * d:].T
    b_f_col = b_f.reshape(nh, 1)
    dcum = decay_fwd(fl_t, b_f_col)
    dcol, drow = dcum.reshape(nh, t, 1), dcum.reshape(nh, 1, t)
    q = mm_nn("b_q", u1, weight("w_b_q", dcum))
    o, lse = attn_fwd(q, k_bf, v_bf, dcol, drow)
    a1 = mm_nn("b_out", o, weight("w_b_out", o))
    s1 = mlp_ple_fwd(1, h3, a1)

    (dh, dgpre, dpu), (d_final, loss_rows) = rowwise(
        "tail", _tail_fwd_bwd, [s1["h_b"], s1["gpre"], s1["pu"], target], [row(final_norm)])

    sent = {}
    tokens = []

    two_level = ("w_mlp_up0", "w_mlp_up1", "w_mlp_down0", "w_mlp_down1", "w_a_in", "w_kvf")
    swapping = []

    def send_grad(name, g):
        g = g.reshape(N_DEV, -1, g.shape[-1])
        if name in two_level:
            sent[name], token = pair_start("rs_pair_" + name, g)
            swapping.append(name)
        else:
            (sent[name],), token = exchange_start("rs_start_" + name, [g], False)
        tokens.append(token)

    def second_stage(after):
        for name in swapping:
            g, half = pair_wait("rs_pairwait_" + name, sent[name], after)
            sent[name], token = chip_start("rs_chip_" + name, pair_sum("rs_sum_" + name, g, half))
            tokens.append(token)
        swapping.clear()

    def after_sends():
        deps = tuple(tokens)
        tokens.clear()
        return deps

    def mlp_ple_bwd(l, s, dh, dgpre, dpu):
        send_grad(f"w_ple_gate{l}", mm_tn(f"d_ple_gate_w{l}", s["u_ple"], dgpre, 1, deps=after_sends()))
        send_grad(f"w_ple_up{l}", mm_tn(f"d_ple_up_w{l}", p_bf[l], dpu, N_DEV, deps=after_sends()))
        du = mm_nt(f"d_ple_gate_x{l}", dgpre, weight(f"w_ple_gate{l}"), deps=after_sends())
        (dh, dh_bf), (d_ple,) = rowwise(f"d_norm_ple{l}", _norm_bwd, [s["h_b"], du, dh], [row(ple_norm[l])])
        send_grad(f"w_mlp_down{l}", mm_tn(f"d_mlp_down_w{l}", s["act"], dh_bf, 1))
        (dpre,) = mm_nt(f"d_mlp_down_x{l}", dh_bf, weight(f"w_mlp_down{l}"), deps=after_sends(),
                        fuse=(_relu2_bwd, (s["pre"],), (BF16,)))
        second_stage(dpre)
        send_grad(f"w_mlp_up{l}", mm_tn(f"d_mlp_up_w{l}", s["u_mlp"], dpre, N_DEV))
        du = mm_nt(f"d_mlp_up_x{l}", dpre, weight(f"w_mlp_up{l}"), deps=after_sends())
        second_stage(du)
        (dh, dh_bf), (d_mlp,) = rowwise(f"d_norm_mlp{l}", _norm_bwd, [s["h_a"], du, dh], [row(mlp_norm[l])])
        return dh, dh_bf, d_ple, d_mlp

    dh, dh_bf, d_ple1, d_mlp1 = mlp_ple_bwd(1, s1, dh, dgpre, dpu)
    send_grad("w_b_out", mm_tn("d_b_out_w", o, dh_bf, 1))
    do = mm_nt("d_b_out_x", dh_bf, weight("w_b_out"), out_dtype=BF16, deps=after_sends())
    dq, dk, dv, ddrow = attn_bwd(q, k_bf, v_bf, dcol, drow, lse, do)
    send_grad("w_b_q", mm_tn("d_b_q_w", u1, dq, 1))
    du1 = mm_nt("d_b_q_x", dq, weight("w_b_q"), deps=after_sends())
    dfl_t, d_b_f = decay_bwd(fl_t, b_f_col, ddrow.reshape(nh, t))
    dhk = jnp.concatenate([dk.astype(BF16), dv.astype(BF16), dfl_t.T.astype(BF16)], axis=1)
    dhk = dhk.reshape(t, N_DEV, -1).transpose(1, 0, 2)
    send_grad("w_kvf", mm_tn("d_kvf_w", u_kv, dhk, N_DEV))
    du_kv = mm_nt("d_kvf_x", dhk, weight("w_kvf"), deps=after_sends())
    second_stage(du_kv)
    (dh,), (d_kv_norm, d_mix1) = rowwise("d_ple_norms", _two_norms_bwd, [h3, du_kv, du1, dh],
                                         [row(kv_norm), row(mix_norm[1])])
    (dgpre, dpu), _ = rowwise("d_ple0", _ple_bwd, [s0["gpre"], s0["pu"], dh])
    dh, dh_bf, d_ple0, d_mlp0 = mlp_ple_bwd(0, s0, dh, dgpre, dpu)
    send_grad("w_a_out", mm_tn("d_a_out_w", og, dh_bf, 1))
    dog = mm_nt("d_a_out_x", dh_bf, weight("w_a_out"), deps=after_sends())
    dz4, d_lgt, d_hg = hgrn_bwd(z, lgt, a_head_gain, states, dog)
    send_grad("w_a_in", mm_tn("d_a_in_w", u0, dz4, N_DEV, stacked=True))
    du0 = mm_nt("d_a_in_x", dz4, weight("w_a_in"), deps=after_sends(), stacked=True)
    second_stage(du0)
    (dx, _), (d_mix0,) = rowwise("d_norm_mix0", _norm_bwd, [x2, du0, dh], [row(mix_norm[0])])

    new = {}
    last = [dx]

    def update(name, parts, w, m, v):
        shp = w.shape
        w3, m3, v3 = (a.reshape(len(parts), -1, shp[-1]) for a in (w, m, v))
        new[name] = tuple(a.reshape(shp) for a in adamw_reduce("adamw_" + name, parts, w3, m3, v3))
        last[0] = new[name][0]

    def receive_update(name, layers, w, m, v):
        parts = {}
        for sfx in layers:
            if name + sfx in two_level:
                mine, land = chip_wait(f"rs_wait_{name}{sfx}", sent[name + sfx], last[0])
                slot = me // 2
            else:
                mine, land = exchange_wait(f"rs_wait_{name}{sfx}", sent[name + sfx], last[0], False)
                slot = me
            parts[sfx] = lax.dynamic_update_slice(land, lax.dynamic_slice_in_dim(mine, slot, 1, 0), (slot, 0, 0))
        update(name, [parts[sfx] for sfx in sorted(layers)], w, m, v)

    both = ("1", "0")
    receive_update("w_b_out", ("",), w_b_out, m_w_b_out, v_w_b_out)
    receive_update("w_b_q", ("",), w_b_q, m_w_b_q, v_w_b_q)
    receive_update("w_kvf", ("",), w_kvf, m_w_kvf, v_w_kvf)
    receive_update("w_ple_gate", both, w_ple_gate, m_w_ple_gate, v_w_ple_gate)
    receive_update("w_ple_up", both, w_ple_up, m_w_ple_up, v_w_ple_up)
    receive_update("w_mlp_down", both, w_mlp_down, m_w_mlp_down, v_w_mlp_down)
    receive_update("w_mlp_up", both, w_mlp_up, m_w_mlp_up, v_w_mlp_up)
    receive_update("w_a_out", ("",), w_a_out, m_w_a_out, v_w_a_out)
    receive_update("w_a_in", ("",), w_a_in, m_w_a_in, v_w_a_in)

    small = dict(mix_norm=jnp.concatenate([d_mix0, d_mix1]), mlp_norm=jnp.concatenate([d_mlp0, d_mlp1]),
                 ple_norm=jnp.concatenate([d_ple0, d_ple1]), a_head_gain=d_hg, kv_norm=d_kv_norm.reshape(d),
                 b_f=d_b_f.reshape(nh), final_norm=d_final.reshape(d))
    small_w = dict(mix_norm=(mix_norm, m_mix_norm, v_mix_norm), mlp_norm=(mlp_norm, m_mlp_norm, v_mlp_norm),
                   ple_norm=(ple_norm, m_ple_norm, v_ple_norm),
                   a_head_gain=(a_head_gain, m_a_head_gain, v_a_head_gain), kv_norm=(kv_norm, m_kv_norm, v_kv_norm),
                   b_f=(b_f, m_b_f, v_b_f), final_norm=(final_norm, m_final_norm, v_final_norm))
    names = list(small)
    packed = _pack_rows([d_lgt] + [small[n] for n in names])
    everyone = all_gather("ag_small_grads", packed, deps=(last[0],))
    n_lgt_rows = d_lgt.size // 128
    lgt_parts = everyone[:, :n_lgt_rows].reshape(N_DEV, 2, d)
    lgt_parts = lax.dynamic_slice_in_dim(lgt_parts, me * a_lb_logits.shape[1], a_lb_logits.shape[1], axis=2)
    update("a_lb_logits", [lgt_parts], a_lb_logits, m_a_lb_logits, v_a_lb_logits)
    rest = everyone[:, n_lgt_rows:]
    like = [small_w[n][0] for n in names]
    packed_w, packed_m, packed_v = (_pack_rows([small_w[n][j] for n in names], rest.shape[1])[None] for j in range(3))
    outs = adamw_reduce("adamw_small", [rest], packed_w, packed_m, packed_v)
    unpacked = [_unpack_rows(a, like) for a in outs]
    for j, n in enumerate(names):
        new[n] = tuple(unpacked[q][j] for q in range(4))

    order = ["mix_norm", "mlp_norm", "ple_norm", "w_a_in", "a_lb_logits", "a_head_gain", "w_a_out", "kv_norm",
             "w_kvf", "b_f", "w_b_q", "w_b_out", "w_mlp_up", "w_mlp_down", "w_ple_gate", "w_ple_up", "final_norm"]
    loss_here, _ = lax.optimization_barrier((loss_rows[0, 0], new["final_norm"][0]))
    loss = lax.psum(loss_here, MESH_AXES)
    result = [loss, dx.reshape(x.shape)]
    for j in range(4):
        result += [new[n][j] for n in order]
    return tuple(result)
```

```python
import functools

import jax
import jax.numpy as jnp
from jax import lax
from jax.experimental import pallas as pl
from jax.experimental.pallas import tpu as pltpu

F32 = jnp.float32
BF16 = jnp.bfloat16
HEAD_DIM = 128
CHUNK = 16
TILE = 128
HEADS_PER_STEP = 4
NORM_EPS = 1e-6
N_DEV = 8
MESH_AXES = ("x", "y", "c")
VMEM_LIMIT_BYTES = 48 * 1024 * 1024
ROW_TILE_BYTES = 1024 * 1024
LR, B1, B2, ADAM_EPS, WD, STEP = 0.001, 0.9, 0.999, 1e-08, 0.01, 10
NEG_BIG = -1e30

NN = (((1,), (0,)), ((), ()))
NT = (((1,), (1,)), ((), ()))
TN = (((0,), (0,)), ((), ()))


def _params(semantics):
    return pltpu.CompilerParams(dimension_semantics=semantics, vmem_limit_bytes=VMEM_LIMIT_BYTES)


def _tile(n, prefs):
    for p in prefs:
        if n % p == 0:
            return p
    return n


def _row_tile(rows, limit):
    for cand in (2048, 1024, 512, 256, 128, 64, 32, 16):
        if cand <= limit and rows % cand == 0:
            return cand
    return rows


def _mm_call(name, a, b, dims, grid, a_spec, b_spec, o_spec, o_shape, acc_shape, k_axes, out_dtype, deps=(),
             fuse=None, split=1):
    nk = 1
    for ax in k_axes:
        nk *= grid[ax]
    fn, extra, out_dtypes = fuse if fuse else (lambda acc: (acc,), (), (out_dtype,))
    n_extra, n_out = len(extra), len(out_dtypes)

    def finish(acc, rest):
        o_refs = rest[n_extra + len(deps):n_extra + len(deps) + n_out]
        for ref, val in zip(o_refs, fn(acc, *[r[...] for r in rest[:n_extra]])):
            ref[...] = val.astype(ref.dtype)

    def product(a_ref, b_ref):
        if split == 1:
            return lax.dot_general(a_ref[...], b_ref[...], dims, preferred_element_type=F32)
        wide = a_ref.shape[1] // split
        return sum(lax.dot_general(a_ref[:, q * wide:(q + 1) * wide], b_ref[q], dims, preferred_element_type=F32)
                   for q in range(split))

    def one_step(a_ref, b_ref, *rest):
        finish(product(a_ref, b_ref), rest)

    def accumulate(a_ref, b_ref, *rest):
        acc_ref = rest[-1]
        k = 0
        for ax in k_axes:
            k = k * grid[ax] + pl.program_id(ax)
        part = product(a_ref, b_ref)

        @pl.when(k == 0)
        def _():
            acc_ref[...] = part

        @pl.when((k > 0) & (k < nk - 1))
        def _():
            acc_ref[...] += part

        @pl.when(k == nk - 1)
        def _():
            finish(acc_ref[...] + part, rest)

    sem = tuple("arbitrary" if ax in k_axes else "parallel" for ax in range(len(grid)))
    outs = pl.pallas_call(
        one_step if nk == 1 else accumulate, name=name, grid=grid,
        in_specs=[a_spec, b_spec] + [o_spec] * n_extra + [pl.BlockSpec(memory_space=pl.ANY)] * len(deps),
        out_specs=[o_spec] * n_out, out_shape=[jax.ShapeDtypeStruct(o_shape, dt) for dt in out_dtypes],
        scratch_shapes=[] if nk == 1 else [pltpu.VMEM(acc_shape, F32)], compiler_params=_params(sem),
    )(a, b, *extra, *deps)
    return outs if fuse else outs[0]


def mm_nn(name, a, b3, out_dtype=F32, out3=False, deps=(), fuse=None):
    m, k = a.shape
    g, _, n = b3.shape
    tm, tk = _tile(m, (1024, 512, 256)), _tile(k, (2048, 1024, 512, 256))
    tn = n if out3 else _tile(n, (1024, 512, 256, 128))
    nj = n // tn
    grid = (m // tm, g, nj, k // tk)
    a_spec = pl.BlockSpec((tm, tk), lambda i, gg, j, kk: (i, kk))
    b_spec = pl.BlockSpec((None, tk, tn), lambda i, gg, j, kk: (gg, kk, j))
    if out3:
        o_spec = pl.BlockSpec((None, tm, tn), lambda i, gg, j, kk: (gg, i, j))
        o_shape = (g, m, n)
    else:
        o_spec = pl.BlockSpec((tm, tn), lambda i, gg, j, kk: (i, gg * nj + j))
        o_shape = (m, g * n)
    return _mm_call(name, a, b3, NN, grid, a_spec, b_spec, o_spec, o_shape, (tm, tn), (3,), out_dtype, deps, fuse)


def mm_nt(name, a, b3, out_dtype=F32, deps=(), fuse=None, stacked=False):
    g, k, n = b3.shape
    a3 = a.ndim == 3 and not stacked
    m = a.shape[1] if a.ndim == 3 else a.shape[0]
    tm, tko = _tile(m, (1024, 512, 256)), _tile(k, (1024, 512, 256))
    tc = n if a3 else _tile(n, (2048, 1024, 512, 256, 128))
    nc = n // tc
    per = g // a.shape[0] if stacked else g
    pair = 2 if (not a3 and nc == 1 and per % 2 == 0 and tc <= 1024) else 1
    grid = (m // tm, k // tko, g // pair, nc)
    if a3:
        a_spec = pl.BlockSpec((None, tm, tc), lambda i, j, gg, c: (gg, i, c))
    elif stacked:
        a_spec = pl.BlockSpec((None, tm, pair * tc),
                              lambda i, j, gg, c: ((gg * pair) // per, i, (((gg * pair) % per) // pair) * nc + c))
    else:
        a_spec = pl.BlockSpec((tm, pair * tc), lambda i, j, gg, c: (i, gg * nc + c))
    if pair == 1:
        b_spec = pl.BlockSpec((None, tko, tc), lambda i, j, gg, c: (gg, j, c))
    else:
        b_spec = pl.BlockSpec((pair, tko, tc), lambda i, j, gg, c: (gg, j, c))
    o_spec = pl.BlockSpec((tm, tko), lambda i, j, gg, c: (i, j))
    return _mm_call(name, a, b3, NT, grid, a_spec, b_spec, o_spec, (m, k), (tm, tko), (2, 3), out_dtype, deps, fuse,
                    pair)


def mm_tn(name, a, b, g, out_dtype=BF16, deps=(), stacked=False):
    t, k = a.shape
    b3 = b.ndim == 3 and not stacked
    n = b.shape[2] if b3 else (b.shape[0] * b.shape[2] if stacked else b.shape[1]) // g
    tm = _tile(k, (1024, 512, 256))
    tn = n if b3 else _tile(n, (1024, 512, 256, 128))
    tt = _tile(t, (2048, 1024, 512, 256))
    nj = n // tn
    grid = (g, k // tm, nj, t // tt)
    a_spec = pl.BlockSpec((tt, tm), lambda gg, i, j, s: (s, i))
    if b3:
        b_spec = pl.BlockSpec((None, tt, tn), lambda gg, i, j, s: (gg, s, j))
    elif stacked:
        per = g // b.shape[0]
        b_spec = pl.BlockSpec((None, tt, tn), lambda gg, i, j, s: (gg // per, s, (gg % per) * nj + j))
    else:
        b_spec = pl.BlockSpec((tt, tn), lambda gg, i, j, s: (s, gg * nj + j))
    o_spec = pl.BlockSpec((None, tm, tn), lambda gg, i, j, s: (gg, i, j))
    return _mm_call(name, a, b, TN, grid, a_spec, b_spec, o_spec, (g, k, n), (tm, tn), (3,), out_dtype, deps)


def rowwise(name, fn, rows, vecs=()):
    t = rows[0].shape[0]
    wmax = max(r.shape[1] for r in rows)
    tr = _row_tile(t, ROW_TILE_BYTES // (4 * wmax))
    row_s = [jax.ShapeDtypeStruct((tr, r.shape[1]), r.dtype) for r in rows]
    vec_s = [jax.ShapeDtypeStruct(v.shape, v.dtype) for v in vecs]
    out_rows_s, out_sums_s = jax.eval_shape(fn, *row_s, *vec_s)
    n_in, n_r = len(rows) + len(vecs), len(out_rows_s)

    def body(*refs):
        i = pl.program_id(0)
        o_rows, o_sums = fn(*[r[...] for r in refs[:n_in]])
        for ref, val in zip(refs[n_in:n_in + n_r], o_rows):
            ref[...] = val

        if out_sums_s:
            @pl.when(i == 0)
            def _():
                for ref in refs[n_in + n_r:]:
                    ref[...] = jnp.zeros_like(ref)

            for ref, val in zip(refs[n_in + n_r:], o_sums):
                ref[...] += val

    in_specs = [pl.BlockSpec((tr, r.shape[1]), lambda i: (i, 0)) for r in rows]
    in_specs += [pl.BlockSpec(v.shape, lambda i: (0, 0)) for v in vecs]
    out_specs = [pl.BlockSpec((tr, s.shape[1]), lambda i: (i, 0)) for s in out_rows_s]
    out_specs += [pl.BlockSpec(s.shape, lambda i: (0, 0)) for s in out_sums_s]
    out_shape = [jax.ShapeDtypeStruct((t, s.shape[1]), s.dtype) for s in out_rows_s]
    out_shape += [jax.ShapeDtypeStruct(s.shape, s.dtype) for s in out_sums_s]
    outs = pl.pallas_call(
        body, name=name, grid=(t // tr,), in_specs=in_specs, out_specs=out_specs, out_shape=out_shape,
        compiler_params=_params(("arbitrary",)),
    )(*rows, *vecs)
    return outs[:n_r], outs[n_r:]


def _rms(x, gain):
    return x * lax.rsqrt(jnp.mean(x * x, axis=-1, keepdims=True) + NORM_EPS) * gain


def _norm_fwd(x, gain):
    return (_rms(x, gain).astype(BF16),), ()


def _add_norm_fwd(h, a, gain):
    h = h + a
    return (h, _rms(h, gain).astype(BF16)), ()


def _relu2(pre):
    r = jnp.maximum(pre, 0.0)
    return pre, r * r


def _ple(h, gpre, pu):
    return h + pu * jax.nn.sigmoid(gpre)


def _ple_two_norms_fwd(h, gpre, pu, gain_a, gain_b):
    h = _ple(h, gpre, pu)
    return (h, _rms(h, gain_a).astype(BF16), _rms(h, gain_b).astype(BF16)), ()


def _tail_fwd_bwd(h, gpre, pu, target, gain):
    def row_loss(h, gpre, pu, gain):
        y = _rms(_ple(h, gpre, pu), gain)
        return 0.5 * jnp.mean(jnp.square(y - target), axis=-1, keepdims=True)

    loss, vjp = jax.vjp(row_loss, h, gpre, pu, gain)
    dh, dgpre, dpu, dgain = vjp(jnp.ones_like(loss))
    loss = jnp.broadcast_to(jnp.sum(loss, axis=0, keepdims=True), (1, 128))
    return (dh, dgpre.astype(BF16), dpu.astype(BF16)), (dgain, loss)


def _norm_bwd(h, du, dh_in, gain):
    _, vjp = jax.vjp(_rms, h, gain)
    dh, dgain = vjp(du)
    dh = dh_in + dh
    return (dh, dh.astype(BF16)), (dgain,)


def _two_norms_ple_bwd(h, du_a, du_b, dh_in, gpre, pu, gain_a, gain_b):
    _, vjp = jax.vjp(lambda h, ga, gb: (_rms(h, ga), _rms(h, gb)), h, gain_a, gain_b)
    dh, dga, dgb = vjp((du_a, du_b))
    dh = dh_in + dh
    _, gate_vjp = jax.vjp(lambda g, u: u * jax.nn.sigmoid(g), gpre, pu)
    dgpre, dpu = gate_vjp(dh)
    return (dh, dgpre.astype(BF16), dpu.astype(BF16)), (dga, dgb)


def _relu2_bwd(dact, pre):
    return (dact * 2.0 * jnp.maximum(pre.astype(F32), 0.0),)


def _bf16_dot(dims_fwd, dims_da, dims_db, swap_da, swap_db):
    @jax.custom_vjp
    def dot(a, b):
        return lax.dot_general(a.astype(BF16), b.astype(BF16), dims_fwd, preferred_element_type=F32)

    def fwd(a, b):
        return dot(a, b), (a, b)

    def bwd(res, ct):
        a, b = res
        ct, a, b = ct.astype(BF16), a.astype(BF16), b.astype(BF16)
        da = lax.dot_general(*((b, ct) if swap_da else (ct, b)), dims_da, preferred_element_type=F32)
        db = lax.dot_general(*((ct, a) if swap_db else (a, ct)), dims_db, preferred_element_type=F32)
        return da, db

    dot.defvjp(fwd, bwd)
    return dot


_dot_nn = _bf16_dot(NN, NT, TN, False, False)
_dot_nt = _bf16_dot(NT, NN, TN, False, True)
_dot_tn = _bf16_dot(TN, NT, NN, True, False)


def _chunk_masks(transposed):
    r = lax.broadcasted_iota(jnp.int32, (TILE, TILE), 0)
    c = lax.broadcasted_iota(jnp.int32, (TILE, TILE), 1)
    same = (r // CHUNK) == (c // CHUNK)
    causal = same & ((r <= c) if transposed else (c <= r))
    return causal, same


def _chunk_scan(x, reverse):
    pos = lax.broadcasted_iota(jnp.int32, x.shape, 0) % CHUNK
    step = 1
    while step < CHUNK:
        if reverse:
            x = x + jnp.where(pos < CHUNK - step, pltpu.roll(x, x.shape[0] - step, axis=0), 0.0)
        else:
            x = x + jnp.where(pos >= step, pltpu.roll(x, step, axis=0), 0.0)
        step *= 2
    return x


def _chunk_total(x):
    return _chunk_scan(x, False) + _chunk_scan(x, True) - x


@jax.custom_vjp
def _chunk_sums(x):
    return _chunk_scan(x, False), _chunk_total(x)


def _chunk_sums_fwd(x):
    return _chunk_sums(x), None


def _chunk_sums_bwd(_, ct):
    return (_chunk_scan(ct[0], True) + _chunk_total(ct[1]),)


_chunk_sums.defvjp(_chunk_sums_fwd, _chunk_sums_bwd)


def _hgrn_tile(q, f, i, g, lgt, hg, st):
    d = q.shape[1]
    l0, l1 = lgt[0:1], lgt[1:2]
    mx = jnp.maximum(l0, l1)
    e0, e1 = jnp.exp(l0 - mx), jnp.exp(l1 - mx)
    lb = e0 / (e0 + e1)
    fg = lb + (1.0 - lb) * jax.nn.sigmoid(f)
    k = 1.0 - fg
    causal, _ = _chunk_masks(False)
    b, b_last = _chunk_sums(jnp.log(fg))
    q_in = q * jax.nn.sigmoid(q) * (d ** -0.5) * jnp.exp(b)
    k_in = k * jnp.exp(-b)
    k_end = k * jnp.exp(b_last - b)
    att = jnp.where(causal, _dot_nt(q_in, k_in), 0.0)
    o_intra = _dot_nn(att, i)
    n_chunks = TILE // CHUNK
    chunk_of_row = lax.broadcasted_iota(jnp.int32, (TILE, 1), 0) // CHUNK

    def spread(a):
        return jnp.concatenate([jnp.where(chunk_of_row == n, a, 0.0) for n in range(n_chunks)], axis=1)

    increments = _dot_tn(i, spread(k_end))
    states = []
    for n in range(n_chunks):
        states.append(st)
        decay = jnp.exp(jnp.mean(b_last[n * CHUNK:(n + 1) * CHUNK], axis=0, keepdims=True))
        st = st * decay + increments[:, n * d:(n + 1) * d]
    o = o_intra + _dot_nt(spread(q_in), jnp.concatenate(states, axis=1))
    o = o * lax.rsqrt(jnp.mean(o * o, axis=-1, keepdims=True) + NORM_EPS) * hg
    return o * (g * jax.nn.sigmoid(g)), st


def hgrn_fwd(z, lgt, hg):
    t, d4 = z.shape
    d = d4 // 4
    nh, nt = d // HEAD_DIM, t // TILE
    hp = HEADS_PER_STEP
    wide = hp * HEAD_DIM

    def body(q_ref, f_ref, i_ref, g_ref, lgt_ref, hg_ref, o_ref, st_out_ref, st_ref):
        tt = pl.program_id(1)

        @pl.when(tt == 0)
        def _():
            st_ref[...] = jnp.zeros_like(st_ref)

        for hh in range(hp):
            cols = slice(hh * HEAD_DIM, (hh + 1) * HEAD_DIM)
            st = st_ref[hh]
            st_out_ref[hh] = st
            o, st = _hgrn_tile(q_ref[:, cols], f_ref[:, cols], i_ref[:, cols], g_ref[:, cols], lgt_ref[:, cols],
                               hg_ref[...], st)
            o_ref[:, cols] = o.astype(o_ref.dtype)
            st_ref[hh] = st

    def part(p):
        return pl.BlockSpec((TILE, wide), lambda h, tt: (tt, p * (nh // hp) + h))

    return pl.pallas_call(
        body, name="hgrn_fwd", grid=(nh // hp, nt),
        in_specs=[part(0), part(1), part(2), part(3),
                  pl.BlockSpec((2, wide), lambda h, tt: (0, h)),
                  pl.BlockSpec((1, HEAD_DIM), lambda h, tt: (0, 0))],
        out_specs=[pl.BlockSpec((TILE, wide), lambda h, tt: (tt, h)),
                   pl.BlockSpec((hp, None, HEAD_DIM, HEAD_DIM), lambda h, tt: (h, tt, 0, 0))],
        out_shape=[jax.ShapeDtypeStruct((t, d), BF16),
                   jax.ShapeDtypeStruct((nh, nt, HEAD_DIM, HEAD_DIM), F32)],
        scratch_shapes=[pltpu.VMEM((hp, HEAD_DIM, HEAD_DIM), F32)],
        compiler_params=_params(("parallel", "arbitrary")),
    )(z, z, z, z, lgt, hg)


def hgrn_bwd(z, lgt, hg, states, dout):
    t, d4 = z.shape
    d = d4 // 4
    nh, nt = d // HEAD_DIM, t // TILE
    hp = HEADS_PER_STEP
    wide = hp * HEAD_DIM

    def body(q_ref, f_ref, i_ref, g_ref, lgt_ref, hg_ref, st_in_ref, do_ref, dz_ref, dlgt_ref, dhg_ref, dst_ref):
        h, tt = pl.program_id(0), pl.program_id(1)

        @pl.when(tt == 0)
        def _():
            dst_ref[...] = jnp.zeros_like(dst_ref)
            dlgt_ref[...] = jnp.zeros_like(dlgt_ref)

        @pl.when((tt == 0) & (h == 0))
        def _():
            dhg_ref[...] = jnp.zeros_like(dhg_ref)

        for hh in range(hp):
            cols = slice(hh * HEAD_DIM, (hh + 1) * HEAD_DIM)
            _, vjp = jax.vjp(_hgrn_tile, q_ref[:, cols], f_ref[:, cols], i_ref[:, cols], g_ref[:, cols],
                             lgt_ref[:, cols], hg_ref[...], st_in_ref[hh])
            grads = vjp((do_ref[:, cols], dst_ref[hh]))
            for p in range(4):
                dz_ref[p, :, cols] = grads[p].astype(dz_ref.dtype)
            dlgt_ref[:, cols] += grads[4]
            dhg_ref[...] += grads[5]
            dst_ref[hh] = grads[6]

    def part(p):
        return pl.BlockSpec((TILE, wide), lambda h, tt: (nt - 1 - tt, p * (nh // hp) + h))

    return pl.pallas_call(
        body, name="hgrn_bwd", grid=(nh // hp, nt),
        in_specs=[part(0), part(1), part(2), part(3),
                  pl.BlockSpec((2, wide), lambda h, tt: (0, h)),
                  pl.BlockSpec((1, HEAD_DIM), lambda h, tt: (0, 0)),
                  pl.BlockSpec((hp, None, HEAD_DIM, HEAD_DIM), lambda h, tt: (h, nt - 1 - tt, 0, 0)),
                  pl.BlockSpec((TILE, wide), lambda h, tt: (nt - 1 - tt, h))],
        out_specs=[pl.BlockSpec((4, TILE, wide), lambda h, tt: (0, nt - 1 - tt, h)),
                   pl.BlockSpec((2, wide), lambda h, tt: (0, h)),
                   pl.BlockSpec((1, HEAD_DIM), lambda h, tt: (0, 0))],
        out_shape=[jax.ShapeDtypeStruct((4, t, d), BF16),
                   jax.ShapeDtypeStruct((2, d), F32),
                   jax.ShapeDtypeStruct((1, HEAD_DIM), F32)],
        scratch_shapes=[pltpu.VMEM((hp, HEAD_DIM, HEAD_DIM), F32)],
        compiler_params=_params(("arbitrary", "arbitrary")),
    )(z, z, z, z, lgt, hg, states, dout)


def _log_sigmoid(x):
    return jnp.minimum(x, 0.0) - jnp.log(1.0 + jnp.exp(-jnp.abs(x)))


def decay_fwd(fl_t, b_f):
    nh, t = fl_t.shape

    def body(fl_ref, b_ref, out_ref):
        r = lax.broadcasted_iota(jnp.int32, (128, 128), 0)
        c = lax.broadcasted_iota(jnp.int32, (128, 128), 1)
        upper = (r <= c).astype(F32)
        carry = jnp.zeros((nh, 1), F32)
        for j in range(t // 128):
            cols = slice(j * 128, (j + 1) * 128)
            ls = _log_sigmoid(fl_ref[:, cols] + b_ref[...])
            out_ref[:, cols] = carry + jnp.dot(ls, upper, precision=lax.Precision.HIGHEST,
                                               preferred_element_type=F32)
            carry = carry + jnp.sum(ls, axis=1, keepdims=True)

    return pl.pallas_call(body, name="decay_fwd", out_shape=jax.ShapeDtypeStruct((nh, t), F32),
                          compiler_params=_params(None))(fl_t, b_f)


def decay_bwd(fl_t, b_f, ddcum):
    nh, t = fl_t.shape

    def body(fl_ref, b_ref, dd_ref, dfl_ref, db_ref):
        r = lax.broadcasted_iota(jnp.int32, (128, 128), 0)
        c = lax.broadcasted_iota(jnp.int32, (128, 128), 1)
        lower = (r >= c).astype(F32)
        carry = jnp.zeros((nh, 1), F32)
        db = jnp.zeros((nh, 1), F32)
        for j in reversed(range(t // 128)):
            cols = slice(j * 128, (j + 1) * 128)
            dd = dd_ref[:, cols]
            dls = carry + jnp.dot(dd, lower, precision=lax.Precision.HIGHEST, preferred_element_type=F32)
            carry = carry + jnp.sum(dd, axis=1, keepdims=True)
            dfl = dls * jax.nn.sigmoid(-(fl_ref[:, cols] + b_ref[...]))
            dfl_ref[:, cols] = dfl
            db = db + jnp.sum(dfl, axis=1, keepdims=True)
        db_ref[...] = db

    return pl.pallas_call(body, name="decay_bwd",
                          out_shape=[jax.ShapeDtypeStruct((nh, t), F32), jax.ShapeDtypeStruct((nh, 1), F32)],
                          compiler_params=_params(None))(fl_t, b_f, ddcum)


def _attn_parts(t):
    tq = _tile(t, (256, 128))
    per_part = 2 if t // tq >= 4 else 1
    return tq, [(first, per_part, (first + per_part) * tq) for first in range(0, t // tq, per_part)]


def _attn_logits(q_ref, k_ref, dcol_ref, drow_ref, row0, tq, keys):
    qs = (q_ref[...] * (HEAD_DIM ** -0.5)).astype(BF16)
    s = lax.dot_general(qs, k_ref[...], NT, preferred_element_type=F32)
    s = s + dcol_ref[...] - drow_ref[...]
    row = row0 + lax.broadcasted_iota(jnp.int32, (tq, keys), 0)
    col = lax.broadcasted_iota(jnp.int32, (tq, keys), 1)
    return qs, jnp.where(col <= row, s, NEG_BIG)


def attn_fwd(q, k, v, dcol, drow):
    t, d = q.shape
    nh = d // HEAD_DIM
    tq, parts = _attn_parts(t)
    o = lse = None
    for first, count, keys in parts:
        def body(q_ref, k_ref, v_ref, dcol_ref, drow_ref, *rest, first=first, keys=keys):
            o_ref, lse_ref = rest[-2:]
            _, s = _attn_logits(q_ref, k_ref, dcol_ref, drow_ref, (first + pl.program_id(1)) * tq, tq, keys)
            m = jnp.max(s, axis=1, keepdims=True)
            p = jnp.exp(s - m)
            l = jnp.sum(p, axis=1, keepdims=True)
            acc = jnp.dot(p.astype(BF16), v_ref[...], preferred_element_type=F32)
            o_ref[...] = (acc / l).astype(o_ref.dtype)
            lse_ref[...] = m + jnp.log(l)

        tile = pl.BlockSpec((tq, HEAD_DIM), lambda h, i, first=first: (first + i, h))
        col = pl.BlockSpec((None, tq, 1), lambda h, i, first=first: (h, first + i, 0))
        seen = pl.BlockSpec((keys, HEAD_DIM), lambda h, i: (0, h))
        carried = [] if o is None else [o, lse]
        o, lse = pl.pallas_call(
            body, name=f"attn_fwd_{first}", grid=(nh, count),
            in_specs=[tile, seen, seen, col, pl.BlockSpec((None, 1, keys), lambda h, i: (h, 0, 0))]
            + [pl.BlockSpec(memory_space=pl.ANY)] * len(carried),
            out_specs=[tile, col],
            out_shape=[jax.ShapeDtypeStruct((t, d), BF16), jax.ShapeDtypeStruct((nh, t, 1), F32)],
            input_output_aliases={5: 0, 6: 1} if carried else {},
            compiler_params=_params(("parallel", "parallel")),
        )(q, k, v, dcol, drow, *carried)
    return o, lse


def attn_bwd(q, k, v, dcol, drow, lse, do):
    t, d = q.shape
    nh = d // HEAD_DIM
    tq, parts = _attn_parts(t)
    dq = dk = dv = ddrow = None
    for first, count, keys in reversed(parts):
        first_call = dq is None

        def body(q_ref, k_ref, v_ref, dcol_ref, drow_ref, lse_ref, do_ref, *rest, first=first, keys=keys,
                 count=count, first_call=first_call):
            dq_ref, dk_ref, dv_ref, ddrow_ref, dk_acc, dv_acc, ddrow_acc = rest[-7:]
            i = pl.program_id(1)

            @pl.when(i == 0)
            def _():
                if first_call:
                    dk_acc[...] = jnp.zeros_like(dk_acc)
                    dv_acc[...] = jnp.zeros_like(dv_acc)
                    ddrow_acc[...] = jnp.zeros_like(ddrow_acc)
                else:
                    dk_acc[...] = rest[1][...]
                    dv_acc[...] = rest[2][...]
                    ddrow_acc[...] = rest[3][...]

            qs, s = _attn_logits(q_ref, k_ref, dcol_ref, drow_ref, (first + i) * tq, tq, keys)
            p = jnp.exp(s - lse_ref[...])
            do = do_ref[...]
            dp = lax.dot_general(do, v_ref[...], NT, preferred_element_type=F32)
            ds = p * (dp - jnp.sum(p * dp, axis=1, keepdims=True))
            dsb = ds.astype(BF16)
            dq_ref[...] = (jnp.dot(dsb, k_ref[...], preferred_element_type=F32) * (HEAD_DIM ** -0.5)).astype(dq_ref.dtype)
            dk_acc[...] += lax.dot_general(dsb, qs, TN, preferred_element_type=F32)
            dv_acc[...] += lax.dot_general(p.astype(BF16), do, TN, preferred_element_type=F32)
            ddrow_acc[...] -= jnp.sum(ds, axis=0, keepdims=True)

            @pl.when(i == count - 1)
            def _():
                dk_ref[...] = dk_acc[...]
                dv_ref[...] = dv_acc[...]
                ddrow_ref[...] = ddrow_acc[...]

        tile = pl.BlockSpec((tq, HEAD_DIM), lambda h, i, first=first: (first + i, h))
        col = pl.BlockSpec((None, tq, 1), lambda h, i, first=first: (h, first + i, 0))
        seen = pl.BlockSpec((keys, HEAD_DIM), lambda h, i: (0, h))
        seen_row = pl.BlockSpec((None, 1, keys), lambda h, i: (h, 0, 0))
        carried = [] if first_call else [dq, dk, dv, ddrow]
        carried_specs = [] if first_call else [pl.BlockSpec(memory_space=pl.ANY), seen, seen, seen_row]
        dq, dk, dv, ddrow = pl.pallas_call(
            body, name=f"attn_bwd_{first}", grid=(nh, count),
            in_specs=[tile, seen, seen, col, seen_row, col, tile] + carried_specs,
            out_specs=[tile, seen, seen, seen_row],
            out_shape=[jax.ShapeDtypeStruct((t, d), BF16), jax.ShapeDtypeStruct((t, d), F32),
                       jax.ShapeDtypeStruct((t, d), F32), jax.ShapeDtypeStruct((nh, 1, t), F32)],
            scratch_shapes=[pltpu.VMEM((keys, HEAD_DIM), F32), pltpu.VMEM((keys, HEAD_DIM), F32),
                            pltpu.VMEM((1, keys), F32)],
            input_output_aliases={} if first_call else {7: 0, 8: 1, 9: 2, 10: 3},
            compiler_params=_params(("parallel", "arbitrary")),
        )(q, k, v, dcol, drow, lse, do, *carried)
    return dq, dk, dv, ddrow


def _my_index():
    return (lax.axis_index("x") * 2 + lax.axis_index("y")) * 2 + lax.axis_index("c")


def _exchange(name, src, gather, deps=()):
    shape = src.shape if gather else src.shape[1:]

    def body(src_ref, *rest):
        out_ref, send_sems, recv_sems, local_sem = rest[len(deps):]
        x, y, c = (lax.axis_index(a) for a in MESH_AXES)
        me = (x * 2 + y) * 2 + c
        mine = src_ref if gather else src_ref.at[me]
        local = pltpu.make_async_copy(mine, out_ref.at[me], local_sem)
        local.start()
        copies = []
        for dlt in range(1, N_DEV):
            dx, dy, dc = dlt // 4, (dlt // 2) % 2, dlt % 2
            px, py, pc = x ^ dx, y ^ dy, c ^ dc
            peer = (px * 2 + py) * 2 + pc
            copies.append(pltpu.make_async_remote_copy(
                src_ref=src_ref if gather else src_ref.at[peer], dst_ref=out_ref.at[me],
                send_sem=send_sems.at[dlt - 1], recv_sem=recv_sems.at[dlt - 1],
                device_id=(px, py, pc), device_id_type=pl.DeviceIdType.MESH))
        for cp in copies:
            cp.start()
        for cp in copies:
            cp.wait_recv()
        for cp in copies:
            cp.wait_send()
        local.wait()

    return pl.pallas_call(
        body, name=name, out_shape=jax.ShapeDtypeStruct((N_DEV,) + tuple(shape), src.dtype),
        in_specs=[pl.BlockSpec(memory_space=pl.ANY)] * (1 + len(deps)), out_specs=pl.BlockSpec(memory_space=pl.ANY),
        scratch_shapes=[pltpu.SemaphoreType.DMA((N_DEV - 1,)), pltpu.SemaphoreType.DMA((N_DEV - 1,)),
                        pltpu.SemaphoreType.DMA],
        compiler_params=pltpu.CompilerParams(has_side_effects=True),
    )(src, *deps)


def all_gather(name, x, deps=()):
    return _exchange(name, x, True, deps)


_HBM = pl.BlockSpec(memory_space=pltpu.HBM)
_SEM = pl.BlockSpec(memory_space=pltpu.SEMAPHORE)
_DATAFLOW = pltpu.SideEffectType.DATAFLOW_SIDE_EFFECTING


def _peer_copies(src_ref, land_ref, send_sems, recv_sems, gather):
    x, y, c = (lax.axis_index(a) for a in MESH_AXES)
    me = (x * 2 + y) * 2 + c
    copies = []
    for dlt in range(1, N_DEV):
        px, py, pc = x ^ (dlt // 4), y ^ ((dlt // 2) % 2), c ^ (dlt % 2)
        peer = (px * 2 + py) * 2 + pc
        copies.append(pltpu.make_async_remote_copy(
            src_ref=src_ref if gather else src_ref.at[peer], dst_ref=land_ref.at[me],
            send_sem=send_sems.at[dlt - 1], recv_sem=recv_sems.at[dlt - 1],
            device_id=(px, py, pc), device_id_type=pl.DeviceIdType.MESH))
    return copies


def exchange_start(name, srcs, gather):
    n = len(srcs)
    lands = [lax.empty((N_DEV,) + tuple(s.shape if gather else s.shape[1:]), s.dtype) for s in srcs]

    def body(*refs):
        src_refs, land_refs = refs[:n], refs[n:2 * n]
        send_sems, recv_sems = refs[2 * n:3 * n], refs[3 * n:4 * n]
        token = refs[-1]
        for j in range(n):
            for cp in _peer_copies(src_refs[j], land_refs[j], send_sems[j], recv_sems[j], gather):
                cp.start()
        token[...] = jnp.zeros_like(token)

    sems = [pltpu.SemaphoreType.DMA((N_DEV - 1,))] * (2 * n)
    thru = [pltpu.HBM(a.shape, a.dtype) for a in list(srcs) + lands]
    outs = pl.pallas_call(
        body, name=name, out_shape=tuple(sems + thru + [jax.ShapeDtypeStruct((8, 128), F32)]),
        in_specs=[_HBM] * (2 * n), out_specs=tuple([_SEM] * (2 * n) + [_HBM] * (2 * n) + [pl.BlockSpec(memory_space=pltpu.VMEM)]),
        input_output_aliases={j: 2 * n + j for j in range(2 * n)},
        compiler_params=pltpu.CompilerParams(has_side_effects=_DATAFLOW),
    )(*[pltpu.with_memory_space_constraint(a, pltpu.HBM) for a in list(srcs) + lands])
    handles = [(outs[j], outs[n + j], outs[2 * n + j], outs[3 * n + j]) for j in range(n)]
    return handles, outs[-1]


def exchange_wait(name, handle, after, gather):
    send_sems, recv_sems, src, land = handle

    def body(src_ref, land_ref, send_ref, recv_ref, after_ref, src_out, land_out):
        for cp in _peer_copies(src_ref, land_ref, send_ref, recv_ref, gather):
            cp.wait_send()
            cp.wait_recv()

    return pl.pallas_call(
        body, name=name, out_shape=(pltpu.HBM(src.shape, src.dtype), pltpu.HBM(land.shape, land.dtype)),
        in_specs=[_HBM, _HBM, _SEM, _SEM, pl.BlockSpec(memory_space=pl.ANY)], out_specs=(_HBM, _HBM),
        input_output_aliases={0: 0, 1: 1},
        compiler_params=pltpu.CompilerParams(has_side_effects=_DATAFLOW),
    )(src, land, send_sems, recv_sems, after)


N_OTHER_CHIPS = 3


def _two_level_places():
    x, y, c = (lax.axis_index(a) for a in MESH_AXES)
    return (x, y, c), (x * 2 + y) * 2 + c, (x, y, 1 - c), [(1 - x, y), (x, 1 - y), (1 - x, 1 - y)]


def _first_copies(land_ref, send_sems, recv_sems):
    (x, y, c), me, other_core, chips = _two_level_places()
    targets = [other_core] + [(cx, cy, c) for cx, cy in chips]
    return [pltpu.make_async_remote_copy(
        src_ref=land_ref.at[me], dst_ref=land_ref.at[me], send_sem=send_sems.at[k], recv_sem=recv_sems.at[k],
        device_id=to, device_id_type=pl.DeviceIdType.MESH) for k, to in enumerate(targets)]


def _passed_on_copies(land_ref, send_sems, recv_sems):
    (x, y, c), me, other_core, chips = _two_level_places()
    copies = []
    for k, (cx, cy) in enumerate(chips):
        slot = land_ref.at[(cx * 2 + cy) * 2 + c]
        copies.append(pltpu.make_async_remote_copy(
            src_ref=slot, dst_ref=slot, send_sem=send_sems.at[k], recv_sem=recv_sems.at[k],
            device_id=other_core, device_id_type=pl.DeviceIdType.MESH))
    return copies


def gather_start(name, lands):
    n = len(lands)

    def body(*refs):
        land_refs, send_sems, recv_sems = refs[:n], refs[n:2 * n], refs[2 * n:3 * n]
        for j in range(n):
            for cp in _first_copies(land_refs[j], send_sems[j], recv_sems[j]):
                cp.start()

    sems = [pltpu.SemaphoreType.DMA((1 + N_OTHER_CHIPS,))] * (2 * n)
    outs = pl.pallas_call(
        body, name=name, out_shape=tuple(sems + [pltpu.HBM(a.shape, a.dtype) for a in lands]),
        in_specs=[_HBM] * n, out_specs=tuple([_SEM] * (2 * n) + [_HBM] * n),
        input_output_aliases={j: 2 * n + j for j in range(n)},
        compiler_params=pltpu.CompilerParams(has_side_effects=_DATAFLOW),
    )(*[pltpu.with_memory_space_constraint(a, pltpu.HBM) for a in lands])
    return [[outs[j], outs[n + j], outs[2 * n + j]] for j in range(n)]


def gather_pass_on(name, handle, after):
    send_sems, recv_sems, land = handle

    def body(land_ref, recv_ref, after_ref, land_out, send2, recv2, token):
        arrivals = _first_copies(land_ref, recv_ref, recv_ref)
        for k, cp in enumerate(_passed_on_copies(land_ref, send2, recv2)):
            arrivals[1 + k].wait_recv()
            cp.start()
        token[...] = jnp.zeros_like(token)

    sem3 = pltpu.SemaphoreType.DMA((N_OTHER_CHIPS,))
    land, send2, recv2, token = pl.pallas_call(
        body, name=name,
        out_shape=(pltpu.HBM(land.shape, land.dtype), sem3, sem3, jax.ShapeDtypeStruct((8, 128), F32)),
        in_specs=[_HBM, _SEM, pl.BlockSpec(memory_space=pl.ANY)],
        out_specs=(_HBM, _SEM, _SEM, pl.BlockSpec(memory_space=pltpu.VMEM)),
        input_output_aliases={0: 0}, compiler_params=pltpu.CompilerParams(has_side_effects=_DATAFLOW),
    )(land, recv_sems, after)
    return [send_sems, recv_sems, land, send2, recv2], token


def gather_wait(name, handle, after):
    send_sems, recv_sems, land, send2, recv2 = handle

    def body(land_ref, send_ref, recv_ref, send2_ref, recv2_ref, after_ref, land_out):
        first = _first_copies(land_ref, send_ref, recv_ref)
        for cp in first:
            cp.wait_send()
        first[0].wait_recv()
        for cp in _passed_on_copies(land_ref, send2_ref, recv2_ref):
            cp.wait_send()
            cp.wait_recv()

    return pl.pallas_call(
        body, name=name, out_shape=pltpu.HBM(land.shape, land.dtype),
        in_specs=[_HBM, _SEM, _SEM, _SEM, _SEM, pl.BlockSpec(memory_space=pl.ANY)], out_specs=_HBM,
        input_output_aliases={0: 0}, compiler_params=pltpu.CompilerParams(has_side_effects=_DATAFLOW),
    )(land, send_sems, recv_sems, send2, recv2, after)


N_CHIPS = 4


def _pair_copies(g_ref, half_ref, send_sems, recv_sems):
    (x, y, c), me, other_core, chips = _two_level_places()
    return [pltpu.make_async_remote_copy(
        src_ref=g_ref.at[chip * 2 + (1 - c)], dst_ref=half_ref.at[chip], send_sem=send_sems.at[chip],
        recv_sem=recv_sems.at[chip], device_id=other_core, device_id_type=pl.DeviceIdType.MESH)
        for chip in range(N_CHIPS)]


def pair_start(name, g):
    half = lax.empty((N_CHIPS,) + g.shape[1:], g.dtype)

    def body(g_ref, half_ref, send_sems, recv_sems, g_out, half_out, token):
        for cp in _pair_copies(g_ref, half_ref, send_sems, recv_sems):
            cp.start()
        token[...] = jnp.zeros_like(token)

    sem = pltpu.SemaphoreType.DMA((N_CHIPS,))
    outs = pl.pallas_call(
        body, name=name,
        out_shape=(sem, sem, pltpu.HBM(g.shape, g.dtype), pltpu.HBM(half.shape, half.dtype),
                   jax.ShapeDtypeStruct((8, 128), F32)),
        in_specs=[_HBM, _HBM], out_specs=(_SEM, _SEM, _HBM, _HBM, pl.BlockSpec(memory_space=pltpu.VMEM)),
        input_output_aliases={0: 2, 1: 3}, compiler_params=pltpu.CompilerParams(has_side_effects=_DATAFLOW),
    )(pltpu.with_memory_space_constraint(g, pltpu.HBM), half)
    return list(outs[:4]), outs[4]


def pair_wait(name, handle, after):
    send_sems, recv_sems, g, half = handle

    def body(g_ref, half_ref, send_ref, recv_ref, after_ref, g_out, half_out):
        for cp in _pair_copies(g_ref, half_ref, send_ref, recv_ref):
            cp.wait_send()
            cp.wait_recv()

    return pl.pallas_call(
        body, name=name, out_shape=(pltpu.HBM(g.shape, g.dtype), pltpu.HBM(half.shape, half.dtype)),
        in_specs=[_HBM, _HBM, _SEM, _SEM, pl.BlockSpec(memory_space=pl.ANY)], out_specs=(_HBM, _HBM),
        input_output_aliases={0: 0, 1: 1}, compiler_params=pltpu.CompilerParams(has_side_effects=_DATAFLOW),
    )(g, half, send_sems, recv_sems, after)


def pair_sum(name, g, half):
    _, r, wd = g.shape
    tr = _row_tile(r, 4 * ROW_TILE_BYTES // (4 * wd))
    kind = lax.axis_index("c").astype(jnp.int32).reshape(1)

    def body(kind_ref, g_ref, half_ref, o_ref):
        o_ref[...] = (g_ref[...].astype(F32) + half_ref[...].astype(F32)).astype(o_ref.dtype)

    spec = pl.BlockSpec((None, tr, wd), lambda chip, i, kind_ref: (chip, i, 0))
    return pl.pallas_call(
        body, name=name,
        grid_spec=pltpu.PrefetchScalarGridSpec(
            num_scalar_prefetch=1, grid=(N_CHIPS, r // tr),
            in_specs=[pl.BlockSpec((None, tr, wd), lambda chip, i, kind_ref: (chip * 2 + kind_ref[0], i, 0)), spec],
            out_specs=spec),
        out_shape=jax.ShapeDtypeStruct((N_CHIPS, r, wd), g.dtype),
        compiler_params=_params(("parallel", "parallel")),
    )(kind, g, half)


def _chip_copies(sums_ref, land_ref, send_sems, recv_sems):
    (x, y, c), me, other_core, chips = _two_level_places()
    return [pltpu.make_async_remote_copy(
        src_ref=sums_ref.at[cx * 2 + cy], dst_ref=land_ref.at[x * 2 + y], send_sem=send_sems.at[k],
        recv_sem=recv_sems.at[k], device_id=(cx, cy, c), device_id_type=pl.DeviceIdType.MESH)
        for k, (cx, cy) in enumerate(chips)]


def chip_start(name, sums):
    land = lax.empty(sums.shape, sums.dtype)

    def body(sums_ref, land_ref, send_sems, recv_sems, sums_out, land_out, token):
        for cp in _chip_copies(sums_ref, land_ref, send_sems, recv_sems):
            cp.start()
        token[...] = jnp.zeros_like(token)

    sem = pltpu.SemaphoreType.DMA((N_OTHER_CHIPS,))
    outs = pl.pallas_call(
        body, name=name,
        out_shape=(sem, sem, pltpu.HBM(sums.shape, sums.dtype), pltpu.HBM(land.shape, land.dtype),
                   jax.ShapeDtypeStruct((8, 128), F32)),
        in_specs=[_HBM, _HBM], out_specs=(_SEM, _SEM, _HBM, _HBM, pl.BlockSpec(memory_space=pltpu.VMEM)),
        input_output_aliases={0: 2, 1: 3}, compiler_params=pltpu.CompilerParams(has_side_effects=_DATAFLOW),
    )(pltpu.with_memory_space_constraint(sums, pltpu.HBM), land)
    return list(outs[:4]), outs[4]


def chip_wait(name, handle, after):
    send_sems, recv_sems, sums, land = handle

    def body(sums_ref, land_ref, send_ref, recv_ref, after_ref, sums_out, land_out):
        for cp in _chip_copies(sums_ref, land_ref, send_ref, recv_ref):
            cp.wait_send()
            cp.wait_recv()

    return pl.pallas_call(
        body, name=name, out_shape=(pltpu.HBM(sums.shape, sums.dtype), pltpu.HBM(land.shape, land.dtype)),
        in_specs=[_HBM, _HBM, _SEM, _SEM, pl.BlockSpec(memory_space=pl.ANY)], out_specs=(_HBM, _HBM),
        input_output_aliases={0: 0, 1: 1}, compiler_params=pltpu.CompilerParams(has_side_effects=_DATAFLOW),
    )(sums, land, send_sems, recv_sems, after)


def adamw_reduce(name, parts, w, m, v):
    nl, r, wd = w.shape
    tr = _row_tile(r, 2 * ROW_TILE_BYTES // (8 * wd))

    def body(*refs):
        p_refs = refs[:nl]
        w_ref, m_ref, v_ref, g_ref, d_ref, nm_ref, nv_ref = refs[nl:]
        layer = pl.program_id(0)
        for j in range(nl):
            @pl.when(layer == j)
            def _(j=j):
                g = p_refs[j][0].astype(F32)
                for sender in range(1, p_refs[j].shape[0]):
                    g = g + p_refs[j][sender].astype(F32)
                nm = B1 * m_ref[...] + (1.0 - B1) * g
                nv = B2 * v_ref[...] + (1.0 - B2) * jnp.square(g)
                m_hat = nm / (1.0 - B1 ** STEP)
                v_hat = nv / (1.0 - B2 ** STEP)
                g_ref[...] = g
                d_ref[...] = -LR * (m_hat / (jnp.sqrt(v_hat) + ADAM_EPS) + WD * w_ref[...])
                nm_ref[...] = nm
                nv_ref[...] = nv

    def part_spec(j):
        return pl.BlockSpec((parts[j].shape[0], tr, wd), lambda l, i: (0, jnp.where(l == j, i, 0), 0))

    spec = pl.BlockSpec((None, tr, wd), lambda l, i: (l, i, 0))
    return pl.pallas_call(
        body, name=name, grid=(nl, r // tr),
        in_specs=[part_spec(j) for j in range(nl)] + [spec, spec, spec],
        out_specs=[spec] * 4, out_shape=[jax.ShapeDtypeStruct((nl, r, wd), F32)] * 4,
        compiler_params=_params(("arbitrary", "arbitrary")),
    )(*parts, w, m, v)


def _pack_rows(vectors, rows=None):
    flat = jnp.concatenate([a.reshape(-1).astype(F32) for a in vectors])
    n = flat.shape[0]
    if rows is None:
        rows = -(-n // 1024) * 8
    return jnp.pad(flat, (0, rows * 128 - n)).reshape(rows, 128)


def _unpack_rows(packed, like):
    flat = packed.reshape(-1)
    out, pos = [], 0
    for a in like:
        out.append(flat[pos:pos + a.size].reshape(a.shape))
        pos += a.size
    return out


def kernel(x, p, mix_norm, mlp_norm, ple_norm, w_a_in, a_lb_logits, a_head_gain, w_a_out, kv_norm, w_kvf, b_f, w_b_q, w_b_out, w_mlp_up, w_mlp_down, w_ple_gate, w_ple_up, final_norm, loss_target, m_mix_norm, m_mlp_norm, m_ple_norm, m_w_a_in, m_a_lb_logits, m_a_head_gain, m_w_a_out, m_kv_norm, m_w_kvf, m_b_f, m_w_b_q, m_w_b_out, m_w_mlp_up, m_w_mlp_down, m_w_ple_gate, m_w_ple_up, m_final_norm, v_mix_norm, v_mlp_norm, v_ple_norm, v_w_a_in, v_a_lb_logits, v_a_head_gain, v_w_a_out, v_kv_norm, v_w_kvf, v_b_f, v_w_b_q, v_w_b_out, v_w_mlp_up, v_w_mlp_down, v_w_ple_gate, v_w_ple_up, v_final_norm):
    t, d = x.shape[1], x.shape[2]
    nh = d // HEAD_DIM
    n_layers = 2
    x2 = x.reshape(t, d)
    target = loss_target.reshape(t, d)
    me = _my_index()

    shards = {"w_a_in": w_a_in[0], "w_a_out": w_a_out[0], "w_kvf": w_kvf, "w_b_q": w_b_q[0], "w_b_out": w_b_out[0]}
    for l in range(n_layers):
        shards.update({f"w_mlp_up{l}": w_mlp_up[l], f"w_mlp_down{l}": w_mlp_down[l],
                       f"w_ple_gate{l}": w_ple_gate[l], f"w_ple_up{l}": w_ple_up[l]})
    first_use = ["a_lb_logits", "w_a_in", "w_a_out", "w_mlp_up0", "w_mlp_down0", "w_ple_gate0", "w_ple_up0", "w_kvf",
                 "w_b_q", "w_b_out", "w_mlp_up1", "w_mlp_down1", "w_ple_gate1", "w_ple_up1"]
    row_sharded = ("w_a_out", "w_b_q", "w_b_out", "w_mlp_down", "w_ple_gate")
    shards_bf = [a_lb_logits] + [shards[n].astype(BF16) for n in first_use[1:]]
    ag_handles = gather_start("ag_start", [
        lax.dynamic_update_slice(lax.empty((N_DEV,) + a.shape, a.dtype), a[None], (me, 0, 0)) for a in shards_bf])
    passed_on = {}
    weights = {}

    def pass_on(j, after):
        if j < len(first_use) and j not in passed_on:
            passed_on[j] = gather_pass_on("ag_pass_" + first_use[j], ag_handles[j], after)

    def weight(name, after=None):
        if name not in weights:
            j = first_use.index(name)
            pass_on(j, after)
            pass_on(j + 1, after)
            behind = passed_on[j + 1][1] if j + 1 in passed_on else after
            g = gather_wait("ag_wait_" + name, passed_on[j][0], behind)
            if name.rstrip("01") in row_sharded:
                g = g.reshape(1, g.shape[0] * g.shape[1], g.shape[2])
            weights[name] = g
        return weights[name]

    lgt = weight("a_lb_logits", x2).transpose(1, 0, 2).reshape(2, d)
    p_bf = [p[l, 0].astype(BF16) for l in range(n_layers)]

    def row(vec):
        return vec.reshape(1, -1)

    def mlp_ple_fwd(l, h_in, a):
        (h_a, u_mlp), _ = rowwise(f"add_norm_mlp{l}", _add_norm_fwd, [h_in, a], [row(mlp_norm[l])])
        pre, act = mm_nn(f"mlp_up{l}", u_mlp, weight(f"w_mlp_up{l}", u_mlp), fuse=(_relu2, (), (BF16, BF16)))
        mo = mm_nn(f"mlp_down{l}", act, weight(f"w_mlp_down{l}", act))
        (h_b, u_ple), _ = rowwise(f"add_norm_ple{l}", _add_norm_fwd, [h_a, mo], [row(ple_norm[l])])
        gpre = mm_nn(f"ple_gate{l}", u_ple, weight(f"w_ple_gate{l}", u_ple))
        pu = mm_nn(f"ple_up{l}", p_bf[l], weight(f"w_ple_up{l}", gpre))
        return dict(h_a=h_a, u_mlp=u_mlp, pre=pre, act=act, h_b=h_b, u_ple=u_ple, gpre=gpre, pu=pu)

    (u0,), _ = rowwise("norm_mix0", _norm_fwd, [x2], [row(mix_norm[0])])
    z = mm_nn("a_in", u0, weight("w_a_in", u0))
    og, states = hgrn_fwd(z, lgt, a_head_gain)
    a0 = mm_nn("a_out", og, weight("w_a_out", og))
    s0 = mlp_ple_fwd(0, x2, a0)
    (h3, u_kv, u1), _ = rowwise("ple_norms", _ple_two_norms_fwd, [s0["h_b"], s0["gpre"], s0["pu"]],
                                [row(kv_norm), row(mix_norm[1])])
    hk = mm_nn("kvf", u_kv, weight("w_kvf", u_kv), out3=True)
    hk = hk.transpose(1, 0, 2).reshape(t, -1)
    k_bf, v_bf = hk[:, :d].astype(BF16), hk[:, d:2 * d].astype(BF16)
    fl_t = hk[:, 2 * d:].T
    b_f_col = b_f.reshape(nh, 1)
    dcum = decay_fwd(fl_t, b_f_col)
    dcol, drow = dcum.reshape(nh, t, 1), dcum.reshape(nh, 1, t)
    q = mm_nn("b_q", u1, weight("w_b_q", dcum))
    o, lse = attn_fwd(q, k_bf, v_bf, dcol, drow)
    a1 = mm_nn("b_out", o, weight("w_b_out", o))
    s1 = mlp_ple_fwd(1, h3, a1)

    (dh, dgpre, dpu), (d_final, loss_rows) = rowwise(
        "tail", _tail_fwd_bwd, [s1["h_b"], s1["gpre"], s1["pu"], target], [row(final_norm)])

    sent = {}
    tokens = []

    two_level = ("w_mlp_up0", "w_mlp_up1", "w_mlp_down0", "w_mlp_down1", "w_a_in", "w_kvf")
    swapping = []

    def send_grad(name, g):
        g = g.reshape(N_DEV, -1, g.shape[-1])
        if name in two_level:
            sent[name], token = pair_start("rs_pair_" + name, g)
            swapping.append(name)
        else:
            (sent[name],), token = exchange_start("rs_start_" + name, [g], False)
        tokens.append(token)

    def second_stage(after):
        for name in swapping:
            g, half = pair_wait("rs_pairwait_" + name, sent[name], after)
            sent[name], token = chip_start("rs_chip_" + name, pair_sum("rs_sum_" + name, g, half))
            tokens.append(token)
        swapping.clear()

    def after_sends():
        deps = tuple(tokens)
        tokens.clear()
        return deps

    def mlp_ple_bwd(l, s, dh, dgpre, dpu):
        send_grad(f"w_ple_gate{l}", mm_tn(f"d_ple_gate_w{l}", s["u_ple"], dgpre, 1, deps=after_sends()))
        send_grad(f"w_ple_up{l}", mm_tn(f"d_ple_up_w{l}", p_bf[l], dpu, N_DEV, deps=after_sends()))
        du = mm_nt(f"d_ple_gate_x{l}", dgpre, weight(f"w_ple_gate{l}"), deps=after_sends())
        (dh, dh_bf), (d_ple,) = rowwise(f"d_norm_ple{l}", _norm_bwd, [s["h_b"], du, dh], [row(ple_norm[l])])
        send_grad(f"w_mlp_down{l}", mm_tn(f"d_mlp_down_w{l}", s["act"], dh_bf, 1))
        (dpre,) = mm_nt(f"d_mlp_down_x{l}", dh_bf, weight(f"w_mlp_down{l}"), deps=after_sends(),
                        fuse=(_relu2_bwd, (s["pre"],), (BF16,)))
        second_stage(dpre)
        send_grad(f"w_mlp_up{l}", mm_tn(f"d_mlp_up_w{l}", s["u_mlp"], dpre, N_DEV))
        du = mm_nt(f"d_mlp_up_x{l}", dpre, weight(f"w_mlp_up{l}"), deps=after_sends())
        second_stage(du)
        (dh, dh_bf), (d_mlp,) = rowwise(f"d_norm_mlp{l}", _norm_bwd, [s["h_a"], du, dh], [row(mlp_norm[l])])
        return dh, dh_bf, d_ple, d_mlp

    dh, dh_bf, d_ple1, d_mlp1 = mlp_ple_bwd(1, s1, dh, dgpre, dpu)
    send_grad("w_b_out", mm_tn("d_b_out_w", o, dh_bf, 1))
    do = mm_nt("d_b_out_x", dh_bf, weight("w_b_out"), out_dtype=BF16, deps=after_sends())
    dq, dk, dv, ddrow = attn_bwd(q, k_bf, v_bf, dcol, drow, lse, do)
    send_grad("w_b_q", mm_tn("d_b_q_w", u1, dq, 1))
    du1 = mm_nt("d_b_q_x", dq, weight("w_b_q"), deps=after_sends())
    dfl_t, d_b_f = decay_bwd(fl_t, b_f_col, ddrow.reshape(nh, t))
    dhk = jnp.concatenate([dk.astype(BF16), dv.astype(BF16), dfl_t.T.astype(BF16)], axis=1)
    dhk = dhk.reshape(t, N_DEV, -1).transpose(1, 0, 2)
    send_grad("w_kvf", mm_tn("d_kvf_w", u_kv, dhk, N_DEV))
    du_kv = mm_nt("d_kvf_x", dhk, weight("w_kvf"), deps=after_sends())
    second_stage(du_kv)
    (dh, dgpre, dpu), (d_kv_norm, d_mix1) = rowwise(
        "d_ple_norms", _two_norms_ple_bwd, [h3, du_kv, du1, dh, s0["gpre"], s0["pu"]],
        [row(kv_norm), row(mix_norm[1])])
    dh, dh_bf, d_ple0, d_mlp0 = mlp_ple_bwd(0, s0, dh, dgpre, dpu)
    send_grad("w_a_out", mm_tn("d_a_out_w", og, dh_bf, 1))
    dog = mm_nt("d_a_out_x", dh_bf, weight("w_a_out"), deps=after_sends())
    dz4, d_lgt, d_hg = hgrn_bwd(z, lgt, a_head_gain, states, dog)
    send_grad("w_a_in", mm_tn("d_a_in_w", u0, dz4, N_DEV, stacked=True))
    du0 = mm_nt("d_a_in_x", dz4, weight("w_a_in"), deps=after_sends(), stacked=True)
    second_stage(du0)
    (dx, _), (d_mix0,) = rowwise("d_norm_mix0", _norm_bwd, [x2, du0, dh], [row(mix_norm[0])])

    new = {}
    last = [dx]

    def update(name, parts, w, m, v):
        shp = w.shape
        w3, m3, v3 = (a.reshape(len(parts), -1, shp[-1]) for a in (w, m, v))
        new[name] = tuple(a.reshape(shp) for a in adamw_reduce("adamw_" + name, parts, w3, m3, v3))
        last[0] = new[name][0]

    def receive_update(name, layers, w, m, v):
        parts = {}
        for sfx in layers:
            if name + sfx in two_level:
                mine, land = chip_wait(f"rs_wait_{name}{sfx}", sent[name + sfx], last[0])
                slot = me // 2
            else:
                mine, land = exchange_wait(f"rs_wait_{name}{sfx}", sent[name + sfx], last[0], False)
                slot = me
            parts[sfx] = lax.dynamic_update_slice(land, lax.dynamic_slice_in_dim(mine, slot, 1, 0), (slot, 0, 0))
        update(name, [parts[sfx] for sfx in sorted(layers)], w, m, v)

    both = ("1", "0")
    receive_update("w_b_out", ("",), w_b_out, m_w_b_out, v_w_b_out)
    receive_update("w_b_q", ("",), w_b_q, m_w_b_q, v_w_b_q)
    receive_update("w_kvf", ("",), w_kvf, m_w_kvf, v_w_kvf)
    receive_update("w_ple_gate", both, w_ple_gate, m_w_ple_gate, v_w_ple_gate)
    receive_update("w_ple_up", both, w_ple_up, m_w_ple_up, v_w_ple_up)
    receive_update("w_mlp_down", both, w_mlp_down, m_w_mlp_down, v_w_mlp_down)
    receive_update("w_mlp_up", both, w_mlp_up, m_w_mlp_up, v_w_mlp_up)
    receive_update("w_a_out", ("",), w_a_out, m_w_a_out, v_w_a_out)
    receive_update("w_a_in", ("",), w_a_in, m_w_a_in, v_w_a_in)

    small = dict(mix_norm=jnp.concatenate([d_mix0, d_mix1]), mlp_norm=jnp.concatenate([d_mlp0, d_mlp1]),
                 ple_norm=jnp.concatenate([d_ple0, d_ple1]), a_head_gain=d_hg, kv_norm=d_kv_norm.reshape(d),
                 b_f=d_b_f.reshape(nh), final_norm=d_final.reshape(d))
    small_w = dict(mix_norm=(mix_norm, m_mix_norm, v_mix_norm), mlp_norm=(mlp_norm, m_mlp_norm, v_mlp_norm),
                   ple_norm=(ple_norm, m_ple_norm, v_ple_norm),
                   a_head_gain=(a_head_gain, m_a_head_gain, v_a_head_gain), kv_norm=(kv_norm, m_kv_norm, v_kv_norm),
                   b_f=(b_f, m_b_f, v_b_f), final_norm=(final_norm, m_final_norm, v_final_norm))
    names = list(small)
    packed = _pack_rows([d_lgt] + [small[n] for n in names])
    everyone = all_gather("ag_small_grads", packed, deps=(last[0],))
    n_lgt_rows = d_lgt.size // 128
    lgt_parts = everyone[:, :n_lgt_rows].reshape(N_DEV, 2, d)
    lgt_parts = lax.dynamic_slice_in_dim(lgt_parts, me * a_lb_logits.shape[1], a_lb_logits.shape[1], axis=2)
    update("a_lb_logits", [lgt_parts], a_lb_logits, m_a_lb_logits, v_a_lb_logits)
    rest = everyone[:, n_lgt_rows:]
    like = [small_w[n][0] for n in names]
    packed_w, packed_m, packed_v = (_pack_rows([small_w[n][j] for n in names], rest.shape[1])[None] for j in range(3))
    outs = adamw_reduce("adamw_small", [rest], packed_w, packed_m, packed_v)
    unpacked = [_unpack_rows(a, like) for a in outs]
    for j, n in enumerate(names):
        new[n] = tuple(unpacked[q][j] for q in range(4))

    order = ["mix_norm", "mlp_norm", "ple_norm", "w_a_in", "a_lb_logits", "a_head_gain", "w_a_out", "kv_norm",
             "w_kvf", "b_f", "w_b_q", "w_b_out", "w_mlp_up", "w_mlp_down", "w_ple_gate", "w_ple_up", "final_norm"]
    loss_here, _ = lax.optimization_barrier((loss_rows[0, 0], new["final_norm"][0]))
    loss = lax.psum(loss_here, MESH_AXES)
    result = [loss, dx.reshape(x.shape)]
    for j in range(4):
        result += [new[n][j] for n in order]
    return tuple(result)
```

```python
import jax
import jax.numpy as jnp
from jax import lax
from jax.experimental import pallas as pl
from jax.experimental.pallas import tpu as pltpu

F32 = jnp.float32
BF16 = jnp.bfloat16
HEAD_DIM = 128
CHUNK = 16
TILE = 128
HEADS_PER_STEP = 4
NORM_EPS = 1e-6
N_DEV = 8
MESH_AXES = ("x", "y", "c")
VMEM_LIMIT_BYTES = 48 * 1024 * 1024
ROW_TILE_BYTES = 2 * 1024 * 1024
LR, B1, B2, ADAM_EPS, WD, STEP = 0.001, 0.9, 0.999, 1e-08, 0.01, 10
NEG_BIG = -1e30

NN = (((1,), (0,)), ((), ()))
NT = (((1,), (1,)), ((), ()))
TN = (((0,), (0,)), ((), ()))


def _params(semantics):
    return pltpu.CompilerParams(dimension_semantics=semantics, vmem_limit_bytes=VMEM_LIMIT_BYTES)


def _tile(n, prefs):
    for p in prefs:
        if n % p == 0:
            return p
    return n


def _row_tile(rows, limit):
    for cand in (2048, 1024, 512, 256, 128, 64, 32, 16):
        if cand <= limit and rows % cand == 0:
            return cand
    return rows


def _mm_call(name, a, b, dims, grid, a_spec, b_spec, o_spec, o_shape, acc_shape, k_axes, out_dtype, deps=(),
             fuse=None, split=1):
    nk = 1
    for ax in k_axes:
        nk *= grid[ax]
    fn, extra, out_dtypes = fuse if fuse else (lambda acc: (acc,), (), (out_dtype,))
    n_extra, n_out = len(extra), len(out_dtypes)

    def finish(acc, rest):
        o_refs = rest[n_extra + len(deps):n_extra + len(deps) + n_out]
        for ref, val in zip(o_refs, fn(acc, *[r[...] for r in rest[:n_extra]])):
            ref[...] = val.astype(ref.dtype)

    def product(a_ref, b_ref):
        if split == 1:
            return lax.dot_general(a_ref[...], b_ref[...], dims, preferred_element_type=F32)
        wide = a_ref.shape[1] // split
        return sum(lax.dot_general(a_ref[:, q * wide:(q + 1) * wide], b_ref[q], dims, preferred_element_type=F32)
                   for q in range(split))

    def one_step(a_ref, b_ref, *rest):
        finish(product(a_ref, b_ref), rest)

    def accumulate(a_ref, b_ref, *rest):
        acc_ref = rest[-1]
        k = 0
        for ax in k_axes:
            k = k * grid[ax] + pl.program_id(ax)
        part = product(a_ref, b_ref)

        @pl.when(k == 0)
        def _():
            acc_ref[...] = part

        @pl.when((k > 0) & (k < nk - 1))
        def _():
            acc_ref[...] += part

        @pl.when(k == nk - 1)
        def _():
            finish(acc_ref[...] + part, rest)

    sem = tuple("arbitrary" if ax in k_axes else "parallel" for ax in range(len(grid)))
    outs = pl.pallas_call(
        one_step if nk == 1 else accumulate, name=name, grid=grid,
        in_specs=[a_spec, b_spec] + [o_spec] * n_extra + [pl.BlockSpec(memory_space=pl.ANY)] * len(deps),
        out_specs=[o_spec] * n_out, out_shape=[jax.ShapeDtypeStruct(o_shape, dt) for dt in out_dtypes],
        scratch_shapes=[] if nk == 1 else [pltpu.VMEM(acc_shape, F32)], compiler_params=_params(sem),
    )(a, b, *extra, *deps)
    return outs if fuse else outs[0]


def mm_nn(name, a, b3, out_dtype=F32, out3=False, deps=(), fuse=None):
    m, k = a.shape
    g, _, n = b3.shape
    tm, tk = _tile(m, (1024, 512, 256)), _tile(k, (2048, 1024, 512, 256))
    tn = n if out3 else _tile(n, (1024, 512, 256, 128))
    nj = n // tn
    grid = (m // tm, g, nj, k // tk)
    a_spec = pl.BlockSpec((tm, tk), lambda i, gg, j, kk: (i, kk))
    b_spec = pl.BlockSpec((None, tk, tn), lambda i, gg, j, kk: (gg, kk, j))
    if out3:
        o_spec = pl.BlockSpec((None, tm, tn), lambda i, gg, j, kk: (gg, i, j))
        o_shape = (g, m, n)
    else:
        o_spec = pl.BlockSpec((tm, tn), lambda i, gg, j, kk: (i, gg * nj + j))
        o_shape = (m, g * n)
    return _mm_call(name, a, b3, NN, grid, a_spec, b_spec, o_spec, o_shape, (tm, tn), (3,), out_dtype, deps, fuse)


def mm_nt(name, a, b3, out_dtype=F32, deps=(), fuse=None, stacked=False):
    g, k, n = b3.shape
    a3 = a.ndim == 3 and not stacked
    m = a.shape[1] if a.ndim == 3 else a.shape[0]
    tm, tko = _tile(m, (1024, 512, 256)), _tile(k, (1024, 512, 256))
    tc = n if a3 else _tile(n, (2048, 1024, 512, 256, 128))
    nc = n // tc
    per = g // a.shape[0] if stacked else g
    pair = 2 if (not a3 and nc == 1 and per % 2 == 0 and tc <= 1024) else 1
    grid = (m // tm, k // tko, g // pair, nc)
    if a3:
        a_spec = pl.BlockSpec((None, tm, tc), lambda i, j, gg, c: (gg, i, c))
    elif stacked:
        a_spec = pl.BlockSpec((None, tm, pair * tc),
                              lambda i, j, gg, c: ((gg * pair) // per, i, (((gg * pair) % per) // pair) * nc + c))
    else:
        a_spec = pl.BlockSpec((tm, pair * tc), lambda i, j, gg, c: (i, gg * nc + c))
    if pair == 1:
        b_spec = pl.BlockSpec((None, tko, tc), lambda i, j, gg, c: (gg, j, c))
    else:
        b_spec = pl.BlockSpec((pair, tko, tc), lambda i, j, gg, c: (gg, j, c))
    o_spec = pl.BlockSpec((tm, tko), lambda i, j, gg, c: (i, j))
    return _mm_call(name, a, b3, NT, grid, a_spec, b_spec, o_spec, (m, k), (tm, tko), (2, 3), out_dtype, deps, fuse,
                    pair)


def mm_tn(name, a, b, g, out_dtype=BF16, deps=(), stacked=False):
    t, k = a.shape
    b3 = b.ndim == 3 and not stacked
    n = b.shape[2] if b3 else (b.shape[0] * b.shape[2] if stacked else b.shape[1]) // g
    tm = _tile(k, (1024, 512, 256))
    tn = n if b3 else _tile(n, (1024, 512, 256, 128))
    tt = _tile(t, (2048, 1024, 512, 256))
    nj = n // tn
    grid = (g, k // tm, nj, t // tt)
    a_spec = pl.BlockSpec((tt, tm), lambda gg, i, j, s: (s, i))
    if b3:
        b_spec = pl.BlockSpec((None, tt, tn), lambda gg, i, j, s: (gg, s, j))
    elif stacked:
        per = g // b.shape[0]
        b_spec = pl.BlockSpec((None, tt, tn), lambda gg, i, j, s: (gg // per, s, (gg % per) * nj + j))
    else:
        b_spec = pl.BlockSpec((tt, tn), lambda gg, i, j, s: (s, gg * nj + j))
    o_spec = pl.BlockSpec((None, tm, tn), lambda gg, i, j, s: (gg, i, j))
    return _mm_call(name, a, b, TN, grid, a_spec, b_spec, o_spec, (g, k, n), (tm, tn), (3,), out_dtype, deps)


def rowwise(name, fn, rows, vecs=()):
    t = rows[0].shape[0]
    wmax = max(r.shape[1] for r in rows)
    tr = _row_tile(t, ROW_TILE_BYTES // (4 * wmax))
    row_s = [jax.ShapeDtypeStruct((tr, r.shape[1]), r.dtype) for r in rows]
    vec_s = [jax.ShapeDtypeStruct(v.shape, v.dtype) for v in vecs]
    out_rows_s, out_sums_s = jax.eval_shape(fn, *row_s, *vec_s)
    n_in, n_r = len(rows) + len(vecs), len(out_rows_s)

    def body(*refs):
        i = pl.program_id(0)
        o_rows, o_sums = fn(*[r[...] for r in refs[:n_in]])
        for ref, val in zip(refs[n_in:n_in + n_r], o_rows):
            ref[...] = val

        if out_sums_s:
            @pl.when(i == 0)
            def _():
                for ref in refs[n_in + n_r:]:
                    ref[...] = jnp.zeros_like(ref)

            for ref, val in zip(refs[n_in + n_r:], o_sums):
                ref[...] += val

    in_specs = [pl.BlockSpec((tr, r.shape[1]), lambda i: (i, 0)) for r in rows]
    in_specs += [pl.BlockSpec(v.shape, lambda i: (0, 0)) for v in vecs]
    out_specs = [pl.BlockSpec((tr, s.shape[1]), lambda i: (i, 0)) for s in out_rows_s]
    out_specs += [pl.BlockSpec(s.shape, lambda i: (0, 0)) for s in out_sums_s]
    out_shape = [jax.ShapeDtypeStruct((t, s.shape[1]), s.dtype) for s in out_rows_s]
    out_shape += [jax.ShapeDtypeStruct(s.shape, s.dtype) for s in out_sums_s]
    outs = pl.pallas_call(
        body, name=name, grid=(t // tr,), in_specs=in_specs, out_specs=out_specs, out_shape=out_shape,
        compiler_params=_params(("arbitrary",)),
    )(*rows, *vecs)
    return outs[:n_r], outs[n_r:]


def _rms(x, gain):
    return x * lax.rsqrt(jnp.mean(x * x, axis=-1, keepdims=True) + NORM_EPS) * gain


def _norm_fwd(x, gain):
    return (_rms(x, gain).astype(BF16),), ()


def _add_norm_fwd(h, a, gain):
    h = h + a
    return (h, _rms(h, gain).astype(BF16)), ()


def _relu2(pre):
    r = jnp.maximum(pre, 0.0)
    return pre, r * r


def _ple(h, gpre, pu):
    return h + pu * jax.nn.sigmoid(gpre)


def _ple_two_norms_fwd(h, gpre, pu, gain_a, gain_b):
    h = _ple(h, gpre, pu)
    return (h, _rms(h, gain_a).astype(BF16), _rms(h, gain_b).astype(BF16)), ()


def _tail_fwd_bwd(h, gpre, pu, target, gain):
    def row_loss(h, gpre, pu, gain):
        y = _rms(_ple(h, gpre, pu), gain)
        return 0.5 * jnp.mean(jnp.square(y - target), axis=-1, keepdims=True)

    loss, vjp = jax.vjp(row_loss, h, gpre, pu, gain)
    dh, dgpre, dpu, dgain = vjp(jnp.ones_like(loss))
    loss = jnp.broadcast_to(jnp.sum(loss, axis=0, keepdims=True), (1, 128))
    return (dh, dgpre.astype(BF16), dpu.astype(BF16)), (dgain, loss)


def _norm_bwd(h, du, dh_in, gain):
    _, vjp = jax.vjp(_rms, h, gain)
    dh, dgain = vjp(du)
    dh = dh_in + dh
    return (dh, dh.astype(BF16)), (dgain,)


def _two_norms_ple_bwd(h, du_a, du_b, dh_in, gpre, pu, gain_a, gain_b):
    _, vjp = jax.vjp(lambda h, ga, gb: (_rms(h, ga), _rms(h, gb)), h, gain_a, gain_b)
    dh, dga, dgb = vjp((du_a, du_b))
    dh = dh_in + dh
    _, gate_vjp = jax.vjp(lambda g, u: u * jax.nn.sigmoid(g), gpre, pu)
    dgpre, dpu = gate_vjp(dh)
    return (dh, dgpre.astype(BF16), dpu.astype(BF16)), (dga, dgb)


def _relu2_bwd(dact, pre):
    return (dact * 2.0 * jnp.maximum(pre.astype(F32), 0.0),)


def _bf16_dot(dims_fwd, dims_da, dims_db, swap_da, swap_db):
    @jax.custom_vjp
    def dot(a, b):
        return lax.dot_general(a.astype(BF16), b.astype(BF16), dims_fwd, preferred_element_type=F32)

    def fwd(a, b):
        return dot(a, b), (a, b)

    def bwd(res, ct):
        a, b = res
        ct, a, b = ct.astype(BF16), a.astype(BF16), b.astype(BF16)
        da = lax.dot_general(*((b, ct) if swap_da else (ct, b)), dims_da, preferred_element_type=F32)
        db = lax.dot_general(*((ct, a) if swap_db else (a, ct)), dims_db, preferred_element_type=F32)
        return da, db

    dot.defvjp(fwd, bwd)
    return dot


_dot_nn = _bf16_dot(NN, NT, TN, False, False)
_dot_nt = _bf16_dot(NT, NN, TN, False, True)
_dot_tn = _bf16_dot(TN, NT, NN, True, False)


def _chunk_causal_mask():
    r = lax.broadcasted_iota(jnp.int32, (TILE, TILE), 0)
    c = lax.broadcasted_iota(jnp.int32, (TILE, TILE), 1)
    return ((r // CHUNK) == (c // CHUNK)) & (c <= r)


def _chunk_scan(x, reverse):
    pos = lax.broadcasted_iota(jnp.int32, x.shape, 0) % CHUNK
    step = 1
    while step < CHUNK:
        if reverse:
            x = x + jnp.where(pos < CHUNK - step, pltpu.roll(x, x.shape[0] - step, axis=0), 0.0)
        else:
            x = x + jnp.where(pos >= step, pltpu.roll(x, step, axis=0), 0.0)
        step *= 2
    return x


def _chunk_total(x):
    return _chunk_scan(x, False) + _chunk_scan(x, True) - x


@jax.custom_vjp
def _chunk_sums(x):
    return _chunk_scan(x, False), _chunk_total(x)


def _chunk_sums_fwd(x):
    return _chunk_sums(x), None


def _chunk_sums_bwd(_, ct):
    return (_chunk_scan(ct[0], True) + _chunk_total(ct[1]),)


_chunk_sums.defvjp(_chunk_sums_fwd, _chunk_sums_bwd)


def _hgrn_tile(q, f, i, g, lgt, hg, st):
    d = q.shape[1]
    l0, l1 = lgt[0:1], lgt[1:2]
    mx = jnp.maximum(l0, l1)
    e0, e1 = jnp.exp(l0 - mx), jnp.exp(l1 - mx)
    lb = e0 / (e0 + e1)
    fg = lb + (1.0 - lb) * jax.nn.sigmoid(f)
    k = 1.0 - fg
    causal = _chunk_causal_mask()
    b, b_last = _chunk_sums(jnp.log(fg))
    q_in = q * jax.nn.sigmoid(q) * (d ** -0.5) * jnp.exp(b)
    k_in = k * jnp.exp(-b)
    k_end = k * jnp.exp(b_last - b)
    att = jnp.where(causal, _dot_nt(q_in, k_in), 0.0)
    o_intra = _dot_nn(att, i)
    n_chunks = TILE // CHUNK
    chunk_of_row = lax.broadcasted_iota(jnp.int32, (TILE, 1), 0) // CHUNK

    def spread(a):
        return jnp.concatenate([jnp.where(chunk_of_row == n, a, 0.0) for n in range(n_chunks)], axis=1)

    increments = _dot_tn(i, spread(k_end))
    states = []
    for n in range(n_chunks):
        states.append(st)
        decay = jnp.exp(jnp.mean(b_last[n * CHUNK:(n + 1) * CHUNK], axis=0, keepdims=True))
        st = st * decay + increments[:, n * d:(n + 1) * d]
    o = o_intra + _dot_nt(spread(q_in), jnp.concatenate(states, axis=1))
    o = o * lax.rsqrt(jnp.mean(o * o, axis=-1, keepdims=True) + NORM_EPS) * hg
    return o * (g * jax.nn.sigmoid(g)), st


def hgrn_fwd(z, lgt, hg):
    t, d4 = z.shape
    d = d4 // 4
    nh, nt = d // HEAD_DIM, t // TILE
    hp = HEADS_PER_STEP
    wide = hp * HEAD_DIM

    def body(q_ref, f_ref, i_ref, g_ref, lgt_ref, hg_ref, o_ref, st_out_ref, st_ref):
        tt = pl.program_id(1)

        @pl.when(tt == 0)
        def _():
            st_ref[...] = jnp.zeros_like(st_ref)

        for hh in range(hp):
            cols = slice(hh * HEAD_DIM, (hh + 1) * HEAD_DIM)
            st = st_ref[hh]
            st_out_ref[hh] = st
            o, st = _hgrn_tile(q_ref[:, cols], f_ref[:, cols], i_ref[:, cols], g_ref[:, cols], lgt_ref[:, cols],
                               hg_ref[...], st)
            o_ref[:, cols] = o.astype(o_ref.dtype)
            st_ref[hh] = st

    def part(p):
        return pl.BlockSpec((TILE, wide), lambda h, tt: (tt, p * (nh // hp) + h))

    return pl.pallas_call(
        body, name="hgrn_fwd", grid=(nh // hp, nt),
        in_specs=[part(0), part(1), part(2), part(3),
                  pl.BlockSpec((2, wide), lambda h, tt: (0, h)),
                  pl.BlockSpec((1, HEAD_DIM), lambda h, tt: (0, 0))],
        out_specs=[pl.BlockSpec((TILE, wide), lambda h, tt: (tt, h)),
                   pl.BlockSpec((hp, None, HEAD_DIM, HEAD_DIM), lambda h, tt: (h, tt, 0, 0))],
        out_shape=[jax.ShapeDtypeStruct((t, d), BF16),
                   jax.ShapeDtypeStruct((nh, nt, HEAD_DIM, HEAD_DIM), F32)],
        scratch_shapes=[pltpu.VMEM((hp, HEAD_DIM, HEAD_DIM), F32)],
        compiler_params=_params(("parallel", "arbitrary")),
    )(z, z, z, z, lgt, hg)


def hgrn_bwd(z, lgt, hg, states, dout):
    t, d4 = z.shape
    d = d4 // 4
    nh, nt = d // HEAD_DIM, t // TILE
    hp = HEADS_PER_STEP
    wide = hp * HEAD_DIM

    def body(q_ref, f_ref, i_ref, g_ref, lgt_ref, hg_ref, st_in_ref, do_ref, dz_ref, dlgt_ref, dhg_ref, dst_ref):
        h, tt = pl.program_id(0), pl.program_id(1)

        @pl.when(tt == 0)
        def _():
            dst_ref[...] = jnp.zeros_like(dst_ref)
            dlgt_ref[...] = jnp.zeros_like(dlgt_ref)

        @pl.when((tt == 0) & (h == 0))
        def _():
            dhg_ref[...] = jnp.zeros_like(dhg_ref)

        for hh in range(hp):
            cols = slice(hh * HEAD_DIM, (hh + 1) * HEAD_DIM)
            _, vjp = jax.vjp(_hgrn_tile, q_ref[:, cols], f_ref[:, cols], i_ref[:, cols], g_ref[:, cols],
                             lgt_ref[:, cols], hg_ref[...], st_in_ref[hh])
            grads = vjp((do_ref[:, cols], dst_ref[hh]))
            for p in range(4):
                dz_ref[p, :, cols] = grads[p].astype(dz_ref.dtype)
            dlgt_ref[:, cols] += grads[4]
            dhg_ref[...] += grads[5]
            dst_ref[hh] = grads[6]

    def part(p):
        return pl.BlockSpec((TILE, wide), lambda h, tt: (nt - 1 - tt, p * (nh // hp) + h))

    return pl.pallas_call(
        body, name="hgrn_bwd", grid=(nh // hp, nt),
        in_specs=[part(0), part(1), part(2), part(3),
                  pl.BlockSpec((2, wide), lambda h, tt: (0, h)),
                  pl.BlockSpec((1, HEAD_DIM), lambda h, tt: (0, 0)),
                  pl.BlockSpec((hp, None, HEAD_DIM, HEAD_DIM), lambda h, tt: (h, nt - 1 - tt, 0, 0)),
                  pl.BlockSpec((TILE, wide), lambda h, tt: (nt - 1 - tt, h))],
        out_specs=[pl.BlockSpec((4, TILE, wide), lambda h, tt: (0, nt - 1 - tt, h)),
                   pl.BlockSpec((2, wide), lambda h, tt: (0, h)),
                   pl.BlockSpec((1, HEAD_DIM), lambda h, tt: (0, 0))],
        out_shape=[jax.ShapeDtypeStruct((4, t, d), BF16),
                   jax.ShapeDtypeStruct((2, d), F32),
                   jax.ShapeDtypeStruct((1, HEAD_DIM), F32)],
        scratch_shapes=[pltpu.VMEM((hp, HEAD_DIM, HEAD_DIM), F32)],
        compiler_params=_params(("arbitrary", "arbitrary")),
    )(z, z, z, z, lgt, hg, states, dout)


def _log_sigmoid(x):
    return jnp.minimum(x, 0.0) - jnp.log(1.0 + jnp.exp(-jnp.abs(x)))


def decay_fwd(fl_t, b_f):
    nh, t = fl_t.shape

    def body(fl_ref, b_ref, out_ref):
        r = lax.broadcasted_iota(jnp.int32, (128, 128), 0)
        c = lax.broadcasted_iota(jnp.int32, (128, 128), 1)
        upper = (r <= c).astype(F32)
        carry = jnp.zeros((nh, 1), F32)
        for j in range(t // 128):
            cols = slice(j * 128, (j + 1) * 128)
            ls = _log_sigmoid(fl_ref[:, cols] + b_ref[...])
            out_ref[:, cols] = carry + jnp.dot(ls, upper, precision=lax.Precision.HIGHEST,
                                               preferred_element_type=F32)
            carry = carry + jnp.sum(ls, axis=1, keepdims=True)

    return pl.pallas_call(body, name="decay_fwd", out_shape=jax.ShapeDtypeStruct((nh, t), F32),
                          compiler_params=_params(None))(fl_t, b_f)


def decay_bwd(fl_t, b_f, ddcum):
    nh, t = fl_t.shape

    def body(fl_ref, b_ref, dd_ref, dfl_ref, db_ref):
        r = lax.broadcasted_iota(jnp.int32, (128, 128), 0)
        c = lax.broadcasted_iota(jnp.int32, (128, 128), 1)
        lower = (r >= c).astype(F32)
        carry = jnp.zeros((nh, 1), F32)
        db = jnp.zeros((nh, 1), F32)
        for j in reversed(range(t // 128)):
            cols = slice(j * 128, (j + 1) * 128)
            dd = dd_ref[:, cols]
            dls = carry + jnp.dot(dd, lower, precision=lax.Precision.HIGHEST, preferred_element_type=F32)
            carry = carry + jnp.sum(dd, axis=1, keepdims=True)
            dfl = dls * jax.nn.sigmoid(-(fl_ref[:, cols] + b_ref[...]))
            dfl_ref[:, cols] = dfl
            db = db + jnp.sum(dfl, axis=1, keepdims=True)
        db_ref[...] = db

    return pl.pallas_call(body, name="decay_bwd",
                          out_shape=[jax.ShapeDtypeStruct((nh, t), F32), jax.ShapeDtypeStruct((nh, 1), F32)],
                          compiler_params=_params(None))(fl_t, b_f, ddcum)


def _attn_parts(t):
    tq = _tile(t, (256, 128))
    per_part = 2 if t // tq >= 4 else 1
    return tq, [(first, per_part, (first + per_part) * tq) for first in range(0, t // tq, per_part)]


def _attn_logits(q_ref, k_ref, dcol_ref, drow_ref, row0, tq, keys):
    qs = (q_ref[...] * (HEAD_DIM ** -0.5)).astype(BF16)
    s = lax.dot_general(qs, k_ref[...], NT, preferred_element_type=F32)
    s = s + dcol_ref[...] - drow_ref[...]
    row = row0 + lax.broadcasted_iota(jnp.int32, (tq, keys), 0)
    col = lax.broadcasted_iota(jnp.int32, (tq, keys), 1)
    return qs, jnp.where(col <= row, s, NEG_BIG)


def attn_fwd(q, k, v, dcol, drow):
    t, d = q.shape
    nh = d // HEAD_DIM
    tq, parts = _attn_parts(t)
    o = lse = None
    for first, count, keys in parts:
        def body(q_ref, k_ref, v_ref, dcol_ref, drow_ref, *rest, first=first, keys=keys):
            o_ref, lse_ref = rest[-2:]
            _, s = _attn_logits(q_ref, k_ref, dcol_ref, drow_ref, (first + pl.program_id(1)) * tq, tq, keys)
            m = jnp.max(s, axis=1, keepdims=True)
            p = jnp.exp(s - m)
            l = jnp.sum(p, axis=1, keepdims=True)
            acc = jnp.dot(p.astype(BF16), v_ref[...], preferred_element_type=F32)
            o_ref[...] = (acc / l).astype(o_ref.dtype)
            lse_ref[...] = m + jnp.log(l)

        tile = pl.BlockSpec((tq, HEAD_DIM), lambda h, i, first=first: (first + i, h))
        col = pl.BlockSpec((None, tq, 1), lambda h, i, first=first: (h, first + i, 0))
        seen = pl.BlockSpec((keys, HEAD_DIM), lambda h, i: (0, h))
        carried = [] if o is None else [o, lse]
        o, lse = pl.pallas_call(
            body, name=f"attn_fwd_{first}", grid=(nh, count),
            in_specs=[tile, seen, seen, col, pl.BlockSpec((None, 1, keys), lambda h, i: (h, 0, 0))]
            + [pl.BlockSpec(memory_space=pl.ANY)] * len(carried),
            out_specs=[tile, col],
            out_shape=[jax.ShapeDtypeStruct((t, d), BF16), jax.ShapeDtypeStruct((nh, t, 1), F32)],
            input_output_aliases={5: 0, 6: 1} if carried else {},
            compiler_params=_params(("parallel", "parallel")),
        )(q, k, v, dcol, drow, *carried)
    return o, lse


def attn_bwd(q, k, v, dcol, drow, lse, do):
    t, d = q.shape
    nh = d // HEAD_DIM
    tq, parts = _attn_parts(t)
    dq = dk = dv = ddrow = None
    for first, count, keys in reversed(parts):
        first_call = dq is None

        def body(q_ref, k_ref, v_ref, dcol_ref, drow_ref, lse_ref, do_ref, *rest, first=first, keys=keys,
                 count=count, first_call=first_call):
            dq_ref, dk_ref, dv_ref, ddrow_ref, dk_acc, dv_acc, ddrow_acc = rest[-7:]
            i = pl.program_id(1)

            @pl.when(i == 0)
            def _():
                if first_call:
                    dk_acc[...] = jnp.zeros_like(dk_acc)
                    dv_acc[...] = jnp.zeros_like(dv_acc)
                    ddrow_acc[...] = jnp.zeros_like(ddrow_acc)
                else:
                    dk_acc[...] = rest[1][...]
                    dv_acc[...] = rest[2][...]
                    ddrow_acc[...] = rest[3][...]

            qs, s = _attn_logits(q_ref, k_ref, dcol_ref, drow_ref, (first + i) * tq, tq, keys)
            p = jnp.exp(s - lse_ref[...])
            do = do_ref[...]
            dp = lax.dot_general(do, v_ref[...], NT, preferred_element_type=F32)
            ds = p * (dp - jnp.sum(p * dp, axis=1, keepdims=True))
            dsb = ds.astype(BF16)
            dq_ref[...] = (jnp.dot(dsb, k_ref[...], preferred_element_type=F32) * (HEAD_DIM ** -0.5)).astype(dq_ref.dtype)
            dk_acc[...] += lax.dot_general(dsb, qs, TN, preferred_element_type=F32)
            dv_acc[...] += lax.dot_general(p.astype(BF16), do, TN, preferred_element_type=F32)
            ddrow_acc[...] -= jnp.sum(ds, axis=0, keepdims=True)

            @pl.when(i == count - 1)
            def _():
                dk_ref[...] = dk_acc[...]
                dv_ref[...] = dv_acc[...]
                ddrow_ref[...] = ddrow_acc[...]

        tile = pl.BlockSpec((tq, HEAD_DIM), lambda h, i, first=first: (first + i, h))
        col = pl.BlockSpec((None, tq, 1), lambda h, i, first=first: (h, first + i, 0))
        seen = pl.BlockSpec((keys, HEAD_DIM), lambda h, i: (0, h))
        seen_row = pl.BlockSpec((None, 1, keys), lambda h, i: (h, 0, 0))
        carried = [] if first_call else [dq, dk, dv, ddrow]
        carried_specs = [] if first_call else [pl.BlockSpec(memory_space=pl.ANY), seen, seen, seen_row]
        dq, dk, dv, ddrow = pl.pallas_call(
            body, name=f"attn_bwd_{first}", grid=(nh, count),
            in_specs=[tile, seen, seen, col, seen_row, col, tile] + carried_specs,
            out_specs=[tile, seen, seen, seen_row],
            out_shape=[jax.ShapeDtypeStruct((t, d), BF16), jax.ShapeDtypeStruct((t, d), F32),
                       jax.ShapeDtypeStruct((t, d), F32), jax.ShapeDtypeStruct((nh, 1, t), F32)],
            scratch_shapes=[pltpu.VMEM((keys, HEAD_DIM), F32), pltpu.VMEM((keys, HEAD_DIM), F32),
                            pltpu.VMEM((1, keys), F32)],
            input_output_aliases={} if first_call else {7: 0, 8: 1, 9: 2, 10: 3},
            compiler_params=_params(("parallel", "arbitrary")),
        )(q, k, v, dcol, drow, lse, do, *carried)
    return dq, dk, dv, ddrow


def _my_index():
    return (lax.axis_index("x") * 2 + lax.axis_index("y")) * 2 + lax.axis_index("c")


def all_gather(name, src, deps=()):
    def body(src_ref, *rest):
        out_ref, send_sems, recv_sems, local_sem = rest[len(deps):]
        x, y, c = (lax.axis_index(a) for a in MESH_AXES)
        me = (x * 2 + y) * 2 + c
        local = pltpu.make_async_copy(src_ref, out_ref.at[me], local_sem)
        local.start()
        copies = []
        for dlt in range(1, N_DEV):
            copies.append(pltpu.make_async_remote_copy(
                src_ref=src_ref, dst_ref=out_ref.at[me], send_sem=send_sems.at[dlt - 1],
                recv_sem=recv_sems.at[dlt - 1], device_id=(x ^ (dlt // 4), y ^ ((dlt // 2) % 2), c ^ (dlt % 2)),
                device_id_type=pl.DeviceIdType.MESH))
        for cp in copies:
            cp.start()
        for cp in copies:
            cp.wait_recv()
        for cp in copies:
            cp.wait_send()
        local.wait()

    return pl.pallas_call(
        body, name=name, out_shape=jax.ShapeDtypeStruct((N_DEV,) + tuple(src.shape), src.dtype),
        in_specs=[pl.BlockSpec(memory_space=pl.ANY)] * (1 + len(deps)), out_specs=pl.BlockSpec(memory_space=pl.ANY),
        scratch_shapes=[pltpu.SemaphoreType.DMA((N_DEV - 1,)), pltpu.SemaphoreType.DMA((N_DEV - 1,)),
                        pltpu.SemaphoreType.DMA],
        compiler_params=pltpu.CompilerParams(has_side_effects=True),
    )(src, *deps)


_HBM = pl.BlockSpec(memory_space=pltpu.HBM)
_SEM = pl.BlockSpec(memory_space=pltpu.SEMAPHORE)
_DATAFLOW = pltpu.SideEffectType.DATAFLOW_SIDE_EFFECTING


def _peer_copies(src_ref, land_ref, send_sems, recv_sems):
    x, y, c = (lax.axis_index(a) for a in MESH_AXES)
    me = (x * 2 + y) * 2 + c
    copies = []
    for dlt in range(1, N_DEV):
        px, py, pc = x ^ (dlt // 4), y ^ ((dlt // 2) % 2), c ^ (dlt % 2)
        peer = (px * 2 + py) * 2 + pc
        copies.append(pltpu.make_async_remote_copy(
            src_ref=src_ref.at[peer], dst_ref=land_ref.at[me],
            send_sem=send_sems.at[dlt - 1], recv_sem=recv_sems.at[dlt - 1],
            device_id=(px, py, pc), device_id_type=pl.DeviceIdType.MESH))
    return copies


def scatter_start(name, srcs):
    n = len(srcs)
    lands = [lax.empty(s.shape, s.dtype) for s in srcs]

    def body(*refs):
        src_refs, land_refs = refs[:n], refs[n:2 * n]
        send_sems, recv_sems = refs[2 * n:3 * n], refs[3 * n:4 * n]
        token = refs[-1]
        for j in range(n):
            for cp in _peer_copies(src_refs[j], land_refs[j], send_sems[j], recv_sems[j]):
                cp.start()
        token[...] = jnp.zeros_like(token)

    sems = [pltpu.SemaphoreType.DMA((N_DEV - 1,))] * (2 * n)
    thru = [pltpu.HBM(a.shape, a.dtype) for a in list(srcs) + lands]
    outs = pl.pallas_call(
        body, name=name, out_shape=tuple(sems + thru + [jax.ShapeDtypeStruct((8, 128), F32)]),
        in_specs=[_HBM] * (2 * n), out_specs=tuple([_SEM] * (2 * n) + [_HBM] * (2 * n) + [pl.BlockSpec(memory_space=pltpu.VMEM)]),
        input_output_aliases={j: 2 * n + j for j in range(2 * n)},
        compiler_params=pltpu.CompilerParams(has_side_effects=_DATAFLOW),
    )(*[pltpu.with_memory_space_constraint(a, pltpu.HBM) for a in list(srcs) + lands])
    handles = [(outs[j], outs[n + j], outs[2 * n + j], outs[3 * n + j]) for j in range(n)]
    return handles, outs[-1]


def scatter_wait(name, handle, after):
    send_sems, recv_sems, src, land = handle

    def body(src_ref, land_ref, send_ref, recv_ref, after_ref, src_out, land_out):
        for cp in _peer_copies(src_ref, land_ref, send_ref, recv_ref):
            cp.wait_send()
            cp.wait_recv()

    return pl.pallas_call(
        body, name=name, out_shape=(pltpu.HBM(src.shape, src.dtype), pltpu.HBM(land.shape, land.dtype)),
        in_specs=[_HBM, _HBM, _SEM, _SEM, pl.BlockSpec(memory_space=pl.ANY)], out_specs=(_HBM, _HBM),
        input_output_aliases={0: 0, 1: 1},
        compiler_params=pltpu.CompilerParams(has_side_effects=_DATAFLOW),
    )(src, land, send_sems, recv_sems, after)


N_OTHER_CHIPS = 3


def _two_level_places():
    x, y, c = (lax.axis_index(a) for a in MESH_AXES)
    return (x, y, c), (x * 2 + y) * 2 + c, (x, y, 1 - c), [(1 - x, y), (x, 1 - y), (1 - x, 1 - y)]


def _first_copies(land_ref, send_sems, recv_sems):
    (x, y, c), me, other_core, chips = _two_level_places()
    targets = [other_core] + [(cx, cy, c) for cx, cy in chips]
    return [pltpu.make_async_remote_copy(
        src_ref=land_ref.at[me], dst_ref=land_ref.at[me], send_sem=send_sems.at[k], recv_sem=recv_sems.at[k],
        device_id=to, device_id_type=pl.DeviceIdType.MESH) for k, to in enumerate(targets)]


def _passed_on_copies(land_ref, send_sems, recv_sems):
    (x, y, c), me, other_core, chips = _two_level_places()
    copies = []
    for k, (cx, cy) in enumerate(chips):
        slot = land_ref.at[(cx * 2 + cy) * 2 + c]
        copies.append(pltpu.make_async_remote_copy(
            src_ref=slot, dst_ref=slot, send_sem=send_sems.at[k], recv_sem=recv_sems.at[k],
            device_id=other_core, device_id_type=pl.DeviceIdType.MESH))
    return copies


def gather_start(name, lands):
    n = len(lands)

    def body(*refs):
        land_refs, send_sems, recv_sems = refs[:n], refs[n:2 * n], refs[2 * n:3 * n]
        for j in range(n):
            for cp in _first_copies(land_refs[j], send_sems[j], recv_sems[j]):
                cp.start()

    sems = [pltpu.SemaphoreType.DMA((1 + N_OTHER_CHIPS,))] * (2 * n)
    outs = pl.pallas_call(
        body, name=name, out_shape=tuple(sems + [pltpu.HBM(a.shape, a.dtype) for a in lands]),
        in_specs=[_HBM] * n, out_specs=tuple([_SEM] * (2 * n) + [_HBM] * n),
        input_output_aliases={j: 2 * n + j for j in range(n)},
        compiler_params=pltpu.CompilerParams(has_side_effects=_DATAFLOW),
    )(*[pltpu.with_memory_space_constraint(a, pltpu.HBM) for a in lands])
    return [[outs[j], outs[n + j], outs[2 * n + j]] for j in range(n)]


def gather_pass_on(name, handle, after):
    send_sems, recv_sems, land = handle

    def body(land_ref, recv_ref, after_ref, land_out, send2, recv2, token):
        arrivals = _first_copies(land_ref, recv_ref, recv_ref)
        for k, cp in enumerate(_passed_on_copies(land_ref, send2, recv2)):
            arrivals[1 + k].wait_recv()
            cp.start()
        token[...] = jnp.zeros_like(token)

    sem3 = pltpu.SemaphoreType.DMA((N_OTHER_CHIPS,))
    land, send2, recv2, token = pl.pallas_call(
        body, name=name,
        out_shape=(pltpu.HBM(land.shape, land.dtype), sem3, sem3, jax.ShapeDtypeStruct((8, 128), F32)),
        in_specs=[_HBM, _SEM, pl.BlockSpec(memory_space=pl.ANY)],
        out_specs=(_HBM, _SEM, _SEM, pl.BlockSpec(memory_space=pltpu.VMEM)),
        input_output_aliases={0: 0}, compiler_params=pltpu.CompilerParams(has_side_effects=_DATAFLOW),
    )(land, recv_sems, after)
    return [send_sems, recv_sems, land, send2, recv2], token


def gather_wait(name, handle, after):
    send_sems, recv_sems, land, send2, recv2 = handle

    def body(land_ref, send_ref, recv_ref, send2_ref, recv2_ref, after_ref, land_out):
        first = _first_copies(land_ref, send_ref, recv_ref)
        for cp in first:
            cp.wait_send()
        first[0].wait_recv()
        for cp in _passed_on_copies(land_ref, send2_ref, recv2_ref):
            cp.wait_send()
            cp.wait_recv()

    return pl.pallas_call(
        body, name=name, out_shape=pltpu.HBM(land.shape, land.dtype),
        in_specs=[_HBM, _SEM, _SEM, _SEM, _SEM, pl.BlockSpec(memory_space=pl.ANY)], out_specs=_HBM,
        input_output_aliases={0: 0}, compiler_params=pltpu.CompilerParams(has_side_effects=_DATAFLOW),
    )(land, send_sems, recv_sems, send2, recv2, after)


N_CHIPS = 4


def _pair_copies(g_ref, half_ref, send_sems, recv_sems):
    (x, y, c), me, other_core, chips = _two_level_places()
    return [pltpu.make_async_remote_copy(
        src_ref=g_ref.at[chip * 2 + (1 - c)], dst_ref=half_ref.at[chip], send_sem=send_sems.at[chip],
        recv_sem=recv_sems.at[chip], device_id=other_core, device_id_type=pl.DeviceIdType.MESH)
        for chip in range(N_CHIPS)]


def pair_start(name, g):
    half = lax.empty((N_CHIPS,) + g.shape[1:], g.dtype)

    def body(g_ref, half_ref, send_sems, recv_sems, g_out, half_out, token):
        for cp in _pair_copies(g_ref, half_ref, send_sems, recv_sems):
            cp.start()
        token[...] = jnp.zeros_like(token)

    sem = pltpu.SemaphoreType.DMA((N_CHIPS,))
    outs = pl.pallas_call(
        body, name=name,
        out_shape=(sem, sem, pltpu.HBM(g.shape, g.dtype), pltpu.HBM(half.shape, half.dtype),
                   jax.ShapeDtypeStruct((8, 128), F32)),
        in_specs=[_HBM, _HBM], out_specs=(_SEM, _SEM, _HBM, _HBM, pl.BlockSpec(memory_space=pltpu.VMEM)),
        input_output_aliases={0: 2, 1: 3}, compiler_params=pltpu.CompilerParams(has_side_effects=_DATAFLOW),
    )(pltpu.with_memory_space_constraint(g, pltpu.HBM), half)
    return list(outs[:4]), outs[4]


def pair_wait(name, handle, after):
    send_sems, recv_sems, g, half = handle

    def body(g_ref, half_ref, send_ref, recv_ref, after_ref, g_out, half_out):
        for cp in _pair_copies(g_ref, half_ref, send_ref, recv_ref):
            cp.wait_send()
            cp.wait_recv()

    return pl.pallas_call(
        body, name=name, out_shape=(pltpu.HBM(g.shape, g.dtype), pltpu.HBM(half.shape, half.dtype)),
        in_specs=[_HBM, _HBM, _SEM, _SEM, pl.BlockSpec(memory_space=pl.ANY)], out_specs=(_HBM, _HBM),
        input_output_aliases={0: 0, 1: 1}, compiler_params=pltpu.CompilerParams(has_side_effects=_DATAFLOW),
    )(g, half, send_sems, recv_sems, after)


def pair_sum(name, g, half):
    _, r, wd = g.shape
    tr = _row_tile(r, 2 * ROW_TILE_BYTES // (4 * wd))
    kind = lax.axis_index("c").astype(jnp.int32).reshape(1)

    def body(kind_ref, g_ref, half_ref, o_ref):
        o_ref[...] = (g_ref[...].astype(F32) + half_ref[...].astype(F32)).astype(o_ref.dtype)

    spec = pl.BlockSpec((None, tr, wd), lambda chip, i, kind_ref: (chip, i, 0))
    return pl.pallas_call(
        body, name=name,
        grid_spec=pltpu.PrefetchScalarGridSpec(
            num_scalar_prefetch=1, grid=(N_CHIPS, r // tr),
            in_specs=[pl.BlockSpec((None, tr, wd), lambda chip, i, kind_ref: (chip * 2 + kind_ref[0], i, 0)), spec],
            out_specs=spec),
        out_shape=jax.ShapeDtypeStruct((N_CHIPS, r, wd), g.dtype),
        compiler_params=_params(("parallel", "parallel")),
    )(kind, g, half)


def _chip_copies(sums_ref, land_ref, send_sems, recv_sems):
    (x, y, c), me, other_core, chips = _two_level_places()
    return [pltpu.make_async_remote_copy(
        src_ref=sums_ref.at[cx * 2 + cy], dst_ref=land_ref.at[x * 2 + y], send_sem=send_sems.at[k],
        recv_sem=recv_sems.at[k], device_id=(cx, cy, c), device_id_type=pl.DeviceIdType.MESH)
        for k, (cx, cy) in enumerate(chips)]


def chip_start(name, sums):
    land = lax.empty(sums.shape, sums.dtype)

    def body(sums_ref, land_ref, send_sems, recv_sems, sums_out, land_out, token):
        for cp in _chip_copies(sums_ref, land_ref, send_sems, recv_sems):
            cp.start()
        token[...] = jnp.zeros_like(token)

    sem = pltpu.SemaphoreType.DMA((N_OTHER_CHIPS,))
    outs = pl.pallas_call(
        body, name=name,
        out_shape=(sem, sem, pltpu.HBM(sums.shape, sums.dtype), pltpu.HBM(land.shape, land.dtype),
                   jax.ShapeDtypeStruct((8, 128), F32)),
        in_specs=[_HBM, _HBM], out_specs=(_SEM, _SEM, _HBM, _HBM, pl.BlockSpec(memory_space=pltpu.VMEM)),
        input_output_aliases={0: 2, 1: 3}, compiler_params=pltpu.CompilerParams(has_side_effects=_DATAFLOW),
    )(pltpu.with_memory_space_constraint(sums, pltpu.HBM), land)
    return list(outs[:4]), outs[4]


def chip_wait(name, handle, after):
    send_sems, recv_sems, sums, land = handle

    def body(sums_ref, land_ref, send_ref, recv_ref, after_ref, sums_out, land_out):
        for cp in _chip_copies(sums_ref, land_ref, send_ref, recv_ref):
            cp.wait_send()
            cp.wait_recv()

    return pl.pallas_call(
        body, name=name, out_shape=(pltpu.HBM(sums.shape, sums.dtype), pltpu.HBM(land.shape, land.dtype)),
        in_specs=[_HBM, _HBM, _SEM, _SEM, pl.BlockSpec(memory_space=pl.ANY)], out_specs=(_HBM, _HBM),
        input_output_aliases={0: 0, 1: 1}, compiler_params=pltpu.CompilerParams(has_side_effects=_DATAFLOW),
    )(sums, land, send_sems, recv_sems, after)


def adamw_reduce(name, parts, w, m, v):
    nl, r, wd = w.shape
    tr = _row_tile(r, ROW_TILE_BYTES // (8 * wd))

    def body(*refs):
        p_refs = refs[:nl]
        w_ref, m_ref, v_ref, g_ref, d_ref, nm_ref, nv_ref = refs[nl:]
        layer = pl.program_id(0)
        for j in range(nl):
            @pl.when(layer == j)
            def _(j=j):
                g = p_refs[j][0].astype(F32)
                for sender in range(1, p_refs[j].shape[0]):
                    g = g + p_refs[j][sender].astype(F32)
                nm = B1 * m_ref[...] + (1.0 - B1) * g
                nv = B2 * v_ref[...] + (1.0 - B2) * jnp.square(g)
                m_hat = nm / (1.0 - B1 ** STEP)
                v_hat = nv / (1.0 - B2 ** STEP)
                g_ref[...] = g
                d_ref[...] = -LR * (m_hat / (jnp.sqrt(v_hat) + ADAM_EPS) + WD * w_ref[...])
                nm_ref[...] = nm
                nv_ref[...] = nv

    def part_spec(j):
        return pl.BlockSpec((parts[j].shape[0], tr, wd), lambda l, i: (0, jnp.where(l == j, i, 0), 0))

    spec = pl.BlockSpec((None, tr, wd), lambda l, i: (l, i, 0))
    return pl.pallas_call(
        body, name=name, grid=(nl, r // tr),
        in_specs=[part_spec(j) for j in range(nl)] + [spec, spec, spec],
        out_specs=[spec] * 4, out_shape=[jax.ShapeDtypeStruct((nl, r, wd), F32)] * 4,
        compiler_params=_params(("arbitrary", "arbitrary")),
    )(*parts, w, m, v)


def _pack_rows(vectors, rows=None):
    flat = jnp.concatenate([a.reshape(-1).astype(F32) for a in vectors])
    n = flat.shape[0]
    if rows is None:
        rows = -(-n // 1024) * 8
    return jnp.pad(flat, (0, rows * 128 - n)).reshape(rows, 128)


def _unpack_rows(packed, like):
    flat = packed.reshape(-1)
    out, pos = [], 0
    for a in like:
        out.append(flat[pos:pos + a.size].reshape(a.shape))
        pos += a.size
    return out


def kernel(x, p, mix_norm, mlp_norm, ple_norm, w_a_in, a_lb_logits, a_head_gain, w_a_out, kv_norm, w_kvf, b_f, w_b_q, w_b_out, w_mlp_up, w_mlp_down, w_ple_gate, w_ple_up, final_norm, loss_target, m_mix_norm, m_mlp_norm, m_ple_norm, m_w_a_in, m_a_lb_logits, m_a_head_gain, m_w_a_out, m_kv_norm, m_w_kvf, m_b_f, m_w_b_q, m_w_b_out, m_w_mlp_up, m_w_mlp_down, m_w_ple_gate, m_w_ple_up, m_final_norm, v_mix_norm, v_mlp_norm, v_ple_norm, v_w_a_in, v_a_lb_logits, v_a_head_gain, v_w_a_out, v_kv_norm, v_w_kvf, v_b_f, v_w_b_q, v_w_b_out, v_w_mlp_up, v_w_mlp_down, v_w_ple_gate, v_w_ple_up, v_final_norm):
    t, d = x.shape[1], x.shape[2]
    nh = d // HEAD_DIM
    n_layers = 2
    x2 = x.reshape(t, d)
    target = loss_target.reshape(t, d)
    me = _my_index()

    shards = {"w_a_in": w_a_in[0], "w_a_out": w_a_out[0], "w_kvf": w_kvf, "w_b_q": w_b_q[0], "w_b_out": w_b_out[0]}
    for l in range(n_layers):
        shards.update({f"w_mlp_up{l}": w_mlp_up[l], f"w_mlp_down{l}": w_mlp_down[l],
                       f"w_ple_gate{l}": w_ple_gate[l], f"w_ple_up{l}": w_ple_up[l]})
    first_use = ["a_lb_logits", "w_a_in", "w_a_out", "w_mlp_up0", "w_mlp_down0", "w_ple_gate0", "w_ple_up0", "w_kvf",
                 "w_b_q", "w_b_out", "w_mlp_up1", "w_mlp_down1", "w_ple_gate1", "w_ple_up1"]
    row_sharded = ("w_a_out", "w_b_q", "w_b_out", "w_mlp_down", "w_ple_gate")
    shards_bf = [a_lb_logits] + [shards[n].astype(BF16) for n in first_use[1:]]
    ag_handles = gather_start("ag_start", [
        lax.dynamic_update_slice(lax.empty((N_DEV,) + a.shape, a.dtype), a[None], (me, 0, 0)) for a in shards_bf])
    passed_on = {}
    weights = {}

    def pass_on(j, after):
        if j < len(first_use) and j not in passed_on:
            passed_on[j] = gather_pass_on("ag_pass_" + first_use[j], ag_handles[j], after)

    def weight(name, after=None):
        if name not in weights:
            j = first_use.index(name)
            pass_on(j, after)
            pass_on(j + 1, after)
            behind = passed_on[j + 1][1] if j + 1 in passed_on else after
            g = gather_wait("ag_wait_" + name, passed_on[j][0], behind)
            if name.rstrip("01") in row_sharded:
                g = g.reshape(1, g.shape[0] * g.shape[1], g.shape[2])
            weights[name] = g
        return weights[name]

    lgt = weight("a_lb_logits", x2).transpose(1, 0, 2).reshape(2, d)
    p_bf = [p[l, 0].astype(BF16) for l in range(n_layers)]

    def row(vec):
        return vec.reshape(1, -1)

    def mlp_ple_fwd(l, h_in, a):
        (h_a, u_mlp), _ = rowwise(f"add_norm_mlp{l}", _add_norm_fwd, [h_in, a], [row(mlp_norm[l])])
        pre, act = mm_nn(f"mlp_up{l}", u_mlp, weight(f"w_mlp_up{l}", u_mlp), fuse=(_relu2, (), (BF16, BF16)))
        mo = mm_nn(f"mlp_down{l}", act, weight(f"w_mlp_down{l}", act))
        (h_b, u_ple), _ = rowwise(f"add_norm_ple{l}", _add_norm_fwd, [h_a, mo], [row(ple_norm[l])])
        gpre = mm_nn(f"ple_gate{l}", u_ple, weight(f"w_ple_gate{l}", u_ple))
        pu = mm_nn(f"ple_up{l}", p_bf[l], weight(f"w_ple_up{l}", gpre))
        return dict(h_a=h_a, u_mlp=u_mlp, pre=pre, act=act, h_b=h_b, u_ple=u_ple, gpre=gpre, pu=pu)

    (u0,), _ = rowwise("norm_mix0", _norm_fwd, [x2], [row(mix_norm[0])])
    z = mm_nn("a_in", u0, weight("w_a_in", u0))
    og, states = hgrn_fwd(z, lgt, a_head_gain)
    a0 = mm_nn("a_out", og, weight("w_a_out", og))
    s0 = mlp_ple_fwd(0, x2, a0)
    (h3, u_kv, u1), _ = rowwise("ple_norms", _ple_two_norms_fwd, [s0["h_b"], s0["gpre"], s0["pu"]],
                                [row(kv_norm), row(mix_norm[1])])
    hk = mm_nn("kvf", u_kv, weight("w_kvf", u_kv), out3=True)
    hk = hk.transpose(1, 0, 2).reshape(t, -1)
    k_bf, v_bf = hk[:, :d].astype(BF16), hk[:, d:2 * d].astype(BF16)
    fl_t = hk[:, 2 * d:].T
    b_f_col = b_f.reshape(nh, 1)
    dcum = decay_fwd(fl_t, b_f_col)
    dcol, drow = dcum.reshape(nh, t, 1), dcum.reshape(nh, 1, t)
    q = mm_nn("b_q", u1, weight("w_b_q", dcum))
    o, lse = attn_fwd(q, k_bf, v_bf, dcol, drow)
    a1 = mm_nn("b_out", o, weight("w_b_out", o))
    s1 = mlp_ple_fwd(1, h3, a1)

    (dh, dgpre, dpu), (d_final, loss_rows) = rowwise(
        "tail", _tail_fwd_bwd, [s1["h_b"], s1["gpre"], s1["pu"], target], [row(final_norm)])

    sent = {}
    tokens = []

    two_level = ("w_mlp_up0", "w_mlp_up1", "w_mlp_down0", "w_mlp_down1", "w_a_in", "w_kvf")
    swapping = []

    def send_grad(name, g):
        g = g.reshape(N_DEV, -1, g.shape[-1])
        if name in two_level:
            sent[name], token = pair_start("rs_pair_" + name, g)
            swapping.append(name)
        else:
            (sent[name],), token = scatter_start("rs_start_" + name, [g])
        tokens.append(token)

    def second_stage(after):
        for name in swapping:
            g, half = pair_wait("rs_pairwait_" + name, sent[name], after)
            sent[name], token = chip_start("rs_chip_" + name, pair_sum("rs_sum_" + name, g, half))
            tokens.append(token)
        swapping.clear()

    def after_sends():
        deps = tuple(tokens)
        tokens.clear()
        return deps

    def mlp_ple_bwd(l, s, dh, dgpre, dpu):
        send_grad(f"w_ple_gate{l}", mm_tn(f"d_ple_gate_w{l}", s["u_ple"], dgpre, 1, deps=after_sends()))
        send_grad(f"w_ple_up{l}", mm_tn(f"d_ple_up_w{l}", p_bf[l], dpu, N_DEV, deps=after_sends()))
        du = mm_nt(f"d_ple_gate_x{l}", dgpre, weight(f"w_ple_gate{l}"), deps=after_sends())
        (dh, dh_bf), (d_ple,) = rowwise(f"d_norm_ple{l}", _norm_bwd, [s["h_b"], du, dh], [row(ple_norm[l])])
        send_grad(f"w_mlp_down{l}", mm_tn(f"d_mlp_down_w{l}", s["act"], dh_bf, 1))
        (dpre,) = mm_nt(f"d_mlp_down_x{l}", dh_bf, weight(f"w_mlp_down{l}"), deps=after_sends(),
                        fuse=(_relu2_bwd, (s["pre"],), (BF16,)))
        second_stage(dpre)
        send_grad(f"w_mlp_up{l}", mm_tn(f"d_mlp_up_w{l}", s["u_mlp"], dpre, N_DEV))
        du = mm_nt(f"d_mlp_up_x{l}", dpre, weight(f"w_mlp_up{l}"), deps=after_sends())
        second_stage(du)
        (dh, dh_bf), (d_mlp,) = rowwise(f"d_norm_mlp{l}", _norm_bwd, [s["h_a"], du, dh], [row(mlp_norm[l])])
        return dh, dh_bf, d_ple, d_mlp

    dh, dh_bf, d_ple1, d_mlp1 = mlp_ple_bwd(1, s1, dh, dgpre, dpu)
    send_grad("w_b_out", mm_tn("d_b_out_w", o, dh_bf, 1))
    do = mm_nt("d_b_out_x", dh_bf, weight("w_b_out"), out_dtype=BF16, deps=after_sends())
    dq, dk, dv, ddrow = attn_bwd(q, k_bf, v_bf, dcol, drow, lse, do)
    send_grad("w_b_q", mm_tn("d_b_q_w", u1, dq, 1))
    du1 = mm_nt("d_b_q_x", dq, weight("w_b_q"), deps=after_sends())
    dfl_t, d_b_f = decay_bwd(fl_t, b_f_col, ddrow.reshape(nh, t))
    dhk = jnp.concatenate([dk.astype(BF16), dv.astype(BF16), dfl_t.T.astype(BF16)], axis=1)
    dhk = dhk.reshape(t, N_DEV, -1).transpose(1, 0, 2)
    send_grad("w_kvf", mm_tn("d_kvf_w", u_kv, dhk, N_DEV))
    du_kv = mm_nt("d_kvf_x", dhk, weight("w_kvf"), deps=after_sends())
    second_stage(du_kv)
    (dh, dgpre, dpu), (d_kv_norm, d_mix1) = rowwise(
        "d_ple_norms", _two_norms_ple_bwd, [h3, du_kv, du1, dh, s0["gpre"], s0["pu"]],
        [row(kv_norm), row(mix_norm[1])])
    dh, dh_bf, d_ple0, d_mlp0 = mlp_ple_bwd(0, s0, dh, dgpre, dpu)
    send_grad("w_a_out", mm_tn("d_a_out_w", og, dh_bf, 1))
    dog = mm_nt("d_a_out_x", dh_bf, weight("w_a_out"), deps=after_sends())
    dz4, d_lgt, d_hg = hgrn_bwd(z, lgt, a_head_gain, states, dog)
    send_grad("w_a_in", mm_tn("d_a_in_w", u0, dz4, N_DEV, stacked=True))
    du0 = mm_nt("d_a_in_x", dz4, weight("w_a_in"), deps=after_sends(), stacked=True)
    second_stage(du0)
    (dx, _), (d_mix0,) = rowwise("d_norm_mix0", _norm_bwd, [x2, du0, dh], [row(mix_norm[0])])

    new = {}
    last = [dx]

    def update(name, parts, w, m, v):
        shp = w.shape
        w3, m3, v3 = (a.reshape(len(parts), -1, shp[-1]) for a in (w, m, v))
        new[name] = tuple(a.reshape(shp) for a in adamw_reduce("adamw_" + name, parts, w3, m3, v3))
        last[0] = new[name][0]

    def receive_update(name, layers, w, m, v):
        parts = {}
        for sfx in layers:
            if name + sfx in two_level:
                mine, land = chip_wait(f"rs_wait_{name}{sfx}", sent[name + sfx], last[0])
                slot = me // 2
            else:
                mine, land = scatter_wait(f"rs_wait_{name}{sfx}", sent[name + sfx], last[0])
                slot = me
            parts[sfx] = lax.dynamic_update_slice(land, lax.dynamic_slice_in_dim(mine, slot, 1, 0), (slot, 0, 0))
        update(name, [parts[sfx] for sfx in sorted(layers)], w, m, v)

    both = ("1", "0")
    receive_update("w_b_out", ("",), w_b_out, m_w_b_out, v_w_b_out)
    receive_update("w_b_q", ("",), w_b_q, m_w_b_q, v_w_b_q)
    receive_update("w_kvf", ("",), w_kvf, m_w_kvf, v_w_kvf)
    receive_update("w_ple_gate", both, w_ple_gate, m_w_ple_gate, v_w_ple_gate)
    receive_update("w_ple_up", both, w_ple_up, m_w_ple_up, v_w_ple_up)
    receive_update("w_mlp_down", both, w_mlp_down, m_w_mlp_down, v_w_mlp_down)
    receive_update("w_mlp_up", both, w_mlp_up, m_w_mlp_up, v_w_mlp_up)
    receive_update("w_a_out", ("",), w_a_out, m_w_a_out, v_w_a_out)
    receive_update("w_a_in", ("",), w_a_in, m_w_a_in, v_w_a_in)

    small = dict(mix_norm=jnp.concatenate([d_mix0, d_mix1]), mlp_norm=jnp.concatenate([d_mlp0, d_mlp1]),
                 ple_norm=jnp.concatenate([d_ple0, d_ple1]), a_head_gain=d_hg, kv_norm=d_kv_norm.reshape(d),
                 b_f=d_b_f.reshape(nh), final_norm=d_final.reshape(d))
    small_w = dict(mix_norm=(mix_norm, m_mix_norm, v_mix_norm), mlp_norm=(mlp_norm, m_mlp_norm, v_mlp_norm),
                   ple_norm=(ple_norm, m_ple_norm, v_ple_norm),
                   a_head_gain=(a_head_gain, m_a_head_gain, v_a_head_gain), kv_norm=(kv_norm, m_kv_norm, v_kv_norm),
                   b_f=(b_f, m_b_f, v_b_f), final_norm=(final_norm, m_final_norm, v_final_norm))
    names = list(small)
    packed = _pack_rows([d_lgt] + [small[n] for n in names])
    everyone = all_gather("ag_small_grads", packed, deps=(last[0],))
    n_lgt_rows = d_lgt.size // 128
    lgt_parts = everyone[:, :n_lgt_rows].reshape(N_DEV, 2, d)
    lgt_parts = lax.dynamic_slice_in_dim(lgt_parts, me * a_lb_logits.shape[1], a_lb_logits.shape[1], axis=2)
    update("a_lb_logits", [lgt_parts], a_lb_logits, m_a_lb_logits, v_a_lb_logits)
    rest = everyone[:, n_lgt_rows:]
    like = [small_w[n][0] for n in names]
    packed_w, packed_m, packed_v = (_pack_rows([small_w[n][j] for n in names], rest.shape[1])[None] for j in range(3))
    outs = adamw_reduce("adamw_small", [rest], packed_w, packed_m, packed_v)
    unpacked = [_unpack_rows(a, like) for a in outs]
    for j, n in enumerate(names):
        new[n] = tuple(unpacked[q][j] for q in range(4))

    order = ["mix_norm", "mlp_norm", "ple_norm", "w_a_in", "a_lb_logits", "a_head_gain", "w_a_out", "kv_norm",
             "w_kvf", "b_f", "w_b_q", "w_b_out", "w_mlp_up", "w_mlp_down", "w_ple_gate", "w_ple_up", "final_norm"]
    loss_here, _ = lax.optimization_barrier((loss_rows[0, 0], new["final_norm"][0]))
    loss = lax.psum(loss_here, MESH_AXES)
    result = [loss, dx.reshape(x.shape)]
    for j in range(4):
        result += [new[n][j] for n in order]
    return tuple(result)
```

```python
import jax
import jax.numpy as jnp
from jax import lax
from jax.experimental import pallas as pl
from jax.experimental.pallas import tpu as pltpu

F32 = jnp.float32
BF16 = jnp.bfloat16
HEAD_DIM = 128
CHUNK = 16
TILE = 128
HEADS_PER_STEP = 4
NORM_EPS = 1e-6
N_DEV = 8
MESH_AXES = ("x", "y", "c")
VMEM_LIMIT_BYTES = 48 * 1024 * 1024
ROW_TILE_BYTES = 2 * 1024 * 1024
LR, B1, B2, ADAM_EPS, WD, STEP = 0.001, 0.9, 0.999, 1e-08, 0.01, 10
NEG_BIG = -1e30

NN = (((1,), (0,)), ((), ()))
NT = (((1,), (1,)), ((), ()))
TN = (((0,), (0,)), ((), ()))


def _params(semantics):
    return pltpu.CompilerParams(dimension_semantics=semantics, vmem_limit_bytes=VMEM_LIMIT_BYTES)


def _tile(n, prefs):
    for p in prefs:
        if n % p == 0:
            return p
    return n


def _row_tile(rows, limit):
    for cand in (2048, 1024, 512, 256, 128, 64, 32, 16):
        if cand <= limit and rows % cand == 0:
            return cand
    return rows


def _mm_call(name, a, b, dims, grid, a_spec, b_spec, o_spec, o_shape, acc_shape, k_axes, out_dtype, deps=(),
             fuse=None, split=1):
    nk = 1
    for ax in k_axes:
        nk *= grid[ax]
    fn, extra, out_dtypes = fuse if fuse else (lambda acc: (acc,), (), (out_dtype,))
    n_extra, n_out = len(extra), len(out_dtypes)

    def finish(acc, rest):
        o_refs = rest[n_extra + len(deps):n_extra + len(deps) + n_out]
        for ref, val in zip(o_refs, fn(acc, *[r[...] for r in rest[:n_extra]])):
            ref[...] = val.astype(ref.dtype)

    def product(a_ref, b_ref):
        if split == 1:
            return lax.dot_general(a_ref[...], b_ref[...], dims, preferred_element_type=F32)
        wide = a_ref.shape[1] // split
        return sum(lax.dot_general(a_ref[:, q * wide:(q + 1) * wide], b_ref[q], dims, preferred_element_type=F32)
                   for q in range(split))

    def one_step(a_ref, b_ref, *rest):
        finish(product(a_ref, b_ref), rest)

    def accumulate(a_ref, b_ref, *rest):
        acc_ref = rest[-1]
        k = 0
        for ax in k_axes:
            k = k * grid[ax] + pl.program_id(ax)
        part = product(a_ref, b_ref)

        @pl.when(k == 0)
        def _():
            acc_ref[...] = part

        @pl.when((k > 0) & (k < nk - 1))
        def _():
            acc_ref[...] += part

        @pl.when(k == nk - 1)
        def _():
            finish(acc_ref[...] + part, rest)

    sem = tuple("arbitrary" if ax in k_axes else "parallel" for ax in range(len(grid)))
    outs = pl.pallas_call(
        one_step if nk == 1 else accumulate, name=name, grid=grid,
        in_specs=[a_spec, b_spec] + [o_spec] * n_extra + [pl.BlockSpec(memory_space=pl.ANY)] * len(deps),
        out_specs=[o_spec] * n_out, out_shape=[jax.ShapeDtypeStruct(o_shape, dt) for dt in out_dtypes],
        scratch_shapes=[] if nk == 1 else [pltpu.VMEM(acc_shape, F32)], compiler_params=_params(sem),
    )(a, b, *extra, *deps)
    return outs if fuse else outs[0]


def mm_nn(name, a, b3, out_dtype=F32, out3=False, deps=(), fuse=None):
    m, k = a.shape
    g, _, n = b3.shape
    tm, tk = _tile(m, (1024, 512, 256)), _tile(k, (2048, 1024, 512, 256))
    tn = n if out3 else _tile(n, (1024, 512, 256, 128))
    nj = n // tn
    grid = (m // tm, g, nj, k // tk)
    a_spec = pl.BlockSpec((tm, tk), lambda i, gg, j, kk: (i, kk))
    b_spec = pl.BlockSpec((None, tk, tn), lambda i, gg, j, kk: (gg, kk, j))
    if out3:
        o_spec = pl.BlockSpec((None, tm, tn), lambda i, gg, j, kk: (gg, i, j))
        o_shape = (g, m, n)
    else:
        o_spec = pl.BlockSpec((tm, tn), lambda i, gg, j, kk: (i, gg * nj + j))
        o_shape = (m, g * n)
    return _mm_call(name, a, b3, NN, grid, a_spec, b_spec, o_spec, o_shape, (tm, tn), (3,), out_dtype, deps, fuse)


def mm_nt(name, a, b3, out_dtype=F32, deps=(), fuse=None, stacked=False):
    g, k, n = b3.shape
    a3 = a.ndim == 3 and not stacked
    m = a.shape[1] if a.ndim == 3 else a.shape[0]
    tm, tko = _tile(m, (1024, 512, 256)), _tile(k, (1024, 512, 256))
    tc = n if a3 else _tile(n, (2048, 1024, 512, 256, 128))
    nc = n // tc
    per = g // a.shape[0] if stacked else g
    pair = 2 if (not a3 and nc == 1 and per % 2 == 0 and tc <= 1024) else 1
    grid = (m // tm, k // tko, g // pair, nc)
    if a3:
        a_spec = pl.BlockSpec((None, tm, tc), lambda i, j, gg, c: (gg, i, c))
    elif stacked:
        a_spec = pl.BlockSpec((None, tm, pair * tc),
                              lambda i, j, gg, c: ((gg * pair) // per, i, (((gg * pair) % per) // pair) * nc + c))
    else:
        a_spec = pl.BlockSpec((tm, pair * tc), lambda i, j, gg, c: (i, gg * nc + c))
    if pair == 1:
        b_spec = pl.BlockSpec((None, tko, tc), lambda i, j, gg, c: (gg, j, c))
    else:
        b_spec = pl.BlockSpec((pair, tko, tc), lambda i, j, gg, c: (gg, j, c))
    o_spec = pl.BlockSpec((tm, tko), lambda i, j, gg, c: (i, j))
    return _mm_call(name, a, b3, NT, grid, a_spec, b_spec, o_spec, (m, k), (tm, tko), (2, 3), out_dtype, deps, fuse,
                    pair)


def mm_tn(name, a, b, g, out_dtype=BF16, deps=(), stacked=False):
    t, k = a.shape
    b3 = b.ndim == 3 and not stacked
    n = b.shape[2] if b3 else (b.shape[0] * b.shape[2] if stacked else b.shape[1]) // g
    tm = _tile(k, (1024, 512, 256))
    tn = n if b3 else _tile(n, (1024, 512, 256, 128))
    tt = _tile(t, (2048, 1024, 512, 256))
    nj = n // tn
    grid = (g, k // tm, nj, t // tt)
    a_spec = pl.BlockSpec((tt, tm), lambda gg, i, j, s: (s, i))
    if b3:
        b_spec = pl.BlockSpec((None, tt, tn), lambda gg, i, j, s: (gg, s, j))
    elif stacked:
        per = g // b.shape[0]
        b_spec = pl.BlockSpec((None, tt, tn), lambda gg, i, j, s: (gg // per, s, (gg % per) * nj + j))
    else:
        b_spec = pl.BlockSpec((tt, tn), lambda gg, i, j, s: (s, gg * nj + j))
    o_spec = pl.BlockSpec((None, tm, tn), lambda gg, i, j, s: (gg, i, j))
    return _mm_call(name, a, b, TN, grid, a_spec, b_spec, o_spec, (g, k, n), (tm, tn), (3,), out_dtype, deps)


def rowwise(name, fn, rows, vecs=()):
    t = rows[0].shape[0]
    wmax = max(r.shape[1] for r in rows)
    tr = _row_tile(t, ROW_TILE_BYTES // (4 * wmax))
    row_s = [jax.ShapeDtypeStruct((tr, r.shape[1]), r.dtype) for r in rows]
    vec_s = [jax.ShapeDtypeStruct(v.shape, v.dtype) for v in vecs]
    out_rows_s, out_sums_s = jax.eval_shape(fn, *row_s, *vec_s)
    n_in, n_r = len(rows) + len(vecs), len(out_rows_s)

    def body(*refs):
        i = pl.program_id(0)
        o_rows, o_sums = fn(*[r[...] for r in refs[:n_in]])
        for ref, val in zip(refs[n_in:n_in + n_r], o_rows):
            ref[...] = val

        if out_sums_s:
            @pl.when(i == 0)
            def _():
                for ref in refs[n_in + n_r:]:
                    ref[...] = jnp.zeros_like(ref)

            for ref, val in zip(refs[n_in + n_r:], o_sums):
                ref[...] += val

    in_specs = [pl.BlockSpec((tr, r.shape[1]), lambda i: (i, 0)) for r in rows]
    in_specs += [pl.BlockSpec(v.shape, lambda i: (0, 0)) for v in vecs]
    out_specs = [pl.BlockSpec((tr, s.shape[1]), lambda i: (i, 0)) for s in out_rows_s]
    out_specs += [pl.BlockSpec(s.shape, lambda i: (0, 0)) for s in out_sums_s]
    out_shape = [jax.ShapeDtypeStruct((t, s.shape[1]), s.dtype) for s in out_rows_s]
    out_shape += [jax.ShapeDtypeStruct(s.shape, s.dtype) for s in out_sums_s]
    outs = pl.pallas_call(
        body, name=name, grid=(t // tr,), in_specs=in_specs, out_specs=out_specs, out_shape=out_shape,
        compiler_params=_params(("arbitrary",)),
    )(*rows, *vecs)
    return outs[:n_r], outs[n_r:]


def _rms(x, gain):
    return x * lax.rsqrt(jnp.mean(x * x, axis=-1, keepdims=True) + NORM_EPS) * gain


def _norm_fwd(x, gain):
    return (_rms(x, gain).astype(BF16),), ()


def _add_norm_fwd(h, a, gain):
    h = h + a
    return (h, _rms(h, gain).astype(BF16)), ()


def _relu2(pre):
    r = jnp.maximum(pre, 0.0)
    return pre, r * r


def _ple(h, gpre, pu):
    return h + pu * jax.nn.sigmoid(gpre)


def _ple_two_norms_fwd(h, gpre, pu, gain_a, gain_b):
    h = _ple(h, gpre, pu)
    return (h, _rms(h, gain_a).astype(BF16), _rms(h, gain_b).astype(BF16)), ()


def _tail_fwd_bwd(h, gpre, pu, target, gain):
    def row_loss(h, gpre, pu, gain):
        y = _rms(_ple(h, gpre, pu), gain)
        return 0.5 * jnp.mean(jnp.square(y - target), axis=-1, keepdims=True)

    loss, vjp = jax.vjp(row_loss, h, gpre, pu, gain)
    dh, dgpre, dpu, dgain = vjp(jnp.ones_like(loss))
    loss = jnp.broadcast_to(jnp.sum(loss, axis=0, keepdims=True), (1, 128))
    return (dh, dgpre.astype(BF16), dpu.astype(BF16)), (dgain, loss)


def _norm_bwd(h, du, dh_in, gain):
    _, vjp = jax.vjp(_rms, h, gain)
    dh, dgain = vjp(du)
    dh = dh_in + dh
    return (dh, dh.astype(BF16)), (dgain,)


def _two_norms_ple_bwd(h, du_a, du_b, dh_in, gpre, pu, gain_a, gain_b):
    _, vjp = jax.vjp(lambda h, ga, gb: (_rms(h, ga), _rms(h, gb)), h, gain_a, gain_b)
    dh, dga, dgb = vjp((du_a, du_b))
    dh = dh_in + dh
    _, gate_vjp = jax.vjp(lambda g, u: u * jax.nn.sigmoid(g), gpre, pu)
    dgpre, dpu = gate_vjp(dh)
    return (dh, dgpre.astype(BF16), dpu.astype(BF16)), (dga, dgb)


def _relu2_bwd(dact, pre):
    return (dact * 2.0 * jnp.maximum(pre.astype(F32), 0.0),)


def _bf16_dot(dims_fwd, dims_da, dims_db, swap_da, swap_db):
    @jax.custom_vjp
    def dot(a, b):
        return lax.dot_general(a.astype(BF16), b.astype(BF16), dims_fwd, preferred_element_type=F32)

    def fwd(a, b):
        return dot(a, b), (a, b)

    def bwd(res, ct):
        a, b = res
        ct, a, b = ct.astype(BF16), a.astype(BF16), b.astype(BF16)
        da = lax.dot_general(*((b, ct) if swap_da else (ct, b)), dims_da, preferred_element_type=F32)
        db = lax.dot_general(*((ct, a) if swap_db else (a, ct)), dims_db, preferred_element_type=F32)
        return da, db

    dot.defvjp(fwd, bwd)
    return dot


_dot_nn = _bf16_dot(NN, NT, TN, False, False)
_dot_nt = _bf16_dot(NT, NN, TN, False, True)
_dot_tn = _bf16_dot(TN, NT, NN, True, False)


def _chunk_causal_mask():
    r = lax.broadcasted_iota(jnp.int32, (TILE, TILE), 0)
    c = lax.broadcasted_iota(jnp.int32, (TILE, TILE), 1)
    return ((r // CHUNK) == (c // CHUNK)) & (c <= r)


def _chunk_scan(x, reverse):
    pos = lax.broadcasted_iota(jnp.int32, x.shape, 0) % CHUNK
    step = 1
    while step < CHUNK:
        if reverse:
            x = x + jnp.where(pos < CHUNK - step, pltpu.roll(x, x.shape[0] - step, axis=0), 0.0)
        else:
            x = x + jnp.where(pos >= step, pltpu.roll(x, step, axis=0), 0.0)
        step *= 2
    return x


def _chunk_total(x):
    return _chunk_scan(x, False) + _chunk_scan(x, True) - x


@jax.custom_vjp
def _chunk_sums(x):
    return _chunk_scan(x, False), _chunk_total(x)


def _chunk_sums_fwd(x):
    return _chunk_sums(x), None


def _chunk_sums_bwd(_, ct):
    return (_chunk_scan(ct[0], True) + _chunk_total(ct[1]),)


_chunk_sums.defvjp(_chunk_sums_fwd, _chunk_sums_bwd)


def _hgrn_tile(q, f, i, g, lgt, hg, st):
    d = q.shape[1]
    l0, l1 = lgt[0:1], lgt[1:2]
    mx = jnp.maximum(l0, l1)
    e0, e1 = jnp.exp(l0 - mx), jnp.exp(l1 - mx)
    lb = e0 / (e0 + e1)
    fg = lb + (1.0 - lb) * jax.nn.sigmoid(f)
    k = 1.0 - fg
    causal = _chunk_causal_mask()
    b, b_last = _chunk_sums(jnp.log(fg))
    q_in = q * jax.nn.sigmoid(q) * (d ** -0.5) * jnp.exp(b)
    k_in = k * jnp.exp(-b)
    k_end = k * jnp.exp(b_last - b)
    att = jnp.where(causal, _dot_nt(q_in, k_in), 0.0)
    o_intra = _dot_nn(att, i)
    n_chunks = TILE // CHUNK
    chunk_of_row = lax.broadcasted_iota(jnp.int32, (TILE, 1), 0) // CHUNK

    def spread(a):
        return jnp.concatenate([jnp.where(chunk_of_row == n, a, 0.0) for n in range(n_chunks)], axis=1)

    increments = _dot_tn(i, spread(k_end))
    states = []
    for n in range(n_chunks):
        states.append(st)
        decay = jnp.exp(jnp.mean(b_last[n * CHUNK:(n + 1) * CHUNK], axis=0, keepdims=True))
        st = st * decay + increments[:, n * d:(n + 1) * d]
    o = o_intra + _dot_nt(spread(q_in), jnp.concatenate(states, axis=1))
    o = o * lax.rsqrt(jnp.mean(o * o, axis=-1, keepdims=True) + NORM_EPS) * hg
    return o * (g * jax.nn.sigmoid(g)), st


def hgrn_fwd(z, lgt, hg):
    t, d4 = z.shape
    d = d4 // 4
    nh, nt = d // HEAD_DIM, t // TILE
    hp = HEADS_PER_STEP
    wide = hp * HEAD_DIM

    def body(q_ref, f_ref, i_ref, g_ref, lgt_ref, hg_ref, o_ref, st_out_ref, st_ref):
        tt = pl.program_id(1)

        @pl.when(tt == 0)
        def _():
            st_ref[...] = jnp.zeros_like(st_ref)

        for hh in range(hp):
            cols = slice(hh * HEAD_DIM, (hh + 1) * HEAD_DIM)
            st = st_ref[hh]
            st_out_ref[hh] = st
            o, st = _hgrn_tile(q_ref[:, cols], f_ref[:, cols], i_ref[:, cols], g_ref[:, cols], lgt_ref[:, cols],
                               hg_ref[...], st)
            o_ref[:, cols] = o.astype(o_ref.dtype)
            st_ref[hh] = st

    def part(p):
        return pl.BlockSpec((TILE, wide), lambda h, tt: (tt, p * (nh // hp) + h))

    return pl.pallas_call(
        body, name="hgrn_fwd", grid=(nh // hp, nt),
        in_specs=[part(0), part(1), part(2), part(3),
                  pl.BlockSpec((2, wide), lambda h, tt: (0, h)),
                  pl.BlockSpec((1, HEAD_DIM), lambda h, tt: (0, 0))],
        out_specs=[pl.BlockSpec((TILE, wide), lambda h, tt: (tt, h)),
                   pl.BlockSpec((hp, None, HEAD_DIM, HEAD_DIM), lambda h, tt: (h, tt, 0, 0))],
        out_shape=[jax.ShapeDtypeStruct((t, d), BF16),
                   jax.ShapeDtypeStruct((nh, nt, HEAD_DIM, HEAD_DIM), F32)],
        scratch_shapes=[pltpu.VMEM((hp, HEAD_DIM, HEAD_DIM), F32)],
        compiler_params=_params(("parallel", "arbitrary")),
    )(z, z, z, z, lgt, hg)


def hgrn_bwd(z, lgt, hg, states, dout):
    t, d4 = z.shape
    d = d4 // 4
    nh, nt = d // HEAD_DIM, t // TILE
    hp = HEADS_PER_STEP
    wide = hp * HEAD_DIM

    def body(q_ref, f_ref, i_ref, g_ref, lgt_ref, hg_ref, st_in_ref, do_ref, dz_ref, dlgt_ref, dhg_ref, dst_ref):
        h, tt = pl.program_id(0), pl.program_id(1)

        @pl.when(tt == 0)
        def _():
            dst_ref[...] = jnp.zeros_like(dst_ref)
            dlgt_ref[...] = jnp.zeros_like(dlgt_ref)

        @pl.when((tt == 0) & (h == 0))
        def _():
            dhg_ref[...] = jnp.zeros_like(dhg_ref)

        for hh in range(hp):
            cols = slice(hh * HEAD_DIM, (hh + 1) * HEAD_DIM)
            _, vjp = jax.vjp(_hgrn_tile, q_ref[:, cols], f_ref[:, cols], i_ref[:, cols], g_ref[:, cols],
                             lgt_ref[:, cols], hg_ref[...], st_in_ref[hh])
            grads = vjp((do_ref[:, cols], dst_ref[hh]))
            for p in range(4):
                dz_ref[p, :, cols] = grads[p].astype(dz_ref.dtype)
            dlgt_ref[:, cols] += grads[4]
            dhg_ref[...] += grads[5]
            dst_ref[hh] = grads[6]

    def part(p):
        return pl.BlockSpec((TILE, wide), lambda h, tt: (nt - 1 - tt, p * (nh // hp) + h))

    return pl.pallas_call(
        body, name="hgrn_bwd", grid=(nh // hp, nt),
        in_specs=[part(0), part(1), part(2), part(3),
                  pl.BlockSpec((2, wide), lambda h, tt: (0, h)),
                  pl.BlockSpec((1, HEAD_DIM), lambda h, tt: (0, 0)),
                  pl.BlockSpec((hp, None, HEAD_DIM, HEAD_DIM), lambda h, tt: (h, nt - 1 - tt, 0, 0)),
                  pl.BlockSpec((TILE, wide), lambda h, tt: (nt - 1 - tt, h))],
        out_specs=[pl.BlockSpec((4, TILE, wide), lambda h, tt: (0, nt - 1 - tt, h)),
                   pl.BlockSpec((2, wide), lambda h, tt: (0, h)),
                   pl.BlockSpec((1, HEAD_DIM), lambda h, tt: (0, 0))],
        out_shape=[jax.ShapeDtypeStruct((4, t, d), BF16),
                   jax.ShapeDtypeStruct((2, d), F32),
                   jax.ShapeDtypeStruct((1, HEAD_DIM), F32)],
        scratch_shapes=[pltpu.VMEM((hp, HEAD_DIM, HEAD_DIM), F32)],
        compiler_params=_params(("arbitrary", "arbitrary")),
    )(z, z, z, z, lgt, hg, states, dout)


def _log_sigmoid(x):
    return jnp.minimum(x, 0.0) - jnp.log(1.0 + jnp.exp(-jnp.abs(x)))


def decay_fwd(fl_t, b_f):
    nh, t = fl_t.shape

    def body(fl_ref, b_ref, out_ref):
        r = lax.broadcasted_iota(jnp.int32, (128, 128), 0)
        c = lax.broadcasted_iota(jnp.int32, (128, 128), 1)
        upper = (r <= c).astype(F32)
        carry = jnp.zeros((nh, 1), F32)
        for j in range(t // 128):
            cols = slice(j * 128, (j + 1) * 128)
            ls = _log_sigmoid(fl_ref[:, cols] + b_ref[...])
            out_ref[:, cols] = carry + jnp.dot(ls, upper, precision=lax.Precision.HIGHEST,
                                               preferred_element_type=F32)
            carry = carry + jnp.sum(ls, axis=1, keepdims=True)

    return pl.pallas_call(body, name="decay_fwd", out_shape=jax.ShapeDtypeStruct((nh, t), F32),
                          compiler_params=_params(None))(fl_t, b_f)


def decay_bwd(fl_t, b_f, ddcum):
    nh, t = fl_t.shape

    def body(fl_ref, b_ref, dd_ref, dfl_ref, db_ref):
        r = lax.broadcasted_iota(jnp.int32, (128, 128), 0)
        c = lax.broadcasted_iota(jnp.int32, (128, 128), 1)
        lower = (r >= c).astype(F32)
        carry = jnp.zeros((nh, 1), F32)
        db = jnp.zeros((nh, 1), F32)
        for j in reversed(range(t // 128)):
            cols = slice(j * 128, (j + 1) * 128)
            dd = dd_ref[:, cols]
            dls = carry + jnp.dot(dd, lower, precision=lax.Precision.HIGHEST, preferred_element_type=F32)
            carry = carry + jnp.sum(dd, axis=1, keepdims=True)
            dfl = dls * jax.nn.sigmoid(-(fl_ref[:, cols] + b_ref[...]))
            dfl_ref[:, cols] = dfl
            db = db + jnp.sum(dfl, axis=1, keepdims=True)
        db_ref[...] = db

    return pl.pallas_call(body, name="decay_bwd",
                          out_shape=[jax.ShapeDtypeStruct((nh, t), F32), jax.ShapeDtypeStruct((nh, 1), F32)],
                          compiler_params=_params(None))(fl_t, b_f, ddcum)


def _attn_parts(t):
    tq = _tile(t, (256, 128))
    per_part = 2 if t // tq >= 4 else 1
    return tq, [(first, per_part, (first + per_part) * tq) for first in range(0, t // tq, per_part)]


def _attn_logits(q_ref, k_ref, dcol_ref, drow_ref, row0, tq, keys):
    qs = (q_ref[...] * (HEAD_DIM ** -0.5)).astype(BF16)
    s = lax.dot_general(qs, k_ref[...], NT, preferred_element_type=F32)
    s = s + dcol_ref[...] - drow_ref[...]
    row = row0 + lax.broadcasted_iota(jnp.int32, (tq, keys), 0)
    col = lax.broadcasted_iota(jnp.int32, (tq, keys), 1)
    return qs, jnp.where(col <= row, s, NEG_BIG)


def attn_fwd(q, kv, dcol, drow):
    t, d = q.shape
    nh = d // HEAD_DIM
    tq, parts = _attn_parts(t)
    o = lse = None
    for first, count, keys in parts:
        def body(q_ref, k_ref, v_ref, dcol_ref, drow_ref, *rest, first=first, keys=keys):
            o_ref, lse_ref = rest[-2:]
            _, s = _attn_logits(q_ref, k_ref, dcol_ref, drow_ref, (first + pl.program_id(1)) * tq, tq, keys)
            m = jnp.max(s, axis=1, keepdims=True)
            p = jnp.exp(s - m)
            l = jnp.sum(p, axis=1, keepdims=True)
            acc = jnp.dot(p.astype(BF16), v_ref[...], preferred_element_type=F32)
            o_ref[...] = (acc / l).astype(o_ref.dtype)
            lse_ref[...] = m + jnp.log(l)

        tile = pl.BlockSpec((tq, HEAD_DIM), lambda h, i, first=first: (first + i, h))
        col = pl.BlockSpec((None, tq, 1), lambda h, i, first=first: (h, first + i, 0))
        seen_k = pl.BlockSpec((keys, HEAD_DIM), lambda h, i: (0, h))
        seen_v = pl.BlockSpec((keys, HEAD_DIM), lambda h, i: (0, nh + h))
        carried = [] if o is None else [o, lse]
        o, lse = pl.pallas_call(
            body, name=f"attn_fwd_{first}", grid=(nh, count),
            in_specs=[tile, seen_k, seen_v, col, pl.BlockSpec((None, 1, keys), lambda h, i: (h, 0, 0))]
            + [pl.BlockSpec(memory_space=pl.ANY)] * len(carried),
            out_specs=[tile, col],
            out_shape=[jax.ShapeDtypeStruct((t, d), BF16), jax.ShapeDtypeStruct((nh, t, 1), F32)],
            input_output_aliases={5: 0, 6: 1} if carried else {},
            compiler_params=_params(("parallel", "parallel")),
        )(q, kv, kv, dcol, drow, *carried)
    return o, lse


def attn_bwd(q, kv, dcol, drow, lse, do):
    t, d = q.shape
    nh = d // HEAD_DIM
    tq, parts = _attn_parts(t)
    dq = dk = dv = ddrow = None
    for first, count, keys in reversed(parts):
        first_call = dq is None

        def body(q_ref, k_ref, v_ref, dcol_ref, drow_ref, lse_ref, do_ref, *rest, first=first, keys=keys,
                 count=count, first_call=first_call):
            dq_ref, dk_ref, dv_ref, ddrow_ref, dk_acc, dv_acc, ddrow_acc = rest[-7:]
            i = pl.program_id(1)

            @pl.when(i == 0)
            def _():
                if first_call:
                    dk_acc[...] = jnp.zeros_like(dk_acc)
                    dv_acc[...] = jnp.zeros_like(dv_acc)
                    ddrow_acc[...] = jnp.zeros_like(ddrow_acc)
                else:
                    dk_acc[...] = rest[1][...]
                    dv_acc[...] = rest[2][...]
                    ddrow_acc[...] = rest[3][...]

            qs, s = _attn_logits(q_ref, k_ref, dcol_ref, drow_ref, (first + i) * tq, tq, keys)
            p = jnp.exp(s - lse_ref[...])
            do = do_ref[...]
            dp = lax.dot_general(do, v_ref[...], NT, preferred_element_type=F32)
            ds = p * (dp - jnp.sum(p * dp, axis=1, keepdims=True))
            dsb = ds.astype(BF16)
            dq_ref[...] = (jnp.dot(dsb, k_ref[...], preferred_element_type=F32) * (HEAD_DIM ** -0.5)).astype(dq_ref.dtype)
            dk_acc[...] += lax.dot_general(dsb, qs, TN, preferred_element_type=F32)
            dv_acc[...] += lax.dot_general(p.astype(BF16), do, TN, preferred_element_type=F32)
            ddrow_acc[...] -= jnp.sum(ds, axis=0, keepdims=True)

            @pl.when(i == count - 1)
            def _():
                dk_ref[...] = dk_acc[...]
                dv_ref[...] = dv_acc[...]
                ddrow_ref[...] = ddrow_acc[...]

        tile = pl.BlockSpec((tq, HEAD_DIM), lambda h, i, first=first: (first + i, h))
        col = pl.BlockSpec((None, tq, 1), lambda h, i, first=first: (h, first + i, 0))
        seen = pl.BlockSpec((keys, HEAD_DIM), lambda h, i: (0, h))
        seen_v = pl.BlockSpec((keys, HEAD_DIM), lambda h, i: (0, nh + h))
        seen_row = pl.BlockSpec((None, 1, keys), lambda h, i: (h, 0, 0))
        carried = [] if first_call else [dq, dk, dv, ddrow]
        carried_specs = [] if first_call else [pl.BlockSpec(memory_space=pl.ANY), seen, seen, seen_row]
        dq, dk, dv, ddrow = pl.pallas_call(
            body, name=f"attn_bwd_{first}", grid=(nh, count),
            in_specs=[tile, seen, seen_v, col, seen_row, col, tile] + carried_specs,
            out_specs=[tile, seen, seen, seen_row],
            out_shape=[jax.ShapeDtypeStruct((t, d), BF16), jax.ShapeDtypeStruct((t, d), F32),
                       jax.ShapeDtypeStruct((t, d), F32), jax.ShapeDtypeStruct((nh, 1, t), F32)],
            scratch_shapes=[pltpu.VMEM((keys, HEAD_DIM), F32), pltpu.VMEM((keys, HEAD_DIM), F32),
                            pltpu.VMEM((1, keys), F32)],
            input_output_aliases={} if first_call else {7: 0, 8: 1, 9: 2, 10: 3},
            compiler_params=_params(("parallel", "arbitrary")),
        )(q, kv, kv, dcol, drow, lse, do, *carried)
    return dq, dk, dv, ddrow


def _my_index():
    return (lax.axis_index("x") * 2 + lax.axis_index("y")) * 2 + lax.axis_index("c")


def all_gather(name, src, deps=()):
    def body(src_ref, *rest):
        out_ref, send_sems, recv_sems, local_sem = rest[len(deps):]
        x, y, c = (lax.axis_index(a) for a in MESH_AXES)
        me = (x * 2 + y) * 2 + c
        local = pltpu.make_async_copy(src_ref, out_ref.at[me], local_sem)
        local.start()
        copies = []
        for dlt in range(1, N_DEV):
            copies.append(pltpu.make_async_remote_copy(
                src_ref=src_ref, dst_ref=out_ref.at[me], send_sem=send_sems.at[dlt - 1],
                recv_sem=recv_sems.at[dlt - 1], device_id=(x ^ (dlt // 4), y ^ ((dlt // 2) % 2), c ^ (dlt % 2)),
                device_id_type=pl.DeviceIdType.MESH))
        for cp in copies:
            cp.start()
        for cp in copies:
            cp.wait_recv()
        for cp in copies:
            cp.wait_send()
        local.wait()

    return pl.pallas_call(
        body, name=name, out_shape=jax.ShapeDtypeStruct((N_DEV,) + tuple(src.shape), src.dtype),
        in_specs=[pl.BlockSpec(memory_space=pl.ANY)] * (1 + len(deps)), out_specs=pl.BlockSpec(memory_space=pl.ANY),
        scratch_shapes=[pltpu.SemaphoreType.DMA((N_DEV - 1,)), pltpu.SemaphoreType.DMA((N_DEV - 1,)),
                        pltpu.SemaphoreType.DMA],
        compiler_params=pltpu.CompilerParams(has_side_effects=True),
    )(src, *deps)


_HBM = pl.BlockSpec(memory_space=pltpu.HBM)
_SEM = pl.BlockSpec(memory_space=pltpu.SEMAPHORE)
_DATAFLOW = pltpu.SideEffectType.DATAFLOW_SIDE_EFFECTING


def _peer_copies(src_ref, land_ref, send_sems, recv_sems):
    x, y, c = (lax.axis_index(a) for a in MESH_AXES)
    me = (x * 2 + y) * 2 + c
    copies = []
    for dlt in range(1, N_DEV):
        px, py, pc = x ^ (dlt // 4), y ^ ((dlt // 2) % 2), c ^ (dlt % 2)
        peer = (px * 2 + py) * 2 + pc
        copies.append(pltpu.make_async_remote_copy(
            src_ref=src_ref.at[peer], dst_ref=land_ref.at[me],
            send_sem=send_sems.at[dlt - 1], recv_sem=recv_sems.at[dlt - 1],
            device_id=(px, py, pc), device_id_type=pl.DeviceIdType.MESH))
    return copies


def scatter_start(name, srcs):
    n = len(srcs)
    lands = [lax.empty(s.shape, s.dtype) for s in srcs]

    def body(*refs):
        src_refs, land_refs = refs[:n], refs[n:2 * n]
        send_sems, recv_sems = refs[2 * n:3 * n], refs[3 * n:4 * n]
        token = refs[-1]
        for j in range(n):
            for cp in _peer_copies(src_refs[j], land_refs[j], send_sems[j], recv_sems[j]):
                cp.start()
        token[...] = jnp.zeros_like(token)

    sems = [pltpu.SemaphoreType.DMA((N_DEV - 1,))] * (2 * n)
    thru = [pltpu.HBM(a.shape, a.dtype) for a in list(srcs) + lands]
    outs = pl.pallas_call(
        body, name=name, out_shape=tuple(sems + thru + [jax.ShapeDtypeStruct((8, 128), F32)]),
        in_specs=[_HBM] * (2 * n), out_specs=tuple([_SEM] * (2 * n) + [_HBM] * (2 * n) + [pl.BlockSpec(memory_space=pltpu.VMEM)]),
        input_output_aliases={j: 2 * n + j for j in range(2 * n)},
        compiler_params=pltpu.CompilerParams(has_side_effects=_DATAFLOW),
    )(*[pltpu.with_memory_space_constraint(a, pltpu.HBM) for a in list(srcs) + lands])
    handles = [(outs[j], outs[n + j], outs[2 * n + j], outs[3 * n + j]) for j in range(n)]
    return handles, outs[-1]


def scatter_wait(name, handle, after):
    send_sems, recv_sems, src, land = handle

    def body(src_ref, land_ref, send_ref, recv_ref, after_ref, src_out, land_out):
        for cp in _peer_copies(src_ref, land_ref, send_ref, recv_ref):
            cp.wait_send()
            cp.wait_recv()

    return pl.pallas_call(
        body, name=name, out_shape=(pltpu.HBM(src.shape, src.dtype), pltpu.HBM(land.shape, land.dtype)),
        in_specs=[_HBM, _HBM, _SEM, _SEM, pl.BlockSpec(memory_space=pl.ANY)], out_specs=(_HBM, _HBM),
        input_output_aliases={0: 0, 1: 1},
        compiler_params=pltpu.CompilerParams(has_side_effects=_DATAFLOW),
    )(src, land, send_sems, recv_sems, after)


N_OTHER_CHIPS = 3


def _two_level_places():
    x, y, c = (lax.axis_index(a) for a in MESH_AXES)
    return (x, y, c), (x * 2 + y) * 2 + c, (x, y, 1 - c), [(1 - x, y), (x, 1 - y), (1 - x, 1 - y)]


def _first_copies(land_ref, send_sems, recv_sems):
    (x, y, c), me, other_core, chips = _two_level_places()
    targets = [other_core] + [(cx, cy, c) for cx, cy in chips]
    return [pltpu.make_async_remote_copy(
        src_ref=land_ref.at[me], dst_ref=land_ref.at[me], send_sem=send_sems.at[k], recv_sem=recv_sems.at[k],
        device_id=to, device_id_type=pl.DeviceIdType.MESH) for k, to in enumerate(targets)]


def _passed_on_copies(land_ref, send_sems, recv_sems):
    (x, y, c), me, other_core, chips = _two_level_places()
    copies = []
    for k, (cx, cy) in enumerate(chips):
        slot = land_ref.at[(cx * 2 + cy) * 2 + c]
        copies.append(pltpu.make_async_remote_copy(
            src_ref=slot, dst_ref=slot, send_sem=send_sems.at[k], recv_sem=recv_sems.at[k],
            device_id=other_core, device_id_type=pl.DeviceIdType.MESH))
    return copies


def gather_start(name, lands):
    n = len(lands)

    def body(*refs):
        land_refs, send_sems, recv_sems = refs[:n], refs[n:2 * n], refs[2 * n:3 * n]
        for j in range(n):
            for cp in _first_copies(land_refs[j], send_sems[j], recv_sems[j]):
                cp.start()

    sems = [pltpu.SemaphoreType.DMA((1 + N_OTHER_CHIPS,))] * (2 * n)
    outs = pl.pallas_call(
        body, name=name, out_shape=tuple(sems + [pltpu.HBM(a.shape, a.dtype) for a in lands]),
        in_specs=[_HBM] * n, out_specs=tuple([_SEM] * (2 * n) + [_HBM] * n),
        input_output_aliases={j: 2 * n + j for j in range(n)},
        compiler_params=pltpu.CompilerParams(has_side_effects=_DATAFLOW),
    )(*[pltpu.with_memory_space_constraint(a, pltpu.HBM) for a in lands])
    return [[outs[j], outs[n + j], outs[2 * n + j]] for j in range(n)]


def gather_pass_on(name, handle, after):
    send_sems, recv_sems, land = handle

    def body(land_ref, recv_ref, after_ref, land_out, send2, recv2, token):
        arrivals = _first_copies(land_ref, recv_ref, recv_ref)
        for k, cp in enumerate(_passed_on_copies(land_ref, send2, recv2)):
            arrivals[1 + k].wait_recv()
            cp.start()
        token[...] = jnp.zeros_like(token)

    sem3 = pltpu.SemaphoreType.DMA((N_OTHER_CHIPS,))
    land, send2, recv2, token = pl.pallas_call(
        body, name=name,
        out_shape=(pltpu.HBM(land.shape, land.dtype), sem3, sem3, jax.ShapeDtypeStruct((8, 128), F32)),
        in_specs=[_HBM, _SEM, pl.BlockSpec(memory_space=pl.ANY)],
        out_specs=(_HBM, _SEM, _SEM, pl.BlockSpec(memory_space=pltpu.VMEM)),
        input_output_aliases={0: 0}, compiler_params=pltpu.CompilerParams(has_side_effects=_DATAFLOW),
    )(land, recv_sems, after)
    return [send_sems, recv_sems, land, send2, recv2], token


def gather_wait(name, handle, after):
    send_sems, recv_sems, land, send2, recv2 = handle

    def body(land_ref, send_ref, recv_ref, send2_ref, recv2_ref, after_ref, land_out):
        first = _first_copies(land_ref, send_ref, recv_ref)
        for cp in first:
            cp.wait_send()
        first[0].wait_recv()
        for cp in _passed_on_copies(land_ref, send2_ref, recv2_ref):
            cp.wait_send()
            cp.wait_recv()

    return pl.pallas_call(
        body, name=name, out_shape=pltpu.HBM(land.shape, land.dtype),
        in_specs=[_HBM, _SEM, _SEM, _SEM, _SEM, pl.BlockSpec(memory_space=pl.ANY)], out_specs=_HBM,
        input_output_aliases={0: 0}, compiler_params=pltpu.CompilerParams(has_side_effects=_DATAFLOW),
    )(land, send_sems, recv_sems, send2, recv2, after)


N_CHIPS = 4


def _pair_copies(g_ref, half_ref, send_sems, recv_sems):
    (x, y, c), me, other_core, chips = _two_level_places()
    return [pltpu.make_async_remote_copy(
        src_ref=g_ref.at[chip * 2 + (1 - c)], dst_ref=half_ref.at[chip], send_sem=send_sems.at[chip],
        recv_sem=recv_sems.at[chip], device_id=other_core, device_id_type=pl.DeviceIdType.MESH)
        for chip in range(N_CHIPS)]


def pair_start(name, g):
    half = lax.empty((N_CHIPS,) + g.shape[1:], g.dtype)

    def body(g_ref, half_ref, send_sems, recv_sems, g_out, half_out, token):
        for cp in _pair_copies(g_ref, half_ref, send_sems, recv_sems):
            cp.start()
        token[...] = jnp.zeros_like(token)

    sem = pltpu.SemaphoreType.DMA((N_CHIPS,))
    outs = pl.pallas_call(
        body, name=name,
        out_shape=(sem, sem, pltpu.HBM(g.shape, g.dtype), pltpu.HBM(half.shape, half.dtype),
                   jax.ShapeDtypeStruct((8, 128), F32)),
        in_specs=[_HBM, _HBM], out_specs=(_SEM, _SEM, _HBM, _HBM, pl.BlockSpec(memory_space=pltpu.VMEM)),
        input_output_aliases={0: 2, 1: 3}, compiler_params=pltpu.CompilerParams(has_side_effects=_DATAFLOW),
    )(pltpu.with_memory_space_constraint(g, pltpu.HBM), half)
    return list(outs[:4]), outs[4]


def pair_wait(name, handle, after):
    send_sems, recv_sems, g, half = handle

    def body(g_ref, half_ref, send_ref, recv_ref, after_ref, g_out, half_out):
        for cp in _pair_copies(g_ref, half_ref, send_ref, recv_ref):
            cp.wait_send()
            cp.wait_recv()

    return pl.pallas_call(
        body, name=name, out_shape=(pltpu.HBM(g.shape, g.dtype), pltpu.HBM(half.shape, half.dtype)),
        in_specs=[_HBM, _HBM, _SEM, _SEM, pl.BlockSpec(memory_space=pl.ANY)], out_specs=(_HBM, _HBM),
        input_output_aliases={0: 0, 1: 1}, compiler_params=pltpu.CompilerParams(has_side_effects=_DATAFLOW),
    )(g, half, send_sems, recv_sems, after)


def pair_sum(name, g, half):
    _, r, wd = g.shape
    tr = _row_tile(r, 2 * ROW_TILE_BYTES // (4 * wd))
    kind = lax.axis_index("c").astype(jnp.int32).reshape(1)

    def body(kind_ref, g_ref, half_ref, o_ref):
        o_ref[...] = (g_ref[...].astype(F32) + half_ref[...].astype(F32)).astype(o_ref.dtype)

    spec = pl.BlockSpec((None, tr, wd), lambda chip, i, kind_ref: (chip, i, 0))
    return pl.pallas_call(
        body, name=name,
        grid_spec=pltpu.PrefetchScalarGridSpec(
            num_scalar_prefetch=1, grid=(N_CHIPS, r // tr),
            in_specs=[pl.BlockSpec((None, tr, wd), lambda chip, i, kind_ref: (chip * 2 + kind_ref[0], i, 0)), spec],
            out_specs=spec),
        out_shape=jax.ShapeDtypeStruct((N_CHIPS, r, wd), g.dtype),
        compiler_params=_params(("parallel", "parallel")),
    )(kind, g, half)


def _chip_copies(sums_ref, land_ref, send_sems, recv_sems):
    (x, y, c), me, other_core, chips = _two_level_places()
    return [pltpu.make_async_remote_copy(
        src_ref=sums_ref.at[cx * 2 + cy], dst_ref=land_ref.at[x * 2 + y], send_sem=send_sems.at[k],
        recv_sem=recv_sems.at[k], device_id=(cx, cy, c), device_id_type=pl.DeviceIdType.MESH)
        for k, (cx, cy) in enumerate(chips)]


def chip_start(name, sums):
    land = lax.empty(sums.shape, sums.dtype)

    def body(sums_ref, land_ref, send_sems, recv_sems, sums_out, land_out, token):
        for cp in _chip_copies(sums_ref, land_ref, send_sems, recv_sems):
            cp.start()
        token[...] = jnp.zeros_like(token)

    sem = pltpu.SemaphoreType.DMA((N_OTHER_CHIPS,))
    outs = pl.pallas_call(
        body, name=name,
        out_shape=(sem, sem, pltpu.HBM(sums.shape, sums.dtype), pltpu.HBM(land.shape, land.dtype),
                   jax.ShapeDtypeStruct((8, 128), F32)),
        in_specs=[_HBM, _HBM], out_specs=(_SEM, _SEM, _HBM, _HBM, pl.BlockSpec(memory_space=pltpu.VMEM)),
        input_output_aliases={0: 2, 1: 3}, compiler_params=pltpu.CompilerParams(has_side_effects=_DATAFLOW),
    )(pltpu.with_memory_space_constraint(sums, pltpu.HBM), land)
    return list(outs[:4]), outs[4]


def chip_wait(name, handle, after):
    send_sems, recv_sems, sums, land = handle

    def body(sums_ref, land_ref, send_ref, recv_ref, after_ref, sums_out, land_out):
        for cp in _chip_copies(sums_ref, land_ref, send_ref, recv_ref):
            cp.wait_send()
            cp.wait_recv()

    return pl.pallas_call(
        body, name=name, out_shape=(pltpu.HBM(sums.shape, sums.dtype), pltpu.HBM(land.shape, land.dtype)),
        in_specs=[_HBM, _HBM, _SEM, _SEM, pl.BlockSpec(memory_space=pl.ANY)], out_specs=(_HBM, _HBM),
        input_output_aliases={0: 0, 1: 1}, compiler_params=pltpu.CompilerParams(has_side_effects=_DATAFLOW),
    )(sums, land, send_sems, recv_sems, after)


def adamw_reduce(name, parts, w, m, v):
    nl, r, wd = w.shape
    tr = _row_tile(r, ROW_TILE_BYTES // (8 * wd))

    def body(*refs):
        p_refs = refs[:nl]
        w_ref, m_ref, v_ref, g_ref, d_ref, nm_ref, nv_ref = refs[nl:]
        layer = pl.program_id(0)
        for j in range(nl):
            @pl.when(layer == j)
            def _(j=j):
                g = p_refs[j][0].astype(F32)
                for sender in range(1, p_refs[j].shape[0]):
                    g = g + p_refs[j][sender].astype(F32)
                nm = B1 * m_ref[...] + (1.0 - B1) * g
                nv = B2 * v_ref[...] + (1.0 - B2) * jnp.square(g)
                m_hat = nm / (1.0 - B1 ** STEP)
                v_hat = nv / (1.0 - B2 ** STEP)
                g_ref[...] = g
                d_ref[...] = -LR * (m_hat / (jnp.sqrt(v_hat) + ADAM_EPS) + WD * w_ref[...])
                nm_ref[...] = nm
                nv_ref[...] = nv

    def part_spec(j):
        return pl.BlockSpec((parts[j].shape[0], tr, wd), lambda l, i: (0, jnp.where(l == j, i, 0), 0))

    spec = pl.BlockSpec((None, tr, wd), lambda l, i: (l, i, 0))
    return pl.pallas_call(
        body, name=name, grid=(nl, r // tr),
        in_specs=[part_spec(j) for j in range(nl)] + [spec, spec, spec],
        out_specs=[spec] * 4, out_shape=[jax.ShapeDtypeStruct((nl, r, wd), F32)] * 4,
        compiler_params=_params(("arbitrary", "arbitrary")),
    )(*parts, w, m, v)


def _pack_rows(vectors, rows=None):
    flat = jnp.concatenate([a.reshape(-1).astype(F32) for a in vectors])
    n = flat.shape[0]
    if rows is None:
        rows = -(-n // 1024) * 8
    return jnp.pad(flat, (0, rows * 128 - n)).reshape(rows, 128)


def _unpack_rows(packed, like):
    flat = packed.reshape(-1)
    out, pos = [], 0
    for a in like:
        out.append(flat[pos:pos + a.size].reshape(a.shape))
        pos += a.size
    return out


def kernel(x, p, mix_norm, mlp_norm, ple_norm, w_a_in, a_lb_logits, a_head_gain, w_a_out, kv_norm, w_kvf, b_f, w_b_q, w_b_out, w_mlp_up, w_mlp_down, w_ple_gate, w_ple_up, final_norm, loss_target, m_mix_norm, m_mlp_norm, m_ple_norm, m_w_a_in, m_a_lb_logits, m_a_head_gain, m_w_a_out, m_kv_norm, m_w_kvf, m_b_f, m_w_b_q, m_w_b_out, m_w_mlp_up, m_w_mlp_down, m_w_ple_gate, m_w_ple_up, m_final_norm, v_mix_norm, v_mlp_norm, v_ple_norm, v_w_a_in, v_a_lb_logits, v_a_head_gain, v_w_a_out, v_kv_norm, v_w_kvf, v_b_f, v_w_b_q, v_w_b_out, v_w_mlp_up, v_w_mlp_down, v_w_ple_gate, v_w_ple_up, v_final_norm):
    t, d = x.shape[1], x.shape[2]
    nh = d // HEAD_DIM
    n_layers = 2
    x2 = x.reshape(t, d)
    target = loss_target.reshape(t, d)
    me = _my_index()

    shards = {"w_a_in": w_a_in[0], "w_a_out": w_a_out[0], "w_kvf": w_kvf, "w_b_q": w_b_q[0], "w_b_out": w_b_out[0]}
    for l in range(n_layers):
        shards.update({f"w_mlp_up{l}": w_mlp_up[l], f"w_mlp_down{l}": w_mlp_down[l],
                       f"w_ple_gate{l}": w_ple_gate[l], f"w_ple_up{l}": w_ple_up[l]})
    first_use = ["a_lb_logits", "w_a_in", "w_a_out", "w_mlp_up0", "w_mlp_down0", "w_ple_gate0", "w_ple_up0", "w_kvf",
                 "w_b_q", "w_b_out", "w_mlp_up1", "w_mlp_down1", "w_ple_gate1", "w_ple_up1"]
    row_sharded = ("w_a_out", "w_b_q", "w_b_out", "w_mlp_down", "w_ple_gate")
    shards_bf = [a_lb_logits] + [shards[n].astype(BF16) for n in first_use[1:]]
    ag_handles = gather_start("ag_start", [
        lax.dynamic_update_slice(lax.empty((N_DEV,) + a.shape, a.dtype), a[None], (me, 0, 0)) for a in shards_bf])
    passed_on = {}
    weights = {}

    def pass_on(j, after):
        if j < len(first_use) and j not in passed_on:
            passed_on[j] = gather_pass_on("ag_pass_" + first_use[j], ag_handles[j], after)

    def weight(name, after=None):
        if name not in weights:
            j = first_use.index(name)
            pass_on(j, after)
            pass_on(j + 1, after)
            behind = passed_on[j + 1][1] if j + 1 in passed_on else after
            g = gather_wait("ag_wait_" + name, passed_on[j][0], behind)
            if name.rstrip("01") in row_sharded:
                g = g.reshape(1, g.shape[0] * g.shape[1], g.shape[2])
            weights[name] = g
        return weights[name]

    lgt = weight("a_lb_logits", x2).transpose(1, 0, 2).reshape(2, d)
    p_bf = [p[l, 0].astype(BF16) for l in range(n_layers)]

    def row(vec):
        return vec.reshape(1, -1)

    def mlp_ple_fwd(l, h_in, a):
        (h_a, u_mlp), _ = rowwise(f"add_norm_mlp{l}", _add_norm_fwd, [h_in, a], [row(mlp_norm[l])])
        pre, act = mm_nn(f"mlp_up{l}", u_mlp, weight(f"w_mlp_up{l}", u_mlp), fuse=(_relu2, (), (BF16, BF16)))
        mo = mm_nn(f"mlp_down{l}", act, weight(f"w_mlp_down{l}", act))
        (h_b, u_ple), _ = rowwise(f"add_norm_ple{l}", _add_norm_fwd, [h_a, mo], [row(ple_norm[l])])
        gpre = mm_nn(f"ple_gate{l}", u_ple, weight(f"w_ple_gate{l}", u_ple))
        pu = mm_nn(f"ple_up{l}", p_bf[l], weight(f"w_ple_up{l}", gpre))
        return dict(h_a=h_a, u_mlp=u_mlp, pre=pre, act=act, h_b=h_b, u_ple=u_ple, gpre=gpre, pu=pu)

    (u0,), _ = rowwise("norm_mix0", _norm_fwd, [x2], [row(mix_norm[0])])
    z = mm_nn("a_in", u0, weight("w_a_in", u0))
    og, states = hgrn_fwd(z, lgt, a_head_gain)
    a0 = mm_nn("a_out", og, weight("w_a_out", og))
    s0 = mlp_ple_fwd(0, x2, a0)
    (h3, u_kv, u1), _ = rowwise("ple_norms", _ple_two_norms_fwd, [s0["h_b"], s0["gpre"], s0["pu"]],
                                [row(kv_norm), row(mix_norm[1])])
    w_kvf_cols = weight("w_kvf", u_kv).transpose(1, 0, 2).reshape(d, 2 * d + nh)
    w_kv = w_kvf_cols[:, :2 * d].reshape(d, 2, d).transpose(1, 0, 2)
    w_f = jnp.pad(w_kvf_cols[:, 2 * d:], ((0, 0), (0, HEAD_DIM - nh)))[None]
    kv = mm_nn("kvf", u_kv, w_kv, out_dtype=BF16)
    fl_t = mm_nn("kvf_forget", u_kv, w_f)[:, :nh].T
    b_f_col = b_f.reshape(nh, 1)
    dcum = decay_fwd(fl_t, b_f_col)
    dcol, drow = dcum.reshape(nh, t, 1), dcum.reshape(nh, 1, t)
    q = mm_nn("b_q", u1, weight("w_b_q", dcum))
    o, lse = attn_fwd(q, kv, dcol, drow)
    a1 = mm_nn("b_out", o, weight("w_b_out", o))
    s1 = mlp_ple_fwd(1, h3, a1)

    (dh, dgpre, dpu), (d_final, loss_rows) = rowwise(
        "tail", _tail_fwd_bwd, [s1["h_b"], s1["gpre"], s1["pu"], target], [row(final_norm)])

    sent = {}
    tokens = []

    two_level = ("w_mlp_up0", "w_mlp_up1", "w_mlp_down0", "w_mlp_down1", "w_a_in", "w_kvf")
    swapping = []

    def send_grad(name, g):
        g = g.reshape(N_DEV, -1, g.shape[-1])
        if name in two_level:
            sent[name], token = pair_start("rs_pair_" + name, g)
            swapping.append(name)
        else:
            (sent[name],), token = scatter_start("rs_start_" + name, [g])
        tokens.append(token)

    def second_stage(after):
        for name in swapping:
            g, half = pair_wait("rs_pairwait_" + name, sent[name], after)
            sent[name], token = chip_start("rs_chip_" + name, pair_sum("rs_sum_" + name, g, half))
            tokens.append(token)
        swapping.clear()

    def after_sends():
        deps = tuple(tokens)
        tokens.clear()
        return deps

    def mlp_ple_bwd(l, s, dh, dgpre, dpu):
        send_grad(f"w_ple_gate{l}", mm_tn(f"d_ple_gate_w{l}", s["u_ple"], dgpre, 1, deps=after_sends()))
        send_grad(f"w_ple_up{l}", mm_tn(f"d_ple_up_w{l}", p_bf[l], dpu, N_DEV, deps=after_sends()))
        du = mm_nt(f"d_ple_gate_x{l}", dgpre, weight(f"w_ple_gate{l}"), deps=after_sends())
        (dh, dh_bf), (d_ple,) = rowwise(f"d_norm_ple{l}", _norm_bwd, [s["h_b"], du, dh], [row(ple_norm[l])])
        send_grad(f"w_mlp_down{l}", mm_tn(f"d_mlp_down_w{l}", s["act"], dh_bf, 1))
        (dpre,) = mm_nt(f"d_mlp_down_x{l}", dh_bf, weight(f"w_mlp_down{l}"), deps=after_sends(),
                        fuse=(_relu2_bwd, (s["pre"],), (BF16,)))
        second_stage(dpre)
        send_grad(f"w_mlp_up{l}", mm_tn(f"d_mlp_up_w{l}", s["u_mlp"], dpre, N_DEV))
        du = mm_nt(f"d_mlp_up_x{l}", dpre, weight(f"w_mlp_up{l}"), deps=after_sends())
        second_stage(du)
        (dh, dh_bf), (d_mlp,) = rowwise(f"d_norm_mlp{l}", _norm_bwd, [s["h_a"], du, dh], [row(mlp_norm[l])])
        return dh, dh_bf, d_ple, d_mlp

    dh, dh_bf, d_ple1, d_mlp1 = mlp_ple_bwd(1, s1, dh, dgpre, dpu)
    send_grad("w_b_out", mm_tn("d_b_out_w", o, dh_bf, 1))
    do = mm_nt("d_b_out_x", dh_bf, weight("w_b_out"), out_dtype=BF16, deps=after_sends())
    dq, dk, dv, ddrow = attn_bwd(q, kv, dcol, drow, lse, do)
    send_grad("w_b_q", mm_tn("d_b_q_w", u1, dq, 1))
    du1 = mm_nt("d_b_q_x", dq, weight("w_b_q"), deps=after_sends())
    dfl_t, d_b_f = decay_bwd(fl_t, b_f_col, ddrow.reshape(nh, t))
    dkv = jnp.stack([dk, dv]).astype(BF16)
    dfl = jnp.pad(dfl_t.T, ((0, 0), (0, HEAD_DIM - nh))).astype(BF16)
    d_w_kv = mm_tn("d_kvf_w", u_kv, dkv, 2)
    d_w_f = mm_tn("d_kvf_forget_w", u_kv, dfl, 1)
    d_w_kvf = jnp.concatenate([d_w_kv[0], d_w_kv[1], d_w_f[0, :, :nh]], axis=1)
    send_grad("w_kvf", d_w_kvf.reshape(d, N_DEV, -1).transpose(1, 0, 2))
    du_f = mm_nt("d_kvf_forget_x", dfl, w_f, deps=after_sends())
    (du_kv,) = mm_nt("d_kvf_x", dkv, w_kv, fuse=(lambda acc, extra: (acc + extra,), (du_f,), (F32,)))
    second_stage(du_kv)
    (dh, dgpre, dpu), (d_kv_norm, d_mix1) = rowwise(
        "d_ple_norms", _two_norms_ple_bwd, [h3, du_kv, du1, dh, s0["gpre"], s0["pu"]],
        [row(kv_norm), row(mix_norm[1])])
    dh, dh_bf, d_ple0, d_mlp0 = mlp_ple_bwd(0, s0, dh, dgpre, dpu)
    send_grad("w_a_out", mm_tn("d_a_out_w", og, dh_bf, 1))
    dog = mm_nt("d_a_out_x", dh_bf, weight("w_a_out"), deps=after_sends())
    dz4, d_lgt, d_hg = hgrn_bwd(z, lgt, a_head_gain, states, dog)
    send_grad("w_a_in", mm_tn("d_a_in_w", u0, dz4, N_DEV, stacked=True))
    du0 = mm_nt("d_a_in_x", dz4, weight("w_a_in"), deps=after_sends(), stacked=True)
    second_stage(du0)
    (dx, _), (d_mix0,) = rowwise("d_norm_mix0", _norm_bwd, [x2, du0, dh], [row(mix_norm[0])])

    new = {}
    last = [dx]

    def update(name, parts, w, m, v):
        shp = w.shape
        w3, m3, v3 = (a.reshape(len(parts), -1, shp[-1]) for a in (w, m, v))
        new[name] = tuple(a.reshape(shp) for a in adamw_reduce("adamw_" + name, parts, w3, m3, v3))
        last[0] = new[name][0]

    def receive_update(name, layers, w, m, v):
        parts = {}
        for sfx in layers:
            if name + sfx in two_level:
                mine, land = chip_wait(f"rs_wait_{name}{sfx}", sent[name + sfx], last[0])
                slot = me // 2
            else:
                mine, land = scatter_wait(f"rs_wait_{name}{sfx}", sent[name + sfx], last[0])
                slot = me
            parts[sfx] = lax.dynamic_update_slice(land, lax.dynamic_slice_in_dim(mine, slot, 1, 0), (slot, 0, 0))
        update(name, [parts[sfx] for sfx in sorted(layers)], w, m, v)

    both = ("1", "0")
    receive_update("w_b_out", ("",), w_b_out, m_w_b_out, v_w_b_out)
    receive_update("w_b_q", ("",), w_b_q, m_w_b_q, v_w_b_q)
    receive_update("w_kvf", ("",), w_kvf, m_w_kvf, v_w_kvf)
    receive_update("w_ple_gate", both, w_ple_gate, m_w_ple_gate, v_w_ple_gate)
    receive_update("w_ple_up", both, w_ple_up, m_w_ple_up, v_w_ple_up)
    receive_update("w_mlp_down", both, w_mlp_down, m_w_mlp_down, v_w_mlp_down)
    receive_update("w_mlp_up", both, w_mlp_up, m_w_mlp_up, v_w_mlp_up)
    receive_update("w_a_out", ("",), w_a_out, m_w_a_out, v_w_a_out)
    receive_update("w_a_in", ("",), w_a_in, m_w_a_in, v_w_a_in)

    small = dict(mix_norm=jnp.concatenate([d_mix0, d_mix1]), mlp_norm=jnp.concatenate([d_mlp0, d_mlp1]),
                 ple_norm=jnp.concatenate([d_ple0, d_ple1]), a_head_gain=d_hg, kv_norm=d_kv_norm.reshape(d),
                 b_f=d_b_f.reshape(nh), final_norm=d_final.reshape(d))
    small_w = dict(mix_norm=(mix_norm, m_mix_norm, v_mix_norm), mlp_norm=(mlp_norm, m_mlp_norm, v_mlp_norm),
                   ple_norm=(ple_norm, m_ple_norm, v_ple_norm),
                   a_head_gain=(a_head_gain, m_a_head_gain, v_a_head_gain), kv_norm=(kv_norm, m_kv_norm, v_kv_norm),
                   b_f=(b_f, m_b_f, v_b_f), final_norm=(final_norm, m_final_norm, v_final_norm))
    names = list(small)
    packed = _pack_rows([d_lgt] + [small[n] for n in names])
    everyone = all_gather("ag_small_grads", packed, deps=(last[0],))
    n_lgt_rows = d_lgt.size // 128
    lgt_parts = everyone[:, :n_lgt_rows].reshape(N_DEV, 2, d)
    lgt_parts = lax.dynamic_slice_in_dim(lgt_parts, me * a_lb_logits.shape[1], a_lb_logits.shape[1], axis=2)
    update("a_lb_logits", [lgt_parts], a_lb_logits, m_a_lb_logits, v_a_lb_logits)
    rest = everyone[:, n_lgt_rows:]
    like = [small_w[n][0] for n in names]
    packed_w, packed_m, packed_v = (_pack_rows([small_w[n][j] for n in names], rest.shape[1])[None] for j in range(3))
    outs = adamw_reduce("adamw_small", [rest], packed_w, packed_m, packed_v)
    unpacked = [_unpack_rows(a, like) for a in outs]
    for j, n in enumerate(names):
        new[n] = tuple(unpacked[q][j] for q in range(4))

    order = ["mix_norm", "mlp_norm", "ple_norm", "w_a_in", "a_lb_logits", "a_head_gain", "w_a_out", "kv_norm",
             "w_kvf", "b_f", "w_b_q", "w_b_out", "w_mlp_up", "w_mlp_down", "w_ple_gate", "w_ple_up", "final_norm"]
    loss_here, _ = lax.optimization_barrier((loss_rows[0, 0], new["final_norm"][0]))
    loss = lax.psum(loss_here, MESH_AXES)
    result = [loss, dx.reshape(x.shape)]
    for j in range(4):
        result += [new[n][j] for n in order]
    return tuple(result)
```

```python
import jax
import jax.numpy as jnp
from jax import lax
from jax.experimental import pallas as pl
from jax.experimental.pallas import tpu as pltpu

F32 = jnp.float32
BF16 = jnp.bfloat16
HEAD_DIM = 128
CHUNK = 16
TILE = 128
HEADS_PER_STEP = 4
NORM_EPS = 1e-6
N_DEV = 8
MESH_AXES = ("x", "y", "c")
VMEM_LIMIT_BYTES = 48 * 1024 * 1024
ROW_TILE_BYTES = 2 * 1024 * 1024
LR, B1, B2, ADAM_EPS, WD, STEP = 0.001, 0.9, 0.999, 1e-08, 0.01, 10
NEG_BIG = -1e30

NN = (((1,), (0,)), ((), ()))
NT = (((1,), (1,)), ((), ()))
TN = (((0,), (0,)), ((), ()))


def _params(semantics):
    return pltpu.CompilerParams(dimension_semantics=semantics, vmem_limit_bytes=VMEM_LIMIT_BYTES)


def _tile(n, prefs):
    for p in prefs:
        if n % p == 0:
            return p
    return n


def _row_tile(rows, limit):
    for cand in (2048, 1024, 512, 256, 128, 64, 32, 16):
        if cand <= limit and rows % cand == 0:
            return cand
    return rows


def _mm_call(name, a, b, dims, grid, a_spec, b_spec, o_spec, o_shape, acc_shape, k_axes, out_dtype, deps=(),
             fuse=None, split=1):
    nk = 1
    for ax in k_axes:
        nk *= grid[ax]
    fn, extra, out_dtypes = fuse if fuse else (lambda acc: (acc,), (), (out_dtype,))
    n_extra, n_out = len(extra), len(out_dtypes)

    def finish(acc, rest):
        o_refs = rest[n_extra + len(deps):n_extra + len(deps) + n_out]
        for ref, val in zip(o_refs, fn(acc, *[r[...] for r in rest[:n_extra]])):
            ref[...] = val.astype(ref.dtype)

    def product(a_ref, b_ref):
        if split == 1:
            return lax.dot_general(a_ref[...], b_ref[...], dims, preferred_element_type=F32)
        wide = a_ref.shape[1] // split
        return sum(lax.dot_general(a_ref[:, q * wide:(q + 1) * wide], b_ref[q], dims, preferred_element_type=F32)
                   for q in range(split))

    def one_step(a_ref, b_ref, *rest):
        finish(product(a_ref, b_ref), rest)

    def accumulate(a_ref, b_ref, *rest):
        acc_ref = rest[-1]
        k = 0
        for ax in k_axes:
            k = k * grid[ax] + pl.program_id(ax)
        part = product(a_ref, b_ref)

        @pl.when(k == 0)
        def _():
            acc_ref[...] = part

        @pl.when((k > 0) & (k < nk - 1))
        def _():
            acc_ref[...] += part

        @pl.when(k == nk - 1)
        def _():
            finish(acc_ref[...] + part, rest)

    sem = tuple("arbitrary" if ax in k_axes else "parallel" for ax in range(len(grid)))
    outs = pl.pallas_call(
        one_step if nk == 1 else accumulate, name=name, grid=grid,
        in_specs=[a_spec, b_spec] + [o_spec] * n_extra + [pl.BlockSpec(memory_space=pl.ANY)] * len(deps),
        out_specs=[o_spec] * n_out, out_shape=[jax.ShapeDtypeStruct(o_shape, dt) for dt in out_dtypes],
        scratch_shapes=[] if nk == 1 else [pltpu.VMEM(acc_shape, F32)], compiler_params=_params(sem),
    )(a, b, *extra, *deps)
    return outs if fuse else outs[0]


def mm_nn(name, a, b3, out_dtype=F32, out3=False, deps=(), fuse=None):
    m, k = a.shape
    g, _, n = b3.shape
    tm, tk = _tile(m, (1024, 512, 256)), _tile(k, (2048, 1024, 512, 256))
    tn = n if out3 else _tile(n, (1024, 512, 256, 128))
    nj = n // tn
    grid = (m // tm, g, nj, k // tk)
    a_spec = pl.BlockSpec((tm, tk), lambda i, gg, j, kk: (i, kk))
    b_spec = pl.BlockSpec((None, tk, tn), lambda i, gg, j, kk: (gg, kk, j))
    if out3:
        o_spec = pl.BlockSpec((None, tm, tn), lambda i, gg, j, kk: (gg, i, j))
        o_shape = (g, m, n)
    else:
        o_spec = pl.BlockSpec((tm, tn), lambda i, gg, j, kk: (i, gg * nj + j))
        o_shape = (m, g * n)
    return _mm_call(name, a, b3, NN, grid, a_spec, b_spec, o_spec, o_shape, (tm, tn), (3,), out_dtype, deps, fuse)


def mm_nt(name, a, b3, out_dtype=F32, deps=(), fuse=None, stacked=False):
    g, k, n = b3.shape
    a3 = a.ndim == 3 and not stacked
    m = a.shape[1] if a.ndim == 3 else a.shape[0]
    tm, tko = _tile(m, (1024, 512, 256)), _tile(k, (1024, 512, 256))
    tc = n if a3 else _tile(n, (2048, 1024, 512, 256, 128))
    nc = n // tc
    per = g // a.shape[0] if stacked else g
    pair = 2 if (not a3 and nc == 1 and per % 2 == 0 and tc <= 1024) else 1
    grid = (m // tm, k // tko, g // pair, nc)
    if a3:
        a_spec = pl.BlockSpec((None, tm, tc), lambda i, j, gg, c: (gg, i, c))
    elif stacked:
        a_spec = pl.BlockSpec((None, tm, pair * tc),
                              lambda i, j, gg, c: ((gg * pair) // per, i, (((gg * pair) % per) // pair) * nc + c))
    else:
        a_spec = pl.BlockSpec((tm, pair * tc), lambda i, j, gg, c: (i, gg * nc + c))
    if pair == 1:
        b_spec = pl.BlockSpec((None, tko, tc), lambda i, j, gg, c: (gg, j, c))
    else:
        b_spec = pl.BlockSpec((pair, tko, tc), lambda i, j, gg, c: (gg, j, c))
    o_spec = pl.BlockSpec((tm, tko), lambda i, j, gg, c: (i, j))
    return _mm_call(name, a, b3, NT, grid, a_spec, b_spec, o_spec, (m, k), (tm, tko), (2, 3), out_dtype, deps, fuse,
                    pair)


def mm_tn(name, a, b, g, out_dtype=BF16, deps=(), stacked=False):
    t, k = a.shape
    b3 = b.ndim == 3 and not stacked
    n = b.shape[2] if b3 else (b.shape[0] * b.shape[2] if stacked else b.shape[1]) // g
    tm = _tile(k, (1024, 512, 256))
    tn = n if b3 else _tile(n, (1024, 512, 256, 128))
    tt = _tile(t, (2048, 1024, 512, 256))
    nj = n // tn
    grid = (g, k // tm, nj, t // tt)
    a_spec = pl.BlockSpec((tt, tm), lambda gg, i, j, s: (s, i))
    if b3:
        b_spec = pl.BlockSpec((None, tt, tn), lambda gg, i, j, s: (gg, s, j))
    elif stacked:
        per = g // b.shape[0]
        b_spec = pl.BlockSpec((None, tt, tn), lambda gg, i, j, s: (gg // per, s, (gg % per) * nj + j))
    else:
        b_spec = pl.BlockSpec((tt, tn), lambda gg, i, j, s: (s, gg * nj + j))
    o_spec = pl.BlockSpec((None, tm, tn), lambda gg, i, j, s: (gg, i, j))
    return _mm_call(name, a, b, TN, grid, a_spec, b_spec, o_spec, (g, k, n), (tm, tn), (3,), out_dtype, deps)


def rowwise(name, fn, rows, vecs=()):
    t = rows[0].shape[0]
    wmax = max(r.shape[1] for r in rows)
    tr = _row_tile(t, ROW_TILE_BYTES // (4 * wmax))
    row_s = [jax.ShapeDtypeStruct((tr, r.shape[1]), r.dtype) for r in rows]
    vec_s = [jax.ShapeDtypeStruct(v.shape, v.dtype) for v in vecs]
    out_rows_s, out_sums_s = jax.eval_shape(fn, *row_s, *vec_s)
    n_in, n_r = len(rows) + len(vecs), len(out_rows_s)

    def body(*refs):
        i = pl.program_id(0)
        o_rows, o_sums = fn(*[r[...] for r in refs[:n_in]])
        for ref, val in zip(refs[n_in:n_in + n_r], o_rows):
            ref[...] = val

        if out_sums_s:
            @pl.when(i == 0)
            def _():
                for ref in refs[n_in + n_r:]:
                    ref[...] = jnp.zeros_like(ref)

            for ref, val in zip(refs[n_in + n_r:], o_sums):
                ref[...] += val

    in_specs = [pl.BlockSpec((tr, r.shape[1]), lambda i: (i, 0)) for r in rows]
    in_specs += [pl.BlockSpec(v.shape, lambda i: (0, 0)) for v in vecs]
    out_specs = [pl.BlockSpec((tr, s.shape[1]), lambda i: (i, 0)) for s in out_rows_s]
    out_specs += [pl.BlockSpec(s.shape, lambda i: (0, 0)) for s in out_sums_s]
    out_shape = [jax.ShapeDtypeStruct((t, s.shape[1]), s.dtype) for s in out_rows_s]
    out_shape += [jax.ShapeDtypeStruct(s.shape, s.dtype) for s in out_sums_s]
    outs = pl.pallas_call(
        body, name=name, grid=(t // tr,), in_specs=in_specs, out_specs=out_specs, out_shape=out_shape,
        compiler_params=_params(("arbitrary",)),
    )(*rows, *vecs)
    return outs[:n_r], outs[n_r:]


def _rms(x, gain):
    return x * lax.rsqrt(jnp.mean(x * x, axis=-1, keepdims=True) + NORM_EPS) * gain


def _norm_fwd(x, gain):
    return (_rms(x, gain).astype(BF16),), ()


def _add_norm_fwd(h, a, gain):
    h = h + a
    return (h, _rms(h, gain).astype(BF16)), ()


def _relu2(pre):
    r = jnp.maximum(pre, 0.0)
    return pre, r * r


def _ple(h, gpre, pu):
    return h + pu * jax.nn.sigmoid(gpre)


def _ple_two_norms_fwd(h, gpre, pu, gain_a, gain_b):
    h = _ple(h, gpre, pu)
    return (h, _rms(h, gain_a).astype(BF16), _rms(h, gain_b).astype(BF16)), ()


def _tail_fwd_bwd(h, gpre, pu, target, gain):
    def row_loss(h, gpre, pu, gain):
        y = _rms(_ple(h, gpre, pu), gain)
        return 0.5 * jnp.mean(jnp.square(y - target), axis=-1, keepdims=True)

    loss, vjp = jax.vjp(row_loss, h, gpre, pu, gain)
    dh, dgpre, dpu, dgain = vjp(jnp.ones_like(loss))
    loss = jnp.broadcast_to(jnp.sum(loss, axis=0, keepdims=True), (1, 128))
    return (dh, dgpre.astype(BF16), dpu.astype(BF16)), (dgain, loss)


def _norm_bwd(h, du, dh_in, gain):
    _, vjp = jax.vjp(_rms, h, gain)
    dh, dgain = vjp(du)
    dh = dh_in + dh
    return (dh, dh.astype(BF16)), (dgain,)


def _two_norms_ple_bwd(h, du_a, du_b, dh_in, gpre, pu, gain_a, gain_b):
    _, vjp = jax.vjp(lambda h, ga, gb: (_rms(h, ga), _rms(h, gb)), h, gain_a, gain_b)
    dh, dga, dgb = vjp((du_a, du_b))
    dh = dh_in + dh
    _, gate_vjp = jax.vjp(lambda g, u: u * jax.nn.sigmoid(g), gpre, pu)
    dgpre, dpu = gate_vjp(dh)
    return (dh, dgpre.astype(BF16), dpu.astype(BF16)), (dga, dgb)


def _relu2_bwd(dact, pre):
    return (dact * 2.0 * jnp.maximum(pre.astype(F32), 0.0),)


def _bf16_dot(dims_fwd, dims_da, dims_db, swap_da, swap_db):
    @jax.custom_vjp
    def dot(a, b):
        return lax.dot_general(a.astype(BF16), b.astype(BF16), dims_fwd, preferred_element_type=F32)

    def fwd(a, b):
        return dot(a, b), (a, b)

    def bwd(res, ct):
        a, b = res
        ct, a, b = ct.astype(BF16), a.astype(BF16), b.astype(BF16)
        da = lax.dot_general(*((b, ct) if swap_da else (ct, b)), dims_da, preferred_element_type=F32)
        db = lax.dot_general(*((ct, a) if swap_db else (a, ct)), dims_db, preferred_element_type=F32)
        return da, db

    dot.defvjp(fwd, bwd)
    return dot


_dot_nn = _bf16_dot(NN, NT, TN, False, False)
_dot_nt = _bf16_dot(NT, NN, TN, False, True)
_dot_tn = _bf16_dot(TN, NT, NN, True, False)


def _chunk_causal_mask():
    r = lax.broadcasted_iota(jnp.int32, (TILE, TILE), 0)
    c = lax.broadcasted_iota(jnp.int32, (TILE, TILE), 1)
    return ((r // CHUNK) == (c // CHUNK)) & (c <= r)


def _chunk_scan(x, reverse):
    pos = lax.broadcasted_iota(jnp.int32, x.shape, 0) % CHUNK
    step = 1
    while step < CHUNK:
        if reverse:
            x = x + jnp.where(pos < CHUNK - step, pltpu.roll(x, x.shape[0] - step, axis=0), 0.0)
        else:
            x = x + jnp.where(pos >= step, pltpu.roll(x, step, axis=0), 0.0)
        step *= 2
    return x


def _chunk_total(x):
    return _chunk_scan(x, False) + _chunk_scan(x, True) - x


@jax.custom_vjp
def _chunk_sums(x):
    return _chunk_scan(x, False), _chunk_total(x)


def _chunk_sums_fwd(x):
    return _chunk_sums(x), None


def _chunk_sums_bwd(_, ct):
    return (_chunk_scan(ct[0], True) + _chunk_total(ct[1]),)


_chunk_sums.defvjp(_chunk_sums_fwd, _chunk_sums_bwd)


def _hgrn_tile(q, f, i, g, lgt, hg, st):
    d = q.shape[1]
    l0, l1 = lgt[0:1], lgt[1:2]
    mx = jnp.maximum(l0, l1)
    e0, e1 = jnp.exp(l0 - mx), jnp.exp(l1 - mx)
    lb = e0 / (e0 + e1)
    fg = lb + (1.0 - lb) * jax.nn.sigmoid(f)
    k = 1.0 - fg
    causal = _chunk_causal_mask()
    b, b_last = _chunk_sums(jnp.log(fg))
    q_in = q * jax.nn.sigmoid(q) * (d ** -0.5) * jnp.exp(b)
    k_in = k * jnp.exp(-b)
    k_end = k * jnp.exp(b_last - b)
    att = jnp.where(causal, _dot_nt(q_in, k_in), 0.0)
    o_intra = _dot_nn(att, i)
    n_chunks = TILE // CHUNK
    chunk_of_row = lax.broadcasted_iota(jnp.int32, (TILE, 1), 0) // CHUNK

    def spread(a):
        return jnp.concatenate([jnp.where(chunk_of_row == n, a, 0.0) for n in range(n_chunks)], axis=1)

    increments = _dot_tn(i, spread(k_end))
    states = []
    for n in range(n_chunks):
        states.append(st)
        decay = jnp.exp(jnp.mean(b_last[n * CHUNK:(n + 1) * CHUNK], axis=0, keepdims=True))
        st = st * decay + increments[:, n * d:(n + 1) * d]
    o = o_intra + _dot_nt(spread(q_in), jnp.concatenate(states, axis=1))
    o = o * lax.rsqrt(jnp.mean(o * o, axis=-1, keepdims=True) + NORM_EPS) * hg
    return o * (g * jax.nn.sigmoid(g)), st


def hgrn_fwd(z, lgt, hg):
    t, d4 = z.shape
    d = d4 // 4
    nh, nt = d // HEAD_DIM, t // TILE
    hp = HEADS_PER_STEP
    wide = hp * HEAD_DIM

    def body(q_ref, f_ref, i_ref, g_ref, lgt_ref, hg_ref, o_ref, st_out_ref, st_ref):
        tt = pl.program_id(1)

        @pl.when(tt == 0)
        def _():
            st_ref[...] = jnp.zeros_like(st_ref)

        for hh in range(hp):
            cols = slice(hh * HEAD_DIM, (hh + 1) * HEAD_DIM)
            st = st_ref[hh]
            st_out_ref[hh] = st
            o, st = _hgrn_tile(q_ref[:, cols], f_ref[:, cols], i_ref[:, cols], g_ref[:, cols], lgt_ref[:, cols],
                               hg_ref[...], st)
            o_ref[:, cols] = o.astype(o_ref.dtype)
            st_ref[hh] = st

    def part(p):
        return pl.BlockSpec((TILE, wide), lambda h, tt: (tt, p * (nh // hp) + h))

    return pl.pallas_call(
        body, name="hgrn_fwd", grid=(nh // hp, nt),
        in_specs=[part(0), part(1), part(2), part(3),
                  pl.BlockSpec((2, wide), lambda h, tt: (0, h)),
                  pl.BlockSpec((1, HEAD_DIM), lambda h, tt: (0, 0))],
        out_specs=[pl.BlockSpec((TILE, wide), lambda h, tt: (tt, h)),
                   pl.BlockSpec((hp, None, HEAD_DIM, HEAD_DIM), lambda h, tt: (h, tt, 0, 0))],
        out_shape=[jax.ShapeDtypeStruct((t, d), BF16),
                   jax.ShapeDtypeStruct((nh, nt, HEAD_DIM, HEAD_DIM), F32)],
        scratch_shapes=[pltpu.VMEM((hp, HEAD_DIM, HEAD_DIM), F32)],
        compiler_params=_params(("parallel", "arbitrary")),
    )(z, z, z, z, lgt, hg)


def hgrn_bwd(z, lgt, hg, states, dout):
    t, d4 = z.shape
    d = d4 // 4
    nh, nt = d // HEAD_DIM, t // TILE
    hp = HEADS_PER_STEP
    wide = hp * HEAD_DIM

    def body(q_ref, f_ref, i_ref, g_ref, lgt_ref, hg_ref, st_in_ref, do_ref, dz_ref, dlgt_ref, dhg_ref, dst_ref):
        h, tt = pl.program_id(0), pl.program_id(1)

        @pl.when(tt == 0)
        def _():
            dst_ref[...] = jnp.zeros_like(dst_ref)
            dlgt_ref[...] = jnp.zeros_like(dlgt_ref)

        @pl.when((tt == 0) & (h == 0))
        def _():
            dhg_ref[...] = jnp.zeros_like(dhg_ref)

        for hh in range(hp):
            cols = slice(hh * HEAD_DIM, (hh + 1) * HEAD_DIM)
            _, vjp = jax.vjp(_hgrn_tile, q_ref[:, cols], f_ref[:, cols], i_ref[:, cols], g_ref[:, cols],
                             lgt_ref[:, cols], hg_ref[...], st_in_ref[hh])
            grads = vjp((do_ref[:, cols], dst_ref[hh]))
            for p in range(4):
                dz_ref[p, :, cols] = grads[p].astype(dz_ref.dtype)
            dlgt_ref[:, cols] += grads[4]
            dhg_ref[...] += grads[5]
            dst_ref[hh] = grads[6]

    def part(p):
        return pl.BlockSpec((TILE, wide), lambda h, tt: (nt - 1 - tt, p * (nh // hp) + h))

    return pl.pallas_call(
        body, name="hgrn_bwd", grid=(nh // hp, nt),
        in_specs=[part(0), part(1), part(2), part(3),
                  pl.BlockSpec((2, wide), lambda h, tt: (0, h)),
                  pl.BlockSpec((1, HEAD_DIM), lambda h, tt: (0, 0)),
                  pl.BlockSpec((hp, None, HEAD_DIM, HEAD_DIM), lambda h, tt: (h, nt - 1 - tt, 0, 0)),
                  pl.BlockSpec((TILE, wide), lambda h, tt: (nt - 1 - tt, h))],
        out_specs=[pl.BlockSpec((4, TILE, wide), lambda h, tt: (0, nt - 1 - tt, h)),
                   pl.BlockSpec((2, wide), lambda h, tt: (0, h)),
                   pl.BlockSpec((1, HEAD_DIM), lambda h, tt: (0, 0))],
        out_shape=[jax.ShapeDtypeStruct((4, t, d), BF16),
                   jax.ShapeDtypeStruct((2, d), F32),
                   jax.ShapeDtypeStruct((1, HEAD_DIM), F32)],
        scratch_shapes=[pltpu.VMEM((hp, HEAD_DIM, HEAD_DIM), F32)],
        compiler_params=_params(("arbitrary", "arbitrary")),
    )(z, z, z, z, lgt, hg, states, dout)


def _log_sigmoid(x):
    return jnp.minimum(x, 0.0) - jnp.log(1.0 + jnp.exp(-jnp.abs(x)))


def decay_fwd(fl_t, b_f):
    nh, t = fl_t.shape

    def body(fl_ref, b_ref, out_ref):
        r = lax.broadcasted_iota(jnp.int32, (128, 128), 0)
        c = lax.broadcasted_iota(jnp.int32, (128, 128), 1)
        upper = (r <= c).astype(F32)
        carry = jnp.zeros((nh, 1), F32)
        for j in range(t // 128):
            cols = slice(j * 128, (j + 1) * 128)
            ls = _log_sigmoid(fl_ref[:, cols] + b_ref[...])
            out_ref[:, cols] = carry + jnp.dot(ls, upper, precision=lax.Precision.HIGHEST,
                                               preferred_element_type=F32)
            carry = carry + jnp.sum(ls, axis=1, keepdims=True)

    return pl.pallas_call(body, name="decay_fwd", out_shape=jax.ShapeDtypeStruct((nh, t), F32),
                          compiler_params=_params(None))(fl_t, b_f)


def decay_bwd(fl_t, b_f, ddcum):
    nh, t = fl_t.shape

    def body(fl_ref, b_ref, dd_ref, dfl_ref, db_ref):
        r = lax.broadcasted_iota(jnp.int32, (128, 128), 0)
        c = lax.broadcasted_iota(jnp.int32, (128, 128), 1)
        lower = (r >= c).astype(F32)
        carry = jnp.zeros((nh, 1), F32)
        db = jnp.zeros((nh, 1), F32)
        for j in reversed(range(t // 128)):
            cols = slice(j * 128, (j + 1) * 128)
            dd = dd_ref[:, cols]
            dls = carry + jnp.dot(dd, lower, precision=lax.Precision.HIGHEST, preferred_element_type=F32)
            carry = carry + jnp.sum(dd, axis=1, keepdims=True)
            dfl = dls * jax.nn.sigmoid(-(fl_ref[:, cols] + b_ref[...]))
            dfl_ref[:, cols] = dfl
            db = db + jnp.sum(dfl, axis=1, keepdims=True)
        db_ref[...] = db

    return pl.pallas_call(body, name="decay_bwd",
                          out_shape=[jax.ShapeDtypeStruct((nh, t), F32), jax.ShapeDtypeStruct((nh, 1), F32)],
                          compiler_params=_params(None))(fl_t, b_f, ddcum)


def _attn_parts(t):
    tq = _tile(t, (256, 128))
    per_part = 2 if t // tq >= 4 else 1
    return tq, [(first, per_part, (first + per_part) * tq) for first in range(0, t // tq, per_part)]


def _attn_logits(q_ref, k_ref, dcol_ref, drow_ref, row0, tq, keys):
    qs = (q_ref[...] * (HEAD_DIM ** -0.5)).astype(BF16)
    s = lax.dot_general(qs, k_ref[...], NT, preferred_element_type=F32)
    s = s + dcol_ref[...] - drow_ref[...]
    row = row0 + lax.broadcasted_iota(jnp.int32, (tq, keys), 0)
    col = lax.broadcasted_iota(jnp.int32, (tq, keys), 1)
    return qs, jnp.where(col <= row, s, NEG_BIG)


def attn_fwd(q, kv, dcol, drow):
    t, d = q.shape
    nh = d // HEAD_DIM
    tq, parts = _attn_parts(t)
    o = lse = None
    for first, count, keys in parts:
        def body(q_ref, k_ref, v_ref, dcol_ref, drow_ref, *rest, first=first, keys=keys):
            o_ref, lse_ref = rest[-2:]
            _, s = _attn_logits(q_ref, k_ref, dcol_ref, drow_ref, (first + pl.program_id(1)) * tq, tq, keys)
            m = jnp.max(s, axis=1, keepdims=True)
            p = jnp.exp(s - m)
            l = jnp.sum(p, axis=1, keepdims=True)
            acc = jnp.dot(p.astype(BF16), v_ref[...], preferred_element_type=F32)
            o_ref[...] = (acc / l).astype(o_ref.dtype)
            lse_ref[...] = m + jnp.log(l)

        tile = pl.BlockSpec((tq, HEAD_DIM), lambda h, i, first=first: (first + i, h))
        col = pl.BlockSpec((None, tq, 1), lambda h, i, first=first: (h, first + i, 0))
        seen_k = pl.BlockSpec((keys, HEAD_DIM), lambda h, i: (0, h))
        seen_v = pl.BlockSpec((keys, HEAD_DIM), lambda h, i: (0, nh + h))
        carried = [] if o is None else [o, lse]
        o, lse = pl.pallas_call(
            body, name=f"attn_fwd_{first}", grid=(nh, count),
            in_specs=[tile, seen_k, seen_v, col, pl.BlockSpec((None, 1, keys), lambda h, i: (h, 0, 0))]
            + [pl.BlockSpec(memory_space=pl.ANY)] * len(carried),
            out_specs=[tile, col],
            out_shape=[jax.ShapeDtypeStruct((t, d), BF16), jax.ShapeDtypeStruct((nh, t, 1), F32)],
            input_output_aliases={5: 0, 6: 1} if carried else {},
            compiler_params=_params(("parallel", "parallel")),
        )(q, kv, kv, dcol, drow, *carried)
    return o, lse


def attn_bwd(q, kv, dcol, drow, lse, do):
    t, d = q.shape
    nh = d // HEAD_DIM
    tq, parts = _attn_parts(t)
    dq = dk = dv = ddrow = None
    for first, count, keys in reversed(parts):
        first_call = dq is None

        def body(q_ref, k_ref, v_ref, dcol_ref, drow_ref, lse_ref, do_ref, *rest, first=first, keys=keys,
                 count=count, first_call=first_call):
            dq_ref, dk_ref, dv_ref, ddrow_ref, dk_acc, dv_acc, ddrow_acc = rest[-7:]
            i = pl.program_id(1)

            @pl.when(i == 0)
            def _():
                if first_call:
                    dk_acc[...] = jnp.zeros_like(dk_acc)
                    dv_acc[...] = jnp.zeros_like(dv_acc)
                    ddrow_acc[...] = jnp.zeros_like(ddrow_acc)
                else:
                    dk_acc[...] = rest[1][...]
                    dv_acc[...] = rest[2][...]
                    ddrow_acc[...] = rest[3][...]

            qs, s = _attn_logits(q_ref, k_ref, dcol_ref, drow_ref, (first + i) * tq, tq, keys)
            p = jnp.exp(s - lse_ref[...])
            do = do_ref[...]
            dp = lax.dot_general(do, v_ref[...], NT, preferred_element_type=F32)
            ds = p * (dp - jnp.sum(p * dp, axis=1, keepdims=True))
            dsb = ds.astype(BF16)
            dq_ref[...] = (jnp.dot(dsb, k_ref[...], preferred_element_type=F32) * (HEAD_DIM ** -0.5)).astype(dq_ref.dtype)
            dk_acc[...] += lax.dot_general(dsb, qs, TN, preferred_element_type=F32)
            dv_acc[...] += lax.dot_general(p.astype(BF16), do, TN, preferred_element_type=F32)
            ddrow_acc[...] -= jnp.sum(ds, axis=0, keepdims=True)

            @pl.when(i == count - 1)
            def _():
                dk_ref[...] = dk_acc[...]
                dv_ref[...] = dv_acc[...]
                ddrow_ref[...] = ddrow_acc[...]

        tile = pl.BlockSpec((tq, HEAD_DIM), lambda h, i, first=first: (first + i, h))
        col = pl.BlockSpec((None, tq, 1), lambda h, i, first=first: (h, first + i, 0))
        seen = pl.BlockSpec((keys, HEAD_DIM), lambda h, i: (0, h))
        seen_v = pl.BlockSpec((keys, HEAD_DIM), lambda h, i: (0, nh + h))
        seen_row = pl.BlockSpec((None, 1, keys), lambda h, i: (h, 0, 0))
        carried = [] if first_call else [dq, dk, dv, ddrow]
        carried_specs = [] if first_call else [pl.BlockSpec(memory_space=pl.ANY), seen, seen, seen_row]
        dq, dk, dv, ddrow = pl.pallas_call(
            body, name=f"attn_bwd_{first}", grid=(nh, count),
            in_specs=[tile, seen, seen_v, col, seen_row, col, tile] + carried_specs,
            out_specs=[tile, seen, seen, seen_row],
            out_shape=[jax.ShapeDtypeStruct((t, d), BF16), jax.ShapeDtypeStruct((t, d), F32),
                       jax.ShapeDtypeStruct((t, d), F32), jax.ShapeDtypeStruct((nh, 1, t), F32)],
            scratch_shapes=[pltpu.VMEM((keys, HEAD_DIM), F32), pltpu.VMEM((keys, HEAD_DIM), F32),
                            pltpu.VMEM((1, keys), F32)],
            input_output_aliases={} if first_call else {7: 0, 8: 1, 9: 2, 10: 3},
            compiler_params=_params(("parallel", "arbitrary")),
        )(q, kv, kv, dcol, drow, lse, do, *carried)
    return dq, dk, dv, ddrow


def _my_index():
    return (lax.axis_index("x") * 2 + lax.axis_index("y")) * 2 + lax.axis_index("c")


def all_gather(name, src, deps=()):
    def body(src_ref, *rest):
        out_ref, send_sems, recv_sems, local_sem = rest[len(deps):]
        x, y, c = (lax.axis_index(a) for a in MESH_AXES)
        me = (x * 2 + y) * 2 + c
        local = pltpu.make_async_copy(src_ref, out_ref.at[me], local_sem)
        local.start()
        copies = []
        for dlt in range(1, N_DEV):
            copies.append(pltpu.make_async_remote_copy(
                src_ref=src_ref, dst_ref=out_ref.at[me], send_sem=send_sems.at[dlt - 1],
                recv_sem=recv_sems.at[dlt - 1], device_id=(x ^ (dlt // 4), y ^ ((dlt // 2) % 2), c ^ (dlt % 2)),
                device_id_type=pl.DeviceIdType.MESH))
        for cp in copies:
            cp.start()
        for cp in copies:
            cp.wait_recv()
        for cp in copies:
            cp.wait_send()
        local.wait()

    return pl.pallas_call(
        body, name=name, out_shape=jax.ShapeDtypeStruct((N_DEV,) + tuple(src.shape), src.dtype),
        in_specs=[pl.BlockSpec(memory_space=pl.ANY)] * (1 + len(deps)), out_specs=pl.BlockSpec(memory_space=pl.ANY),
        scratch_shapes=[pltpu.SemaphoreType.DMA((N_DEV - 1,)), pltpu.SemaphoreType.DMA((N_DEV - 1,)),
                        pltpu.SemaphoreType.DMA],
        compiler_params=pltpu.CompilerParams(has_side_effects=True),
    )(src, *deps)


_HBM = pl.BlockSpec(memory_space=pltpu.HBM)
_SEM = pl.BlockSpec(memory_space=pltpu.SEMAPHORE)
_DATAFLOW = pltpu.SideEffectType.DATAFLOW_SIDE_EFFECTING


def _peer_copies(src_ref, land_ref, send_sems, recv_sems):
    x, y, c = (lax.axis_index(a) for a in MESH_AXES)
    me = (x * 2 + y) * 2 + c
    copies = []
    for dlt in range(1, N_DEV):
        px, py, pc = x ^ (dlt // 4), y ^ ((dlt // 2) % 2), c ^ (dlt % 2)
        peer = (px * 2 + py) * 2 + pc
        copies.append(pltpu.make_async_remote_copy(
            src_ref=src_ref.at[peer], dst_ref=land_ref.at[me],
            send_sem=send_sems.at[dlt - 1], recv_sem=recv_sems.at[dlt - 1],
            device_id=(px, py, pc), device_id_type=pl.DeviceIdType.MESH))
    return copies


def scatter_start(name, srcs):
    n = len(srcs)
    lands = [lax.empty(s.shape, s.dtype) for s in srcs]

    def body(*refs):
        src_refs, land_refs = refs[:n], refs[n:2 * n]
        send_sems, recv_sems = refs[2 * n:3 * n], refs[3 * n:4 * n]
        token = refs[-1]
        for j in range(n):
            for cp in _peer_copies(src_refs[j], land_refs[j], send_sems[j], recv_sems[j]):
                cp.start()
        token[...] = jnp.zeros_like(token)

    sems = [pltpu.SemaphoreType.DMA((N_DEV - 1,))] * (2 * n)
    thru = [pltpu.HBM(a.shape, a.dtype) for a in list(srcs) + lands]
    outs = pl.pallas_call(
        body, name=name, out_shape=tuple(sems + thru + [jax.ShapeDtypeStruct((8, 128), F32)]),
        in_specs=[_HBM] * (2 * n), out_specs=tuple([_SEM] * (2 * n) + [_HBM] * (2 * n) + [pl.BlockSpec(memory_space=pltpu.VMEM)]),
        input_output_aliases={j: 2 * n + j for j in range(2 * n)},
        compiler_params=pltpu.CompilerParams(has_side_effects=_DATAFLOW),
    )(*[pltpu.with_memory_space_constraint(a, pltpu.HBM) for a in list(srcs) + lands])
    handles = [(outs[j], outs[n + j], outs[2 * n + j], outs[3 * n + j]) for j in range(n)]
    return handles, outs[-1]


def scatter_wait(name, handle, after):
    send_sems, recv_sems, src, land = handle

    def body(src_ref, land_ref, send_ref, recv_ref, after_ref, src_out, land_out):
        for cp in _peer_copies(src_ref, land_ref, send_ref, recv_ref):
            cp.wait_send()
            cp.wait_recv()

    return pl.pallas_call(
        body, name=name, out_shape=(pltpu.HBM(src.shape, src.dtype), pltpu.HBM(land.shape, land.dtype)),
        in_specs=[_HBM, _HBM, _SEM, _SEM, pl.BlockSpec(memory_space=pl.ANY)], out_specs=(_HBM, _HBM),
        input_output_aliases={0: 0, 1: 1},
        compiler_params=pltpu.CompilerParams(has_side_effects=_DATAFLOW),
    )(src, land, send_sems, recv_sems, after)


N_OTHER_CHIPS = 3


def _two_level_places():
    x, y, c = (lax.axis_index(a) for a in MESH_AXES)
    return (x, y, c), (x * 2 + y) * 2 + c, (x, y, 1 - c), [(1 - x, y), (x, 1 - y), (1 - x, 1 - y)]


def _first_copies(land_ref, send_sems, recv_sems):
    (x, y, c), me, other_core, chips = _two_level_places()
    targets = [other_core] + [(cx, cy, c) for cx, cy in chips]
    return [pltpu.make_async_remote_copy(
        src_ref=land_ref.at[me], dst_ref=land_ref.at[me], send_sem=send_sems.at[k], recv_sem=recv_sems.at[k],
        device_id=to, device_id_type=pl.DeviceIdType.MESH) for k, to in enumerate(targets)]


def _passed_on_copies(land_ref, send_sems, recv_sems):
    (x, y, c), me, other_core, chips = _two_level_places()
    copies = []
    for k, (cx, cy) in enumerate(chips):
        slot = land_ref.at[(cx * 2 + cy) * 2 + c]
        copies.append(pltpu.make_async_remote_copy(
            src_ref=slot, dst_ref=slot, send_sem=send_sems.at[k], recv_sem=recv_sems.at[k],
            device_id=other_core, device_id_type=pl.DeviceIdType.MESH))
    return copies


def gather_start(name, lands):
    n = len(lands)

    def body(*refs):
        land_refs, send_sems, recv_sems = refs[:n], refs[n:2 * n], refs[2 * n:3 * n]
        for j in range(n):
            for cp in _first_copies(land_refs[j], send_sems[j], recv_sems[j]):
                cp.start()

    sems = [pltpu.SemaphoreType.DMA((1 + N_OTHER_CHIPS,))] * (2 * n)
    outs = pl.pallas_call(
        body, name=name, out_shape=tuple(sems + [pltpu.HBM(a.shape, a.dtype) for a in lands]),
        in_specs=[_HBM] * n, out_specs=tuple([_SEM] * (2 * n) + [_HBM] * n),
        input_output_aliases={j: 2 * n + j for j in range(n)},
        compiler_params=pltpu.CompilerParams(has_side_effects=_DATAFLOW),
    )(*[pltpu.with_memory_space_constraint(a, pltpu.HBM) for a in lands])
    return [[outs[j], outs[n + j], outs[2 * n + j]] for j in range(n)]


def gather_pass_on(name, handle, after):
    send_sems, recv_sems, land = handle

    def body(land_ref, recv_ref, after_ref, land_out, send2, recv2, token):
        arrivals = _first_copies(land_ref, recv_ref, recv_ref)
        for k, cp in enumerate(_passed_on_copies(land_ref, send2, recv2)):
            arrivals[1 + k].wait_recv()
            cp.start()
        token[...] = jnp.zeros_like(token)

    sem3 = pltpu.SemaphoreType.DMA((N_OTHER_CHIPS,))
    land, send2, recv2, token = pl.pallas_call(
        body, name=name,
        out_shape=(pltpu.HBM(land.shape, land.dtype), sem3, sem3, jax.ShapeDtypeStruct((8, 128), F32)),
        in_specs=[_HBM, _SEM, pl.BlockSpec(memory_space=pl.ANY)],
        out_specs=(_HBM, _SEM, _SEM, pl.BlockSpec(memory_space=pltpu.VMEM)),
        input_output_aliases={0: 0}, compiler_params=pltpu.CompilerParams(has_side_effects=_DATAFLOW),
    )(land, recv_sems, after)
    return [send_sems, recv_sems, land, send2, recv2], token


def gather_wait(name, handle, after):
    send_sems, recv_sems, land, send2, recv2 = handle

    def body(land_ref, send_ref, recv_ref, send2_ref, recv2_ref, after_ref, land_out):
        first = _first_copies(land_ref, send_ref, recv_ref)
        for cp in first:
            cp.wait_send()
        first[0].wait_recv()
        for cp in _passed_on_copies(land_ref, send2_ref, recv2_ref):
            cp.wait_send()
            cp.wait_recv()

    return pl.pallas_call(
        body, name=name, out_shape=pltpu.HBM(land.shape, land.dtype),
        in_specs=[_HBM, _SEM, _SEM, _SEM, _SEM, pl.BlockSpec(memory_space=pl.ANY)], out_specs=_HBM,
        input_output_aliases={0: 0}, compiler_params=pltpu.CompilerParams(has_side_effects=_DATAFLOW),
    )(land, send_sems, recv_sems, send2, recv2, after)


N_CHIPS = 4


def _pair_copies(g_ref, half_ref, send_sems, recv_sems):
    (x, y, c), me, other_core, chips = _two_level_places()
    return [pltpu.make_async_remote_copy(
        src_ref=g_ref.at[chip * 2 + (1 - c)], dst_ref=half_ref.at[chip], send_sem=send_sems.at[chip],
        recv_sem=recv_sems.at[chip], device_id=other_core, device_id_type=pl.DeviceIdType.MESH)
        for chip in range(N_CHIPS)]


def pair_start(name, g):
    half = lax.empty((N_CHIPS,) + g.shape[1:], g.dtype)

    def body(g_ref, half_ref, send_sems, recv_sems, g_out, half_out, token):
        for cp in _pair_copies(g_ref, half_ref, send_sems, recv_sems):
            cp.start()
        token[...] = jnp.zeros_like(token)

    sem = pltpu.SemaphoreType.DMA((N_CHIPS,))
    outs = pl.pallas_call(
        body, name=name,
        out_shape=(sem, sem, pltpu.HBM(g.shape, g.dtype), pltpu.HBM(half.shape, half.dtype),
                   jax.ShapeDtypeStruct((8, 128), F32)),
        in_specs=[_HBM, _HBM], out_specs=(_SEM, _SEM, _HBM, _HBM, pl.BlockSpec(memory_space=pltpu.VMEM)),
        input_output_aliases={0: 2, 1: 3}, compiler_params=pltpu.CompilerParams(has_side_effects=_DATAFLOW),
    )(pltpu.with_memory_space_constraint(g, pltpu.HBM), half)
    return list(outs[:4]), outs[4]


def pair_wait(name, handle, after):
    send_sems, recv_sems, g, half = handle

    def body(g_ref, half_ref, send_ref, recv_ref, after_ref, g_out, half_out):
        for cp in _pair_copies(g_ref, half_ref, send_ref, recv_ref):
            cp.wait_send()
            cp.wait_recv()

    return pl.pallas_call(
        body, name=name, out_shape=(pltpu.HBM(g.shape, g.dtype), pltpu.HBM(half.shape, half.dtype)),
        in_specs=[_HBM, _HBM, _SEM, _SEM, pl.BlockSpec(memory_space=pl.ANY)], out_specs=(_HBM, _HBM),
        input_output_aliases={0: 0, 1: 1}, compiler_params=pltpu.CompilerParams(has_side_effects=_DATAFLOW),
    )(g, half, send_sems, recv_sems, after)


def pair_sum(name, g, half):
    _, r, wd = g.shape
    tr = _row_tile(r, 2 * ROW_TILE_BYTES // (4 * wd))
    kind = lax.axis_index("c").astype(jnp.int32).reshape(1)

    def body(kind_ref, g_ref, half_ref, o_ref):
        o_ref[...] = (g_ref[...].astype(F32) + half_ref[...].astype(F32)).astype(o_ref.dtype)

    spec = pl.BlockSpec((None, tr, wd), lambda chip, i, kind_ref: (chip, i, 0))
    return pl.pallas_call(
        body, name=name,
        grid_spec=pltpu.PrefetchScalarGridSpec(
            num_scalar_prefetch=1, grid=(N_CHIPS, r // tr),
            in_specs=[pl.BlockSpec((None, tr, wd), lambda chip, i, kind_ref: (chip * 2 + kind_ref[0], i, 0)), spec],
            out_specs=spec),
        out_shape=jax.ShapeDtypeStruct((N_CHIPS, r, wd), g.dtype),
        compiler_params=_params(("parallel", "parallel")),
    )(kind, g, half)


def _chip_copies(sums_ref, land_ref, send_sems, recv_sems):
    (x, y, c), me, other_core, chips = _two_level_places()
    return [pltpu.make_async_remote_copy(
        src_ref=sums_ref.at[cx * 2 + cy], dst_ref=land_ref.at[x * 2 + y], send_sem=send_sems.at[k],
        recv_sem=recv_sems.at[k], device_id=(cx, cy, c), device_id_type=pl.DeviceIdType.MESH)
        for k, (cx, cy) in enumerate(chips)]


def chip_start(name, sums):
    land = lax.empty(sums.shape, sums.dtype)

    def body(sums_ref, land_ref, send_sems, recv_sems, sums_out, land_out, token):
        for cp in _chip_copies(sums_ref, land_ref, send_sems, recv_sems):
            cp.start()
        token[...] = jnp.zeros_like(token)

    sem = pltpu.SemaphoreType.DMA((N_OTHER_CHIPS,))
    outs = pl.pallas_call(
        body, name=name,
        out_shape=(sem, sem, pltpu.HBM(sums.shape, sums.dtype), pltpu.HBM(land.shape, land.dtype),
                   jax.ShapeDtypeStruct((8, 128), F32)),
        in_specs=[_HBM, _HBM], out_specs=(_SEM, _SEM, _HBM, _HBM, pl.BlockSpec(memory_space=pltpu.VMEM)),
        input_output_aliases={0: 2, 1: 3}, compiler_params=pltpu.CompilerParams(has_side_effects=_DATAFLOW),
    )(pltpu.with_memory_space_constraint(sums, pltpu.HBM), land)
    return list(outs[:4]), outs[4]


def chip_wait(name, handle, after):
    send_sems, recv_sems, sums, land = handle

    def body(sums_ref, land_ref, send_ref, recv_ref, after_ref, sums_out, land_out):
        for cp in _chip_copies(sums_ref, land_ref, send_ref, recv_ref):
            cp.wait_send()
            cp.wait_recv()

    return pl.pallas_call(
        body, name=name, out_shape=(pltpu.HBM(sums.shape, sums.dtype), pltpu.HBM(land.shape, land.dtype)),
        in_specs=[_HBM, _HBM, _SEM, _SEM, pl.BlockSpec(memory_space=pl.ANY)], out_specs=(_HBM, _HBM),
        input_output_aliases={0: 0, 1: 1}, compiler_params=pltpu.CompilerParams(has_side_effects=_DATAFLOW),
    )(sums, land, send_sems, recv_sems, after)


def adamw_reduce(name, parts, w, m, v):
    nl, r, wd = w.shape
    tr = _row_tile(r, ROW_TILE_BYTES // (8 * wd))

    def body(*refs):
        p_refs = refs[:nl]
        w_ref, m_ref, v_ref, g_ref, d_ref, nm_ref, nv_ref = refs[nl:]
        layer = pl.program_id(0)
        for j in range(nl):
            @pl.when(layer == j)
            def _(j=j):
                g = p_refs[j][0].astype(F32)
                for sender in range(1, p_refs[j].shape[0]):
                    g = g + p_refs[j][sender].astype(F32)
                nm = B1 * m_ref[...] + (1.0 - B1) * g
                nv = B2 * v_ref[...] + (1.0 - B2) * jnp.square(g)
                m_hat = nm / (1.0 - B1 ** STEP)
                v_hat = nv / (1.0 - B2 ** STEP)
                g_ref[...] = g
                d_ref[...] = -LR * (m_hat / (jnp.sqrt(v_hat) + ADAM_EPS) + WD * w_ref[...])
                nm_ref[...] = nm
                nv_ref[...] = nv

    def part_spec(j):
        return pl.BlockSpec((parts[j].shape[0], tr, wd), lambda l, i: (0, jnp.where(l == j, i, 0), 0))

    spec = pl.BlockSpec((None, tr, wd), lambda l, i: (l, i, 0))
    return pl.pallas_call(
        body, name=name, grid=(nl, r // tr),
        in_specs=[part_spec(j) for j in range(nl)] + [spec, spec, spec],
        out_specs=[spec] * 4, out_shape=[jax.ShapeDtypeStruct((nl, r, wd), F32)] * 4,
        compiler_params=_params(("arbitrary", "arbitrary")),
    )(*parts, w, m, v)


def _pack_rows(vectors, rows=None):
    flat = jnp.concatenate([a.reshape(-1).astype(F32) for a in vectors])
    n = flat.shape[0]
    if rows is None:
        rows = -(-n // 1024) * 8
    return jnp.pad(flat, (0, rows * 128 - n)).reshape(rows, 128)


def _unpack_rows(packed, like):
    flat = packed.reshape(-1)
    out, pos = [], 0
    for a in like:
        out.append(flat[pos:pos + a.size].reshape(a.shape))
        pos += a.size
    return out


def kernel(x, p, mix_norm, mlp_norm, ple_norm, w_a_in, a_lb_logits, a_head_gain, w_a_out, kv_norm, w_kvf, b_f, w_b_q, w_b_out, w_mlp_up, w_mlp_down, w_ple_gate, w_ple_up, final_norm, loss_target, m_mix_norm, m_mlp_norm, m_ple_norm, m_w_a_in, m_a_lb_logits, m_a_head_gain, m_w_a_out, m_kv_norm, m_w_kvf, m_b_f, m_w_b_q, m_w_b_out, m_w_mlp_up, m_w_mlp_down, m_w_ple_gate, m_w_ple_up, m_final_norm, v_mix_norm, v_mlp_norm, v_ple_norm, v_w_a_in, v_a_lb_logits, v_a_head_gain, v_w_a_out, v_kv_norm, v_w_kvf, v_b_f, v_w_b_q, v_w_b_out, v_w_mlp_up, v_w_mlp_down, v_w_ple_gate, v_w_ple_up, v_final_norm):
    t, d = x.shape[1], x.shape[2]
    nh = d // HEAD_DIM
    n_layers = 2
    x2 = x.reshape(t, d)
    target = loss_target.reshape(t, d)
    me = _my_index()

    shards = {"w_a_in": w_a_in[0], "w_a_out": w_a_out[0], "w_kvf": w_kvf, "w_b_q": w_b_q[0], "w_b_out": w_b_out[0]}
    for l in range(n_layers):
        shards.update({f"w_mlp_up{l}": w_mlp_up[l], f"w_mlp_down{l}": w_mlp_down[l],
                       f"w_ple_gate{l}": w_ple_gate[l], f"w_ple_up{l}": w_ple_up[l]})
    first_use = ["a_lb_logits", "w_a_in", "w_a_out", "w_mlp_up0", "w_mlp_down0", "w_ple_gate0", "w_ple_up0", "w_kvf",
                 "w_b_q", "w_b_out", "w_mlp_up1", "w_mlp_down1", "w_ple_gate1", "w_ple_up1"]
    row_sharded = ("w_a_out", "w_b_q", "w_b_out", "w_mlp_down", "w_ple_gate")
    shards_bf = [a_lb_logits] + [shards[n].astype(BF16) for n in first_use[1:]]
    ag_handles = gather_start("ag_start", [
        lax.dynamic_update_slice(lax.empty((N_DEV,) + a.shape, a.dtype), a[None], (me, 0, 0)) for a in shards_bf])
    passed_on = {}
    weights = {}

    def pass_on(j, after):
        if j < len(first_use) and j not in passed_on:
            passed_on[j] = gather_pass_on("ag_pass_" + first_use[j], ag_handles[j], after)

    def weight(name, after=None):
        if name not in weights:
            j = first_use.index(name)
            pass_on(j, after)
            pass_on(j + 1, after)
            behind = passed_on[j + 1][1] if j + 1 in passed_on else after
            g = gather_wait("ag_wait_" + name, passed_on[j][0], behind)
            if name.rstrip("01") in row_sharded:
                g = g.reshape(1, g.shape[0] * g.shape[1], g.shape[2])
            weights[name] = g
        return weights[name]

    lgt = weight("a_lb_logits", x2).transpose(1, 0, 2).reshape(2, d)
    p_bf = [p[l, 0].astype(BF16) for l in range(n_layers)]

    def row(vec):
        return vec.reshape(1, -1)

    def mlp_ple_fwd(l, h_in, a):
        (h_a, u_mlp), _ = rowwise(f"add_norm_mlp{l}", _add_norm_fwd, [h_in, a], [row(mlp_norm[l])])
        pre, act = mm_nn(f"mlp_up{l}", u_mlp, weight(f"w_mlp_up{l}", u_mlp), fuse=(_relu2, (), (BF16, BF16)))
        mo = mm_nn(f"mlp_down{l}", act, weight(f"w_mlp_down{l}", act))
        (h_b, u_ple), _ = rowwise(f"add_norm_ple{l}", _add_norm_fwd, [h_a, mo], [row(ple_norm[l])])
        gpre = mm_nn(f"ple_gate{l}", u_ple, weight(f"w_ple_gate{l}", u_ple))
        pu = mm_nn(f"ple_up{l}", p_bf[l], weight(f"w_ple_up{l}", gpre))
        return dict(h_a=h_a, u_mlp=u_mlp, pre=pre, act=act, h_b=h_b, u_ple=u_ple, gpre=gpre, pu=pu)

    (u0,), _ = rowwise("norm_mix0", _norm_fwd, [x2], [row(mix_norm[0])])
    z = mm_nn("a_in", u0, weight("w_a_in", u0))
    og, states = hgrn_fwd(z, lgt, a_head_gain)
    a0 = mm_nn("a_out", og, weight("w_a_out", og))
    s0 = mlp_ple_fwd(0, x2, a0)
    (h3, u_kv, u1), _ = rowwise("ple_norms", _ple_two_norms_fwd, [s0["h_b"], s0["gpre"], s0["pu"]],
                                [row(kv_norm), row(mix_norm[1])])
    w_kvf_cols = weight("w_kvf", u_kv).transpose(1, 0, 2).reshape(d, 2 * d + nh)
    w_kv = w_kvf_cols[:, :2 * d].reshape(d, 2, d).transpose(1, 0, 2)
    w_f = jnp.pad(w_kvf_cols[:, 2 * d:], ((0, 0), (0, HEAD_DIM - nh)))[None]
    kv = mm_nn("kvf", u_kv, w_kv, out_dtype=BF16)
    fl_t = mm_nn("kvf_forget", u_kv, w_f)[:, :nh].T
    b_f_col = b_f.reshape(nh, 1)
    dcum = decay_fwd(fl_t, b_f_col)
    dcol, drow = dcum.reshape(nh, t, 1), dcum.reshape(nh, 1, t)
    q = mm_nn("b_q", u1, weight("w_b_q", dcum))
    o, lse = attn_fwd(q, kv, dcol, drow)
    a1 = mm_nn("b_out", o, weight("w_b_out", o))
    s1 = mlp_ple_fwd(1, h3, a1)

    (dh, dgpre, dpu), (d_final, loss_rows) = rowwise(
        "tail", _tail_fwd_bwd, [s1["h_b"], s1["gpre"], s1["pu"], target], [row(final_norm)])

    sent = {}
    tokens = []

    two_level = ("w_mlp_up0", "w_mlp_down0", "w_a_in")
    swapping = []

    def send_grad(name, g):
        g = g.reshape(N_DEV, -1, g.shape[-1])
        if name in two_level:
            sent[name], token = pair_start("rs_pair_" + name, g)
            swapping.append(name)
        else:
            (sent[name],), token = scatter_start("rs_start_" + name, [g])
        tokens.append(token)

    def second_stage(after):
        for name in swapping:
            g, half = pair_wait("rs_pairwait_" + name, sent[name], after)
            sent[name], token = chip_start("rs_chip_" + name, pair_sum("rs_sum_" + name, g, half))
            tokens.append(token)
        swapping.clear()

    def after_sends():
        deps = tuple(tokens)
        tokens.clear()
        return deps

    def mlp_ple_bwd(l, s, dh, dgpre, dpu):
        send_grad(f"w_ple_gate{l}", mm_tn(f"d_ple_gate_w{l}", s["u_ple"], dgpre, 1, deps=after_sends()))
        send_grad(f"w_ple_up{l}", mm_tn(f"d_ple_up_w{l}", p_bf[l], dpu, N_DEV, deps=after_sends()))
        du = mm_nt(f"d_ple_gate_x{l}", dgpre, weight(f"w_ple_gate{l}"), deps=after_sends())
        (dh, dh_bf), (d_ple,) = rowwise(f"d_norm_ple{l}", _norm_bwd, [s["h_b"], du, dh], [row(ple_norm[l])])
        send_grad(f"w_mlp_down{l}", mm_tn(f"d_mlp_down_w{l}", s["act"], dh_bf, 1))
        (dpre,) = mm_nt(f"d_mlp_down_x{l}", dh_bf, weight(f"w_mlp_down{l}"), deps=after_sends(),
                        fuse=(_relu2_bwd, (s["pre"],), (BF16,)))
        second_stage(dpre)
        send_grad(f"w_mlp_up{l}", mm_tn(f"d_mlp_up_w{l}", s["u_mlp"], dpre, N_DEV))
        du = mm_nt(f"d_mlp_up_x{l}", dpre, weight(f"w_mlp_up{l}"), deps=after_sends())
        second_stage(du)
        (dh, dh_bf), (d_mlp,) = rowwise(f"d_norm_mlp{l}", _norm_bwd, [s["h_a"], du, dh], [row(mlp_norm[l])])
        return dh, dh_bf, d_ple, d_mlp

    dh, dh_bf, d_ple1, d_mlp1 = mlp_ple_bwd(1, s1, dh, dgpre, dpu)
    send_grad("w_b_out", mm_tn("d_b_out_w", o, dh_bf, 1))
    do = mm_nt("d_b_out_x", dh_bf, weight("w_b_out"), out_dtype=BF16, deps=after_sends())
    dq, dk, dv, ddrow = attn_bwd(q, kv, dcol, drow, lse, do)
    send_grad("w_b_q", mm_tn("d_b_q_w", u1, dq, 1))
    du1 = mm_nt("d_b_q_x", dq, weight("w_b_q"), deps=after_sends())
    dfl_t, d_b_f = decay_bwd(fl_t, b_f_col, ddrow.reshape(nh, t))
    dkv = jnp.stack([dk, dv]).astype(BF16)
    dfl = jnp.pad(dfl_t.T, ((0, 0), (0, HEAD_DIM - nh))).astype(BF16)
    d_w_kv = mm_tn("d_kvf_w", u_kv, dkv, 2)
    d_w_f = mm_tn("d_kvf_forget_w", u_kv, dfl, 1)
    d_w_kvf = jnp.concatenate([d_w_kv[0], d_w_kv[1], d_w_f[0, :, :nh]], axis=1)
    send_grad("w_kvf", d_w_kvf.reshape(d, N_DEV, -1).transpose(1, 0, 2))
    du_f = mm_nt("d_kvf_forget_x", dfl, w_f, deps=after_sends())
    (du_kv,) = mm_nt("d_kvf_x", dkv, w_kv, fuse=(lambda acc, extra: (acc + extra,), (du_f,), (F32,)))
    second_stage(du_kv)
    (dh, dgpre, dpu), (d_kv_norm, d_mix1) = rowwise(
        "d_ple_norms", _two_norms_ple_bwd, [h3, du_kv, du1, dh, s0["gpre"], s0["pu"]],
        [row(kv_norm), row(mix_norm[1])])
    dh, dh_bf, d_ple0, d_mlp0 = mlp_ple_bwd(0, s0, dh, dgpre, dpu)
    send_grad("w_a_out", mm_tn("d_a_out_w", og, dh_bf, 1))
    dog = mm_nt("d_a_out_x", dh_bf, weight("w_a_out"), deps=after_sends())
    dz4, d_lgt, d_hg = hgrn_bwd(z, lgt, a_head_gain, states, dog)
    send_grad("w_a_in", mm_tn("d_a_in_w", u0, dz4, N_DEV, stacked=True))
    du0 = mm_nt("d_a_in_x", dz4, weight("w_a_in"), deps=after_sends(), stacked=True)
    second_stage(du0)
    (dx, _), (d_mix0,) = rowwise("d_norm_mix0", _norm_bwd, [x2, du0, dh], [row(mix_norm[0])])

    new = {}
    last = [dx]

    def update(name, parts, w, m, v):
        shp = w.shape
        w3, m3, v3 = (a.reshape(len(parts), -1, shp[-1]) for a in (w, m, v))
        new[name] = tuple(a.reshape(shp) for a in adamw_reduce("adamw_" + name, parts, w3, m3, v3))
        last[0] = new[name][0]

    def receive_update(name, layers, w, m, v):
        parts = {}
        for sfx in layers:
            if name + sfx in two_level:
                mine, land = chip_wait(f"rs_wait_{name}{sfx}", sent[name + sfx], last[0])
                slot = me // 2
            else:
                mine, land = scatter_wait(f"rs_wait_{name}{sfx}", sent[name + sfx], last[0])
                slot = me
            parts[sfx] = lax.dynamic_update_slice(land, lax.dynamic_slice_in_dim(mine, slot, 1, 0), (slot, 0, 0))
        update(name, [parts[sfx] for sfx in sorted(layers)], w, m, v)

    both = ("1", "0")
    receive_update("w_b_out", ("",), w_b_out, m_w_b_out, v_w_b_out)
    receive_update("w_b_q", ("",), w_b_q, m_w_b_q, v_w_b_q)
    receive_update("w_kvf", ("",), w_kvf, m_w_kvf, v_w_kvf)
    receive_update("w_ple_gate", both, w_ple_gate, m_w_ple_gate, v_w_ple_gate)
    receive_update("w_ple_up", both, w_ple_up, m_w_ple_up, v_w_ple_up)
    receive_update("w_mlp_down", both, w_mlp_down, m_w_mlp_down, v_w_mlp_down)
    receive_update("w_mlp_up", both, w_mlp_up, m_w_mlp_up, v_w_mlp_up)
    receive_update("w_a_out", ("",), w_a_out, m_w_a_out, v_w_a_out)
    receive_update("w_a_in", ("",), w_a_in, m_w_a_in, v_w_a_in)

    small = dict(mix_norm=jnp.concatenate([d_mix0, d_mix1]), mlp_norm=jnp.concatenate([d_mlp0, d_mlp1]),
                 ple_norm=jnp.concatenate([d_ple0, d_ple1]), a_head_gain=d_hg, kv_norm=d_kv_norm.reshape(d),
                 b_f=d_b_f.reshape(nh), final_norm=d_final.reshape(d))
    small_w = dict(mix_norm=(mix_norm, m_mix_norm, v_mix_norm), mlp_norm=(mlp_norm, m_mlp_norm, v_mlp_norm),
                   ple_norm=(ple_norm, m_ple_norm, v_ple_norm),
                   a_head_gain=(a_head_gain, m_a_head_gain, v_a_head_gain), kv_norm=(kv_norm, m_kv_norm, v_kv_norm),
                   b_f=(b_f, m_b_f, v_b_f), final_norm=(final_norm, m_final_norm, v_final_norm))
    names = list(small)
    packed = _pack_rows([d_lgt] + [small[n] for n in names])
    everyone = all_gather("ag_small_grads", packed, deps=(last[0],))
    n_lgt_rows = d_lgt.size // 128
    lgt_parts = everyone[:, :n_lgt_rows].reshape(N_DEV, 2, d)
    lgt_parts = lax.dynamic_slice_in_dim(lgt_parts, me * a_lb_logits.shape[1], a_lb_logits.shape[1], axis=2)
    update("a_lb_logits", [lgt_parts], a_lb_logits, m_a_lb_logits, v_a_lb_logits)
    rest = everyone[:, n_lgt_rows:]
    like = [small_w[n][0] for n in names]
    packed_w, packed_m, packed_v = (_pack_rows([small_w[n][j] for n in names], rest.shape[1])[None] for j in range(3))
    outs = adamw_reduce("adamw_small", [rest], packed_w, packed_m, packed_v)
    unpacked = [_unpack_rows(a, like) for a in outs]
    for j, n in enumerate(names):
        new[n] = tuple(unpacked[q][j] for q in range(4))

    order = ["mix_norm", "mlp_norm", "ple_norm", "w_a_in", "a_lb_logits", "a_head_gain", "w_a_out", "kv_norm",
             "w_kvf", "b_f", "w_b_q", "w_b_out", "w_mlp_up", "w_mlp_down", "w_ple_gate", "w_ple_up", "final_norm"]
    loss_here, _ = lax.optimization_barrier((loss_rows[0, 0], new["final_norm"][0]))
    loss = lax.psum(loss_here, MESH_AXES)
    result = [loss, dx.reshape(x.shape)]
    for j in range(4):
        result += [new[n][j] for n in order]
    return tuple(result)
```

```python
import jax
import jax.numpy as jnp
from jax import lax
from jax.experimental import pallas as pl
from jax.experimental.pallas import tpu as pltpu

F32 = jnp.float32
BF16 = jnp.bfloat16
HEAD_DIM = 128
CHUNK = 16
TILE = 128
HEADS_PER_STEP = 4
NORM_EPS = 1e-6
N_DEV = 8
MESH_AXES = ("x", "y", "c")
VMEM_LIMIT_BYTES = 48 * 1024 * 1024
ROW_TILE_BYTES = 2 * 1024 * 1024
LR, B1, B2, ADAM_EPS, WD, STEP = 0.001, 0.9, 0.999, 1e-08, 0.01, 10
NEG_BIG = -1e30

NN = (((1,), (0,)), ((), ()))
NT = (((1,), (1,)), ((), ()))
TN = (((0,), (0,)), ((), ()))


def _params(semantics):
    return pltpu.CompilerParams(dimension_semantics=semantics, vmem_limit_bytes=VMEM_LIMIT_BYTES)


def _tile(n, prefs):
    for p in prefs:
        if n % p == 0:
            return p
    return n


def _row_tile(rows, limit):
    for cand in (2048, 1024, 512, 256, 128, 64, 32, 16):
        if cand <= limit and rows % cand == 0:
            return cand
    return rows


def _mm_call(name, a, b, dims, grid, a_spec, b_spec, o_spec, o_shape, acc_shape, k_axes, out_dtype, deps=(),
             fuse=None, split=1):
    nk = 1
    for ax in k_axes:
        nk *= grid[ax]
    fn, extra, out_dtypes = fuse if fuse else (lambda acc: (acc,), (), (out_dtype,))
    n_extra, n_out = len(extra), len(out_dtypes)

    def finish(acc, rest):
        o_refs = rest[n_extra + len(deps):n_extra + len(deps) + n_out]
        for ref, val in zip(o_refs, fn(acc, *[r[...] for r in rest[:n_extra]])):
            ref[...] = val.astype(ref.dtype)

    def product(a_ref, b_ref):
        if split == 1:
            return lax.dot_general(a_ref[...], b_ref[...], dims, preferred_element_type=F32)
        wide = a_ref.shape[1] // split
        return sum(lax.dot_general(a_ref[:, q * wide:(q + 1) * wide], b_ref[q], dims, preferred_element_type=F32)
                   for q in range(split))

    def one_step(a_ref, b_ref, *rest):
        finish(product(a_ref, b_ref), rest)

    def accumulate(a_ref, b_ref, *rest):
        acc_ref = rest[-1]
        k = 0
        for ax in k_axes:
            k = k * grid[ax] + pl.program_id(ax)
        part = product(a_ref, b_ref)

        @pl.when(k == 0)
        def _():
            acc_ref[...] = part

        @pl.when((k > 0) & (k < nk - 1))
        def _():
            acc_ref[...] += part

        @pl.when(k == nk - 1)
        def _():
            finish(acc_ref[...] + part, rest)

    sem = tuple("arbitrary" if ax in k_axes else "parallel" for ax in range(len(grid)))
    outs = pl.pallas_call(
        one_step if nk == 1 else accumulate, name=name, grid=grid,
        in_specs=[a_spec, b_spec] + [o_spec] * n_extra + [pl.BlockSpec(memory_space=pl.ANY)] * len(deps),
        out_specs=[o_spec] * n_out, out_shape=[jax.ShapeDtypeStruct(o_shape, dt) for dt in out_dtypes],
        scratch_shapes=[] if nk == 1 else [pltpu.VMEM(acc_shape, F32)], compiler_params=_params(sem),
    )(a, b, *extra, *deps)
    return outs if fuse else outs[0]


def mm_nn(name, a, b3, out_dtype=F32, out3=False, deps=(), fuse=None):
    m, k = a.shape
    g, _, n = b3.shape
    tm, tk = _tile(m, (1024, 512, 256)), _tile(k, (2048, 1024, 512, 256))
    tn = n if out3 else _tile(n, (1024, 512, 256, 128))
    nj = n // tn
    grid = (m // tm, g, nj, k // tk)
    a_spec = pl.BlockSpec((tm, tk), lambda i, gg, j, kk: (i, kk))
    b_spec = pl.BlockSpec((None, tk, tn), lambda i, gg, j, kk: (gg, kk, j))
    if out3:
        o_spec = pl.BlockSpec((None, tm, tn), lambda i, gg, j, kk: (gg, i, j))
        o_shape = (g, m, n)
    else:
        o_spec = pl.BlockSpec((tm, tn), lambda i, gg, j, kk: (i, gg * nj + j))
        o_shape = (m, g * n)
    return _mm_call(name, a, b3, NN, grid, a_spec, b_spec, o_spec, o_shape, (tm, tn), (3,), out_dtype, deps, fuse)


def mm_nt(name, a, b3, out_dtype=F32, deps=(), fuse=None, stacked=False):
    g, k, n = b3.shape
    a3 = a.ndim == 3 and not stacked
    m = a.shape[1] if a.ndim == 3 else a.shape[0]
    tm, tko = _tile(m, (1024, 512, 256)), _tile(k, (1024, 512, 256))
    tc = n if a3 else _tile(n, (2048, 1024, 512, 256, 128))
    nc = n // tc
    per = g // a.shape[0] if stacked else g
    pair = 2 if (not a3 and nc == 1 and per % 2 == 0 and tc <= 1024) else 1
    grid = (m // tm, k // tko, g // pair, nc)
    if a3:
        a_spec = pl.BlockSpec((None, tm, tc), lambda i, j, gg, c: (gg, i, c))
    elif stacked:
        a_spec = pl.BlockSpec((None, tm, pair * tc),
                              lambda i, j, gg, c: ((gg * pair) // per, i, (((gg * pair) % per) // pair) * nc + c))
    else:
        a_spec = pl.BlockSpec((tm, pair * tc), lambda i, j, gg, c: (i, gg * nc + c))
    if pair == 1:
        b_spec = pl.BlockSpec((None, tko, tc), lambda i, j, gg, c: (gg, j, c))
    else:
        b_spec = pl.BlockSpec((pair, tko, tc), lambda i, j, gg, c: (gg, j, c))
    o_spec = pl.BlockSpec((tm, tko), lambda i, j, gg, c: (i, j))
    return _mm_call(name, a, b3, NT, grid, a_spec, b_spec, o_spec, (m, k), (tm, tko), (2, 3), out_dtype, deps, fuse,
                    pair)


def mm_tn(name, a, b, g, out_dtype=BF16, deps=(), stacked=False):
    t, k = a.shape
    b3 = b.ndim == 3 and not stacked
    n = b.shape[2] if b3 else (b.shape[0] * b.shape[2] if stacked else b.shape[1]) // g
    tm = _tile(k, (1024, 512, 256))
    tn = n if b3 else _tile(n, (1024, 512, 256, 128))
    tt = _tile(t, (2048, 1024, 512, 256))
    nj = n // tn
    grid = (g, k // tm, nj, t // tt)
    a_spec = pl.BlockSpec((tt, tm), lambda gg, i, j, s: (s, i))
    if b3:
        b_spec = pl.BlockSpec((None, tt, tn), lambda gg, i, j, s: (gg, s, j))
    elif stacked:
        per = g // b.shape[0]
        b_spec = pl.BlockSpec((None, tt, tn), lambda gg, i, j, s: (gg // per, s, (gg % per) * nj + j))
    else:
        b_spec = pl.BlockSpec((tt, tn), lambda gg, i, j, s: (s, gg * nj + j))
    o_spec = pl.BlockSpec((None, tm, tn), lambda gg, i, j, s: (gg, i, j))
    return _mm_call(name, a, b, TN, grid, a_spec, b_spec, o_spec, (g, k, n), (tm, tn), (3,), out_dtype, deps)


def rowwise(name, fn, rows, vecs=()):
    t = rows[0].shape[0]
    wmax = max(r.shape[1] for r in rows)
    tr = _row_tile(t, ROW_TILE_BYTES // (4 * wmax))
    row_s = [jax.ShapeDtypeStruct((tr, r.shape[1]), r.dtype) for r in rows]
    vec_s = [jax.ShapeDtypeStruct(v.shape, v.dtype) for v in vecs]
    out_rows_s, out_sums_s = jax.eval_shape(fn, *row_s, *vec_s)
    n_in, n_r = len(rows) + len(vecs), len(out_rows_s)

    def body(*refs):
        i = pl.program_id(0)
        o_rows, o_sums = fn(*[r[...] for r in refs[:n_in]])
        for ref, val in zip(refs[n_in:n_in + n_r], o_rows):
            ref[...] = val

        if out_sums_s:
            @pl.when(i == 0)
            def _():
                for ref in refs[n_in + n_r:]:
                    ref[...] = jnp.zeros_like(ref)

            for ref, val in zip(refs[n_in + n_r:], o_sums):
                ref[...] += val

    in_specs = [pl.BlockSpec((tr, r.shape[1]), lambda i: (i, 0)) for r in rows]
    in_specs += [pl.BlockSpec(v.shape, lambda i: (0, 0)) for v in vecs]
    out_specs = [pl.BlockSpec((tr, s.shape[1]), lambda i: (i, 0)) for s in out_rows_s]
    out_specs += [pl.BlockSpec(s.shape, lambda i: (0, 0)) for s in out_sums_s]
    out_shape = [jax.ShapeDtypeStruct((t, s.shape[1]), s.dtype) for s in out_rows_s]
    out_shape += [jax.ShapeDtypeStruct(s.shape, s.dtype) for s in out_sums_s]
    outs = pl.pallas_call(
        body, name=name, grid=(t // tr,), in_specs=in_specs, out_specs=out_specs, out_shape=out_shape,
        compiler_params=_params(("arbitrary",)),
    )(*rows, *vecs)
    return outs[:n_r], outs[n_r:]


def _rms(x, gain):
    return x * lax.rsqrt(jnp.mean(x * x, axis=-1, keepdims=True) + NORM_EPS) * gain


def _norm_fwd(x, gain):
    return (_rms(x, gain).astype(BF16),), ()


def _add_norm_fwd(h, a, gain):
    h = h + a
    return (h, _rms(h, gain).astype(BF16)), ()


def _relu2(pre):
    r = jnp.maximum(pre, 0.0)
    return pre, r * r


def _ple(h, gpre, pu):
    return h + pu * jax.nn.sigmoid(gpre)


def _ple_two_norms_fwd(h, gpre, pu, gain_a, gain_b):
    h = _ple(h, gpre, pu)
    return (h, _rms(h, gain_a).astype(BF16), _rms(h, gain_b).astype(BF16)), ()


def _tail_fwd_bwd(h, gpre, pu, target, gain):
    def row_loss(h, gpre, pu, gain):
        y = _rms(_ple(h, gpre, pu), gain)
        return 0.5 * jnp.mean(jnp.square(y - target), axis=-1, keepdims=True)

    loss, vjp = jax.vjp(row_loss, h, gpre, pu, gain)
    dh, dgpre, dpu, dgain = vjp(jnp.ones_like(loss))
    loss = jnp.broadcast_to(jnp.sum(loss, axis=0, keepdims=True), (1, 128))
    return (dh, dgpre.astype(BF16), dpu.astype(BF16)), (dgain, loss)


def _norm_bwd(h, du, dh_in, gain):
    _, vjp = jax.vjp(_rms, h, gain)
    dh, dgain = vjp(du)
    dh = dh_in + dh
    return (dh, dh.astype(BF16)), (dgain,)


def _two_norms_ple_bwd(h, du_a, du_b, dh_in, gpre, pu, gain_a, gain_b):
    _, vjp = jax.vjp(lambda h, ga, gb: (_rms(h, ga), _rms(h, gb)), h, gain_a, gain_b)
    dh, dga, dgb = vjp((du_a, du_b))
    dh = dh_in + dh
    _, gate_vjp = jax.vjp(lambda g, u: u * jax.nn.sigmoid(g), gpre, pu)
    dgpre, dpu = gate_vjp(dh)
    return (dh, dgpre.astype(BF16), dpu.astype(BF16)), (dga, dgb)


def _relu2_bwd(dact, pre):
    return (dact * 2.0 * jnp.maximum(pre.astype(F32), 0.0),)


def _bf16_dot(dims_fwd, dims_da, dims_db, swap_da, swap_db):
    @jax.custom_vjp
    def dot(a, b):
        return lax.dot_general(a.astype(BF16), b.astype(BF16), dims_fwd, preferred_element_type=F32)

    def fwd(a, b):
        return dot(a, b), (a, b)

    def bwd(res, ct):
        a, b = res
        ct, a, b = ct.astype(BF16), a.astype(BF16), b.astype(BF16)
        da = lax.dot_general(*((b, ct) if swap_da else (ct, b)), dims_da, preferred_element_type=F32)
        db = lax.dot_general(*((ct, a) if swap_db else (a, ct)), dims_db, preferred_element_type=F32)
        return da, db

    dot.defvjp(fwd, bwd)
    return dot


_dot_nn = _bf16_dot(NN, NT, TN, False, False)
_dot_nt = _bf16_dot(NT, NN, TN, False, True)
_dot_tn = _bf16_dot(TN, NT, NN, True, False)


def _chunk_causal_mask():
    r = lax.broadcasted_iota(jnp.int32, (TILE, TILE), 0)
    c = lax.broadcasted_iota(jnp.int32, (TILE, TILE), 1)
    return ((r // CHUNK) == (c // CHUNK)) & (c <= r)


def _chunk_scan(x, reverse):
    pos = lax.broadcasted_iota(jnp.int32, x.shape, 0) % CHUNK
    step = 1
    while step < CHUNK:
        if reverse:
            x = x + jnp.where(pos < CHUNK - step, pltpu.roll(x, x.shape[0] - step, axis=0), 0.0)
        else:
            x = x + jnp.where(pos >= step, pltpu.roll(x, step, axis=0), 0.0)
        step *= 2
    return x


def _chunk_total(x):
    return _chunk_scan(x, False) + _chunk_scan(x, True) - x


@jax.custom_vjp
def _chunk_sums(x):
    return _chunk_scan(x, False), _chunk_total(x)


def _chunk_sums_fwd(x):
    return _chunk_sums(x), None


def _chunk_sums_bwd(_, ct):
    return (_chunk_scan(ct[0], True) + _chunk_total(ct[1]),)


_chunk_sums.defvjp(_chunk_sums_fwd, _chunk_sums_bwd)


def _hgrn_tile(q, f, i, g, lgt, hg, st):
    d = q.shape[1]
    l0, l1 = lgt[0:1], lgt[1:2]
    mx = jnp.maximum(l0, l1)
    e0, e1 = jnp.exp(l0 - mx), jnp.exp(l1 - mx)
    lb = e0 / (e0 + e1)
    fg = lb + (1.0 - lb) * jax.nn.sigmoid(f)
    k = 1.0 - fg
    causal = _chunk_causal_mask()
    b, b_last = _chunk_sums(jnp.log(fg))
    q_in = q * jax.nn.sigmoid(q) * (d ** -0.5) * jnp.exp(b)
    k_in = k * jnp.exp(-b)
    k_end = k * jnp.exp(b_last - b)
    att = jnp.where(causal, _dot_nt(q_in, k_in), 0.0)
    o_intra = _dot_nn(att, i)
    n_chunks = TILE // CHUNK
    chunk_of_row = lax.broadcasted_iota(jnp.int32, (TILE, 1), 0) // CHUNK

    def spread(a):
        return jnp.concatenate([jnp.where(chunk_of_row == n, a, 0.0) for n in range(n_chunks)], axis=1)

    increments = _dot_tn(i, spread(k_end))
    states = []
    for n in range(n_chunks):
        states.append(st)
        decay = jnp.exp(jnp.mean(b_last[n * CHUNK:(n + 1) * CHUNK], axis=0, keepdims=True))
        st = st * decay + increments[:, n * d:(n + 1) * d]
    o = o_intra + _dot_nt(spread(q_in), jnp.concatenate(states, axis=1))
    o = o * lax.rsqrt(jnp.mean(o * o, axis=-1, keepdims=True) + NORM_EPS) * hg
    return o * (g * jax.nn.sigmoid(g)), st


def hgrn_fwd(z, lgt, hg):
    t, d4 = z.shape
    d = d4 // 4
    nh, nt = d // HEAD_DIM, t // TILE
    hp = HEADS_PER_STEP
    wide = hp * HEAD_DIM

    def body(q_ref, f_ref, i_ref, g_ref, lgt_ref, hg_ref, o_ref, st_out_ref, st_ref):
        tt = pl.program_id(1)

        @pl.when(tt == 0)
        def _():
            st_ref[...] = jnp.zeros_like(st_ref)

        for hh in range(hp):
            cols = slice(hh * HEAD_DIM, (hh + 1) * HEAD_DIM)
            st = st_ref[hh]
            st_out_ref[hh] = st
            o, st = _hgrn_tile(q_ref[:, cols], f_ref[:, cols], i_ref[:, cols], g_ref[:, cols], lgt_ref[:, cols],
                               hg_ref[...], st)
            o_ref[:, cols] = o.astype(o_ref.dtype)
            st_ref[hh] = st

    def part(p):
        return pl.BlockSpec((TILE, wide), lambda h, tt: (tt, p * (nh // hp) + h))

    return pl.pallas_call(
        body, name="hgrn_fwd", grid=(nh // hp, nt),
        in_specs=[part(0), part(1), part(2), part(3),
                  pl.BlockSpec((2, wide), lambda h, tt: (0, h)),
                  pl.BlockSpec((1, HEAD_DIM), lambda h, tt: (0, 0))],
        out_specs=[pl.BlockSpec((TILE, wide), lambda h, tt: (tt, h)),
                   pl.BlockSpec((hp, None, HEAD_DIM, HEAD_DIM), lambda h, tt: (h, tt, 0, 0))],
        out_shape=[jax.ShapeDtypeStruct((t, d), BF16),
                   jax.ShapeDtypeStruct((nh, nt, HEAD_DIM, HEAD_DIM), F32)],
        scratch_shapes=[pltpu.VMEM((hp, HEAD_DIM, HEAD_DIM), F32)],
        compiler_params=_params(("parallel", "arbitrary")),
    )(z, z, z, z, lgt, hg)


def hgrn_bwd(z, lgt, hg, states, dout):
    t, d4 = z.shape
    d = d4 // 4
    nh, nt = d // HEAD_DIM, t // TILE
    hp = HEADS_PER_STEP
    wide = hp * HEAD_DIM

    def body(q_ref, f_ref, i_ref, g_ref, lgt_ref, hg_ref, st_in_ref, do_ref, dz_ref, dlgt_ref, dhg_ref, dst_ref):
        h, tt = pl.program_id(0), pl.program_id(1)

        @pl.when(tt == 0)
        def _():
            dst_ref[...] = jnp.zeros_like(dst_ref)
            dlgt_ref[...] = jnp.zeros_like(dlgt_ref)

        @pl.when((tt == 0) & (h == 0))
        def _():
            dhg_ref[...] = jnp.zeros_like(dhg_ref)

        for hh in range(hp):
            cols = slice(hh * HEAD_DIM, (hh + 1) * HEAD_DIM)
            _, vjp = jax.vjp(_hgrn_tile, q_ref[:, cols], f_ref[:, cols], i_ref[:, cols], g_ref[:, cols],
                             lgt_ref[:, cols], hg_ref[...], st_in_ref[hh])
            grads = vjp((do_ref[:, cols], dst_ref[hh]))
            for p in range(4):
                dz_ref[p, :, cols] = grads[p].astype(dz_ref.dtype)
            dlgt_ref[:, cols] += grads[4]
            dhg_ref[...] += grads[5]
            dst_ref[hh] = grads[6]

    def part(p):
        return pl.BlockSpec((TILE, wide), lambda h, tt: (nt - 1 - tt, p * (nh // hp) + h))

    return pl.pallas_call(
        body, name="hgrn_bwd", grid=(nh // hp, nt),
        in_specs=[part(0), part(1), part(2), part(3),
                  pl.BlockSpec((2, wide), lambda h, tt: (0, h)),
                  pl.BlockSpec((1, HEAD_DIM), lambda h, tt: (0, 0)),
                  pl.BlockSpec((hp, None, HEAD_DIM, HEAD_DIM), lambda h, tt: (h, nt - 1 - tt, 0, 0)),
                  pl.BlockSpec((TILE, wide), lambda h, tt: (nt - 1 - tt, h))],
        out_specs=[pl.BlockSpec((4, TILE, wide), lambda h, tt: (0, nt - 1 - tt, h)),
                   pl.BlockSpec((2, wide), lambda h, tt: (0, h)),
                   pl.BlockSpec((1, HEAD_DIM), lambda h, tt: (0, 0))],
        out_shape=[jax.ShapeDtypeStruct((4, t, d), BF16),
                   jax.ShapeDtypeStruct((2, d), F32),
                   jax.ShapeDtypeStruct((1, HEAD_DIM), F32)],
        scratch_shapes=[pltpu.VMEM((hp, HEAD_DIM, HEAD_DIM), F32)],
        compiler_params=_params(("arbitrary", "arbitrary")),
    )(z, z, z, z, lgt, hg, states, dout)


def _log_sigmoid(x):
    return jnp.minimum(x, 0.0) - jnp.log(1.0 + jnp.exp(-jnp.abs(x)))


def decay_fwd(fl_t, b_f):
    nh, t = fl_t.shape

    def body(fl_ref, b_ref, out_ref):
        r = lax.broadcasted_iota(jnp.int32, (128, 128), 0)
        c = lax.broadcasted_iota(jnp.int32, (128, 128), 1)
        upper = (r <= c).astype(F32)
        carry = jnp.zeros((nh, 1), F32)
        for j in range(t // 128):
            cols = slice(j * 128, (j + 1) * 128)
            ls = _log_sigmoid(fl_ref[:, cols] + b_ref[...])
            out_ref[:, cols] = carry + jnp.dot(ls, upper, precision=lax.Precision.HIGHEST,
                                               preferred_element_type=F32)
            carry = carry + jnp.sum(ls, axis=1, keepdims=True)

    return pl.pallas_call(body, name="decay_fwd", out_shape=jax.ShapeDtypeStruct((nh, t), F32),
                          compiler_params=_params(None))(fl_t, b_f)


def decay_bwd(fl_t, b_f, ddcum):
    nh, t = fl_t.shape

    def body(fl_ref, b_ref, dd_ref, dfl_ref, db_ref):
        r = lax.broadcasted_iota(jnp.int32, (128, 128), 0)
        c = lax.broadcasted_iota(jnp.int32, (128, 128), 1)
        lower = (r >= c).astype(F32)
        carry = jnp.zeros((nh, 1), F32)
        db = jnp.zeros((nh, 1), F32)
        for j in reversed(range(t // 128)):
            cols = slice(j * 128, (j + 1) * 128)
            dd = dd_ref[:, cols]
            dls = carry + jnp.dot(dd, lower, precision=lax.Precision.HIGHEST, preferred_element_type=F32)
            carry = carry + jnp.sum(dd, axis=1, keepdims=True)
            dfl = dls * jax.nn.sigmoid(-(fl_ref[:, cols] + b_ref[...]))
            dfl_ref[:, cols] = dfl
            db = db + jnp.sum(dfl, axis=1, keepdims=True)
        db_ref[...] = db

    return pl.pallas_call(body, name="decay_bwd",
                          out_shape=[jax.ShapeDtypeStruct((nh, t), F32), jax.ShapeDtypeStruct((nh, 1), F32)],
                          compiler_params=_params(None))(fl_t, b_f, ddcum)


def _attn_parts(t):
    tq = _tile(t, (256, 128))
    per_part = 2 if t // tq >= 4 else 1
    return tq, [(first, per_part, (first + per_part) * tq) for first in range(0, t // tq, per_part)]


def _attn_logits(q_ref, k_ref, dcol_ref, drow_ref, row0, tq, keys):
    qs = (q_ref[...] * (HEAD_DIM ** -0.5)).astype(BF16)
    s = lax.dot_general(qs, k_ref[...], NT, preferred_element_type=F32)
    s = s + dcol_ref[...] - drow_ref[...]
    row = row0 + lax.broadcasted_iota(jnp.int32, (tq, keys), 0)
    col = lax.broadcasted_iota(jnp.int32, (tq, keys), 1)
    return qs, jnp.where(col <= row, s, NEG_BIG)


def attn_fwd(q, kv, dcol, drow):
    t, d = q.shape
    nh = d // HEAD_DIM
    tq, parts = _attn_parts(t)
    o = lse = None
    for first, count, keys in parts:
        def body(q_ref, k_ref, v_ref, dcol_ref, drow_ref, *rest, first=first, keys=keys):
            o_ref, lse_ref = rest[-2:]
            _, s = _attn_logits(q_ref, k_ref, dcol_ref, drow_ref, (first + pl.program_id(1)) * tq, tq, keys)
            m = jnp.max(s, axis=1, keepdims=True)
            p = jnp.exp(s - m)
            l = jnp.sum(p, axis=1, keepdims=True)
            acc = jnp.dot(p.astype(BF16), v_ref[...], preferred_element_type=F32)
            o_ref[...] = (acc / l).astype(o_ref.dtype)
            lse_ref[...] = m + jnp.log(l)

        tile = pl.BlockSpec((tq, HEAD_DIM), lambda h, i, first=first: (first + i, h))
        col = pl.BlockSpec((None, tq, 1), lambda h, i, first=first: (h, first + i, 0))
        seen_k = pl.BlockSpec((keys, HEAD_DIM), lambda h, i: (0, h))
        seen_v = pl.BlockSpec((keys, HEAD_DIM), lambda h, i: (0, nh + h))
        carried = [] if o is None else [o, lse]
        o, lse = pl.pallas_call(
            body, name=f"attn_fwd_{first}", grid=(nh, count),
            in_specs=[tile, seen_k, seen_v, col, pl.BlockSpec((None, 1, keys), lambda h, i: (h, 0, 0))]
            + [pl.BlockSpec(memory_space=pl.ANY)] * len(carried),
            out_specs=[tile, col],
            out_shape=[jax.ShapeDtypeStruct((t, d), BF16), jax.ShapeDtypeStruct((nh, t, 1), F32)],
            input_output_aliases={5: 0, 6: 1} if carried else {},
            compiler_params=_params(("parallel", "parallel")),
        )(q, kv, kv, dcol, drow, *carried)
    return o, lse


def attn_bwd(q, kv, dcol, drow, lse, do):
    t, d = q.shape
    nh = d // HEAD_DIM
    tq, parts = _attn_parts(t)
    dq = dk = dv = ddrow = None
    for first, count, keys in reversed(parts):
        first_call = dq is None

        def body(q_ref, k_ref, v_ref, dcol_ref, drow_ref, lse_ref, do_ref, *rest, first=first, keys=keys,
                 count=count, first_call=first_call):
            dq_ref, dk_ref, dv_ref, ddrow_ref, dk_acc, dv_acc, ddrow_acc = rest[-7:]
            i = pl.program_id(1)

            @pl.when(i == 0)
            def _():
                if first_call:
                    dk_acc[...] = jnp.zeros_like(dk_acc)
                    dv_acc[...] = jnp.zeros_like(dv_acc)
                    ddrow_acc[...] = jnp.zeros_like(ddrow_acc)
                else:
                    dk_acc[...] = rest[1][...]
                    dv_acc[...] = rest[2][...]
                    ddrow_acc[...] = rest[3][...]

            qs, s = _attn_logits(q_ref, k_ref, dcol_ref, drow_ref, (first + i) * tq, tq, keys)
            p = jnp.exp(s - lse_ref[...])
            do = do_ref[...]
            dp = lax.dot_general(do, v_ref[...], NT, preferred_element_type=F32)
            ds = p * (dp - jnp.sum(p * dp, axis=1, keepdims=True))
            dsb = ds.astype(BF16)
            dq_ref[...] = (jnp.dot(dsb, k_ref[...], preferred_element_type=F32) * (HEAD_DIM ** -0.5)).astype(dq_ref.dtype)
            dk_acc[...] += lax.dot_general(dsb, qs, TN, preferred_element_type=F32)
            dv_acc[...] += lax.dot_general(p.astype(BF16), do, TN, preferred_element_type=F32)
            ddrow_acc[...] -= jnp.sum(ds, axis=0, keepdims=True)

            @pl.when(i == count - 1)
            def _():
                dk_ref[...] = dk_acc[...]
                dv_ref[...] = dv_acc[...]
                ddrow_ref[...] = ddrow_acc[...]

        tile = pl.BlockSpec((tq, HEAD_DIM), lambda h, i, first=first: (first + i, h))
        col = pl.BlockSpec((None, tq, 1), lambda h, i, first=first: (h, first + i, 0))
        seen = pl.BlockSpec((keys, HEAD_DIM), lambda h, i: (0, h))
        seen_v = pl.BlockSpec((keys, HEAD_DIM), lambda h, i: (0, nh + h))
        seen_row = pl.BlockSpec((None, 1, keys), lambda h, i: (h, 0, 0))
        carried = [] if first_call else [dq, dk, dv, ddrow]
        carried_specs = [] if first_call else [pl.BlockSpec(memory_space=pl.ANY), seen, seen, seen_row]
        dq, dk, dv, ddrow = pl.pallas_call(
            body, name=f"attn_bwd_{first}", grid=(nh, count),
            in_specs=[tile, seen, seen_v, col, seen_row, col, tile] + carried_specs,
            out_specs=[tile, seen, seen, seen_row],
            out_shape=[jax.ShapeDtypeStruct((t, d), BF16), jax.ShapeDtypeStruct((t, d), F32),
                       jax.ShapeDtypeStruct((t, d), F32), jax.ShapeDtypeStruct((nh, 1, t), F32)],
            scratch_shapes=[pltpu.VMEM((keys, HEAD_DIM), F32), pltpu.VMEM((keys, HEAD_DIM), F32),
                            pltpu.VMEM((1, keys), F32)],
            input_output_aliases={} if first_call else {7: 0, 8: 1, 9: 2, 10: 3},
            compiler_params=_params(("parallel", "arbitrary")),
        )(q, kv, kv, dcol, drow, lse, do, *carried)
    return dq, dk, dv, ddrow


def _my_index():
    return (lax.axis_index("x") * 2 + lax.axis_index("y")) * 2 + lax.axis_index("c")


def all_gather(name, src, deps=()):
    def body(src_ref, *rest):
        out_ref, send_sems, recv_sems, local_sem = rest[len(deps):]
        x, y, c = (lax.axis_index(a) for a in MESH_AXES)
        me = (x * 2 + y) * 2 + c
        local = pltpu.make_async_copy(src_ref, out_ref.at[me], local_sem)
        local.start()
        copies = []
        for dlt in range(1, N_DEV):
            copies.append(pltpu.make_async_remote_copy(
                src_ref=src_ref, dst_ref=out_ref.at[me], send_sem=send_sems.at[dlt - 1],
                recv_sem=recv_sems.at[dlt - 1], device_id=(x ^ (dlt // 4), y ^ ((dlt // 2) % 2), c ^ (dlt % 2)),
                device_id_type=pl.DeviceIdType.MESH))
        for cp in copies:
            cp.start()
        for cp in copies:
            cp.wait_recv()
        for cp in copies:
            cp.wait_send()
        local.wait()

    return pl.pallas_call(
        body, name=name, out_shape=jax.ShapeDtypeStruct((N_DEV,) + tuple(src.shape), src.dtype),
        in_specs=[pl.BlockSpec(memory_space=pl.ANY)] * (1 + len(deps)), out_specs=pl.BlockSpec(memory_space=pl.ANY),
        scratch_shapes=[pltpu.SemaphoreType.DMA((N_DEV - 1,)), pltpu.SemaphoreType.DMA((N_DEV - 1,)),
                        pltpu.SemaphoreType.DMA],
        compiler_params=pltpu.CompilerParams(has_side_effects=True),
    )(src, *deps)


_HBM = pl.BlockSpec(memory_space=pltpu.HBM)
_SEM = pl.BlockSpec(memory_space=pltpu.SEMAPHORE)
_DATAFLOW = pltpu.SideEffectType.DATAFLOW_SIDE_EFFECTING


def _peer_copies(src_ref, land_ref, send_sems, recv_sems):
    x, y, c = (lax.axis_index(a) for a in MESH_AXES)
    me = (x * 2 + y) * 2 + c
    copies = []
    for dlt in range(1, N_DEV):
        px, py, pc = x ^ (dlt // 4), y ^ ((dlt // 2) % 2), c ^ (dlt % 2)
        peer = (px * 2 + py) * 2 + pc
        copies.append(pltpu.make_async_remote_copy(
            src_ref=src_ref.at[peer], dst_ref=land_ref.at[me],
            send_sem=send_sems.at[dlt - 1], recv_sem=recv_sems.at[dlt - 1],
            device_id=(px, py, pc), device_id_type=pl.DeviceIdType.MESH))
    return copies


def scatter_start(name, srcs):
    n = len(srcs)
    lands = [lax.empty(s.shape, s.dtype) for s in srcs]

    def body(*refs):
        src_refs, land_refs = refs[:n], refs[n:2 * n]
        send_sems, recv_sems = refs[2 * n:3 * n], refs[3 * n:4 * n]
        token = refs[-1]
        for j in range(n):
            for cp in _peer_copies(src_refs[j], land_refs[j], send_sems[j], recv_sems[j]):
                cp.start()
        token[...] = jnp.zeros_like(token)

    sems = [pltpu.SemaphoreType.DMA((N_DEV - 1,))] * (2 * n)
    thru = [pltpu.HBM(a.shape, a.dtype) for a in list(srcs) + lands]
    outs = pl.pallas_call(
        body, name=name, out_shape=tuple(sems + thru + [jax.ShapeDtypeStruct((8, 128), F32)]),
        in_specs=[_HBM] * (2 * n), out_specs=tuple([_SEM] * (2 * n) + [_HBM] * (2 * n) + [pl.BlockSpec(memory_space=pltpu.VMEM)]),
        input_output_aliases={j: 2 * n + j for j in range(2 * n)},
        compiler_params=pltpu.CompilerParams(has_side_effects=_DATAFLOW),
    )(*[pltpu.with_memory_space_constraint(a, pltpu.HBM) for a in list(srcs) + lands])
    handles = [(outs[j], outs[n + j], outs[2 * n + j], outs[3 * n + j]) for j in range(n)]
    return handles, outs[-1]


def scatter_wait(name, handle, after):
    send_sems, recv_sems, src, land = handle

    def body(src_ref, land_ref, send_ref, recv_ref, after_ref, src_out, land_out):
        for cp in _peer_copies(src_ref, land_ref, send_ref, recv_ref):
            cp.wait_send()
            cp.wait_recv()

    return pl.pallas_call(
        body, name=name, out_shape=(pltpu.HBM(src.shape, src.dtype), pltpu.HBM(land.shape, land.dtype)),
        in_specs=[_HBM, _HBM, _SEM, _SEM, pl.BlockSpec(memory_space=pl.ANY)], out_specs=(_HBM, _HBM),
        input_output_aliases={0: 0, 1: 1},
        compiler_params=pltpu.CompilerParams(has_side_effects=_DATAFLOW),
    )(src, land, send_sems, recv_sems, after)


N_OTHER_CHIPS = 3


def _two_level_places():
    x, y, c = (lax.axis_index(a) for a in MESH_AXES)
    return (x, y, c), (x * 2 + y) * 2 + c, (x, y, 1 - c), [(1 - x, y), (x, 1 - y), (1 - x, 1 - y)]


def _first_copies(land_ref, send_sems, recv_sems):
    (x, y, c), me, other_core, chips = _two_level_places()
    targets = [other_core] + [(cx, cy, c) for cx, cy in chips]
    return [pltpu.make_async_remote_copy(
        src_ref=land_ref.at[me], dst_ref=land_ref.at[me], send_sem=send_sems.at[k], recv_sem=recv_sems.at[k],
        device_id=to, device_id_type=pl.DeviceIdType.MESH) for k, to in enumerate(targets)]


def _passed_on_copies(land_ref, send_sems, recv_sems):
    (x, y, c), me, other_core, chips = _two_level_places()
    copies = []
    for k, (cx, cy) in enumerate(chips):
        slot = land_ref.at[(cx * 2 + cy) * 2 + c]
        copies.append(pltpu.make_async_remote_copy(
            src_ref=slot, dst_ref=slot, send_sem=send_sems.at[k], recv_sem=recv_sems.at[k],
            device_id=other_core, device_id_type=pl.DeviceIdType.MESH))
    return copies


def gather_start(name, lands):
    n = len(lands)

    def body(*refs):
        land_refs, send_sems, recv_sems = refs[:n], refs[n:2 * n], refs[2 * n:3 * n]
        for j in range(n):
            for cp in _first_copies(land_refs[j], send_sems[j], recv_sems[j]):
                cp.start()

    sems = [pltpu.SemaphoreType.DMA((1 + N_OTHER_CHIPS,))] * (2 * n)
    outs = pl.pallas_call(
        body, name=name, out_shape=tuple(sems + [pltpu.HBM(a.shape, a.dtype) for a in lands]),
        in_specs=[_HBM] * n, out_specs=tuple([_SEM] * (2 * n) + [_HBM] * n),
        input_output_aliases={j: 2 * n + j for j in range(n)},
        compiler_params=pltpu.CompilerParams(has_side_effects=_DATAFLOW),
    )(*[pltpu.with_memory_space_constraint(a, pltpu.HBM) for a in lands])
    return [[outs[j], outs[n + j], outs[2 * n + j]] for j in range(n)]


def gather_pass_on(name, handle, after):
    send_sems, recv_sems, land = handle

    def body(land_ref, recv_ref, after_ref, land_out, send2, recv2, token):
        arrivals = _first_copies(land_ref, recv_ref, recv_ref)
        for k, cp in enumerate(_passed_on_copies(land_ref, send2, recv2)):
            arrivals[1 + k].wait_recv()
            cp.start()
        token[...] = jnp.zeros_like(token)

    sem3 = pltpu.SemaphoreType.DMA((N_OTHER_CHIPS,))
    land, send2, recv2, token = pl.pallas_call(
        body, name=name,
        out_shape=(pltpu.HBM(land.shape, land.dtype), sem3, sem3, jax.ShapeDtypeStruct((8, 128), F32)),
        in_specs=[_HBM, _SEM, pl.BlockSpec(memory_space=pl.ANY)],
        out_specs=(_HBM, _SEM, _SEM, pl.BlockSpec(memory_space=pltpu.VMEM)),
        input_output_aliases={0: 0}, compiler_params=pltpu.CompilerParams(has_side_effects=_DATAFLOW),
    )(land, recv_sems, after)
    return [send_sems, recv_sems, land, send2, recv2], token


def gather_wait(name, handle, after):
    send_sems, recv_sems, land, send2, recv2 = handle

    def body(land_ref, send_ref, recv_ref, send2_ref, recv2_ref, after_ref, land_out):
        first = _first_copies(land_ref, send_ref, recv_ref)
        for cp in first:
            cp.wait_send()
        first[0].wait_recv()
        for cp in _passed_on_copies(land_ref, send2_ref, recv2_ref):
            cp.wait_send()
            cp.wait_recv()

    return pl.pallas_call(
        body, name=name, out_shape=pltpu.HBM(land.shape, land.dtype),
        in_specs=[_HBM, _SEM, _SEM, _SEM, _SEM, pl.BlockSpec(memory_space=pl.ANY)], out_specs=_HBM,
        input_output_aliases={0: 0}, compiler_params=pltpu.CompilerParams(has_side_effects=_DATAFLOW),
    )(land, send_sems, recv_sems, send2, recv2, after)


N_CHIPS = 4


def _pair_copies(g_ref, half_ref, send_sems, recv_sems):
    (x, y, c), me, other_core, chips = _two_level_places()
    return [pltpu.make_async_remote_copy(
        src_ref=g_ref.at[chip * 2 + (1 - c)], dst_ref=half_ref.at[chip], send_sem=send_sems.at[chip],
        recv_sem=recv_sems.at[chip], device_id=other_core, device_id_type=pl.DeviceIdType.MESH)
        for chip in range(N_CHIPS)]


def pair_start(name, g):
    half = lax.empty((N_CHIPS,) + g.shape[1:], g.dtype)

    def body(g_ref, half_ref, send_sems, recv_sems, g_out, half_out, token):
        for cp in _pair_copies(g_ref, half_ref, send_sems, recv_sems):
            cp.start()
        token[...] = jnp.zeros_like(token)

    sem = pltpu.SemaphoreType.DMA((N_CHIPS,))
    outs = pl.pallas_call(
        body, name=name,
        out_shape=(sem, sem, pltpu.HBM(g.shape, g.dtype), pltpu.HBM(half.shape, half.dtype),
                   jax.ShapeDtypeStruct((8, 128), F32)),
        in_specs=[_HBM, _HBM], out_specs=(_SEM, _SEM, _HBM, _HBM, pl.BlockSpec(memory_space=pltpu.VMEM)),
        input_output_aliases={0: 2, 1: 3}, compiler_params=pltpu.CompilerParams(has_side_effects=_DATAFLOW),
    )(pltpu.with_memory_space_constraint(g, pltpu.HBM), half)
    return list(outs[:4]), outs[4]


def pair_wait(name, handle, after):
    send_sems, recv_sems, g, half = handle

    def body(g_ref, half_ref, send_ref, recv_ref, after_ref, g_out, half_out):
        for cp in _pair_copies(g_ref, half_ref, send_ref, recv_ref):
            cp.wait_send()
            cp.wait_recv()

    return pl.pallas_call(
        body, name=name, out_shape=(pltpu.HBM(g.shape, g.dtype), pltpu.HBM(half.shape, half.dtype)),
        in_specs=[_HBM, _HBM, _SEM, _SEM, pl.BlockSpec(memory_space=pl.ANY)], out_specs=(_HBM, _HBM),
        input_output_aliases={0: 0, 1: 1}, compiler_params=pltpu.CompilerParams(has_side_effects=_DATAFLOW),
    )(g, half, send_sems, recv_sems, after)


def pair_sum(name, g, half):
    _, r, wd = g.shape
    tr = _row_tile(r, 2 * ROW_TILE_BYTES // (4 * wd))
    kind = lax.axis_index("c").astype(jnp.int32).reshape(1)

    def body(kind_ref, g_ref, half_ref, o_ref):
        o_ref[...] = (g_ref[...].astype(F32) + half_ref[...].astype(F32)).astype(o_ref.dtype)

    spec = pl.BlockSpec((None, tr, wd), lambda chip, i, kind_ref: (chip, i, 0))
    return pl.pallas_call(
        body, name=name,
        grid_spec=pltpu.PrefetchScalarGridSpec(
            num_scalar_prefetch=1, grid=(N_CHIPS, r // tr),
            in_specs=[pl.BlockSpec((None, tr, wd), lambda chip, i, kind_ref: (chip * 2 + kind_ref[0], i, 0)), spec],
            out_specs=spec),
        out_shape=jax.ShapeDtypeStruct((N_CHIPS, r, wd), g.dtype),
        compiler_params=_params(("parallel", "parallel")),
    )(kind, g, half)


def _chip_copies(sums_ref, land_ref, send_sems, recv_sems):
    (x, y, c), me, other_core, chips = _two_level_places()
    return [pltpu.make_async_remote_copy(
        src_ref=sums_ref.at[cx * 2 + cy], dst_ref=land_ref.at[x * 2 + y], send_sem=send_sems.at[k],
        recv_sem=recv_sems.at[k], device_id=(cx, cy, c), device_id_type=pl.DeviceIdType.MESH)
        for k, (cx, cy) in enumerate(chips)]


def chip_start(name, sums):
    land = lax.empty(sums.shape, sums.dtype)

    def body(sums_ref, land_ref, send_sems, recv_sems, sums_out, land_out, token):
        for cp in _chip_copies(sums_ref, land_ref, send_sems, recv_sems):
            cp.start()
        token[...] = jnp.zeros_like(token)

    sem = pltpu.SemaphoreType.DMA((N_OTHER_CHIPS,))
    outs = pl.pallas_call(
        body, name=name,
        out_shape=(sem, sem, pltpu.HBM(sums.shape, sums.dtype), pltpu.HBM(land.shape, land.dtype),
                   jax.ShapeDtypeStruct((8, 128), F32)),
        in_specs=[_HBM, _HBM], out_specs=(_SEM, _SEM, _HBM, _HBM, pl.BlockSpec(memory_space=pltpu.VMEM)),
        input_output_aliases={0: 2, 1: 3}, compiler_params=pltpu.CompilerParams(has_side_effects=_DATAFLOW),
    )(pltpu.with_memory_space_constraint(sums, pltpu.HBM), land)
    return list(outs[:4]), outs[4]


def chip_wait(name, handle, after):
    send_sems, recv_sems, sums, land = handle

    def body(sums_ref, land_ref, send_ref, recv_ref, after_ref, sums_out, land_out):
        for cp in _chip_copies(sums_ref, land_ref, send_ref, recv_ref):
            cp.wait_send()
            cp.wait_recv()

    return pl.pallas_call(
        body, name=name, out_shape=(pltpu.HBM(sums.shape, sums.dtype), pltpu.HBM(land.shape, land.dtype)),
        in_specs=[_HBM, _HBM, _SEM, _SEM, pl.BlockSpec(memory_space=pl.ANY)], out_specs=(_HBM, _HBM),
        input_output_aliases={0: 0, 1: 1}, compiler_params=pltpu.CompilerParams(has_side_effects=_DATAFLOW),
    )(sums, land, send_sems, recv_sems, after)


def adamw_reduce(name, parts, w, m, v):
    nl, r, wd = w.shape
    tr = _row_tile(r, ROW_TILE_BYTES // (8 * wd))

    def body(*refs):
        p_refs = refs[:nl]
        w_ref, m_ref, v_ref, g_ref, d_ref, nm_ref, nv_ref = refs[nl:]
        layer = pl.program_id(0)
        for j in range(nl):
            @pl.when(layer == j)
            def _(j=j):
                g = p_refs[j][0].astype(F32)
                for sender in range(1, p_refs[j].shape[0]):
                    g = g + p_refs[j][sender].astype(F32)
                nm = B1 * m_ref[...] + (1.0 - B1) * g
                nv = B2 * v_ref[...] + (1.0 - B2) * jnp.square(g)
                m_hat = nm / (1.0 - B1 ** STEP)
                v_hat = nv / (1.0 - B2 ** STEP)
                g_ref[...] = g
                d_ref[...] = -LR * (m_hat / (jnp.sqrt(v_hat) + ADAM_EPS) + WD * w_ref[...])
                nm_ref[...] = nm
                nv_ref[...] = nv

    def part_spec(j):
        return pl.BlockSpec((parts[j].shape[0], tr, wd), lambda l, i: (0, jnp.where(l == j, i, 0), 0))

    spec = pl.BlockSpec((None, tr, wd), lambda l, i: (l, i, 0))
    return pl.pallas_call(
        body, name=name, grid=(nl, r // tr),
        in_specs=[part_spec(j) for j in range(nl)] + [spec, spec, spec],
        out_specs=[spec] * 4, out_shape=[jax.ShapeDtypeStruct((nl, r, wd), F32)] * 4,
        compiler_params=_params(("arbitrary", "arbitrary")),
    )(*parts, w, m, v)


def _pack_rows(vectors, rows=None):
    flat = jnp.concatenate([a.reshape(-1).astype(F32) for a in vectors])
    n = flat.shape[0]
    if rows is None:
        rows = -(-n // 1024) * 8
    return jnp.pad(flat, (0, rows * 128 - n)).reshape(rows, 128)


def _unpack_rows(packed, like):
    flat = packed.reshape(-1)
    out, pos = [], 0
    for a in like:
        out.append(flat[pos:pos + a.size].reshape(a.shape))
        pos += a.size
    return out


def kernel(x, p, mix_norm, mlp_norm, ple_norm, w_a_in, a_lb_logits, a_head_gain, w_a_out, kv_norm, w_kvf, b_f, w_b_q, w_b_out, w_mlp_up, w_mlp_down, w_ple_gate, w_ple_up, final_norm, loss_target, m_mix_norm, m_mlp_norm, m_ple_norm, m_w_a_in, m_a_lb_logits, m_a_head_gain, m_w_a_out, m_kv_norm, m_w_kvf, m_b_f, m_w_b_q, m_w_b_out, m_w_mlp_up, m_w_mlp_down, m_w_ple_gate, m_w_ple_up, m_final_norm, v_mix_norm, v_mlp_norm, v_ple_norm, v_w_a_in, v_a_lb_logits, v_a_head_gain, v_w_a_out, v_kv_norm, v_w_kvf, v_b_f, v_w_b_q, v_w_b_out, v_w_mlp_up, v_w_mlp_down, v_w_ple_gate, v_w_ple_up, v_final_norm):
    t, d = x.shape[1], x.shape[2]
    nh = d // HEAD_DIM
    n_layers = 2
    x2 = x.reshape(t, d)
    target = loss_target.reshape(t, d)
    me = _my_index()

    shards = {"w_a_in": w_a_in[0], "w_a_out": w_a_out[0], "w_kvf": w_kvf, "w_b_q": w_b_q[0], "w_b_out": w_b_out[0]}
    for l in range(n_layers):
        shards.update({f"w_mlp_up{l}": w_mlp_up[l], f"w_mlp_down{l}": w_mlp_down[l],
                       f"w_ple_gate{l}": w_ple_gate[l], f"w_ple_up{l}": w_ple_up[l]})
    first_use = ["a_lb_logits", "w_a_in", "w_a_out", "w_mlp_up0", "w_mlp_down0", "w_ple_gate0", "w_ple_up0", "w_kvf",
                 "w_b_q", "w_b_out", "w_mlp_up1", "w_mlp_down1", "w_ple_gate1", "w_ple_up1"]
    row_sharded = ("w_a_out", "w_b_q", "w_b_out", "w_mlp_down", "w_ple_gate")
    shards_bf = [a_lb_logits] + [shards[n].astype(BF16) for n in first_use[1:]]
    ag_handles = gather_start("ag_start", [
        lax.dynamic_update_slice(lax.empty((N_DEV,) + a.shape, a.dtype), a[None], (me, 0, 0)) for a in shards_bf])
    passed_on = {}
    weights = {}

    def pass_on(j, after):
        if j < len(first_use) and j not in passed_on:
            passed_on[j] = gather_pass_on("ag_pass_" + first_use[j], ag_handles[j], after)

    def weight(name, after=None):
        if name not in weights:
            j = first_use.index(name)
            pass_on(j, after)
            pass_on(j + 1, after)
            behind = passed_on[j + 1][1] if j + 1 in passed_on else after
            g = gather_wait("ag_wait_" + name, passed_on[j][0], behind)
            if name.rstrip("01") in row_sharded:
                g = g.reshape(1, g.shape[0] * g.shape[1], g.shape[2])
            weights[name] = g
        return weights[name]

    lgt = weight("a_lb_logits", x2).transpose(1, 0, 2).reshape(2, d)
    p_bf = [p[l, 0].astype(BF16) for l in range(n_layers)]

    def row(vec):
        return vec.reshape(1, -1)

    def mlp_ple_fwd(l, h_in, a):
        (h_a, u_mlp), _ = rowwise(f"add_norm_mlp{l}", _add_norm_fwd, [h_in, a], [row(mlp_norm[l])])
        pre, act = mm_nn(f"mlp_up{l}", u_mlp, weight(f"w_mlp_up{l}", u_mlp), fuse=(_relu2, (), (BF16, BF16)))
        mo = mm_nn(f"mlp_down{l}", act, weight(f"w_mlp_down{l}", act))
        (h_b, u_ple), _ = rowwise(f"add_norm_ple{l}", _add_norm_fwd, [h_a, mo], [row(ple_norm[l])])
        gpre = mm_nn(f"ple_gate{l}", u_ple, weight(f"w_ple_gate{l}", u_ple))
        pu = mm_nn(f"ple_up{l}", p_bf[l], weight(f"w_ple_up{l}", gpre))
        return dict(h_a=h_a, u_mlp=u_mlp, pre=pre, act=act, h_b=h_b, u_ple=u_ple, gpre=gpre, pu=pu)

    (u0,), _ = rowwise("norm_mix0", _norm_fwd, [x2], [row(mix_norm[0])])
    z = mm_nn("a_in", u0, weight("w_a_in", u0))
    og, states = hgrn_fwd(z, lgt, a_head_gain)
    a0 = mm_nn("a_out", og, weight("w_a_out", og))
    s0 = mlp_ple_fwd(0, x2, a0)
    (h3, u_kv, u1), _ = rowwise("ple_norms", _ple_two_norms_fwd, [s0["h_b"], s0["gpre"], s0["pu"]],
                                [row(kv_norm), row(mix_norm[1])])
    w_kvf_cols = weight("w_kvf", u_kv).transpose(1, 0, 2).reshape(d, 2 * d + nh)
    w_kv = w_kvf_cols[:, :2 * d].reshape(d, 2, d).transpose(1, 0, 2)
    w_f = jnp.pad(w_kvf_cols[:, 2 * d:], ((0, 0), (0, HEAD_DIM - nh)))[None]
    kv = mm_nn("kvf", u_kv, w_kv, out_dtype=BF16)
    fl_t = mm_nn("kvf_forget", u_kv, w_f)[:, :nh].T
    b_f_col = b_f.reshape(nh, 1)
    dcum = decay_fwd(fl_t, b_f_col)
    dcol, drow = dcum.reshape(nh, t, 1), dcum.reshape(nh, 1, t)
    q = mm_nn("b_q", u1, weight("w_b_q", dcum))
    o, lse = attn_fwd(q, kv, dcol, drow)
    a1 = mm_nn("b_out", o, weight("w_b_out", o))
    s1 = mlp_ple_fwd(1, h3, a1)

    (dh, dgpre, dpu), (d_final, loss_rows) = rowwise(
        "tail", _tail_fwd_bwd, [s1["h_b"], s1["gpre"], s1["pu"], target], [row(final_norm)])

    sent = {}
    tokens = []

    two_level = ("w_mlp_up0", "w_a_in")
    swapping = []

    def send_grad(name, g):
        g = g.reshape(N_DEV, -1, g.shape[-1])
        if name in two_level:
            sent[name], token = pair_start("rs_pair_" + name, g)
            swapping.append(name)
        else:
            (sent[name],), token = scatter_start("rs_start_" + name, [g])
        tokens.append(token)

    def send_grads_together(grads):
        names = list(grads)
        handles, token = scatter_start("rs_start_" + names[0], [
            grads[n].reshape(N_DEV, -1, grads[n].shape[-1]) for n in names])
        sent.update(zip(names, handles))
        tokens.append(token)

    def second_stage(after):
        for name in swapping:
            g, half = pair_wait("rs_pairwait_" + name, sent[name], after)
            sent[name], token = chip_start("rs_chip_" + name, pair_sum("rs_sum_" + name, g, half))
            tokens.append(token)
        swapping.clear()

    def after_sends():
        deps = tuple(tokens)
        tokens.clear()
        return deps

    def mlp_ple_bwd(l, s, dh, dgpre, dpu):
        send_grads_together({
            f"w_ple_gate{l}": mm_tn(f"d_ple_gate_w{l}", s["u_ple"], dgpre, 1, deps=after_sends()),
            f"w_ple_up{l}": mm_tn(f"d_ple_up_w{l}", p_bf[l], dpu, N_DEV)})
        du = mm_nt(f"d_ple_gate_x{l}", dgpre, weight(f"w_ple_gate{l}"), deps=after_sends())
        (dh, dh_bf), (d_ple,) = rowwise(f"d_norm_ple{l}", _norm_bwd, [s["h_b"], du, dh], [row(ple_norm[l])])
        send_grad(f"w_mlp_down{l}", mm_tn(f"d_mlp_down_w{l}", s["act"], dh_bf, 1))
        (dpre,) = mm_nt(f"d_mlp_down_x{l}", dh_bf, weight(f"w_mlp_down{l}"), deps=after_sends(),
                        fuse=(_relu2_bwd, (s["pre"],), (BF16,)))
        second_stage(dpre)
        send_grad(f"w_mlp_up{l}", mm_tn(f"d_mlp_up_w{l}", s["u_mlp"], dpre, N_DEV))
        du = mm_nt(f"d_mlp_up_x{l}", dpre, weight(f"w_mlp_up{l}"), deps=after_sends())
        second_stage(du)
        (dh, dh_bf), (d_mlp,) = rowwise(f"d_norm_mlp{l}", _norm_bwd, [s["h_a"], du, dh], [row(mlp_norm[l])])
        return dh, dh_bf, d_ple, d_mlp

    dh, dh_bf, d_ple1, d_mlp1 = mlp_ple_bwd(1, s1, dh, dgpre, dpu)
    send_grad("w_b_out", mm_tn("d_b_out_w", o, dh_bf, 1))
    do = mm_nt("d_b_out_x", dh_bf, weight("w_b_out"), out_dtype=BF16, deps=after_sends())
    dq, dk, dv, ddrow = attn_bwd(q, kv, dcol, drow, lse, do)
    send_grad("w_b_q", mm_tn("d_b_q_w", u1, dq, 1))
    du1 = mm_nt("d_b_q_x", dq, weight("w_b_q"), deps=after_sends())
    dfl_t, d_b_f = decay_bwd(fl_t, b_f_col, ddrow.reshape(nh, t))
    dkv = jnp.stack([dk, dv]).astype(BF16)
    dfl = jnp.pad(dfl_t.T, ((0, 0), (0, HEAD_DIM - nh))).astype(BF16)
    d_w_kv = mm_tn("d_kvf_w", u_kv, dkv, 2)
    d_w_f = mm_tn("d_kvf_forget_w", u_kv, dfl, 1)
    d_w_kvf = jnp.concatenate([d_w_kv[0], d_w_kv[1], d_w_f[0, :, :nh]], axis=1)
    send_grad("w_kvf", d_w_kvf.reshape(d, N_DEV, -1).transpose(1, 0, 2))
    du_f = mm_nt("d_kvf_forget_x", dfl, w_f, deps=after_sends())
    (du_kv,) = mm_nt("d_kvf_x", dkv, w_kv, fuse=(lambda acc, extra: (acc + extra,), (du_f,), (F32,)))
    second_stage(du_kv)
    (dh, dgpre, dpu), (d_kv_norm, d_mix1) = rowwise(
        "d_ple_norms", _two_norms_ple_bwd, [h3, du_kv, du1, dh, s0["gpre"], s0["pu"]],
        [row(kv_norm), row(mix_norm[1])])
    dh, dh_bf, d_ple0, d_mlp0 = mlp_ple_bwd(0, s0, dh, dgpre, dpu)
    send_grad("w_a_out", mm_tn("d_a_out_w", og, dh_bf, 1))
    dog = mm_nt("d_a_out_x", dh_bf, weight("w_a_out"), deps=after_sends())
    dz4, d_lgt, d_hg = hgrn_bwd(z, lgt, a_head_gain, states, dog)
    send_grad("w_a_in", mm_tn("d_a_in_w", u0, dz4, N_DEV, stacked=True))
    du0 = mm_nt("d_a_in_x", dz4, weight("w_a_in"), deps=after_sends(), stacked=True)
    second_stage(du0)
    (dx, _), (d_mix0,) = rowwise("d_norm_mix0", _norm_bwd, [x2, du0, dh], [row(mix_norm[0])])

    new = {}
    last = [dx]

    def update(name, parts, w, m, v):
        shp = w.shape
        w3, m3, v3 = (a.reshape(len(parts), -1, shp[-1]) for a in (w, m, v))
        new[name] = tuple(a.reshape(shp) for a in adamw_reduce("adamw_" + name, parts, w3, m3, v3))
        last[0] = new[name][0]

    def receive_update(name, layers, w, m, v):
        parts = {}
        for sfx in layers:
            if name + sfx in two_level:
                mine, land = chip_wait(f"rs_wait_{name}{sfx}", sent[name + sfx], last[0])
                slot = me // 2
            else:
                mine, land = scatter_wait(f"rs_wait_{name}{sfx}", sent[name + sfx], last[0])
                slot = me
            parts[sfx] = lax.dynamic_update_slice(land, lax.dynamic_slice_in_dim(mine, slot, 1, 0), (slot, 0, 0))
        update(name, [parts[sfx] for sfx in sorted(layers)], w, m, v)

    both = ("1", "0")
    receive_update("w_b_out", ("",), w_b_out, m_w_b_out, v_w_b_out)
    receive_update("w_b_q", ("",), w_b_q, m_w_b_q, v_w_b_q)
    receive_update("w_kvf", ("",), w_kvf, m_w_kvf, v_w_kvf)
    receive_update("w_ple_gate", both, w_ple_gate, m_w_ple_gate, v_w_ple_gate)
    receive_update("w_ple_up", both, w_ple_up, m_w_ple_up, v_w_ple_up)
    receive_update("w_mlp_down", both, w_mlp_down, m_w_mlp_down, v_w_mlp_down)
    receive_update("w_mlp_up", both, w_mlp_up, m_w_mlp_up, v_w_mlp_up)
    receive_update("w_a_out", ("",), w_a_out, m_w_a_out, v_w_a_out)
    receive_update("w_a_in", ("",), w_a_in, m_w_a_in, v_w_a_in)

    small = dict(mix_norm=jnp.concatenate([d_mix0, d_mix1]), mlp_norm=jnp.concatenate([d_mlp0, d_mlp1]),
                 ple_norm=jnp.concatenate([d_ple0, d_ple1]), a_head_gain=d_hg, kv_norm=d_kv_norm.reshape(d),
                 b_f=d_b_f.reshape(nh), final_norm=d_final.reshape(d))
    small_w = dict(mix_norm=(mix_norm, m_mix_norm, v_mix_norm), mlp_norm=(mlp_norm, m_mlp_norm, v_mlp_norm),
                   ple_norm=(ple_norm, m_ple_norm, v_ple_norm),
                   a_head_gain=(a_head_gain, m_a_head_gain, v_a_head_gain), kv_norm=(kv_norm, m_kv_norm, v_kv_norm),
                   b_f=(b_f, m_b_f, v_b_f), final_norm=(final_norm, m_final_norm, v_final_norm))
    names = list(small)
    packed = _pack_rows([d_lgt] + [small[n] for n in names])
    everyone = all_gather("ag_small_grads", packed, deps=(last[0],))
    n_lgt_rows = d_lgt.size // 128
    lgt_parts = everyone[:, :n_lgt_rows].reshape(N_DEV, 2, d)
    lgt_parts = lax.dynamic_slice_in_dim(lgt_parts, me * a_lb_logits.shape[1], a_lb_logits.shape[1], axis=2)
    update("a_lb_logits", [lgt_parts], a_lb_logits, m_a_lb_logits, v_a_lb_logits)
    rest = everyone[:, n_lgt_rows:]
    like = [small_w[n][0] for n in names]
    packed_w, packed_m, packed_v = (_pack_rows([small_w[n][j] for n in names], rest.shape[1])[None] for j in range(3))
    outs = adamw_reduce("adamw_small", [rest], packed_w, packed_m, packed_v)
    unpacked = [_unpack_rows(a, like) for a in outs]
    for j, n in enumerate(names):
        new[n] = tuple(unpacked[q][j] for q in range(4))

    order = ["mix_norm", "mlp_norm", "ple_norm", "w_a_in", "a_lb_logits", "a_head_gain", "w_a_out", "kv_norm",
             "w_kvf", "b_f", "w_b_q", "w_b_out", "w_mlp_up", "w_mlp_down", "w_ple_gate", "w_ple_up", "final_norm"]
    loss_here, _ = lax.optimization_barrier((loss_rows[0, 0], new["final_norm"][0]))
    loss = lax.psum(loss_here, MESH_AXES)
    result = [loss, dx.reshape(x.shape)]
    for j in range(4):
        result += [new[n][j] for n in order]
    return tuple(result)
```

```python
import jax
import jax.numpy as jnp
from jax import lax
from jax.experimental import pallas as pl
from jax.experimental.pallas import tpu as pltpu

F32 = jnp.float32
BF16 = jnp.bfloat16
HEAD_DIM = 128
CHUNK = 16
TILE = 128
HEADS_PER_STEP = 4
NORM_EPS = 1e-6
N_DEV = 8
MESH_AXES = ("x", "y", "c")
VMEM_LIMIT_BYTES = 48 * 1024 * 1024
ROW_TILE_BYTES = 2 * 1024 * 1024
LR, B1, B2, ADAM_EPS, WD, STEP = 0.001, 0.9, 0.999, 1e-08, 0.01, 10
NEG_BIG = -1e30

NN = (((1,), (0,)), ((), ()))
NT = (((1,), (1,)), ((), ()))
TN = (((0,), (0,)), ((), ()))


def _params(semantics):
    return pltpu.CompilerParams(dimension_semantics=semantics, vmem_limit_bytes=VMEM_LIMIT_BYTES)


def _tile(n, prefs):
    for p in prefs:
        if n % p == 0:
            return p
    return n


def _row_tile(rows, limit):
    for cand in (2048, 1024, 512, 256, 128, 64, 32, 16):
        if cand <= limit and rows % cand == 0:
            return cand
    return rows


def _mm_call(name, a, b, dims, grid, a_spec, b_spec, o_spec, o_shape, acc_shape, k_axes, out_dtype, deps=(),
             fuse=None, split=1):
    nk = 1
    for ax in k_axes:
        nk *= grid[ax]
    fn, extra, out_dtypes = fuse if fuse else (lambda acc: (acc,), (), (out_dtype,))
    n_extra, n_out = len(extra), len(out_dtypes)

    def finish(acc, rest):
        o_refs = rest[n_extra + len(deps):n_extra + len(deps) + n_out]
        for ref, val in zip(o_refs, fn(acc, *[r[...] for r in rest[:n_extra]])):
            ref[...] = val.astype(ref.dtype)

    def product(a_ref, b_ref):
        if split == 1:
            return lax.dot_general(a_ref[...], b_ref[...], dims, preferred_element_type=F32)
        wide = a_ref.shape[1] // split
        return sum(lax.dot_general(a_ref[:, q * wide:(q + 1) * wide], b_ref[q], dims, preferred_element_type=F32)
                   for q in range(split))

    def one_step(a_ref, b_ref, *rest):
        finish(product(a_ref, b_ref), rest)

    def accumulate(a_ref, b_ref, *rest):
        acc_ref = rest[-1]
        k = 0
        for ax in k_axes:
            k = k * grid[ax] + pl.program_id(ax)
        part = product(a_ref, b_ref)

        @pl.when(k == 0)
        def _():
            acc_ref[...] = part

        @pl.when((k > 0) & (k < nk - 1))
        def _():
            acc_ref[...] += part

        @pl.when(k == nk - 1)
        def _():
            finish(acc_ref[...] + part, rest)

    sem = tuple("arbitrary" if ax in k_axes else "parallel" for ax in range(len(grid)))
    outs = pl.pallas_call(
        one_step if nk == 1 else accumulate, name=name, grid=grid,
        in_specs=[a_spec, b_spec] + [o_spec] * n_extra + [pl.BlockSpec(memory_space=pl.ANY)] * len(deps),
        out_specs=[o_spec] * n_out, out_shape=[jax.ShapeDtypeStruct(o_shape, dt) for dt in out_dtypes],
        scratch_shapes=[] if nk == 1 else [pltpu.VMEM(acc_shape, F32)], compiler_params=_params(sem),
    )(a, b, *extra, *deps)
    return outs if fuse else outs[0]


def mm_nn(name, a, b3, out_dtype=F32, out3=False, deps=(), fuse=None):
    m, k = a.shape
    g, _, n = b3.shape
    tm, tk = _tile(m, (1024, 512, 256)), _tile(k, (2048, 1024, 512, 256))
    tn = n if out3 else _tile(n, (1024, 512, 256, 128))
    nj = n // tn
    grid = (m // tm, g, nj, k // tk)
    a_spec = pl.BlockSpec((tm, tk), lambda i, gg, j, kk: (i, kk))
    b_spec = pl.BlockSpec((None, tk, tn), lambda i, gg, j, kk: (gg, kk, j))
    if out3:
        o_spec = pl.BlockSpec((None, tm, tn), lambda i, gg, j, kk: (gg, i, j))
        o_shape = (g, m, n)
    else:
        o_spec = pl.BlockSpec((tm, tn), lambda i, gg, j, kk: (i, gg * nj + j))
        o_shape = (m, g * n)
    return _mm_call(name, a, b3, NN, grid, a_spec, b_spec, o_spec, o_shape, (tm, tn), (3,), out_dtype, deps, fuse)


def mm_nt(name, a, b3, out_dtype=F32, deps=(), fuse=None, stacked=False):
    g, k, n = b3.shape
    a3 = a.ndim == 3 and not stacked
    m = a.shape[1] if a.ndim == 3 else a.shape[0]
    tm, tko = _tile(m, (1024, 512, 256)), _tile(k, (1024, 512, 256))
    tc = n if a3 else _tile(n, (2048, 1024, 512, 256, 128))
    nc = n // tc
    per = g // a.shape[0] if stacked else g
    pair = 2 if (not a3 and nc == 1 and per % 2 == 0 and tc <= 1024) else 1
    grid = (m // tm, k // tko, g // pair, nc)
    if a3:
        a_spec = pl.BlockSpec((None, tm, tc), lambda i, j, gg, c: (gg, i, c))
    elif stacked:
        a_spec = pl.BlockSpec((None, tm, pair * tc),
                              lambda i, j, gg, c: ((gg * pair) // per, i, (((gg * pair) % per) // pair) * nc + c))
    else:
        a_spec = pl.BlockSpec((tm, pair * tc), lambda i, j, gg, c: (i, gg * nc + c))
    if pair == 1:
        b_spec = pl.BlockSpec((None, tko, tc), lambda i, j, gg, c: (gg, j, c))
    else:
        b_spec = pl.BlockSpec((pair, tko, tc), lambda i, j, gg, c: (gg, j, c))
    o_spec = pl.BlockSpec((tm, tko), lambda i, j, gg, c: (i, j))
    return _mm_call(name, a, b3, NT, grid, a_spec, b_spec, o_spec, (m, k), (tm, tko), (2, 3), out_dtype, deps, fuse,
                    pair)


def mm_tn(name, a, b, g, out_dtype=BF16, deps=(), stacked=False):
    t, k = a.shape
    b3 = b.ndim == 3 and not stacked
    n = b.shape[2] if b3 else (b.shape[0] * b.shape[2] if stacked else b.shape[1]) // g
    tm = _tile(k, (1024, 512, 256))
    tn = n if b3 else _tile(n, (1024, 512, 256, 128))
    tt = _tile(t, (2048, 1024, 512, 256))
    nj = n // tn
    grid = (g, k // tm, nj, t // tt)
    a_spec = pl.BlockSpec((tt, tm), lambda gg, i, j, s: (s, i))
    if b3:
        b_spec = pl.BlockSpec((None, tt, tn), lambda gg, i, j, s: (gg, s, j))
    elif stacked:
        per = g // b.shape[0]
        b_spec = pl.BlockSpec((None, tt, tn), lambda gg, i, j, s: (gg // per, s, (gg % per) * nj + j))
    else:
        b_spec = pl.BlockSpec((tt, tn), lambda gg, i, j, s: (s, gg * nj + j))
    o_spec = pl.BlockSpec((None, tm, tn), lambda gg, i, j, s: (gg, i, j))
    return _mm_call(name, a, b, TN, grid, a_spec, b_spec, o_spec, (g, k, n), (tm, tn), (3,), out_dtype, deps)


def rowwise(name, fn, rows, vecs=()):
    t = rows[0].shape[0]
    wmax = max(r.shape[1] for r in rows)
    tr = _row_tile(t, ROW_TILE_BYTES // (4 * wmax))
    row_s = [jax.ShapeDtypeStruct((tr, r.shape[1]), r.dtype) for r in rows]
    vec_s = [jax.ShapeDtypeStruct(v.shape, v.dtype) for v in vecs]
    out_rows_s, out_sums_s = jax.eval_shape(fn, *row_s, *vec_s)
    n_in, n_r = len(rows) + len(vecs), len(out_rows_s)

    def body(*refs):
        i = pl.program_id(0)
        o_rows, o_sums = fn(*[r[...] for r in refs[:n_in]])
        for ref, val in zip(refs[n_in:n_in + n_r], o_rows):
            ref[...] = val

        if out_sums_s:
            @pl.when(i == 0)
            def _():
                for ref in refs[n_in + n_r:]:
                    ref[...] = jnp.zeros_like(ref)

            for ref, val in zip(refs[n_in + n_r:], o_sums):
                ref[...] += val

    in_specs = [pl.BlockSpec((tr, r.shape[1]), lambda i: (i, 0)) for r in rows]
    in_specs += [pl.BlockSpec(v.shape, lambda i: (0, 0)) for v in vecs]
    out_specs = [pl.BlockSpec((tr, s.shape[1]), lambda i: (i, 0)) for s in out_rows_s]
    out_specs += [pl.BlockSpec(s.shape, lambda i: (0, 0)) for s in out_sums_s]
    out_shape = [jax.ShapeDtypeStruct((t, s.shape[1]), s.dtype) for s in out_rows_s]
    out_shape += [jax.ShapeDtypeStruct(s.shape, s.dtype) for s in out_sums_s]
    outs = pl.pallas_call(
        body, name=name, grid=(t // tr,), in_specs=in_specs, out_specs=out_specs, out_shape=out_shape,
        compiler_params=_params(("arbitrary",)),
    )(*rows, *vecs)
    return outs[:n_r], outs[n_r:]


def _rms(x, gain):
    return x * lax.rsqrt(jnp.mean(x * x, axis=-1, keepdims=True) + NORM_EPS) * gain


def _norm_fwd(x, gain):
    return (_rms(x, gain).astype(BF16),), ()


def _add_norm_fwd(h, a, gain):
    h = h + a
    return (h, _rms(h, gain).astype(BF16)), ()


def _relu2(pre):
    r = jnp.maximum(pre, 0.0)
    return pre, r * r


def _ple(h, gpre, pu):
    return h + pu * jax.nn.sigmoid(gpre)


def _ple_two_norms_fwd(h, gpre, pu, gain_a, gain_b):
    h = _ple(h, gpre, pu)
    return (h, _rms(h, gain_a).astype(BF16), _rms(h, gain_b).astype(BF16)), ()


def _tail_fwd_bwd(h, gpre, pu, target, gain):
    def row_loss(h, gpre, pu, gain):
        y = _rms(_ple(h, gpre, pu), gain)
        return 0.5 * jnp.mean(jnp.square(y - target), axis=-1, keepdims=True)

    loss, vjp = jax.vjp(row_loss, h, gpre, pu, gain)
    dh, dgpre, dpu, dgain = vjp(jnp.ones_like(loss))
    loss = jnp.broadcast_to(jnp.sum(loss, axis=0, keepdims=True), (1, 128))
    return (dh, dgpre.astype(BF16), dpu.astype(BF16)), (dgain, loss)


def _norm_bwd(h, du, dh_in, gain):
    _, vjp = jax.vjp(_rms, h, gain)
    dh, dgain = vjp(du)
    dh = dh_in + dh
    return (dh, dh.astype(BF16)), (dgain,)


def _two_norms_ple_bwd(h, du_a, du_b, dh_in, gpre, pu, gain_a, gain_b):
    _, vjp = jax.vjp(lambda h, ga, gb: (_rms(h, ga), _rms(h, gb)), h, gain_a, gain_b)
    dh, dga, dgb = vjp((du_a, du_b))
    dh = dh_in + dh
    _, gate_vjp = jax.vjp(lambda g, u: u * jax.nn.sigmoid(g), gpre, pu)
    dgpre, dpu = gate_vjp(dh)
    return (dh, dgpre.astype(BF16), dpu.astype(BF16)), (dga, dgb)


def _relu2_bwd(dact, pre):
    return (dact * 2.0 * jnp.maximum(pre.astype(F32), 0.0),)


def _bf16_dot(dims_fwd, dims_da, dims_db, swap_da, swap_db):
    @jax.custom_vjp
    def dot(a, b):
        return lax.dot_general(a.astype(BF16), b.astype(BF16), dims_fwd, preferred_element_type=F32)

    def fwd(a, b):
        return dot(a, b), (a, b)

    def bwd(res, ct):
        a, b = res
        ct, a, b = ct.astype(BF16), a.astype(BF16), b.astype(BF16)
        da = lax.dot_general(*((b, ct) if swap_da else (ct, b)), dims_da, preferred_element_type=F32)
        db = lax.dot_general(*((ct, a) if swap_db else (a, ct)), dims_db, preferred_element_type=F32)
        return da, db

    dot.defvjp(fwd, bwd)
    return dot


_dot_nn = _bf16_dot(NN, NT, TN, False, False)
_dot_nt = _bf16_dot(NT, NN, TN, False, True)
_dot_tn = _bf16_dot(TN, NT, NN, True, False)


def _chunk_causal_mask():
    r = lax.broadcasted_iota(jnp.int32, (TILE, TILE), 0)
    c = lax.broadcasted_iota(jnp.int32, (TILE, TILE), 1)
    return ((r // CHUNK) == (c // CHUNK)) & (c <= r)


def _chunk_scan(x, reverse):
    pos = lax.broadcasted_iota(jnp.int32, x.shape, 0) % CHUNK
    step = 1
    while step < CHUNK:
        if reverse:
            x = x + jnp.where(pos < CHUNK - step, pltpu.roll(x, x.shape[0] - step, axis=0), 0.0)
        else:
            x = x + jnp.where(pos >= step, pltpu.roll(x, step, axis=0), 0.0)
        step *= 2
    return x


def _chunk_total(x):
    return _chunk_scan(x, False) + _chunk_scan(x, True) - x


@jax.custom_vjp
def _chunk_sums(x):
    return _chunk_scan(x, False), _chunk_total(x)


def _chunk_sums_fwd(x):
    return _chunk_sums(x), None


def _chunk_sums_bwd(_, ct):
    return (_chunk_scan(ct[0], True) + _chunk_total(ct[1]),)


_chunk_sums.defvjp(_chunk_sums_fwd, _chunk_sums_bwd)


def _hgrn_tile(q, f, i, g, lgt, hg, st):
    d = q.shape[1]
    l0, l1 = lgt[0:1], lgt[1:2]
    mx = jnp.maximum(l0, l1)
    e0, e1 = jnp.exp(l0 - mx), jnp.exp(l1 - mx)
    lb = e0 / (e0 + e1)
    fg = lb + (1.0 - lb) * jax.nn.sigmoid(f)
    k = 1.0 - fg
    causal = _chunk_causal_mask()
    b, b_last = _chunk_sums(jnp.log(fg))
    q_in = q * jax.nn.sigmoid(q) * (d ** -0.5) * jnp.exp(b)
    k_in = k * jnp.exp(-b)
    k_end = k * jnp.exp(b_last - b)
    att = jnp.where(causal, _dot_nt(q_in, k_in), 0.0)
    o_intra = _dot_nn(att, i)
    n_chunks = TILE // CHUNK
    chunk_of_row = lax.broadcasted_iota(jnp.int32, (TILE, 1), 0) // CHUNK

    def spread(a):
        return jnp.concatenate([jnp.where(chunk_of_row == n, a, 0.0) for n in range(n_chunks)], axis=1)

    increments = _dot_tn(i, spread(k_end))
    states = []
    for n in range(n_chunks):
        states.append(st)
        decay = jnp.exp(jnp.mean(b_last[n * CHUNK:(n + 1) * CHUNK], axis=0, keepdims=True))
        st = st * decay + increments[:, n * d:(n + 1) * d]
    o = o_intra + _dot_nt(spread(q_in), jnp.concatenate(states, axis=1))
    o = o * lax.rsqrt(jnp.mean(o * o, axis=-1, keepdims=True) + NORM_EPS) * hg
    return o * (g * jax.nn.sigmoid(g)), st


def hgrn_fwd(z, lgt, hg):
    t, d4 = z.shape
    d = d4 // 4
    nh, nt = d // HEAD_DIM, t // TILE
    hp = HEADS_PER_STEP
    wide = hp * HEAD_DIM

    def body(q_ref, f_ref, i_ref, g_ref, lgt_ref, hg_ref, o_ref, st_out_ref, st_ref):
        tt = pl.program_id(1)

        @pl.when(tt == 0)
        def _():
            st_ref[...] = jnp.zeros_like(st_ref)

        for hh in range(hp):
            cols = slice(hh * HEAD_DIM, (hh + 1) * HEAD_DIM)
            st = st_ref[hh]
            st_out_ref[hh] = st
            o, st = _hgrn_tile(q_ref[:, cols], f_ref[:, cols], i_ref[:, cols], g_ref[:, cols], lgt_ref[:, cols],
                               hg_ref[...], st)
            o_ref[:, cols] = o.astype(o_ref.dtype)
            st_ref[hh] = st

    def part(p):
        return pl.BlockSpec((TILE, wide), lambda h, tt: (tt, p * (nh // hp) + h))

    return pl.pallas_call(
        body, name="hgrn_fwd", grid=(nh // hp, nt),
        in_specs=[part(0), part(1), part(2), part(3),
                  pl.BlockSpec((2, wide), lambda h, tt: (0, h)),
                  pl.BlockSpec((1, HEAD_DIM), lambda h, tt: (0, 0))],
        out_specs=[pl.BlockSpec((TILE, wide), lambda h, tt: (tt, h)),
                   pl.BlockSpec((hp, None, HEAD_DIM, HEAD_DIM), lambda h, tt: (h, tt, 0, 0))],
        out_shape=[jax.ShapeDtypeStruct((t, d), BF16),
                   jax.ShapeDtypeStruct((nh, nt, HEAD_DIM, HEAD_DIM), F32)],
        scratch_shapes=[pltpu.VMEM((hp, HEAD_DIM, HEAD_DIM), F32)],
        compiler_params=_params(("parallel", "arbitrary")),
    )(z, z, z, z, lgt, hg)


def hgrn_bwd(z, lgt, hg, states, dout):
    t, d4 = z.shape
    d = d4 // 4
    nh, nt = d // HEAD_DIM, t // TILE
    hp = HEADS_PER_STEP
    wide = hp * HEAD_DIM

    def body(q_ref, f_ref, i_ref, g_ref, lgt_ref, hg_ref, st_in_ref, do_ref, dz_ref, dlgt_ref, dhg_ref, dst_ref):
        h, tt = pl.program_id(0), pl.program_id(1)

        @pl.when(tt == 0)
        def _():
            dst_ref[...] = jnp.zeros_like(dst_ref)
            dlgt_ref[...] = jnp.zeros_like(dlgt_ref)

        @pl.when((tt == 0) & (h == 0))
        def _():
            dhg_ref[...] = jnp.zeros_like(dhg_ref)

        for hh in range(hp):
            cols = slice(hh * HEAD_DIM, (hh + 1) * HEAD_DIM)
            _, vjp = jax.vjp(_hgrn_tile, q_ref[:, cols], f_ref[:, cols], i_ref[:, cols], g_ref[:, cols],
                             lgt_ref[:, cols], hg_ref[...], st_in_ref[hh])
            grads = vjp((do_ref[:, cols], dst_ref[hh]))
            for p in range(4):
                dz_ref[p, :, cols] = grads[p].astype(dz_ref.dtype)
            dlgt_ref[:, cols] += grads[4]
            dhg_ref[...] += grads[5]
            dst_ref[hh] = grads[6]

    def part(p):
        return pl.BlockSpec((TILE, wide), lambda h, tt: (nt - 1 - tt, p * (nh // hp) + h))

    return pl.pallas_call(
        body, name="hgrn_bwd", grid=(nh // hp, nt),
        in_specs=[part(0), part(1), part(2), part(3),
                  pl.BlockSpec((2, wide), lambda h, tt: (0, h)),
                  pl.BlockSpec((1, HEAD_DIM), lambda h, tt: (0, 0)),
                  pl.BlockSpec((hp, None, HEAD_DIM, HEAD_DIM), lambda h, tt: (h, nt - 1 - tt, 0, 0)),
                  pl.BlockSpec((TILE, wide), lambda h, tt: (nt - 1 - tt, h))],
        out_specs=[pl.BlockSpec((4, TILE, wide), lambda h, tt: (0, nt - 1 - tt, h)),
                   pl.BlockSpec((2, wide), lambda h, tt: (0, h)),
                   pl.BlockSpec((1, HEAD_DIM), lambda h, tt: (0, 0))],
        out_shape=[jax.ShapeDtypeStruct((4, t, d), BF16),
                   jax.ShapeDtypeStruct((2, d), F32),
                   jax.ShapeDtypeStruct((1, HEAD_DIM), F32)],
        scratch_shapes=[pltpu.VMEM((hp, HEAD_DIM, HEAD_DIM), F32)],
        compiler_params=_params(("arbitrary", "arbitrary")),
    )(z, z, z, z, lgt, hg, states, dout)


def _log_sigmoid(x):
    return jnp.minimum(x, 0.0) - jnp.log(1.0 + jnp.exp(-jnp.abs(x)))


def decay_fwd(fl_t, b_f):
    nh, t = fl_t.shape

    def body(fl_ref, b_ref, out_ref):
        r = lax.broadcasted_iota(jnp.int32, (128, 128), 0)
        c = lax.broadcasted_iota(jnp.int32, (128, 128), 1)
        upper = (r <= c).astype(F32)
        carry = jnp.zeros((nh, 1), F32)
        for j in range(t // 128):
            cols = slice(j * 128, (j + 1) * 128)
            ls = _log_sigmoid(fl_ref[:, cols] + b_ref[...])
            out_ref[:, cols] = carry + jnp.dot(ls, upper, precision=lax.Precision.HIGHEST,
                                               preferred_element_type=F32)
            carry = carry + jnp.sum(ls, axis=1, keepdims=True)

    return pl.pallas_call(body, name="decay_fwd", out_shape=jax.ShapeDtypeStruct((nh, t), F32),
                          compiler_params=_params(None))(fl_t, b_f)


def decay_bwd(fl_t, b_f, ddcum):
    nh, t = fl_t.shape

    def body(fl_ref, b_ref, dd_ref, dfl_ref, db_ref):
        r = lax.broadcasted_iota(jnp.int32, (128, 128), 0)
        c = lax.broadcasted_iota(jnp.int32, (128, 128), 1)
        lower = (r >= c).astype(F32)
        carry = jnp.zeros((nh, 1), F32)
        db = jnp.zeros((nh, 1), F32)
        for j in reversed(range(t // 128)):
            cols = slice(j * 128, (j + 1) * 128)
            dd = dd_ref[:, cols]
            dls = carry + jnp.dot(dd, lower, precision=lax.Precision.HIGHEST, preferred_element_type=F32)
            carry = carry + jnp.sum(dd, axis=1, keepdims=True)
            dfl = dls * jax.nn.sigmoid(-(fl_ref[:, cols] + b_ref[...]))
            dfl_ref[:, cols] = dfl
            db = db + jnp.sum(dfl, axis=1, keepdims=True)
        db_ref[...] = db

    return pl.pallas_call(body, name="decay_bwd",
                          out_shape=[jax.ShapeDtypeStruct((nh, t), F32), jax.ShapeDtypeStruct((nh, 1), F32)],
                          compiler_params=_params(None))(fl_t, b_f, ddcum)


def _attn_parts(t):
    tq = _tile(t, (256, 128))
    per_part = 2 if t // tq >= 4 else 1
    return tq, [(first, per_part, (first + per_part) * tq) for first in range(0, t // tq, per_part)]


def _attn_logits(q_ref, k_ref, dcol_ref, drow_ref, row0, tq, keys):
    qs = (q_ref[...] * (HEAD_DIM ** -0.5)).astype(BF16)
    s = lax.dot_general(qs, k_ref[...], NT, preferred_element_type=F32)
    s = s + dcol_ref[...] - drow_ref[...]
    row = row0 + lax.broadcasted_iota(jnp.int32, (tq, keys), 0)
    col = lax.broadcasted_iota(jnp.int32, (tq, keys), 1)
    return qs, jnp.where(col <= row, s, NEG_BIG)


def attn_fwd(q, kv, dcol, drow):
    t, d = q.shape
    nh = d // HEAD_DIM
    tq, parts = _attn_parts(t)
    o = lse = None
    for first, count, keys in parts:
        def body(q_ref, k_ref, v_ref, dcol_ref, drow_ref, *rest, first=first, keys=keys):
            o_ref, lse_ref = rest[-2:]
            _, s = _attn_logits(q_ref, k_ref, dcol_ref, drow_ref, (first + pl.program_id(1)) * tq, tq, keys)
            m = jnp.max(s, axis=1, keepdims=True)
            p = jnp.exp(s - m)
            l = jnp.sum(p, axis=1, keepdims=True)
            acc = jnp.dot(p.astype(BF16), v_ref[...], preferred_element_type=F32)
            o_ref[...] = (acc / l).astype(o_ref.dtype)
            lse_ref[...] = m + jnp.log(l)

        tile = pl.BlockSpec((tq, HEAD_DIM), lambda h, i, first=first: (first + i, h))
        col = pl.BlockSpec((None, tq, 1), lambda h, i, first=first: (h, first + i, 0))
        seen_k = pl.BlockSpec((keys, HEAD_DIM), lambda h, i: (0, h))
        seen_v = pl.BlockSpec((keys, HEAD_DIM), lambda h, i: (0, nh + h))
        carried = [] if o is None else [o, lse]
        o, lse = pl.pallas_call(
            body, name=f"attn_fwd_{first}", grid=(nh, count),
            in_specs=[tile, seen_k, seen_v, col, pl.BlockSpec((None, 1, keys), lambda h, i: (h, 0, 0))]
            + [pl.BlockSpec(memory_space=pl.ANY)] * len(carried),
            out_specs=[tile, col],
            out_shape=[jax.ShapeDtypeStruct((t, d), BF16), jax.ShapeDtypeStruct((nh, t, 1), F32)],
            input_output_aliases={5: 0, 6: 1} if carried else {},
            compiler_params=_params(("parallel", "parallel")),
        )(q, kv, kv, dcol, drow, *carried)
    return o, lse


def attn_bwd(q, kv, dcol, drow, lse, do):
    t, d = q.shape
    nh = d // HEAD_DIM
    tq, parts = _attn_parts(t)
    dq = dk = dv = ddrow = None
    for first, count, keys in reversed(parts):
        first_call = dq is None

        def body(q_ref, k_ref, v_ref, dcol_ref, drow_ref, lse_ref, do_ref, *rest, first=first, keys=keys,
                 count=count, first_call=first_call):
            dq_ref, dk_ref, dv_ref, ddrow_ref, dk_acc, dv_acc, ddrow_acc = rest[-7:]
            i = pl.program_id(1)

            @pl.when(i == 0)
            def _():
                if first_call:
                    dk_acc[...] = jnp.zeros_like(dk_acc)
                    dv_acc[...] = jnp.zeros_like(dv_acc)
                    ddrow_acc[...] = jnp.zeros_like(ddrow_acc)
                else:
                    dk_acc[...] = rest[1][...]
                    dv_acc[...] = rest[2][...]
                    ddrow_acc[...] = rest[3][...]

            qs, s = _attn_logits(q_ref, k_ref, dcol_ref, drow_ref, (first + i) * tq, tq, keys)
            p = jnp.exp(s - lse_ref[...])
            do = do_ref[...]
            dp = lax.dot_general(do, v_ref[...], NT, preferred_element_type=F32)
            ds = p * (dp - jnp.sum(p * dp, axis=1, keepdims=True))
            dsb = ds.astype(BF16)
            dq_ref[...] = (jnp.dot(dsb, k_ref[...], preferred_element_type=F32) * (HEAD_DIM ** -0.5)).astype(dq_ref.dtype)
            dk_acc[...] += lax.dot_general(dsb, qs, TN, preferred_element_type=F32)
            dv_acc[...] += lax.dot_general(p.astype(BF16), do, TN, preferred_element_type=F32)
            ddrow_acc[...] -= jnp.sum(ds, axis=0, keepdims=True)

            @pl.when(i == count - 1)
            def _():
                dk_ref[...] = dk_acc[...]
                dv_ref[...] = dv_acc[...]
                ddrow_ref[...] = ddrow_acc[...]

        tile = pl.BlockSpec((tq, HEAD_DIM), lambda h, i, first=first: (first + i, h))
        col = pl.BlockSpec((None, tq, 1), lambda h, i, first=first: (h, first + i, 0))
        seen = pl.BlockSpec((keys, HEAD_DIM), lambda h, i: (0, h))
        seen_v = pl.BlockSpec((keys, HEAD_DIM), lambda h, i: (0, nh + h))
        seen_row = pl.BlockSpec((None, 1, keys), lambda h, i: (h, 0, 0))
        carried = [] if first_call else [dq, dk, dv, ddrow]
        carried_specs = [] if first_call else [pl.BlockSpec(memory_space=pl.ANY), seen, seen, seen_row]
        dq, dk, dv, ddrow = pl.pallas_call(
            body, name=f"attn_bwd_{first}", grid=(nh, count),
            in_specs=[tile, seen, seen_v, col, seen_row, col, tile] + carried_specs,
            out_specs=[tile, seen, seen, seen_row],
            out_shape=[jax.ShapeDtypeStruct((t, d), BF16), jax.ShapeDtypeStruct((t, d), F32),
                       jax.ShapeDtypeStruct((t, d), F32), jax.ShapeDtypeStruct((nh, 1, t), F32)],
            scratch_shapes=[pltpu.VMEM((keys, HEAD_DIM), F32), pltpu.VMEM((keys, HEAD_DIM), F32),
                            pltpu.VMEM((1, keys), F32)],
            input_output_aliases={} if first_call else {7: 0, 8: 1, 9: 2, 10: 3},
            compiler_params=_params(("parallel", "arbitrary")),
        )(q, kv, kv, dcol, drow, lse, do, *carried)
    return dq, dk, dv, ddrow


def _my_index():
    return (lax.axis_index("x") * 2 + lax.axis_index("y")) * 2 + lax.axis_index("c")


def all_gather(name, src, deps=()):
    def body(src_ref, *rest):
        out_ref, send_sems, recv_sems, local_sem = rest[len(deps):]
        x, y, c = (lax.axis_index(a) for a in MESH_AXES)
        me = (x * 2 + y) * 2 + c
        local = pltpu.make_async_copy(src_ref, out_ref.at[me], local_sem)
        local.start()
        copies = []
        for dlt in range(1, N_DEV):
            copies.append(pltpu.make_async_remote_copy(
                src_ref=src_ref, dst_ref=out_ref.at[me], send_sem=send_sems.at[dlt - 1],
                recv_sem=recv_sems.at[dlt - 1], device_id=(x ^ (dlt // 4), y ^ ((dlt // 2) % 2), c ^ (dlt % 2)),
                device_id_type=pl.DeviceIdType.MESH))
        for cp in copies:
            cp.start()
        for cp in copies:
            cp.wait_recv()
        for cp in copies:
            cp.wait_send()
        local.wait()

    return pl.pallas_call(
        body, name=name, out_shape=jax.ShapeDtypeStruct((N_DEV,) + tuple(src.shape), src.dtype),
        in_specs=[pl.BlockSpec(memory_space=pl.ANY)] * (1 + len(deps)), out_specs=pl.BlockSpec(memory_space=pl.ANY),
        scratch_shapes=[pltpu.SemaphoreType.DMA((N_DEV - 1,)), pltpu.SemaphoreType.DMA((N_DEV - 1,)),
                        pltpu.SemaphoreType.DMA],
        compiler_params=pltpu.CompilerParams(has_side_effects=True),
    )(src, *deps)


_HBM = pl.BlockSpec(memory_space=pltpu.HBM)
_SEM = pl.BlockSpec(memory_space=pltpu.SEMAPHORE)
_DATAFLOW = pltpu.SideEffectType.DATAFLOW_SIDE_EFFECTING


def _peer_copies(src_ref, land_ref, send_sems, recv_sems):
    x, y, c = (lax.axis_index(a) for a in MESH_AXES)
    me = (x * 2 + y) * 2 + c
    copies = []
    for dlt in range(1, N_DEV):
        px, py, pc = x ^ (dlt // 4), y ^ ((dlt // 2) % 2), c ^ (dlt % 2)
        peer = (px * 2 + py) * 2 + pc
        copies.append(pltpu.make_async_remote_copy(
            src_ref=src_ref.at[peer], dst_ref=land_ref.at[me],
            send_sem=send_sems.at[dlt - 1], recv_sem=recv_sems.at[dlt - 1],
            device_id=(px, py, pc), device_id_type=pl.DeviceIdType.MESH))
    return copies


def _own_copy(src_ref, land_ref, send_sems):
    me = (lax.axis_index("x") * 2 + lax.axis_index("y")) * 2 + lax.axis_index("c")
    return pltpu.make_async_copy(src_ref.at[me], land_ref.at[me], send_sems.at[N_DEV - 1])


def scatter_start(name, srcs):
    n = len(srcs)
    lands = [lax.empty(s.shape, s.dtype) for s in srcs]

    def body(*refs):
        src_refs, land_refs = refs[:n], refs[n:2 * n]
        send_sems, recv_sems = refs[2 * n:3 * n], refs[3 * n:4 * n]
        token = refs[-1]
        for j in range(n):
            for cp in _peer_copies(src_refs[j], land_refs[j], send_sems[j], recv_sems[j]):
                cp.start()
            _own_copy(src_refs[j], land_refs[j], send_sems[j]).start()
        token[...] = jnp.zeros_like(token)

    sems = [pltpu.SemaphoreType.DMA((N_DEV,))] * n + [pltpu.SemaphoreType.DMA((N_DEV - 1,))] * n
    thru = [pltpu.HBM(a.shape, a.dtype) for a in list(srcs) + lands]
    outs = pl.pallas_call(
        body, name=name, out_shape=tuple(sems + thru + [jax.ShapeDtypeStruct((8, 128), F32)]),
        in_specs=[_HBM] * (2 * n), out_specs=tuple([_SEM] * (2 * n) + [_HBM] * (2 * n) + [pl.BlockSpec(memory_space=pltpu.VMEM)]),
        input_output_aliases={j: 2 * n + j for j in range(2 * n)},
        compiler_params=pltpu.CompilerParams(has_side_effects=_DATAFLOW),
    )(*[pltpu.with_memory_space_constraint(a, pltpu.HBM) for a in list(srcs) + lands])
    handles = [(outs[j], outs[n + j], outs[2 * n + j], outs[3 * n + j]) for j in range(n)]
    return handles, outs[-1]


def scatter_wait(name, handle, after):
    send_sems, recv_sems, src, land = handle

    def body(src_ref, land_ref, send_ref, recv_ref, after_ref, src_out, land_out):
        for cp in _peer_copies(src_ref, land_ref, send_ref, recv_ref):
            cp.wait_send()
            cp.wait_recv()
        _own_copy(src_ref, land_ref, send_ref).wait()

    return pl.pallas_call(
        body, name=name, out_shape=(pltpu.HBM(src.shape, src.dtype), pltpu.HBM(land.shape, land.dtype)),
        in_specs=[_HBM, _HBM, _SEM, _SEM, pl.BlockSpec(memory_space=pl.ANY)], out_specs=(_HBM, _HBM),
        input_output_aliases={0: 0, 1: 1},
        compiler_params=pltpu.CompilerParams(has_side_effects=_DATAFLOW),
    )(src, land, send_sems, recv_sems, after)


N_OTHER_CHIPS = 3


def _two_level_places():
    x, y, c = (lax.axis_index(a) for a in MESH_AXES)
    return (x, y, c), (x * 2 + y) * 2 + c, (x, y, 1 - c), [(1 - x, y), (x, 1 - y), (1 - x, 1 - y)]


def _first_copies(land_ref, send_sems, recv_sems):
    (x, y, c), me, other_core, chips = _two_level_places()
    targets = [other_core] + [(cx, cy, c) for cx, cy in chips]
    return [pltpu.make_async_remote_copy(
        src_ref=land_ref.at[me], dst_ref=land_ref.at[me], send_sem=send_sems.at[k], recv_sem=recv_sems.at[k],
        device_id=to, device_id_type=pl.DeviceIdType.MESH) for k, to in enumerate(targets)]


def _passed_on_copies(land_ref, send_sems, recv_sems):
    (x, y, c), me, other_core, chips = _two_level_places()
    copies = []
    for k, (cx, cy) in enumerate(chips):
        slot = land_ref.at[(cx * 2 + cy) * 2 + c]
        copies.append(pltpu.make_async_remote_copy(
            src_ref=slot, dst_ref=slot, send_sem=send_sems.at[k], recv_sem=recv_sems.at[k],
            device_id=other_core, device_id_type=pl.DeviceIdType.MESH))
    return copies


def gather_start(name, lands):
    n = len(lands)

    def body(*refs):
        land_refs, send_sems, recv_sems = refs[:n], refs[n:2 * n], refs[2 * n:3 * n]
        for j in range(n):
            for cp in _first_copies(land_refs[j], send_sems[j], recv_sems[j]):
                cp.start()

    sems = [pltpu.SemaphoreType.DMA((1 + N_OTHER_CHIPS,))] * (2 * n)
    outs = pl.pallas_call(
        body, name=name, out_shape=tuple(sems + [pltpu.HBM(a.shape, a.dtype) for a in lands]),
        in_specs=[_HBM] * n, out_specs=tuple([_SEM] * (2 * n) + [_HBM] * n),
        input_output_aliases={j: 2 * n + j for j in range(n)},
        compiler_params=pltpu.CompilerParams(has_side_effects=_DATAFLOW),
    )(*[pltpu.with_memory_space_constraint(a, pltpu.HBM) for a in lands])
    return [[outs[j], outs[n + j], outs[2 * n + j]] for j in range(n)]


def gather_pass_on(name, handle, after):
    send_sems, recv_sems, land = handle

    def body(land_ref, recv_ref, after_ref, land_out, send2, recv2, token):
        arrivals = _first_copies(land_ref, recv_ref, recv_ref)
        for k, cp in enumerate(_passed_on_copies(land_ref, send2, recv2)):
            arrivals[1 + k].wait_recv()
            cp.start()
        token[...] = jnp.zeros_like(token)

    sem3 = pltpu.SemaphoreType.DMA((N_OTHER_CHIPS,))
    land, send2, recv2, token = pl.pallas_call(
        body, name=name,
        out_shape=(pltpu.HBM(land.shape, land.dtype), sem3, sem3, jax.ShapeDtypeStruct((8, 128), F32)),
        in_specs=[_HBM, _SEM, pl.BlockSpec(memory_space=pl.ANY)],
        out_specs=(_HBM, _SEM, _SEM, pl.BlockSpec(memory_space=pltpu.VMEM)),
        input_output_aliases={0: 0}, compiler_params=pltpu.CompilerParams(has_side_effects=_DATAFLOW),
    )(land, recv_sems, after)
    return [send_sems, recv_sems, land, send2, recv2], token


def gather_wait(name, handle, after):
    send_sems, recv_sems, land, send2, recv2 = handle

    def body(land_ref, send_ref, recv_ref, send2_ref, recv2_ref, after_ref, land_out):
        first = _first_copies(land_ref, send_ref, recv_ref)
        for cp in first:
            cp.wait_send()
        first[0].wait_recv()
        for cp in _passed_on_copies(land_ref, send2_ref, recv2_ref):
            cp.wait_send()
            cp.wait_recv()

    return pl.pallas_call(
        body, name=name, out_shape=pltpu.HBM(land.shape, land.dtype),
        in_specs=[_HBM, _SEM, _SEM, _SEM, _SEM, pl.BlockSpec(memory_space=pl.ANY)], out_specs=_HBM,
        input_output_aliases={0: 0}, compiler_params=pltpu.CompilerParams(has_side_effects=_DATAFLOW),
    )(land, send_sems, recv_sems, send2, recv2, after)


N_CHIPS = 4


def _pair_copies(g_ref, half_ref, send_sems, recv_sems):
    (x, y, c), me, other_core, chips = _two_level_places()
    return [pltpu.make_async_remote_copy(
        src_ref=g_ref.at[chip * 2 + (1 - c)], dst_ref=half_ref.at[chip], send_sem=send_sems.at[chip],
        recv_sem=recv_sems.at[chip], device_id=other_core, device_id_type=pl.DeviceIdType.MESH)
        for chip in range(N_CHIPS)]


def pair_start(name, g):
    half = lax.empty((N_CHIPS,) + g.shape[1:], g.dtype)

    def body(g_ref, half_ref, send_sems, recv_sems, g_out, half_out, token):
        for cp in _pair_copies(g_ref, half_ref, send_sems, recv_sems):
            cp.start()
        token[...] = jnp.zeros_like(token)

    sem = pltpu.SemaphoreType.DMA((N_CHIPS,))
    outs = pl.pallas_call(
        body, name=name,
        out_shape=(sem, sem, pltpu.HBM(g.shape, g.dtype), pltpu.HBM(half.shape, half.dtype),
                   jax.ShapeDtypeStruct((8, 128), F32)),
        in_specs=[_HBM, _HBM], out_specs=(_SEM, _SEM, _HBM, _HBM, pl.BlockSpec(memory_space=pltpu.VMEM)),
        input_output_aliases={0: 2, 1: 3}, compiler_params=pltpu.CompilerParams(has_side_effects=_DATAFLOW),
    )(pltpu.with_memory_space_constraint(g, pltpu.HBM), half)
    return list(outs[:4]), outs[4]


def pair_wait(name, handle, after):
    send_sems, recv_sems, g, half = handle

    def body(g_ref, half_ref, send_ref, recv_ref, after_ref, g_out, half_out):
        for cp in _pair_copies(g_ref, half_ref, send_ref, recv_ref):
            cp.wait_send()
            cp.wait_recv()

    return pl.pallas_call(
        body, name=name, out_shape=(pltpu.HBM(g.shape, g.dtype), pltpu.HBM(half.shape, half.dtype)),
        in_specs=[_HBM, _HBM, _SEM, _SEM, pl.BlockSpec(memory_space=pl.ANY)], out_specs=(_HBM, _HBM),
        input_output_aliases={0: 0, 1: 1}, compiler_params=pltpu.CompilerParams(has_side_effects=_DATAFLOW),
    )(g, half, send_sems, recv_sems, after)


def pair_sum(name, g, half):
    _, r, wd = g.shape
    tr = _row_tile(r, 2 * ROW_TILE_BYTES // (4 * wd))
    kind = lax.axis_index("c").astype(jnp.int32).reshape(1)

    def body(kind_ref, g_ref, half_ref, o_ref):
        o_ref[...] = (g_ref[...].astype(F32) + half_ref[...].astype(F32)).astype(o_ref.dtype)

    spec = pl.BlockSpec((None, tr, wd), lambda chip, i, kind_ref: (chip, i, 0))
    return pl.pallas_call(
        body, name=name,
        grid_spec=pltpu.PrefetchScalarGridSpec(
            num_scalar_prefetch=1, grid=(N_CHIPS, r // tr),
            in_specs=[pl.BlockSpec((None, tr, wd), lambda chip, i, kind_ref: (chip * 2 + kind_ref[0], i, 0)), spec],
            out_specs=spec),
        out_shape=jax.ShapeDtypeStruct((N_CHIPS, r, wd), g.dtype),
        compiler_params=_params(("parallel", "parallel")),
    )(kind, g, half)


def _chip_copies(sums_ref, land_ref, send_sems, recv_sems):
    (x, y, c), me, other_core, chips = _two_level_places()
    return [pltpu.make_async_remote_copy(
        src_ref=sums_ref.at[cx * 2 + cy], dst_ref=land_ref.at[x * 2 + y], send_sem=send_sems.at[k],
        recv_sem=recv_sems.at[k], device_id=(cx, cy, c), device_id_type=pl.DeviceIdType.MESH)
        for k, (cx, cy) in enumerate(chips)]


def chip_start(name, sums):
    land = lax.empty(sums.shape, sums.dtype)

    def body(sums_ref, land_ref, send_sems, recv_sems, sums_out, land_out, token):
        for cp in _chip_copies(sums_ref, land_ref, send_sems, recv_sems):
            cp.start()
        token[...] = jnp.zeros_like(token)

    sem = pltpu.SemaphoreType.DMA((N_OTHER_CHIPS,))
    outs = pl.pallas_call(
        body, name=name,
        out_shape=(sem, sem, pltpu.HBM(sums.shape, sums.dtype), pltpu.HBM(land.shape, land.dtype),
                   jax.ShapeDtypeStruct((8, 128), F32)),
        in_specs=[_HBM, _HBM], out_specs=(_SEM, _SEM, _HBM, _HBM, pl.BlockSpec(memory_space=pltpu.VMEM)),
        input_output_aliases={0: 2, 1: 3}, compiler_params=pltpu.CompilerParams(has_side_effects=_DATAFLOW),
    )(pltpu.with_memory_space_constraint(sums, pltpu.HBM), land)
    return list(outs[:4]), outs[4]


def chip_wait(name, handle, after):
    send_sems, recv_sems, sums, land = handle

    def body(sums_ref, land_ref, send_ref, recv_ref, after_ref, sums_out, land_out):
        for cp in _chip_copies(sums_ref, land_ref, send_ref, recv_ref):
            cp.wait_send()
            cp.wait_recv()

    return pl.pallas_call(
        body, name=name, out_shape=(pltpu.HBM(sums.shape, sums.dtype), pltpu.HBM(land.shape, land.dtype)),
        in_specs=[_HBM, _HBM, _SEM, _SEM, pl.BlockSpec(memory_space=pl.ANY)], out_specs=(_HBM, _HBM),
        input_output_aliases={0: 0, 1: 1}, compiler_params=pltpu.CompilerParams(has_side_effects=_DATAFLOW),
    )(sums, land, send_sems, recv_sems, after)


def adamw_reduce(name, parts, w, m, v):
    nl, r, wd = w.shape
    tr = _row_tile(r, ROW_TILE_BYTES // (8 * wd))

    def body(*refs):
        p_refs = refs[:nl]
        w_ref, m_ref, v_ref, g_ref, d_ref, nm_ref, nv_ref = refs[nl:]
        layer = pl.program_id(0)
        for j in range(nl):
            @pl.when(layer == j)
            def _(j=j):
                g = p_refs[j][0].astype(F32)
                for sender in range(1, p_refs[j].shape[0]):
                    g = g + p_refs[j][sender].astype(F32)
                nm = B1 * m_ref[...] + (1.0 - B1) * g
                nv = B2 * v_ref[...] + (1.0 - B2) * jnp.square(g)
                m_hat = nm / (1.0 - B1 ** STEP)
                v_hat = nv / (1.0 - B2 ** STEP)
                g_ref[...] = g
                d_ref[...] = -LR * (m_hat / (jnp.sqrt(v_hat) + ADAM_EPS) + WD * w_ref[...])
                nm_ref[...] = nm
                nv_ref[...] = nv

    def part_spec(j):
        return pl.BlockSpec((parts[j].shape[0], tr, wd), lambda l, i: (0, jnp.where(l == j, i, 0), 0))

    spec = pl.BlockSpec((None, tr, wd), lambda l, i: (l, i, 0))
    return pl.pallas_call(
        body, name=name, grid=(nl, r // tr),
        in_specs=[part_spec(j) for j in range(nl)] + [spec, spec, spec],
        out_specs=[spec] * 4, out_shape=[jax.ShapeDtypeStruct((nl, r, wd), F32)] * 4,
        compiler_params=_params(("arbitrary", "arbitrary")),
    )(*parts, w, m, v)


def _pack_rows(vectors, rows=None):
    flat = jnp.concatenate([a.reshape(-1).astype(F32) for a in vectors])
    n = flat.shape[0]
    if rows is None:
        rows = -(-n // 1024) * 8
    return jnp.pad(flat, (0, rows * 128 - n)).reshape(rows, 128)


def _unpack_rows(packed, like):
    flat = packed.reshape(-1)
    out, pos = [], 0
    for a in like:
        out.append(flat[pos:pos + a.size].reshape(a.shape))
        pos += a.size
    return out


def kernel(x, p, mix_norm, mlp_norm, ple_norm, w_a_in, a_lb_logits, a_head_gain, w_a_out, kv_norm, w_kvf, b_f, w_b_q, w_b_out, w_mlp_up, w_mlp_down, w_ple_gate, w_ple_up, final_norm, loss_target, m_mix_norm, m_mlp_norm, m_ple_norm, m_w_a_in, m_a_lb_logits, m_a_head_gain, m_w_a_out, m_kv_norm, m_w_kvf, m_b_f, m_w_b_q, m_w_b_out, m_w_mlp_up, m_w_mlp_down, m_w_ple_gate, m_w_ple_up, m_final_norm, v_mix_norm, v_mlp_norm, v_ple_norm, v_w_a_in, v_a_lb_logits, v_a_head_gain, v_w_a_out, v_kv_norm, v_w_kvf, v_b_f, v_w_b_q, v_w_b_out, v_w_mlp_up, v_w_mlp_down, v_w_ple_gate, v_w_ple_up, v_final_norm):
    t, d = x.shape[1], x.shape[2]
    nh = d // HEAD_DIM
    n_layers = 2
    x2 = x.reshape(t, d)
    target = loss_target.reshape(t, d)
    me = _my_index()

    shards = {"w_a_in": w_a_in[0], "w_a_out": w_a_out[0], "w_kvf": w_kvf, "w_b_q": w_b_q[0], "w_b_out": w_b_out[0]}
    for l in range(n_layers):
        shards.update({f"w_mlp_up{l}": w_mlp_up[l], f"w_mlp_down{l}": w_mlp_down[l],
                       f"w_ple_gate{l}": w_ple_gate[l], f"w_ple_up{l}": w_ple_up[l]})
    first_use = ["a_lb_logits", "w_a_in", "w_a_out", "w_mlp_up0", "w_mlp_down0", "w_ple_gate0", "w_ple_up0", "w_kvf",
                 "w_b_q", "w_b_out", "w_mlp_up1", "w_mlp_down1", "w_ple_gate1", "w_ple_up1"]
    row_sharded = ("w_a_out", "w_b_q", "w_b_out", "w_mlp_down", "w_ple_gate")
    shards_bf = [a_lb_logits] + [shards[n].astype(BF16) for n in first_use[1:]]
    ag_handles = gather_start("ag_start", [
        lax.dynamic_update_slice(lax.empty((N_DEV,) + a.shape, a.dtype), a[None], (me, 0, 0)) for a in shards_bf])
    passed_on = {}
    weights = {}

    def pass_on(j, after):
        if j < len(first_use) and j not in passed_on:
            passed_on[j] = gather_pass_on("ag_pass_" + first_use[j], ag_handles[j], after)

    def weight(name, after=None):
        if name not in weights:
            j = first_use.index(name)
            pass_on(j, after)
            pass_on(j + 1, after)
            behind = passed_on[j + 1][1] if j + 1 in passed_on else after
            g = gather_wait("ag_wait_" + name, passed_on[j][0], behind)
            if name.rstrip("01") in row_sharded:
                g = g.reshape(1, g.shape[0] * g.shape[1], g.shape[2])
            weights[name] = g
        return weights[name]

    lgt = weight("a_lb_logits", x2).transpose(1, 0, 2).reshape(2, d)
    p_bf = [p[l, 0].astype(BF16) for l in range(n_layers)]

    def row(vec):
        return vec.reshape(1, -1)

    def mlp_ple_fwd(l, h_in, a):
        (h_a, u_mlp), _ = rowwise(f"add_norm_mlp{l}", _add_norm_fwd, [h_in, a], [row(mlp_norm[l])])
        pre, act = mm_nn(f"mlp_up{l}", u_mlp, weight(f"w_mlp_up{l}", u_mlp), fuse=(_relu2, (), (BF16, BF16)))
        mo = mm_nn(f"mlp_down{l}", act, weight(f"w_mlp_down{l}", act))
        (h_b, u_ple), _ = rowwise(f"add_norm_ple{l}", _add_norm_fwd, [h_a, mo], [row(ple_norm[l])])
        gpre = mm_nn(f"ple_gate{l}", u_ple, weight(f"w_ple_gate{l}", u_ple))
        pu = mm_nn(f"ple_up{l}", p_bf[l], weight(f"w_ple_up{l}", gpre))
        return dict(h_a=h_a, u_mlp=u_mlp, pre=pre, act=act, h_b=h_b, u_ple=u_ple, gpre=gpre, pu=pu)

    (u0,), _ = rowwise("norm_mix0", _norm_fwd, [x2], [row(mix_norm[0])])
    z = mm_nn("a_in", u0, weight("w_a_in", u0))
    og, states = hgrn_fwd(z, lgt, a_head_gain)
    a0 = mm_nn("a_out", og, weight("w_a_out", og))
    s0 = mlp_ple_fwd(0, x2, a0)
    (h3, u_kv, u1), _ = rowwise("ple_norms", _ple_two_norms_fwd, [s0["h_b"], s0["gpre"], s0["pu"]],
                                [row(kv_norm), row(mix_norm[1])])
    w_kvf_cols = weight("w_kvf", u_kv).transpose(1, 0, 2).reshape(d, 2 * d + nh)
    w_kv = w_kvf_cols[:, :2 * d].reshape(d, 2, d).transpose(1, 0, 2)
    w_f = jnp.pad(w_kvf_cols[:, 2 * d:], ((0, 0), (0, HEAD_DIM - nh)))[None]
    kv = mm_nn("kvf", u_kv, w_kv, out_dtype=BF16)
    fl_t = mm_nn("kvf_forget", u_kv, w_f)[:, :nh].T
    b_f_col = b_f.reshape(nh, 1)
    dcum = decay_fwd(fl_t, b_f_col)
    dcol, drow = dcum.reshape(nh, t, 1), dcum.reshape(nh, 1, t)
    q = mm_nn("b_q", u1, weight("w_b_q", dcum))
    o, lse = attn_fwd(q, kv, dcol, drow)
    a1 = mm_nn("b_out", o, weight("w_b_out", o))
    s1 = mlp_ple_fwd(1, h3, a1)

    (dh, dgpre, dpu), (d_final, loss_rows) = rowwise(
        "tail", _tail_fwd_bwd, [s1["h_b"], s1["gpre"], s1["pu"], target], [row(final_norm)])

    sent = {}
    tokens = []

    two_level = ("w_mlp_up0", "w_a_in")
    swapping = []

    def send_grad(name, g):
        g = g.reshape(N_DEV, -1, g.shape[-1])
        if name in two_level:
            sent[name], token = pair_start("rs_pair_" + name, g)
            swapping.append(name)
        else:
            (sent[name],), token = scatter_start("rs_start_" + name, [g])
        tokens.append(token)

    def send_grads_together(grads):
        names = list(grads)
        handles, token = scatter_start("rs_start_" + names[0], [
            grads[n].reshape(N_DEV, -1, grads[n].shape[-1]) for n in names])
        sent.update(zip(names, handles))
        tokens.append(token)

    def second_stage(after):
        for name in swapping:
            g, half = pair_wait("rs_pairwait_" + name, sent[name], after)
            sent[name], token = chip_start("rs_chip_" + name, pair_sum("rs_sum_" + name, g, half))
            tokens.append(token)
        swapping.clear()

    def after_sends():
        deps = tuple(tokens)
        tokens.clear()
        return deps

    def mlp_ple_bwd(l, s, dh, dgpre, dpu):
        send_grads_together({
            f"w_ple_gate{l}": mm_tn(f"d_ple_gate_w{l}", s["u_ple"], dgpre, 1, deps=after_sends()),
            f"w_ple_up{l}": mm_tn(f"d_ple_up_w{l}", p_bf[l], dpu, N_DEV)})
        du = mm_nt(f"d_ple_gate_x{l}", dgpre, weight(f"w_ple_gate{l}"), deps=after_sends())
        (dh, dh_bf), (d_ple,) = rowwise(f"d_norm_ple{l}", _norm_bwd, [s["h_b"], du, dh], [row(ple_norm[l])])
        send_grad(f"w_mlp_down{l}", mm_tn(f"d_mlp_down_w{l}", s["act"], dh_bf, 1))
        (dpre,) = mm_nt(f"d_mlp_down_x{l}", dh_bf, weight(f"w_mlp_down{l}"), deps=after_sends(),
                        fuse=(_relu2_bwd, (s["pre"],), (BF16,)))
        second_stage(dpre)
        send_grad(f"w_mlp_up{l}", mm_tn(f"d_mlp_up_w{l}", s["u_mlp"], dpre, N_DEV))
        du = mm_nt(f"d_mlp_up_x{l}", dpre, weight(f"w_mlp_up{l}"), deps=after_sends())
        second_stage(du)
        (dh, dh_bf), (d_mlp,) = rowwise(f"d_norm_mlp{l}", _norm_bwd, [s["h_a"], du, dh], [row(mlp_norm[l])])
        return dh, dh_bf, d_ple, d_mlp

    dh, dh_bf, d_ple1, d_mlp1 = mlp_ple_bwd(1, s1, dh, dgpre, dpu)
    send_grad("w_b_out", mm_tn("d_b_out_w", o, dh_bf, 1))
    do = mm_nt("d_b_out_x", dh_bf, weight("w_b_out"), out_dtype=BF16, deps=after_sends())
    dq, dk, dv, ddrow = attn_bwd(q, kv, dcol, drow, lse, do)
    send_grad("w_b_q", mm_tn("d_b_q_w", u1, dq, 1))
    du1 = mm_nt("d_b_q_x", dq, weight("w_b_q"), deps=after_sends())
    dfl_t, d_b_f = decay_bwd(fl_t, b_f_col, ddrow.reshape(nh, t))
    dkv = jnp.stack([dk, dv]).astype(BF16)
    dfl = jnp.pad(dfl_t.T, ((0, 0), (0, HEAD_DIM - nh))).astype(BF16)
    d_w_kv = mm_tn("d_kvf_w", u_kv, dkv, 2)
    d_w_f = mm_tn("d_kvf_forget_w", u_kv, dfl, 1)
    d_w_kvf = jnp.concatenate([d_w_kv[0], d_w_kv[1], d_w_f[0, :, :nh]], axis=1)
    send_grad("w_kvf", d_w_kvf.reshape(d, N_DEV, -1).transpose(1, 0, 2))
    du_f = mm_nt("d_kvf_forget_x", dfl, w_f, deps=after_sends())
    (du_kv,) = mm_nt("d_kvf_x", dkv, w_kv, fuse=(lambda acc, extra: (acc + extra,), (du_f,), (F32,)))
    second_stage(du_kv)
    (dh, dgpre, dpu), (d_kv_norm, d_mix1) = rowwise(
        "d_ple_norms", _two_norms_ple_bwd, [h3, du_kv, du1, dh, s0["gpre"], s0["pu"]],
        [row(kv_norm), row(mix_norm[1])])
    dh, dh_bf, d_ple0, d_mlp0 = mlp_ple_bwd(0, s0, dh, dgpre, dpu)
    send_grad("w_a_out", mm_tn("d_a_out_w", og, dh_bf, 1))
    dog = mm_nt("d_a_out_x", dh_bf, weight("w_a_out"), deps=after_sends())
    dz4, d_lgt, d_hg = hgrn_bwd(z, lgt, a_head_gain, states, dog)
    send_grad("w_a_in", mm_tn("d_a_in_w", u0, dz4, N_DEV, stacked=True))
    du0 = mm_nt("d_a_in_x", dz4, weight("w_a_in"), deps=after_sends(), stacked=True)
    second_stage(du0)
    (dx, _), (d_mix0,) = rowwise("d_norm_mix0", _norm_bwd, [x2, du0, dh], [row(mix_norm[0])])

    new = {}
    last = [dx]

    def update(name, parts, w, m, v):
        shp = w.shape
        w3, m3, v3 = (a.reshape(len(parts), -1, shp[-1]) for a in (w, m, v))
        new[name] = tuple(a.reshape(shp) for a in adamw_reduce("adamw_" + name, parts, w3, m3, v3))
        last[0] = new[name][0]

    def receive_update(name, layers, w, m, v):
        parts = {}
        for sfx in layers:
            if name + sfx in two_level:
                mine, land = chip_wait(f"rs_wait_{name}{sfx}", sent[name + sfx], last[0])
                chip = me // 2
                parts[sfx] = lax.dynamic_update_slice(land, lax.dynamic_slice_in_dim(mine, chip, 1, 0), (chip, 0, 0))
            else:
                _, parts[sfx] = scatter_wait(f"rs_wait_{name}{sfx}", sent[name + sfx], last[0])
        update(name, [parts[sfx] for sfx in sorted(layers)], w, m, v)

    both = ("1", "0")
    receive_update("w_b_out", ("",), w_b_out, m_w_b_out, v_w_b_out)
    receive_update("w_b_q", ("",), w_b_q, m_w_b_q, v_w_b_q)
    receive_update("w_kvf", ("",), w_kvf, m_w_kvf, v_w_kvf)
    receive_update("w_ple_gate", both, w_ple_gate, m_w_ple_gate, v_w_ple_gate)
    receive_update("w_ple_up", both, w_ple_up, m_w_ple_up, v_w_ple_up)
    receive_update("w_mlp_down", both, w_mlp_down, m_w_mlp_down, v_w_mlp_down)
    receive_update("w_mlp_up", both, w_mlp_up, m_w_mlp_up, v_w_mlp_up)
    receive_update("w_a_out", ("",), w_a_out, m_w_a_out, v_w_a_out)
    receive_update("w_a_in", ("",), w_a_in, m_w_a_in, v_w_a_in)

    small = dict(mix_norm=jnp.concatenate([d_mix0, d_mix1]), mlp_norm=jnp.concatenate([d_mlp0, d_mlp1]),
                 ple_norm=jnp.concatenate([d_ple0, d_ple1]), a_head_gain=d_hg, kv_norm=d_kv_norm.reshape(d),
                 b_f=d_b_f.reshape(nh), final_norm=d_final.reshape(d))
    small_w = dict(mix_norm=(mix_norm, m_mix_norm, v_mix_norm), mlp_norm=(mlp_norm, m_mlp_norm, v_mlp_norm),
                   ple_norm=(ple_norm, m_ple_norm, v_ple_norm),
                   a_head_gain=(a_head_gain, m_a_head_gain, v_a_head_gain), kv_norm=(kv_norm, m_kv_norm, v_kv_norm),
                   b_f=(b_f, m_b_f, v_b_f), final_norm=(final_norm, m_final_norm, v_final_norm))
    names = list(small)
    packed = _pack_rows([d_lgt] + [small[n] for n in names])
    everyone = all_gather("ag_small_grads", packed, deps=(last[0],))
    n_lgt_rows = d_lgt.size // 128
    lgt_parts = everyone[:, :n_lgt_rows].reshape(N_DEV, 2, d)
    lgt_parts = lax.dynamic_slice_in_dim(lgt_parts, me * a_lb_logits.shape[1], a_lb_logits.shape[1], axis=2)
    update("a_lb_logits", [lgt_parts], a_lb_logits, m_a_lb_logits, v_a_lb_logits)
    rest = everyone[:, n_lgt_rows:]
    like = [small_w[n][0] for n in names]
    packed_w, packed_m, packed_v = (_pack_rows([small_w[n][j] for n in names], rest.shape[1])[None] for j in range(3))
    outs = adamw_reduce("adamw_small", [rest], packed_w, packed_m, packed_v)
    unpacked = [_unpack_rows(a, like) for a in outs]
    for j, n in enumerate(names):
        new[n] = tuple(unpacked[q][j] for q in range(4))

    order = ["mix_norm", "mlp_norm", "ple_norm", "w_a_in", "a_lb_logits", "a_head_gain", "w_a_out", "kv_norm",
             "w_kvf", "b_f", "w_b_q", "w_b_out", "w_mlp_up", "w_mlp_down", "w_ple_gate", "w_ple_up", "final_norm"]
    loss_here, _ = lax.optimization_barrier((loss_rows[0, 0], new["final_norm"][0]))
    loss = lax.psum(loss_here, MESH_AXES)
    result = [loss, dx.reshape(x.shape)]
    for j in range(4):
        result += [new[n][j] for n in order]
    return tuple(result)
```

```python
import jax
import jax.numpy as jnp
from jax import lax
from jax.experimental import pallas as pl
from jax.experimental.pallas import tpu as pltpu

F32 = jnp.float32
BF16 = jnp.bfloat16
HEAD_DIM = 128
CHUNK = 16
TILE = 128
HEADS_PER_STEP = 4
NORM_EPS = 1e-6
N_DEV = 8
MESH_AXES = ("x", "y", "c")
VMEM_LIMIT_BYTES = 48 * 1024 * 1024
ROW_TILE_BYTES = 2 * 1024 * 1024
LR, B1, B2, ADAM_EPS, WD, STEP = 0.001, 0.9, 0.999, 1e-08, 0.01, 10
NEG_BIG = -1e30

NN = (((1,), (0,)), ((), ()))
NT = (((1,), (1,)), ((), ()))
TN = (((0,), (0,)), ((), ()))


def _params(semantics):
    return pltpu.CompilerParams(dimension_semantics=semantics, vmem_limit_bytes=VMEM_LIMIT_BYTES)


def _tile(n, prefs):
    for p in prefs:
        if n % p == 0:
            return p
    return n


def _row_tile(rows, limit):
    for cand in (2048, 1024, 512, 256, 128, 64, 32, 16):
        if cand <= limit and rows % cand == 0:
            return cand
    return rows


def _mm_call(name, a, b, dims, grid, a_spec, b_spec, o_spec, o_shape, acc_shape, k_axes, out_dtype, deps=(),
             fuse=None, split=1):
    nk = 1
    for ax in k_axes:
        nk *= grid[ax]
    fn, extra, out_dtypes = fuse if fuse else (lambda acc: (acc,), (), (out_dtype,))
    n_extra, n_out = len(extra), len(out_dtypes)

    def finish(acc, rest):
        o_refs = rest[n_extra + len(deps):n_extra + len(deps) + n_out]
        for ref, val in zip(o_refs, fn(acc, *[r[...] for r in rest[:n_extra]])):
            ref[...] = val.astype(ref.dtype)

    def product(a_ref, b_ref):
        if split == 1:
            return lax.dot_general(a_ref[...], b_ref[...], dims, preferred_element_type=F32)
        wide = a_ref.shape[1] // split
        return sum(lax.dot_general(a_ref[:, q * wide:(q + 1) * wide], b_ref[q], dims, preferred_element_type=F32)
                   for q in range(split))

    def one_step(a_ref, b_ref, *rest):
        finish(product(a_ref, b_ref), rest)

    def accumulate(a_ref, b_ref, *rest):
        acc_ref = rest[-1]
        k = 0
        for ax in k_axes:
            k = k * grid[ax] + pl.program_id(ax)
        part = product(a_ref, b_ref)

        @pl.when(k == 0)
        def _():
            acc_ref[...] = part

        @pl.when((k > 0) & (k < nk - 1))
        def _():
            acc_ref[...] += part

        @pl.when(k == nk - 1)
        def _():
            finish(acc_ref[...] + part, rest)

    sem = tuple("arbitrary" if ax in k_axes else "parallel" for ax in range(len(grid)))
    outs = pl.pallas_call(
        one_step if nk == 1 else accumulate, name=name, grid=grid,
        in_specs=[a_spec, b_spec] + [o_spec] * n_extra + [pl.BlockSpec(memory_space=pl.ANY)] * len(deps),
        out_specs=[o_spec] * n_out, out_shape=[jax.ShapeDtypeStruct(o_shape, dt) for dt in out_dtypes],
        scratch_shapes=[] if nk == 1 else [pltpu.VMEM(acc_shape, F32)], compiler_params=_params(sem),
    )(a, b, *extra, *deps)
    return outs if fuse else outs[0]


def mm_nn(name, a, b3, out_dtype=F32, out3=False, deps=(), fuse=None):
    m, k = a.shape
    g, _, n = b3.shape
    tm, tk = _tile(m, (1024, 512, 256)), _tile(k, (2048, 1024, 512, 256))
    tn = n if out3 else _tile(n, (1024, 512, 256, 128))
    nj = n // tn
    grid = (m // tm, g, nj, k // tk)
    a_spec = pl.BlockSpec((tm, tk), lambda i, gg, j, kk: (i, kk))
    b_spec = pl.BlockSpec((None, tk, tn), lambda i, gg, j, kk: (gg, kk, j))
    if out3:
        o_spec = pl.BlockSpec((None, tm, tn), lambda i, gg, j, kk: (gg, i, j))
        o_shape = (g, m, n)
    else:
        o_spec = pl.BlockSpec((tm, tn), lambda i, gg, j, kk: (i, gg * nj + j))
        o_shape = (m, g * n)
    return _mm_call(name, a, b3, NN, grid, a_spec, b_spec, o_spec, o_shape, (tm, tn), (3,), out_dtype, deps, fuse)


def mm_nt(name, a, b3, out_dtype=F32, deps=(), fuse=None, stacked=False):
    g, k, n = b3.shape
    a3 = a.ndim == 3 and not stacked
    m = a.shape[1] if a.ndim == 3 else a.shape[0]
    tm, tko = _tile(m, (1024, 512, 256)), _tile(k, (1024, 512, 256))
    tc = n if a3 else _tile(n, (2048, 1024, 512, 256, 128))
    nc = n // tc
    per = g // a.shape[0] if stacked else g
    pair = 2 if (not a3 and nc == 1 and per % 2 == 0 and tc <= 1024) else 1
    grid = (m // tm, k // tko, g // pair, nc)
    if a3:
        a_spec = pl.BlockSpec((None, tm, tc), lambda i, j, gg, c: (gg, i, c))
    elif stacked:
        a_spec = pl.BlockSpec((None, tm, pair * tc),
                              lambda i, j, gg, c: ((gg * pair) // per, i, (((gg * pair) % per) // pair) * nc + c))
    else:
        a_spec = pl.BlockSpec((tm, pair * tc), lambda i, j, gg, c: (i, gg * nc + c))
    if pair == 1:
        b_spec = pl.BlockSpec((None, tko, tc), lambda i, j, gg, c: (gg, j, c))
    else:
        b_spec = pl.BlockSpec((pair, tko, tc), lambda i, j, gg, c: (gg, j, c))
    o_spec = pl.BlockSpec((tm, tko), lambda i, j, gg, c: (i, j))
    return _mm_call(name, a, b3, NT, grid, a_spec, b_spec, o_spec, (m, k), (tm, tko), (2, 3), out_dtype, deps, fuse,
                    pair)


def mm_tn(name, a, b, g, out_dtype=BF16, deps=(), stacked=False):
    t, k = a.shape
    b3 = b.ndim == 3 and not stacked
    n = b.shape[2] if b3 else (b.shape[0] * b.shape[2] if stacked else b.shape[1]) // g
    tm = _tile(k, (1024, 512, 256))
    tn = n if b3 else _tile(n, (1024, 512, 256, 128))
    tt = _tile(t, (2048, 1024, 512, 256))
    nj = n // tn
    grid = (g, k // tm, nj, t // tt)
    a_spec = pl.BlockSpec((tt, tm), lambda gg, i, j, s: (s, i))
    if b3:
        b_spec = pl.BlockSpec((None, tt, tn), lambda gg, i, j, s: (gg, s, j))
    elif stacked:
        per = g // b.shape[0]
        b_spec = pl.BlockSpec((None, tt, tn), lambda gg, i, j, s: (gg // per, s, (gg % per) * nj + j))
    else:
        b_spec = pl.BlockSpec((tt, tn), lambda gg, i, j, s: (s, gg * nj + j))
    o_spec = pl.BlockSpec((None, tm, tn), lambda gg, i, j, s: (gg, i, j))
    return _mm_call(name, a, b, TN, grid, a_spec, b_spec, o_spec, (g, k, n), (tm, tn), (3,), out_dtype, deps)


def rowwise(name, fn, rows, vecs=()):
    t = rows[0].shape[0]
    wmax = max(r.shape[1] for r in rows)
    tr = _row_tile(t, ROW_TILE_BYTES // (4 * wmax))
    row_s = [jax.ShapeDtypeStruct((tr, r.shape[1]), r.dtype) for r in rows]
    vec_s = [jax.ShapeDtypeStruct(v.shape, v.dtype) for v in vecs]
    out_rows_s, out_sums_s = jax.eval_shape(fn, *row_s, *vec_s)
    n_in, n_r = len(rows) + len(vecs), len(out_rows_s)

    def body(*refs):
        i = pl.program_id(0)
        o_rows, o_sums = fn(*[r[...] for r in refs[:n_in]])
        for ref, val in zip(refs[n_in:n_in + n_r], o_rows):
            ref[...] = val

        if out_sums_s:
            @pl.when(i == 0)
            def _():
                for ref in refs[n_in + n_r:]:
                    ref[...] = jnp.zeros_like(ref)

            for ref, val in zip(refs[n_in + n_r:], o_sums):
                ref[...] += val

    in_specs = [pl.BlockSpec((tr, r.shape[1]), lambda i: (i, 0)) for r in rows]
    in_specs += [pl.BlockSpec(v.shape, lambda i: (0, 0)) for v in vecs]
    out_specs = [pl.BlockSpec((tr, s.shape[1]), lambda i: (i, 0)) for s in out_rows_s]
    out_specs += [pl.BlockSpec(s.shape, lambda i: (0, 0)) for s in out_sums_s]
    out_shape = [jax.ShapeDtypeStruct((t, s.shape[1]), s.dtype) for s in out_rows_s]
    out_shape += [jax.ShapeDtypeStruct(s.shape, s.dtype) for s in out_sums_s]
    outs = pl.pallas_call(
        body, name=name, grid=(t // tr,), in_specs=in_specs, out_specs=out_specs, out_shape=out_shape,
        compiler_params=_params(("arbitrary",)),
    )(*rows, *vecs)
    return outs[:n_r], outs[n_r:]


def _rms(x, gain):
    return x * lax.rsqrt(jnp.mean(x * x, axis=-1, keepdims=True) + NORM_EPS) * gain


def _norm_fwd(x, gain):
    return (_rms(x, gain).astype(BF16),), ()


def _add_norm_fwd(h, a, gain):
    h = h + a
    return (h, _rms(h, gain).astype(BF16)), ()


def _relu2(pre):
    r = jnp.maximum(pre, 0.0)
    return pre, r * r


def _ple(h, gpre, pu):
    return h + pu * jax.nn.sigmoid(gpre)


def _ple_two_norms_fwd(h, gpre, pu, gain_a, gain_b):
    h = _ple(h, gpre, pu)
    return (h, _rms(h, gain_a).astype(BF16), _rms(h, gain_b).astype(BF16)), ()


def _tail_fwd_bwd(h, gpre, pu, target, gain):
    def row_loss(h, gpre, pu, gain):
        y = _rms(_ple(h, gpre, pu), gain)
        return 0.5 * jnp.mean(jnp.square(y - target), axis=-1, keepdims=True)

    loss, vjp = jax.vjp(row_loss, h, gpre, pu, gain)
    dh, dgpre, dpu, dgain = vjp(jnp.ones_like(loss))
    loss = jnp.broadcast_to(jnp.sum(loss, axis=0, keepdims=True), (1, 128))
    return (dh, dgpre.astype(BF16), dpu.astype(BF16)), (dgain, loss)


def _norm_bwd(h, du, dh_in, gain):
    _, vjp = jax.vjp(_rms, h, gain)
    dh, dgain = vjp(du.astype(F32))
    dh = dh_in + dh
    return (dh, dh.astype(BF16)), (dgain,)


def _two_norms_ple_bwd(h, du_a, du_b, dh_in, gpre, pu, gain_a, gain_b):
    _, vjp = jax.vjp(lambda h, ga, gb: (_rms(h, ga), _rms(h, gb)), h, gain_a, gain_b)
    dh, dga, dgb = vjp((du_a.astype(F32), du_b.astype(F32)))
    dh = dh_in + dh
    _, gate_vjp = jax.vjp(lambda g, u: u * jax.nn.sigmoid(g), gpre, pu)
    dgpre, dpu = gate_vjp(dh)
    return (dh, dgpre.astype(BF16), dpu.astype(BF16)), (dga, dgb)


def _relu2_bwd(dact, pre):
    return (dact * 2.0 * jnp.maximum(pre.astype(F32), 0.0),)


def _bf16_dot(dims_fwd, dims_da, dims_db, swap_da, swap_db):
    @jax.custom_vjp
    def dot(a, b):
        return lax.dot_general(a.astype(BF16), b.astype(BF16), dims_fwd, preferred_element_type=F32)

    def fwd(a, b):
        return dot(a, b), (a, b)

    def bwd(res, ct):
        a, b = res
        ct, a, b = ct.astype(BF16), a.astype(BF16), b.astype(BF16)
        da = lax.dot_general(*((b, ct) if swap_da else (ct, b)), dims_da, preferred_element_type=F32)
        db = lax.dot_general(*((ct, a) if swap_db else (a, ct)), dims_db, preferred_element_type=F32)
        return da, db

    dot.defvjp(fwd, bwd)
    return dot


_dot_nn = _bf16_dot(NN, NT, TN, False, False)
_dot_nt = _bf16_dot(NT, NN, TN, False, True)
_dot_tn = _bf16_dot(TN, NT, NN, True, False)


def _chunk_causal_mask():
    r = lax.broadcasted_iota(jnp.int32, (TILE, TILE), 0)
    c = lax.broadcasted_iota(jnp.int32, (TILE, TILE), 1)
    return ((r // CHUNK) == (c // CHUNK)) & (c <= r)


def _chunk_scan(x, reverse):
    pos = lax.broadcasted_iota(jnp.int32, x.shape, 0) % CHUNK
    step = 1
    while step < CHUNK:
        if reverse:
            x = x + jnp.where(pos < CHUNK - step, pltpu.roll(x, x.shape[0] - step, axis=0), 0.0)
        else:
            x = x + jnp.where(pos >= step, pltpu.roll(x, step, axis=0), 0.0)
        step *= 2
    return x


def _chunk_total(x):
    return _chunk_scan(x, False) + _chunk_scan(x, True) - x


@jax.custom_vjp
def _chunk_sums(x):
    return _chunk_scan(x, False), _chunk_total(x)


def _chunk_sums_fwd(x):
    return _chunk_sums(x), None


def _chunk_sums_bwd(_, ct):
    return (_chunk_scan(ct[0], True) + _chunk_total(ct[1]),)


_chunk_sums.defvjp(_chunk_sums_fwd, _chunk_sums_bwd)


def _hgrn_tile(q, f, i, g, lgt, hg, st):
    d = q.shape[1]
    l0, l1 = lgt[0:1], lgt[1:2]
    mx = jnp.maximum(l0, l1)
    e0, e1 = jnp.exp(l0 - mx), jnp.exp(l1 - mx)
    lb = e0 / (e0 + e1)
    fg = lb + (1.0 - lb) * jax.nn.sigmoid(f)
    k = 1.0 - fg
    causal = _chunk_causal_mask()
    b, b_last = _chunk_sums(jnp.log(fg))
    q_in = q * jax.nn.sigmoid(q) * (d ** -0.5) * jnp.exp(b)
    k_in = k * jnp.exp(-b)
    k_end = k * jnp.exp(b_last - b)
    att = jnp.where(causal, _dot_nt(q_in, k_in), 0.0)
    o_intra = _dot_nn(att, i)
    n_chunks = TILE // CHUNK
    chunk_of_row = lax.broadcasted_iota(jnp.int32, (TILE, 1), 0) // CHUNK

    def spread(a):
        return jnp.concatenate([jnp.where(chunk_of_row == n, a, 0.0) for n in range(n_chunks)], axis=1)

    increments = _dot_tn(i, spread(k_end))
    states = []
    for n in range(n_chunks):
        states.append(st)
        decay = jnp.exp(jnp.mean(b_last[n * CHUNK:(n + 1) * CHUNK], axis=0, keepdims=True))
        st = st * decay + increments[:, n * d:(n + 1) * d]
    o = o_intra + _dot_nt(spread(q_in), jnp.concatenate(states, axis=1))
    o = o * lax.rsqrt(jnp.mean(o * o, axis=-1, keepdims=True) + NORM_EPS) * hg
    return o * (g * jax.nn.sigmoid(g)), st


def hgrn_fwd(z, lgt, hg):
    t, d4 = z.shape
    d = d4 // 4
    nh, nt = d // HEAD_DIM, t // TILE
    hp = HEADS_PER_STEP
    wide = hp * HEAD_DIM

    def body(q_ref, f_ref, i_ref, g_ref, lgt_ref, hg_ref, o_ref, st_out_ref, st_ref):
        tt = pl.program_id(1)

        @pl.when(tt == 0)
        def _():
            st_ref[...] = jnp.zeros_like(st_ref)

        for hh in range(hp):
            cols = slice(hh * HEAD_DIM, (hh + 1) * HEAD_DIM)
            st = st_ref[hh]
            st_out_ref[hh] = st
            o, st = _hgrn_tile(q_ref[:, cols], f_ref[:, cols], i_ref[:, cols], g_ref[:, cols], lgt_ref[:, cols],
                               hg_ref[...], st)
            o_ref[:, cols] = o.astype(o_ref.dtype)
            st_ref[hh] = st

    def part(p):
        return pl.BlockSpec((TILE, wide), lambda h, tt: (tt, p * (nh // hp) + h))

    return pl.pallas_call(
        body, name="hgrn_fwd", grid=(nh // hp, nt),
        in_specs=[part(0), part(1), part(2), part(3),
                  pl.BlockSpec((2, wide), lambda h, tt: (0, h)),
                  pl.BlockSpec((1, HEAD_DIM), lambda h, tt: (0, 0))],
        out_specs=[pl.BlockSpec((TILE, wide), lambda h, tt: (tt, h)),
                   pl.BlockSpec((hp, None, HEAD_DIM, HEAD_DIM), lambda h, tt: (h, tt, 0, 0))],
        out_shape=[jax.ShapeDtypeStruct((t, d), BF16),
                   jax.ShapeDtypeStruct((nh, nt, HEAD_DIM, HEAD_DIM), F32)],
        scratch_shapes=[pltpu.VMEM((hp, HEAD_DIM, HEAD_DIM), F32)],
        compiler_params=_params(("parallel", "arbitrary")),
    )(z, z, z, z, lgt, hg)


def hgrn_bwd(z, lgt, hg, states, dout):
    t, d4 = z.shape
    d = d4 // 4
    nh, nt = d // HEAD_DIM, t // TILE
    hp = HEADS_PER_STEP
    wide = hp * HEAD_DIM

    def body(q_ref, f_ref, i_ref, g_ref, lgt_ref, hg_ref, st_in_ref, do_ref, dz_ref, dlgt_ref, dhg_ref, dst_ref):
        h, tt = pl.program_id(0), pl.program_id(1)

        @pl.when(tt == 0)
        def _():
            dst_ref[...] = jnp.zeros_like(dst_ref)
            dlgt_ref[...] = jnp.zeros_like(dlgt_ref)

        @pl.when((tt == 0) & (h == 0))
        def _():
            dhg_ref[...] = jnp.zeros_like(dhg_ref)

        for hh in range(hp):
            cols = slice(hh * HEAD_DIM, (hh + 1) * HEAD_DIM)
            _, vjp = jax.vjp(_hgrn_tile, q_ref[:, cols], f_ref[:, cols], i_ref[:, cols], g_ref[:, cols],
                             lgt_ref[:, cols], hg_ref[...], st_in_ref[hh])
            grads = vjp((do_ref[:, cols], dst_ref[hh]))
            for p in range(4):
                dz_ref[p, :, cols] = grads[p].astype(dz_ref.dtype)
            dlgt_ref[:, cols] += grads[4]
            dhg_ref[...] += grads[5]
            dst_ref[hh] = grads[6]

    def part(p):
        return pl.BlockSpec((TILE, wide), lambda h, tt: (nt - 1 - tt, p * (nh // hp) + h))

    return pl.pallas_call(
        body, name="hgrn_bwd", grid=(nh // hp, nt),
        in_specs=[part(0), part(1), part(2), part(3),
                  pl.BlockSpec((2, wide), lambda h, tt: (0, h)),
                  pl.BlockSpec((1, HEAD_DIM), lambda h, tt: (0, 0)),
                  pl.BlockSpec((hp, None, HEAD_DIM, HEAD_DIM), lambda h, tt: (h, nt - 1 - tt, 0, 0)),
                  pl.BlockSpec((TILE, wide), lambda h, tt: (nt - 1 - tt, h))],
        out_specs=[pl.BlockSpec((4, TILE, wide), lambda h, tt: (0, nt - 1 - tt, h)),
                   pl.BlockSpec((2, wide), lambda h, tt: (0, h)),
                   pl.BlockSpec((1, HEAD_DIM), lambda h, tt: (0, 0))],
        out_shape=[jax.ShapeDtypeStruct((4, t, d), BF16),
                   jax.ShapeDtypeStruct((2, d), F32),
                   jax.ShapeDtypeStruct((1, HEAD_DIM), F32)],
        scratch_shapes=[pltpu.VMEM((hp, HEAD_DIM, HEAD_DIM), F32)],
        compiler_params=_params(("arbitrary", "arbitrary")),
    )(z, z, z, z, lgt, hg, states, dout)


def _log_sigmoid(x):
    return jnp.minimum(x, 0.0) - jnp.log(1.0 + jnp.exp(-jnp.abs(x)))


def decay_fwd(fl_t, b_f):
    nh, t = fl_t.shape

    def body(fl_ref, b_ref, out_ref):
        r = lax.broadcasted_iota(jnp.int32, (128, 128), 0)
        c = lax.broadcasted_iota(jnp.int32, (128, 128), 1)
        upper = (r <= c).astype(F32)
        carry = jnp.zeros((nh, 1), F32)
        for j in range(t // 128):
            cols = slice(j * 128, (j + 1) * 128)
            ls = _log_sigmoid(fl_ref[:, cols] + b_ref[...])
            out_ref[:, cols] = carry + jnp.dot(ls, upper, precision=lax.Precision.HIGHEST,
                                               preferred_element_type=F32)
            carry = carry + jnp.sum(ls, axis=1, keepdims=True)

    return pl.pallas_call(body, name="decay_fwd", out_shape=jax.ShapeDtypeStruct((nh, t), F32),
                          compiler_params=_params(None))(fl_t, b_f)


def decay_bwd(fl_t, b_f, ddcum):
    nh, t = fl_t.shape

    def body(fl_ref, b_ref, dd_ref, dfl_ref, db_ref):
        r = lax.broadcasted_iota(jnp.int32, (128, 128), 0)
        c = lax.broadcasted_iota(jnp.int32, (128, 128), 1)
        lower = (r >= c).astype(F32)
        carry = jnp.zeros((nh, 1), F32)
        db = jnp.zeros((nh, 1), F32)
        for j in reversed(range(t // 128)):
            cols = slice(j * 128, (j + 1) * 128)
            dd = dd_ref[:, cols]
            dls = carry + jnp.dot(dd, lower, precision=lax.Precision.HIGHEST, preferred_element_type=F32)
            carry = carry + jnp.sum(dd, axis=1, keepdims=True)
            dfl = dls * jax.nn.sigmoid(-(fl_ref[:, cols] + b_ref[...]))
            dfl_ref[:, cols] = dfl
            db = db + jnp.sum(dfl, axis=1, keepdims=True)
        db_ref[...] = db

    return pl.pallas_call(body, name="decay_bwd",
                          out_shape=[jax.ShapeDtypeStruct((nh, t), F32), jax.ShapeDtypeStruct((nh, 1), F32)],
                          compiler_params=_params(None))(fl_t, b_f, ddcum)


def _attn_parts(t):
    tq = _tile(t, (256, 128))
    per_part = 2 if t // tq >= 4 else 1
    return tq, [(first, per_part, (first + per_part) * tq) for first in range(0, t // tq, per_part)]


def _attn_logits(q_ref, k_ref, dcol_ref, drow_ref, row0, tq, keys):
    qs = (q_ref[...] * (HEAD_DIM ** -0.5)).astype(BF16)
    s = lax.dot_general(qs, k_ref[...], NT, preferred_element_type=F32)
    s = s + dcol_ref[...] - drow_ref[...]
    row = row0 + lax.broadcasted_iota(jnp.int32, (tq, keys), 0)
    col = lax.broadcasted_iota(jnp.int32, (tq, keys), 1)
    return qs, jnp.where(col <= row, s, NEG_BIG)


def attn_fwd(q, kv, dcol, drow):
    t, d = q.shape
    nh = d // HEAD_DIM
    tq, parts = _attn_parts(t)
    o = lse = None
    for first, count, keys in parts:
        def body(q_ref, k_ref, v_ref, dcol_ref, drow_ref, *rest, first=first, keys=keys):
            o_ref, lse_ref = rest[-2:]
            _, s = _attn_logits(q_ref, k_ref, dcol_ref, drow_ref, (first + pl.program_id(1)) * tq, tq, keys)
            m = jnp.max(s, axis=1, keepdims=True)
            p = jnp.exp(s - m)
            l = jnp.sum(p, axis=1, keepdims=True)
            acc = jnp.dot(p.astype(BF16), v_ref[...], preferred_element_type=F32)
            o_ref[...] = (acc / l).astype(o_ref.dtype)
            lse_ref[...] = m + jnp.log(l)

        tile = pl.BlockSpec((tq, HEAD_DIM), lambda h, i, first=first: (first + i, h))
        col = pl.BlockSpec((None, tq, 1), lambda h, i, first=first: (h, first + i, 0))
        seen_k = pl.BlockSpec((keys, HEAD_DIM), lambda h, i: (0, h))
        seen_v = pl.BlockSpec((keys, HEAD_DIM), lambda h, i: (0, nh + h))
        carried = [] if o is None else [o, lse]
        o, lse = pl.pallas_call(
            body, name=f"attn_fwd_{first}", grid=(nh, count),
            in_specs=[tile, seen_k, seen_v, col, pl.BlockSpec((None, 1, keys), lambda h, i: (h, 0, 0))]
            + [pl.BlockSpec(memory_space=pl.ANY)] * len(carried),
            out_specs=[tile, col],
            out_shape=[jax.ShapeDtypeStruct((t, d), BF16), jax.ShapeDtypeStruct((nh, t, 1), F32)],
            input_output_aliases={5: 0, 6: 1} if carried else {},
            compiler_params=_params(("parallel", "parallel")),
        )(q, kv, kv, dcol, drow, *carried)
    return o, lse


def attn_bwd(q, kv, dcol, drow, lse, do):
    t, d = q.shape
    nh = d // HEAD_DIM
    tq, parts = _attn_parts(t)
    dq = dk = dv = ddrow = None
    for first, count, keys in reversed(parts):
        first_call = dq is None

        def body(q_ref, k_ref, v_ref, dcol_ref, drow_ref, lse_ref, do_ref, *rest, first=first, keys=keys,
                 count=count, first_call=first_call):
            dq_ref, dk_ref, dv_ref, ddrow_ref, dk_acc, dv_acc, ddrow_acc = rest[-7:]
            i = pl.program_id(1)

            @pl.when(i == 0)
            def _():
                if first_call:
                    dk_acc[...] = jnp.zeros_like(dk_acc)
                    dv_acc[...] = jnp.zeros_like(dv_acc)
                    ddrow_acc[...] = jnp.zeros_like(ddrow_acc)
                else:
                    dk_acc[...] = rest[1][...]
                    dv_acc[...] = rest[2][...]
                    ddrow_acc[...] = rest[3][...]

            qs, s = _attn_logits(q_ref, k_ref, dcol_ref, drow_ref, (first + i) * tq, tq, keys)
            p = jnp.exp(s - lse_ref[...])
            do = do_ref[...]
            dp = lax.dot_general(do, v_ref[...], NT, preferred_element_type=F32)
            ds = p * (dp - jnp.sum(p * dp, axis=1, keepdims=True))
            dsb = ds.astype(BF16)
            dq_ref[...] = (jnp.dot(dsb, k_ref[...], preferred_element_type=F32) * (HEAD_DIM ** -0.5)).astype(dq_ref.dtype)
            dk_acc[...] += lax.dot_general(dsb, qs, TN, preferred_element_type=F32)
            dv_acc[...] += lax.dot_general(p.astype(BF16), do, TN, preferred_element_type=F32)
            ddrow_acc[...] -= jnp.sum(ds, axis=0, keepdims=True)

            @pl.when(i == count - 1)
            def _():
                dk_ref[...] = dk_acc[...]
                dv_ref[...] = dv_acc[...]
                ddrow_ref[...] = ddrow_acc[...]

        tile = pl.BlockSpec((tq, HEAD_DIM), lambda h, i, first=first: (first + i, h))
        col = pl.BlockSpec((None, tq, 1), lambda h, i, first=first: (h, first + i, 0))
        seen = pl.BlockSpec((keys, HEAD_DIM), lambda h, i: (0, h))
        seen_v = pl.BlockSpec((keys, HEAD_DIM), lambda h, i: (0, nh + h))
        seen_row = pl.BlockSpec((None, 1, keys), lambda h, i: (h, 0, 0))
        carried = [] if first_call else [dq, dk, dv, ddrow]
        carried_specs = [] if first_call else [pl.BlockSpec(memory_space=pl.ANY), seen, seen, seen_row]
        dq, dk, dv, ddrow = pl.pallas_call(
            body, name=f"attn_bwd_{first}", grid=(nh, count),
            in_specs=[tile, seen, seen_v, col, seen_row, col, tile] + carried_specs,
            out_specs=[tile, seen, seen, seen_row],
            out_shape=[jax.ShapeDtypeStruct((t, d), BF16), jax.ShapeDtypeStruct((t, d), F32),
                       jax.ShapeDtypeStruct((t, d), F32), jax.ShapeDtypeStruct((nh, 1, t), F32)],
            scratch_shapes=[pltpu.VMEM((keys, HEAD_DIM), F32), pltpu.VMEM((keys, HEAD_DIM), F32),
                            pltpu.VMEM((1, keys), F32)],
            input_output_aliases={} if first_call else {7: 0, 8: 1, 9: 2, 10: 3},
            compiler_params=_params(("parallel", "arbitrary")),
        )(q, kv, kv, dcol, drow, lse, do, *carried)
    return dq, dk, dv, ddrow


def _my_index():
    return (lax.axis_index("x") * 2 + lax.axis_index("y")) * 2 + lax.axis_index("c")


def all_gather(name, src, deps=()):
    def body(src_ref, *rest):
        out_ref, send_sems, recv_sems, local_sem = rest[len(deps):]
        x, y, c = (lax.axis_index(a) for a in MESH_AXES)
        me = (x * 2 + y) * 2 + c
        local = pltpu.make_async_copy(src_ref, out_ref.at[me], local_sem)
        local.start()
        copies = []
        for dlt in range(1, N_DEV):
            copies.append(pltpu.make_async_remote_copy(
                src_ref=src_ref, dst_ref=out_ref.at[me], send_sem=send_sems.at[dlt - 1],
                recv_sem=recv_sems.at[dlt - 1], device_id=(x ^ (dlt // 4), y ^ ((dlt // 2) % 2), c ^ (dlt % 2)),
                device_id_type=pl.DeviceIdType.MESH))
        for cp in copies:
            cp.start()
        for cp in copies:
            cp.wait_recv()
        for cp in copies:
            cp.wait_send()
        local.wait()

    return pl.pallas_call(
        body, name=name, out_shape=jax.ShapeDtypeStruct((N_DEV,) + tuple(src.shape), src.dtype),
        in_specs=[pl.BlockSpec(memory_space=pl.ANY)] * (1 + len(deps)), out_specs=pl.BlockSpec(memory_space=pl.ANY),
        scratch_shapes=[pltpu.SemaphoreType.DMA((N_DEV - 1,)), pltpu.SemaphoreType.DMA((N_DEV - 1,)),
                        pltpu.SemaphoreType.DMA],
        compiler_params=pltpu.CompilerParams(has_side_effects=True),
    )(src, *deps)


_HBM = pl.BlockSpec(memory_space=pltpu.HBM)
_SEM = pl.BlockSpec(memory_space=pltpu.SEMAPHORE)
_DATAFLOW = pltpu.SideEffectType.DATAFLOW_SIDE_EFFECTING


def _peer_copies(src_ref, land_ref, send_sems, recv_sems):
    x, y, c = (lax.axis_index(a) for a in MESH_AXES)
    me = (x * 2 + y) * 2 + c
    copies = []
    for dlt in range(1, N_DEV):
        px, py, pc = x ^ (dlt // 4), y ^ ((dlt // 2) % 2), c ^ (dlt % 2)
        peer = (px * 2 + py) * 2 + pc
        copies.append(pltpu.make_async_remote_copy(
            src_ref=src_ref.at[peer], dst_ref=land_ref.at[me],
            send_sem=send_sems.at[dlt - 1], recv_sem=recv_sems.at[dlt - 1],
            device_id=(px, py, pc), device_id_type=pl.DeviceIdType.MESH))
    return copies


def _own_copy(src_ref, land_ref, send_sems):
    me = (lax.axis_index("x") * 2 + lax.axis_index("y")) * 2 + lax.axis_index("c")
    return pltpu.make_async_copy(src_ref.at[me], land_ref.at[me], send_sems.at[N_DEV - 1])


def scatter_start(name, srcs):
    n = len(srcs)
    lands = [lax.empty(s.shape, s.dtype) for s in srcs]

    def body(*refs):
        src_refs, land_refs = refs[:n], refs[n:2 * n]
        send_sems, recv_sems = refs[2 * n:3 * n], refs[3 * n:4 * n]
        token = refs[-1]
        for j in range(n):
            for cp in _peer_copies(src_refs[j], land_refs[j], send_sems[j], recv_sems[j]):
                cp.start()
            _own_copy(src_refs[j], land_refs[j], send_sems[j]).start()
        token[...] = jnp.zeros_like(token)

    sems = [pltpu.SemaphoreType.DMA((N_DEV,))] * n + [pltpu.SemaphoreType.DMA((N_DEV - 1,))] * n
    thru = [pltpu.HBM(a.shape, a.dtype) for a in list(srcs) + lands]
    outs = pl.pallas_call(
        body, name=name, out_shape=tuple(sems + thru + [jax.ShapeDtypeStruct((8, 128), F32)]),
        in_specs=[_HBM] * (2 * n), out_specs=tuple([_SEM] * (2 * n) + [_HBM] * (2 * n) + [pl.BlockSpec(memory_space=pltpu.VMEM)]),
        input_output_aliases={j: 2 * n + j for j in range(2 * n)},
        compiler_params=pltpu.CompilerParams(has_side_effects=_DATAFLOW),
    )(*[pltpu.with_memory_space_constraint(a, pltpu.HBM) for a in list(srcs) + lands])
    handles = [(outs[j], outs[n + j], outs[2 * n + j], outs[3 * n + j]) for j in range(n)]
    return handles, outs[-1]


def scatter_wait(name, handle, after):
    send_sems, recv_sems, src, land = handle

    def body(src_ref, land_ref, send_ref, recv_ref, after_ref, src_out, land_out):
        for cp in _peer_copies(src_ref, land_ref, send_ref, recv_ref):
            cp.wait_send()
            cp.wait_recv()
        _own_copy(src_ref, land_ref, send_ref).wait()

    return pl.pallas_call(
        body, name=name, out_shape=(pltpu.HBM(src.shape, src.dtype), pltpu.HBM(land.shape, land.dtype)),
        in_specs=[_HBM, _HBM, _SEM, _SEM, pl.BlockSpec(memory_space=pl.ANY)], out_specs=(_HBM, _HBM),
        input_output_aliases={0: 0, 1: 1},
        compiler_params=pltpu.CompilerParams(has_side_effects=_DATAFLOW),
    )(src, land, send_sems, recv_sems, after)


N_OTHER_CHIPS = 3


def _two_level_places():
    x, y, c = (lax.axis_index(a) for a in MESH_AXES)
    return (x, y, c), (x * 2 + y) * 2 + c, (x, y, 1 - c), [(1 - x, y), (x, 1 - y), (1 - x, 1 - y)]


def _first_copies(land_ref, send_sems, recv_sems):
    (x, y, c), me, other_core, chips = _two_level_places()
    targets = [other_core] + [(cx, cy, c) for cx, cy in chips]
    return [pltpu.make_async_remote_copy(
        src_ref=land_ref.at[me], dst_ref=land_ref.at[me], send_sem=send_sems.at[k], recv_sem=recv_sems.at[k],
        device_id=to, device_id_type=pl.DeviceIdType.MESH) for k, to in enumerate(targets)]


def _passed_on_copies(land_ref, send_sems, recv_sems):
    (x, y, c), me, other_core, chips = _two_level_places()
    copies = []
    for k, (cx, cy) in enumerate(chips):
        slot = land_ref.at[(cx * 2 + cy) * 2 + c]
        copies.append(pltpu.make_async_remote_copy(
            src_ref=slot, dst_ref=slot, send_sem=send_sems.at[k], recv_sem=recv_sems.at[k],
            device_id=other_core, device_id_type=pl.DeviceIdType.MESH))
    return copies


def gather_start(name, lands):
    n = len(lands)

    def body(*refs):
        land_refs, send_sems, recv_sems = refs[:n], refs[n:2 * n], refs[2 * n:3 * n]
        for j in range(n):
            for cp in _first_copies(land_refs[j], send_sems[j], recv_sems[j]):
                cp.start()

    sems = [pltpu.SemaphoreType.DMA((1 + N_OTHER_CHIPS,))] * (2 * n)
    outs = pl.pallas_call(
        body, name=name, out_shape=tuple(sems + [pltpu.HBM(a.shape, a.dtype) for a in lands]),
        in_specs=[_HBM] * n, out_specs=tuple([_SEM] * (2 * n) + [_HBM] * n),
        input_output_aliases={j: 2 * n + j for j in range(n)},
        compiler_params=pltpu.CompilerParams(has_side_effects=_DATAFLOW),
    )(*[pltpu.with_memory_space_constraint(a, pltpu.HBM) for a in lands])
    return [[outs[j], outs[n + j], outs[2 * n + j]] for j in range(n)]


def gather_pass_on(name, handle, after):
    send_sems, recv_sems, land = handle

    def body(land_ref, recv_ref, after_ref, land_out, send2, recv2, token):
        arrivals = _first_copies(land_ref, recv_ref, recv_ref)
        for k, cp in enumerate(_passed_on_copies(land_ref, send2, recv2)):
            arrivals[1 + k].wait_recv()
            cp.start()
        token[...] = jnp.zeros_like(token)

    sem3 = pltpu.SemaphoreType.DMA((N_OTHER_CHIPS,))
    land, send2, recv2, token = pl.pallas_call(
        body, name=name,
        out_shape=(pltpu.HBM(land.shape, land.dtype), sem3, sem3, jax.ShapeDtypeStruct((8, 128), F32)),
        in_specs=[_HBM, _SEM, pl.BlockSpec(memory_space=pl.ANY)],
        out_specs=(_HBM, _SEM, _SEM, pl.BlockSpec(memory_space=pltpu.VMEM)),
        input_output_aliases={0: 0}, compiler_params=pltpu.CompilerParams(has_side_effects=_DATAFLOW),
    )(land, recv_sems, after)
    return [send_sems, recv_sems, land, send2, recv2], token


def gather_wait(name, handle, after):
    send_sems, recv_sems, land, send2, recv2 = handle

    def body(land_ref, send_ref, recv_ref, send2_ref, recv2_ref, after_ref, land_out):
        first = _first_copies(land_ref, send_ref, recv_ref)
        for cp in first:
            cp.wait_send()
        first[0].wait_recv()
        for cp in _passed_on_copies(land_ref, send2_ref, recv2_ref):
            cp.wait_send()
            cp.wait_recv()

    return pl.pallas_call(
        body, name=name, out_shape=pltpu.HBM(land.shape, land.dtype),
        in_specs=[_HBM, _SEM, _SEM, _SEM, _SEM, pl.BlockSpec(memory_space=pl.ANY)], out_specs=_HBM,
        input_output_aliases={0: 0}, compiler_params=pltpu.CompilerParams(has_side_effects=_DATAFLOW),
    )(land, send_sems, recv_sems, send2, recv2, after)


N_CHIPS = 4


def _pair_copies(g_ref, half_ref, send_sems, recv_sems):
    (x, y, c), me, other_core, chips = _two_level_places()
    return [pltpu.make_async_remote_copy(
        src_ref=g_ref.at[chip * 2 + (1 - c)], dst_ref=half_ref.at[chip], send_sem=send_sems.at[chip],
        recv_sem=recv_sems.at[chip], device_id=other_core, device_id_type=pl.DeviceIdType.MESH)
        for chip in range(N_CHIPS)]


def pair_start(name, g):
    half = lax.empty((N_CHIPS,) + g.shape[1:], g.dtype)

    def body(g_ref, half_ref, send_sems, recv_sems, g_out, half_out, token):
        for cp in _pair_copies(g_ref, half_ref, send_sems, recv_sems):
            cp.start()
        token[...] = jnp.zeros_like(token)

    sem = pltpu.SemaphoreType.DMA((N_CHIPS,))
    outs = pl.pallas_call(
        body, name=name,
        out_shape=(sem, sem, pltpu.HBM(g.shape, g.dtype), pltpu.HBM(half.shape, half.dtype),
                   jax.ShapeDtypeStruct((8, 128), F32)),
        in_specs=[_HBM, _HBM], out_specs=(_SEM, _SEM, _HBM, _HBM, pl.BlockSpec(memory_space=pltpu.VMEM)),
        input_output_aliases={0: 2, 1: 3}, compiler_params=pltpu.CompilerParams(has_side_effects=_DATAFLOW),
    )(pltpu.with_memory_space_constraint(g, pltpu.HBM), half)
    return list(outs[:4]), outs[4]


def pair_wait(name, handle, after):
    send_sems, recv_sems, g, half = handle

    def body(g_ref, half_ref, send_ref, recv_ref, after_ref, g_out, half_out):
        for cp in _pair_copies(g_ref, half_ref, send_ref, recv_ref):
            cp.wait_send()
            cp.wait_recv()

    return pl.pallas_call(
        body, name=name, out_shape=(pltpu.HBM(g.shape, g.dtype), pltpu.HBM(half.shape, half.dtype)),
        in_specs=[_HBM, _HBM, _SEM, _SEM, pl.BlockSpec(memory_space=pl.ANY)], out_specs=(_HBM, _HBM),
        input_output_aliases={0: 0, 1: 1}, compiler_params=pltpu.CompilerParams(has_side_effects=_DATAFLOW),
    )(g, half, send_sems, recv_sems, after)


def pair_sum(name, g, half):
    _, r, wd = g.shape
    tr = _row_tile(r, 2 * ROW_TILE_BYTES // (4 * wd))
    kind = lax.axis_index("c").astype(jnp.int32).reshape(1)

    def body(kind_ref, g_ref, half_ref, o_ref):
        o_ref[...] = (g_ref[...].astype(F32) + half_ref[...].astype(F32)).astype(o_ref.dtype)

    spec = pl.BlockSpec((None, tr, wd), lambda chip, i, kind_ref: (chip, i, 0))
    return pl.pallas_call(
        body, name=name,
        grid_spec=pltpu.PrefetchScalarGridSpec(
            num_scalar_prefetch=1, grid=(N_CHIPS, r // tr),
            in_specs=[pl.BlockSpec((None, tr, wd), lambda chip, i, kind_ref: (chip * 2 + kind_ref[0], i, 0)), spec],
            out_specs=spec),
        out_shape=jax.ShapeDtypeStruct((N_CHIPS, r, wd), g.dtype),
        compiler_params=_params(("parallel", "parallel")),
    )(kind, g, half)


def _chip_copies(sums_ref, land_ref, send_sems, recv_sems):
    (x, y, c), me, other_core, chips = _two_level_places()
    return [pltpu.make_async_remote_copy(
        src_ref=sums_ref.at[cx * 2 + cy], dst_ref=land_ref.at[x * 2 + y], send_sem=send_sems.at[k],
        recv_sem=recv_sems.at[k], device_id=(cx, cy, c), device_id_type=pl.DeviceIdType.MESH)
        for k, (cx, cy) in enumerate(chips)]


def chip_start(name, sums):
    land = lax.empty(sums.shape, sums.dtype)

    def body(sums_ref, land_ref, send_sems, recv_sems, sums_out, land_out, token):
        for cp in _chip_copies(sums_ref, land_ref, send_sems, recv_sems):
            cp.start()
        token[...] = jnp.zeros_like(token)

    sem = pltpu.SemaphoreType.DMA((N_OTHER_CHIPS,))
    outs = pl.pallas_call(
        body, name=name,
        out_shape=(sem, sem, pltpu.HBM(sums.shape, sums.dtype), pltpu.HBM(land.shape, land.dtype),
                   jax.ShapeDtypeStruct((8, 128), F32)),
        in_specs=[_HBM, _HBM], out_specs=(_SEM, _SEM, _HBM, _HBM, pl.BlockSpec(memory_space=pltpu.VMEM)),
        input_output_aliases={0: 2, 1: 3}, compiler_params=pltpu.CompilerParams(has_side_effects=_DATAFLOW),
    )(pltpu.with_memory_space_constraint(sums, pltpu.HBM), land)
    return list(outs[:4]), outs[4]


def chip_wait(name, handle, after):
    send_sems, recv_sems, sums, land = handle

    def body(sums_ref, land_ref, send_ref, recv_ref, after_ref, sums_out, land_out):
        for cp in _chip_copies(sums_ref, land_ref, send_ref, recv_ref):
            cp.wait_send()
            cp.wait_recv()

    return pl.pallas_call(
        body, name=name, out_shape=(pltpu.HBM(sums.shape, sums.dtype), pltpu.HBM(land.shape, land.dtype)),
        in_specs=[_HBM, _HBM, _SEM, _SEM, pl.BlockSpec(memory_space=pl.ANY)], out_specs=(_HBM, _HBM),
        input_output_aliases={0: 0, 1: 1}, compiler_params=pltpu.CompilerParams(has_side_effects=_DATAFLOW),
    )(sums, land, send_sems, recv_sems, after)


def adamw_reduce(name, parts, w, m, v):
    nl, r, wd = w.shape
    tr = _row_tile(r, ROW_TILE_BYTES // (8 * wd))

    def body(*refs):
        p_refs = refs[:nl]
        w_ref, m_ref, v_ref, g_ref, d_ref, nm_ref, nv_ref = refs[nl:]
        layer = pl.program_id(0)
        for j in range(nl):
            @pl.when(layer == j)
            def _(j=j):
                g = p_refs[j][0].astype(F32)
                for sender in range(1, p_refs[j].shape[0]):
                    g = g + p_refs[j][sender].astype(F32)
                nm = B1 * m_ref[...] + (1.0 - B1) * g
                nv = B2 * v_ref[...] + (1.0 - B2) * jnp.square(g)
                m_hat = nm / (1.0 - B1 ** STEP)
                v_hat = nv / (1.0 - B2 ** STEP)
                g_ref[...] = g
                d_ref[...] = -LR * (m_hat / (jnp.sqrt(v_hat) + ADAM_EPS) + WD * w_ref[...])
                nm_ref[...] = nm
                nv_ref[...] = nv

    def part_spec(j):
        return pl.BlockSpec((parts[j].shape[0], tr, wd), lambda l, i: (0, jnp.where(l == j, i, 0), 0))

    spec = pl.BlockSpec((None, tr, wd), lambda l, i: (l, i, 0))
    return pl.pallas_call(
        body, name=name, grid=(nl, r // tr),
        in_specs=[part_spec(j) for j in range(nl)] + [spec, spec, spec],
        out_specs=[spec] * 4, out_shape=[jax.ShapeDtypeStruct((nl, r, wd), F32)] * 4,
        compiler_params=_params(("arbitrary", "arbitrary")),
    )(*parts, w, m, v)


def _pack_rows(vectors, rows=None):
    flat = jnp.concatenate([a.reshape(-1).astype(F32) for a in vectors])
    n = flat.shape[0]
    if rows is None:
        rows = -(-n // 1024) * 8
    return jnp.pad(flat, (0, rows * 128 - n)).reshape(rows, 128)


def _unpack_rows(packed, like):
    flat = packed.reshape(-1)
    out, pos = [], 0
    for a in like:
        out.append(flat[pos:pos + a.size].reshape(a.shape))
        pos += a.size
    return out


def kernel(x, p, mix_norm, mlp_norm, ple_norm, w_a_in, a_lb_logits, a_head_gain, w_a_out, kv_norm, w_kvf, b_f, w_b_q, w_b_out, w_mlp_up, w_mlp_down, w_ple_gate, w_ple_up, final_norm, loss_target, m_mix_norm, m_mlp_norm, m_ple_norm, m_w_a_in, m_a_lb_logits, m_a_head_gain, m_w_a_out, m_kv_norm, m_w_kvf, m_b_f, m_w_b_q, m_w_b_out, m_w_mlp_up, m_w_mlp_down, m_w_ple_gate, m_w_ple_up, m_final_norm, v_mix_norm, v_mlp_norm, v_ple_norm, v_w_a_in, v_a_lb_logits, v_a_head_gain, v_w_a_out, v_kv_norm, v_w_kvf, v_b_f, v_w_b_q, v_w_b_out, v_w_mlp_up, v_w_mlp_down, v_w_ple_gate, v_w_ple_up, v_final_norm):
    t, d = x.shape[1], x.shape[2]
    nh = d // HEAD_DIM
    n_layers = 2
    x2 = x.reshape(t, d)
    target = loss_target.reshape(t, d)
    me = _my_index()

    shards = {"w_a_in": w_a_in[0], "w_a_out": w_a_out[0], "w_kvf": w_kvf, "w_b_q": w_b_q[0], "w_b_out": w_b_out[0]}
    for l in range(n_layers):
        shards.update({f"w_mlp_up{l}": w_mlp_up[l], f"w_mlp_down{l}": w_mlp_down[l],
                       f"w_ple_gate{l}": w_ple_gate[l], f"w_ple_up{l}": w_ple_up[l]})
    first_use = ["a_lb_logits", "w_a_in", "w_a_out", "w_mlp_up0", "w_mlp_down0", "w_ple_gate0", "w_ple_up0", "w_kvf",
                 "w_b_q", "w_b_out", "w_mlp_up1", "w_mlp_down1", "w_ple_gate1", "w_ple_up1"]
    row_sharded = ("w_a_out", "w_b_q", "w_b_out", "w_mlp_down", "w_ple_gate")
    shards_bf = [a_lb_logits] + [shards[n].astype(BF16) for n in first_use[1:]]
    ag_handles = gather_start("ag_start", [
        lax.dynamic_update_slice(lax.empty((N_DEV,) + a.shape, a.dtype), a[None], (me, 0, 0)) for a in shards_bf])
    passed_on = {}
    weights = {}

    def pass_on(j, after):
        if j < len(first_use) and j not in passed_on:
            passed_on[j] = gather_pass_on("ag_pass_" + first_use[j], ag_handles[j], after)

    def weight(name, after=None):
        if name not in weights:
            j = first_use.index(name)
            pass_on(j, after)
            pass_on(j + 1, after)
            behind = passed_on[j + 1][1] if j + 1 in passed_on else after
            g = gather_wait("ag_wait_" + name, passed_on[j][0], behind)
            if name.rstrip("01") in row_sharded:
                g = g.reshape(1, g.shape[0] * g.shape[1], g.shape[2])
            weights[name] = g
        return weights[name]

    lgt = weight("a_lb_logits", x2).transpose(1, 0, 2).reshape(2, d)
    p_bf = [p[l, 0].astype(BF16) for l in range(n_layers)]

    def row(vec):
        return vec.reshape(1, -1)

    def mlp_ple_fwd(l, h_in, a):
        (h_a, u_mlp), _ = rowwise(f"add_norm_mlp{l}", _add_norm_fwd, [h_in, a], [row(mlp_norm[l])])
        pre, act = mm_nn(f"mlp_up{l}", u_mlp, weight(f"w_mlp_up{l}", u_mlp), fuse=(_relu2, (), (BF16, BF16)))
        mo = mm_nn(f"mlp_down{l}", act, weight(f"w_mlp_down{l}", act))
        (h_b, u_ple), _ = rowwise(f"add_norm_ple{l}", _add_norm_fwd, [h_a, mo], [row(ple_norm[l])])
        gpre = mm_nn(f"ple_gate{l}", u_ple, weight(f"w_ple_gate{l}", u_ple))
        pu = mm_nn(f"ple_up{l}", p_bf[l], weight(f"w_ple_up{l}", gpre))
        return dict(h_a=h_a, u_mlp=u_mlp, pre=pre, act=act, h_b=h_b, u_ple=u_ple, gpre=gpre, pu=pu)

    (u0,), _ = rowwise("norm_mix0", _norm_fwd, [x2], [row(mix_norm[0])])
    z = mm_nn("a_in", u0, weight("w_a_in", u0))
    og, states = hgrn_fwd(z, lgt, a_head_gain)
    a0 = mm_nn("a_out", og, weight("w_a_out", og))
    s0 = mlp_ple_fwd(0, x2, a0)
    (h3, u_kv, u1), _ = rowwise("ple_norms", _ple_two_norms_fwd, [s0["h_b"], s0["gpre"], s0["pu"]],
                                [row(kv_norm), row(mix_norm[1])])
    w_kvf_cols = weight("w_kvf", u_kv).transpose(1, 0, 2).reshape(d, 2 * d + nh)
    w_kv = w_kvf_cols[:, :2 * d].reshape(d, 2, d).transpose(1, 0, 2)
    w_f = jnp.pad(w_kvf_cols[:, 2 * d:], ((0, 0), (0, HEAD_DIM - nh)))[None]
    kv = mm_nn("kvf", u_kv, w_kv, out_dtype=BF16)
    fl_t = mm_nn("kvf_forget", u_kv, w_f)[:, :nh].T
    b_f_col = b_f.reshape(nh, 1)
    dcum = decay_fwd(fl_t, b_f_col)
    dcol, drow = dcum.reshape(nh, t, 1), dcum.reshape(nh, 1, t)
    q = mm_nn("b_q", u1, weight("w_b_q", dcum))
    o, lse = attn_fwd(q, kv, dcol, drow)
    a1 = mm_nn("b_out", o, weight("w_b_out", o))
    s1 = mlp_ple_fwd(1, h3, a1)

    (dh, dgpre, dpu), (d_final, loss_rows) = rowwise(
        "tail", _tail_fwd_bwd, [s1["h_b"], s1["gpre"], s1["pu"], target], [row(final_norm)])

    sent = {}
    tokens = []

    two_level = ("w_mlp_up0", "w_a_in")
    swapping = []

    def send_grad(name, g):
        g = g.reshape(N_DEV, -1, g.shape[-1])
        if name in two_level:
            sent[name], token = pair_start("rs_pair_" + name, g)
            swapping.append(name)
        else:
            (sent[name],), token = scatter_start("rs_start_" + name, [g])
        tokens.append(token)

    def send_grads_together(grads):
        names = list(grads)
        handles, token = scatter_start("rs_start_" + names[0], [
            grads[n].reshape(N_DEV, -1, grads[n].shape[-1]) for n in names])
        sent.update(zip(names, handles))
        tokens.append(token)

    def second_stage(after):
        for name in swapping:
            g, half = pair_wait("rs_pairwait_" + name, sent[name], after)
            sent[name], token = chip_start("rs_chip_" + name, pair_sum("rs_sum_" + name, g, half))
            tokens.append(token)
        swapping.clear()

    def after_sends():
        deps = tuple(tokens)
        tokens.clear()
        return deps

    def mlp_ple_bwd(l, s, dh, dgpre, dpu):
        send_grads_together({
            f"w_ple_gate{l}": mm_tn(f"d_ple_gate_w{l}", s["u_ple"], dgpre, 1, deps=after_sends()),
            f"w_ple_up{l}": mm_tn(f"d_ple_up_w{l}", p_bf[l], dpu, N_DEV)})
        du = mm_nt(f"d_ple_gate_x{l}", dgpre, weight(f"w_ple_gate{l}"), out_dtype=BF16, deps=after_sends())
        (dh, dh_bf), (d_ple,) = rowwise(f"d_norm_ple{l}", _norm_bwd, [s["h_b"], du, dh], [row(ple_norm[l])])
        send_grad(f"w_mlp_down{l}", mm_tn(f"d_mlp_down_w{l}", s["act"], dh_bf, 1))
        (dpre,) = mm_nt(f"d_mlp_down_x{l}", dh_bf, weight(f"w_mlp_down{l}"), deps=after_sends(),
                        fuse=(_relu2_bwd, (s["pre"],), (BF16,)))
        second_stage(dpre)
        send_grad(f"w_mlp_up{l}", mm_tn(f"d_mlp_up_w{l}", s["u_mlp"], dpre, N_DEV))
        du = mm_nt(f"d_mlp_up_x{l}", dpre, weight(f"w_mlp_up{l}"), out_dtype=BF16, deps=after_sends())
        second_stage(du)
        (dh, dh_bf), (d_mlp,) = rowwise(f"d_norm_mlp{l}", _norm_bwd, [s["h_a"], du, dh], [row(mlp_norm[l])])
        return dh, dh_bf, d_ple, d_mlp

    dh, dh_bf, d_ple1, d_mlp1 = mlp_ple_bwd(1, s1, dh, dgpre, dpu)
    send_grad("w_b_out", mm_tn("d_b_out_w", o, dh_bf, 1))
    do = mm_nt("d_b_out_x", dh_bf, weight("w_b_out"), out_dtype=BF16, deps=after_sends())
    dq, dk, dv, ddrow = attn_bwd(q, kv, dcol, drow, lse, do)
    send_grad("w_b_q", mm_tn("d_b_q_w", u1, dq, 1))
    du1 = mm_nt("d_b_q_x", dq, weight("w_b_q"), out_dtype=BF16, deps=after_sends())
    dfl_t, d_b_f = decay_bwd(fl_t, b_f_col, ddrow.reshape(nh, t))
    dkv = jnp.stack([dk, dv]).astype(BF16)
    dfl = jnp.pad(dfl_t.T, ((0, 0), (0, HEAD_DIM - nh))).astype(BF16)
    d_w_kv = mm_tn("d_kvf_w", u_kv, dkv, 2)
    d_w_f = mm_tn("d_kvf_forget_w", u_kv, dfl, 1)
    d_w_kvf = jnp.concatenate([d_w_kv[0], d_w_kv[1], d_w_f[0, :, :nh]], axis=1)
    send_grad("w_kvf", d_w_kvf.reshape(d, N_DEV, -1).transpose(1, 0, 2))
    du_f = mm_nt("d_kvf_forget_x", dfl, w_f, deps=after_sends())
    (du_kv,) = mm_nt("d_kvf_x", dkv, w_kv, fuse=(lambda acc, extra: (acc + extra,), (du_f,), (BF16,)))
    second_stage(du_kv)
    (dh, dgpre, dpu), (d_kv_norm, d_mix1) = rowwise(
        "d_ple_norms", _two_norms_ple_bwd, [h3, du_kv, du1, dh, s0["gpre"], s0["pu"]],
        [row(kv_norm), row(mix_norm[1])])
    dh, dh_bf, d_ple0, d_mlp0 = mlp_ple_bwd(0, s0, dh, dgpre, dpu)
    send_grad("w_a_out", mm_tn("d_a_out_w", og, dh_bf, 1))
    dog = mm_nt("d_a_out_x", dh_bf, weight("w_a_out"), deps=after_sends())
    dz4, d_lgt, d_hg = hgrn_bwd(z, lgt, a_head_gain, states, dog)
    send_grad("w_a_in", mm_tn("d_a_in_w", u0, dz4, N_DEV, stacked=True))
    du0 = mm_nt("d_a_in_x", dz4, weight("w_a_in"), out_dtype=BF16, deps=after_sends(), stacked=True)
    second_stage(du0)
    (dx, _), (d_mix0,) = rowwise("d_norm_mix0", _norm_bwd, [x2, du0, dh], [row(mix_norm[0])])

    new = {}
    last = [dx]

    def update(name, parts, w, m, v):
        shp = w.shape
        w3, m3, v3 = (a.reshape(len(parts), -1, shp[-1]) for a in (w, m, v))
        new[name] = tuple(a.reshape(shp) for a in adamw_reduce("adamw_" + name, parts, w3, m3, v3))
        last[0] = new[name][0]

    def receive_update(name, layers, w, m, v):
        parts = {}
        for sfx in layers:
            if name + sfx in two_level:
                mine, land = chip_wait(f"rs_wait_{name}{sfx}", sent[name + sfx], last[0])
                chip = me // 2
                parts[sfx] = lax.dynamic_update_slice(land, lax.dynamic_slice_in_dim(mine, chip, 1, 0), (chip, 0, 0))
            else:
                _, parts[sfx] = scatter_wait(f"rs_wait_{name}{sfx}", sent[name + sfx], last[0])
        update(name, [parts[sfx] for sfx in sorted(layers)], w, m, v)

    both = ("1", "0")
    receive_update("w_b_out", ("",), w_b_out, m_w_b_out, v_w_b_out)
    receive_update("w_b_q", ("",), w_b_q, m_w_b_q, v_w_b_q)
    receive_update("w_kvf", ("",), w_kvf, m_w_kvf, v_w_kvf)
    receive_update("w_ple_gate", both, w_ple_gate, m_w_ple_gate, v_w_ple_gate)
    receive_update("w_ple_up", both, w_ple_up, m_w_ple_up, v_w_ple_up)
    receive_update("w_mlp_down", both, w_mlp_down, m_w_mlp_down, v_w_mlp_down)
    receive_update("w_mlp_up", both, w_mlp_up, m_w_mlp_up, v_w_mlp_up)
    receive_update("w_a_out", ("",), w_a_out, m_w_a_out, v_w_a_out)
    receive_update("w_a_in", ("",), w_a_in, m_w_a_in, v_w_a_in)

    small = dict(mix_norm=jnp.concatenate([d_mix0, d_mix1]), mlp_norm=jnp.concatenate([d_mlp0, d_mlp1]),
                 ple_norm=jnp.concatenate([d_ple0, d_ple1]), a_head_gain=d_hg, kv_norm=d_kv_norm.reshape(d),
                 b_f=d_b_f.reshape(nh), final_norm=d_final.reshape(d))
    small_w = dict(mix_norm=(mix_norm, m_mix_norm, v_mix_norm), mlp_norm=(mlp_norm, m_mlp_norm, v_mlp_norm),
                   ple_norm=(ple_norm, m_ple_norm, v_ple_norm),
                   a_head_gain=(a_head_gain, m_a_head_gain, v_a_head_gain), kv_norm=(kv_norm, m_kv_norm, v_kv_norm),
                   b_f=(b_f, m_b_f, v_b_f), final_norm=(final_norm, m_final_norm, v_final_norm))
    names = list(small)
    packed = _pack_rows([d_lgt] + [small[n] for n in names])
    everyone = all_gather("ag_small_grads", packed, deps=(last[0],))
    n_lgt_rows = d_lgt.size // 128
    lgt_parts = everyone[:, :n_lgt_rows].reshape(N_DEV, 2, d)
    lgt_parts = lax.dynamic_slice_in_dim(lgt_parts, me * a_lb_logits.shape[1], a_lb_logits.shape[1], axis=2)
    update("a_lb_logits", [lgt_parts], a_lb_logits, m_a_lb_logits, v_a_lb_logits)
    rest = everyone[:, n_lgt_rows:]
    like = [small_w[n][0] for n in names]
    packed_w, packed_m, packed_v = (_pack_rows([small_w[n][j] for n in names], rest.shape[1])[None] for j in range(3))
    outs = adamw_reduce("adamw_small", [rest], packed_w, packed_m, packed_v)
    unpacked = [_unpack_rows(a, like) for a in outs]
    for j, n in enumerate(names):
        new[n] = tuple(unpacked[q][j] for q in range(4))

    order = ["mix_norm", "mlp_norm", "ple_norm", "w_a_in", "a_lb_logits", "a_head_gain", "w_a_out", "kv_norm",
             "w_kvf", "b_f", "w_b_q", "w_b_out", "w_mlp_up", "w_mlp_down", "w_ple_gate", "w_ple_up", "final_norm"]
    loss_here, _ = lax.optimization_barrier((loss_rows[0, 0], new["final_norm"][0]))
    loss = lax.psum(loss_here, MESH_AXES)
    result = [loss, dx.reshape(x.shape)]
    for j in range(4):
        result += [new[n][j] for n in order]
    return tuple(result)
```

```python
import jax
import jax.numpy as jnp
from jax import lax
from jax.experimental import pallas as pl
from jax.experimental.pallas import tpu as pltpu

F32 = jnp.float32
BF16 = jnp.bfloat16
HEAD_DIM = 128
CHUNK = 16
TILE = 128
HEADS_PER_STEP = 4
NORM_EPS = 1e-6
N_DEV = 8
MESH_AXES = ("x", "y", "c")
VMEM_LIMIT_BYTES = 48 * 1024 * 1024
ROW_TILE_BYTES = 2 * 1024 * 1024
LR, B1, B2, ADAM_EPS, WD, STEP = 0.001, 0.9, 0.999, 1e-08, 0.01, 10
NEG_BIG = -1e30

NN = (((1,), (0,)), ((), ()))
NT = (((1,), (1,)), ((), ()))
TN = (((0,), (0,)), ((), ()))


def _params(semantics):
    return pltpu.CompilerParams(dimension_semantics=semantics, vmem_limit_bytes=VMEM_LIMIT_BYTES)


def _tile(n, prefs):
    for p in prefs:
        if n % p == 0:
            return p
    return n


def _row_tile(rows, limit):
    for cand in (2048, 1024, 512, 256, 128, 64, 32, 16):
        if cand <= limit and rows % cand == 0:
            return cand
    return rows


def _mm_call(name, a, b, dims, grid, a_spec, b_spec, o_spec, o_shape, acc_shape, k_axes, out_dtype, deps=(),
             fuse=None, split=1):
    nk = 1
    for ax in k_axes:
        nk *= grid[ax]
    fn, extra, out_dtypes = fuse if fuse else (lambda acc: (acc,), (), (out_dtype,))
    n_extra, n_out = len(extra), len(out_dtypes)

    def finish(acc, rest):
        o_refs = rest[n_extra + len(deps):n_extra + len(deps) + n_out]
        for ref, val in zip(o_refs, fn(acc, *[r[...] for r in rest[:n_extra]])):
            ref[...] = val.astype(ref.dtype)

    def product(a_ref, b_ref):
        if split == 1:
            return lax.dot_general(a_ref[...], b_ref[...], dims, preferred_element_type=F32)
        wide = a_ref.shape[1] // split
        return sum(lax.dot_general(a_ref[:, q * wide:(q + 1) * wide], b_ref[q], dims, preferred_element_type=F32)
                   for q in range(split))

    def one_step(a_ref, b_ref, *rest):
        finish(product(a_ref, b_ref), rest)

    def accumulate(a_ref, b_ref, *rest):
        acc_ref = rest[-1]
        k = 0
        for ax in k_axes:
            k = k * grid[ax] + pl.program_id(ax)
        part = product(a_ref, b_ref)

        @pl.when(k == 0)
        def _():
            acc_ref[...] = part

        @pl.when((k > 0) & (k < nk - 1))
        def _():
            acc_ref[...] += part

        @pl.when(k == nk - 1)
        def _():
            finish(acc_ref[...] + part, rest)

    sem = tuple("arbitrary" if ax in k_axes else "parallel" for ax in range(len(grid)))
    outs = pl.pallas_call(
        one_step if nk == 1 else accumulate, name=name, grid=grid,
        in_specs=[a_spec, b_spec] + [o_spec] * n_extra + [pl.BlockSpec(memory_space=pl.ANY)] * len(deps),
        out_specs=[o_spec] * n_out, out_shape=[jax.ShapeDtypeStruct(o_shape, dt) for dt in out_dtypes],
        scratch_shapes=[] if nk == 1 else [pltpu.VMEM(acc_shape, F32)], compiler_params=_params(sem),
    )(a, b, *extra, *deps)
    return outs if fuse else outs[0]


def mm_nn(name, a, b3, out_dtype=F32, out3=False, deps=(), fuse=None):
    m, k = a.shape
    g, _, n = b3.shape
    tm, tk = _tile(m, (1024, 512, 256)), _tile(k, (2048, 1024, 512, 256))
    tn = n if out3 else _tile(n, (1024, 512, 256, 128))
    nj = n // tn
    grid = (m // tm, g, nj, k // tk)
    a_spec = pl.BlockSpec((tm, tk), lambda i, gg, j, kk: (i, kk))
    b_spec = pl.BlockSpec((None, tk, tn), lambda i, gg, j, kk: (gg, kk, j))
    if out3:
        o_spec = pl.BlockSpec((None, tm, tn), lambda i, gg, j, kk: (gg, i, j))
        o_shape = (g, m, n)
    else:
        o_spec = pl.BlockSpec((tm, tn), lambda i, gg, j, kk: (i, gg * nj + j))
        o_shape = (m, g * n)
    return _mm_call(name, a, b3, NN, grid, a_spec, b_spec, o_spec, o_shape, (tm, tn), (3,), out_dtype, deps, fuse)


def mm_nt(name, a, b3, out_dtype=F32, deps=(), fuse=None, stacked=False):
    g, k, n = b3.shape
    a3 = a.ndim == 3 and not stacked
    m = a.shape[1] if a.ndim == 3 else a.shape[0]
    tm, tko = _tile(m, (1024, 512, 256)), _tile(k, (1024, 512, 256))
    tc = n if a3 else _tile(n, (2048, 1024, 512, 256, 128))
    nc = n // tc
    per = g // a.shape[0] if stacked else g
    pair = 2 if (not a3 and nc == 1 and per % 2 == 0 and tc <= 1024) else 1
    grid = (m // tm, k // tko, g // pair, nc)
    if a3:
        a_spec = pl.BlockSpec((None, tm, tc), lambda i, j, gg, c: (gg, i, c))
    elif stacked:
        a_spec = pl.BlockSpec((None, tm, pair * tc),
                              lambda i, j, gg, c: ((gg * pair) // per, i, (((gg * pair) % per) // pair) * nc + c))
    else:
        a_spec = pl.BlockSpec((tm, pair * tc), lambda i, j, gg, c: (i, gg * nc + c))
    if pair == 1:
        b_spec = pl.BlockSpec((None, tko, tc), lambda i, j, gg, c: (gg, j, c))
    else:
        b_spec = pl.BlockSpec((pair, tko, tc), lambda i, j, gg, c: (gg, j, c))
    o_spec = pl.BlockSpec((tm, tko), lambda i, j, gg, c: (i, j))
    return _mm_call(name, a, b3, NT, grid, a_spec, b_spec, o_spec, (m, k), (tm, tko), (2, 3), out_dtype, deps, fuse,
                    pair)


def mm_tn(name, a, b, g, out_dtype=BF16, deps=(), stacked=False):
    t, k = a.shape
    b3 = b.ndim == 3 and not stacked
    n = b.shape[2] if b3 else (b.shape[0] * b.shape[2] if stacked else b.shape[1]) // g
    tm = _tile(k, (1024, 512, 256))
    tn = n if b3 else _tile(n, (1024, 512, 256, 128))
    tt = _tile(t, (2048, 1024, 512, 256))
    nj = n // tn
    grid = (g, k // tm, nj, t // tt)
    a_spec = pl.BlockSpec((tt, tm), lambda gg, i, j, s: (s, i))
    if b3:
        b_spec = pl.BlockSpec((None, tt, tn), lambda gg, i, j, s: (gg, s, j))
    elif stacked:
        per = g // b.shape[0]
        b_spec = pl.BlockSpec((None, tt, tn), lambda gg, i, j, s: (gg // per, s, (gg % per) * nj + j))
    else:
        b_spec = pl.BlockSpec((tt, tn), lambda gg, i, j, s: (s, gg * nj + j))
    o_spec = pl.BlockSpec((None, tm, tn), lambda gg, i, j, s: (gg, i, j))
    return _mm_call(name, a, b, TN, grid, a_spec, b_spec, o_spec, (g, k, n), (tm, tn), (3,), out_dtype, deps)


def rowwise(name, fn, rows, vecs=()):
    t = rows[0].shape[0]
    wmax = max(r.shape[1] for r in rows)
    tr = _row_tile(t, ROW_TILE_BYTES // (4 * wmax))
    row_s = [jax.ShapeDtypeStruct((tr, r.shape[1]), r.dtype) for r in rows]
    vec_s = [jax.ShapeDtypeStruct(v.shape, v.dtype) for v in vecs]
    out_rows_s, out_sums_s = jax.eval_shape(fn, *row_s, *vec_s)
    n_in, n_r = len(rows) + len(vecs), len(out_rows_s)

    def body(*refs):
        i = pl.program_id(0)
        o_rows, o_sums = fn(*[r[...] for r in refs[:n_in]])
        for ref, val in zip(refs[n_in:n_in + n_r], o_rows):
            ref[...] = val

        if out_sums_s:
            @pl.when(i == 0)
            def _():
                for ref in refs[n_in + n_r:]:
                    ref[...] = jnp.zeros_like(ref)

            for ref, val in zip(refs[n_in + n_r:], o_sums):
                ref[...] += val

    in_specs = [pl.BlockSpec((tr, r.shape[1]), lambda i: (i, 0)) for r in rows]
    in_specs += [pl.BlockSpec(v.shape, lambda i: (0, 0)) for v in vecs]
    out_specs = [pl.BlockSpec((tr, s.shape[1]), lambda i: (i, 0)) for s in out_rows_s]
    out_specs += [pl.BlockSpec(s.shape, lambda i: (0, 0)) for s in out_sums_s]
    out_shape = [jax.ShapeDtypeStruct((t, s.shape[1]), s.dtype) for s in out_rows_s]
    out_shape += [jax.ShapeDtypeStruct(s.shape, s.dtype) for s in out_sums_s]
    outs = pl.pallas_call(
        body, name=name, grid=(t // tr,), in_specs=in_specs, out_specs=out_specs, out_shape=out_shape,
        compiler_params=_params(("arbitrary",)),
    )(*rows, *vecs)
    return outs[:n_r], outs[n_r:]


def _rms(x, gain):
    return x * lax.rsqrt(jnp.mean(x * x, axis=-1, keepdims=True) + NORM_EPS) * gain


def _norm_fwd(x, gain):
    return (_rms(x, gain).astype(BF16),), ()


def _add_norm_fwd(h, a, gain):
    h = h + a
    return (h, _rms(h, gain).astype(BF16)), ()


def _relu2(pre):
    r = jnp.maximum(pre, 0.0)
    return pre, r * r


def _ple(h, gpre, pu):
    return h + pu * jax.nn.sigmoid(gpre)


def _ple_two_norms_fwd(h, gpre, pu, gain_a, gain_b):
    h = _ple(h, gpre, pu)
    return (h, _rms(h, gain_a).astype(BF16), _rms(h, gain_b).astype(BF16)), ()


def _tail_fwd_bwd(h, gpre, pu, target, gain):
    def row_loss(h, gpre, pu, gain):
        y = _rms(_ple(h, gpre, pu), gain)
        return 0.5 * jnp.mean(jnp.square(y - target), axis=-1, keepdims=True)

    loss, vjp = jax.vjp(row_loss, h, gpre, pu, gain)
    dh, dgpre, dpu, dgain = vjp(jnp.ones_like(loss))
    loss = jnp.broadcast_to(jnp.sum(loss, axis=0, keepdims=True), (1, 128))
    return (dh, dgpre.astype(BF16), dpu.astype(BF16)), (dgain, loss)


def _norm_bwd(h, du, dh_in, gain):
    _, vjp = jax.vjp(_rms, h, gain)
    dh, dgain = vjp(du.astype(F32))
    dh = dh_in + dh
    return (dh, dh.astype(BF16)), (dgain,)


def _two_norms_ple_bwd(h, du_a, du_b, dh_in, gpre, pu, gain_a, gain_b):
    _, vjp = jax.vjp(lambda h, ga, gb: (_rms(h, ga), _rms(h, gb)), h, gain_a, gain_b)
    dh, dga, dgb = vjp((du_a.astype(F32), du_b.astype(F32)))
    dh = dh_in + dh
    _, gate_vjp = jax.vjp(lambda g, u: u * jax.nn.sigmoid(g), gpre, pu)
    dgpre, dpu = gate_vjp(dh)
    return (dh, dgpre.astype(BF16), dpu.astype(BF16)), (dga, dgb)


def _relu2_bwd(dact, pre):
    return (dact * 2.0 * jnp.maximum(pre.astype(F32), 0.0),)


def _bf16_dot(dims_fwd, dims_da, dims_db, swap_da, swap_db):
    @jax.custom_vjp
    def dot(a, b):
        return lax.dot_general(a.astype(BF16), b.astype(BF16), dims_fwd, preferred_element_type=F32)

    def fwd(a, b):
        return dot(a, b), (a, b)

    def bwd(res, ct):
        a, b = res
        ct, a, b = ct.astype(BF16), a.astype(BF16), b.astype(BF16)
        da = lax.dot_general(*((b, ct) if swap_da else (ct, b)), dims_da, preferred_element_type=F32)
        db = lax.dot_general(*((ct, a) if swap_db else (a, ct)), dims_db, preferred_element_type=F32)
        return da, db

    dot.defvjp(fwd, bwd)
    return dot


_dot_nn = _bf16_dot(NN, NT, TN, False, False)
_dot_nt = _bf16_dot(NT, NN, TN, False, True)
_dot_tn = _bf16_dot(TN, NT, NN, True, False)


def _chunk_causal_mask():
    r = lax.broadcasted_iota(jnp.int32, (TILE, TILE), 0)
    c = lax.broadcasted_iota(jnp.int32, (TILE, TILE), 1)
    return ((r // CHUNK) == (c // CHUNK)) & (c <= r)


def _chunk_scan(x, reverse):
    pos = lax.broadcasted_iota(jnp.int32, x.shape, 0) % CHUNK
    step = 1
    while step < CHUNK:
        if reverse:
            x = x + jnp.where(pos < CHUNK - step, pltpu.roll(x, x.shape[0] - step, axis=0), 0.0)
        else:
            x = x + jnp.where(pos >= step, pltpu.roll(x, step, axis=0), 0.0)
        step *= 2
    return x


def _chunk_total(x):
    return _chunk_scan(x, False) + _chunk_scan(x, True) - x


@jax.custom_vjp
def _chunk_sums(x):
    return _chunk_scan(x, False), _chunk_total(x)


def _chunk_sums_fwd(x):
    return _chunk_sums(x), None


def _chunk_sums_bwd(_, ct):
    return (_chunk_scan(ct[0], True) + _chunk_total(ct[1]),)


_chunk_sums.defvjp(_chunk_sums_fwd, _chunk_sums_bwd)


def _hgrn_tile(q, f, i, g, lgt, hg, st):
    d = q.shape[1]
    l0, l1 = lgt[0:1], lgt[1:2]
    mx = jnp.maximum(l0, l1)
    e0, e1 = jnp.exp(l0 - mx), jnp.exp(l1 - mx)
    lb = e0 / (e0 + e1)
    fg = lb + (1.0 - lb) * jax.nn.sigmoid(f)
    k = 1.0 - fg
    causal = _chunk_causal_mask()
    b, b_last = _chunk_sums(jnp.log(fg))
    q_in = q * jax.nn.sigmoid(q) * (d ** -0.5) * jnp.exp(b)
    k_in = k * jnp.exp(-b)
    k_end = k * jnp.exp(b_last - b)
    att = jnp.where(causal, _dot_nt(q_in, k_in), 0.0)
    o_intra = _dot_nn(att, i)
    n_chunks = TILE // CHUNK
    chunk_of_row = lax.broadcasted_iota(jnp.int32, (TILE, 1), 0) // CHUNK

    def spread(a):
        return jnp.concatenate([jnp.where(chunk_of_row == n, a, 0.0) for n in range(n_chunks)], axis=1)

    increments = _dot_tn(i, spread(k_end))
    states = []
    for n in range(n_chunks):
        states.append(st)
        decay = jnp.exp(jnp.mean(b_last[n * CHUNK:(n + 1) * CHUNK], axis=0, keepdims=True))
        st = st * decay + increments[:, n * d:(n + 1) * d]
    o = o_intra + _dot_nt(spread(q_in), jnp.concatenate(states, axis=1))
    o = o * lax.rsqrt(jnp.mean(o * o, axis=-1, keepdims=True) + NORM_EPS) * hg
    return o * (g * jax.nn.sigmoid(g)), st


def hgrn_fwd(z, lgt, hg):
    t, d4 = z.shape
    d = d4 // 4
    nh, nt = d // HEAD_DIM, t // TILE
    hp = HEADS_PER_STEP
    wide = hp * HEAD_DIM

    def body(q_ref, f_ref, i_ref, g_ref, lgt_ref, hg_ref, o_ref, st_out_ref, st_ref):
        tt = pl.program_id(1)

        @pl.when(tt == 0)
        def _():
            st_ref[...] = jnp.zeros_like(st_ref)

        for hh in range(hp):
            cols = slice(hh * HEAD_DIM, (hh + 1) * HEAD_DIM)
            st = st_ref[hh]
            st_out_ref[hh] = st
            o, st = _hgrn_tile(q_ref[:, cols], f_ref[:, cols], i_ref[:, cols], g_ref[:, cols], lgt_ref[:, cols],
                               hg_ref[...], st)
            o_ref[:, cols] = o.astype(o_ref.dtype)
            st_ref[hh] = st

    def part(p):
        return pl.BlockSpec((TILE, wide), lambda h, tt: (tt, p * (nh // hp) + h))

    return pl.pallas_call(
        body, name="hgrn_fwd", grid=(nh // hp, nt),
        in_specs=[part(0), part(1), part(2), part(3),
                  pl.BlockSpec((2, wide), lambda h, tt: (0, h)),
                  pl.BlockSpec((1, HEAD_DIM), lambda h, tt: (0, 0))],
        out_specs=[pl.BlockSpec((TILE, wide), lambda h, tt: (tt, h)),
                   pl.BlockSpec((hp, None, HEAD_DIM, HEAD_DIM), lambda h, tt: (h, tt, 0, 0))],
        out_shape=[jax.ShapeDtypeStruct((t, d), BF16),
                   jax.ShapeDtypeStruct((nh, nt, HEAD_DIM, HEAD_DIM), F32)],
        scratch_shapes=[pltpu.VMEM((hp, HEAD_DIM, HEAD_DIM), F32)],
        compiler_params=_params(("parallel", "arbitrary")),
    )(z, z, z, z, lgt, hg)


def hgrn_bwd(z, lgt, hg, states, dout):
    t, d4 = z.shape
    d = d4 // 4
    nh, nt = d // HEAD_DIM, t // TILE
    hp = HEADS_PER_STEP
    wide = hp * HEAD_DIM

    def body(q_ref, f_ref, i_ref, g_ref, lgt_ref, hg_ref, st_in_ref, do_ref, dz_ref, dlgt_ref, dhg_ref, dst_ref):
        h, tt = pl.program_id(0), pl.program_id(1)

        @pl.when(tt == 0)
        def _():
            dst_ref[...] = jnp.zeros_like(dst_ref)
            dlgt_ref[...] = jnp.zeros_like(dlgt_ref)

        @pl.when((tt == 0) & (h == 0))
        def _():
            dhg_ref[...] = jnp.zeros_like(dhg_ref)

        for hh in range(hp):
            cols = slice(hh * HEAD_DIM, (hh + 1) * HEAD_DIM)
            _, vjp = jax.vjp(_hgrn_tile, q_ref[:, cols], f_ref[:, cols], i_ref[:, cols], g_ref[:, cols],
                             lgt_ref[:, cols], hg_ref[...], st_in_ref[hh])
            grads = vjp((do_ref[:, cols].astype(F32), dst_ref[hh]))
            for p in range(4):
                dz_ref[p, :, cols] = grads[p].astype(dz_ref.dtype)
            dlgt_ref[:, cols] += grads[4]
            dhg_ref[...] += grads[5]
            dst_ref[hh] = grads[6]

    def part(p):
        return pl.BlockSpec((TILE, wide), lambda h, tt: (nt - 1 - tt, p * (nh // hp) + h))

    return pl.pallas_call(
        body, name="hgrn_bwd", grid=(nh // hp, nt),
        in_specs=[part(0), part(1), part(2), part(3),
                  pl.BlockSpec((2, wide), lambda h, tt: (0, h)),
                  pl.BlockSpec((1, HEAD_DIM), lambda h, tt: (0, 0)),
                  pl.BlockSpec((hp, None, HEAD_DIM, HEAD_DIM), lambda h, tt: (h, nt - 1 - tt, 0, 0)),
                  pl.BlockSpec((TILE, wide), lambda h, tt: (nt - 1 - tt, h))],
        out_specs=[pl.BlockSpec((4, TILE, wide), lambda h, tt: (0, nt - 1 - tt, h)),
                   pl.BlockSpec((2, wide), lambda h, tt: (0, h)),
                   pl.BlockSpec((1, HEAD_DIM), lambda h, tt: (0, 0))],
        out_shape=[jax.ShapeDtypeStruct((4, t, d), BF16),
                   jax.ShapeDtypeStruct((2, d), F32),
                   jax.ShapeDtypeStruct((1, HEAD_DIM), F32)],
        scratch_shapes=[pltpu.VMEM((hp, HEAD_DIM, HEAD_DIM), F32)],
        compiler_params=_params(("arbitrary", "arbitrary")),
    )(z, z, z, z, lgt, hg, states, dout)


def _log_sigmoid(x):
    return jnp.minimum(x, 0.0) - jnp.log(1.0 + jnp.exp(-jnp.abs(x)))


def decay_fwd(fl_t, b_f):
    nh, t = fl_t.shape

    def body(fl_ref, b_ref, out_ref):
        r = lax.broadcasted_iota(jnp.int32, (128, 128), 0)
        c = lax.broadcasted_iota(jnp.int32, (128, 128), 1)
        upper = (r <= c).astype(F32)
        carry = jnp.zeros((nh, 1), F32)
        for j in range(t // 128):
            cols = slice(j * 128, (j + 1) * 128)
            ls = _log_sigmoid(fl_ref[:, cols] + b_ref[...])
            out_ref[:, cols] = carry + jnp.dot(ls, upper, precision=lax.Precision.HIGHEST,
                                               preferred_element_type=F32)
            carry = carry + jnp.sum(ls, axis=1, keepdims=True)

    return pl.pallas_call(body, name="decay_fwd", out_shape=jax.ShapeDtypeStruct((nh, t), F32),
                          compiler_params=_params(None))(fl_t, b_f)


def decay_bwd(fl_t, b_f, ddcum):
    nh, t = fl_t.shape

    def body(fl_ref, b_ref, dd_ref, dfl_ref, db_ref):
        r = lax.broadcasted_iota(jnp.int32, (128, 128), 0)
        c = lax.broadcasted_iota(jnp.int32, (128, 128), 1)
        lower = (r >= c).astype(F32)
        carry = jnp.zeros((nh, 1), F32)
        db = jnp.zeros((nh, 1), F32)
        for j in reversed(range(t // 128)):
            cols = slice(j * 128, (j + 1) * 128)
            dd = dd_ref[:, cols]
            dls = carry + jnp.dot(dd, lower, precision=lax.Precision.HIGHEST, preferred_element_type=F32)
            carry = carry + jnp.sum(dd, axis=1, keepdims=True)
            dfl = dls * jax.nn.sigmoid(-(fl_ref[:, cols] + b_ref[...]))
            dfl_ref[:, cols] = dfl
            db = db + jnp.sum(dfl, axis=1, keepdims=True)
        db_ref[...] = db

    return pl.pallas_call(body, name="decay_bwd",
                          out_shape=[jax.ShapeDtypeStruct((nh, t), F32), jax.ShapeDtypeStruct((nh, 1), F32)],
                          compiler_params=_params(None))(fl_t, b_f, ddcum)


def _attn_parts(t):
    tq = _tile(t, (256, 128))
    per_part = 2 if t // tq >= 4 else 1
    return tq, [(first, per_part, (first + per_part) * tq) for first in range(0, t // tq, per_part)]


def _attn_logits(q_ref, k_ref, dcol_ref, drow_ref, row0, tq, keys):
    qs = (q_ref[...] * (HEAD_DIM ** -0.5)).astype(BF16)
    s = lax.dot_general(qs, k_ref[...], NT, preferred_element_type=F32)
    s = s + dcol_ref[...] - drow_ref[...]
    row = row0 + lax.broadcasted_iota(jnp.int32, (tq, keys), 0)
    col = lax.broadcasted_iota(jnp.int32, (tq, keys), 1)
    return qs, jnp.where(col <= row, s, NEG_BIG)


def attn_fwd(q, kv, dcol, drow):
    t, d = q.shape
    nh = d // HEAD_DIM
    tq, parts = _attn_parts(t)
    o = lse = None
    for first, count, keys in parts:
        def body(q_ref, k_ref, v_ref, dcol_ref, drow_ref, *rest, first=first, keys=keys):
            o_ref, lse_ref = rest[-2:]
            _, s = _attn_logits(q_ref, k_ref, dcol_ref, drow_ref, (first + pl.program_id(1)) * tq, tq, keys)
            m = jnp.max(s, axis=1, keepdims=True)
            p = jnp.exp(s - m)
            l = jnp.sum(p, axis=1, keepdims=True)
            acc = jnp.dot(p.astype(BF16), v_ref[...], preferred_element_type=F32)
            o_ref[...] = (acc / l).astype(o_ref.dtype)
            lse_ref[...] = m + jnp.log(l)

        tile = pl.BlockSpec((tq, HEAD_DIM), lambda h, i, first=first: (first + i, h))
        col = pl.BlockSpec((None, tq, 1), lambda h, i, first=first: (h, first + i, 0))
        seen_k = pl.BlockSpec((keys, HEAD_DIM), lambda h, i: (0, h))
        seen_v = pl.BlockSpec((keys, HEAD_DIM), lambda h, i: (0, nh + h))
        carried = [] if o is None else [o, lse]
        o, lse = pl.pallas_call(
            body, name=f"attn_fwd_{first}", grid=(nh, count),
            in_specs=[tile, seen_k, seen_v, col, pl.BlockSpec((None, 1, keys), lambda h, i: (h, 0, 0))]
            + [pl.BlockSpec(memory_space=pl.ANY)] * len(carried),
            out_specs=[tile, col],
            out_shape=[jax.ShapeDtypeStruct((t, d), BF16), jax.ShapeDtypeStruct((nh, t, 1), F32)],
            input_output_aliases={5: 0, 6: 1} if carried else {},
            compiler_params=_params(("parallel", "parallel")),
        )(q, kv, kv, dcol, drow, *carried)
    return o, lse


def attn_bwd(q, kv, dcol, drow, lse, do):
    t, d = q.shape
    nh = d // HEAD_DIM
    tq, parts = _attn_parts(t)
    dq = dk = dv = ddrow = None
    for first, count, keys in reversed(parts):
        first_call = dq is None

        def body(q_ref, k_ref, v_ref, dcol_ref, drow_ref, lse_ref, do_ref, *rest, first=first, keys=keys,
                 count=count, first_call=first_call):
            dq_ref, dk_ref, dv_ref, ddrow_ref, dk_acc, dv_acc, ddrow_acc = rest[-7:]
            i = pl.program_id(1)

            @pl.when(i == 0)
            def _():
                if first_call:
                    dk_acc[...] = jnp.zeros_like(dk_acc)
                    dv_acc[...] = jnp.zeros_like(dv_acc)
                    ddrow_acc[...] = jnp.zeros_like(ddrow_acc)
                else:
                    dk_acc[...] = rest[1][...]
                    dv_acc[...] = rest[2][...]
                    ddrow_acc[...] = rest[3][...]

            qs, s = _attn_logits(q_ref, k_ref, dcol_ref, drow_ref, (first + i) * tq, tq, keys)
            p = jnp.exp(s - lse_ref[...])
            do = do_ref[...]
            dp = lax.dot_general(do, v_ref[...], NT, preferred_element_type=F32)
            ds = p * (dp - jnp.sum(p * dp, axis=1, keepdims=True))
            dsb = ds.astype(BF16)
            dq_ref[...] = (jnp.dot(dsb, k_ref[...], preferred_element_type=F32) * (HEAD_DIM ** -0.5)).astype(dq_ref.dtype)
            dk_acc[...] += lax.dot_general(dsb, qs, TN, preferred_element_type=F32)
            dv_acc[...] += lax.dot_general(p.astype(BF16), do, TN, preferred_element_type=F32)
            ddrow_acc[...] -= jnp.sum(ds, axis=0, keepdims=True)

            @pl.when(i == count - 1)
            def _():
                dk_ref[...] = dk_acc[...]
                dv_ref[...] = dv_acc[...]
                ddrow_ref[...] = ddrow_acc[...]

        tile = pl.BlockSpec((tq, HEAD_DIM), lambda h, i, first=first: (first + i, h))
        col = pl.BlockSpec((None, tq, 1), lambda h, i, first=first: (h, first + i, 0))
        seen = pl.BlockSpec((keys, HEAD_DIM), lambda h, i: (0, h))
        seen_v = pl.BlockSpec((keys, HEAD_DIM), lambda h, i: (0, nh + h))
        seen_row = pl.BlockSpec((None, 1, keys), lambda h, i: (h, 0, 0))
        carried = [] if first_call else [dq, dk, dv, ddrow]
        carried_specs = [] if first_call else [pl.BlockSpec(memory_space=pl.ANY), seen, seen, seen_row]
        dq, dk, dv, ddrow = pl.pallas_call(
            body, name=f"attn_bwd_{first}", grid=(nh, count),
            in_specs=[tile, seen, seen_v, col, seen_row, col, tile] + carried_specs,
            out_specs=[tile, seen, seen, seen_row],
            out_shape=[jax.ShapeDtypeStruct((t, d), BF16), jax.ShapeDtypeStruct((t, d), F32),
                       jax.ShapeDtypeStruct((t, d), F32), jax.ShapeDtypeStruct((nh, 1, t), F32)],
            scratch_shapes=[pltpu.VMEM((keys, HEAD_DIM), F32), pltpu.VMEM((keys, HEAD_DIM), F32),
                            pltpu.VMEM((1, keys), F32)],
            input_output_aliases={} if first_call else {7: 0, 8: 1, 9: 2, 10: 3},
            compiler_params=_params(("parallel", "arbitrary")),
        )(q, kv, kv, dcol, drow, lse, do, *carried)
    return dq, dk, dv, ddrow


def _my_index():
    return (lax.axis_index("x") * 2 + lax.axis_index("y")) * 2 + lax.axis_index("c")


def all_gather(name, src, deps=()):
    def body(src_ref, *rest):
        out_ref, send_sems, recv_sems, local_sem = rest[len(deps):]
        x, y, c = (lax.axis_index(a) for a in MESH_AXES)
        me = (x * 2 + y) * 2 + c
        local = pltpu.make_async_copy(src_ref, out_ref.at[me], local_sem)
        local.start()
        copies = []
        for dlt in range(1, N_DEV):
            copies.append(pltpu.make_async_remote_copy(
                src_ref=src_ref, dst_ref=out_ref.at[me], send_sem=send_sems.at[dlt - 1],
                recv_sem=recv_sems.at[dlt - 1], device_id=(x ^ (dlt // 4), y ^ ((dlt // 2) % 2), c ^ (dlt % 2)),
                device_id_type=pl.DeviceIdType.MESH))
        for cp in copies:
            cp.start()
        for cp in copies:
            cp.wait_recv()
        for cp in copies:
            cp.wait_send()
        local.wait()

    return pl.pallas_call(
        body, name=name, out_shape=jax.ShapeDtypeStruct((N_DEV,) + tuple(src.shape), src.dtype),
        in_specs=[pl.BlockSpec(memory_space=pl.ANY)] * (1 + len(deps)), out_specs=pl.BlockSpec(memory_space=pl.ANY),
        scratch_shapes=[pltpu.SemaphoreType.DMA((N_DEV - 1,)), pltpu.SemaphoreType.DMA((N_DEV - 1,)),
                        pltpu.SemaphoreType.DMA],
        compiler_params=pltpu.CompilerParams(has_side_effects=True),
    )(src, *deps)


_HBM = pl.BlockSpec(memory_space=pltpu.HBM)
_SEM = pl.BlockSpec(memory_space=pltpu.SEMAPHORE)
_DATAFLOW = pltpu.SideEffectType.DATAFLOW_SIDE_EFFECTING


def _peer_copies(src_ref, land_ref, send_sems, recv_sems):
    x, y, c = (lax.axis_index(a) for a in MESH_AXES)
    me = (x * 2 + y) * 2 + c
    copies = []
    for dlt in range(1, N_DEV):
        px, py, pc = x ^ (dlt // 4), y ^ ((dlt // 2) % 2), c ^ (dlt % 2)
        peer = (px * 2 + py) * 2 + pc
        copies.append(pltpu.make_async_remote_copy(
            src_ref=src_ref.at[peer], dst_ref=land_ref.at[me],
            send_sem=send_sems.at[dlt - 1], recv_sem=recv_sems.at[dlt - 1],
            device_id=(px, py, pc), device_id_type=pl.DeviceIdType.MESH))
    return copies


def _own_copy(src_ref, land_ref, send_sems):
    me = (lax.axis_index("x") * 2 + lax.axis_index("y")) * 2 + lax.axis_index("c")
    return pltpu.make_async_copy(src_ref.at[me], land_ref.at[me], send_sems.at[N_DEV - 1])


def scatter_start(name, srcs):
    n = len(srcs)
    lands = [lax.empty(s.shape, s.dtype) for s in srcs]

    def body(*refs):
        src_refs, land_refs = refs[:n], refs[n:2 * n]
        send_sems, recv_sems = refs[2 * n:3 * n], refs[3 * n:4 * n]
        token = refs[-1]
        for j in range(n):
            for cp in _peer_copies(src_refs[j], land_refs[j], send_sems[j], recv_sems[j]):
                cp.start()
            _own_copy(src_refs[j], land_refs[j], send_sems[j]).start()
        token[...] = jnp.zeros_like(token)

    sems = [pltpu.SemaphoreType.DMA((N_DEV,))] * n + [pltpu.SemaphoreType.DMA((N_DEV - 1,))] * n
    thru = [pltpu.HBM(a.shape, a.dtype) for a in list(srcs) + lands]
    outs = pl.pallas_call(
        body, name=name, out_shape=tuple(sems + thru + [jax.ShapeDtypeStruct((8, 128), F32)]),
        in_specs=[_HBM] * (2 * n), out_specs=tuple([_SEM] * (2 * n) + [_HBM] * (2 * n) + [pl.BlockSpec(memory_space=pltpu.VMEM)]),
        input_output_aliases={j: 2 * n + j for j in range(2 * n)},
        compiler_params=pltpu.CompilerParams(has_side_effects=_DATAFLOW),
    )(*[pltpu.with_memory_space_constraint(a, pltpu.HBM) for a in list(srcs) + lands])
    handles = [(outs[j], outs[n + j], outs[2 * n + j], outs[3 * n + j]) for j in range(n)]
    return handles, outs[-1]


def scatter_wait(name, handle, after):
    send_sems, recv_sems, src, land = handle

    def body(src_ref, land_ref, send_ref, recv_ref, after_ref, src_out, land_out):
        for cp in _peer_copies(src_ref, land_ref, send_ref, recv_ref):
            cp.wait_send()
            cp.wait_recv()
        _own_copy(src_ref, land_ref, send_ref).wait()

    return pl.pallas_call(
        body, name=name, out_shape=(pltpu.HBM(src.shape, src.dtype), pltpu.HBM(land.shape, land.dtype)),
        in_specs=[_HBM, _HBM, _SEM, _SEM, pl.BlockSpec(memory_space=pl.ANY)], out_specs=(_HBM, _HBM),
        input_output_aliases={0: 0, 1: 1},
        compiler_params=pltpu.CompilerParams(has_side_effects=_DATAFLOW),
    )(src, land, send_sems, recv_sems, after)


N_OTHER_CHIPS = 3


def _two_level_places():
    x, y, c = (lax.axis_index(a) for a in MESH_AXES)
    return (x, y, c), (x * 2 + y) * 2 + c, (x, y, 1 - c), [(1 - x, y), (x, 1 - y), (1 - x, 1 - y)]


def _first_copies(land_ref, send_sems, recv_sems):
    (x, y, c), me, other_core, chips = _two_level_places()
    targets = [other_core] + [(cx, cy, c) for cx, cy in chips]
    return [pltpu.make_async_remote_copy(
        src_ref=land_ref.at[me], dst_ref=land_ref.at[me], send_sem=send_sems.at[k], recv_sem=recv_sems.at[k],
        device_id=to, device_id_type=pl.DeviceIdType.MESH) for k, to in enumerate(targets)]


def _passed_on_copies(land_ref, send_sems, recv_sems):
    (x, y, c), me, other_core, chips = _two_level_places()
    copies = []
    for k, (cx, cy) in enumerate(chips):
        slot = land_ref.at[(cx * 2 + cy) * 2 + c]
        copies.append(pltpu.make_async_remote_copy(
            src_ref=slot, dst_ref=slot, send_sem=send_sems.at[k], recv_sem=recv_sems.at[k],
            device_id=other_core, device_id_type=pl.DeviceIdType.MESH))
    return copies


def gather_start(name, lands):
    n = len(lands)

    def body(*refs):
        land_refs, send_sems, recv_sems = refs[:n], refs[n:2 * n], refs[2 * n:3 * n]
        for j in range(n):
            for cp in _first_copies(land_refs[j], send_sems[j], recv_sems[j]):
                cp.start()

    sems = [pltpu.SemaphoreType.DMA((1 + N_OTHER_CHIPS,))] * (2 * n)
    outs = pl.pallas_call(
        body, name=name, out_shape=tuple(sems + [pltpu.HBM(a.shape, a.dtype) for a in lands]),
        in_specs=[_HBM] * n, out_specs=tuple([_SEM] * (2 * n) + [_HBM] * n),
        input_output_aliases={j: 2 * n + j for j in range(n)},
        compiler_params=pltpu.CompilerParams(has_side_effects=_DATAFLOW),
    )(*[pltpu.with_memory_space_constraint(a, pltpu.HBM) for a in lands])
    return [[outs[j], outs[n + j], outs[2 * n + j]] for j in range(n)]


def gather_pass_on(name, handle, after):
    send_sems, recv_sems, land = handle

    def body(land_ref, recv_ref, after_ref, land_out, send2, recv2, token):
        arrivals = _first_copies(land_ref, recv_ref, recv_ref)
        for k, cp in enumerate(_passed_on_copies(land_ref, send2, recv2)):
            arrivals[1 + k].wait_recv()
            cp.start()
        token[...] = jnp.zeros_like(token)

    sem3 = pltpu.SemaphoreType.DMA((N_OTHER_CHIPS,))
    land, send2, recv2, token = pl.pallas_call(
        body, name=name,
        out_shape=(pltpu.HBM(land.shape, land.dtype), sem3, sem3, jax.ShapeDtypeStruct((8, 128), F32)),
        in_specs=[_HBM, _SEM, pl.BlockSpec(memory_space=pl.ANY)],
        out_specs=(_HBM, _SEM, _SEM, pl.BlockSpec(memory_space=pltpu.VMEM)),
        input_output_aliases={0: 0}, compiler_params=pltpu.CompilerParams(has_side_effects=_DATAFLOW),
    )(land, recv_sems, after)
    return [send_sems, recv_sems, land, send2, recv2], token


def gather_wait(name, handle, after):
    send_sems, recv_sems, land, send2, recv2 = handle

    def body(land_ref, send_ref, recv_ref, send2_ref, recv2_ref, after_ref, land_out):
        first = _first_copies(land_ref, send_ref, recv_ref)
        for cp in first:
            cp.wait_send()
        first[0].wait_recv()
        for cp in _passed_on_copies(land_ref, send2_ref, recv2_ref):
            cp.wait_send()
            cp.wait_recv()

    return pl.pallas_call(
        body, name=name, out_shape=pltpu.HBM(land.shape, land.dtype),
        in_specs=[_HBM, _SEM, _SEM, _SEM, _SEM, pl.BlockSpec(memory_space=pl.ANY)], out_specs=_HBM,
        input_output_aliases={0: 0}, compiler_params=pltpu.CompilerParams(has_side_effects=_DATAFLOW),
    )(land, send_sems, recv_sems, send2, recv2, after)


N_CHIPS = 4


def _pair_copies(g_ref, half_ref, send_sems, recv_sems):
    (x, y, c), me, other_core, chips = _two_level_places()
    return [pltpu.make_async_remote_copy(
        src_ref=g_ref.at[chip * 2 + (1 - c)], dst_ref=half_ref.at[chip], send_sem=send_sems.at[chip],
        recv_sem=recv_sems.at[chip], device_id=other_core, device_id_type=pl.DeviceIdType.MESH)
        for chip in range(N_CHIPS)]


def pair_start(name, g):
    half = lax.empty((N_CHIPS,) + g.shape[1:], g.dtype)

    def body(g_ref, half_ref, send_sems, recv_sems, g_out, half_out, token):
        for cp in _pair_copies(g_ref, half_ref, send_sems, recv_sems):
            cp.start()
        token[...] = jnp.zeros_like(token)

    sem = pltpu.SemaphoreType.DMA((N_CHIPS,))
    outs = pl.pallas_call(
        body, name=name,
        out_shape=(sem, sem, pltpu.HBM(g.shape, g.dtype), pltpu.HBM(half.shape, half.dtype),
                   jax.ShapeDtypeStruct((8, 128), F32)),
        in_specs=[_HBM, _HBM], out_specs=(_SEM, _SEM, _HBM, _HBM, pl.BlockSpec(memory_space=pltpu.VMEM)),
        input_output_aliases={0: 2, 1: 3}, compiler_params=pltpu.CompilerParams(has_side_effects=_DATAFLOW),
    )(pltpu.with_memory_space_constraint(g, pltpu.HBM), half)
    return list(outs[:4]), outs[4]


def pair_wait(name, handle, after):
    send_sems, recv_sems, g, half = handle

    def body(g_ref, half_ref, send_ref, recv_ref, after_ref, g_out, half_out):
        for cp in _pair_copies(g_ref, half_ref, send_ref, recv_ref):
            cp.wait_send()
            cp.wait_recv()

    return pl.pallas_call(
        body, name=name, out_shape=(pltpu.HBM(g.shape, g.dtype), pltpu.HBM(half.shape, half.dtype)),
        in_specs=[_HBM, _HBM, _SEM, _SEM, pl.BlockSpec(memory_space=pl.ANY)], out_specs=(_HBM, _HBM),
        input_output_aliases={0: 0, 1: 1}, compiler_params=pltpu.CompilerParams(has_side_effects=_DATAFLOW),
    )(g, half, send_sems, recv_sems, after)


def pair_sum(name, g, half):
    _, r, wd = g.shape
    tr = _row_tile(r, 2 * ROW_TILE_BYTES // (4 * wd))
    kind = lax.axis_index("c").astype(jnp.int32).reshape(1)

    def body(kind_ref, g_ref, half_ref, o_ref):
        o_ref[...] = (g_ref[...].astype(F32) + half_ref[...].astype(F32)).astype(o_ref.dtype)

    spec = pl.BlockSpec((None, tr, wd), lambda chip, i, kind_ref: (chip, i, 0))
    return pl.pallas_call(
        body, name=name,
        grid_spec=pltpu.PrefetchScalarGridSpec(
            num_scalar_prefetch=1, grid=(N_CHIPS, r // tr),
            in_specs=[pl.BlockSpec((None, tr, wd), lambda chip, i, kind_ref: (chip * 2 + kind_ref[0], i, 0)), spec],
            out_specs=spec),
        out_shape=jax.ShapeDtypeStruct((N_CHIPS, r, wd), g.dtype),
        compiler_params=_params(("parallel", "parallel")),
    )(kind, g, half)


def _chip_copies(sums_ref, land_ref, send_sems, recv_sems):
    (x, y, c), me, other_core, chips = _two_level_places()
    return [pltpu.make_async_remote_copy(
        src_ref=sums_ref.at[cx * 2 + cy], dst_ref=land_ref.at[x * 2 + y], send_sem=send_sems.at[k],
        recv_sem=recv_sems.at[k], device_id=(cx, cy, c), device_id_type=pl.DeviceIdType.MESH)
        for k, (cx, cy) in enumerate(chips)]


def chip_start(name, sums):
    land = lax.empty(sums.shape, sums.dtype)

    def body(sums_ref, land_ref, send_sems, recv_sems, sums_out, land_out, token):
        for cp in _chip_copies(sums_ref, land_ref, send_sems, recv_sems):
            cp.start()
        token[...] = jnp.zeros_like(token)

    sem = pltpu.SemaphoreType.DMA((N_OTHER_CHIPS,))
    outs = pl.pallas_call(
        body, name=name,
        out_shape=(sem, sem, pltpu.HBM(sums.shape, sums.dtype), pltpu.HBM(land.shape, land.dtype),
                   jax.ShapeDtypeStruct((8, 128), F32)),
        in_specs=[_HBM, _HBM], out_specs=(_SEM, _SEM, _HBM, _HBM, pl.BlockSpec(memory_space=pltpu.VMEM)),
        input_output_aliases={0: 2, 1: 3}, compiler_params=pltpu.CompilerParams(has_side_effects=_DATAFLOW),
    )(pltpu.with_memory_space_constraint(sums, pltpu.HBM), land)
    return list(outs[:4]), outs[4]


def chip_wait(name, handle, after):
    send_sems, recv_sems, sums, land = handle

    def body(sums_ref, land_ref, send_ref, recv_ref, after_ref, sums_out, land_out):
        for cp in _chip_copies(sums_ref, land_ref, send_ref, recv_ref):
            cp.wait_send()
            cp.wait_recv()

    return pl.pallas_call(
        body, name=name, out_shape=(pltpu.HBM(sums.shape, sums.dtype), pltpu.HBM(land.shape, land.dtype)),
        in_specs=[_HBM, _HBM, _SEM, _SEM, pl.BlockSpec(memory_space=pl.ANY)], out_specs=(_HBM, _HBM),
        input_output_aliases={0: 0, 1: 1}, compiler_params=pltpu.CompilerParams(has_side_effects=_DATAFLOW),
    )(sums, land, send_sems, recv_sems, after)


def adamw_reduce(name, parts, w, m, v):
    nl, r, wd = w.shape
    tr = _row_tile(r, ROW_TILE_BYTES // (8 * wd))

    def body(*refs):
        p_refs = refs[:nl]
        w_ref, m_ref, v_ref, g_ref, d_ref, nm_ref, nv_ref = refs[nl:]
        layer = pl.program_id(0)
        for j in range(nl):
            @pl.when(layer == j)
            def _(j=j):
                g = p_refs[j][0].astype(F32)
                for sender in range(1, p_refs[j].shape[0]):
                    g = g + p_refs[j][sender].astype(F32)
                nm = B1 * m_ref[...] + (1.0 - B1) * g
                nv = B2 * v_ref[...] + (1.0 - B2) * jnp.square(g)
                m_hat = nm / (1.0 - B1 ** STEP)
                v_hat = nv / (1.0 - B2 ** STEP)
                g_ref[...] = g
                d_ref[...] = -LR * (m_hat / (jnp.sqrt(v_hat) + ADAM_EPS) + WD * w_ref[...])
                nm_ref[...] = nm
                nv_ref[...] = nv

    def part_spec(j):
        return pl.BlockSpec((parts[j].shape[0], tr, wd), lambda l, i: (0, jnp.where(l == j, i, 0), 0))

    spec = pl.BlockSpec((None, tr, wd), lambda l, i: (l, i, 0))
    return pl.pallas_call(
        body, name=name, grid=(nl, r // tr),
        in_specs=[part_spec(j) for j in range(nl)] + [spec, spec, spec],
        out_specs=[spec] * 4, out_shape=[jax.ShapeDtypeStruct((nl, r, wd), F32)] * 4,
        compiler_params=_params(("arbitrary", "arbitrary")),
    )(*parts, w, m, v)


def _pack_rows(vectors, rows=None):
    flat = jnp.concatenate([a.reshape(-1).astype(F32) for a in vectors])
    n = flat.shape[0]
    if rows is None:
        rows = -(-n // 1024) * 8
    return jnp.pad(flat, (0, rows * 128 - n)).reshape(rows, 128)


def _unpack_rows(packed, like):
    flat = packed.reshape(-1)
    out, pos = [], 0
    for a in like:
        out.append(flat[pos:pos + a.size].reshape(a.shape))
        pos += a.size
    return out


def kernel(x, p, mix_norm, mlp_norm, ple_norm, w_a_in, a_lb_logits, a_head_gain, w_a_out, kv_norm, w_kvf, b_f, w_b_q, w_b_out, w_mlp_up, w_mlp_down, w_ple_gate, w_ple_up, final_norm, loss_target, m_mix_norm, m_mlp_norm, m_ple_norm, m_w_a_in, m_a_lb_logits, m_a_head_gain, m_w_a_out, m_kv_norm, m_w_kvf, m_b_f, m_w_b_q, m_w_b_out, m_w_mlp_up, m_w_mlp_down, m_w_ple_gate, m_w_ple_up, m_final_norm, v_mix_norm, v_mlp_norm, v_ple_norm, v_w_a_in, v_a_lb_logits, v_a_head_gain, v_w_a_out, v_kv_norm, v_w_kvf, v_b_f, v_w_b_q, v_w_b_out, v_w_mlp_up, v_w_mlp_down, v_w_ple_gate, v_w_ple_up, v_final_norm):
    t, d = x.shape[1], x.shape[2]
    nh = d // HEAD_DIM
    n_layers = 2
    x2 = x.reshape(t, d)
    target = loss_target.reshape(t, d)
    me = _my_index()

    shards = {"w_a_in": w_a_in[0], "w_a_out": w_a_out[0], "w_kvf": w_kvf, "w_b_q": w_b_q[0], "w_b_out": w_b_out[0]}
    for l in range(n_layers):
        shards.update({f"w_mlp_up{l}": w_mlp_up[l], f"w_mlp_down{l}": w_mlp_down[l],
                       f"w_ple_gate{l}": w_ple_gate[l], f"w_ple_up{l}": w_ple_up[l]})
    first_use = ["a_lb_logits", "w_a_in", "w_a_out", "w_mlp_up0", "w_mlp_down0", "w_ple_gate0", "w_ple_up0", "w_kvf",
                 "w_b_q", "w_b_out", "w_mlp_up1", "w_mlp_down1", "w_ple_gate1", "w_ple_up1"]
    row_sharded = ("w_a_out", "w_b_q", "w_b_out", "w_mlp_down", "w_ple_gate")
    shards_bf = [a_lb_logits] + [shards[n].astype(BF16) for n in first_use[1:]]
    ag_handles = gather_start("ag_start", [
        lax.dynamic_update_slice(lax.empty((N_DEV,) + a.shape, a.dtype), a[None], (me, 0, 0)) for a in shards_bf])
    passed_on = {}
    weights = {}

    def pass_on(j, after):
        if j < len(first_use) and j not in passed_on:
            passed_on[j] = gather_pass_on("ag_pass_" + first_use[j], ag_handles[j], after)

    def weight(name, after=None):
        if name not in weights:
            j = first_use.index(name)
            pass_on(j, after)
            pass_on(j + 1, after)
            behind = passed_on[j + 1][1] if j + 1 in passed_on else after
            g = gather_wait("ag_wait_" + name, passed_on[j][0], behind)
            if name.rstrip("01") in row_sharded:
                g = g.reshape(1, g.shape[0] * g.shape[1], g.shape[2])
            weights[name] = g
        return weights[name]

    lgt = weight("a_lb_logits", x2).transpose(1, 0, 2).reshape(2, d)
    p_bf = [p[l, 0].astype(BF16) for l in range(n_layers)]

    def row(vec):
        return vec.reshape(1, -1)

    def mlp_ple_fwd(l, h_in, a):
        (h_a, u_mlp), _ = rowwise(f"add_norm_mlp{l}", _add_norm_fwd, [h_in, a], [row(mlp_norm[l])])
        pre, act = mm_nn(f"mlp_up{l}", u_mlp, weight(f"w_mlp_up{l}", u_mlp), fuse=(_relu2, (), (BF16, BF16)))
        mo = mm_nn(f"mlp_down{l}", act, weight(f"w_mlp_down{l}", act))
        (h_b, u_ple), _ = rowwise(f"add_norm_ple{l}", _add_norm_fwd, [h_a, mo], [row(ple_norm[l])])
        gpre = mm_nn(f"ple_gate{l}", u_ple, weight(f"w_ple_gate{l}", u_ple))
        pu = mm_nn(f"ple_up{l}", p_bf[l], weight(f"w_ple_up{l}", gpre))
        return dict(h_a=h_a, u_mlp=u_mlp, pre=pre, act=act, h_b=h_b, u_ple=u_ple, gpre=gpre, pu=pu)

    (u0,), _ = rowwise("norm_mix0", _norm_fwd, [x2], [row(mix_norm[0])])
    z = mm_nn("a_in", u0, weight("w_a_in", u0))
    og, states = hgrn_fwd(z, lgt, a_head_gain)
    a0 = mm_nn("a_out", og, weight("w_a_out", og))
    s0 = mlp_ple_fwd(0, x2, a0)
    (h3, u_kv, u1), _ = rowwise("ple_norms", _ple_two_norms_fwd, [s0["h_b"], s0["gpre"], s0["pu"]],
                                [row(kv_norm), row(mix_norm[1])])
    w_kvf_cols = weight("w_kvf", u_kv).transpose(1, 0, 2).reshape(d, 2 * d + nh)
    w_kv = w_kvf_cols[:, :2 * d].reshape(d, 2, d).transpose(1, 0, 2)
    w_f = jnp.pad(w_kvf_cols[:, 2 * d:], ((0, 0), (0, HEAD_DIM - nh)))[None]
    kv = mm_nn("kvf", u_kv, w_kv, out_dtype=BF16)
    fl_t = mm_nn("kvf_forget", u_kv, w_f)[:, :nh].T
    b_f_col = b_f.reshape(nh, 1)
    dcum = decay_fwd(fl_t, b_f_col)
    dcol, drow = dcum.reshape(nh, t, 1), dcum.reshape(nh, 1, t)
    q = mm_nn("b_q", u1, weight("w_b_q", dcum))
    o, lse = attn_fwd(q, kv, dcol, drow)
    a1 = mm_nn("b_out", o, weight("w_b_out", o))
    s1 = mlp_ple_fwd(1, h3, a1)

    (dh, dgpre, dpu), (d_final, loss_rows) = rowwise(
        "tail", _tail_fwd_bwd, [s1["h_b"], s1["gpre"], s1["pu"], target], [row(final_norm)])

    sent = {}
    tokens = []

    two_level = ("w_mlp_up0", "w_a_in")
    swapping = []

    def send_grad(name, g):
        g = g.reshape(N_DEV, -1, g.shape[-1])
        if name in two_level:
            sent[name], token = pair_start("rs_pair_" + name, g)
            swapping.append(name)
        else:
            (sent[name],), token = scatter_start("rs_start_" + name, [g])
        tokens.append(token)

    def send_grads_together(grads):
        names = list(grads)
        handles, token = scatter_start("rs_start_" + names[0], [
            grads[n].reshape(N_DEV, -1, grads[n].shape[-1]) for n in names])
        sent.update(zip(names, handles))
        tokens.append(token)

    def second_stage(after):
        for name in swapping:
            g, half = pair_wait("rs_pairwait_" + name, sent[name], after)
            sent[name], token = chip_start("rs_chip_" + name, pair_sum("rs_sum_" + name, g, half))
            tokens.append(token)
        swapping.clear()

    def after_sends():
        deps = tuple(tokens)
        tokens.clear()
        return deps

    def mlp_ple_bwd(l, s, dh, dgpre, dpu):
        send_grads_together({
            f"w_ple_gate{l}": mm_tn(f"d_ple_gate_w{l}", s["u_ple"], dgpre, 1, deps=after_sends()),
            f"w_ple_up{l}": mm_tn(f"d_ple_up_w{l}", p_bf[l], dpu, N_DEV)})
        du = mm_nt(f"d_ple_gate_x{l}", dgpre, weight(f"w_ple_gate{l}"), out_dtype=BF16, deps=after_sends())
        (dh, dh_bf), (d_ple,) = rowwise(f"d_norm_ple{l}", _norm_bwd, [s["h_b"], du, dh], [row(ple_norm[l])])
        send_grad(f"w_mlp_down{l}", mm_tn(f"d_mlp_down_w{l}", s["act"], dh_bf, 1))
        (dpre,) = mm_nt(f"d_mlp_down_x{l}", dh_bf, weight(f"w_mlp_down{l}"), deps=after_sends(),
                        fuse=(_relu2_bwd, (s["pre"],), (BF16,)))
        second_stage(dpre)
        send_grad(f"w_mlp_up{l}", mm_tn(f"d_mlp_up_w{l}", s["u_mlp"], dpre, N_DEV))
        du = mm_nt(f"d_mlp_up_x{l}", dpre, weight(f"w_mlp_up{l}"), out_dtype=BF16, deps=after_sends())
        second_stage(du)
        (dh, dh_bf), (d_mlp,) = rowwise(f"d_norm_mlp{l}", _norm_bwd, [s["h_a"], du, dh], [row(mlp_norm[l])])
        return dh, dh_bf, d_ple, d_mlp

    dh, dh_bf, d_ple1, d_mlp1 = mlp_ple_bwd(1, s1, dh, dgpre, dpu)
    send_grad("w_b_out", mm_tn("d_b_out_w", o, dh_bf, 1))
    do = mm_nt("d_b_out_x", dh_bf, weight("w_b_out"), out_dtype=BF16, deps=after_sends())
    dq, dk, dv, ddrow = attn_bwd(q, kv, dcol, drow, lse, do)
    send_grad("w_b_q", mm_tn("d_b_q_w", u1, dq, 1))
    du1 = mm_nt("d_b_q_x", dq, weight("w_b_q"), out_dtype=BF16, deps=after_sends())
    dfl_t, d_b_f = decay_bwd(fl_t, b_f_col, ddrow.reshape(nh, t))
    dkv = jnp.stack([dk, dv]).astype(BF16)
    dfl = jnp.pad(dfl_t.T, ((0, 0), (0, HEAD_DIM - nh))).astype(BF16)
    d_w_kv = mm_tn("d_kvf_w", u_kv, dkv, 2)
    d_w_f = mm_tn("d_kvf_forget_w", u_kv, dfl, 1)
    d_w_kvf = jnp.concatenate([d_w_kv[0], d_w_kv[1], d_w_f[0, :, :nh]], axis=1)
    per_owner = (2 * d + nh) // N_DEV
    send_grad("w_kvf", jnp.stack([d_w_kvf[:, p * per_owner:(p + 1) * per_owner] for p in range(N_DEV)]))
    du_f = mm_nt("d_kvf_forget_x", dfl, w_f, deps=after_sends())
    (du_kv,) = mm_nt("d_kvf_x", dkv, w_kv, fuse=(lambda acc, extra: (acc + extra,), (du_f,), (BF16,)))
    second_stage(du_kv)
    (dh, dgpre, dpu), (d_kv_norm, d_mix1) = rowwise(
        "d_ple_norms", _two_norms_ple_bwd, [h3, du_kv, du1, dh, s0["gpre"], s0["pu"]],
        [row(kv_norm), row(mix_norm[1])])
    dh, dh_bf, d_ple0, d_mlp0 = mlp_ple_bwd(0, s0, dh, dgpre, dpu)
    send_grad("w_a_out", mm_tn("d_a_out_w", og, dh_bf, 1))
    dog = mm_nt("d_a_out_x", dh_bf, weight("w_a_out"), out_dtype=BF16, deps=after_sends())
    dz4, d_lgt, d_hg = hgrn_bwd(z, lgt, a_head_gain, states, dog)
    send_grad("w_a_in", mm_tn("d_a_in_w", u0, dz4, N_DEV, stacked=True))
    du0 = mm_nt("d_a_in_x", dz4, weight("w_a_in"), out_dtype=BF16, deps=after_sends(), stacked=True)
    second_stage(du0)
    (dx, _), (d_mix0,) = rowwise("d_norm_mix0", _norm_bwd, [x2, du0, dh], [row(mix_norm[0])])

    new = {}
    last = [dx]

    def update(name, parts, w, m, v):
        shp = w.shape
        w3, m3, v3 = (a.reshape(len(parts), -1, shp[-1]) for a in (w, m, v))
        new[name] = tuple(a.reshape(shp) for a in adamw_reduce("adamw_" + name, parts, w3, m3, v3))
        last[0] = new[name][0]

    def receive_update(name, layers, w, m, v):
        parts = {}
        for sfx in layers:
            if name + sfx in two_level:
                mine, land = chip_wait(f"rs_wait_{name}{sfx}", sent[name + sfx], last[0])
                chip = me // 2
                parts[sfx] = lax.dynamic_update_slice(land, lax.dynamic_slice_in_dim(mine, chip, 1, 0), (chip, 0, 0))
            else:
                _, parts[sfx] = scatter_wait(f"rs_wait_{name}{sfx}", sent[name + sfx], last[0])
        update(name, [parts[sfx] for sfx in sorted(layers)], w, m, v)

    both = ("1", "0")
    receive_update("w_b_out", ("",), w_b_out, m_w_b_out, v_w_b_out)
    receive_update("w_b_q", ("",), w_b_q, m_w_b_q, v_w_b_q)
    receive_update("w_kvf", ("",), w_kvf, m_w_kvf, v_w_kvf)
    receive_update("w_ple_gate", both, w_ple_gate, m_w_ple_gate, v_w_ple_gate)
    receive_update("w_ple_up", both, w_ple_up, m_w_ple_up, v_w_ple_up)
    receive_update("w_mlp_down", both, w_mlp_down, m_w_mlp_down, v_w_mlp_down)
    receive_update("w_mlp_up", both, w_mlp_up, m_w_mlp_up, v_w_mlp_up)
    receive_update("w_a_out", ("",), w_a_out, m_w_a_out, v_w_a_out)
    receive_update("w_a_in", ("",), w_a_in, m_w_a_in, v_w_a_in)

    small = dict(mix_norm=jnp.concatenate([d_mix0, d_mix1]), mlp_norm=jnp.concatenate([d_mlp0, d_mlp1]),
                 ple_norm=jnp.concatenate([d_ple0, d_ple1]), a_head_gain=d_hg, kv_norm=d_kv_norm.reshape(d),
                 b_f=d_b_f.reshape(nh), final_norm=d_final.reshape(d))
    small_w = dict(mix_norm=(mix_norm, m_mix_norm, v_mix_norm), mlp_norm=(mlp_norm, m_mlp_norm, v_mlp_norm),
                   ple_norm=(ple_norm, m_ple_norm, v_ple_norm),
                   a_head_gain=(a_head_gain, m_a_head_gain, v_a_head_gain), kv_norm=(kv_norm, m_kv_norm, v_kv_norm),
                   b_f=(b_f, m_b_f, v_b_f), final_norm=(final_norm, m_final_norm, v_final_norm))
    names = list(small)
    packed = _pack_rows([d_lgt] + [small[n] for n in names])
    everyone = all_gather("ag_small_grads", packed, deps=(last[0],))
    n_lgt_rows = d_lgt.size // 128
    lgt_parts = everyone[:, :n_lgt_rows].reshape(N_DEV, 2, d)
    lgt_parts = lax.dynamic_slice_in_dim(lgt_parts, me * a_lb_logits.shape[1], a_lb_logits.shape[1], axis=2)
    update("a_lb_logits", [lgt_parts], a_lb_logits, m_a_lb_logits, v_a_lb_logits)
    rest = everyone[:, n_lgt_rows:]
    like = [small_w[n][0] for n in names]
    packed_w, packed_m, packed_v = (_pack_rows([small_w[n][j] for n in names], rest.shape[1])[None] for j in range(3))
    outs = adamw_reduce("adamw_small", [rest], packed_w, packed_m, packed_v)
    unpacked = [_unpack_rows(a, like) for a in outs]
    for j, n in enumerate(names):
        new[n] = tuple(unpacked[q][j] for q in range(4))

    order = ["mix_norm", "mlp_norm", "ple_norm", "w_a_in", "a_lb_logits", "a_head_gain", "w_a_out", "kv_norm",
             "w_kvf", "b_f", "w_b_q", "w_b_out", "w_mlp_up", "w_mlp_down", "w_ple_gate", "w_ple_up", "final_norm"]
    loss_here, _ = lax.optimization_barrier((loss_rows[0, 0], new["final_norm"][0]))
    loss = lax.psum(loss_here, MESH_AXES)
    result = [loss, dx.reshape(x.shape)]
    for j in range(4):
        result += [new[n][j] for n in order]
    return tuple(result)
```

```python
import jax
import jax.numpy as jnp
from jax import lax
from jax.experimental import pallas as pl
from jax.experimental.pallas import tpu as pltpu

F32 = jnp.float32
BF16 = jnp.bfloat16
HEAD_DIM = 128
CHUNK = 16
TILE = 256
HEADS_PER_STEP = 4
NORM_EPS = 1e-6
N_DEV = 8
MESH_AXES = ("x", "y", "c")
VMEM_LIMIT_BYTES = 48 * 1024 * 1024
ROW_TILE_BYTES = 2 * 1024 * 1024
LR, B1, B2, ADAM_EPS, WD, STEP = 0.001, 0.9, 0.999, 1e-08, 0.01, 10
NEG_BIG = -1e30

NN = (((1,), (0,)), ((), ()))
NT = (((1,), (1,)), ((), ()))
TN = (((0,), (0,)), ((), ()))


def _params(semantics):
    return pltpu.CompilerParams(dimension_semantics=semantics, vmem_limit_bytes=VMEM_LIMIT_BYTES)


def _tile(n, prefs):
    for p in prefs:
        if n % p == 0:
            return p
    return n


def _row_tile(rows, limit):
    for cand in (2048, 1024, 512, 256, 128, 64, 32, 16):
        if cand <= limit and rows % cand == 0:
            return cand
    return rows


def _mm_call(name, a, b, dims, grid, a_spec, b_spec, o_spec, o_shape, acc_shape, k_axes, out_dtype, deps=(),
             fuse=None, split=1):
    nk = 1
    for ax in k_axes:
        nk *= grid[ax]
    fn, extra, out_dtypes = fuse if fuse else (lambda acc: (acc,), (), (out_dtype,))
    n_extra, n_out = len(extra), len(out_dtypes)

    def finish(acc, rest):
        o_refs = rest[n_extra + len(deps):n_extra + len(deps) + n_out]
        for ref, val in zip(o_refs, fn(acc, *[r[...] for r in rest[:n_extra]])):
            ref[...] = val.astype(ref.dtype)

    def product(a_ref, b_ref):
        if split == 1:
            return lax.dot_general(a_ref[...], b_ref[...], dims, preferred_element_type=F32)
        wide = a_ref.shape[1] // split
        return sum(lax.dot_general(a_ref[:, q * wide:(q + 1) * wide], b_ref[q], dims, preferred_element_type=F32)
                   for q in range(split))

    def one_step(a_ref, b_ref, *rest):
        finish(product(a_ref, b_ref), rest)

    def accumulate(a_ref, b_ref, *rest):
        acc_ref = rest[-1]
        k = 0
        for ax in k_axes:
            k = k * grid[ax] + pl.program_id(ax)
        part = product(a_ref, b_ref)

        @pl.when(k == 0)
        def _():
            acc_ref[...] = part

        @pl.when((k > 0) & (k < nk - 1))
        def _():
            acc_ref[...] += part

        @pl.when(k == nk - 1)
        def _():
            finish(acc_ref[...] + part, rest)

    sem = tuple("arbitrary" if ax in k_axes else "parallel" for ax in range(len(grid)))
    outs = pl.pallas_call(
        one_step if nk == 1 else accumulate, name=name, grid=grid,
        in_specs=[a_spec, b_spec] + [o_spec] * n_extra + [pl.BlockSpec(memory_space=pl.ANY)] * len(deps),
        out_specs=[o_spec] * n_out, out_shape=[jax.ShapeDtypeStruct(o_shape, dt) for dt in out_dtypes],
        scratch_shapes=[] if nk == 1 else [pltpu.VMEM(acc_shape, F32)], compiler_params=_params(sem),
    )(a, b, *extra, *deps)
    return outs if fuse else outs[0]


def mm_nn(name, a, b3, out_dtype=F32, out3=False, deps=(), fuse=None):
    m, k = a.shape
    g, _, n = b3.shape
    tm, tk = _tile(m, (1024, 512, 256)), _tile(k, (2048, 1024, 512, 256))
    tn = n if out3 else _tile(n, (1024, 512, 256, 128))
    nj = n // tn
    grid = (m // tm, g, nj, k // tk)
    a_spec = pl.BlockSpec((tm, tk), lambda i, gg, j, kk: (i, kk))
    b_spec = pl.BlockSpec((None, tk, tn), lambda i, gg, j, kk: (gg, kk, j))
    if out3:
        o_spec = pl.BlockSpec((None, tm, tn), lambda i, gg, j, kk: (gg, i, j))
        o_shape = (g, m, n)
    else:
        o_spec = pl.BlockSpec((tm, tn), lambda i, gg, j, kk: (i, gg * nj + j))
        o_shape = (m, g * n)
    return _mm_call(name, a, b3, NN, grid, a_spec, b_spec, o_spec, o_shape, (tm, tn), (3,), out_dtype, deps, fuse)


def mm_nt(name, a, b3, out_dtype=F32, deps=(), fuse=None, stacked=False):
    g, k, n = b3.shape
    a3 = a.ndim == 3 and not stacked
    m = a.shape[1] if a.ndim == 3 else a.shape[0]
    tm, tko = _tile(m, (1024, 512, 256)), _tile(k, (1024, 512, 256))
    tc = n if a3 else _tile(n, (2048, 1024, 512, 256, 128))
    nc = n // tc
    per = g // a.shape[0] if stacked else g
    pair = 2 if (not a3 and nc == 1 and per % 2 == 0 and tc <= 1024) else 1
    grid = (m // tm, k // tko, g // pair, nc)
    if a3:
        a_spec = pl.BlockSpec((None, tm, tc), lambda i, j, gg, c: (gg, i, c))
    elif stacked:
        a_spec = pl.BlockSpec((None, tm, pair * tc),
                              lambda i, j, gg, c: ((gg * pair) // per, i, (((gg * pair) % per) // pair) * nc + c))
    else:
        a_spec = pl.BlockSpec((tm, pair * tc), lambda i, j, gg, c: (i, gg * nc + c))
    if pair == 1:
        b_spec = pl.BlockSpec((None, tko, tc), lambda i, j, gg, c: (gg, j, c))
    else:
        b_spec = pl.BlockSpec((pair, tko, tc), lambda i, j, gg, c: (gg, j, c))
    o_spec = pl.BlockSpec((tm, tko), lambda i, j, gg, c: (i, j))
    return _mm_call(name, a, b3, NT, grid, a_spec, b_spec, o_spec, (m, k), (tm, tko), (2, 3), out_dtype, deps, fuse,
                    pair)


def mm_tn(name, a, b, g, out_dtype=BF16, deps=(), stacked=False):
    t, k = a.shape
    b3 = b.ndim == 3 and not stacked
    n = b.shape[2] if b3 else (b.shape[0] * b.shape[2] if stacked else b.shape[1]) // g
    tm = _tile(k, (1024, 512, 256))
    tn = n if b3 else _tile(n, (1024, 512, 256, 128))
    tt = _tile(t, (2048, 1024, 512, 256))
    nj = n // tn
    grid = (g, k // tm, nj, t // tt)
    a_spec = pl.BlockSpec((tt, tm), lambda gg, i, j, s: (s, i))
    if b3:
        b_spec = pl.BlockSpec((None, tt, tn), lambda gg, i, j, s: (gg, s, j))
    elif stacked:
        per = g // b.shape[0]
        b_spec = pl.BlockSpec((None, tt, tn), lambda gg, i, j, s: (gg // per, s, (gg % per) * nj + j))
    else:
        b_spec = pl.BlockSpec((tt, tn), lambda gg, i, j, s: (s, gg * nj + j))
    o_spec = pl.BlockSpec((None, tm, tn), lambda gg, i, j, s: (gg, i, j))
    return _mm_call(name, a, b, TN, grid, a_spec, b_spec, o_spec, (g, k, n), (tm, tn), (3,), out_dtype, deps)


def rowwise(name, fn, rows, vecs=()):
    t = rows[0].shape[0]
    wmax = max(r.shape[1] for r in rows)
    tr = _row_tile(t, ROW_TILE_BYTES // (4 * wmax))
    row_s = [jax.ShapeDtypeStruct((tr, r.shape[1]), r.dtype) for r in rows]
    vec_s = [jax.ShapeDtypeStruct(v.shape, v.dtype) for v in vecs]
    out_rows_s, out_sums_s = jax.eval_shape(fn, *row_s, *vec_s)
    n_in, n_r = len(rows) + len(vecs), len(out_rows_s)

    def body(*refs):
        i = pl.program_id(0)
        o_rows, o_sums = fn(*[r[...] for r in refs[:n_in]])
        for ref, val in zip(refs[n_in:n_in + n_r], o_rows):
            ref[...] = val

        if out_sums_s:
            @pl.when(i == 0)
            def _():
                for ref in refs[n_in + n_r:]:
                    ref[...] = jnp.zeros_like(ref)

            for ref, val in zip(refs[n_in + n_r:], o_sums):
                ref[...] += val

    in_specs = [pl.BlockSpec((tr, r.shape[1]), lambda i: (i, 0)) for r in rows]
    in_specs += [pl.BlockSpec(v.shape, lambda i: (0, 0)) for v in vecs]
    out_specs = [pl.BlockSpec((tr, s.shape[1]), lambda i: (i, 0)) for s in out_rows_s]
    out_specs += [pl.BlockSpec(s.shape, lambda i: (0, 0)) for s in out_sums_s]
    out_shape = [jax.ShapeDtypeStruct((t, s.shape[1]), s.dtype) for s in out_rows_s]
    out_shape += [jax.ShapeDtypeStruct(s.shape, s.dtype) for s in out_sums_s]
    outs = pl.pallas_call(
        body, name=name, grid=(t // tr,), in_specs=in_specs, out_specs=out_specs, out_shape=out_shape,
        compiler_params=_params(("arbitrary",)),
    )(*rows, *vecs)
    return outs[:n_r], outs[n_r:]


def _rms(x, gain):
    return x * lax.rsqrt(jnp.mean(x * x, axis=-1, keepdims=True) + NORM_EPS) * gain


def _norm_fwd(x, gain):
    return (_rms(x, gain).astype(BF16),), ()


def _add_norm_fwd(h, a, gain):
    h = h + a
    return (h, _rms(h, gain).astype(BF16)), ()


def _relu2(pre):
    r = jnp.maximum(pre, 0.0)
    return pre, r * r


def _ple(h, gpre, pu):
    return h + pu * jax.nn.sigmoid(gpre)


def _ple_two_norms_fwd(h, gpre, pu, gain_a, gain_b):
    h = _ple(h, gpre, pu)
    return (h, _rms(h, gain_a).astype(BF16), _rms(h, gain_b).astype(BF16)), ()


def _tail_fwd_bwd(h, gpre, pu, target, gain):
    def row_loss(h, gpre, pu, gain):
        y = _rms(_ple(h, gpre, pu), gain)
        return 0.5 * jnp.mean(jnp.square(y - target), axis=-1, keepdims=True)

    loss, vjp = jax.vjp(row_loss, h, gpre, pu, gain)
    dh, dgpre, dpu, dgain = vjp(jnp.ones_like(loss))
    loss = jnp.broadcast_to(jnp.sum(loss, axis=0, keepdims=True), (1, 128))
    return (dh, dgpre.astype(BF16), dpu.astype(BF16)), (dgain, loss)


def _norm_bwd(h, du, dh_in, gain):
    _, vjp = jax.vjp(_rms, h, gain)
    dh, dgain = vjp(du.astype(F32))
    dh = dh_in + dh
    return (dh, dh.astype(BF16)), (dgain,)


def _two_norms_ple_bwd(h, du_a, du_b, dh_in, gpre, pu, gain_a, gain_b):
    _, vjp = jax.vjp(lambda h, ga, gb: (_rms(h, ga), _rms(h, gb)), h, gain_a, gain_b)
    dh, dga, dgb = vjp((du_a.astype(F32), du_b.astype(F32)))
    dh = dh_in + dh
    _, gate_vjp = jax.vjp(lambda g, u: u * jax.nn.sigmoid(g), gpre, pu)
    dgpre, dpu = gate_vjp(dh)
    return (dh, dgpre.astype(BF16), dpu.astype(BF16)), (dga, dgb)


def _relu2_bwd(dact, pre):
    return (dact * 2.0 * jnp.maximum(pre.astype(F32), 0.0),)


def _bf16_dot(dims_fwd, dims_da, dims_db, swap_da, swap_db):
    @jax.custom_vjp
    def dot(a, b):
        return lax.dot_general(a.astype(BF16), b.astype(BF16), dims_fwd, preferred_element_type=F32)

    def fwd(a, b):
        return dot(a, b), (a, b)

    def bwd(res, ct):
        a, b = res
        ct, a, b = ct.astype(BF16), a.astype(BF16), b.astype(BF16)
        da = lax.dot_general(*((b, ct) if swap_da else (ct, b)), dims_da, preferred_element_type=F32)
        db = lax.dot_general(*((ct, a) if swap_db else (a, ct)), dims_db, preferred_element_type=F32)
        return da, db

    dot.defvjp(fwd, bwd)
    return dot


_dot_nn = _bf16_dot(NN, NT, TN, False, False)
_dot_nt = _bf16_dot(NT, NN, TN, False, True)
_dot_tn = _bf16_dot(TN, NT, NN, True, False)


def _chunk_causal_mask():
    r = lax.broadcasted_iota(jnp.int32, (TILE, TILE), 0)
    c = lax.broadcasted_iota(jnp.int32, (TILE, TILE), 1)
    return ((r // CHUNK) == (c // CHUNK)) & (c <= r)


def _chunk_scan(x, reverse):
    pos = lax.broadcasted_iota(jnp.int32, x.shape, 0) % CHUNK
    step = 1
    while step < CHUNK:
        if reverse:
            x = x + jnp.where(pos < CHUNK - step, pltpu.roll(x, x.shape[0] - step, axis=0), 0.0)
        else:
            x = x + jnp.where(pos >= step, pltpu.roll(x, step, axis=0), 0.0)
        step *= 2
    return x


def _chunk_total(x):
    return _chunk_scan(x, False) + _chunk_scan(x, True) - x


@jax.custom_vjp
def _chunk_sums(x):
    return _chunk_scan(x, False), _chunk_total(x)


def _chunk_sums_fwd(x):
    return _chunk_sums(x), None


def _chunk_sums_bwd(_, ct):
    return (_chunk_scan(ct[0], True) + _chunk_total(ct[1]),)


_chunk_sums.defvjp(_chunk_sums_fwd, _chunk_sums_bwd)


def _hgrn_tile(q, f, i, g, lgt, hg, st):
    d = q.shape[1]
    l0, l1 = lgt[0:1], lgt[1:2]
    mx = jnp.maximum(l0, l1)
    e0, e1 = jnp.exp(l0 - mx), jnp.exp(l1 - mx)
    lb = e0 / (e0 + e1)
    fg = lb + (1.0 - lb) * jax.nn.sigmoid(f)
    k = 1.0 - fg
    causal = _chunk_causal_mask()
    b, b_last = _chunk_sums(jnp.log(fg))
    q_in = q * jax.nn.sigmoid(q) * (d ** -0.5) * jnp.exp(b)
    k_in = k * jnp.exp(-b)
    k_end = k * jnp.exp(b_last - b)
    att = jnp.where(causal, _dot_nt(q_in, k_in), 0.0)
    o_intra = _dot_nn(att, i)
    n_chunks = TILE // CHUNK
    chunk_of_row = lax.broadcasted_iota(jnp.int32, (TILE, 1), 0) // CHUNK

    def spread(a):
        return jnp.concatenate([jnp.where(chunk_of_row == n, a, 0.0) for n in range(n_chunks)], axis=1)

    increments = _dot_tn(i, spread(k_end))
    states = []
    for n in range(n_chunks):
        states.append(st)
        decay = jnp.exp(jnp.mean(b_last[n * CHUNK:(n + 1) * CHUNK], axis=0, keepdims=True))
        st = st * decay + increments[:, n * d:(n + 1) * d]
    o = o_intra + _dot_nt(spread(q_in), jnp.concatenate(states, axis=1))
    o = o * lax.rsqrt(jnp.mean(o * o, axis=-1, keepdims=True) + NORM_EPS) * hg
    return o * (g * jax.nn.sigmoid(g)), st


def hgrn_fwd(z, lgt, hg):
    t, d4 = z.shape
    d = d4 // 4
    nh, nt = d // HEAD_DIM, t // TILE
    hp = HEADS_PER_STEP
    wide = hp * HEAD_DIM

    def body(q_ref, f_ref, i_ref, g_ref, lgt_ref, hg_ref, o_ref, st_out_ref, st_ref):
        tt = pl.program_id(1)

        @pl.when(tt == 0)
        def _():
            st_ref[...] = jnp.zeros_like(st_ref)

        for hh in range(hp):
            cols = slice(hh * HEAD_DIM, (hh + 1) * HEAD_DIM)
            st = st_ref[hh]
            st_out_ref[hh] = st
            o, st = _hgrn_tile(q_ref[:, cols], f_ref[:, cols], i_ref[:, cols], g_ref[:, cols], lgt_ref[:, cols],
                               hg_ref[...], st)
            o_ref[:, cols] = o.astype(o_ref.dtype)
            st_ref[hh] = st

    def part(p):
        return pl.BlockSpec((TILE, wide), lambda h, tt: (tt, p * (nh // hp) + h))

    return pl.pallas_call(
        body, name="hgrn_fwd", grid=(nh // hp, nt),
        in_specs=[part(0), part(1), part(2), part(3),
                  pl.BlockSpec((2, wide), lambda h, tt: (0, h)),
                  pl.BlockSpec((1, HEAD_DIM), lambda h, tt: (0, 0))],
        out_specs=[pl.BlockSpec((TILE, wide), lambda h, tt: (tt, h)),
                   pl.BlockSpec((hp, None, HEAD_DIM, HEAD_DIM), lambda h, tt: (h, tt, 0, 0))],
        out_shape=[jax.ShapeDtypeStruct((t, d), BF16),
                   jax.ShapeDtypeStruct((nh, nt, HEAD_DIM, HEAD_DIM), F32)],
        scratch_shapes=[pltpu.VMEM((hp, HEAD_DIM, HEAD_DIM), F32)],
        compiler_params=_params(("parallel", "arbitrary")),
    )(z, z, z, z, lgt, hg)


def hgrn_bwd(z, lgt, hg, states, dout):
    t, d4 = z.shape
    d = d4 // 4
    nh, nt = d // HEAD_DIM, t // TILE
    hp = HEADS_PER_STEP
    wide = hp * HEAD_DIM

    def body(q_ref, f_ref, i_ref, g_ref, lgt_ref, hg_ref, st_in_ref, do_ref, dz_ref, dlgt_ref, dhg_ref, dst_ref):
        h, tt = pl.program_id(0), pl.program_id(1)

        @pl.when(tt == 0)
        def _():
            dst_ref[...] = jnp.zeros_like(dst_ref)
            dlgt_ref[...] = jnp.zeros_like(dlgt_ref)

        @pl.when((tt == 0) & (h == 0))
        def _():
            dhg_ref[...] = jnp.zeros_like(dhg_ref)

        for hh in range(hp):
            cols = slice(hh * HEAD_DIM, (hh + 1) * HEAD_DIM)
            _, vjp = jax.vjp(_hgrn_tile, q_ref[:, cols], f_ref[:, cols], i_ref[:, cols], g_ref[:, cols],
                             lgt_ref[:, cols], hg_ref[...], st_in_ref[hh])
            grads = vjp((do_ref[:, cols].astype(F32), dst_ref[hh]))
            for p in range(4):
                dz_ref[p, :, cols] = grads[p].astype(dz_ref.dtype)
            dlgt_ref[:, cols] += grads[4]
            dhg_ref[...] += grads[5]
            dst_ref[hh] = grads[6]

    def part(p):
        return pl.BlockSpec((TILE, wide), lambda h, tt: (nt - 1 - tt, p * (nh // hp) + h))

    return pl.pallas_call(
        body, name="hgrn_bwd", grid=(nh // hp, nt),
        in_specs=[part(0), part(1), part(2), part(3),
                  pl.BlockSpec((2, wide), lambda h, tt: (0, h)),
                  pl.BlockSpec((1, HEAD_DIM), lambda h, tt: (0, 0)),
                  pl.BlockSpec((hp, None, HEAD_DIM, HEAD_DIM), lambda h, tt: (h, nt - 1 - tt, 0, 0)),
                  pl.BlockSpec((TILE, wide), lambda h, tt: (nt - 1 - tt, h))],
        out_specs=[pl.BlockSpec((4, TILE, wide), lambda h, tt: (0, nt - 1 - tt, h)),
                   pl.BlockSpec((2, wide), lambda h, tt: (0, h)),
                   pl.BlockSpec((1, HEAD_DIM), lambda h, tt: (0, 0))],
        out_shape=[jax.ShapeDtypeStruct((4, t, d), BF16),
                   jax.ShapeDtypeStruct((2, d), F32),
                   jax.ShapeDtypeStruct((1, HEAD_DIM), F32)],
        scratch_shapes=[pltpu.VMEM((hp, HEAD_DIM, HEAD_DIM), F32)],
        compiler_params=_params(("arbitrary", "arbitrary")),
    )(z, z, z, z, lgt, hg, states, dout)


def _log_sigmoid(x):
    return jnp.minimum(x, 0.0) - jnp.log(1.0 + jnp.exp(-jnp.abs(x)))


def decay_fwd(fl_t, b_f):
    nh, t = fl_t.shape

    def body(fl_ref, b_ref, out_ref):
        r = lax.broadcasted_iota(jnp.int32, (128, 128), 0)
        c = lax.broadcasted_iota(jnp.int32, (128, 128), 1)
        upper = (r <= c).astype(F32)
        carry = jnp.zeros((nh, 1), F32)
        for j in range(t // 128):
            cols = slice(j * 128, (j + 1) * 128)
            ls = _log_sigmoid(fl_ref[:, cols] + b_ref[...])
            out_ref[:, cols] = carry + jnp.dot(ls, upper, precision=lax.Precision.HIGHEST,
                                               preferred_element_type=F32)
            carry = carry + jnp.sum(ls, axis=1, keepdims=True)

    return pl.pallas_call(body, name="decay_fwd", out_shape=jax.ShapeDtypeStruct((nh, t), F32),
                          compiler_params=_params(None))(fl_t, b_f)


def decay_bwd(fl_t, b_f, ddcum):
    nh, t = fl_t.shape

    def body(fl_ref, b_ref, dd_ref, dfl_ref, db_ref):
        r = lax.broadcasted_iota(jnp.int32, (128, 128), 0)
        c = lax.broadcasted_iota(jnp.int32, (128, 128), 1)
        lower = (r >= c).astype(F32)
        carry = jnp.zeros((nh, 1), F32)
        db = jnp.zeros((nh, 1), F32)
        for j in reversed(range(t // 128)):
            cols = slice(j * 128, (j + 1) * 128)
            dd = dd_ref[:, cols]
            dls = carry + jnp.dot(dd, lower, precision=lax.Precision.HIGHEST, preferred_element_type=F32)
            carry = carry + jnp.sum(dd, axis=1, keepdims=True)
            dfl = dls * jax.nn.sigmoid(-(fl_ref[:, cols] + b_ref[...]))
            dfl_ref[:, cols] = dfl
            db = db + jnp.sum(dfl, axis=1, keepdims=True)
        db_ref[...] = db

    return pl.pallas_call(body, name="decay_bwd",
                          out_shape=[jax.ShapeDtypeStruct((nh, t), F32), jax.ShapeDtypeStruct((nh, 1), F32)],
                          compiler_params=_params(None))(fl_t, b_f, ddcum)


def _attn_parts(t):
    tq = _tile(t, (256, 128))
    per_part = 2 if t // tq >= 4 else 1
    return tq, [(first, per_part, (first + per_part) * tq) for first in range(0, t // tq, per_part)]


def _attn_logits(q_ref, k_ref, dcol_ref, drow_ref, row0, tq, keys):
    qs = (q_ref[...] * (HEAD_DIM ** -0.5)).astype(BF16)
    s = lax.dot_general(qs, k_ref[...], NT, preferred_element_type=F32)
    s = s + dcol_ref[...] - drow_ref[...]
    row = row0 + lax.broadcasted_iota(jnp.int32, (tq, keys), 0)
    col = lax.broadcasted_iota(jnp.int32, (tq, keys), 1)
    return qs, jnp.where(col <= row, s, NEG_BIG)


def attn_fwd(q, kv, dcol, drow):
    t, d = q.shape
    nh = d // HEAD_DIM
    tq, parts = _attn_parts(t)
    o = lse = None
    for first, count, keys in parts:
        def body(q_ref, k_ref, v_ref, dcol_ref, drow_ref, *rest, first=first, keys=keys):
            o_ref, lse_ref = rest[-2:]
            _, s = _attn_logits(q_ref, k_ref, dcol_ref, drow_ref, (first + pl.program_id(1)) * tq, tq, keys)
            m = jnp.max(s, axis=1, keepdims=True)
            p = jnp.exp(s - m)
            l = jnp.sum(p, axis=1, keepdims=True)
            acc = jnp.dot(p.astype(BF16), v_ref[...], preferred_element_type=F32)
            o_ref[...] = (acc / l).astype(o_ref.dtype)
            lse_ref[...] = m + jnp.log(l)

        tile = pl.BlockSpec((tq, HEAD_DIM), lambda h, i, first=first: (first + i, h))
        col = pl.BlockSpec((None, tq, 1), lambda h, i, first=first: (h, first + i, 0))
        seen_k = pl.BlockSpec((keys, HEAD_DIM), lambda h, i: (0, h))
        seen_v = pl.BlockSpec((keys, HEAD_DIM), lambda h, i: (0, nh + h))
        carried = [] if o is None else [o, lse]
        o, lse = pl.pallas_call(
            body, name=f"attn_fwd_{first}", grid=(nh, count),
            in_specs=[tile, seen_k, seen_v, col, pl.BlockSpec((None, 1, keys), lambda h, i: (h, 0, 0))]
            + [pl.BlockSpec(memory_space=pl.ANY)] * len(carried),
            out_specs=[tile, col],
            out_shape=[jax.ShapeDtypeStruct((t, d), BF16), jax.ShapeDtypeStruct((nh, t, 1), F32)],
            input_output_aliases={5: 0, 6: 1} if carried else {},
            compiler_params=_params(("parallel", "parallel")),
        )(q, kv, kv, dcol, drow, *carried)
    return o, lse


def attn_bwd(q, kv, dcol, drow, lse, do):
    t, d = q.shape
    nh = d // HEAD_DIM
    tq, parts = _attn_parts(t)
    dq = dk = dv = ddrow = None
    for first, count, keys in reversed(parts):
        first_call = dq is None

        def body(q_ref, k_ref, v_ref, dcol_ref, drow_ref, lse_ref, do_ref, *rest, first=first, keys=keys,
                 count=count, first_call=first_call):
            dq_ref, dk_ref, dv_ref, ddrow_ref, dk_acc, dv_acc, ddrow_acc = rest[-7:]
            i = pl.program_id(1)

            @pl.when(i == 0)
            def _():
                if first_call:
                    dk_acc[...] = jnp.zeros_like(dk_acc)
                    dv_acc[...] = jnp.zeros_like(dv_acc)
                    ddrow_acc[...] = jnp.zeros_like(ddrow_acc)
                else:
                    dk_acc[...] = rest[1][...]
                    dv_acc[...] = rest[2][...]
                    ddrow_acc[...] = rest[3][...]

            qs, s = _attn_logits(q_ref, k_ref, dcol_ref, drow_ref, (first + i) * tq, tq, keys)
            p = jnp.exp(s - lse_ref[...])
            do = do_ref[...]
            dp = lax.dot_general(do, v_ref[...], NT, preferred_element_type=F32)
            ds = p * (dp - jnp.sum(p * dp, axis=1, keepdims=True))
            dsb = ds.astype(BF16)
            dq_ref[...] = (jnp.dot(dsb, k_ref[...], preferred_element_type=F32) * (HEAD_DIM ** -0.5)).astype(dq_ref.dtype)
            dk_acc[...] += lax.dot_general(dsb, qs, TN, preferred_element_type=F32)
            dv_acc[...] += lax.dot_general(p.astype(BF16), do, TN, preferred_element_type=F32)
            ddrow_acc[...] -= jnp.sum(ds, axis=0, keepdims=True)

            @pl.when(i == count - 1)
            def _():
                dk_ref[...] = dk_acc[...]
                dv_ref[...] = dv_acc[...]
                ddrow_ref[...] = ddrow_acc[...]

        tile = pl.BlockSpec((tq, HEAD_DIM), lambda h, i, first=first: (first + i, h))
        col = pl.BlockSpec((None, tq, 1), lambda h, i, first=first: (h, first + i, 0))
        seen = pl.BlockSpec((keys, HEAD_DIM), lambda h, i: (0, h))
        seen_v = pl.BlockSpec((keys, HEAD_DIM), lambda h, i: (0, nh + h))
        seen_row = pl.BlockSpec((None, 1, keys), lambda h, i: (h, 0, 0))
        carried = [] if first_call else [dq, dk, dv, ddrow]
        carried_specs = [] if first_call else [pl.BlockSpec(memory_space=pl.ANY), seen, seen, seen_row]
        dq, dk, dv, ddrow = pl.pallas_call(
            body, name=f"attn_bwd_{first}", grid=(nh, count),
            in_specs=[tile, seen, seen_v, col, seen_row, col, tile] + carried_specs,
            out_specs=[tile, seen, seen, seen_row],
            out_shape=[jax.ShapeDtypeStruct((t, d), BF16), jax.ShapeDtypeStruct((t, d), F32),
                       jax.ShapeDtypeStruct((t, d), F32), jax.ShapeDtypeStruct((nh, 1, t), F32)],
            scratch_shapes=[pltpu.VMEM((keys, HEAD_DIM), F32), pltpu.VMEM((keys, HEAD_DIM), F32),
                            pltpu.VMEM((1, keys), F32)],
            input_output_aliases={} if first_call else {7: 0, 8: 1, 9: 2, 10: 3},
            compiler_params=_params(("parallel", "arbitrary")),
        )(q, kv, kv, dcol, drow, lse, do, *carried)
    return dq, dk, dv, ddrow


def _my_index():
    return (lax.axis_index("x") * 2 + lax.axis_index("y")) * 2 + lax.axis_index("c")


def all_gather(name, src, deps=()):
    def body(src_ref, *rest):
        out_ref, send_sems, recv_sems, local_sem = rest[len(deps):]
        x, y, c = (lax.axis_index(a) for a in MESH_AXES)
        me = (x * 2 + y) * 2 + c
        local = pltpu.make_async_copy(src_ref, out_ref.at[me], local_sem)
        local.start()
        copies = []
        for dlt in range(1, N_DEV):
            copies.append(pltpu.make_async_remote_copy(
                src_ref=src_ref, dst_ref=out_ref.at[me], send_sem=send_sems.at[dlt - 1],
                recv_sem=recv_sems.at[dlt - 1], device_id=(x ^ (dlt // 4), y ^ ((dlt // 2) % 2), c ^ (dlt % 2)),
                device_id_type=pl.DeviceIdType.MESH))
        for cp in copies:
            cp.start()
        for cp in copies:
            cp.wait_recv()
        for cp in copies:
            cp.wait_send()
        local.wait()

    return pl.pallas_call(
        body, name=name, out_shape=jax.ShapeDtypeStruct((N_DEV,) + tuple(src.shape), src.dtype),
        in_specs=[pl.BlockSpec(memory_space=pl.ANY)] * (1 + len(deps)), out_specs=pl.BlockSpec(memory_space=pl.ANY),
        scratch_shapes=[pltpu.SemaphoreType.DMA((N_DEV - 1,)), pltpu.SemaphoreType.DMA((N_DEV - 1,)),
                        pltpu.SemaphoreType.DMA],
        compiler_params=pltpu.CompilerParams(has_side_effects=True),
    )(src, *deps)


_HBM = pl.BlockSpec(memory_space=pltpu.HBM)
_SEM = pl.BlockSpec(memory_space=pltpu.SEMAPHORE)
_DATAFLOW = pltpu.SideEffectType.DATAFLOW_SIDE_EFFECTING


def _peer_copies(src_ref, land_ref, send_sems, recv_sems):
    x, y, c = (lax.axis_index(a) for a in MESH_AXES)
    me = (x * 2 + y) * 2 + c
    copies = []
    for dlt in range(1, N_DEV):
        px, py, pc = x ^ (dlt // 4), y ^ ((dlt // 2) % 2), c ^ (dlt % 2)
        peer = (px * 2 + py) * 2 + pc
        copies.append(pltpu.make_async_remote_copy(
            src_ref=src_ref.at[peer], dst_ref=land_ref.at[me],
            send_sem=send_sems.at[dlt - 1], recv_sem=recv_sems.at[dlt - 1],
            device_id=(px, py, pc), device_id_type=pl.DeviceIdType.MESH))
    return copies


def _own_copy(src_ref, land_ref, send_sems):
    me = (lax.axis_index("x") * 2 + lax.axis_index("y")) * 2 + lax.axis_index("c")
    return pltpu.make_async_copy(src_ref.at[me], land_ref.at[me], send_sems.at[N_DEV - 1])


def scatter_start(name, srcs):
    n = len(srcs)
    lands = [lax.empty(s.shape, s.dtype) for s in srcs]

    def body(*refs):
        src_refs, land_refs = refs[:n], refs[n:2 * n]
        send_sems, recv_sems = refs[2 * n:3 * n], refs[3 * n:4 * n]
        token = refs[-1]
        for j in range(n):
            for cp in _peer_copies(src_refs[j], land_refs[j], send_sems[j], recv_sems[j]):
                cp.start()
            _own_copy(src_refs[j], land_refs[j], send_sems[j]).start()
        token[...] = jnp.zeros_like(token)

    sems = [pltpu.SemaphoreType.DMA((N_DEV,))] * n + [pltpu.SemaphoreType.DMA((N_DEV - 1,))] * n
    thru = [pltpu.HBM(a.shape, a.dtype) for a in list(srcs) + lands]
    outs = pl.pallas_call(
        body, name=name, out_shape=tuple(sems + thru + [jax.ShapeDtypeStruct((8, 128), F32)]),
        in_specs=[_HBM] * (2 * n), out_specs=tuple([_SEM] * (2 * n) + [_HBM] * (2 * n) + [pl.BlockSpec(memory_space=pltpu.VMEM)]),
        input_output_aliases={j: 2 * n + j for j in range(2 * n)},
        compiler_params=pltpu.CompilerParams(has_side_effects=_DATAFLOW),
    )(*[pltpu.with_memory_space_constraint(a, pltpu.HBM) for a in list(srcs) + lands])
    handles = [(outs[j], outs[n + j], outs[2 * n + j], outs[3 * n + j]) for j in range(n)]
    return handles, outs[-1]


def scatter_wait(name, handle, after):
    send_sems, recv_sems, src, land = handle

    def body(src_ref, land_ref, send_ref, recv_ref, after_ref, src_out, land_out):
        for cp in _peer_copies(src_ref, land_ref, send_ref, recv_ref):
            cp.wait_send()
            cp.wait_recv()
        _own_copy(src_ref, land_ref, send_ref).wait()

    return pl.pallas_call(
        body, name=name, out_shape=(pltpu.HBM(src.shape, src.dtype), pltpu.HBM(land.shape, land.dtype)),
        in_specs=[_HBM, _HBM, _SEM, _SEM, pl.BlockSpec(memory_space=pl.ANY)], out_specs=(_HBM, _HBM),
        input_output_aliases={0: 0, 1: 1},
        compiler_params=pltpu.CompilerParams(has_side_effects=_DATAFLOW),
    )(src, land, send_sems, recv_sems, after)


N_OTHER_CHIPS = 3


def _two_level_places():
    x, y, c = (lax.axis_index(a) for a in MESH_AXES)
    return (x, y, c), (x * 2 + y) * 2 + c, (x, y, 1 - c), [(1 - x, y), (x, 1 - y), (1 - x, 1 - y)]


def _first_copies(land_ref, send_sems, recv_sems):
    (x, y, c), me, other_core, chips = _two_level_places()
    targets = [other_core] + [(cx, cy, c) for cx, cy in chips]
    return [pltpu.make_async_remote_copy(
        src_ref=land_ref.at[me], dst_ref=land_ref.at[me], send_sem=send_sems.at[k], recv_sem=recv_sems.at[k],
        device_id=to, device_id_type=pl.DeviceIdType.MESH) for k, to in enumerate(targets)]


def _passed_on_copies(land_ref, send_sems, recv_sems):
    (x, y, c), me, other_core, chips = _two_level_places()
    copies = []
    for k, (cx, cy) in enumerate(chips):
        slot = land_ref.at[(cx * 2 + cy) * 2 + c]
        copies.append(pltpu.make_async_remote_copy(
            src_ref=slot, dst_ref=slot, send_sem=send_sems.at[k], recv_sem=recv_sems.at[k],
            device_id=other_core, device_id_type=pl.DeviceIdType.MESH))
    return copies


def gather_start(name, lands):
    n = len(lands)

    def body(*refs):
        land_refs, send_sems, recv_sems = refs[:n], refs[n:2 * n], refs[2 * n:3 * n]
        for j in range(n):
            for cp in _first_copies(land_refs[j], send_sems[j], recv_sems[j]):
                cp.start()

    sems = [pltpu.SemaphoreType.DMA((1 + N_OTHER_CHIPS,))] * (2 * n)
    outs = pl.pallas_call(
        body, name=name, out_shape=tuple(sems + [pltpu.HBM(a.shape, a.dtype) for a in lands]),
        in_specs=[_HBM] * n, out_specs=tuple([_SEM] * (2 * n) + [_HBM] * n),
        input_output_aliases={j: 2 * n + j for j in range(n)},
        compiler_params=pltpu.CompilerParams(has_side_effects=_DATAFLOW),
    )(*[pltpu.with_memory_space_constraint(a, pltpu.HBM) for a in lands])
    return [[outs[j], outs[n + j], outs[2 * n + j]] for j in range(n)]


def gather_pass_on(name, handle, after):
    send_sems, recv_sems, land = handle

    def body(land_ref, recv_ref, after_ref, land_out, send2, recv2, token):
        arrivals = _first_copies(land_ref, recv_ref, recv_ref)
        for k, cp in enumerate(_passed_on_copies(land_ref, send2, recv2)):
            arrivals[1 + k].wait_recv()
            cp.start()
        token[...] = jnp.zeros_like(token)

    sem3 = pltpu.SemaphoreType.DMA((N_OTHER_CHIPS,))
    land, send2, recv2, token = pl.pallas_call(
        body, name=name,
        out_shape=(pltpu.HBM(land.shape, land.dtype), sem3, sem3, jax.ShapeDtypeStruct((8, 128), F32)),
        in_specs=[_HBM, _SEM, pl.BlockSpec(memory_space=pl.ANY)],
        out_specs=(_HBM, _SEM, _SEM, pl.BlockSpec(memory_space=pltpu.VMEM)),
        input_output_aliases={0: 0}, compiler_params=pltpu.CompilerParams(has_side_effects=_DATAFLOW),
    )(land, recv_sems, after)
    return [send_sems, recv_sems, land, send2, recv2], token


def gather_wait(name, handle, after):
    send_sems, recv_sems, land, send2, recv2 = handle

    def body(land_ref, send_ref, recv_ref, send2_ref, recv2_ref, after_ref, land_out):
        first = _first_copies(land_ref, send_ref, recv_ref)
        for cp in first:
            cp.wait_send()
        first[0].wait_recv()
        for cp in _passed_on_copies(land_ref, send2_ref, recv2_ref):
            cp.wait_send()
            cp.wait_recv()

    return pl.pallas_call(
        body, name=name, out_shape=pltpu.HBM(land.shape, land.dtype),
        in_specs=[_HBM, _SEM, _SEM, _SEM, _SEM, pl.BlockSpec(memory_space=pl.ANY)], out_specs=_HBM,
        input_output_aliases={0: 0}, compiler_params=pltpu.CompilerParams(has_side_effects=_DATAFLOW),
    )(land, send_sems, recv_sems, send2, recv2, after)


N_CHIPS = 4


def _pair_copies(g_ref, half_ref, send_sems, recv_sems):
    (x, y, c), me, other_core, chips = _two_level_places()
    return [pltpu.make_async_remote_copy(
        src_ref=g_ref.at[chip * 2 + (1 - c)], dst_ref=half_ref.at[chip], send_sem=send_sems.at[chip],
        recv_sem=recv_sems.at[chip], device_id=other_core, device_id_type=pl.DeviceIdType.MESH)
        for chip in range(N_CHIPS)]


def pair_start(name, g):
    half = lax.empty((N_CHIPS,) + g.shape[1:], g.dtype)

    def body(g_ref, half_ref, send_sems, recv_sems, g_out, half_out, token):
        for cp in _pair_copies(g_ref, half_ref, send_sems, recv_sems):
            cp.start()
        token[...] = jnp.zeros_like(token)

    sem = pltpu.SemaphoreType.DMA((N_CHIPS,))
    outs = pl.pallas_call(
        body, name=name,
        out_shape=(sem, sem, pltpu.HBM(g.shape, g.dtype), pltpu.HBM(half.shape, half.dtype),
                   jax.ShapeDtypeStruct((8, 128), F32)),
        in_specs=[_HBM, _HBM], out_specs=(_SEM, _SEM, _HBM, _HBM, pl.BlockSpec(memory_space=pltpu.VMEM)),
        input_output_aliases={0: 2, 1: 3}, compiler_params=pltpu.CompilerParams(has_side_effects=_DATAFLOW),
    )(pltpu.with_memory_space_constraint(g, pltpu.HBM), half)
    return list(outs[:4]), outs[4]


def pair_wait(name, handle, after):
    send_sems, recv_sems, g, half = handle

    def body(g_ref, half_ref, send_ref, recv_ref, after_ref, g_out, half_out):
        for cp in _pair_copies(g_ref, half_ref, send_ref, recv_ref):
            cp.wait_send()
            cp.wait_recv()

    return pl.pallas_call(
        body, name=name, out_shape=(pltpu.HBM(g.shape, g.dtype), pltpu.HBM(half.shape, half.dtype)),
        in_specs=[_HBM, _HBM, _SEM, _SEM, pl.BlockSpec(memory_space=pl.ANY)], out_specs=(_HBM, _HBM),
        input_output_aliases={0: 0, 1: 1}, compiler_params=pltpu.CompilerParams(has_side_effects=_DATAFLOW),
    )(g, half, send_sems, recv_sems, after)


def pair_sum(name, g, half):
    _, r, wd = g.shape
    tr = _row_tile(r, 2 * ROW_TILE_BYTES // (4 * wd))
    kind = lax.axis_index("c").astype(jnp.int32).reshape(1)

    def body(kind_ref, g_ref, half_ref, o_ref):
        o_ref[...] = (g_ref[...].astype(F32) + half_ref[...].astype(F32)).astype(o_ref.dtype)

    spec = pl.BlockSpec((None, tr, wd), lambda chip, i, kind_ref: (chip, i, 0))
    return pl.pallas_call(
        body, name=name,
        grid_spec=pltpu.PrefetchScalarGridSpec(
            num_scalar_prefetch=1, grid=(N_CHIPS, r // tr),
            in_specs=[pl.BlockSpec((None, tr, wd), lambda chip, i, kind_ref: (chip * 2 + kind_ref[0], i, 0)), spec],
            out_specs=spec),
        out_shape=jax.ShapeDtypeStruct((N_CHIPS, r, wd), g.dtype),
        compiler_params=_params(("parallel", "parallel")),
    )(kind, g, half)


def _chip_copies(sums_ref, land_ref, send_sems, recv_sems):
    (x, y, c), me, other_core, chips = _two_level_places()
    return [pltpu.make_async_remote_copy(
        src_ref=sums_ref.at[cx * 2 + cy], dst_ref=land_ref.at[x * 2 + y], send_sem=send_sems.at[k],
        recv_sem=recv_sems.at[k], device_id=(cx, cy, c), device_id_type=pl.DeviceIdType.MESH)
        for k, (cx, cy) in enumerate(chips)]


def chip_start(name, sums):
    land = lax.empty(sums.shape, sums.dtype)

    def body(sums_ref, land_ref, send_sems, recv_sems, sums_out, land_out, token):
        for cp in _chip_copies(sums_ref, land_ref, send_sems, recv_sems):
            cp.start()
        token[...] = jnp.zeros_like(token)

    sem = pltpu.SemaphoreType.DMA((N_OTHER_CHIPS,))
    outs = pl.pallas_call(
        body, name=name,
        out_shape=(sem, sem, pltpu.HBM(sums.shape, sums.dtype), pltpu.HBM(land.shape, land.dtype),
                   jax.ShapeDtypeStruct((8, 128), F32)),
        in_specs=[_HBM, _HBM], out_specs=(_SEM, _SEM, _HBM, _HBM, pl.BlockSpec(memory_space=pltpu.VMEM)),
        input_output_aliases={0: 2, 1: 3}, compiler_params=pltpu.CompilerParams(has_side_effects=_DATAFLOW),
    )(pltpu.with_memory_space_constraint(sums, pltpu.HBM), land)
    return list(outs[:4]), outs[4]


def chip_wait(name, handle, after):
    send_sems, recv_sems, sums, land = handle

    def body(sums_ref, land_ref, send_ref, recv_ref, after_ref, sums_out, land_out):
        for cp in _chip_copies(sums_ref, land_ref, send_ref, recv_ref):
            cp.wait_send()
            cp.wait_recv()

    return pl.pallas_call(
        body, name=name, out_shape=(pltpu.HBM(sums.shape, sums.dtype), pltpu.HBM(land.shape, land.dtype)),
        in_specs=[_HBM, _HBM, _SEM, _SEM, pl.BlockSpec(memory_space=pl.ANY)], out_specs=(_HBM, _HBM),
        input_output_aliases={0: 0, 1: 1}, compiler_params=pltpu.CompilerParams(has_side_effects=_DATAFLOW),
    )(sums, land, send_sems, recv_sems, after)


def adamw_reduce(name, parts, w, m, v):
    nl, r, wd = w.shape
    tr = _row_tile(r, ROW_TILE_BYTES // (8 * wd))

    def body(*refs):
        p_refs = refs[:nl]
        w_ref, m_ref, v_ref, g_ref, d_ref, nm_ref, nv_ref = refs[nl:]
        layer = pl.program_id(0)
        for j in range(nl):
            @pl.when(layer == j)
            def _(j=j):
                g = p_refs[j][0].astype(F32)
                for sender in range(1, p_refs[j].shape[0]):
                    g = g + p_refs[j][sender].astype(F32)
                nm = B1 * m_ref[...] + (1.0 - B1) * g
                nv = B2 * v_ref[...] + (1.0 - B2) * jnp.square(g)
                m_hat = nm / (1.0 - B1 ** STEP)
                v_hat = nv / (1.0 - B2 ** STEP)
                g_ref[...] = g
                d_ref[...] = -LR * (m_hat / (jnp.sqrt(v_hat) + ADAM_EPS) + WD * w_ref[...])
                nm_ref[...] = nm
                nv_ref[...] = nv

    def part_spec(j):
        return pl.BlockSpec((parts[j].shape[0], tr, wd), lambda l, i: (0, jnp.where(l == j, i, 0), 0))

    spec = pl.BlockSpec((None, tr, wd), lambda l, i: (l, i, 0))
    return pl.pallas_call(
        body, name=name, grid=(nl, r // tr),
        in_specs=[part_spec(j) for j in range(nl)] + [spec, spec, spec],
        out_specs=[spec] * 4, out_shape=[jax.ShapeDtypeStruct((nl, r, wd), F32)] * 4,
        compiler_params=_params(("arbitrary", "arbitrary")),
    )(*parts, w, m, v)


def _pack_rows(vectors, rows=None):
    flat = jnp.concatenate([a.reshape(-1).astype(F32) for a in vectors])
    n = flat.shape[0]
    if rows is None:
        rows = -(-n // 1024) * 8
    return jnp.pad(flat, (0, rows * 128 - n)).reshape(rows, 128)


def _unpack_rows(packed, like):
    flat = packed.reshape(-1)
    out, pos = [], 0
    for a in like:
        out.append(flat[pos:pos + a.size].reshape(a.shape))
        pos += a.size
    return out


def kernel(x, p, mix_norm, mlp_norm, ple_norm, w_a_in, a_lb_logits, a_head_gain, w_a_out, kv_norm, w_kvf, b_f, w_b_q, w_b_out, w_mlp_up, w_mlp_down, w_ple_gate, w_ple_up, final_norm, loss_target, m_mix_norm, m_mlp_norm, m_ple_norm, m_w_a_in, m_a_lb_logits, m_a_head_gain, m_w_a_out, m_kv_norm, m_w_kvf, m_b_f, m_w_b_q, m_w_b_out, m_w_mlp_up, m_w_mlp_down, m_w_ple_gate, m_w_ple_up, m_final_norm, v_mix_norm, v_mlp_norm, v_ple_norm, v_w_a_in, v_a_lb_logits, v_a_head_gain, v_w_a_out, v_kv_norm, v_w_kvf, v_b_f, v_w_b_q, v_w_b_out, v_w_mlp_up, v_w_mlp_down, v_w_ple_gate, v_w_ple_up, v_final_norm):
    t, d = x.shape[1], x.shape[2]
    nh = d // HEAD_DIM
    n_layers = 2
    x2 = x.reshape(t, d)
    target = loss_target.reshape(t, d)
    me = _my_index()

    shards = {"w_a_in": w_a_in[0], "w_a_out": w_a_out[0], "w_kvf": w_kvf, "w_b_q": w_b_q[0], "w_b_out": w_b_out[0]}
    for l in range(n_layers):
        shards.update({f"w_mlp_up{l}": w_mlp_up[l], f"w_mlp_down{l}": w_mlp_down[l],
                       f"w_ple_gate{l}": w_ple_gate[l], f"w_ple_up{l}": w_ple_up[l]})
    first_use = ["a_lb_logits", "w_a_in", "w_a_out", "w_mlp_up0", "w_mlp_down0", "w_ple_gate0", "w_ple_up0", "w_kvf",
                 "w_b_q", "w_b_out", "w_mlp_up1", "w_mlp_down1", "w_ple_gate1", "w_ple_up1"]
    row_sharded = ("w_a_out", "w_b_q", "w_b_out", "w_mlp_down", "w_ple_gate")
    shards_bf = [a_lb_logits] + [shards[n].astype(BF16) for n in first_use[1:]]
    ag_handles = gather_start("ag_start", [
        lax.dynamic_update_slice(lax.empty((N_DEV,) + a.shape, a.dtype), a[None], (me, 0, 0)) for a in shards_bf])
    passed_on = {}
    weights = {}

    def pass_on(j, after):
        if j < len(first_use) and j not in passed_on:
            passed_on[j] = gather_pass_on("ag_pass_" + first_use[j], ag_handles[j], after)

    def weight(name, after=None):
        if name not in weights:
            j = first_use.index(name)
            pass_on(j, after)
            pass_on(j + 1, after)
            behind = passed_on[j + 1][1] if j + 1 in passed_on else after
            g = gather_wait("ag_wait_" + name, passed_on[j][0], behind)
            if name.rstrip("01") in row_sharded:
                g = g.reshape(1, g.shape[0] * g.shape[1], g.shape[2])
            weights[name] = g
        return weights[name]

    lgt = weight("a_lb_logits", x2).transpose(1, 0, 2).reshape(2, d)
    p_bf = [p[l, 0].astype(BF16) for l in range(n_layers)]

    def row(vec):
        return vec.reshape(1, -1)

    def mlp_ple_fwd(l, h_in, a):
        (h_a, u_mlp), _ = rowwise(f"add_norm_mlp{l}", _add_norm_fwd, [h_in, a], [row(mlp_norm[l])])
        pre, act = mm_nn(f"mlp_up{l}", u_mlp, weight(f"w_mlp_up{l}", u_mlp), fuse=(_relu2, (), (BF16, BF16)))
        mo = mm_nn(f"mlp_down{l}", act, weight(f"w_mlp_down{l}", act))
        (h_b, u_ple), _ = rowwise(f"add_norm_ple{l}", _add_norm_fwd, [h_a, mo], [row(ple_norm[l])])
        gpre = mm_nn(f"ple_gate{l}", u_ple, weight(f"w_ple_gate{l}", u_ple))
        pu = mm_nn(f"ple_up{l}", p_bf[l], weight(f"w_ple_up{l}", gpre))
        return dict(h_a=h_a, u_mlp=u_mlp, pre=pre, act=act, h_b=h_b, u_ple=u_ple, gpre=gpre, pu=pu)

    (u0,), _ = rowwise("norm_mix0", _norm_fwd, [x2], [row(mix_norm[0])])
    z = mm_nn("a_in", u0, weight("w_a_in", u0))
    og, states = hgrn_fwd(z, lgt, a_head_gain)
    a0 = mm_nn("a_out", og, weight("w_a_out", og))
    s0 = mlp_ple_fwd(0, x2, a0)
    (h3, u_kv, u1), _ = rowwise("ple_norms", _ple_two_norms_fwd, [s0["h_b"], s0["gpre"], s0["pu"]],
                                [row(kv_norm), row(mix_norm[1])])
    w_kvf_cols = weight("w_kvf", u_kv).transpose(1, 0, 2).reshape(d, 2 * d + nh)
    w_kv = w_kvf_cols[:, :2 * d].reshape(d, 2, d).transpose(1, 0, 2)
    w_f = jnp.pad(w_kvf_cols[:, 2 * d:], ((0, 0), (0, HEAD_DIM - nh)))[None]
    kv = mm_nn("kvf", u_kv, w_kv, out_dtype=BF16)
    fl_t = mm_nn("kvf_forget", u_kv, w_f)[:, :nh].T
    b_f_col = b_f.reshape(nh, 1)
    dcum = decay_fwd(fl_t, b_f_col)
    dcol, drow = dcum.reshape(nh, t, 1), dcum.reshape(nh, 1, t)
    q = mm_nn("b_q", u1, weight("w_b_q", dcum))
    o, lse = attn_fwd(q, kv, dcol, drow)
    a1 = mm_nn("b_out", o, weight("w_b_out", o))
    s1 = mlp_ple_fwd(1, h3, a1)

    (dh, dgpre, dpu), (d_final, loss_rows) = rowwise(
        "tail", _tail_fwd_bwd, [s1["h_b"], s1["gpre"], s1["pu"], target], [row(final_norm)])

    sent = {}
    tokens = []

    two_level = ("w_mlp_up0", "w_a_in")
    swapping = []

    def send_grad(name, g):
        g = g.reshape(N_DEV, -1, g.shape[-1])
        if name in two_level:
            sent[name], token = pair_start("rs_pair_" + name, g)
            swapping.append(name)
        else:
            (sent[name],), token = scatter_start("rs_start_" + name, [g])
        tokens.append(token)

    def send_grads_together(grads):
        names = list(grads)
        handles, token = scatter_start("rs_start_" + names[0], [
            grads[n].reshape(N_DEV, -1, grads[n].shape[-1]) for n in names])
        sent.update(zip(names, handles))
        tokens.append(token)

    def second_stage(after):
        for name in swapping:
            g, half = pair_wait("rs_pairwait_" + name, sent[name], after)
            sent[name], token = chip_start("rs_chip_" + name, pair_sum("rs_sum_" + name, g, half))
            tokens.append(token)
        swapping.clear()

    def after_sends():
        deps = tuple(tokens)
        tokens.clear()
        return deps

    def mlp_ple_bwd(l, s, dh, dgpre, dpu):
        send_grads_together({
            f"w_ple_gate{l}": mm_tn(f"d_ple_gate_w{l}", s["u_ple"], dgpre, 1, deps=after_sends()),
            f"w_ple_up{l}": mm_tn(f"d_ple_up_w{l}", p_bf[l], dpu, N_DEV)})
        du = mm_nt(f"d_ple_gate_x{l}", dgpre, weight(f"w_ple_gate{l}"), out_dtype=BF16, deps=after_sends())
        (dh, dh_bf), (d_ple,) = rowwise(f"d_norm_ple{l}", _norm_bwd, [s["h_b"], du, dh], [row(ple_norm[l])])
        send_grad(f"w_mlp_down{l}", mm_tn(f"d_mlp_down_w{l}", s["act"], dh_bf, 1))
        (dpre,) = mm_nt(f"d_mlp_down_x{l}", dh_bf, weight(f"w_mlp_down{l}"), deps=after_sends(),
                        fuse=(_relu2_bwd, (s["pre"],), (BF16,)))
        second_stage(dpre)
        send_grad(f"w_mlp_up{l}", mm_tn(f"d_mlp_up_w{l}", s["u_mlp"], dpre, N_DEV))
        du = mm_nt(f"d_mlp_up_x{l}", dpre, weight(f"w_mlp_up{l}"), out_dtype=BF16, deps=after_sends())
        second_stage(du)
        (dh, dh_bf), (d_mlp,) = rowwise(f"d_norm_mlp{l}", _norm_bwd, [s["h_a"], du, dh], [row(mlp_norm[l])])
        return dh, dh_bf, d_ple, d_mlp

    dh, dh_bf, d_ple1, d_mlp1 = mlp_ple_bwd(1, s1, dh, dgpre, dpu)
    send_grad("w_b_out", mm_tn("d_b_out_w", o, dh_bf, 1))
    do = mm_nt("d_b_out_x", dh_bf, weight("w_b_out"), out_dtype=BF16, deps=after_sends())
    dq, dk, dv, ddrow = attn_bwd(q, kv, dcol, drow, lse, do)
    send_grad("w_b_q", mm_tn("d_b_q_w", u1, dq, 1))
    du1 = mm_nt("d_b_q_x", dq, weight("w_b_q"), out_dtype=BF16, deps=after_sends())
    dfl_t, d_b_f = decay_bwd(fl_t, b_f_col, ddrow.reshape(nh, t))
    dkv = jnp.stack([dk, dv]).astype(BF16)
    dfl = jnp.pad(dfl_t.T, ((0, 0), (0, HEAD_DIM - nh))).astype(BF16)
    d_w_kv = mm_tn("d_kvf_w", u_kv, dkv, 2)
    d_w_f = mm_tn("d_kvf_forget_w", u_kv, dfl, 1)
    d_w_kvf = jnp.concatenate([d_w_kv[0], d_w_kv[1], d_w_f[0, :, :nh]], axis=1)
    per_owner = (2 * d + nh) // N_DEV
    send_grad("w_kvf", jnp.stack([d_w_kvf[:, p * per_owner:(p + 1) * per_owner] for p in range(N_DEV)]))
    du_f = mm_nt("d_kvf_forget_x", dfl, w_f, deps=after_sends())
    (du_kv,) = mm_nt("d_kvf_x", dkv, w_kv, fuse=(lambda acc, extra: (acc + extra,), (du_f,), (BF16,)))
    second_stage(du_kv)
    (dh, dgpre, dpu), (d_kv_norm, d_mix1) = rowwise(
        "d_ple_norms", _two_norms_ple_bwd, [h3, du_kv, du1, dh, s0["gpre"], s0["pu"]],
        [row(kv_norm), row(mix_norm[1])])
    dh, dh_bf, d_ple0, d_mlp0 = mlp_ple_bwd(0, s0, dh, dgpre, dpu)
    send_grad("w_a_out", mm_tn("d_a_out_w", og, dh_bf, 1))
    dog = mm_nt("d_a_out_x", dh_bf, weight("w_a_out"), out_dtype=BF16, deps=after_sends())
    dz4, d_lgt, d_hg = hgrn_bwd(z, lgt, a_head_gain, states, dog)
    send_grad("w_a_in", mm_tn("d_a_in_w", u0, dz4, N_DEV, stacked=True))
    du0 = mm_nt("d_a_in_x", dz4, weight("w_a_in"), out_dtype=BF16, deps=after_sends(), stacked=True)
    second_stage(du0)
    (dx, _), (d_mix0,) = rowwise("d_norm_mix0", _norm_bwd, [x2, du0, dh], [row(mix_norm[0])])

    new = {}
    last = [dx]

    def update(name, parts, w, m, v):
        shp = w.shape
        w3, m3, v3 = (a.reshape(len(parts), -1, shp[-1]) for a in (w, m, v))
        new[name] = tuple(a.reshape(shp) for a in adamw_reduce("adamw_" + name, parts, w3, m3, v3))
        last[0] = new[name][0]

    def receive_update(name, layers, w, m, v):
        parts = {}
        for sfx in layers:
            if name + sfx in two_level:
                mine, land = chip_wait(f"rs_wait_{name}{sfx}", sent[name + sfx], last[0])
                chip = me // 2
                parts[sfx] = lax.dynamic_update_slice(land, lax.dynamic_slice_in_dim(mine, chip, 1, 0), (chip, 0, 0))
            else:
                _, parts[sfx] = scatter_wait(f"rs_wait_{name}{sfx}", sent[name + sfx], last[0])
        update(name, [parts[sfx] for sfx in sorted(layers)], w, m, v)

    both = ("1", "0")
    receive_update("w_b_out", ("",), w_b_out, m_w_b_out, v_w_b_out)
    receive_update("w_b_q", ("",), w_b_q, m_w_b_q, v_w_b_q)
    receive_update("w_kvf", ("",), w_kvf, m_w_kvf, v_w_kvf)
    receive_update("w_ple_gate", both, w_ple_gate, m_w_ple_gate, v_w_ple_gate)
    receive_update("w_ple_up", both, w_ple_up, m_w_ple_up, v_w_ple_up)
    receive_update("w_mlp_down", both, w_mlp_down, m_w_mlp_down, v_w_mlp_down)
    receive_update("w_mlp_up", both, w_mlp_up, m_w_mlp_up, v_w_mlp_up)
    receive_update("w_a_out", ("",), w_a_out, m_w_a_out, v_w_a_out)
    receive_update("w_a_in", ("",), w_a_in, m_w_a_in, v_w_a_in)

    small = dict(mix_norm=jnp.concatenate([d_mix0, d_mix1]), mlp_norm=jnp.concatenate([d_mlp0, d_mlp1]),
                 ple_norm=jnp.concatenate([d_ple0, d_ple1]), a_head_gain=d_hg, kv_norm=d_kv_norm.reshape(d),
                 b_f=d_b_f.reshape(nh), final_norm=d_final.reshape(d))
    small_w = dict(mix_norm=(mix_norm, m_mix_norm, v_mix_norm), mlp_norm=(mlp_norm, m_mlp_norm, v_mlp_norm),
                   ple_norm=(ple_norm, m_ple_norm, v_ple_norm),
                   a_head_gain=(a_head_gain, m_a_head_gain, v_a_head_gain), kv_norm=(kv_norm, m_kv_norm, v_kv_norm),
                   b_f=(b_f, m_b_f, v_b_f), final_norm=(final_norm, m_final_norm, v_final_norm))
    names = list(small)
    packed = _pack_rows([d_lgt] + [small[n] for n in names])
    everyone = all_gather("ag_small_grads", packed, deps=(last[0],))
    n_lgt_rows = d_lgt.size // 128
    lgt_parts = everyone[:, :n_lgt_rows].reshape(N_DEV, 2, d)
    lgt_parts = lax.dynamic_slice_in_dim(lgt_parts, me * a_lb_logits.shape[1], a_lb_logits.shape[1], axis=2)
    update("a_lb_logits", [lgt_parts], a_lb_logits, m_a_lb_logits, v_a_lb_logits)
    rest = everyone[:, n_lgt_rows:]
    like = [small_w[n][0] for n in names]
    packed_w, packed_m, packed_v = (_pack_rows([small_w[n][j] for n in names], rest.shape[1])[None] for j in range(3))
    outs = adamw_reduce("adamw_small", [rest], packed_w, packed_m, packed_v)
    unpacked = [_unpack_rows(a, like) for a in outs]
    for j, n in enumerate(names):
        new[n] = tuple(unpacked[q][j] for q in range(4))

    order = ["mix_norm", "mlp_norm", "ple_norm", "w_a_in", "a_lb_logits", "a_head_gain", "w_a_out", "kv_norm",
             "w_kvf", "b_f", "w_b_q", "w_b_out", "w_mlp_up", "w_mlp_down", "w_ple_gate", "w_ple_up", "final_norm"]
    loss_here, _ = lax.optimization_barrier((loss_rows[0, 0], new["final_norm"][0]))
    loss = lax.psum(loss_here, MESH_AXES)
    result = [loss, dx.reshape(x.shape)]
    for j in range(4):
        result += [new[n][j] for n in order]
    return tuple(result)
```

```python
import jax
import jax.numpy as jnp
from jax import lax
from jax.experimental import pallas as pl
from jax.experimental.pallas import tpu as pltpu

F32 = jnp.float32
BF16 = jnp.bfloat16
HEAD_DIM = 128
CHUNK = 16
TILE = 256
HEADS_PER_STEP = 4
NORM_EPS = 1e-6
N_DEV = 8
MESH_AXES = ("x", "y", "c")
VMEM_LIMIT_BYTES = 48 * 1024 * 1024
ROW_TILE_BYTES = 2 * 1024 * 1024
LR, B1, B2, ADAM_EPS, WD, STEP = 0.001, 0.9, 0.999, 1e-08, 0.01, 10
NEG_BIG = -1e30

NN = (((1,), (0,)), ((), ()))
NT = (((1,), (1,)), ((), ()))
TN = (((0,), (0,)), ((), ()))


def _params(semantics):
    return pltpu.CompilerParams(dimension_semantics=semantics, vmem_limit_bytes=VMEM_LIMIT_BYTES)


def _tile(n, prefs):
    for p in prefs:
        if n % p == 0:
            return p
    return n


def _row_tile(rows, limit):
    for cand in (2048, 1024, 512, 256, 128, 64, 32, 16):
        if cand <= limit and rows % cand == 0:
            return cand
    return rows


def _mm_call(name, a, b, dims, grid, a_spec, b_spec, o_spec, o_shape, acc_shape, k_axes, out_dtype, deps=(),
             fuse=None, split=1):
    nk = 1
    for ax in k_axes:
        nk *= grid[ax]
    fn, extra, out_dtypes = fuse if fuse else (lambda acc: (acc,), (), (out_dtype,))
    n_extra, n_out = len(extra), len(out_dtypes)

    def finish(acc, rest):
        o_refs = rest[n_extra + len(deps):n_extra + len(deps) + n_out]
        for ref, val in zip(o_refs, fn(acc, *[r[...] for r in rest[:n_extra]])):
            ref[...] = val.astype(ref.dtype)

    def product(a_ref, b_ref):
        if split == 1:
            return lax.dot_general(a_ref[...], b_ref[...], dims, preferred_element_type=F32)
        wide = a_ref.shape[1] // split
        return sum(lax.dot_general(a_ref[:, q * wide:(q + 1) * wide], b_ref[q], dims, preferred_element_type=F32)
                   for q in range(split))

    def one_step(a_ref, b_ref, *rest):
        finish(product(a_ref, b_ref), rest)

    def accumulate(a_ref, b_ref, *rest):
        acc_ref = rest[-1]
        k = 0
        for ax in k_axes:
            k = k * grid[ax] + pl.program_id(ax)
        part = product(a_ref, b_ref)

        @pl.when(k == 0)
        def _():
            acc_ref[...] = part

        @pl.when((k > 0) & (k < nk - 1))
        def _():
            acc_ref[...] += part

        @pl.when(k == nk - 1)
        def _():
            finish(acc_ref[...] + part, rest)

    sem = tuple("arbitrary" if ax in k_axes else "parallel" for ax in range(len(grid)))
    outs = pl.pallas_call(
        one_step if nk == 1 else accumulate, name=name, grid=grid,
        in_specs=[a_spec, b_spec] + [o_spec] * n_extra + [pl.BlockSpec(memory_space=pl.ANY)] * len(deps),
        out_specs=[o_spec] * n_out, out_shape=[jax.ShapeDtypeStruct(o_shape, dt) for dt in out_dtypes],
        scratch_shapes=[] if nk == 1 else [pltpu.VMEM(acc_shape, F32)], compiler_params=_params(sem),
    )(a, b, *extra, *deps)
    return outs if fuse else outs[0]


def mm_nn(name, a, b3, out_dtype=F32, out3=False, deps=(), fuse=None):
    m, k = a.shape
    g, _, n = b3.shape
    tm, tk = _tile(m, (1024, 512, 256)), _tile(k, (2048, 1024, 512, 256))
    tn = n if out3 else _tile(n, (1024, 512, 256, 128))
    nj = n // tn
    grid = (m // tm, g, nj, k // tk)
    a_spec = pl.BlockSpec((tm, tk), lambda i, gg, j, kk: (i, kk))
    b_spec = pl.BlockSpec((None, tk, tn), lambda i, gg, j, kk: (gg, kk, j))
    if out3:
        o_spec = pl.BlockSpec((None, tm, tn), lambda i, gg, j, kk: (gg, i, j))
        o_shape = (g, m, n)
    else:
        o_spec = pl.BlockSpec((tm, tn), lambda i, gg, j, kk: (i, gg * nj + j))
        o_shape = (m, g * n)
    return _mm_call(name, a, b3, NN, grid, a_spec, b_spec, o_spec, o_shape, (tm, tn), (3,), out_dtype, deps, fuse)


def mm_nt(name, a, b3, out_dtype=F32, deps=(), fuse=None, stacked=False):
    g, k, n = b3.shape
    a3 = a.ndim == 3 and not stacked
    m = a.shape[1] if a.ndim == 3 else a.shape[0]
    tm, tko = _tile(m, (1024, 512, 256)), _tile(k, (1024, 512, 256))
    tc = n if a3 else _tile(n, (2048, 1024, 512, 256, 128))
    nc = n // tc
    per = g // a.shape[0] if stacked else g
    pair = 2 if (not a3 and nc == 1 and per % 2 == 0 and tc <= 1024) else 1
    grid = (m // tm, k // tko, g // pair, nc)
    if a3:
        a_spec = pl.BlockSpec((None, tm, tc), lambda i, j, gg, c: (gg, i, c))
    elif stacked:
        a_spec = pl.BlockSpec((None, tm, pair * tc),
                              lambda i, j, gg, c: ((gg * pair) // per, i, (((gg * pair) % per) // pair) * nc + c))
    else:
        a_spec = pl.BlockSpec((tm, pair * tc), lambda i, j, gg, c: (i, gg * nc + c))
    if pair == 1:
        b_spec = pl.BlockSpec((None, tko, tc), lambda i, j, gg, c: (gg, j, c))
    else:
        b_spec = pl.BlockSpec((pair, tko, tc), lambda i, j, gg, c: (gg, j, c))
    o_spec = pl.BlockSpec((tm, tko), lambda i, j, gg, c: (i, j))
    return _mm_call(name, a, b3, NT, grid, a_spec, b_spec, o_spec, (m, k), (tm, tko), (2, 3), out_dtype, deps, fuse,
                    pair)


def mm_tn(name, a, b, g, out_dtype=BF16, deps=(), stacked=False):
    t, k = a.shape
    b3 = b.ndim == 3 and not stacked
    n = b.shape[2] if b3 else (b.shape[0] * b.shape[2] if stacked else b.shape[1]) // g
    tm = _tile(k, (1024, 512, 256))
    tn = n if b3 else _tile(n, (1024, 512, 256, 128))
    tt = _tile(t, (2048, 1024, 512, 256))
    nj = n // tn
    grid = (g, k // tm, nj, t // tt)
    a_spec = pl.BlockSpec((tt, tm), lambda gg, i, j, s: (s, i))
    if b3:
        b_spec = pl.BlockSpec((None, tt, tn), lambda gg, i, j, s: (gg, s, j))
    elif stacked:
        per = g // b.shape[0]
        b_spec = pl.BlockSpec((None, tt, tn), lambda gg, i, j, s: (gg // per, s, (gg % per) * nj + j))
    else:
        b_spec = pl.BlockSpec((tt, tn), lambda gg, i, j, s: (s, gg * nj + j))
    o_spec = pl.BlockSpec((None, tm, tn), lambda gg, i, j, s: (gg, i, j))
    return _mm_call(name, a, b, TN, grid, a_spec, b_spec, o_spec, (g, k, n), (tm, tn), (3,), out_dtype, deps)


def rowwise(name, fn, rows, vecs=()):
    t = rows[0].shape[0]
    wmax = max(r.shape[1] for r in rows)
    tr = _row_tile(t, ROW_TILE_BYTES // (4 * wmax))
    row_s = [jax.ShapeDtypeStruct((tr, r.shape[1]), r.dtype) for r in rows]
    vec_s = [jax.ShapeDtypeStruct(v.shape, v.dtype) for v in vecs]
    out_rows_s, out_sums_s = jax.eval_shape(fn, *row_s, *vec_s)
    n_in, n_r = len(rows) + len(vecs), len(out_rows_s)

    def body(*refs):
        i = pl.program_id(0)
        o_rows, o_sums = fn(*[r[...] for r in refs[:n_in]])
        for ref, val in zip(refs[n_in:n_in + n_r], o_rows):
            ref[...] = val

        if out_sums_s:
            @pl.when(i == 0)
            def _():
                for ref in refs[n_in + n_r:]:
                    ref[...] = jnp.zeros_like(ref)

            for ref, val in zip(refs[n_in + n_r:], o_sums):
                ref[...] += val

    in_specs = [pl.BlockSpec((tr, r.shape[1]), lambda i: (i, 0)) for r in rows]
    in_specs += [pl.BlockSpec(v.shape, lambda i: (0, 0)) for v in vecs]
    out_specs = [pl.BlockSpec((tr, s.shape[1]), lambda i: (i, 0)) for s in out_rows_s]
    out_specs += [pl.BlockSpec(s.shape, lambda i: (0, 0)) for s in out_sums_s]
    out_shape = [jax.ShapeDtypeStruct((t, s.shape[1]), s.dtype) for s in out_rows_s]
    out_shape += [jax.ShapeDtypeStruct(s.shape, s.dtype) for s in out_sums_s]
    outs = pl.pallas_call(
        body, name=name, grid=(t // tr,), in_specs=in_specs, out_specs=out_specs, out_shape=out_shape,
        compiler_params=_params(("arbitrary",)),
    )(*rows, *vecs)
    return outs[:n_r], outs[n_r:]


def _rms(x, gain):
    return x * lax.rsqrt(jnp.mean(x * x, axis=-1, keepdims=True) + NORM_EPS) * gain


def _norm_fwd(x, gain):
    return (_rms(x, gain).astype(BF16),), ()


def _add_norm_fwd(h, a, gain):
    h = h + a
    return (h, _rms(h, gain).astype(BF16)), ()


def _relu2(pre):
    r = jnp.maximum(pre, 0.0)
    return pre, r * r


def _ple(h, gpre, pu):
    return h + pu * jax.nn.sigmoid(gpre)


def _ple_two_norms_fwd(h, gpre, pu, gain_a, gain_b):
    h = _ple(h, gpre, pu)
    return (h, _rms(h, gain_a).astype(BF16), _rms(h, gain_b).astype(BF16)), ()


def _tail_fwd_bwd(h, gpre, pu, target, gain):
    def row_loss(h, gpre, pu, gain):
        y = _rms(_ple(h, gpre, pu), gain)
        return 0.5 * jnp.mean(jnp.square(y - target), axis=-1, keepdims=True)

    loss, vjp = jax.vjp(row_loss, h, gpre, pu, gain)
    dh, dgpre, dpu, dgain = vjp(jnp.ones_like(loss))
    loss = jnp.broadcast_to(jnp.sum(loss, axis=0, keepdims=True), (1, 128))
    return (dh, dgpre.astype(BF16), dpu.astype(BF16)), (dgain, loss)


def _norm_bwd(h, du, dh_in, gain):
    _, vjp = jax.vjp(_rms, h, gain)
    dh, dgain = vjp(du.astype(F32))
    dh = dh_in + dh
    return (dh, dh.astype(BF16)), (dgain,)


def _two_norms_ple_bwd(h, du_a, du_b, dh_in, gpre, pu, gain_a, gain_b):
    _, vjp = jax.vjp(lambda h, ga, gb: (_rms(h, ga), _rms(h, gb)), h, gain_a, gain_b)
    dh, dga, dgb = vjp((du_a.astype(F32), du_b.astype(F32)))
    dh = dh_in + dh
    _, gate_vjp = jax.vjp(lambda g, u: u * jax.nn.sigmoid(g), gpre, pu)
    dgpre, dpu = gate_vjp(dh)
    return (dh, dgpre.astype(BF16), dpu.astype(BF16)), (dga, dgb)


def _relu2_bwd(dact, pre):
    return (dact * 2.0 * jnp.maximum(pre.astype(F32), 0.0),)


def _bf16_dot(dims_fwd, dims_da, dims_db, swap_da, swap_db):
    @jax.custom_vjp
    def dot(a, b):
        return lax.dot_general(a.astype(BF16), b.astype(BF16), dims_fwd, preferred_element_type=F32)

    def fwd(a, b):
        return dot(a, b), (a, b)

    def bwd(res, ct):
        a, b = res
        ct, a, b = ct.astype(BF16), a.astype(BF16), b.astype(BF16)
        da = lax.dot_general(*((b, ct) if swap_da else (ct, b)), dims_da, preferred_element_type=F32)
        db = lax.dot_general(*((ct, a) if swap_db else (a, ct)), dims_db, preferred_element_type=F32)
        return da, db

    dot.defvjp(fwd, bwd)
    return dot


_dot_nn = _bf16_dot(NN, NT, TN, False, False)
_dot_nt = _bf16_dot(NT, NN, TN, False, True)
_dot_tn = _bf16_dot(TN, NT, NN, True, False)


def _chunk_causal_mask():
    r = lax.broadcasted_iota(jnp.int32, (TILE, TILE), 0)
    c = lax.broadcasted_iota(jnp.int32, (TILE, TILE), 1)
    return ((r // CHUNK) == (c // CHUNK)) & (c <= r)


def _chunk_scan(x, reverse):
    pos = lax.broadcasted_iota(jnp.int32, x.shape, 0) % CHUNK
    step = 1
    while step < CHUNK:
        if reverse:
            x = x + jnp.where(pos < CHUNK - step, pltpu.roll(x, x.shape[0] - step, axis=0), 0.0)
        else:
            x = x + jnp.where(pos >= step, pltpu.roll(x, step, axis=0), 0.0)
        step *= 2
    return x


def _chunk_total(x):
    return _chunk_scan(x, False) + _chunk_scan(x, True) - x


@jax.custom_vjp
def _chunk_sums(x):
    return _chunk_scan(x, False), _chunk_total(x)


def _chunk_sums_fwd(x):
    return _chunk_sums(x), None


def _chunk_sums_bwd(_, ct):
    return (_chunk_scan(ct[0], True) + _chunk_total(ct[1]),)


_chunk_sums.defvjp(_chunk_sums_fwd, _chunk_sums_bwd)


def _hgrn_tile(q, f, i, g, lgt, hg, st):
    d = q.shape[1]
    l0, l1 = lgt[0:1], lgt[1:2]
    mx = jnp.maximum(l0, l1)
    e0, e1 = jnp.exp(l0 - mx), jnp.exp(l1 - mx)
    lb = e0 / (e0 + e1)
    fg = lb + (1.0 - lb) * jax.nn.sigmoid(f)
    k = 1.0 - fg
    causal = _chunk_causal_mask()
    b, b_last = _chunk_sums(jnp.log(fg))
    q_in = q * jax.nn.sigmoid(q) * (d ** -0.5) * jnp.exp(b)
    k_in = k * jnp.exp(-b)
    k_end = k * jnp.exp(b_last - b)
    att = jnp.where(causal, _dot_nt(q_in, k_in), 0.0)
    o_intra = _dot_nn(att, i)
    n_chunks = TILE // CHUNK
    chunk_of_row = lax.broadcasted_iota(jnp.int32, (TILE, 1), 0) // CHUNK

    def spread(a):
        return jnp.concatenate([jnp.where(chunk_of_row == n, a, 0.0) for n in range(n_chunks)], axis=1)

    increments = _dot_tn(i, spread(k_end))
    states = []
    for n in range(n_chunks):
        states.append(st)
        decay = jnp.exp(jnp.mean(b_last[n * CHUNK:(n + 1) * CHUNK], axis=0, keepdims=True))
        st = st * decay + increments[:, n * d:(n + 1) * d]
    o = o_intra + _dot_nt(spread(q_in), jnp.concatenate(states, axis=1))
    o = o * lax.rsqrt(jnp.mean(o * o, axis=-1, keepdims=True) + NORM_EPS) * hg
    return o * (g * jax.nn.sigmoid(g)), st


def hgrn_fwd(z, lgt, hg):
    t, d4 = z.shape
    d = d4 // 4
    nh, nt = d // HEAD_DIM, t // TILE
    hp = HEADS_PER_STEP
    wide = hp * HEAD_DIM

    def body(q_ref, f_ref, i_ref, g_ref, lgt_ref, hg_ref, o_ref, st_out_ref, st_ref):
        tt = pl.program_id(1)

        @pl.when(tt == 0)
        def _():
            st_ref[...] = jnp.zeros_like(st_ref)

        for hh in range(hp):
            cols = slice(hh * HEAD_DIM, (hh + 1) * HEAD_DIM)
            st = st_ref[hh]
            st_out_ref[hh] = st
            o, st = _hgrn_tile(q_ref[:, cols], f_ref[:, cols], i_ref[:, cols], g_ref[:, cols], lgt_ref[:, cols],
                               hg_ref[...], st)
            o_ref[:, cols] = o.astype(o_ref.dtype)
            st_ref[hh] = st

    def part(p):
        return pl.BlockSpec((TILE, wide), lambda h, tt: (tt, p * (nh // hp) + h))

    return pl.pallas_call(
        body, name="hgrn_fwd", grid=(nh // hp, nt),
        in_specs=[part(0), part(1), part(2), part(3),
                  pl.BlockSpec((2, wide), lambda h, tt: (0, h)),
                  pl.BlockSpec((1, HEAD_DIM), lambda h, tt: (0, 0))],
        out_specs=[pl.BlockSpec((TILE, wide), lambda h, tt: (tt, h)),
                   pl.BlockSpec((hp, None, HEAD_DIM, HEAD_DIM), lambda h, tt: (h, tt, 0, 0))],
        out_shape=[jax.ShapeDtypeStruct((t, d), BF16),
                   jax.ShapeDtypeStruct((nh, nt, HEAD_DIM, HEAD_DIM), F32)],
        scratch_shapes=[pltpu.VMEM((hp, HEAD_DIM, HEAD_DIM), F32)],
        compiler_params=_params(("parallel", "arbitrary")),
    )(z, z, z, z, lgt, hg)


def hgrn_bwd(z, lgt, hg, states, dout):
    t, d4 = z.shape
    d = d4 // 4
    nh, nt = d // HEAD_DIM, t // TILE
    hp = HEADS_PER_STEP
    wide = hp * HEAD_DIM

    def body(q_ref, f_ref, i_ref, g_ref, lgt_ref, hg_ref, st_in_ref, do_ref, dz_ref, dlgt_ref, dhg_ref, dst_ref):
        h, tt = pl.program_id(0), pl.program_id(1)

        @pl.when(tt == 0)
        def _():
            dst_ref[...] = jnp.zeros_like(dst_ref)
            dlgt_ref[...] = jnp.zeros_like(dlgt_ref)

        @pl.when((tt == 0) & (h == 0))
        def _():
            dhg_ref[...] = jnp.zeros_like(dhg_ref)

        for hh in range(hp):
            cols = slice(hh * HEAD_DIM, (hh + 1) * HEAD_DIM)
            _, vjp = jax.vjp(_hgrn_tile, q_ref[:, cols], f_ref[:, cols], i_ref[:, cols], g_ref[:, cols],
                             lgt_ref[:, cols], hg_ref[...], st_in_ref[hh])
            grads = vjp((do_ref[:, cols].astype(F32), dst_ref[hh]))
            for p in range(4):
                dz_ref[p, :, cols] = grads[p].astype(dz_ref.dtype)
            dlgt_ref[:, cols] += grads[4]
            dhg_ref[...] += grads[5]
            dst_ref[hh] = grads[6]

    def part(p):
        return pl.BlockSpec((TILE, wide), lambda h, tt: (nt - 1 - tt, p * (nh // hp) + h))

    return pl.pallas_call(
        body, name="hgrn_bwd", grid=(nh // hp, nt),
        in_specs=[part(0), part(1), part(2), part(3),
                  pl.BlockSpec((2, wide), lambda h, tt: (0, h)),
                  pl.BlockSpec((1, HEAD_DIM), lambda h, tt: (0, 0)),
                  pl.BlockSpec((hp, None, HEAD_DIM, HEAD_DIM), lambda h, tt: (h, nt - 1 - tt, 0, 0)),
                  pl.BlockSpec((TILE, wide), lambda h, tt: (nt - 1 - tt, h))],
        out_specs=[pl.BlockSpec((4, TILE, wide), lambda h, tt: (0, nt - 1 - tt, h)),
                   pl.BlockSpec((2, wide), lambda h, tt: (0, h)),
                   pl.BlockSpec((1, HEAD_DIM), lambda h, tt: (0, 0))],
        out_shape=[jax.ShapeDtypeStruct((4, t, d), BF16),
                   jax.ShapeDtypeStruct((2, d), F32),
                   jax.ShapeDtypeStruct((1, HEAD_DIM), F32)],
        scratch_shapes=[pltpu.VMEM((hp, HEAD_DIM, HEAD_DIM), F32)],
        compiler_params=_params(("arbitrary", "arbitrary")),
    )(z, z, z, z, lgt, hg, states, dout)


def _log_sigmoid(x):
    return jnp.minimum(x, 0.0) - jnp.log(1.0 + jnp.exp(-jnp.abs(x)))


def decay_fwd(fl_t, b_f):
    nh, t = fl_t.shape

    def body(fl_ref, b_ref, out_ref):
        r = lax.broadcasted_iota(jnp.int32, (128, 128), 0)
        c = lax.broadcasted_iota(jnp.int32, (128, 128), 1)
        upper = (r <= c).astype(F32)
        carry = jnp.zeros((nh, 1), F32)
        for j in range(t // 128):
            cols = slice(j * 128, (j + 1) * 128)
            ls = _log_sigmoid(fl_ref[:, cols] + b_ref[...])
            out_ref[:, cols] = carry + jnp.dot(ls, upper, precision=lax.Precision.HIGHEST,
                                               preferred_element_type=F32)
            carry = carry + jnp.sum(ls, axis=1, keepdims=True)

    return pl.pallas_call(body, name="decay_fwd", out_shape=jax.ShapeDtypeStruct((nh, t), F32),
                          compiler_params=_params(None))(fl_t, b_f)


def decay_bwd(fl_t, b_f, ddcum):
    nh, t = fl_t.shape

    def body(fl_ref, b_ref, dd_ref, dfl_ref, db_ref):
        r = lax.broadcasted_iota(jnp.int32, (128, 128), 0)
        c = lax.broadcasted_iota(jnp.int32, (128, 128), 1)
        lower = (r >= c).astype(F32)
        carry = jnp.zeros((nh, 1), F32)
        db = jnp.zeros((nh, 1), F32)
        for j in reversed(range(t // 128)):
            cols = slice(j * 128, (j + 1) * 128)
            dd = dd_ref[:, cols]
            dls = carry + jnp.dot(dd, lower, precision=lax.Precision.HIGHEST, preferred_element_type=F32)
            carry = carry + jnp.sum(dd, axis=1, keepdims=True)
            dfl = dls * jax.nn.sigmoid(-(fl_ref[:, cols] + b_ref[...]))
            dfl_ref[:, cols] = dfl
            db = db + jnp.sum(dfl, axis=1, keepdims=True)
        db_ref[...] = db

    return pl.pallas_call(body, name="decay_bwd",
                          out_shape=[jax.ShapeDtypeStruct((nh, t), F32), jax.ShapeDtypeStruct((nh, 1), F32)],
                          compiler_params=_params(None))(fl_t, b_f, ddcum)


def _attn_parts(t):
    tq = _tile(t, (256, 128))
    per_part = 2 if t // tq >= 4 else 1
    return tq, [(first, per_part, (first + per_part) * tq) for first in range(0, t // tq, per_part)]


def _attn_logits(q_ref, k_ref, dcol_ref, drow_ref, row0, tq, keys):
    qs = (q_ref[...] * (HEAD_DIM ** -0.5)).astype(BF16)
    s = lax.dot_general(qs, k_ref[...], NT, preferred_element_type=F32)
    s = s + dcol_ref[...] - drow_ref[...]
    row = row0 + lax.broadcasted_iota(jnp.int32, (tq, keys), 0)
    col = lax.broadcasted_iota(jnp.int32, (tq, keys), 1)
    return qs, jnp.where(col <= row, s, NEG_BIG)


def attn_fwd(q, kv, dcol, drow):
    t, d = q.shape
    nh = d // HEAD_DIM
    tq, parts = _attn_parts(t)
    o = lse = None
    for first, count, keys in parts:
        def body(q_ref, k_ref, v_ref, dcol_ref, drow_ref, *rest, first=first, keys=keys):
            o_ref, lse_ref = rest[-2:]
            _, s = _attn_logits(q_ref, k_ref, dcol_ref, drow_ref, (first + pl.program_id(1)) * tq, tq, keys)
            m = jnp.max(s, axis=1, keepdims=True)
            p = jnp.exp(s - m)
            l = jnp.sum(p, axis=1, keepdims=True)
            acc = jnp.dot(p.astype(BF16), v_ref[...], preferred_element_type=F32)
            o_ref[...] = (acc / l).astype(o_ref.dtype)
            lse_ref[...] = m + jnp.log(l)

        tile = pl.BlockSpec((tq, HEAD_DIM), lambda h, i, first=first: (first + i, h))
        col = pl.BlockSpec((None, tq, 1), lambda h, i, first=first: (h, first + i, 0))
        seen_k = pl.BlockSpec((keys, HEAD_DIM), lambda h, i: (0, h))
        seen_v = pl.BlockSpec((keys, HEAD_DIM), lambda h, i: (0, nh + h))
        carried = [] if o is None else [o, lse]
        o, lse = pl.pallas_call(
            body, name=f"attn_fwd_{first}", grid=(nh, count),
            in_specs=[tile, seen_k, seen_v, col, pl.BlockSpec((None, 1, keys), lambda h, i: (h, 0, 0))]
            + [pl.BlockSpec(memory_space=pl.ANY)] * len(carried),
            out_specs=[tile, col],
            out_shape=[jax.ShapeDtypeStruct((t, d), BF16), jax.ShapeDtypeStruct((nh, t, 1), F32)],
            input_output_aliases={5: 0, 6: 1} if carried else {},
            compiler_params=_params(("parallel", "parallel")),
        )(q, kv, kv, dcol, drow, *carried)
    return o, lse


def attn_bwd(q, kv, dcol, drow, lse, do):
    t, d = q.shape
    nh = d // HEAD_DIM
    tq, parts = _attn_parts(t)
    dq = dk = dv = ddrow = None
    for first, count, keys in reversed(parts):
        first_call = dq is None

        def body(q_ref, k_ref, v_ref, dcol_ref, drow_ref, lse_ref, do_ref, *rest, first=first, keys=keys,
                 count=count, first_call=first_call):
            dq_ref, dk_ref, dv_ref, ddrow_ref, dk_acc, dv_acc, ddrow_acc = rest[-7:]
            i = pl.program_id(1)

            @pl.when(i == 0)
            def _():
                if first_call:
                    dk_acc[...] = jnp.zeros_like(dk_acc)
                    dv_acc[...] = jnp.zeros_like(dv_acc)
                    ddrow_acc[...] = jnp.zeros_like(ddrow_acc)
                else:
                    dk_acc[...] = rest[1][...]
                    dv_acc[...] = rest[2][...]
                    ddrow_acc[...] = rest[3][...]

            qs, s = _attn_logits(q_ref, k_ref, dcol_ref, drow_ref, (first + i) * tq, tq, keys)
            p = jnp.exp(s - lse_ref[...])
            do = do_ref[...]
            dp = lax.dot_general(do, v_ref[...], NT, preferred_element_type=F32)
            ds = p * (dp - jnp.sum(p * dp, axis=1, keepdims=True))
            dsb = ds.astype(BF16)
            dq_ref[...] = (jnp.dot(dsb, k_ref[...], preferred_element_type=F32) * (HEAD_DIM ** -0.5)).astype(dq_ref.dtype)
            dk_acc[...] += lax.dot_general(dsb, qs, TN, preferred_element_type=F32)
            dv_acc[...] += lax.dot_general(p.astype(BF16), do, TN, preferred_element_type=F32)
            ddrow_acc[...] -= jnp.sum(ds, axis=0, keepdims=True)

            @pl.when(i == count - 1)
            def _():
                dk_ref[...] = dk_acc[...]
                dv_ref[...] = dv_acc[...]
                ddrow_ref[...] = ddrow_acc[...]

        tile = pl.BlockSpec((tq, HEAD_DIM), lambda h, i, first=first: (first + i, h))
        col = pl.BlockSpec((None, tq, 1), lambda h, i, first=first: (h, first + i, 0))
        seen = pl.BlockSpec((keys, HEAD_DIM), lambda h, i: (0, h))
        seen_v = pl.BlockSpec((keys, HEAD_DIM), lambda h, i: (0, nh + h))
        seen_row = pl.BlockSpec((None, 1, keys), lambda h, i: (h, 0, 0))
        carried = [] if first_call else [dq, dk, dv, ddrow]
        carried_specs = [] if first_call else [pl.BlockSpec(memory_space=pl.ANY), seen, seen, seen_row]
        dq, dk, dv, ddrow = pl.pallas_call(
            body, name=f"attn_bwd_{first}", grid=(nh, count),
            in_specs=[tile, seen, seen_v, col, seen_row, col, tile] + carried_specs,
            out_specs=[tile, seen, seen, seen_row],
            out_shape=[jax.ShapeDtypeStruct((t, d), BF16), jax.ShapeDtypeStruct((t, d), F32),
                       jax.ShapeDtypeStruct((t, d), F32), jax.ShapeDtypeStruct((nh, 1, t), F32)],
            scratch_shapes=[pltpu.VMEM((keys, HEAD_DIM), F32), pltpu.VMEM((keys, HEAD_DIM), F32),
                            pltpu.VMEM((1, keys), F32)],
            input_output_aliases={} if first_call else {7: 0, 8: 1, 9: 2, 10: 3},
            compiler_params=_params(("parallel", "arbitrary")),
        )(q, kv, kv, dcol, drow, lse, do, *carried)
    return dq, dk, dv, ddrow


def _my_index():
    return (lax.axis_index("x") * 2 + lax.axis_index("y")) * 2 + lax.axis_index("c")


def all_gather(name, src, deps=()):
    def body(src_ref, *rest):
        out_ref, send_sems, recv_sems, local_sem = rest[len(deps):]
        x, y, c = (lax.axis_index(a) for a in MESH_AXES)
        me = (x * 2 + y) * 2 + c
        local = pltpu.make_async_copy(src_ref, out_ref.at[me], local_sem)
        local.start()
        copies = []
        for dlt in range(1, N_DEV):
            copies.append(pltpu.make_async_remote_copy(
                src_ref=src_ref, dst_ref=out_ref.at[me], send_sem=send_sems.at[dlt - 1],
                recv_sem=recv_sems.at[dlt - 1], device_id=(x ^ (dlt // 4), y ^ ((dlt // 2) % 2), c ^ (dlt % 2)),
                device_id_type=pl.DeviceIdType.MESH))
        for cp in copies:
            cp.start()
        for cp in copies:
            cp.wait_recv()
        for cp in copies:
            cp.wait_send()
        local.wait()

    return pl.pallas_call(
        body, name=name, out_shape=jax.ShapeDtypeStruct((N_DEV,) + tuple(src.shape), src.dtype),
        in_specs=[pl.BlockSpec(memory_space=pl.ANY)] * (1 + len(deps)), out_specs=pl.BlockSpec(memory_space=pl.ANY),
        scratch_shapes=[pltpu.SemaphoreType.DMA((N_DEV - 1,)), pltpu.SemaphoreType.DMA((N_DEV - 1,)),
                        pltpu.SemaphoreType.DMA],
        compiler_params=pltpu.CompilerParams(has_side_effects=True),
    )(src, *deps)


_HBM = pl.BlockSpec(memory_space=pltpu.HBM)
_SEM = pl.BlockSpec(memory_space=pltpu.SEMAPHORE)
_DATAFLOW = pltpu.SideEffectType.DATAFLOW_SIDE_EFFECTING


def _peer_copies(src_ref, land_ref, send_sems, recv_sems):
    x, y, c = (lax.axis_index(a) for a in MESH_AXES)
    me = (x * 2 + y) * 2 + c
    copies = []
    for dlt in range(1, N_DEV):
        px, py, pc = x ^ (dlt // 4), y ^ ((dlt // 2) % 2), c ^ (dlt % 2)
        peer = (px * 2 + py) * 2 + pc
        copies.append(pltpu.make_async_remote_copy(
            src_ref=src_ref.at[peer], dst_ref=land_ref.at[me],
            send_sem=send_sems.at[dlt - 1], recv_sem=recv_sems.at[dlt - 1],
            device_id=(px, py, pc), device_id_type=pl.DeviceIdType.MESH))
    return copies


def _own_copy(src_ref, land_ref, send_sems):
    me = (lax.axis_index("x") * 2 + lax.axis_index("y")) * 2 + lax.axis_index("c")
    return pltpu.make_async_copy(src_ref.at[me], land_ref.at[me], send_sems.at[N_DEV - 1])


def scatter_start(name, srcs):
    n = len(srcs)
    lands = [lax.empty(s.shape, s.dtype) for s in srcs]

    def body(*refs):
        src_refs, land_refs = refs[:n], refs[n:2 * n]
        send_sems, recv_sems = refs[2 * n:3 * n], refs[3 * n:4 * n]
        token = refs[-1]
        for j in range(n):
            for cp in _peer_copies(src_refs[j], land_refs[j], send_sems[j], recv_sems[j]):
                cp.start()
            _own_copy(src_refs[j], land_refs[j], send_sems[j]).start()
        token[...] = jnp.zeros_like(token)

    sems = [pltpu.SemaphoreType.DMA((N_DEV,))] * n + [pltpu.SemaphoreType.DMA((N_DEV - 1,))] * n
    thru = [pltpu.HBM(a.shape, a.dtype) for a in list(srcs) + lands]
    outs = pl.pallas_call(
        body, name=name, out_shape=tuple(sems + thru + [jax.ShapeDtypeStruct((8, 128), F32)]),
        in_specs=[_HBM] * (2 * n), out_specs=tuple([_SEM] * (2 * n) + [_HBM] * (2 * n) + [pl.BlockSpec(memory_space=pltpu.VMEM)]),
        input_output_aliases={j: 2 * n + j for j in range(2 * n)},
        compiler_params=pltpu.CompilerParams(has_side_effects=_DATAFLOW),
    )(*[pltpu.with_memory_space_constraint(a, pltpu.HBM) for a in list(srcs) + lands])
    handles = [(outs[j], outs[n + j], outs[2 * n + j], outs[3 * n + j]) for j in range(n)]
    return handles, outs[-1]


def scatter_wait(name, handle, after):
    send_sems, recv_sems, src, land = handle

    def body(src_ref, land_ref, send_ref, recv_ref, after_ref, src_out, land_out):
        for cp in _peer_copies(src_ref, land_ref, send_ref, recv_ref):
            cp.wait_send()
            cp.wait_recv()
        _own_copy(src_ref, land_ref, send_ref).wait()

    return pl.pallas_call(
        body, name=name, out_shape=(pltpu.HBM(src.shape, src.dtype), pltpu.HBM(land.shape, land.dtype)),
        in_specs=[_HBM, _HBM, _SEM, _SEM, pl.BlockSpec(memory_space=pl.ANY)], out_specs=(_HBM, _HBM),
        input_output_aliases={0: 0, 1: 1},
        compiler_params=pltpu.CompilerParams(has_side_effects=_DATAFLOW),
    )(src, land, send_sems, recv_sems, after)


N_OTHER_CHIPS = 3


def _two_level_places():
    x, y, c = (lax.axis_index(a) for a in MESH_AXES)
    return (x, y, c), (x * 2 + y) * 2 + c, (x, y, 1 - c), [(1 - x, y), (x, 1 - y), (1 - x, 1 - y)]


def _first_copies(land_ref, send_sems, recv_sems):
    (x, y, c), me, other_core, chips = _two_level_places()
    targets = [other_core] + [(cx, cy, c) for cx, cy in chips]
    return [pltpu.make_async_remote_copy(
        src_ref=land_ref.at[me], dst_ref=land_ref.at[me], send_sem=send_sems.at[k], recv_sem=recv_sems.at[k],
        device_id=to, device_id_type=pl.DeviceIdType.MESH) for k, to in enumerate(targets)]


def _passed_on_copies(land_ref, send_sems, recv_sems):
    (x, y, c), me, other_core, chips = _two_level_places()
    copies = []
    for k, (cx, cy) in enumerate(chips):
        slot = land_ref.at[(cx * 2 + cy) * 2 + c]
        copies.append(pltpu.make_async_remote_copy(
            src_ref=slot, dst_ref=slot, send_sem=send_sems.at[k], recv_sem=recv_sems.at[k],
            device_id=other_core, device_id_type=pl.DeviceIdType.MESH))
    return copies


def gather_start(name, lands):
    n = len(lands)

    def body(*refs):
        land_refs, send_sems, recv_sems = refs[:n], refs[n:2 * n], refs[2 * n:3 * n]
        for j in range(n):
            for cp in _first_copies(land_refs[j], send_sems[j], recv_sems[j]):
                cp.start()

    sems = [pltpu.SemaphoreType.DMA((1 + N_OTHER_CHIPS,))] * (2 * n)
    outs = pl.pallas_call(
        body, name=name, out_shape=tuple(sems + [pltpu.HBM(a.shape, a.dtype) for a in lands]),
        in_specs=[_HBM] * n, out_specs=tuple([_SEM] * (2 * n) + [_HBM] * n),
        input_output_aliases={j: 2 * n + j for j in range(n)},
        compiler_params=pltpu.CompilerParams(has_side_effects=_DATAFLOW),
    )(*[pltpu.with_memory_space_constraint(a, pltpu.HBM) for a in lands])
    return [[outs[j], outs[n + j], outs[2 * n + j]] for j in range(n)]


def gather_pass_on(name, handle, after):
    send_sems, recv_sems, land = handle

    def body(land_ref, recv_ref, after_ref, land_out, send2, recv2, token):
        arrivals = _first_copies(land_ref, recv_ref, recv_ref)
        for k, cp in enumerate(_passed_on_copies(land_ref, send2, recv2)):
            arrivals[1 + k].wait_recv()
            cp.start()
        token[...] = jnp.zeros_like(token)

    sem3 = pltpu.SemaphoreType.DMA((N_OTHER_CHIPS,))
    land, send2, recv2, token = pl.pallas_call(
        body, name=name,
        out_shape=(pltpu.HBM(land.shape, land.dtype), sem3, sem3, jax.ShapeDtypeStruct((8, 128), F32)),
        in_specs=[_HBM, _SEM, pl.BlockSpec(memory_space=pl.ANY)],
        out_specs=(_HBM, _SEM, _SEM, pl.BlockSpec(memory_space=pltpu.VMEM)),
        input_output_aliases={0: 0}, compiler_params=pltpu.CompilerParams(has_side_effects=_DATAFLOW),
    )(land, recv_sems, after)
    return [send_sems, recv_sems, land, send2, recv2], token


def gather_wait(name, handle, after):
    send_sems, recv_sems, land, send2, recv2 = handle

    def body(land_ref, send_ref, recv_ref, send2_ref, recv2_ref, after_ref, land_out):
        first = _first_copies(land_ref, send_ref, recv_ref)
        for cp in first:
            cp.wait_send()
        first[0].wait_recv()
        for cp in _passed_on_copies(land_ref, send2_ref, recv2_ref):
            cp.wait_send()
            cp.wait_recv()

    return pl.pallas_call(
        body, name=name, out_shape=pltpu.HBM(land.shape, land.dtype),
        in_specs=[_HBM, _SEM, _SEM, _SEM, _SEM, pl.BlockSpec(memory_space=pl.ANY)], out_specs=_HBM,
        input_output_aliases={0: 0}, compiler_params=pltpu.CompilerParams(has_side_effects=_DATAFLOW),
    )(land, send_sems, recv_sems, send2, recv2, after)


N_CHIPS = 4


def _pair_copies(g_ref, half_ref, send_sems, recv_sems):
    (x, y, c), me, other_core, chips = _two_level_places()
    return [pltpu.make_async_remote_copy(
        src_ref=g_ref.at[chip * 2 + (1 - c)], dst_ref=half_ref.at[chip], send_sem=send_sems.at[chip],
        recv_sem=recv_sems.at[chip], device_id=other_core, device_id_type=pl.DeviceIdType.MESH)
        for chip in range(N_CHIPS)]


def pair_start(name, g):
    half = lax.empty((N_CHIPS,) + g.shape[1:], g.dtype)

    def body(g_ref, half_ref, send_sems, recv_sems, g_out, half_out, token):
        for cp in _pair_copies(g_ref, half_ref, send_sems, recv_sems):
            cp.start()
        token[...] = jnp.zeros_like(token)

    sem = pltpu.SemaphoreType.DMA((N_CHIPS,))
    outs = pl.pallas_call(
        body, name=name,
        out_shape=(sem, sem, pltpu.HBM(g.shape, g.dtype), pltpu.HBM(half.shape, half.dtype),
                   jax.ShapeDtypeStruct((8, 128), F32)),
        in_specs=[_HBM, _HBM], out_specs=(_SEM, _SEM, _HBM, _HBM, pl.BlockSpec(memory_space=pltpu.VMEM)),
        input_output_aliases={0: 2, 1: 3}, compiler_params=pltpu.CompilerParams(has_side_effects=_DATAFLOW),
    )(pltpu.with_memory_space_constraint(g, pltpu.HBM), half)
    return list(outs[:4]), outs[4]


def pair_wait(name, handle, after):
    send_sems, recv_sems, g, half = handle

    def body(g_ref, half_ref, send_ref, recv_ref, after_ref, g_out, half_out):
        for cp in _pair_copies(g_ref, half_ref, send_ref, recv_ref):
            cp.wait_send()
            cp.wait_recv()

    return pl.pallas_call(
        body, name=name, out_shape=(pltpu.HBM(g.shape, g.dtype), pltpu.HBM(half.shape, half.dtype)),
        in_specs=[_HBM, _HBM, _SEM, _SEM, pl.BlockSpec(memory_space=pl.ANY)], out_specs=(_HBM, _HBM),
        input_output_aliases={0: 0, 1: 1}, compiler_params=pltpu.CompilerParams(has_side_effects=_DATAFLOW),
    )(g, half, send_sems, recv_sems, after)


def pair_sum(name, g, half):
    _, r, wd = g.shape
    tr = _row_tile(r, 2 * ROW_TILE_BYTES // (4 * wd))
    kind = lax.axis_index("c").astype(jnp.int32).reshape(1)

    def body(kind_ref, g_ref, half_ref, o_ref):
        o_ref[...] = (g_ref[...].astype(F32) + half_ref[...].astype(F32)).astype(o_ref.dtype)

    spec = pl.BlockSpec((None, tr, wd), lambda chip, i, kind_ref: (chip, i, 0))
    return pl.pallas_call(
        body, name=name,
        grid_spec=pltpu.PrefetchScalarGridSpec(
            num_scalar_prefetch=1, grid=(N_CHIPS, r // tr),
            in_specs=[pl.BlockSpec((None, tr, wd), lambda chip, i, kind_ref: (chip * 2 + kind_ref[0], i, 0)), spec],
            out_specs=spec),
        out_shape=jax.ShapeDtypeStruct((N_CHIPS, r, wd), g.dtype),
        compiler_params=_params(("parallel", "parallel")),
    )(kind, g, half)


def _chip_copies(sums_ref, land_ref, send_sems, recv_sems):
    (x, y, c), me, other_core, chips = _two_level_places()
    return [pltpu.make_async_remote_copy(
        src_ref=sums_ref.at[cx * 2 + cy], dst_ref=land_ref.at[x * 2 + y], send_sem=send_sems.at[k],
        recv_sem=recv_sems.at[k], device_id=(cx, cy, c), device_id_type=pl.DeviceIdType.MESH)
        for k, (cx, cy) in enumerate(chips)]


def chip_start(name, sums):
    land = lax.empty(sums.shape, sums.dtype)

    def body(sums_ref, land_ref, send_sems, recv_sems, sums_out, land_out, token):
        for cp in _chip_copies(sums_ref, land_ref, send_sems, recv_sems):
            cp.start()
        token[...] = jnp.zeros_like(token)

    sem = pltpu.SemaphoreType.DMA((N_OTHER_CHIPS,))
    outs = pl.pallas_call(
        body, name=name,
        out_shape=(sem, sem, pltpu.HBM(sums.shape, sums.dtype), pltpu.HBM(land.shape, land.dtype),
                   jax.ShapeDtypeStruct((8, 128), F32)),
        in_specs=[_HBM, _HBM], out_specs=(_SEM, _SEM, _HBM, _HBM, pl.BlockSpec(memory_space=pltpu.VMEM)),
        input_output_aliases={0: 2, 1: 3}, compiler_params=pltpu.CompilerParams(has_side_effects=_DATAFLOW),
    )(pltpu.with_memory_space_constraint(sums, pltpu.HBM), land)
    return list(outs[:4]), outs[4]


def chip_wait(name, handle, after):
    send_sems, recv_sems, sums, land = handle

    def body(sums_ref, land_ref, send_ref, recv_ref, after_ref, sums_out, land_out):
        for cp in _chip_copies(sums_ref, land_ref, send_ref, recv_ref):
            cp.wait_send()
            cp.wait_recv()

    return pl.pallas_call(
        body, name=name, out_shape=(pltpu.HBM(sums.shape, sums.dtype), pltpu.HBM(land.shape, land.dtype)),
        in_specs=[_HBM, _HBM, _SEM, _SEM, pl.BlockSpec(memory_space=pl.ANY)], out_specs=(_HBM, _HBM),
        input_output_aliases={0: 0, 1: 1}, compiler_params=pltpu.CompilerParams(has_side_effects=_DATAFLOW),
    )(sums, land, send_sems, recv_sems, after)


def adamw_reduce(name, parts, w, m, v):
    nl, r, wd = w.shape
    tr = _row_tile(r, ROW_TILE_BYTES // (8 * wd))

    def body(*refs):
        p_refs = refs[:nl]
        w_ref, m_ref, v_ref, g_ref, d_ref, nm_ref, nv_ref = refs[nl:]
        layer = pl.program_id(0)
        for j in range(nl):
            @pl.when(layer == j)
            def _(j=j):
                g = p_refs[j][0].astype(F32)
                for sender in range(1, p_refs[j].shape[0]):
                    g = g + p_refs[j][sender].astype(F32)
                nm = B1 * m_ref[...] + (1.0 - B1) * g
                nv = B2 * v_ref[...] + (1.0 - B2) * jnp.square(g)
                m_hat = nm / (1.0 - B1 ** STEP)
                v_hat = nv / (1.0 - B2 ** STEP)
                g_ref[...] = g
                d_ref[...] = -LR * (m_hat / (jnp.sqrt(v_hat) + ADAM_EPS) + WD * w_ref[...])
                nm_ref[...] = nm
                nv_ref[...] = nv

    def part_spec(j):
        return pl.BlockSpec((parts[j].shape[0], tr, wd), lambda l, i: (0, jnp.where(l == j, i, 0), 0))

    spec = pl.BlockSpec((None, tr, wd), lambda l, i: (l, i, 0))
    return pl.pallas_call(
        body, name=name, grid=(nl, r // tr),
        in_specs=[part_spec(j) for j in range(nl)] + [spec, spec, spec],
        out_specs=[spec] * 4, out_shape=[jax.ShapeDtypeStruct((nl, r, wd), F32)] * 4,
        compiler_params=_params(("arbitrary", "arbitrary")),
    )(*parts, w, m, v)


def _pack_rows(vectors, rows=None):
    flat = jnp.concatenate([a.reshape(-1).astype(F32) for a in vectors])
    n = flat.shape[0]
    if rows is None:
        rows = -(-n // 1024) * 8
    return jnp.pad(flat, (0, rows * 128 - n)).reshape(rows, 128)


def _unpack_rows(packed, like):
    flat = packed.reshape(-1)
    out, pos = [], 0
    for a in like:
        out.append(flat[pos:pos + a.size].reshape(a.shape))
        pos += a.size
    return out


def kernel(x, p, mix_norm, mlp_norm, ple_norm, w_a_in, a_lb_logits, a_head_gain, w_a_out, kv_norm, w_kvf, b_f, w_b_q, w_b_out, w_mlp_up, w_mlp_down, w_ple_gate, w_ple_up, final_norm, loss_target, m_mix_norm, m_mlp_norm, m_ple_norm, m_w_a_in, m_a_lb_logits, m_a_head_gain, m_w_a_out, m_kv_norm, m_w_kvf, m_b_f, m_w_b_q, m_w_b_out, m_w_mlp_up, m_w_mlp_down, m_w_ple_gate, m_w_ple_up, m_final_norm, v_mix_norm, v_mlp_norm, v_ple_norm, v_w_a_in, v_a_lb_logits, v_a_head_gain, v_w_a_out, v_kv_norm, v_w_kvf, v_b_f, v_w_b_q, v_w_b_out, v_w_mlp_up, v_w_mlp_down, v_w_ple_gate, v_w_ple_up, v_final_norm):
    t, d = x.shape[1], x.shape[2]
    nh = d // HEAD_DIM
    n_layers = 2
    x2 = x.reshape(t, d)
    target = loss_target.reshape(t, d)
    me = _my_index()

    shards = {"w_a_in": w_a_in[0], "w_a_out": w_a_out[0], "w_kvf": w_kvf, "w_b_q": w_b_q[0], "w_b_out": w_b_out[0]}
    for l in range(n_layers):
        shards.update({f"w_mlp_up{l}": w_mlp_up[l], f"w_mlp_down{l}": w_mlp_down[l],
                       f"w_ple_gate{l}": w_ple_gate[l], f"w_ple_up{l}": w_ple_up[l]})
    first_use = ["a_lb_logits", "w_ple_up0", "w_ple_up1", "w_a_in", "w_a_out", "w_mlp_up0", "w_mlp_down0", "w_ple_gate0",
                 "w_kvf", "w_b_q", "w_b_out", "w_mlp_up1", "w_mlp_down1", "w_ple_gate1"]
    row_sharded = ("w_a_out", "w_b_q", "w_b_out", "w_mlp_down", "w_ple_gate")
    shards_bf = [a_lb_logits] + [shards[n].astype(BF16) for n in first_use[1:]]
    ag_handles = gather_start("ag_start", [
        lax.dynamic_update_slice(lax.empty((N_DEV,) + a.shape, a.dtype), a[None], (me, 0, 0)) for a in shards_bf])
    passed_on = {}
    weights = {}

    def pass_on(j, after):
        if j < len(first_use) and j not in passed_on:
            passed_on[j] = gather_pass_on("ag_pass_" + first_use[j], ag_handles[j], after)

    def weight(name, after=None, ahead=True):
        if name not in weights:
            j = first_use.index(name)
            pass_on(j, after)
            if ahead:
                pass_on(j + 1, after)
            behind = passed_on[j + 1][1] if j + 1 in passed_on else after
            g = gather_wait("ag_wait_" + name, passed_on[j][0], behind)
            if name.rstrip("01") in row_sharded:
                g = g.reshape(1, g.shape[0] * g.shape[1], g.shape[2])
            weights[name] = g
        return weights[name]

    lgt = weight("a_lb_logits", x2).transpose(1, 0, 2).reshape(2, d)
    p_bf = [p[l, 0].astype(BF16) for l in range(n_layers)]
    pu_early = [mm_nn(f"ple_up{l}", p_bf[l], weight(f"w_ple_up{l}", x2, ahead=l + 1 < n_layers))
                for l in range(n_layers)]

    def row(vec):
        return vec.reshape(1, -1)

    def mlp_ple_fwd(l, h_in, a):
        (h_a, u_mlp), _ = rowwise(f"add_norm_mlp{l}", _add_norm_fwd, [h_in, a], [row(mlp_norm[l])])
        pre, act = mm_nn(f"mlp_up{l}", u_mlp, weight(f"w_mlp_up{l}", u_mlp), fuse=(_relu2, (), (BF16, BF16)))
        mo = mm_nn(f"mlp_down{l}", act, weight(f"w_mlp_down{l}", act))
        (h_b, u_ple), _ = rowwise(f"add_norm_ple{l}", _add_norm_fwd, [h_a, mo], [row(ple_norm[l])])
        gpre = mm_nn(f"ple_gate{l}", u_ple, weight(f"w_ple_gate{l}", u_ple))
        return dict(h_a=h_a, u_mlp=u_mlp, pre=pre, act=act, h_b=h_b, u_ple=u_ple, gpre=gpre, pu=pu_early[l])

    (u0,), _ = rowwise("norm_mix0", _norm_fwd, [x2], [row(mix_norm[0])])
    z = mm_nn("a_in", u0, weight("w_a_in", pu_early[-1]))
    og, states = hgrn_fwd(z, lgt, a_head_gain)
    a0 = mm_nn("a_out", og, weight("w_a_out", og))
    s0 = mlp_ple_fwd(0, x2, a0)
    (h3, u_kv, u1), _ = rowwise("ple_norms", _ple_two_norms_fwd, [s0["h_b"], s0["gpre"], s0["pu"]],
                                [row(kv_norm), row(mix_norm[1])])
    w_kvf_cols = weight("w_kvf", u_kv).transpose(1, 0, 2).reshape(d, 2 * d + nh)
    w_kv = w_kvf_cols[:, :2 * d].reshape(d, 2, d).transpose(1, 0, 2)
    w_f = jnp.pad(w_kvf_cols[:, 2 * d:], ((0, 0), (0, HEAD_DIM - nh)))[None]
    kv = mm_nn("kvf", u_kv, w_kv, out_dtype=BF16)
    fl_t = mm_nn("kvf_forget", u_kv, w_f)[:, :nh].T
    b_f_col = b_f.reshape(nh, 1)
    dcum = decay_fwd(fl_t, b_f_col)
    dcol, drow = dcum.reshape(nh, t, 1), dcum.reshape(nh, 1, t)
    q = mm_nn("b_q", u1, weight("w_b_q", dcum))
    o, lse = attn_fwd(q, kv, dcol, drow)
    a1 = mm_nn("b_out", o, weight("w_b_out", o))
    s1 = mlp_ple_fwd(1, h3, a1)

    (dh, dgpre, dpu), (d_final, loss_rows) = rowwise(
        "tail", _tail_fwd_bwd, [s1["h_b"], s1["gpre"], s1["pu"], target], [row(final_norm)])

    sent = {}
    tokens = []

    two_level = ("w_mlp_up0", "w_a_in")
    swapping = []

    def send_grad(name, g):
        g = g.reshape(N_DEV, -1, g.shape[-1])
        if name in two_level:
            sent[name], token = pair_start("rs_pair_" + name, g)
            swapping.append(name)
        else:
            (sent[name],), token = scatter_start("rs_start_" + name, [g])
        tokens.append(token)

    def send_grads_together(grads):
        names = list(grads)
        handles, token = scatter_start("rs_start_" + names[0], [
            grads[n].reshape(N_DEV, -1, grads[n].shape[-1]) for n in names])
        sent.update(zip(names, handles))
        tokens.append(token)

    def second_stage(after):
        for name in swapping:
            g, half = pair_wait("rs_pairwait_" + name, sent[name], after)
            sent[name], token = chip_start("rs_chip_" + name, pair_sum("rs_sum_" + name, g, half))
            tokens.append(token)
        swapping.clear()

    def after_sends():
        deps = tuple(tokens)
        tokens.clear()
        return deps

    def mlp_ple_bwd(l, s, dh, dgpre, dpu):
        send_grads_together({
            f"w_ple_gate{l}": mm_tn(f"d_ple_gate_w{l}", s["u_ple"], dgpre, 1, deps=after_sends()),
            f"w_ple_up{l}": mm_tn(f"d_ple_up_w{l}", p_bf[l], dpu, N_DEV)})
        du = mm_nt(f"d_ple_gate_x{l}", dgpre, weight(f"w_ple_gate{l}"), out_dtype=BF16, deps=after_sends())
        (dh, dh_bf), (d_ple,) = rowwise(f"d_norm_ple{l}", _norm_bwd, [s["h_b"], du, dh], [row(ple_norm[l])])
        send_grad(f"w_mlp_down{l}", mm_tn(f"d_mlp_down_w{l}", s["act"], dh_bf, 1))
        (dpre,) = mm_nt(f"d_mlp_down_x{l}", dh_bf, weight(f"w_mlp_down{l}"), deps=after_sends(),
                        fuse=(_relu2_bwd, (s["pre"],), (BF16,)))
        second_stage(dpre)
        send_grad(f"w_mlp_up{l}", mm_tn(f"d_mlp_up_w{l}", s["u_mlp"], dpre, N_DEV))
        du = mm_nt(f"d_mlp_up_x{l}", dpre, weight(f"w_mlp_up{l}"), out_dtype=BF16, deps=after_sends())
        second_stage(du)
        (dh, dh_bf), (d_mlp,) = rowwise(f"d_norm_mlp{l}", _norm_bwd, [s["h_a"], du, dh], [row(mlp_norm[l])])
        return dh, dh_bf, d_ple, d_mlp

    dh, dh_bf, d_ple1, d_mlp1 = mlp_ple_bwd(1, s1, dh, dgpre, dpu)
    send_grad("w_b_out", mm_tn("d_b_out_w", o, dh_bf, 1))
    do = mm_nt("d_b_out_x", dh_bf, weight("w_b_out"), out_dtype=BF16, deps=after_sends())
    dq, dk, dv, ddrow = attn_bwd(q, kv, dcol, drow, lse, do)
    send_grad("w_b_q", mm_tn("d_b_q_w", u1, dq, 1))
    du1 = mm_nt("d_b_q_x", dq, weight("w_b_q"), out_dtype=BF16, deps=after_sends())
    dfl_t, d_b_f = decay_bwd(fl_t, b_f_col, ddrow.reshape(nh, t))
    dkv = jnp.stack([dk, dv]).astype(BF16)
    dfl = jnp.pad(dfl_t.T, ((0, 0), (0, HEAD_DIM - nh))).astype(BF16)
    d_w_kv = mm_tn("d_kvf_w", u_kv, dkv, 2)
    d_w_f = mm_tn("d_kvf_forget_w", u_kv, dfl, 1)
    d_w_kvf = jnp.concatenate([d_w_kv[0], d_w_kv[1], d_w_f[0, :, :nh]], axis=1)
    per_owner = (2 * d + nh) // N_DEV
    send_grad("w_kvf", jnp.stack([d_w_kvf[:, p * per_owner:(p + 1) * per_owner] for p in range(N_DEV)]))
    du_f = mm_nt("d_kvf_forget_x", dfl, w_f, deps=after_sends())
    (du_kv,) = mm_nt("d_kvf_x", dkv, w_kv, fuse=(lambda acc, extra: (acc + extra,), (du_f,), (BF16,)))
    second_stage(du_kv)
    (dh, dgpre, dpu), (d_kv_norm, d_mix1) = rowwise(
        "d_ple_norms", _two_norms_ple_bwd, [h3, du_kv, du1, dh, s0["gpre"], s0["pu"]],
        [row(kv_norm), row(mix_norm[1])])
    dh, dh_bf, d_ple0, d_mlp0 = mlp_ple_bwd(0, s0, dh, dgpre, dpu)
    send_grad("w_a_out", mm_tn("d_a_out_w", og, dh_bf, 1))
    dog = mm_nt("d_a_out_x", dh_bf, weight("w_a_out"), out_dtype=BF16, deps=after_sends())
    dz4, d_lgt, d_hg = hgrn_bwd(z, lgt, a_head_gain, states, dog)
    send_grad("w_a_in", mm_tn("d_a_in_w", u0, dz4, N_DEV, stacked=True))
    du0 = mm_nt("d_a_in_x", dz4, weight("w_a_in"), out_dtype=BF16, deps=after_sends(), stacked=True)
    second_stage(du0)
    (dx, _), (d_mix0,) = rowwise("d_norm_mix0", _norm_bwd, [x2, du0, dh], [row(mix_norm[0])])

    new = {}
    last = [dx]

    def update(name, parts, w, m, v):
        shp = w.shape
        w3, m3, v3 = (a.reshape(len(parts), -1, shp[-1]) for a in (w, m, v))
        new[name] = tuple(a.reshape(shp) for a in adamw_reduce("adamw_" + name, parts, w3, m3, v3))
        last[0] = new[name][0]

    def receive_update(name, layers, w, m, v):
        parts = {}
        for sfx in layers:
            if name + sfx in two_level:
                mine, land = chip_wait(f"rs_wait_{name}{sfx}", sent[name + sfx], last[0])
                chip = me // 2
                parts[sfx] = lax.dynamic_update_slice(land, lax.dynamic_slice_in_dim(mine, chip, 1, 0), (chip, 0, 0))
            else:
                _, parts[sfx] = scatter_wait(f"rs_wait_{name}{sfx}", sent[name + sfx], last[0])
        update(name, [parts[sfx] for sfx in sorted(layers)], w, m, v)

    both = ("1", "0")
    receive_update("w_b_out", ("",), w_b_out, m_w_b_out, v_w_b_out)
    receive_update("w_b_q", ("",), w_b_q, m_w_b_q, v_w_b_q)
    receive_update("w_kvf", ("",), w_kvf, m_w_kvf, v_w_kvf)
    receive_update("w_ple_gate", both, w_ple_gate, m_w_ple_gate, v_w_ple_gate)
    receive_update("w_ple_up", both, w_ple_up, m_w_ple_up, v_w_ple_up)
    receive_update("w_mlp_down", both, w_mlp_down, m_w_mlp_down, v_w_mlp_down)
    receive_update("w_mlp_up", both, w_mlp_up, m_w_mlp_up, v_w_mlp_up)
    receive_update("w_a_out", ("",), w_a_out, m_w_a_out, v_w_a_out)
    receive_update("w_a_in", ("",), w_a_in, m_w_a_in, v_w_a_in)

    small = dict(mix_norm=jnp.concatenate([d_mix0, d_mix1]), mlp_norm=jnp.concatenate([d_mlp0, d_mlp1]),
                 ple_norm=jnp.concatenate([d_ple0, d_ple1]), a_head_gain=d_hg, kv_norm=d_kv_norm.reshape(d),
                 b_f=d_b_f.reshape(nh), final_norm=d_final.reshape(d))
    small_w = dict(mix_norm=(mix_norm, m_mix_norm, v_mix_norm), mlp_norm=(mlp_norm, m_mlp_norm, v_mlp_norm),
                   ple_norm=(ple_norm, m_ple_norm, v_ple_norm),
                   a_head_gain=(a_head_gain, m_a_head_gain, v_a_head_gain), kv_norm=(kv_norm, m_kv_norm, v_kv_norm),
                   b_f=(b_f, m_b_f, v_b_f), final_norm=(final_norm, m_final_norm, v_final_norm))
    names = list(small)
    packed = _pack_rows([d_lgt] + [small[n] for n in names])
    everyone = all_gather("ag_small_grads", packed, deps=(last[0],))
    n_lgt_rows = d_lgt.size // 128
    lgt_parts = everyone[:, :n_lgt_rows].reshape(N_DEV, 2, d)
    lgt_parts = lax.dynamic_slice_in_dim(lgt_parts, me * a_lb_logits.shape[1], a_lb_logits.shape[1], axis=2)
    update("a_lb_logits", [lgt_parts], a_lb_logits, m_a_lb_logits, v_a_lb_logits)
    rest = everyone[:, n_lgt_rows:]
    like = [small_w[n][0] for n in names]
    packed_w, packed_m, packed_v = (_pack_rows([small_w[n][j] for n in names], rest.shape[1])[None] for j in range(3))
    outs = adamw_reduce("adamw_small", [rest], packed_w, packed_m, packed_v)
    unpacked = [_unpack_rows(a, like) for a in outs]
    for j, n in enumerate(names):
        new[n] = tuple(unpacked[q][j] for q in range(4))

    order = ["mix_norm", "mlp_norm", "ple_norm", "w_a_in", "a_lb_logits", "a_head_gain", "w_a_out", "kv_norm",
             "w_kvf", "b_f", "w_b_q", "w_b_out", "w_mlp_up", "w_mlp_down", "w_ple_gate", "w_ple_up", "final_norm"]
    loss_here, _ = lax.optimization_barrier((loss_rows[0, 0], new["final_norm"][0]))
    loss = lax.psum(loss_here, MESH_AXES)
    result = [loss, dx.reshape(x.shape)]
    for j in range(4):
        result += [new[n][j] for n in order]
    return tuple(result)
```

```python
import jax
import jax.numpy as jnp
from jax import lax
from jax.experimental import pallas as pl
from jax.experimental.pallas import tpu as pltpu

F32 = jnp.float32
BF16 = jnp.bfloat16
HEAD_DIM = 128
CHUNK = 16
TILE = 256
HEADS_PER_STEP = 4
NORM_EPS = 1e-6
N_DEV = 8
MESH_AXES = ("x", "y", "c")
VMEM_LIMIT_BYTES = 48 * 1024 * 1024
ROW_TILE_BYTES = 2 * 1024 * 1024
LR, B1, B2, ADAM_EPS, WD, STEP = 0.001, 0.9, 0.999, 1e-08, 0.01, 10
NEG_BIG = -1e30

NN = (((1,), (0,)), ((), ()))
NT = (((1,), (1,)), ((), ()))
TN = (((0,), (0,)), ((), ()))


def _params(semantics):
    return pltpu.CompilerParams(dimension_semantics=semantics, vmem_limit_bytes=VMEM_LIMIT_BYTES)


def _tile(n, prefs):
    for p in prefs:
        if n % p == 0:
            return p
    return n


def _row_tile(rows, limit):
    for cand in (2048, 1024, 512, 256, 128, 64, 32, 16):
        if cand <= limit and rows % cand == 0:
            return cand
    return rows


def _mm_call(name, a, b, dims, grid, a_spec, b_spec, o_spec, o_shape, acc_shape, k_axes, out_dtype, deps=(),
             fuse=None, split=1):
    nk = 1
    for ax in k_axes:
        nk *= grid[ax]
    fn, extra, out_dtypes = fuse if fuse else (lambda acc: (acc,), (), (out_dtype,))
    n_extra, n_out = len(extra), len(out_dtypes)

    def finish(acc, rest):
        o_refs = rest[n_extra + len(deps):n_extra + len(deps) + n_out]
        for ref, val in zip(o_refs, fn(acc, *[r[...] for r in rest[:n_extra]])):
            ref[...] = val.astype(ref.dtype)

    def product(a_ref, b_ref):
        if split == 1:
            return lax.dot_general(a_ref[...], b_ref[...], dims, preferred_element_type=F32)
        wide = a_ref.shape[1] // split
        return sum(lax.dot_general(a_ref[:, q * wide:(q + 1) * wide], b_ref[q], dims, preferred_element_type=F32)
                   for q in range(split))

    def one_step(a_ref, b_ref, *rest):
        finish(product(a_ref, b_ref), rest)

    def accumulate(a_ref, b_ref, *rest):
        acc_ref = rest[-1]
        k = 0
        for ax in k_axes:
            k = k * grid[ax] + pl.program_id(ax)
        part = product(a_ref, b_ref)

        @pl.when(k == 0)
        def _():
            acc_ref[...] = part

        @pl.when((k > 0) & (k < nk - 1))
        def _():
            acc_ref[...] += part

        @pl.when(k == nk - 1)
        def _():
            finish(acc_ref[...] + part, rest)

    sem = tuple("arbitrary" if ax in k_axes else "parallel" for ax in range(len(grid)))
    outs = pl.pallas_call(
        one_step if nk == 1 else accumulate, name=name, grid=grid,
        in_specs=[a_spec, b_spec] + [o_spec] * n_extra + [pl.BlockSpec(memory_space=pl.ANY)] * len(deps),
        out_specs=[o_spec] * n_out, out_shape=[jax.ShapeDtypeStruct(o_shape, dt) for dt in out_dtypes],
        scratch_shapes=[] if nk == 1 else [pltpu.VMEM(acc_shape, F32)], compiler_params=_params(sem),
    )(a, b, *extra, *deps)
    return outs if fuse else outs[0]


def mm_nn(name, a, b3, out_dtype=F32, out3=False, deps=(), fuse=None):
    m, k = a.shape
    g, _, n = b3.shape
    tm, tk = _tile(m, (1024, 512, 256)), _tile(k, (2048, 1024, 512, 256))
    tn = n if out3 else _tile(n, (1024, 512, 256, 128))
    nj = n // tn
    grid = (m // tm, g, nj, k // tk)
    a_spec = pl.BlockSpec((tm, tk), lambda i, gg, j, kk: (i, kk))
    b_spec = pl.BlockSpec((None, tk, tn), lambda i, gg, j, kk: (gg, kk, j))
    if out3:
        o_spec = pl.BlockSpec((None, tm, tn), lambda i, gg, j, kk: (gg, i, j))
        o_shape = (g, m, n)
    else:
        o_spec = pl.BlockSpec((tm, tn), lambda i, gg, j, kk: (i, gg * nj + j))
        o_shape = (m, g * n)
    return _mm_call(name, a, b3, NN, grid, a_spec, b_spec, o_spec, o_shape, (tm, tn), (3,), out_dtype, deps, fuse)


def mm_nt(name, a, b3, out_dtype=F32, deps=(), fuse=None, stacked=False):
    g, k, n = b3.shape
    a3 = a.ndim == 3 and not stacked
    m = a.shape[1] if a.ndim == 3 else a.shape[0]
    tm, tko = _tile(m, (1024, 512, 256)), _tile(k, (1024, 512, 256))
    tc = n if a3 else _tile(n, (2048, 1024, 512, 256, 128))
    nc = n // tc
    per = g // a.shape[0] if stacked else g
    pair = 2 if (not a3 and nc == 1 and per % 2 == 0 and tc <= 1024) else 1
    grid = (m // tm, k // tko, g // pair, nc)
    if a3:
        a_spec = pl.BlockSpec((None, tm, tc), lambda i, j, gg, c: (gg, i, c))
    elif stacked:
        a_spec = pl.BlockSpec((None, tm, pair * tc),
                              lambda i, j, gg, c: ((gg * pair) // per, i, (((gg * pair) % per) // pair) * nc + c))
    else:
        a_spec = pl.BlockSpec((tm, pair * tc), lambda i, j, gg, c: (i, gg * nc + c))
    if pair == 1:
        b_spec = pl.BlockSpec((None, tko, tc), lambda i, j, gg, c: (gg, j, c))
    else:
        b_spec = pl.BlockSpec((pair, tko, tc), lambda i, j, gg, c: (gg, j, c))
    o_spec = pl.BlockSpec((tm, tko), lambda i, j, gg, c: (i, j))
    return _mm_call(name, a, b3, NT, grid, a_spec, b_spec, o_spec, (m, k), (tm, tko), (2, 3), out_dtype, deps, fuse,
                    pair)


def mm_tn(name, a, b, g, out_dtype=BF16, deps=(), stacked=False):
    t, k = a.shape
    b3 = b.ndim == 3 and not stacked
    n = b.shape[2] if b3 else (b.shape[0] * b.shape[2] if stacked else b.shape[1]) // g
    tm = _tile(k, (1024, 512, 256))
    tn = n if b3 else _tile(n, (1024, 512, 256, 128))
    tt = _tile(t, (2048, 1024, 512, 256))
    nj = n // tn
    grid = (g, k // tm, nj, t // tt)
    a_spec = pl.BlockSpec((tt, tm), lambda gg, i, j, s: (s, i))
    if b3:
        b_spec = pl.BlockSpec((None, tt, tn), lambda gg, i, j, s: (gg, s, j))
    elif stacked:
        per = g // b.shape[0]
        b_spec = pl.BlockSpec((None, tt, tn), lambda gg, i, j, s: (gg // per, s, (gg % per) * nj + j))
    else:
        b_spec = pl.BlockSpec((tt, tn), lambda gg, i, j, s: (s, gg * nj + j))
    o_spec = pl.BlockSpec((None, tm, tn), lambda gg, i, j, s: (gg, i, j))
    return _mm_call(name, a, b, TN, grid, a_spec, b_spec, o_spec, (g, k, n), (tm, tn), (3,), out_dtype, deps)


def rowwise(name, fn, rows, vecs=()):
    t = rows[0].shape[0]
    wmax = max(r.shape[1] for r in rows)
    tr = _row_tile(t, ROW_TILE_BYTES // (4 * wmax))
    row_s = [jax.ShapeDtypeStruct((tr, r.shape[1]), r.dtype) for r in rows]
    vec_s = [jax.ShapeDtypeStruct(v.shape, v.dtype) for v in vecs]
    out_rows_s, out_sums_s = jax.eval_shape(fn, *row_s, *vec_s)
    n_in, n_r = len(rows) + len(vecs), len(out_rows_s)

    def body(*refs):
        i = pl.program_id(0)
        o_rows, o_sums = fn(*[r[...] for r in refs[:n_in]])
        for ref, val in zip(refs[n_in:n_in + n_r], o_rows):
            ref[...] = val

        if out_sums_s:
            @pl.when(i == 0)
            def _():
                for ref in refs[n_in + n_r:]:
                    ref[...] = jnp.zeros_like(ref)

            for ref, val in zip(refs[n_in + n_r:], o_sums):
                ref[...] += val

    in_specs = [pl.BlockSpec((tr, r.shape[1]), lambda i: (i, 0)) for r in rows]
    in_specs += [pl.BlockSpec(v.shape, lambda i: (0, 0)) for v in vecs]
    out_specs = [pl.BlockSpec((tr, s.shape[1]), lambda i: (i, 0)) for s in out_rows_s]
    out_specs += [pl.BlockSpec(s.shape, lambda i: (0, 0)) for s in out_sums_s]
    out_shape = [jax.ShapeDtypeStruct((t, s.shape[1]), s.dtype) for s in out_rows_s]
    out_shape += [jax.ShapeDtypeStruct(s.shape, s.dtype) for s in out_sums_s]
    outs = pl.pallas_call(
        body, name=name, grid=(t // tr,), in_specs=in_specs, out_specs=out_specs, out_shape=out_shape,
        compiler_params=_params(("arbitrary",)),
    )(*rows, *vecs)
    return outs[:n_r], outs[n_r:]


def _rms(x, gain):
    return x * lax.rsqrt(jnp.mean(x * x, axis=-1, keepdims=True) + NORM_EPS) * gain


def _norm_fwd(x, gain):
    return (_rms(x, gain).astype(BF16),), ()


def _add_norm_fwd(h, a, gain):
    h = h + a
    return (h, _rms(h, gain).astype(BF16)), ()


def _relu2(pre):
    r = jnp.maximum(pre, 0.0)
    return pre, r * r


def _ple(h, gpre, pu):
    return h + pu * jax.nn.sigmoid(gpre)


def _ple_two_norms_fwd(h, gpre, pu, gain_a, gain_b):
    h = _ple(h, gpre, pu)
    return (h, _rms(h, gain_a).astype(BF16), _rms(h, gain_b).astype(BF16)), ()


def _tail_fwd_bwd(h, gpre, pu, target, gain):
    def row_loss(h, gpre, pu, gain):
        y = _rms(_ple(h, gpre, pu), gain)
        return 0.5 * jnp.mean(jnp.square(y - target), axis=-1, keepdims=True)

    loss, vjp = jax.vjp(row_loss, h, gpre, pu, gain)
    dh, dgpre, dpu, dgain = vjp(jnp.ones_like(loss))
    loss = jnp.broadcast_to(jnp.sum(loss, axis=0, keepdims=True), (1, 128))
    return (dh, dgpre.astype(BF16), dpu.astype(BF16)), (dgain, loss)


def _norm_bwd(h, du, dh_in, gain):
    _, vjp = jax.vjp(_rms, h, gain)
    dh, dgain = vjp(du.astype(F32))
    dh = dh_in + dh
    return (dh, dh.astype(BF16)), (dgain,)


def _two_norms_ple_bwd(h, du_a, du_b, dh_in, gpre, pu, gain_a, gain_b):
    _, vjp = jax.vjp(lambda h, ga, gb: (_rms(h, ga), _rms(h, gb)), h, gain_a, gain_b)
    dh, dga, dgb = vjp((du_a.astype(F32), du_b.astype(F32)))
    dh = dh_in + dh
    _, gate_vjp = jax.vjp(lambda g, u: u * jax.nn.sigmoid(g), gpre, pu)
    dgpre, dpu = gate_vjp(dh)
    return (dh, dgpre.astype(BF16), dpu.astype(BF16)), (dga, dgb)


def _relu2_bwd(dact, pre):
    return (dact * 2.0 * jnp.maximum(pre.astype(F32), 0.0),)


def _bf16_dot(dims_fwd, dims_da, dims_db, swap_da, swap_db):
    @jax.custom_vjp
    def dot(a, b):
        return lax.dot_general(a.astype(BF16), b.astype(BF16), dims_fwd, preferred_element_type=F32)

    def fwd(a, b):
        return dot(a, b), (a, b)

    def bwd(res, ct):
        a, b = res
        ct, a, b = ct.astype(BF16), a.astype(BF16), b.astype(BF16)
        da = lax.dot_general(*((b, ct) if swap_da else (ct, b)), dims_da, preferred_element_type=F32)
        db = lax.dot_general(*((ct, a) if swap_db else (a, ct)), dims_db, preferred_element_type=F32)
        return da, db

    dot.defvjp(fwd, bwd)
    return dot


_dot_nn = _bf16_dot(NN, NT, TN, False, False)
_dot_nt = _bf16_dot(NT, NN, TN, False, True)
_dot_tn = _bf16_dot(TN, NT, NN, True, False)


def _chunk_causal_mask():
    r = lax.broadcasted_iota(jnp.int32, (TILE, TILE), 0)
    c = lax.broadcasted_iota(jnp.int32, (TILE, TILE), 1)
    return ((r // CHUNK) == (c // CHUNK)) & (c <= r)


def _chunk_scan(x, reverse):
    pos = lax.broadcasted_iota(jnp.int32, x.shape, 0) % CHUNK
    step = 1
    while step < CHUNK:
        if reverse:
            x = x + jnp.where(pos < CHUNK - step, pltpu.roll(x, x.shape[0] - step, axis=0), 0.0)
        else:
            x = x + jnp.where(pos >= step, pltpu.roll(x, step, axis=0), 0.0)
        step *= 2
    return x


def _chunk_total(x):
    return _chunk_scan(x, False) + _chunk_scan(x, True) - x


@jax.custom_vjp
def _chunk_sums(x):
    return _chunk_scan(x, False), _chunk_total(x)


def _chunk_sums_fwd(x):
    return _chunk_sums(x), None


def _chunk_sums_bwd(_, ct):
    return (_chunk_scan(ct[0], True) + _chunk_total(ct[1]),)


_chunk_sums.defvjp(_chunk_sums_fwd, _chunk_sums_bwd)


def _hgrn_tile(q, f, i, g, lgt, hg, st):
    d = q.shape[1]
    l0, l1 = lgt[0:1], lgt[1:2]
    mx = jnp.maximum(l0, l1)
    e0, e1 = jnp.exp(l0 - mx), jnp.exp(l1 - mx)
    lb = e0 / (e0 + e1)
    fg = lb + (1.0 - lb) * jax.nn.sigmoid(f)
    k = 1.0 - fg
    causal = _chunk_causal_mask()
    b, b_last = _chunk_sums(jnp.log(fg))
    q_in = q * jax.nn.sigmoid(q) * (d ** -0.5) * jnp.exp(b)
    k_in = k * jnp.exp(-b)
    k_end = k * jnp.exp(b_last - b)
    att = jnp.where(causal, _dot_nt(q_in, k_in), 0.0)
    o_intra = _dot_nn(att, i)
    n_chunks = TILE // CHUNK
    chunk_of_row = lax.broadcasted_iota(jnp.int32, (TILE, 1), 0) // CHUNK

    def spread(a):
        return jnp.concatenate([jnp.where(chunk_of_row == n, a, 0.0) for n in range(n_chunks)], axis=1)

    increments = _dot_tn(i, spread(k_end))
    states = []
    for n in range(n_chunks):
        states.append(st)
        decay = jnp.exp(jnp.mean(b_last[n * CHUNK:(n + 1) * CHUNK], axis=0, keepdims=True))
        st = st * decay + increments[:, n * d:(n + 1) * d]
    o = o_intra + _dot_nt(spread(q_in), jnp.concatenate(states, axis=1))
    o = o * lax.rsqrt(jnp.mean(o * o, axis=-1, keepdims=True) + NORM_EPS) * hg
    return o * (g * jax.nn.sigmoid(g)), st


def hgrn_fwd(z, lgt, hg):
    t, d4 = z.shape
    d = d4 // 4
    nh, nt = d // HEAD_DIM, t // TILE
    hp = HEADS_PER_STEP
    wide = hp * HEAD_DIM

    def body(q_ref, f_ref, i_ref, g_ref, lgt_ref, hg_ref, o_ref, st_out_ref, st_ref):
        tt = pl.program_id(1)

        @pl.when(tt == 0)
        def _():
            st_ref[...] = jnp.zeros_like(st_ref)

        for hh in range(hp):
            cols = slice(hh * HEAD_DIM, (hh + 1) * HEAD_DIM)
            st = st_ref[hh]
            st_out_ref[hh] = st
            o, st = _hgrn_tile(q_ref[:, cols], f_ref[:, cols], i_ref[:, cols], g_ref[:, cols], lgt_ref[:, cols],
                               hg_ref[...], st)
            o_ref[:, cols] = o.astype(o_ref.dtype)
            st_ref[hh] = st

    def part(p):
        return pl.BlockSpec((TILE, wide), lambda h, tt: (tt, p * (nh // hp) + h))

    return pl.pallas_call(
        body, name="hgrn_fwd", grid=(nh // hp, nt),
        in_specs=[part(0), part(1), part(2), part(3),
                  pl.BlockSpec((2, wide), lambda h, tt: (0, h)),
                  pl.BlockSpec((1, HEAD_DIM), lambda h, tt: (0, 0))],
        out_specs=[pl.BlockSpec((TILE, wide), lambda h, tt: (tt, h)),
                   pl.BlockSpec((hp, None, HEAD_DIM, HEAD_DIM), lambda h, tt: (h, tt, 0, 0))],
        out_shape=[jax.ShapeDtypeStruct((t, d), BF16),
                   jax.ShapeDtypeStruct((nh, nt, HEAD_DIM, HEAD_DIM), F32)],
        scratch_shapes=[pltpu.VMEM((hp, HEAD_DIM, HEAD_DIM), F32)],
        compiler_params=_params(("parallel", "arbitrary")),
    )(z, z, z, z, lgt, hg)


def hgrn_bwd(z, lgt, hg, states, dout):
    t, d4 = z.shape
    d = d4 // 4
    nh, nt = d // HEAD_DIM, t // TILE
    hp = HEADS_PER_STEP
    wide = hp * HEAD_DIM

    def body(q_ref, f_ref, i_ref, g_ref, lgt_ref, hg_ref, st_in_ref, do_ref, dz_ref, dlgt_ref, dhg_ref, dst_ref):
        h, tt = pl.program_id(0), pl.program_id(1)

        @pl.when(tt == 0)
        def _():
            dst_ref[...] = jnp.zeros_like(dst_ref)
            dlgt_ref[...] = jnp.zeros_like(dlgt_ref)

        @pl.when((tt == 0) & (h == 0))
        def _():
            dhg_ref[...] = jnp.zeros_like(dhg_ref)

        for hh in range(hp):
            cols = slice(hh * HEAD_DIM, (hh + 1) * HEAD_DIM)
            _, vjp = jax.vjp(_hgrn_tile, q_ref[:, cols], f_ref[:, cols], i_ref[:, cols], g_ref[:, cols],
                             lgt_ref[:, cols], hg_ref[...], st_in_ref[hh])
            grads = vjp((do_ref[:, cols].astype(F32), dst_ref[hh]))
            for p in range(4):
                dz_ref[p, :, cols] = grads[p].astype(dz_ref.dtype)
            dlgt_ref[:, cols] += grads[4]
            dhg_ref[...] += grads[5]
            dst_ref[hh] = grads[6]

    def part(p):
        return pl.BlockSpec((TILE, wide), lambda h, tt: (nt - 1 - tt, p * (nh // hp) + h))

    return pl.pallas_call(
        body, name="hgrn_bwd", grid=(nh // hp, nt),
        in_specs=[part(0), part(1), part(2), part(3),
                  pl.BlockSpec((2, wide), lambda h, tt: (0, h)),
                  pl.BlockSpec((1, HEAD_DIM), lambda h, tt: (0, 0)),
                  pl.BlockSpec((hp, None, HEAD_DIM, HEAD_DIM), lambda h, tt: (h, nt - 1 - tt, 0, 0)),
                  pl.BlockSpec((TILE, wide), lambda h, tt: (nt - 1 - tt, h))],
        out_specs=[pl.BlockSpec((4, TILE, wide), lambda h, tt: (0, nt - 1 - tt, h)),
                   pl.BlockSpec((2, wide), lambda h, tt: (0, h)),
                   pl.BlockSpec((1, HEAD_DIM), lambda h, tt: (0, 0))],
        out_shape=[jax.ShapeDtypeStruct((4, t, d), BF16),
                   jax.ShapeDtypeStruct((2, d), F32),
                   jax.ShapeDtypeStruct((1, HEAD_DIM), F32)],
        scratch_shapes=[pltpu.VMEM((hp, HEAD_DIM, HEAD_DIM), F32)],
        compiler_params=_params(("arbitrary", "arbitrary")),
    )(z, z, z, z, lgt, hg, states, dout)


def _log_sigmoid(x):
    return jnp.minimum(x, 0.0) - jnp.log(1.0 + jnp.exp(-jnp.abs(x)))


def decay_fwd(fl_t, b_f):
    nh, t = fl_t.shape

    def body(fl_ref, b_ref, out_ref):
        r = lax.broadcasted_iota(jnp.int32, (128, 128), 0)
        c = lax.broadcasted_iota(jnp.int32, (128, 128), 1)
        upper = (r <= c).astype(F32)
        carry = jnp.zeros((nh, 1), F32)
        for j in range(t // 128):
            cols = slice(j * 128, (j + 1) * 128)
            ls = _log_sigmoid(fl_ref[:, cols] + b_ref[...])
            out_ref[:, cols] = carry + jnp.dot(ls, upper, precision=lax.Precision.HIGHEST,
                                               preferred_element_type=F32)
            carry = carry + jnp.sum(ls, axis=1, keepdims=True)

    return pl.pallas_call(body, name="decay_fwd", out_shape=jax.ShapeDtypeStruct((nh, t), F32),
                          compiler_params=_params(None))(fl_t, b_f)


def decay_bwd(fl_t, b_f, ddcum):
    nh, t = fl_t.shape

    def body(fl_ref, b_ref, dd_ref, dfl_ref, db_ref):
        r = lax.broadcasted_iota(jnp.int32, (128, 128), 0)
        c = lax.broadcasted_iota(jnp.int32, (128, 128), 1)
        lower = (r >= c).astype(F32)
        carry = jnp.zeros((nh, 1), F32)
        db = jnp.zeros((nh, 1), F32)
        for j in reversed(range(t // 128)):
            cols = slice(j * 128, (j + 1) * 128)
            dd = dd_ref[:, cols]
            dls = carry + jnp.dot(dd, lower, precision=lax.Precision.HIGHEST, preferred_element_type=F32)
            carry = carry + jnp.sum(dd, axis=1, keepdims=True)
            dfl = dls * jax.nn.sigmoid(-(fl_ref[:, cols] + b_ref[...]))
            dfl_ref[:, cols] = dfl
            db = db + jnp.sum(dfl, axis=1, keepdims=True)
        db_ref[...] = db

    return pl.pallas_call(body, name="decay_bwd",
                          out_shape=[jax.ShapeDtypeStruct((nh, t), F32), jax.ShapeDtypeStruct((nh, 1), F32)],
                          compiler_params=_params(None))(fl_t, b_f, ddcum)


def _attn_parts(t):
    tq = _tile(t, (256, 128))
    per_part = 2 if t // tq >= 4 else 1
    return tq, [(first, per_part, (first + per_part) * tq) for first in range(0, t // tq, per_part)]


def _attn_logits(q_ref, k_ref, dcol_ref, drow_ref, row0, tq, keys):
    qs = (q_ref[...] * (HEAD_DIM ** -0.5)).astype(BF16)
    s = lax.dot_general(qs, k_ref[...], NT, preferred_element_type=F32)
    s = s + dcol_ref[...] - drow_ref[...]
    row = row0 + lax.broadcasted_iota(jnp.int32, (tq, keys), 0)
    col = lax.broadcasted_iota(jnp.int32, (tq, keys), 1)
    return qs, jnp.where(col <= row, s, NEG_BIG)


def attn_fwd(q, kv, dcol, drow):
    t, d = q.shape
    nh = d // HEAD_DIM
    tq, parts = _attn_parts(t)
    o = lse = None
    for first, count, keys in parts:
        def body(q_ref, k_ref, v_ref, dcol_ref, drow_ref, *rest, first=first, keys=keys):
            o_ref, lse_ref = rest[-2:]
            _, s = _attn_logits(q_ref, k_ref, dcol_ref, drow_ref, (first + pl.program_id(1)) * tq, tq, keys)
            m = jnp.max(s, axis=1, keepdims=True)
            p = jnp.exp(s - m)
            l = jnp.sum(p, axis=1, keepdims=True)
            acc = jnp.dot(p.astype(BF16), v_ref[...], preferred_element_type=F32)
            o_ref[...] = (acc / l).astype(o_ref.dtype)
            lse_ref[...] = m + jnp.log(l)

        tile = pl.BlockSpec((tq, HEAD_DIM), lambda h, i, first=first: (first + i, h))
        col = pl.BlockSpec((None, tq, 1), lambda h, i, first=first: (h, first + i, 0))
        seen_k = pl.BlockSpec((keys, HEAD_DIM), lambda h, i: (0, h))
        seen_v = pl.BlockSpec((keys, HEAD_DIM), lambda h, i: (0, nh + h))
        carried = [] if o is None else [o, lse]
        o, lse = pl.pallas_call(
            body, name=f"attn_fwd_{first}", grid=(nh, count),
            in_specs=[tile, seen_k, seen_v, col, pl.BlockSpec((None, 1, keys), lambda h, i: (h, 0, 0))]
            + [pl.BlockSpec(memory_space=pl.ANY)] * len(carried),
            out_specs=[tile, col],
            out_shape=[jax.ShapeDtypeStruct((t, d), BF16), jax.ShapeDtypeStruct((nh, t, 1), F32)],
            input_output_aliases={5: 0, 6: 1} if carried else {},
            compiler_params=_params(("parallel", "parallel")),
        )(q, kv, kv, dcol, drow, *carried)
    return o, lse


def attn_bwd(q, kv, dcol, drow, lse, do):
    t, d = q.shape
    nh = d // HEAD_DIM
    tq, parts = _attn_parts(t)
    dq = dk = dv = ddrow = None
    for first, count, keys in reversed(parts):
        first_call = dq is None

        def body(q_ref, k_ref, v_ref, dcol_ref, drow_ref, lse_ref, do_ref, *rest, first=first, keys=keys,
                 count=count, first_call=first_call):
            dq_ref, dk_ref, dv_ref, ddrow_ref, dk_acc, dv_acc, ddrow_acc = rest[-7:]
            i = pl.program_id(1)

            @pl.when(i == 0)
            def _():
                if first_call:
                    dk_acc[...] = jnp.zeros_like(dk_acc)
                    dv_acc[...] = jnp.zeros_like(dv_acc)
                    ddrow_acc[...] = jnp.zeros_like(ddrow_acc)
                else:
                    dk_acc[...] = rest[1][...]
                    dv_acc[...] = rest[2][...]
                    ddrow_acc[...] = rest[3][...]

            qs, s = _attn_logits(q_ref, k_ref, dcol_ref, drow_ref, (first + i) * tq, tq, keys)
            p = jnp.exp(s - lse_ref[...])
            do = do_ref[...]
            dp = lax.dot_general(do, v_ref[...], NT, preferred_element_type=F32)
            ds = p * (dp - jnp.sum(p * dp, axis=1, keepdims=True))
            dsb = ds.astype(BF16)
            dq_ref[...] = (jnp.dot(dsb, k_ref[...], preferred_element_type=F32) * (HEAD_DIM ** -0.5)).astype(dq_ref.dtype)
            dk_acc[...] += lax.dot_general(dsb, qs, TN, preferred_element_type=F32)
            dv_acc[...] += lax.dot_general(p.astype(BF16), do, TN, preferred_element_type=F32)
            ddrow_acc[...] -= jnp.sum(ds, axis=0, keepdims=True)

            @pl.when(i == count - 1)
            def _():
                dk_ref[...] = dk_acc[...]
                dv_ref[...] = dv_acc[...]
                ddrow_ref[...] = ddrow_acc[...]

        tile = pl.BlockSpec((tq, HEAD_DIM), lambda h, i, first=first: (first + i, h))
        col = pl.BlockSpec((None, tq, 1), lambda h, i, first=first: (h, first + i, 0))
        seen = pl.BlockSpec((keys, HEAD_DIM), lambda h, i: (0, h))
        seen_v = pl.BlockSpec((keys, HEAD_DIM), lambda h, i: (0, nh + h))
        seen_row = pl.BlockSpec((None, 1, keys), lambda h, i: (h, 0, 0))
        carried = [] if first_call else [dq, dk, dv, ddrow]
        carried_specs = [] if first_call else [pl.BlockSpec(memory_space=pl.ANY), seen, seen, seen_row]
        dq, dk, dv, ddrow = pl.pallas_call(
            body, name=f"attn_bwd_{first}", grid=(nh, count),
            in_specs=[tile, seen, seen_v, col, seen_row, col, tile] + carried_specs,
            out_specs=[tile, seen, seen, seen_row],
            out_shape=[jax.ShapeDtypeStruct((t, d), BF16), jax.ShapeDtypeStruct((t, d), F32),
                       jax.ShapeDtypeStruct((t, d), F32), jax.ShapeDtypeStruct((nh, 1, t), F32)],
            scratch_shapes=[pltpu.VMEM((keys, HEAD_DIM), F32), pltpu.VMEM((keys, HEAD_DIM), F32),
                            pltpu.VMEM((1, keys), F32)],
            input_output_aliases={} if first_call else {7: 0, 8: 1, 9: 2, 10: 3},
            compiler_params=_params(("parallel", "arbitrary")),
        )(q, kv, kv, dcol, drow, lse, do, *carried)
    return dq, dk, dv, ddrow


def _my_index():
    return (lax.axis_index("x") * 2 + lax.axis_index("y")) * 2 + lax.axis_index("c")


def all_gather(name, src, deps=()):
    def body(src_ref, *rest):
        out_ref, send_sems, recv_sems, local_sem = rest[len(deps):]
        x, y, c = (lax.axis_index(a) for a in MESH_AXES)
        me = (x * 2 + y) * 2 + c
        local = pltpu.make_async_copy(src_ref, out_ref.at[me], local_sem)
        local.start()
        copies = []
        for dlt in range(1, N_DEV):
            copies.append(pltpu.make_async_remote_copy(
                src_ref=src_ref, dst_ref=out_ref.at[me], send_sem=send_sems.at[dlt - 1],
                recv_sem=recv_sems.at[dlt - 1], device_id=(x ^ (dlt // 4), y ^ ((dlt // 2) % 2), c ^ (dlt % 2)),
                device_id_type=pl.DeviceIdType.MESH))
        for cp in copies:
            cp.start()
        for cp in copies:
            cp.wait_recv()
        for cp in copies:
            cp.wait_send()
        local.wait()

    return pl.pallas_call(
        body, name=name, out_shape=jax.ShapeDtypeStruct((N_DEV,) + tuple(src.shape), src.dtype),
        in_specs=[pl.BlockSpec(memory_space=pl.ANY)] * (1 + len(deps)), out_specs=pl.BlockSpec(memory_space=pl.ANY),
        scratch_shapes=[pltpu.SemaphoreType.DMA((N_DEV - 1,)), pltpu.SemaphoreType.DMA((N_DEV - 1,)),
                        pltpu.SemaphoreType.DMA],
        compiler_params=pltpu.CompilerParams(has_side_effects=True),
    )(src, *deps)


_HBM = pl.BlockSpec(memory_space=pltpu.HBM)
_SEM = pl.BlockSpec(memory_space=pltpu.SEMAPHORE)
_DATAFLOW = pltpu.SideEffectType.DATAFLOW_SIDE_EFFECTING


def _peer_copies(src_ref, land_ref, send_sems, recv_sems):
    x, y, c = (lax.axis_index(a) for a in MESH_AXES)
    me = (x * 2 + y) * 2 + c
    copies = []
    for dlt in range(1, N_DEV):
        px, py, pc = x ^ (dlt // 4), y ^ ((dlt // 2) % 2), c ^ (dlt % 2)
        peer = (px * 2 + py) * 2 + pc
        copies.append(pltpu.make_async_remote_copy(
            src_ref=src_ref.at[peer], dst_ref=land_ref.at[me],
            send_sem=send_sems.at[dlt - 1], recv_sem=recv_sems.at[dlt - 1],
            device_id=(px, py, pc), device_id_type=pl.DeviceIdType.MESH))
    return copies


def _own_copy(src_ref, land_ref, send_sems):
    me = (lax.axis_index("x") * 2 + lax.axis_index("y")) * 2 + lax.axis_index("c")
    return pltpu.make_async_copy(src_ref.at[me], land_ref.at[me], send_sems.at[N_DEV - 1])


def scatter_start(name, srcs):
    n = len(srcs)
    lands = [lax.empty(s.shape, s.dtype) for s in srcs]

    def body(*refs):
        src_refs, land_refs = refs[:n], refs[n:2 * n]
        send_sems, recv_sems = refs[2 * n:3 * n], refs[3 * n:4 * n]
        token = refs[-1]
        for j in range(n):
            for cp in _peer_copies(src_refs[j], land_refs[j], send_sems[j], recv_sems[j]):
                cp.start()
            _own_copy(src_refs[j], land_refs[j], send_sems[j]).start()
        token[...] = jnp.zeros_like(token)

    sems = [pltpu.SemaphoreType.DMA((N_DEV,))] * n + [pltpu.SemaphoreType.DMA((N_DEV - 1,))] * n
    thru = [pltpu.HBM(a.shape, a.dtype) for a in list(srcs) + lands]
    outs = pl.pallas_call(
        body, name=name, out_shape=tuple(sems + thru + [jax.ShapeDtypeStruct((8, 128), F32)]),
        in_specs=[_HBM] * (2 * n), out_specs=tuple([_SEM] * (2 * n) + [_HBM] * (2 * n) + [pl.BlockSpec(memory_space=pltpu.VMEM)]),
        input_output_aliases={j: 2 * n + j for j in range(2 * n)},
        compiler_params=pltpu.CompilerParams(has_side_effects=_DATAFLOW),
    )(*[pltpu.with_memory_space_constraint(a, pltpu.HBM) for a in list(srcs) + lands])
    handles = [(outs[j], outs[n + j], outs[2 * n + j], outs[3 * n + j]) for j in range(n)]
    return handles, outs[-1]


def scatter_wait(name, handle, after):
    send_sems, recv_sems, src, land = handle

    def body(src_ref, land_ref, send_ref, recv_ref, after_ref, src_out, land_out):
        for cp in _peer_copies(src_ref, land_ref, send_ref, recv_ref):
            cp.wait_send()
            cp.wait_recv()
        _own_copy(src_ref, land_ref, send_ref).wait()

    return pl.pallas_call(
        body, name=name, out_shape=(pltpu.HBM(src.shape, src.dtype), pltpu.HBM(land.shape, land.dtype)),
        in_specs=[_HBM, _HBM, _SEM, _SEM, pl.BlockSpec(memory_space=pl.ANY)], out_specs=(_HBM, _HBM),
        input_output_aliases={0: 0, 1: 1},
        compiler_params=pltpu.CompilerParams(has_side_effects=_DATAFLOW),
    )(src, land, send_sems, recv_sems, after)


N_OTHER_CHIPS = 3


def _two_level_places():
    x, y, c = (lax.axis_index(a) for a in MESH_AXES)
    return (x, y, c), (x * 2 + y) * 2 + c, (x, y, 1 - c), [(1 - x, y), (x, 1 - y), (1 - x, 1 - y)]


def _first_copies(land_ref, send_sems, recv_sems):
    (x, y, c), me, other_core, chips = _two_level_places()
    targets = [other_core] + [(cx, cy, c) for cx, cy in chips]
    return [pltpu.make_async_remote_copy(
        src_ref=land_ref.at[me], dst_ref=land_ref.at[me], send_sem=send_sems.at[k], recv_sem=recv_sems.at[k],
        device_id=to, device_id_type=pl.DeviceIdType.MESH) for k, to in enumerate(targets)]


def _passed_on_copies(land_ref, send_sems, recv_sems):
    (x, y, c), me, other_core, chips = _two_level_places()
    copies = []
    for k, (cx, cy) in enumerate(chips):
        slot = land_ref.at[(cx * 2 + cy) * 2 + c]
        copies.append(pltpu.make_async_remote_copy(
            src_ref=slot, dst_ref=slot, send_sem=send_sems.at[k], recv_sem=recv_sems.at[k],
            device_id=other_core, device_id_type=pl.DeviceIdType.MESH))
    return copies


def gather_start(name, lands):
    n = len(lands)

    def body(*refs):
        land_refs, send_sems, recv_sems = refs[:n], refs[n:2 * n], refs[2 * n:3 * n]
        for j in range(n):
            for cp in _first_copies(land_refs[j], send_sems[j], recv_sems[j]):
                cp.start()

    sems = [pltpu.SemaphoreType.DMA((1 + N_OTHER_CHIPS,))] * (2 * n)
    outs = pl.pallas_call(
        body, name=name, out_shape=tuple(sems + [pltpu.HBM(a.shape, a.dtype) for a in lands]),
        in_specs=[_HBM] * n, out_specs=tuple([_SEM] * (2 * n) + [_HBM] * n),
        input_output_aliases={j: 2 * n + j for j in range(n)},
        compiler_params=pltpu.CompilerParams(has_side_effects=_DATAFLOW),
    )(*[pltpu.with_memory_space_constraint(a, pltpu.HBM) for a in lands])
    return [[outs[j], outs[n + j], outs[2 * n + j]] for j in range(n)]


def gather_pass_on(name, handle, after):
    send_sems, recv_sems, land = handle

    def body(land_ref, recv_ref, after_ref, land_out, send2, recv2, token):
        arrivals = _first_copies(land_ref, recv_ref, recv_ref)
        for k, cp in enumerate(_passed_on_copies(land_ref, send2, recv2)):
            arrivals[1 + k].wait_recv()
            cp.start()
        token[...] = jnp.zeros_like(token)

    sem3 = pltpu.SemaphoreType.DMA((N_OTHER_CHIPS,))
    land, send2, recv2, token = pl.pallas_call(
        body, name=name,
        out_shape=(pltpu.HBM(land.shape, land.dtype), sem3, sem3, jax.ShapeDtypeStruct((8, 128), F32)),
        in_specs=[_HBM, _SEM, pl.BlockSpec(memory_space=pl.ANY)],
        out_specs=(_HBM, _SEM, _SEM, pl.BlockSpec(memory_space=pltpu.VMEM)),
        input_output_aliases={0: 0}, compiler_params=pltpu.CompilerParams(has_side_effects=_DATAFLOW),
    )(land, recv_sems, after)
    return [send_sems, recv_sems, land, send2, recv2], token


def gather_wait(name, handle, after):
    send_sems, recv_sems, land, send2, recv2 = handle

    def body(land_ref, send_ref, recv_ref, send2_ref, recv2_ref, after_ref, land_out):
        first = _first_copies(land_ref, send_ref, recv_ref)
        for cp in first:
            cp.wait_send()
        first[0].wait_recv()
        for cp in _passed_on_copies(land_ref, send2_ref, recv2_ref):
            cp.wait_send()
            cp.wait_recv()

    return pl.pallas_call(
        body, name=name, out_shape=pltpu.HBM(land.shape, land.dtype),
        in_specs=[_HBM, _SEM, _SEM, _SEM, _SEM, pl.BlockSpec(memory_space=pl.ANY)], out_specs=_HBM,
        input_output_aliases={0: 0}, compiler_params=pltpu.CompilerParams(has_side_effects=_DATAFLOW),
    )(land, send_sems, recv_sems, send2, recv2, after)


N_CHIPS = 4


def _pair_copies(g_ref, half_ref, send_sems, recv_sems):
    (x, y, c), me, other_core, chips = _two_level_places()
    return [pltpu.make_async_remote_copy(
        src_ref=g_ref.at[chip * 2 + (1 - c)], dst_ref=half_ref.at[chip], send_sem=send_sems.at[chip],
        recv_sem=recv_sems.at[chip], device_id=other_core, device_id_type=pl.DeviceIdType.MESH)
        for chip in range(N_CHIPS)]


def pair_start(name, g):
    half = lax.empty((N_CHIPS,) + g.shape[1:], g.dtype)

    def body(g_ref, half_ref, send_sems, recv_sems, g_out, half_out, token):
        for cp in _pair_copies(g_ref, half_ref, send_sems, recv_sems):
            cp.start()
        token[...] = jnp.zeros_like(token)

    sem = pltpu.SemaphoreType.DMA((N_CHIPS,))
    outs = pl.pallas_call(
        body, name=name,
        out_shape=(sem, sem, pltpu.HBM(g.shape, g.dtype), pltpu.HBM(half.shape, half.dtype),
                   jax.ShapeDtypeStruct((8, 128), F32)),
        in_specs=[_HBM, _HBM], out_specs=(_SEM, _SEM, _HBM, _HBM, pl.BlockSpec(memory_space=pltpu.VMEM)),
        input_output_aliases={0: 2, 1: 3}, compiler_params=pltpu.CompilerParams(has_side_effects=_DATAFLOW),
    )(pltpu.with_memory_space_constraint(g, pltpu.HBM), half)
    return list(outs[:4]), outs[4]


def pair_wait(name, handle, after):
    send_sems, recv_sems, g, half = handle

    def body(g_ref, half_ref, send_ref, recv_ref, after_ref, g_out, half_out):
        for cp in _pair_copies(g_ref, half_ref, send_ref, recv_ref):
            cp.wait_send()
            cp.wait_recv()

    return pl.pallas_call(
        body, name=name, out_shape=(pltpu.HBM(g.shape, g.dtype), pltpu.HBM(half.shape, half.dtype)),
        in_specs=[_HBM, _HBM, _SEM, _SEM, pl.BlockSpec(memory_space=pl.ANY)], out_specs=(_HBM, _HBM),
        input_output_aliases={0: 0, 1: 1}, compiler_params=pltpu.CompilerParams(has_side_effects=_DATAFLOW),
    )(g, half, send_sems, recv_sems, after)


def pair_sum(name, g, half):
    _, r, wd = g.shape
    tr = _row_tile(r, 2 * ROW_TILE_BYTES // (4 * wd))
    kind = lax.axis_index("c").astype(jnp.int32).reshape(1)

    def body(kind_ref, g_ref, half_ref, o_ref):
        o_ref[...] = (g_ref[...].astype(F32) + half_ref[...].astype(F32)).astype(o_ref.dtype)

    spec = pl.BlockSpec((None, tr, wd), lambda chip, i, kind_ref: (chip, i, 0))
    return pl.pallas_call(
        body, name=name,
        grid_spec=pltpu.PrefetchScalarGridSpec(
            num_scalar_prefetch=1, grid=(N_CHIPS, r // tr),
            in_specs=[pl.BlockSpec((None, tr, wd), lambda chip, i, kind_ref: (chip * 2 + kind_ref[0], i, 0)), spec],
            out_specs=spec),
        out_shape=jax.ShapeDtypeStruct((N_CHIPS, r, wd), g.dtype),
        compiler_params=_params(("parallel", "parallel")),
    )(kind, g, half)


def _chip_copies(sums_ref, land_ref, send_sems, recv_sems):
    (x, y, c), me, other_core, chips = _two_level_places()
    return [pltpu.make_async_remote_copy(
        src_ref=sums_ref.at[cx * 2 + cy], dst_ref=land_ref.at[x * 2 + y], send_sem=send_sems.at[k],
        recv_sem=recv_sems.at[k], device_id=(cx, cy, c), device_id_type=pl.DeviceIdType.MESH)
        for k, (cx, cy) in enumerate(chips)]


def chip_start(name, sums):
    land = lax.empty(sums.shape, sums.dtype)

    def body(sums_ref, land_ref, send_sems, recv_sems, sums_out, land_out, token):
        for cp in _chip_copies(sums_ref, land_ref, send_sems, recv_sems):
            cp.start()
        token[...] = jnp.zeros_like(token)

    sem = pltpu.SemaphoreType.DMA((N_OTHER_CHIPS,))
    outs = pl.pallas_call(
        body, name=name,
        out_shape=(sem, sem, pltpu.HBM(sums.shape, sums.dtype), pltpu.HBM(land.shape, land.dtype),
                   jax.ShapeDtypeStruct((8, 128), F32)),
        in_specs=[_HBM, _HBM], out_specs=(_SEM, _SEM, _HBM, _HBM, pl.BlockSpec(memory_space=pltpu.VMEM)),
        input_output_aliases={0: 2, 1: 3}, compiler_params=pltpu.CompilerParams(has_side_effects=_DATAFLOW),
    )(pltpu.with_memory_space_constraint(sums, pltpu.HBM), land)
    return list(outs[:4]), outs[4]


def chip_wait(name, handle, after):
    send_sems, recv_sems, sums, land = handle

    def body(sums_ref, land_ref, send_ref, recv_ref, after_ref, sums_out, land_out):
        for cp in _chip_copies(sums_ref, land_ref, send_ref, recv_ref):
            cp.wait_send()
            cp.wait_recv()

    return pl.pallas_call(
        body, name=name, out_shape=(pltpu.HBM(sums.shape, sums.dtype), pltpu.HBM(land.shape, land.dtype)),
        in_specs=[_HBM, _HBM, _SEM, _SEM, pl.BlockSpec(memory_space=pl.ANY)], out_specs=(_HBM, _HBM),
        input_output_aliases={0: 0, 1: 1}, compiler_params=pltpu.CompilerParams(has_side_effects=_DATAFLOW),
    )(sums, land, send_sems, recv_sems, after)


def adamw_reduce(name, parts, w, m, v):
    nl, r, wd = w.shape
    tr = _row_tile(r, ROW_TILE_BYTES // (8 * wd))

    def body(*refs):
        p_refs = refs[:nl]
        w_ref, m_ref, v_ref, g_ref, d_ref, nm_ref, nv_ref = refs[nl:]
        layer = pl.program_id(0)
        for j in range(nl):
            @pl.when(layer == j)
            def _(j=j):
                g = p_refs[j][0].astype(F32)
                for sender in range(1, p_refs[j].shape[0]):
                    g = g + p_refs[j][sender].astype(F32)
                nm = B1 * m_ref[...] + (1.0 - B1) * g
                nv = B2 * v_ref[...] + (1.0 - B2) * jnp.square(g)
                m_hat = nm / (1.0 - B1 ** STEP)
                v_hat = nv / (1.0 - B2 ** STEP)
                g_ref[...] = g
                d_ref[...] = -LR * (m_hat / (jnp.sqrt(v_hat) + ADAM_EPS) + WD * w_ref[...])
                nm_ref[...] = nm
                nv_ref[...] = nv

    def part_spec(j):
        return pl.BlockSpec((parts[j].shape[0], tr, wd), lambda l, i: (0, jnp.where(l == j, i, 0), 0))

    spec = pl.BlockSpec((None, tr, wd), lambda l, i: (l, i, 0))
    return pl.pallas_call(
        body, name=name, grid=(nl, r // tr),
        in_specs=[part_spec(j) for j in range(nl)] + [spec, spec, spec],
        out_specs=[spec] * 4, out_shape=[jax.ShapeDtypeStruct((nl, r, wd), F32)] * 4,
        compiler_params=_params(("arbitrary", "arbitrary")),
    )(*parts, w, m, v)


def _pack_rows(vectors, rows=None):
    flat = jnp.concatenate([a.reshape(-1).astype(F32) for a in vectors])
    n = flat.shape[0]
    if rows is None:
        rows = -(-n // 1024) * 8
    return jnp.pad(flat, (0, rows * 128 - n)).reshape(rows, 128)


def _unpack_rows(packed, like):
    flat = packed.reshape(-1)
    out, pos = [], 0
    for a in like:
        out.append(flat[pos:pos + a.size].reshape(a.shape))
        pos += a.size
    return out


def kernel(x, p, mix_norm, mlp_norm, ple_norm, w_a_in, a_lb_logits, a_head_gain, w_a_out, kv_norm, w_kvf, b_f, w_b_q, w_b_out, w_mlp_up, w_mlp_down, w_ple_gate, w_ple_up, final_norm, loss_target, m_mix_norm, m_mlp_norm, m_ple_norm, m_w_a_in, m_a_lb_logits, m_a_head_gain, m_w_a_out, m_kv_norm, m_w_kvf, m_b_f, m_w_b_q, m_w_b_out, m_w_mlp_up, m_w_mlp_down, m_w_ple_gate, m_w_ple_up, m_final_norm, v_mix_norm, v_mlp_norm, v_ple_norm, v_w_a_in, v_a_lb_logits, v_a_head_gain, v_w_a_out, v_kv_norm, v_w_kvf, v_b_f, v_w_b_q, v_w_b_out, v_w_mlp_up, v_w_mlp_down, v_w_ple_gate, v_w_ple_up, v_final_norm):
    t, d = x.shape[1], x.shape[2]
    nh = d // HEAD_DIM
    n_layers = 2
    x2 = x.reshape(t, d)
    target = loss_target.reshape(t, d)
    me = _my_index()

    shards = {"w_a_in": w_a_in[0], "w_a_out": w_a_out[0], "w_kvf": w_kvf, "w_b_q": w_b_q[0], "w_b_out": w_b_out[0]}
    for l in range(n_layers):
        shards.update({f"w_mlp_up{l}": w_mlp_up[l], f"w_mlp_down{l}": w_mlp_down[l],
                       f"w_ple_gate{l}": w_ple_gate[l], f"w_ple_up{l}": w_ple_up[l]})
    first_use = ["a_lb_logits", "w_ple_up0", "w_ple_up1", "w_a_in", "w_a_out", "w_mlp_up0", "w_mlp_down0", "w_ple_gate0",
                 "w_kvf", "w_b_q", "w_b_out", "w_mlp_up1", "w_mlp_down1", "w_ple_gate1"]
    row_sharded = ("w_a_out", "w_b_q", "w_b_out", "w_mlp_down", "w_ple_gate")
    shards_bf = [a_lb_logits] + [shards[n].astype(BF16) for n in first_use[1:]]
    ag_handles = gather_start("ag_start", [
        lax.dynamic_update_slice(lax.empty((N_DEV,) + a.shape, a.dtype), a[None], (me, 0, 0)) for a in shards_bf])
    passed_on = {}
    weights = {}

    def pass_on(j, after):
        if j < len(first_use) and j not in passed_on:
            passed_on[j] = gather_pass_on("ag_pass_" + first_use[j], ag_handles[j], after)

    def weight(name, after=None, ahead=True):
        if name not in weights:
            j = first_use.index(name)
            pass_on(j, after)
            if ahead:
                pass_on(j + 1, after)
            behind = passed_on[j + 1][1] if j + 1 in passed_on else after
            g = gather_wait("ag_wait_" + name, passed_on[j][0], behind)
            if name.rstrip("01") in row_sharded:
                g = g.reshape(1, g.shape[0] * g.shape[1], g.shape[2])
            weights[name] = g
        return weights[name]

    lgt = weight("a_lb_logits", x2).transpose(1, 0, 2).reshape(2, d)
    p_bf = [p[l, 0].astype(BF16) for l in range(n_layers)]
    pu_early = [mm_nn(f"ple_up{l}", p_bf[l], weight(f"w_ple_up{l}", x2, ahead=l + 1 < n_layers))
                for l in range(n_layers)]

    def row(vec):
        return vec.reshape(1, -1)

    def mlp_ple_fwd(l, h_in, a):
        (h_a, u_mlp), _ = rowwise(f"add_norm_mlp{l}", _add_norm_fwd, [h_in, a], [row(mlp_norm[l])])
        pre, act = mm_nn(f"mlp_up{l}", u_mlp, weight(f"w_mlp_up{l}", u_mlp, ahead=l > 0), fuse=(_relu2, (), (BF16, BF16)))
        mo = mm_nn(f"mlp_down{l}", act, weight(f"w_mlp_down{l}", act))
        (h_b, u_ple), _ = rowwise(f"add_norm_ple{l}", _add_norm_fwd, [h_a, mo], [row(ple_norm[l])])
        gpre = mm_nn(f"ple_gate{l}", u_ple, weight(f"w_ple_gate{l}", u_ple))
        return dict(h_a=h_a, u_mlp=u_mlp, pre=pre, act=act, h_b=h_b, u_ple=u_ple, gpre=gpre, pu=pu_early[l])

    (u0,), _ = rowwise("norm_mix0", _norm_fwd, [x2], [row(mix_norm[0])])
    z = mm_nn("a_in", u0, weight("w_a_in", pu_early[-1], ahead=False))
    og, states = hgrn_fwd(z, lgt, a_head_gain)
    a0 = mm_nn("a_out", og, weight("w_a_out", og))
    s0 = mlp_ple_fwd(0, x2, a0)
    (h3, u_kv, u1), _ = rowwise("ple_norms", _ple_two_norms_fwd, [s0["h_b"], s0["gpre"], s0["pu"]],
                                [row(kv_norm), row(mix_norm[1])])
    w_kvf_cols = weight("w_kvf", u_kv).transpose(1, 0, 2).reshape(d, 2 * d + nh)
    w_kv = w_kvf_cols[:, :2 * d].reshape(d, 2, d).transpose(1, 0, 2)
    w_f = jnp.pad(w_kvf_cols[:, 2 * d:], ((0, 0), (0, HEAD_DIM - nh)))[None]
    kv = mm_nn("kvf", u_kv, w_kv, out_dtype=BF16)
    fl_t = mm_nn("kvf_forget", u_kv, w_f)[:, :nh].T
    b_f_col = b_f.reshape(nh, 1)
    dcum = decay_fwd(fl_t, b_f_col)
    dcol, drow = dcum.reshape(nh, t, 1), dcum.reshape(nh, 1, t)
    q = mm_nn("b_q", u1, weight("w_b_q", dcum))
    o, lse = attn_fwd(q, kv, dcol, drow)
    a1 = mm_nn("b_out", o, weight("w_b_out", o))
    s1 = mlp_ple_fwd(1, h3, a1)

    (dh, dgpre, dpu), (d_final, loss_rows) = rowwise(
        "tail", _tail_fwd_bwd, [s1["h_b"], s1["gpre"], s1["pu"], target], [row(final_norm)])

    sent = {}
    tokens = []

    two_level = ("w_mlp_up0", "w_a_in")
    swapping = []

    def send_grad(name, g):
        g = g.reshape(N_DEV, -1, g.shape[-1])
        if name in two_level:
            sent[name], token = pair_start("rs_pair_" + name, g)
            swapping.append(name)
        else:
            (sent[name],), token = scatter_start("rs_start_" + name, [g])
        tokens.append(token)

    def send_grads_together(grads):
        names = list(grads)
        handles, token = scatter_start("rs_start_" + names[0], [
            grads[n].reshape(N_DEV, -1, grads[n].shape[-1]) for n in names])
        sent.update(zip(names, handles))
        tokens.append(token)

    def second_stage(after):
        for name in swapping:
            g, half = pair_wait("rs_pairwait_" + name, sent[name], after)
            sent[name], token = chip_start("rs_chip_" + name, pair_sum("rs_sum_" + name, g, half))
            tokens.append(token)
        swapping.clear()

    def after_sends():
        deps = tuple(tokens)
        tokens.clear()
        return deps

    def mlp_ple_bwd(l, s, dh, dgpre, dpu):
        send_grads_together({
            f"w_ple_gate{l}": mm_tn(f"d_ple_gate_w{l}", s["u_ple"], dgpre, 1, deps=after_sends()),
            f"w_ple_up{l}": mm_tn(f"d_ple_up_w{l}", p_bf[l], dpu, N_DEV)})
        du = mm_nt(f"d_ple_gate_x{l}", dgpre, weight(f"w_ple_gate{l}"), out_dtype=BF16, deps=after_sends())
        (dh, dh_bf), (d_ple,) = rowwise(f"d_norm_ple{l}", _norm_bwd, [s["h_b"], du, dh], [row(ple_norm[l])])
        send_grad(f"w_mlp_down{l}", mm_tn(f"d_mlp_down_w{l}", s["act"], dh_bf, 1))
        (dpre,) = mm_nt(f"d_mlp_down_x{l}", dh_bf, weight(f"w_mlp_down{l}"), deps=after_sends(),
                        fuse=(_relu2_bwd, (s["pre"],), (BF16,)))
        second_stage(dpre)
        send_grad(f"w_mlp_up{l}", mm_tn(f"d_mlp_up_w{l}", s["u_mlp"], dpre, N_DEV))
        du = mm_nt(f"d_mlp_up_x{l}", dpre, weight(f"w_mlp_up{l}"), out_dtype=BF16, deps=after_sends())
        second_stage(du)
        (dh, dh_bf), (d_mlp,) = rowwise(f"d_norm_mlp{l}", _norm_bwd, [s["h_a"], du, dh], [row(mlp_norm[l])])
        return dh, dh_bf, d_ple, d_mlp

    dh, dh_bf, d_ple1, d_mlp1 = mlp_ple_bwd(1, s1, dh, dgpre, dpu)
    send_grad("w_b_out", mm_tn("d_b_out_w", o, dh_bf, 1))
    do = mm_nt("d_b_out_x", dh_bf, weight("w_b_out"), out_dtype=BF16, deps=after_sends())
    dq, dk, dv, ddrow = attn_bwd(q, kv, dcol, drow, lse, do)
    send_grad("w_b_q", mm_tn("d_b_q_w", u1, dq, 1))
    du1 = mm_nt("d_b_q_x", dq, weight("w_b_q"), out_dtype=BF16, deps=after_sends())
    dfl_t, d_b_f = decay_bwd(fl_t, b_f_col, ddrow.reshape(nh, t))
    dkv = jnp.stack([dk, dv]).astype(BF16)
    dfl = jnp.pad(dfl_t.T, ((0, 0), (0, HEAD_DIM - nh))).astype(BF16)
    d_w_kv = mm_tn("d_kvf_w", u_kv, dkv, 2)
    d_w_f = mm_tn("d_kvf_forget_w", u_kv, dfl, 1)
    d_w_kvf = jnp.concatenate([d_w_kv[0], d_w_kv[1], d_w_f[0, :, :nh]], axis=1)
    per_owner = (2 * d + nh) // N_DEV
    send_grad("w_kvf", jnp.stack([d_w_kvf[:, p * per_owner:(p + 1) * per_owner] for p in range(N_DEV)]))
    du_f = mm_nt("d_kvf_forget_x", dfl, w_f, deps=after_sends())
    (du_kv,) = mm_nt("d_kvf_x", dkv, w_kv, fuse=(lambda acc, extra: (acc + extra,), (du_f,), (BF16,)))
    second_stage(du_kv)
    (dh, dgpre, dpu), (d_kv_norm, d_mix1) = rowwise(
        "d_ple_norms", _two_norms_ple_bwd, [h3, du_kv, du1, dh, s0["gpre"], s0["pu"]],
        [row(kv_norm), row(mix_norm[1])])
    dh, dh_bf, d_ple0, d_mlp0 = mlp_ple_bwd(0, s0, dh, dgpre, dpu)
    send_grad("w_a_out", mm_tn("d_a_out_w", og, dh_bf, 1))
    dog = mm_nt("d_a_out_x", dh_bf, weight("w_a_out"), out_dtype=BF16, deps=after_sends())
    dz4, d_lgt, d_hg = hgrn_bwd(z, lgt, a_head_gain, states, dog)
    send_grad("w_a_in", mm_tn("d_a_in_w", u0, dz4, N_DEV, stacked=True))
    du0 = mm_nt("d_a_in_x", dz4, weight("w_a_in"), out_dtype=BF16, deps=after_sends(), stacked=True)
    second_stage(du0)
    (dx, _), (d_mix0,) = rowwise("d_norm_mix0", _norm_bwd, [x2, du0, dh], [row(mix_norm[0])])

    new = {}
    last = [dx]

    def update(name, parts, w, m, v):
        shp = w.shape
        w3, m3, v3 = (a.reshape(len(parts), -1, shp[-1]) for a in (w, m, v))
        new[name] = tuple(a.reshape(shp) for a in adamw_reduce("adamw_" + name, parts, w3, m3, v3))
        last[0] = new[name][0]

    def receive_update(name, layers, w, m, v):
        parts = {}
        for sfx in layers:
            if name + sfx in two_level:
                mine, land = chip_wait(f"rs_wait_{name}{sfx}", sent[name + sfx], last[0])
                chip = me // 2
                parts[sfx] = lax.dynamic_update_slice(land, lax.dynamic_slice_in_dim(mine, chip, 1, 0), (chip, 0, 0))
            else:
                _, parts[sfx] = scatter_wait(f"rs_wait_{name}{sfx}", sent[name + sfx], last[0])
        update(name, [parts[sfx] for sfx in sorted(layers)], w, m, v)

    both = ("1", "0")
    receive_update("w_b_out", ("",), w_b_out, m_w_b_out, v_w_b_out)
    receive_update("w_b_q", ("",), w_b_q, m_w_b_q, v_w_b_q)
    receive_update("w_kvf", ("",), w_kvf, m_w_kvf, v_w_kvf)
    receive_update("w_ple_gate", both, w_ple_gate, m_w_ple_gate, v_w_ple_gate)
    receive_update("w_ple_up", both, w_ple_up, m_w_ple_up, v_w_ple_up)
    receive_update("w_mlp_down", both, w_mlp_down, m_w_mlp_down, v_w_mlp_down)
    receive_update("w_mlp_up", both, w_mlp_up, m_w_mlp_up, v_w_mlp_up)
    receive_update("w_a_out", ("",), w_a_out, m_w_a_out, v_w_a_out)
    receive_update("w_a_in", ("",), w_a_in, m_w_a_in, v_w_a_in)

    small = dict(mix_norm=jnp.concatenate([d_mix0, d_mix1]), mlp_norm=jnp.concatenate([d_mlp0, d_mlp1]),
                 ple_norm=jnp.concatenate([d_ple0, d_ple1]), a_head_gain=d_hg, kv_norm=d_kv_norm.reshape(d),
                 b_f=d_b_f.reshape(nh), final_norm=d_final.reshape(d))
    small_w = dict(mix_norm=(mix_norm, m_mix_norm, v_mix_norm), mlp_norm=(mlp_norm, m_mlp_norm, v_mlp_norm),
                   ple_norm=(ple_norm, m_ple_norm, v_ple_norm),
                   a_head_gain=(a_head_gain, m_a_head_gain, v_a_head_gain), kv_norm=(kv_norm, m_kv_norm, v_kv_norm),
                   b_f=(b_f, m_b_f, v_b_f), final_norm=(final_norm, m_final_norm, v_final_norm))
    names = list(small)
    packed = _pack_rows([d_lgt] + [small[n] for n in names])
    everyone = all_gather("ag_small_grads", packed, deps=(last[0],))
    n_lgt_rows = d_lgt.size // 128
    lgt_parts = everyone[:, :n_lgt_rows].reshape(N_DEV, 2, d)
    lgt_parts = lax.dynamic_slice_in_dim(lgt_parts, me * a_lb_logits.shape[1], a_lb_logits.shape[1], axis=2)
    update("a_lb_logits", [lgt_parts], a_lb_logits, m_a_lb_logits, v_a_lb_logits)
    rest = everyone[:, n_lgt_rows:]
    like = [small_w[n][0] for n in names]
    packed_w, packed_m, packed_v = (_pack_rows([small_w[n][j] for n in names], rest.shape[1])[None] for j in range(3))
    outs = adamw_reduce("adamw_small", [rest], packed_w, packed_m, packed_v)
    unpacked = [_unpack_rows(a, like) for a in outs]
    for j, n in enumerate(names):
        new[n] = tuple(unpacked[q][j] for q in range(4))

    order = ["mix_norm", "mlp_norm", "ple_norm", "w_a_in", "a_lb_logits", "a_head_gain", "w_a_out", "kv_norm",
             "w_kvf", "b_f", "w_b_q", "w_b_out", "w_mlp_up", "w_mlp_down", "w_ple_gate", "w_ple_up", "final_norm"]
    loss_here, _ = lax.optimization_barrier((loss_rows[0, 0], new["final_norm"][0]))
    loss = lax.psum(loss_here, MESH_AXES)
    result = [loss, dx.reshape(x.shape)]
    for j in range(4):
        result += [new[n][j] for n in order]
    return tuple(result)
```
